```python
import math
import jax, jax.numpy as jnp
from jax import lax
import numpy as np

D_MODEL = 4096
BATCH = 16
SEQ = 2048
DEPTH = 1

D_SSM = D_MODEL // 2
SSM_GROUP = 16
N_SSM_GROUPS = D_SSM // SSM_GROUP
SSM_STATE = 64
D_POOL = D_MODEL // 2
POOL_WINDOWS = (2, 4, 8, 16)
N_POOL_GROUPS = len(POOL_WINDOWS)
POOL_GROUP = D_POOL // N_POOL_GROUPS
D_IN = D_SSM + D_POOL + 2 * D_MODEL
D_FF = 11008
CONV_WIDTH = 3
EPS = 1e-6
MIN_NEG_REAL = -1e-4

kernel_name = "s5_pool_gated_hybrid_block"


def rms_norm(x, g):
    xf = x.astype(jnp.float32)
    r = xf * lax.rsqrt(jnp.mean(xf * xf, axis=-1, keepdims=True) + EPS)
    return (r * g.astype(jnp.float32)).astype(x.dtype)


def s5_branch(u, lam_re, lam_im, log_step, b_re, b_im, c_re, c_im, d, glu_w, glu_b):
    bsz, L, _ = u.shape
    uf = u.astype(jnp.float32)
    ug = uf.reshape(bsz, L, N_SSM_GROUPS, SSM_GROUP)
    lr = jnp.minimum(lam_re.astype(jnp.float32), MIN_NEG_REAL)
    li = lam_im.astype(jnp.float32)
    dt = jnp.exp(log_step.astype(jnp.float32))[:, None]
    mag = jnp.exp(lr * dt)
    ang = li * dt
    ab_re = mag * jnp.cos(ang)
    ab_im = mag * jnp.sin(ang)
    nr = ab_re - 1.0
    ni = ab_im
    den = lr * lr + li * li
    f_re = (nr * lr + ni * li) / den
    f_im = (ni * lr - nr * li) / den
    br = b_re.astype(jnp.float32)
    bi = b_im.astype(jnp.float32)
    bb_re = f_re[..., None] * br - f_im[..., None] * bi
    bb_im = f_re[..., None] * bi + f_im[..., None] * br
    bu_re = jnp.einsum('blgj,gpj->blgp', ug, bb_re)
    bu_im = jnp.einsum('blgj,gpj->blgp', ug, bb_im)
    a_re = jnp.broadcast_to(ab_re, (1, L) + ab_re.shape)
    a_im = jnp.broadcast_to(ab_im, (1, L) + ab_im.shape)

    def combine(e1, e2):
        a1r, a1i, b1r, b1i = e1
        a2r, a2i, b2r, b2i = e2
        return (a2r * a1r - a2i * a1i,
                a2r * a1i + a2i * a1r,
                a2r * b1r - a2i * b1i + b2r,
                a2r * b1i + a2i * b1r + b2i)

    _, _, s_re, s_im = lax.associative_scan(combine, (a_re, a_im, bu_re, bu_im), axis=1)
    y = (jnp.einsum('blgp,gjp->blgj', s_re, c_re.astype(jnp.float32))
         - jnp.einsum('blgp,gjp->blgj', s_im, c_im.astype(jnp.float32)))
    y = y.reshape(bsz, L, D_SSM) + d.astype(jnp.float32) * uf
    y = jax.nn.gelu(y, approximate=True)
    y = y * jax.nn.sigmoid(y @ glu_w.astype(jnp.float32) + glu_b.astype(jnp.float32))
    return y.astype(u.dtype)


def pool_branch(v, pool_w, pool_b, pool_scale):
    bsz, L, _ = v.shape
    vf = v.astype(jnp.float32).reshape(bsz, L, N_POOL_GROUPS, POOL_GROUP)
    cs = jnp.pad(jnp.cumsum(vf, axis=1), ((0, 0), (1, 0), (0, 0), (0, 0)))
    t = jnp.arange(1, L + 1)
    pooled = []
    for gi, win in enumerate(POOL_WINDOWS):
        start = jnp.maximum(t - win, 0)
        s = cs[:, 1:, gi] - jnp.take(cs[:, :, gi], start, axis=1)
        cnt = (t - start).astype(jnp.float32)
        pooled.append(s / cnt[None, :, None])
    z = jnp.stack(pooled, axis=2) - vf
    z = jnp.einsum('blgc,gcd->blgd', z, pool_w.astype(jnp.float32)) + pool_b.astype(jnp.float32)
    z = z.reshape(bsz, L, D_POOL) * pool_scale.astype(jnp.float32)
    return z.astype(v.dtype)


def causal_depthwise_conv(u, w, b):
    L = u.shape[1]
    up = jnp.pad(u, ((0, 0), (CONV_WIDTH - 1, 0), (0, 0)))
    y = b
    for k in range(CONV_WIDTH):
        y = y + up[:, k:k + L] * w[k]
    return y


def _fwd_setup_inputs(seed: int = 0) -> dict:
    key = jax.random.key(seed)
    ks = jax.random.split(key, 32)
    f32 = jnp.float32
    nrm = lambda k, shape, s: (jax.random.normal(k, shape, f32) * s).astype(f32)
    gain = lambda k: (1.0 + 0.02 * jax.random.normal(k, (DEPTH, D_MODEL), f32)).astype(f32)
    G, P, GC = N_SSM_GROUPS, SSM_STATE, SSM_GROUP
    n_idx = jnp.arange(P, dtype=f32)
    lam_re = -0.5 + 0.01 * jax.random.normal(ks[3], (DEPTH, G, P), f32)
    lam_im = math.pi * n_idx[None, None, :] + 0.01 * jax.random.normal(ks[4], (DEPTH, G, P), f32)
    log_step = jax.random.uniform(ks[5], (DEPTH, G), f32, math.log(1e-3), math.log(1e-1))
    return {
        "x": nrm(ks[0], (BATCH, SEQ, D_MODEL), 1.0),
        "norm_pre_mix": gain(ks[1]),
        "w_in": nrm(ks[2], (DEPTH, D_MODEL, D_IN), D_MODEL ** -0.5),
        "ssm_lambda_re": lam_re,
        "ssm_lambda_im": lam_im,
        "ssm_log_step": log_step,
        "ssm_b_re": nrm(ks[6], (DEPTH, G, P, GC), (2.0 * GC) ** -0.5),
        "ssm_b_im": nrm(ks[7], (DEPTH, G, P, GC), (2.0 * GC) ** -0.5),
        "ssm_c_re": nrm(ks[8], (DEPTH, G, GC, P), (2.0 * P) ** -0.5),
        "ssm_c_im": nrm(ks[9], (DEPTH, G, GC, P), (2.0 * P) ** -0.5),
        "ssm_d": nrm(ks[10], (DEPTH, D_SSM), 1.0),
        "ssm_glu_w": nrm(ks[11], (DEPTH, D_SSM, D_SSM), D_SSM ** -0.5),
        "ssm_glu_b": nrm(ks[12], (DEPTH, D_SSM), 0.02),
        "pool_w": nrm(ks[13], (DEPTH, N_POOL_GROUPS, POOL_GROUP, POOL_GROUP), POOL_GROUP ** -0.5),
        "pool_b": nrm(ks[14], (DEPTH, N_POOL_GROUPS, POOL_GROUP), 0.02),
        "pool_scale": (1.0 + 0.1 * jax.random.normal(ks[15], (DEPTH, D_POOL), f32)).astype(f32),
        "w_branch_ssm": nrm(ks[16], (DEPTH, D_SSM, D_MODEL), D_SSM ** -0.5),
        "w_branch_pool": nrm(ks[17], (DEPTH, D_POOL, D_MODEL), D_POOL ** -0.5),
        "w_out": nrm(ks[18], (DEPTH, D_MODEL, D_MODEL), D_MODEL ** -0.5),
        "norm_post_mix": gain(ks[19]),
        "norm_pre_ffn": gain(ks[20]),
        "w_up": nrm(ks[21], (DEPTH, D_MODEL, 2 * D_FF), D_MODEL ** -0.5),
        "ffn_conv_w": nrm(ks[22], (DEPTH, CONV_WIDTH, 2 * D_FF), CONV_WIDTH ** -0.5),
        "ffn_conv_b": nrm(ks[23], (DEPTH, 2 * D_FF), 0.02),
        "w_down": nrm(ks[24], (DEPTH, D_FF, D_MODEL), D_FF ** -0.5),
        "norm_post_ffn": gain(ks[25]),
    }


def _fwd_reference(x, norm_pre_mix, w_in, ssm_lambda_re, ssm_lambda_im, ssm_log_step,
              ssm_b_re, ssm_b_im, ssm_c_re, ssm_c_im, ssm_d, ssm_glu_w, ssm_glu_b,
              pool_w, pool_b, pool_scale, w_branch_ssm, w_branch_pool, w_out,
              norm_post_mix, norm_pre_ffn, w_up, ffn_conv_w, ffn_conv_b, w_down,
              norm_post_ffn):
    h = x
    for i in range(DEPTH):
        a = rms_norm(h, norm_pre_mix[i])
        proj = a @ w_in[i]
        u_ssm = proj[..., :D_SSM]
        u_pool = proj[..., D_SSM:D_SSM + D_POOL]
        g_ssm = proj[..., D_SSM + D_POOL:D_SSM + D_POOL + D_MODEL]
        g_pool = proj[..., D_SSM + D_POOL + D_MODEL:]
        y_ssm = s5_branch(u_ssm, ssm_lambda_re[i], ssm_lambda_im[i], ssm_log_step[i],
                          ssm_b_re[i], ssm_b_im[i], ssm_c_re[i], ssm_c_im[i], ssm_d[i],
                          ssm_glu_w[i], ssm_glu_b[i]) @ w_branch_ssm[i]
        y_pool = pool_branch(u_pool, pool_w[i], pool_b[i], pool_scale[i]) @ w_branch_pool[i]
        merged = jax.nn.sigmoid(g_ssm) * y_ssm + jax.nn.sigmoid(g_pool) * y_pool
        h = h + rms_norm(merged @ w_out[i], norm_post_mix[i])
        c = rms_norm(h, norm_pre_ffn[i])
        up = causal_depthwise_conv(c @ w_up[i], ffn_conv_w[i], ffn_conv_b[i])
        f = jax.nn.gelu(up[..., :D_FF], approximate=True) * up[..., D_FF:]
        h = h + rms_norm(f @ w_down[i], norm_post_ffn[i])
    return h


import jax as _jax
import jax.numpy as _jnp

TWIN_FORMAT = 'train_step'
FWD_PARAMS = ['x', 'norm_pre_mix', 'w_in', 'ssm_lambda_re', 'ssm_lambda_im', 'ssm_log_step', 'ssm_b_re', 'ssm_b_im', 'ssm_c_re', 'ssm_c_im', 'ssm_d', 'ssm_glu_w', 'ssm_glu_b', 'pool_w', 'pool_b', 'pool_scale', 'w_branch_ssm', 'w_branch_pool', 'w_out', 'norm_post_mix', 'norm_pre_ffn', 'w_up', 'ffn_conv_w', 'ffn_conv_b', 'w_down', 'norm_post_ffn']
TWIN_WEIGHTS = ['norm_pre_mix', 'w_in', 'ssm_lambda_re', 'ssm_lambda_im', 'ssm_log_step', 'ssm_b_re', 'ssm_b_im', 'ssm_c_re', 'ssm_c_im', 'ssm_d', 'ssm_glu_w', 'ssm_glu_b', 'pool_w', 'pool_b', 'pool_scale', 'w_branch_ssm', 'w_branch_pool', 'w_out', 'norm_post_mix', 'norm_pre_ffn', 'w_up', 'ffn_conv_w', 'ffn_conv_b', 'w_down', 'norm_post_ffn']
TWIN_DIFF_INPUT = 'x'
TWIN_INPUTS = ['x', 'norm_pre_mix', 'w_in', 'ssm_lambda_re', 'ssm_lambda_im', 'ssm_log_step', 'ssm_b_re', 'ssm_b_im', 'ssm_c_re', 'ssm_c_im', 'ssm_d', 'ssm_glu_w', 'ssm_glu_b', 'pool_w', 'pool_b', 'pool_scale', 'w_branch_ssm', 'w_branch_pool', 'w_out', 'norm_post_mix', 'norm_pre_ffn', 'w_up', 'ffn_conv_w', 'ffn_conv_b', 'w_down', 'norm_post_ffn', 'loss_target', 'm_norm_pre_mix', 'm_w_in', 'm_ssm_lambda_re', 'm_ssm_lambda_im', 'm_ssm_log_step', 'm_ssm_b_re', 'm_ssm_b_im', 'm_ssm_c_re', 'm_ssm_c_im', 'm_ssm_d', 'm_ssm_glu_w', 'm_ssm_glu_b', 'm_pool_w', 'm_pool_b', 'm_pool_scale', 'm_w_branch_ssm', 'm_w_branch_pool', 'm_w_out', 'm_norm_post_mix', 'm_norm_pre_ffn', 'm_w_up', 'm_ffn_conv_w', 'm_ffn_conv_b', 'm_w_down', 'm_norm_post_ffn', 'v_norm_pre_mix', 'v_w_in', 'v_ssm_lambda_re', 'v_ssm_lambda_im', 'v_ssm_log_step', 'v_ssm_b_re', 'v_ssm_b_im', 'v_ssm_c_re', 'v_ssm_c_im', 'v_ssm_d', 'v_ssm_glu_w', 'v_ssm_glu_b', 'v_pool_w', 'v_pool_b', 'v_pool_scale', 'v_w_branch_ssm', 'v_w_branch_pool', 'v_w_out', 'v_norm_post_mix', 'v_norm_pre_ffn', 'v_w_up', 'v_ffn_conv_w', 'v_ffn_conv_b', 'v_w_down', 'v_norm_post_ffn']
TWIN_OUTPUTS = ['loss', 'grad_x', 'grad_norm_pre_mix', 'grad_w_in', 'grad_ssm_lambda_re', 'grad_ssm_lambda_im', 'grad_ssm_log_step', 'grad_ssm_b_re', 'grad_ssm_b_im', 'grad_ssm_c_re', 'grad_ssm_c_im', 'grad_ssm_d', 'grad_ssm_glu_w', 'grad_ssm_glu_b', 'grad_pool_w', 'grad_pool_b', 'grad_pool_scale', 'grad_w_branch_ssm', 'grad_w_branch_pool', 'grad_w_out', 'grad_norm_post_mix', 'grad_norm_pre_ffn', 'grad_w_up', 'grad_ffn_conv_w', 'grad_ffn_conv_b', 'grad_w_down', 'grad_norm_post_ffn', 'delta_norm_pre_mix', 'delta_w_in', 'delta_ssm_lambda_re', 'delta_ssm_lambda_im', 'delta_ssm_log_step', 'delta_ssm_b_re', 'delta_ssm_b_im', 'delta_ssm_c_re', 'delta_ssm_c_im', 'delta_ssm_d', 'delta_ssm_glu_w', 'delta_ssm_glu_b', 'delta_pool_w', 'delta_pool_b', 'delta_pool_scale', 'delta_w_branch_ssm', 'delta_w_branch_pool', 'delta_w_out', 'delta_norm_post_mix', 'delta_norm_pre_ffn', 'delta_w_up', 'delta_ffn_conv_w', 'delta_ffn_conv_b', 'delta_w_down', 'delta_norm_post_ffn', 'new_m_norm_pre_mix', 'new_m_w_in', 'new_m_ssm_lambda_re', 'new_m_ssm_lambda_im', 'new_m_ssm_log_step', 'new_m_ssm_b_re', 'new_m_ssm_b_im', 'new_m_ssm_c_re', 'new_m_ssm_c_im', 'new_m_ssm_d', 'new_m_ssm_glu_w', 'new_m_ssm_glu_b', 'new_m_pool_w', 'new_m_pool_b', 'new_m_pool_scale', 'new_m_w_branch_ssm', 'new_m_w_branch_pool', 'new_m_w_out', 'new_m_norm_post_mix', 'new_m_norm_pre_ffn', 'new_m_w_up', 'new_m_ffn_conv_w', 'new_m_ffn_conv_b', 'new_m_w_down', 'new_m_norm_post_ffn', 'new_v_norm_pre_mix', 'new_v_w_in', 'new_v_ssm_lambda_re', 'new_v_ssm_lambda_im', 'new_v_ssm_log_step', 'new_v_ssm_b_re', 'new_v_ssm_b_im', 'new_v_ssm_c_re', 'new_v_ssm_c_im', 'new_v_ssm_d', 'new_v_ssm_glu_w', 'new_v_ssm_glu_b', 'new_v_pool_w', 'new_v_pool_b', 'new_v_pool_scale', 'new_v_w_branch_ssm', 'new_v_w_branch_pool', 'new_v_w_out', 'new_v_norm_post_mix', 'new_v_norm_pre_ffn', 'new_v_w_up', 'new_v_ffn_conv_w', 'new_v_ffn_conv_b', 'new_v_w_down', 'new_v_norm_post_ffn']
TWIN_LEAF_KINDS = {'loss': 'loss', 'grad_x': 'grad_x', 'grad_norm_pre_mix': 'grad_w', 'grad_w_in': 'grad_w', 'grad_ssm_lambda_re': 'grad_w', 'grad_ssm_lambda_im': 'grad_w', 'grad_ssm_log_step': 'grad_w', 'grad_ssm_b_re': 'grad_w', 'grad_ssm_b_im': 'grad_w', 'grad_ssm_c_re': 'grad_w', 'grad_ssm_c_im': 'grad_w', 'grad_ssm_d': 'grad_w', 'grad_ssm_glu_w': 'grad_w', 'grad_ssm_glu_b': 'grad_w', 'grad_pool_w': 'grad_w', 'grad_pool_b': 'grad_w', 'grad_pool_scale': 'grad_w', 'grad_w_branch_ssm': 'grad_w', 'grad_w_branch_pool': 'grad_w', 'grad_w_out': 'grad_w', 'grad_norm_post_mix': 'grad_w', 'grad_norm_pre_ffn': 'grad_w', 'grad_w_up': 'grad_w', 'grad_ffn_conv_w': 'grad_w', 'grad_ffn_conv_b': 'grad_w', 'grad_w_down': 'grad_w', 'grad_norm_post_ffn': 'grad_w', 'delta_norm_pre_mix': 'delta_w', 'delta_w_in': 'delta_w', 'delta_ssm_lambda_re': 'delta_w', 'delta_ssm_lambda_im': 'delta_w', 'delta_ssm_log_step': 'delta_w', 'delta_ssm_b_re': 'delta_w', 'delta_ssm_b_im': 'delta_w', 'delta_ssm_c_re': 'delta_w', 'delta_ssm_c_im': 'delta_w', 'delta_ssm_d': 'delta_w', 'delta_ssm_glu_w': 'delta_w', 'delta_ssm_glu_b': 'delta_w', 'delta_pool_w': 'delta_w', 'delta_pool_b': 'delta_w', 'delta_pool_scale': 'delta_w', 'delta_w_branch_ssm': 'delta_w', 'delta_w_branch_pool': 'delta_w', 'delta_w_out': 'delta_w', 'delta_norm_post_mix': 'delta_w', 'delta_norm_pre_ffn': 'delta_w', 'delta_w_up': 'delta_w', 'delta_ffn_conv_w': 'delta_w', 'delta_ffn_conv_b': 'delta_w', 'delta_w_down': 'delta_w', 'delta_norm_post_ffn': 'delta_w', 'new_m_norm_pre_mix': 'new_m', 'new_m_w_in': 'new_m', 'new_m_ssm_lambda_re': 'new_m', 'new_m_ssm_lambda_im': 'new_m', 'new_m_ssm_log_step': 'new_m', 'new_m_ssm_b_re': 'new_m', 'new_m_ssm_b_im': 'new_m', 'new_m_ssm_c_re': 'new_m', 'new_m_ssm_c_im': 'new_m', 'new_m_ssm_d': 'new_m', 'new_m_ssm_glu_w': 'new_m', 'new_m_ssm_glu_b': 'new_m', 'new_m_pool_w': 'new_m', 'new_m_pool_b': 'new_m', 'new_m_pool_scale': 'new_m', 'new_m_w_branch_ssm': 'new_m', 'new_m_w_branch_pool': 'new_m', 'new_m_w_out': 'new_m', 'new_m_norm_post_mix': 'new_m', 'new_m_norm_pre_ffn': 'new_m', 'new_m_w_up': 'new_m', 'new_m_ffn_conv_w': 'new_m', 'new_m_ffn_conv_b': 'new_m', 'new_m_w_down': 'new_m', 'new_m_norm_post_ffn': 'new_m', 'new_v_norm_pre_mix': 'new_v', 'new_v_w_in': 'new_v', 'new_v_ssm_lambda_re': 'new_v', 'new_v_ssm_lambda_im': 'new_v', 'new_v_ssm_log_step': 'new_v', 'new_v_ssm_b_re': 'new_v', 'new_v_ssm_b_im': 'new_v', 'new_v_ssm_c_re': 'new_v', 'new_v_ssm_c_im': 'new_v', 'new_v_ssm_d': 'new_v', 'new_v_ssm_glu_w': 'new_v', 'new_v_ssm_glu_b': 'new_v', 'new_v_pool_w': 'new_v', 'new_v_pool_b': 'new_v', 'new_v_pool_scale': 'new_v', 'new_v_w_branch_ssm': 'new_v', 'new_v_w_branch_pool': 'new_v', 'new_v_w_out': 'new_v', 'new_v_norm_post_mix': 'new_v', 'new_v_norm_pre_ffn': 'new_v', 'new_v_w_up': 'new_v', 'new_v_ffn_conv_w': 'new_v', 'new_v_ffn_conv_b': 'new_v', 'new_v_w_down': 'new_v', 'new_v_norm_post_ffn': 'new_v'}


def _forward(args):
    return _fwd_reference(*[args[k] for k in FWD_PARAMS])


def _output_shape():
    def fwd():
        inp = _fwd_setup_inputs(0)
        return _fwd_reference(*[inp[k] for k in FWD_PARAMS])
    out = _jax.eval_shape(fwd)
    return out.shape, out.dtype

N_MICROBATCH = 1
ADAM_LR = 0.001
ADAM_B1 = 0.9
ADAM_B2 = 0.999
ADAM_EPS = 1e-08
ADAM_WD = 0.01
ADAM_STEP = 10
PER_EXAMPLE_BATCH_AXIS = {'x': 0, 'loss_target': 0}
SHARED_INPUTS = []
_WEIGHT_DTYPES = {'norm_pre_mix': _jnp.float32, 'w_in': _jnp.float32, 'ssm_lambda_re': _jnp.float32, 'ssm_lambda_im': _jnp.float32, 'ssm_log_step': _jnp.float32, 'ssm_b_re': _jnp.float32, 'ssm_b_im': _jnp.float32, 'ssm_c_re': _jnp.float32, 'ssm_c_im': _jnp.float32, 'ssm_d': _jnp.float32, 'ssm_glu_w': _jnp.float32, 'ssm_glu_b': _jnp.float32, 'pool_w': _jnp.float32, 'pool_b': _jnp.float32, 'pool_scale': _jnp.float32, 'w_branch_ssm': _jnp.float32, 'w_branch_pool': _jnp.float32, 'w_out': _jnp.float32, 'norm_post_mix': _jnp.float32, 'norm_pre_ffn': _jnp.float32, 'w_up': _jnp.float32, 'ffn_conv_w': _jnp.float32, 'ffn_conv_b': _jnp.float32, 'w_down': _jnp.float32, 'norm_post_ffn': _jnp.float32}
MOMENT_SCALE = {'norm_pre_mix': 1.400741e-01, 'w_in': 8.052674e-02, 'ssm_lambda_re': 3.209530e-03, 'ssm_lambda_im': 2.873715e-03, 'ssm_log_step': 2.302014e+00, 'ssm_b_re': 1.971828e-03, 'ssm_b_im': 2.002888e-03, 'ssm_c_re': 3.933201e-03, 'ssm_c_im': 3.960485e-03, 'ssm_d': 1.341967e-01, 'ssm_glu_w': 2.310008e-02, 'ssm_glu_b': 5.643118e-02, 'pool_w': 1.790888e-01, 'pool_b': 7.989518e-01, 'pool_scale': 1.815511e-01, 'w_branch_ssm': 9.123510e-02, 'w_branch_pool': 1.303644e-01, 'w_out': 1.584072e-01, 'norm_post_mix': 8.013179e+00, 'norm_pre_ffn': 1.322480e-01, 'w_up': 5.664881e-02, 'ffn_conv_w': 6.049085e-02, 'ffn_conv_b': 2.477016e-01, 'w_down': 1.058994e-01, 'norm_post_ffn': 7.991383e+00}


def _to_microbatches(a, axis):
    t = _jnp.moveaxis(a, axis, 0)
    t = t.reshape((N_MICROBATCH, t.shape[0] // N_MICROBATCH) + t.shape[1:])
    return _jnp.moveaxis(t, 1, axis + 1)


def setup_inputs(seed: int = 0) -> dict:
    inp = _fwd_setup_inputs(seed)
    key = _jax.random.fold_in(_jax.random.key(seed), 7919)
    shape, _ = _output_shape()
    out = dict(inp)
    out["loss_target"] = _jax.random.normal(_jax.random.fold_in(key, 0), shape, _jnp.float32)
    for i, name in enumerate(TWIN_WEIGHTS):
        w = inp[name].astype(_jnp.float32)
        if MOMENT_SCALE is None:
            s = _jnp.sqrt(_jnp.mean(_jnp.square(w)) + 1e-30)
        else:
            s = MOMENT_SCALE[name]
        km, kv = _jax.random.split(_jax.random.fold_in(key, i + 1))
        out[name] = w
        out["m_" + name] = s * _jax.random.normal(km, w.shape, _jnp.float32)
        out["v_" + name] = (s * s) * _jax.random.uniform(kv, w.shape, _jnp.float32, 0.5, 1.5)
    if N_MICROBATCH > 1:
        for name, axis in PER_EXAMPLE_BATCH_AXIS.items():
            out[name] = _to_microbatches(out[name], axis)
    return {'x': out['x'], 'norm_pre_mix': out['norm_pre_mix'], 'w_in': out['w_in'], 'ssm_lambda_re': out['ssm_lambda_re'], 'ssm_lambda_im': out['ssm_lambda_im'], 'ssm_log_step': out['ssm_log_step'], 'ssm_b_re': out['ssm_b_re'], 'ssm_b_im': out['ssm_b_im'], 'ssm_c_re': out['ssm_c_re'], 'ssm_c_im': out['ssm_c_im'], 'ssm_d': out['ssm_d'], 'ssm_glu_w': out['ssm_glu_w'], 'ssm_glu_b': out['ssm_glu_b'], 'pool_w': out['pool_w'], 'pool_b': out['pool_b'], 'pool_scale': out['pool_scale'], 'w_branch_ssm': out['w_branch_ssm'], 'w_branch_pool': out['w_branch_pool'], 'w_out': out['w_out'], 'norm_post_mix': out['norm_post_mix'], 'norm_pre_ffn': out['norm_pre_ffn'], 'w_up': out['w_up'], 'ffn_conv_w': out['ffn_conv_w'], 'ffn_conv_b': out['ffn_conv_b'], 'w_down': out['w_down'], 'norm_post_ffn': out['norm_post_ffn'], 'loss_target': out['loss_target'], 'm_norm_pre_mix': out['m_norm_pre_mix'], 'm_w_in': out['m_w_in'], 'm_ssm_lambda_re': out['m_ssm_lambda_re'], 'm_ssm_lambda_im': out['m_ssm_lambda_im'], 'm_ssm_log_step': out['m_ssm_log_step'], 'm_ssm_b_re': out['m_ssm_b_re'], 'm_ssm_b_im': out['m_ssm_b_im'], 'm_ssm_c_re': out['m_ssm_c_re'], 'm_ssm_c_im': out['m_ssm_c_im'], 'm_ssm_d': out['m_ssm_d'], 'm_ssm_glu_w': out['m_ssm_glu_w'], 'm_ssm_glu_b': out['m_ssm_glu_b'], 'm_pool_w': out['m_pool_w'], 'm_pool_b': out['m_pool_b'], 'm_pool_scale': out['m_pool_scale'], 'm_w_branch_ssm': out['m_w_branch_ssm'], 'm_w_branch_pool': out['m_w_branch_pool'], 'm_w_out': out['m_w_out'], 'm_norm_post_mix': out['m_norm_post_mix'], 'm_norm_pre_ffn': out['m_norm_pre_ffn'], 'm_w_up': out['m_w_up'], 'm_ffn_conv_w': out['m_ffn_conv_w'], 'm_ffn_conv_b': out['m_ffn_conv_b'], 'm_w_down': out['m_w_down'], 'm_norm_post_ffn': out['m_norm_post_ffn'], 'v_norm_pre_mix': out['v_norm_pre_mix'], 'v_w_in': out['v_w_in'], 'v_ssm_lambda_re': out['v_ssm_lambda_re'], 'v_ssm_lambda_im': out['v_ssm_lambda_im'], 'v_ssm_log_step': out['v_ssm_log_step'], 'v_ssm_b_re': out['v_ssm_b_re'], 'v_ssm_b_im': out['v_ssm_b_im'], 'v_ssm_c_re': out['v_ssm_c_re'], 'v_ssm_c_im': out['v_ssm_c_im'], 'v_ssm_d': out['v_ssm_d'], 'v_ssm_glu_w': out['v_ssm_glu_w'], 'v_ssm_glu_b': out['v_ssm_glu_b'], 'v_pool_w': out['v_pool_w'], 'v_pool_b': out['v_pool_b'], 'v_pool_scale': out['v_pool_scale'], 'v_w_branch_ssm': out['v_w_branch_ssm'], 'v_w_branch_pool': out['v_w_branch_pool'], 'v_w_out': out['v_w_out'], 'v_norm_post_mix': out['v_norm_post_mix'], 'v_norm_pre_ffn': out['v_norm_pre_ffn'], 'v_w_up': out['v_w_up'], 'v_ffn_conv_w': out['v_ffn_conv_w'], 'v_ffn_conv_b': out['v_ffn_conv_b'], 'v_w_down': out['v_w_down'], 'v_norm_post_ffn': out['v_norm_post_ffn']}


def _loss(weights, diff, rest, loss_target):
    with _jax.named_scope("forward"):
        args = {**rest, TWIN_DIFF_INPUT: diff, **{k: w.astype(_WEIGHT_DTYPES[k]) for k, w in weights.items()}}
        y = _forward(args)
    with _jax.named_scope("loss_head"):
        err = _jnp.square(y.astype(_jnp.float32) - loss_target)
        return 0.5 * _jnp.sum(_jnp.mean(err, axis=-1)) if err.ndim else 0.5 * err


def _adamw(w, g, m, v):
    m = ADAM_B1 * m + (1.0 - ADAM_B1) * g
    v = ADAM_B2 * v + (1.0 - ADAM_B2) * _jnp.square(g)
    m_hat = m / (1.0 - ADAM_B1 ** ADAM_STEP)
    v_hat = v / (1.0 - ADAM_B2 ** ADAM_STEP)
    delta = -ADAM_LR * (m_hat / (_jnp.sqrt(v_hat) + ADAM_EPS) + ADAM_WD * w)
    return delta, m, v


def reference(x, norm_pre_mix, w_in, ssm_lambda_re, ssm_lambda_im, ssm_log_step, ssm_b_re, ssm_b_im, ssm_c_re, ssm_c_im, ssm_d, ssm_glu_w, ssm_glu_b, pool_w, pool_b, pool_scale, w_branch_ssm, w_branch_pool, w_out, norm_post_mix, norm_pre_ffn, w_up, ffn_conv_w, ffn_conv_b, w_down, norm_post_ffn, loss_target, m_norm_pre_mix, m_w_in, m_ssm_lambda_re, m_ssm_lambda_im, m_ssm_log_step, m_ssm_b_re, m_ssm_b_im, m_ssm_c_re, m_ssm_c_im, m_ssm_d, m_ssm_glu_w, m_ssm_glu_b, m_pool_w, m_pool_b, m_pool_scale, m_w_branch_ssm, m_w_branch_pool, m_w_out, m_norm_post_mix, m_norm_pre_ffn, m_w_up, m_ffn_conv_w, m_ffn_conv_b, m_w_down, m_norm_post_ffn, v_norm_pre_mix, v_w_in, v_ssm_lambda_re, v_ssm_lambda_im, v_ssm_log_step, v_ssm_b_re, v_ssm_b_im, v_ssm_c_re, v_ssm_c_im, v_ssm_d, v_ssm_glu_w, v_ssm_glu_b, v_pool_w, v_pool_b, v_pool_scale, v_w_branch_ssm, v_w_branch_pool, v_w_out, v_norm_post_mix, v_norm_pre_ffn, v_w_up, v_ffn_conv_w, v_ffn_conv_b, v_w_down, v_norm_post_ffn):
    given = dict(x=x, norm_pre_mix=norm_pre_mix, w_in=w_in, ssm_lambda_re=ssm_lambda_re, ssm_lambda_im=ssm_lambda_im, ssm_log_step=ssm_log_step, ssm_b_re=ssm_b_re, ssm_b_im=ssm_b_im, ssm_c_re=ssm_c_re, ssm_c_im=ssm_c_im, ssm_d=ssm_d, ssm_glu_w=ssm_glu_w, ssm_glu_b=ssm_glu_b, pool_w=pool_w, pool_b=pool_b, pool_scale=pool_scale, w_branch_ssm=w_branch_ssm, w_branch_pool=w_branch_pool, w_out=w_out, norm_post_mix=norm_post_mix, norm_pre_ffn=norm_pre_ffn, w_up=w_up, ffn_conv_w=ffn_conv_w, ffn_conv_b=ffn_conv_b, w_down=w_down, norm_post_ffn=norm_post_ffn, loss_target=loss_target, m_norm_pre_mix=m_norm_pre_mix, m_w_in=m_w_in, m_ssm_lambda_re=m_ssm_lambda_re, m_ssm_lambda_im=m_ssm_lambda_im, m_ssm_log_step=m_ssm_log_step, m_ssm_b_re=m_ssm_b_re, m_ssm_b_im=m_ssm_b_im, m_ssm_c_re=m_ssm_c_re, m_ssm_c_im=m_ssm_c_im, m_ssm_d=m_ssm_d, m_ssm_glu_w=m_ssm_glu_w, m_ssm_glu_b=m_ssm_glu_b, m_pool_w=m_pool_w, m_pool_b=m_pool_b, m_pool_scale=m_pool_scale, m_w_branch_ssm=m_w_branch_ssm, m_w_branch_pool=m_w_branch_pool, m_w_out=m_w_out, m_norm_post_mix=m_norm_post_mix, m_norm_pre_ffn=m_norm_pre_ffn, m_w_up=m_w_up, m_ffn_conv_w=m_ffn_conv_w, m_ffn_conv_b=m_ffn_conv_b, m_w_down=m_w_down, m_norm_post_ffn=m_norm_post_ffn, v_norm_pre_mix=v_norm_pre_mix, v_w_in=v_w_in, v_ssm_lambda_re=v_ssm_lambda_re, v_ssm_lambda_im=v_ssm_lambda_im, v_ssm_log_step=v_ssm_log_step, v_ssm_b_re=v_ssm_b_re, v_ssm_b_im=v_ssm_b_im, v_ssm_c_re=v_ssm_c_re, v_ssm_c_im=v_ssm_c_im, v_ssm_d=v_ssm_d, v_ssm_glu_w=v_ssm_glu_w, v_ssm_glu_b=v_ssm_glu_b, v_pool_w=v_pool_w, v_pool_b=v_pool_b, v_pool_scale=v_pool_scale, v_w_branch_ssm=v_w_branch_ssm, v_w_branch_pool=v_w_branch_pool, v_w_out=v_w_out, v_norm_post_mix=v_norm_post_mix, v_norm_pre_ffn=v_norm_pre_ffn, v_w_up=v_w_up, v_ffn_conv_w=v_ffn_conv_w, v_ffn_conv_b=v_ffn_conv_b, v_w_down=v_w_down, v_norm_post_ffn=v_norm_post_ffn)
    weights = {n: given[n] for n in TWIN_WEIGHTS}
    shared = {n: given[n] for n in SHARED_INPUTS}
    per_example = {n: given[n] for n in ['x']}
    grad_fn = _jax.value_and_grad(_loss, argnums=(0, 1))

    def one_microbatch(ex, loss_target):
        ex = dict(ex)
        diff = ex.pop(TWIN_DIFF_INPUT)
        return grad_fn(weights, diff, {**shared, **ex}, loss_target)

    if N_MICROBATCH == 1:
        loss, (grad_w, grad_x) = one_microbatch(per_example, given["loss_target"])
    else:
        def body(carry, xs):
            loss_sum, grad_sum = carry
            l_k, (gw_k, gx_k) = one_microbatch(xs[0], xs[1])
            with _jax.named_scope("update"):
                return (loss_sum + l_k, _jax.tree.map(_jnp.add, grad_sum, gw_k)), gx_k

        init = (_jnp.zeros((), _jnp.float32), _jax.tree.map(_jnp.zeros_like, weights))
        (loss, grad_w), grad_x = _jax.lax.scan(body, init, (per_example, given["loss_target"]))
    with _jax.named_scope("update"):
        delta_w, new_m, new_v = {}, {}, {}
        for n in TWIN_WEIGHTS:
            delta_w[n], new_m[n], new_v[n] = _adamw(weights[n], grad_w[n], given["m_" + n], given["v_" + n])
    return (loss, grad_x, *[grad_w[n] for n in TWIN_WEIGHTS], *[delta_w[n] for n in TWIN_WEIGHTS],
            *[new_m[n] for n in TWIN_WEIGHTS], *[new_v[n] for n in TWIN_WEIGHTS])
```

```python
import functools
import math

import jax
import jax.numpy as jnp
from jax import lax
from jax.experimental import pallas as pl
from jax.experimental.pallas import tpu as pltpu

F32 = jnp.float32
BF16 = jnp.bfloat16
BS = pl.BlockSpec

NDEV = 8
SSM_GROUP = 16
SSM_STATE = 64
GROUPS_PER_CHUNK = 16
POOL_WINDOWS = (2, 4, 8, 16)
EPS = 1e-6
MIN_NEG_REAL = -1e-4
ADAM_LR, ADAM_B1, ADAM_B2, ADAM_EPS, ADAM_WD, ADAM_STEP = 0.001, 0.9, 0.999, 1e-08, 0.01, 10
LANES = 128
SUBLANES = 8
VMEM_LIMIT = 56 * 1024 * 1024

_DIMS = {"nn": (((1,), (0,)), ((), ())), "nt": (((1,), (1,)), ((), ())), "tn": (((0,), (0,)), ((), ()))}


def _tile(dim, pref, mult=LANES):
    if dim <= pref:
        return dim
    t = (pref // mult) * mult
    while t >= mult:
        if dim % t == 0:
            return t
        t -= mult
    return dim


def _pc(name, body, grid, ins, in_specs, outs, out_specs, scratch=()):
    multi = isinstance(outs, (list, tuple))
    return pl.pallas_call(
        body, name=name, grid=grid, in_specs=list(in_specs),
        out_specs=list(out_specs) if multi else out_specs,
        out_shape=list(outs) if multi else outs, scratch_shapes=list(scratch),
        compiler_params=pltpu.CompilerParams(dimension_semantics=("arbitrary",) * len(grid),
                                             vmem_limit_bytes=VMEM_LIMIT),
    )(*ins)


def _sds(shape, dtype):
    return jax.ShapeDtypeStruct(tuple(shape), dtype)


def _gelu(x):
    k = math.sqrt(2.0 / math.pi)
    return 0.5 * x * (1.0 + jnp.tanh(k * (x + 0.044715 * (x * x * x))))


def _gelu_grad(x):
    k = math.sqrt(2.0 / math.pi)
    t = jnp.tanh(k * (x + 0.044715 * (x * x * x)))
    return 0.5 * (1.0 + t) + 0.5 * x * (1.0 - t * t) * (k * (1.0 + 3.0 * 0.044715 * x * x))


def _sigmoid(x):
    return jax.nn.sigmoid(x)


def _fused_matmul(name, grid, pairs, acc_shapes, extras, outs, epilogue):
    n_p, n_e, n_o = len(pairs), len(extras), len(outs)
    rank = len(grid)
    nk = grid[-1]

    def body(*refs):
        ab = refs[:2 * n_p]
        ex = refs[2 * n_p:2 * n_p + n_e]
        o = refs[2 * n_p + n_e:2 * n_p + n_e + n_o]
        accs = refs[2 * n_p + n_e + n_o:]
        ids = [pl.program_id(d) for d in range(rank)]
        k = ids[-1]

        @pl.when(k == 0)
        def _():
            for acc in accs:
                acc[...] = jnp.zeros_like(acc)

        for p in range(n_p):
            a = ab[2 * p][...].astype(BF16)
            b = ab[2 * p + 1][...].astype(BF16)
            accs[pairs[p][5]][...] += lax.dot_general(a, b, _DIMS[pairs[p][4]], preferred_element_type=F32)

        @pl.when(k == nk - 1)
        def _():
            epilogue(ids, [acc[...] for acc in accs], ex, o)

    ins, in_specs = [], []
    for a, a_spec, b, b_spec, _, _ in pairs:
        ins += [a, b]
        in_specs += [a_spec, b_spec]
    for e, e_spec in extras:
        ins.append(e)
        in_specs.append(e_spec)
    res = _pc(name, body, grid, ins, in_specs, [s for s, _ in outs], [sp for _, sp in outs],
              scratch=[pltpu.VMEM(tuple(s), F32) for s in acc_shapes])
    return res


def _store(vals):
    def epilogue(ids, accs, ex, o):
        for r, v in zip(o, vals(accs)):
            r[...] = v.astype(r.dtype)
    return epilogue


def _rowsum_into(ref, first, v):
    s = jnp.sum(v, axis=0, keepdims=True)

    @pl.when(first)
    def _():
        ref[...] = s

    @pl.when(jnp.logical_not(first))
    def _():
        ref[...] += s


def _mesh_pos():
    return lax.axis_index("x"), lax.axis_index("y"), lax.axis_index("c")


def _slot(p):
    return 4 * p[0] + 2 * p[1] + p[2]


def _comm_gather(name, arrs):
    n = len(arrs)

    def body(*refs):
        ins, outs = refs[:n], refs[n:2 * n]
        send_sems, recv_sems, local_sems = refs[2 * n:]
        x, y, c = _mesh_pos()
        me, sib = (x, y, c), (x, y, 1 - c)
        chips = [(1 - x, y), (x, 1 - y), (1 - x, 1 - y)]

        def copy(r, k, block, to, src=None):
            dst = outs[r].at[_slot(block)]
            return pltpu.make_async_remote_copy(
                src_ref=dst if src is None else src, dst_ref=dst,
                send_sem=send_sems.at[r, k], recv_sem=recv_sems.at[r, k],
                device_id=to, device_id_type=pl.DeviceIdType.MESH)

        mine, first, passed = [], [], []
        for r in range(n):
            m = pltpu.make_async_copy(ins[r], outs[r].at[_slot(me)], local_sems.at[r])
            m.start()
            mine.append(m)
            f = [copy(r, 0, me, sib, src=ins[r])]
            f += [copy(r, 1 + j, me, (*chip, c), src=ins[r]) for j, chip in enumerate(chips)]
            for cp in f:
                cp.start()
            first.append(f)
        for r in range(n):
            ps = [copy(r, 4 + j, (*chip, c), sib) for j, chip in enumerate(chips)]
            for j, chip in enumerate(chips):
                copy(r, 1 + j, (*chip, c), me).wait_recv()
                ps[j].start()
            passed.append(ps)
        for r in range(n):
            copy(r, 0, sib, me).wait_recv()
            for j, chip in enumerate(chips):
                copy(r, 4 + j, (*chip, 1 - c), me).wait_recv()
            for cp in first[r] + passed[r]:
                cp.wait_send()
            mine[r].wait()

    any_spec = BS(memory_space=pl.ANY)
    return pl.pallas_call(
        body, name=name,
        in_specs=[any_spec] * n, out_specs=[any_spec] * n,
        out_shape=[_sds((NDEV,) + a.shape, a.dtype) for a in arrs],
        scratch_shapes=[pltpu.SemaphoreType.DMA((n, 7)), pltpu.SemaphoreType.DMA((n, 7)),
                        pltpu.SemaphoreType.DMA((n,))],
    )(*arrs)


def _comm_exchange(name, arrs):
    n = len(arrs)

    def body(*refs):
        ins, outs = refs[:n], refs[n:2 * n]
        send_sems, recv_sems, local_sems = refs[2 * n:]
        x, y, c = _mesh_pos()
        me = (x, y, c)
        peers = []
        for k in range(1, NDEV):
            kx, ky, kc = (k >> 2) & 1, (k >> 1) & 1, k & 1
            peers.append((1 - x if kx else x, 1 - y if ky else y, 1 - c if kc else c))

        def copy(r, k, peer):
            return pltpu.make_async_remote_copy(
                src_ref=ins[r].at[_slot(peer)], dst_ref=outs[r].at[_slot(me)],
                send_sem=send_sems.at[r, k], recv_sem=recv_sems.at[r, k],
                device_id=peer, device_id_type=pl.DeviceIdType.MESH)

        def arrival(r, k, peer):
            return pltpu.make_async_remote_copy(
                src_ref=ins[r].at[_slot(me)], dst_ref=outs[r].at[_slot(peer)],
                send_sem=send_sems.at[r, k], recv_sem=recv_sems.at[r, k],
                device_id=peer, device_id_type=pl.DeviceIdType.MESH)

        mine, sent = [], []
        for r in range(n):
            m = pltpu.make_async_copy(ins[r].at[_slot(me)], outs[r].at[_slot(me)], local_sems.at[r])
            m.start()
            mine.append(m)
            for k, peer in enumerate(peers):
                cp = copy(r, k, peer)
                cp.start()
                sent.append(cp)
        for r in range(n):
            for k, peer in enumerate(peers):
                arrival(r, k, peer).wait_recv()
            mine[r].wait()
        for cp in sent:
            cp.wait_send()

    any_spec = BS(memory_space=pl.ANY)
    return pl.pallas_call(
        body, name=name,
        in_specs=[any_spec] * n, out_specs=[any_spec] * n,
        out_shape=[_sds(a.shape, a.dtype) for a in arrs],
        scratch_shapes=[pltpu.SemaphoreType.DMA((n, 7)), pltpu.SemaphoreType.DMA((n, 7)),
                        pltpu.SemaphoreType.DMA((n,))],
    )(*arrs)


def _adamw(name, w, m, v, parts):
    R, C = w.shape
    S = parts.shape[0]
    tr = _tile(R, max(SUBLANES, (256 * 1024) // C), SUBLANES)

    def body(w_ref, m_ref, v_ref, p_ref, g_ref, d_ref, nm_ref, nv_ref):
        g = p_ref[0].astype(F32)
        for s in range(1, S):
            g = g + p_ref[s].astype(F32)
        m2 = ADAM_B1 * m_ref[...] + (1.0 - ADAM_B1) * g
        v2 = ADAM_B2 * v_ref[...] + (1.0 - ADAM_B2) * (g * g)
        m_hat = m2 / (1.0 - ADAM_B1 ** ADAM_STEP)
        v_hat = v2 / (1.0 - ADAM_B2 ** ADAM_STEP)
        g_ref[...] = g
        d_ref[...] = -ADAM_LR * (m_hat / (jnp.sqrt(v_hat) + ADAM_EPS) + ADAM_WD * w_ref[...])
        nm_ref[...] = m2
        nv_ref[...] = v2

    blk = BS((tr, C), lambda i: (i, 0))
    return _pc(name, body, (R // tr,), [w, m, v, parts],
               [blk, blk, blk, BS((S, tr, C), lambda i: (0, i, 0))],
               [_sds((R, C), F32)] * 4, [blk] * 4)


def _ssm_disc(lam_re, lam_im, log_step, br_t, bi_t):
    lr = jnp.minimum(lam_re, MIN_NEG_REAL)
    li = lam_im
    dt = jnp.exp(log_step)
    mag = jnp.exp(lr * dt)
    ang = li * dt
    ab_re = mag * jnp.cos(ang)
    ab_im = mag * jnp.sin(ang)
    nr = ab_re - 1.0
    ni = ab_im
    den = lr * lr + li * li
    f_re = (nr * lr + ni * li) / den
    f_im = (ni * lr - nr * li) / den
    bb_re = f_re[None] * br_t - f_im[None] * bi_t
    bb_im = f_re[None] * bi_t + f_im[None] * br_t
    return ab_re, ab_im, bb_re, bb_im


def _ssm_param_fwd(lam_re, lam_im, log_step, br_t, bi_t):
    G, P = lam_re.shape

    def body(lr_ref, li_ref, ls_ref, br_ref, bi_ref, pw_re_ref, pw_im_ref, bbr_ref, bbi_ref):
        ab_re, ab_im, bb_re, bb_im = _ssm_disc(lr_ref[...], li_ref[...], ls_ref[...], br_ref[...], bi_ref[...])
        bbr_ref[...] = bb_re
        bbi_ref[...] = bb_im
        pr, pi = ab_re, ab_im
        for r in range(SUBLANES):
            pw_re_ref[r] = pr
            pw_im_ref[r] = pi
            pr, pi = pr * ab_re - pi * ab_im, pr * ab_im + pi * ab_re

    full = lambda a: BS(a.shape, lambda i: (0,) * a.ndim)
    ins = [lam_re, lam_im, log_step, br_t, bi_t]
    outs = [_sds((SUBLANES, G, P), F32)] * 2 + [_sds(br_t.shape, F32)] * 2
    return _pc("ssm_param_fwd", body, (1,), ins, [full(a) for a in ins], outs, [full(o) for o in outs])


def _ssm_param_bwd(lam_re, lam_im, log_step, br_t, bi_t, d_ab_re, d_ab_im, d_bbr, d_bbi):
    def body(lr_ref, li_ref, ls_ref, br_ref, bi_ref, dar_ref, dai_ref, dbr_ref, dbi_ref,
             o_lr, o_li, o_ls, o_br, o_bi):
        prim = (lr_ref[...], li_ref[...], ls_ref[...], br_ref[...], bi_ref[...])
        _, vjp = jax.vjp(_ssm_disc, *prim)
        dar = dar_ref[0]
        dai = dai_ref[0]
        for k in range(1, dar_ref.shape[0]):
            dar = dar + dar_ref[k]
            dai = dai + dai_ref[k]
        g = vjp((dar, dai, dbr_ref[...], dbi_ref[...]))
        for r, v in zip((o_lr, o_li, o_ls, o_br, o_bi), g):
            r[...] = v

    full = lambda a: BS(a.shape, lambda i: (0,) * a.ndim)
    ins = [lam_re, lam_im, log_step, br_t, bi_t, d_ab_re, d_ab_im, d_bbr, d_bbi]
    outs = [_sds(a.shape, F32) for a in (lam_re, lam_im, log_step, br_t, bi_t)]
    return _pc("ssm_param_bwd", body, (1,), ins, [full(a) for a in ins], outs, [full(o) for o in outs])


def _bcast_row(ref, r, w):
    return jnp.broadcast_to(ref[pl.ds(r, 1), :], (SUBLANES, w))


def _pick_row(x, row, r):
    return jnp.broadcast_to(jnp.sum(jnp.where(row == r, x, 0.0), axis=0, keepdims=True), x.shape)


def _scan_fwd(bu_re, bu_im, pw_re, pw_im, nseq, L):
    N, SL = bu_re.shape
    W = _tile(SL, 256)

    def body(bre_ref, bim_ref, pre_ref, pim_ref, sre_ref, sim_ref):
        pre, pim = pre_ref[...], pim_ref[...]
        steps = [(k, _bcast_row(pre_ref, k - 1, W), _bcast_row(pim_ref, k - 1, W)) for k in (1, 2, 4)]
        row = lax.broadcasted_iota(jnp.int32, (SUBLANES, W), 0)

        def step(i, carry):
            cr, ci = carry
            r0 = pl.multiple_of(i * SUBLANES, SUBLANES)
            xr = bre_ref[pl.ds(r0, SUBLANES), :]
            xi = bim_ref[pl.ds(r0, SUBLANES), :]
            for k, ar, ai in steps:
                sr = pltpu.roll(xr, k, axis=0)
                si = pltpu.roll(xi, k, axis=0)
                keep = row >= k
                xr, xi = (xr + jnp.where(keep, ar * sr - ai * si, 0.0),
                          xi + jnp.where(keep, ar * si + ai * sr, 0.0))
            xr, xi = xr + (pre * cr - pim * ci), xi + (pre * ci + pim * cr)
            sre_ref[pl.ds(r0, SUBLANES), :] = xr
            sim_ref[pl.ds(r0, SUBLANES), :] = xi
            return _pick_row(xr, row, SUBLANES - 1), _pick_row(xi, row, SUBLANES - 1)

        zero = jnp.zeros((SUBLANES, W), F32)
        lax.fori_loop(0, L // SUBLANES, step, (zero, zero))

    blk = BS((L, W), lambda s, j: (s, j))
    pw = BS((SUBLANES, W), lambda s, j: (0, j))
    return _pc("ssm_scan_fwd", body, (nseq, SL // W), [bu_re, bu_im, pw_re, pw_im], [blk, blk, pw, pw],
               [_sds((N, SL), F32)] * 2, [blk, blk])


def _scan_bwd(ds_re, ds_im, s_re, s_im, pw_re, pw_im, pwf_re, pwf_im, nseq, L):
    N, SL = ds_re.shape
    W = _tile(SL, 256)
    nt = L // SUBLANES

    def body(dsr_ref, dsi_ref, sre_ref, sim_ref, pre_ref, pim_ref, fre_ref, fim_ref,
             lre_ref, lim_ref, dar_ref, dai_ref):
        fre, fim = fre_ref[...], -fim_ref[...]
        steps = [(k, _bcast_row(pre_ref, k - 1, W), -_bcast_row(pim_ref, k - 1, W)) for k in (1, 2, 4)]
        row = lax.broadcasted_iota(jnp.int32, (SUBLANES, W), 0)

        def step(ii, carry):
            cr, ci, acr, aci = carry
            i = nt - 1 - ii
            r0 = pl.multiple_of(i * SUBLANES, SUBLANES)
            xr = dsr_ref[pl.ds(r0, SUBLANES), :]
            xi = dsi_ref[pl.ds(r0, SUBLANES), :]
            for k, ar, ai in steps:
                sr = pltpu.roll(xr, SUBLANES - k, axis=0)
                si = pltpu.roll(xi, SUBLANES - k, axis=0)
                keep = row < SUBLANES - k
                xr, xi = (xr + jnp.where(keep, ar * sr - ai * si, 0.0),
                          xi + jnp.where(keep, ar * si + ai * sr, 0.0))
            xr, xi = xr + (fre * cr - fim * ci), xi + (fre * ci + fim * cr)
            lre_ref[pl.ds(r0, SUBLANES), :] = xr
            lim_ref[pl.ds(r0, SUBLANES), :] = xi
            p0 = pl.multiple_of(jnp.maximum(i - 1, 0) * SUBLANES, SUBLANES)
            has_prev = i > 0
            spr = jnp.where(row == 0,
                            jnp.where(has_prev, pltpu.roll(sre_ref[pl.ds(p0, SUBLANES), :], 1, axis=0), 0.0),
                            pltpu.roll(sre_ref[pl.ds(r0, SUBLANES), :], 1, axis=0))
            spi = jnp.where(row == 0,
                            jnp.where(has_prev, pltpu.roll(sim_ref[pl.ds(p0, SUBLANES), :], 1, axis=0), 0.0),
                            pltpu.roll(sim_ref[pl.ds(r0, SUBLANES), :], 1, axis=0))
            acr = acr + (xr * spr + xi * spi)
            aci = aci + (xi * spr - xr * spi)
            return _pick_row(xr, row, 0), _pick_row(xi, row, 0), acr, aci

        zero = jnp.zeros((SUBLANES, W), F32)
        _, _, acr, aci = lax.fori_loop(0, nt, step, (zero, zero, zero, zero))
        dar_ref[...] = acr
        dai_ref[...] = aci

    blk = BS((L, W), lambda s, j: (s, j))
    pw = BS((SUBLANES, W), lambda s, j: (0, j))
    da = BS((None, SUBLANES, W), lambda s, j: (s, 0, j))
    return _pc("ssm_scan_bwd", body, (nseq, SL // W),
               [ds_re, ds_im, s_re, s_im, pw_re, pw_im, pwf_re, pwf_im], [blk] * 4 + [pw] * 4,
               [_sds((N, SL), F32)] * 2 + [_sds((nseq, SUBLANES, SL), F32)] * 2, [blk, blk, da, da])


def _pool_select(g, vals):
    return jnp.where(g == 0, vals[0], jnp.where(g == 1, vals[1], jnp.where(g == 2, vals[2], vals[3])))


def _pool_fwd(proj, col0, DP, nseq, L):
    N = proj.shape[0]
    PG = DP // len(POOL_WINDOWS)
    W = _tile(PG, 256)

    def body(v_ref, z_ref):
        g = pl.program_id(1) // (PG // W)
        v = v_ref[...]
        row = lax.broadcasted_iota(jnp.int32, (L, W), 0)
        sums, s, k = [], v, 1
        for _ in POOL_WINDOWS:
            s = s + jnp.where(row >= k, pltpu.roll(s, k, axis=0), 0.0)
            sums.append(s)
            k *= 2
        win = _pool_select(g, [float(w) for w in POOL_WINDOWS])
        cnt = jnp.minimum((row + 1).astype(F32), win)
        z_ref[...] = (_pool_select(g, sums) / cnt - v).astype(z_ref.dtype)

    return _pc("pool_fwd", body, (nseq, DP // W), [proj], [BS((L, W), lambda s, j: (s, col0 // W + j))],
               _sds((N, DP), BF16), BS((L, W), lambda s, j: (s, j)))


def _pool_bwd(dz, nseq, L):
    N, DP = dz.shape
    PG = DP // len(POOL_WINDOWS)
    W = _tile(PG, 256)

    def body(dz_ref, dv_ref):
        g = pl.program_id(1) // (PG // W)
        d = dz_ref[...]
        row = lax.broadcasted_iota(jnp.int32, (L, W), 0)
        win = _pool_select(g, [float(w) for w in POOL_WINDOWS])
        s = d / jnp.minimum((row + 1).astype(F32), win)
        sums, k = [], 1
        for _ in POOL_WINDOWS:
            s = s + jnp.where(row < L - k, pltpu.roll(s, L - k, axis=0), 0.0)
            sums.append(s)
            k *= 2
        dv_ref[...] = (_pool_select(g, sums) - d).astype(dv_ref.dtype)

    blk = BS((L, W), lambda s, j: (s, j))
    return _pc("pool_bwd", body, (nseq, DP // W), [dz], [blk], _sds((N, DP), BF16), blk)


def _rstd(x):
    return lax.rsqrt(jnp.mean(x * x, axis=-1, keepdims=True) + EPS)


def _norm_bwd(dy, xhat, rstd, gain):
    t = dy * gain
    return rstd * (t - xhat * jnp.mean(t * xhat, axis=-1, keepdims=True))


def _pre_norm(x, g1):
    N, D = x.shape
    tr = _tile(N, 128, SUBLANES)

    def body(x_ref, g_ref, a_ref):
        xv = x_ref[...]
        a_ref[...] = (xv * _rstd(xv) * g_ref[...]).astype(a_ref.dtype)

    row = BS((tr, D), lambda i: (i, 0))
    vec = BS((1, D), lambda i: (0, 0))
    return _pc("pre_norm", body, (N // tr,), [x, g1], [row, vec], _sds((N, D), BF16), row)


def _mid_norm(x, o, g2, g3):
    N, D = x.shape
    tr = _tile(N, 128, SUBLANES)

    def body(x_ref, o_ref, g2_ref, g3_ref, h1_ref, c_ref):
        ov = o_ref[...]
        h1 = x_ref[...] + ov * _rstd(ov) * g2_ref[...]
        h1_ref[...] = h1
        c_ref[...] = (h1 * _rstd(h1) * g3_ref[...]).astype(c_ref.dtype)

    row = BS((tr, D), lambda i: (i, 0))
    vec = BS((1, D), lambda i: (0, 0))
    return _pc("mid_norm", body, (N // tr,), [x, o, g2, g3], [row, row, vec, vec],
               [_sds((N, D), F32), _sds((N, D), BF16)], [row, row])


def _post_ffn(h1, dn, tgt, g4):
    N, D = h1.shape
    tr = _tile(N, 128, SUBLANES)

    def body(h1_ref, dn_ref, t_ref, g_ref, dh2_ref, ddn_ref, lossv_ref, dg_ref):
        first = pl.program_id(0) == 0
        dnv = dn_ref[...]
        rstd = _rstd(dnv)
        xhat = dnv * rstd
        gain = g_ref[...]
        err = (h1_ref[...] + xhat * gain) - t_ref[...]
        dh2 = err / float(D)
        dh2_ref[...] = dh2
        ddn_ref[...] = _norm_bwd(dh2, xhat, rstd, gain).astype(ddn_ref.dtype)
        _rowsum_into(lossv_ref, first, err * err)
        _rowsum_into(dg_ref, first, dh2 * xhat)

    row = BS((tr, D), lambda i: (i, 0))
    vec = BS((1, D), lambda i: (0, 0))
    return _pc("post_ffn", body, (N // tr,), [h1, dn, tgt, g4], [row, row, row, vec],
               [_sds((N, D), F32), _sds((N, D), BF16), _sds((1, D), F32), _sds((1, D), F32)], [row, row, vec, vec])


def _mid_bwd(dh2, dc, h1, o, g2, g3):
    N, D = h1.shape
    tr = _tile(N, 128, SUBLANES)

    def body(dh2_ref, dc_ref, h1_ref, o_ref, g2_ref, g3_ref, dh1_ref, do_ref, dg2_ref, dg3_ref):
        first = pl.program_id(0) == 0
        h1 = h1_ref[...]
        r3 = _rstd(h1)
        hc = h1 * r3
        dcv = dc_ref[...]
        dh1 = dh2_ref[...] + _norm_bwd(dcv, hc, r3, g3_ref[...])
        dh1_ref[...] = dh1
        ov = o_ref[...]
        r2 = _rstd(ov)
        ho = ov * r2
        do_ref[...] = _norm_bwd(dh1, ho, r2, g2_ref[...]).astype(do_ref.dtype)
        _rowsum_into(dg3_ref, first, dcv * hc)
        _rowsum_into(dg2_ref, first, dh1 * ho)

    row = BS((tr, D), lambda i: (i, 0))
    vec = BS((1, D), lambda i: (0, 0))
    return _pc("mid_bwd", body, (N // tr,), [dh2, dc, h1, o, g2, g3], [row] * 4 + [vec, vec],
               [_sds((N, D), F32), _sds((N, D), BF16), _sds((1, D), F32), _sds((1, D), F32)], [row, row, vec, vec])


def _pre_bwd(x, da, dh1, g1):
    N, D = x.shape
    tr = _tile(N, 128, SUBLANES)

    def body(x_ref, da_ref, dh1_ref, g_ref, dx_ref, dg_ref):
        first = pl.program_id(0) == 0
        xv = x_ref[...]
        r1 = _rstd(xv)
        xh = xv * r1
        dav = da_ref[...]
        dx_ref[...] = dh1_ref[...] + _norm_bwd(dav, xh, r1, g_ref[...])
        _rowsum_into(dg_ref, first, dav * xh)

    row = BS((tr, D), lambda i: (i, 0))
    vec = BS((1, D), lambda i: (0, 0))
    return _pc("pre_bwd", body, (N // tr,), [x, da, dh1, g1], [row, row, row, vec],
               [_sds((N, D), F32), _sds((1, D), F32)], [row, vec])


def _conv_rows(x_ref, halo_ref, first):
    x = x_ref[...]
    tr = x.shape[0]
    xx = jnp.concatenate([jnp.where(first, 0.0, halo_ref[...]), x], axis=0)
    x1 = pltpu.roll(xx, 1, axis=0)[SUBLANES:]
    x2 = pltpu.roll(xx, 2, axis=0)[SUBLANES:]
    del tr
    return x, x1, x2


def _conv_apply(rows, w_ref, b_ref):
    x, x1, x2 = rows
    return ((b_ref[...] + x2 * w_ref[pl.ds(0, 1), :]) + x1 * w_ref[pl.ds(1, 1), :]) + x * w_ref[pl.ds(2, 1), :]


def _gate_specs(N, FC, TR, half):
    tile = BS((None, TR, FC), lambda jj, i: (jj + half, i, 0))
    halo = BS((None, SUBLANES, FC), lambda jj, i: (jj + half, jnp.maximum(i * (TR // SUBLANES) - 1, 0), 0))
    cw = BS((None, 3, FC), lambda jj, i: (jj + half, 0, 0))
    cb = BS((None, 1, FC), lambda jj, i: (jj + half, 0, 0))
    return tile, halo, cw, cb


def _gate_fwd(up_pre, cw, cb, L):
    nb, N, FC = up_pre.shape
    half = nb // 2
    TR = _tile(L, 128, SUBLANES)

    def body(xa_ref, ha_ref, wa_ref, ba_ref, xb_ref, hb_ref, wb_ref, bb_ref, f_ref):
        first = (pl.program_id(1) % (L // TR)) == 0
        ua = _conv_apply(_conv_rows(xa_ref, ha_ref, first), wa_ref, ba_ref)
        ub = _conv_apply(_conv_rows(xb_ref, hb_ref, first), wb_ref, bb_ref)
        f_ref[...] = (_gelu(ua) * ub).astype(f_ref.dtype)

    sa, sb = _gate_specs(N, FC, TR, 0), _gate_specs(N, FC, TR, half)
    return _pc("gate_fwd", body, (half, N // TR), [up_pre, up_pre, cw, cb] * 2, list(sa) + list(sb),
               _sds((half, N, FC), BF16), BS((None, TR, FC), lambda jj, i: (jj, i, 0)))


def _gate_bwd(up_pre, cw, cb, df, L):
    nb, N, FC = up_pre.shape
    half = nb // 2
    TR = _tile(L, 128, SUBLANES)

    def body(xa_ref, ha_ref, wa_ref, ba_ref, xb_ref, hb_ref, wb_ref, bb_ref, df_ref,
             da_ref, db_ref, dwa_ref, dwb_ref, dba_ref, dbb_ref):
        i = pl.program_id(1)
        first_row = i == 0
        first = (i % (L // TR)) == 0
        ra = _conv_rows(xa_ref, ha_ref, first)
        rb = _conv_rows(xb_ref, hb_ref, first)
        ua = _conv_apply(ra, wa_ref, ba_ref)
        ub = _conv_apply(rb, wb_ref, bb_ref)
        dfv = df_ref[...].astype(F32)
        dua = dfv * ub * _gelu_grad(ua)
        dub = dfv * _gelu(ua)
        da_ref[...] = dua
        db_ref[...] = dub
        for rows, du, dw_ref, dbias_ref in ((ra, dua, dwa_ref, dba_ref), (rb, dub, dwb_ref, dbb_ref)):
            x, x1, x2 = rows
            _rowsum_into(dbias_ref, first_row, du)
            for k, xs in enumerate((x2, x1, x)):
                _rowsum_into(dw_ref.at[pl.ds(k, 1), :], first_row, du * xs)

    sa, sb = _gate_specs(N, FC, TR, 0), _gate_specs(N, FC, TR, half)
    tile = BS((None, TR, FC), lambda jj, i: (jj, i, 0))
    dw = BS((None, 3, FC), lambda jj, i: (jj, 0, 0))
    dbias = BS((None, 1, FC), lambda jj, i: (jj, 0, 0))
    return _pc("gate_bwd", body, (half, N // TR), [up_pre, up_pre, cw, cb] * 2 + [df], list(sa) + list(sb) + [tile],
               [_sds((half, N, FC), F32)] * 2 + [_sds((half, 3, FC), F32)] * 2 + [_sds((half, 1, FC), F32)] * 2,
               [tile, tile, dw, dw, dbias, dbias])


def _conv_bwd(dup, cw, L, half):
    nb, N, FC = dup.shape
    TR = _tile(L, 128, SUBLANES)
    nrb = N // SUBLANES

    def body(x_ref, h_ref, w_ref, o_ref):
        last = ((pl.program_id(1) + 1) % (L // TR)) == 0
        x = x_ref[...]
        xx = jnp.concatenate([x, jnp.where(last, 0.0, h_ref[...])], axis=0)
        x1 = pltpu.roll(xx, TR + SUBLANES - 1, axis=0)[:TR]
        x2 = pltpu.roll(xx, TR + SUBLANES - 2, axis=0)[:TR]
        o_ref[...] = (x * w_ref[pl.ds(2, 1), :] + x1 * w_ref[pl.ds(1, 1), :] + x2 * w_ref[pl.ds(0, 1), :]
                      ).astype(o_ref.dtype)

    tile = BS((None, TR, FC), lambda jj, i: (jj, i, 0))
    halo = BS((None, SUBLANES, FC), lambda jj, i: (jj, jnp.minimum((i + 1) * (TR // SUBLANES), nrb - 1), 0))
    w = BS((None, 3, FC), lambda jj, i: (jj + half, 0, 0))
    return _pc("conv_bwd_%d" % half, body, (nb, N // TR), [dup, dup, cw], [tile, halo, w],
               _sds((nb, N, FC), BF16), tile)


def _pack(arrs):
    parts = []
    for a in arrs:
        flat = a.reshape(-1).astype(F32)
        pad = (-flat.shape[0]) % (SUBLANES * LANES)
        parts.append(jnp.pad(flat, (0, pad)))
    return jnp.concatenate(parts).reshape(-1, LANES)


def _unpack(packed, shapes):
    flat = packed.reshape(-1)
    out, off = [], 0
    for s in shapes:
        n = math.prod(s)
        out.append(flat[off:off + n].reshape(s))
        off += n + ((-n) % (SUBLANES * LANES))
    return out


def _small_sum(gathered, loss_rows, d_model):
    S, R, C = gathered.shape

    def body(p_ref, tot_ref, loss_ref):
        t = p_ref[0]
        for s in range(1, S):
            t = t + p_ref[s]
        tot_ref[...] = t
        loss_ref[...] = jnp.full((1, 1), 0.5 / d_model, F32) * jnp.sum(t[:loss_rows])

    return _pc("small_sum", body, (1,), [gathered], [BS((S, R, C), lambda i: (0, 0, 0))],
               [_sds((R, C), F32), _sds((1, 1), F32)], [BS((R, C), lambda i: (0, 0)), BS((1, 1), lambda i: (0, 0))])


def _block_diag_in(bb_t, nch):
    J, G, P = bb_t.shape
    gl = G // nch
    b = bb_t.reshape(J, nch, gl, P).transpose(1, 0, 2, 3)
    eye = jnp.eye(gl, dtype=F32)
    w = eye[None, :, None, :, None] * b[:, None, :, :, :]
    return w.reshape(nch, gl * J, gl * P)


def _block_diag_in_grad(dw, J, G, P):
    nch = dw.shape[0]
    gl = G // nch
    d = dw.reshape(nch, gl, J, gl, P)
    d = jnp.einsum("cgjgp->jcgp", d)
    return d.reshape(J, G, P)


def _block_diag_out(c, nch):
    G, J, P = c.shape
    gl = G // nch
    cc = c.reshape(nch, gl, J, P).transpose(0, 1, 3, 2)
    eye = jnp.eye(gl, dtype=F32)
    w = cc[:, :, :, None, :] * eye[None, :, None, :, None]
    return w.reshape(nch, gl * P, gl * J)


def _block_diag_out_grad(dw, G, J, P):
    nch = dw.shape[0]
    gl = G // nch
    d = dw.reshape(nch, gl, P, gl, J)
    d = jnp.einsum("cgpgj->cgjp", d)
    return d.reshape(G, J, P)


def kernel(x, norm_pre_mix, w_in, ssm_lambda_re, ssm_lambda_im, ssm_log_step, ssm_b_re, ssm_b_im, ssm_c_re, ssm_c_im, ssm_d, ssm_glu_w, ssm_glu_b, pool_w, pool_b, pool_scale, w_branch_ssm, w_branch_pool, w_out, norm_post_mix, norm_pre_ffn, w_up, ffn_conv_w, ffn_conv_b, w_down, norm_post_ffn, loss_target, m_norm_pre_mix, m_w_in, m_ssm_lambda_re, m_ssm_lambda_im, m_ssm_log_step, m_ssm_b_re, m_ssm_b_im, m_ssm_c_re, m_ssm_c_im, m_ssm_d, m_ssm_glu_w, m_ssm_glu_b, m_pool_w, m_pool_b, m_pool_scale, m_w_branch_ssm, m_w_branch_pool, m_w_out, m_norm_post_mix, m_norm_pre_ffn, m_w_up, m_ffn_conv_w, m_ffn_conv_b, m_w_down, m_norm_post_ffn, v_norm_pre_mix, v_w_in, v_ssm_lambda_re, v_ssm_lambda_im, v_ssm_log_step, v_ssm_b_re, v_ssm_b_im, v_ssm_c_re, v_ssm_c_im, v_ssm_d, v_ssm_glu_w, v_ssm_glu_b, v_pool_w, v_pool_b, v_pool_scale, v_w_branch_ssm, v_w_branch_pool, v_w_out, v_norm_post_mix, v_norm_pre_ffn, v_w_up, v_ffn_conv_w, v_ffn_conv_b, v_w_down, v_norm_post_ffn):
    args = dict(locals())
    names = ["norm_pre_mix", "w_in", "ssm_lambda_re", "ssm_lambda_im", "ssm_log_step", "ssm_b_re", "ssm_b_im",
             "ssm_c_re", "ssm_c_im", "ssm_d", "ssm_glu_w", "ssm_glu_b", "pool_w", "pool_b", "pool_scale",
             "w_branch_ssm", "w_branch_pool", "w_out", "norm_post_mix", "norm_pre_ffn", "w_up", "ffn_conv_w",
             "ffn_conv_b", "w_down", "norm_post_ffn"]

    nseq, L, D = x.shape
    N = nseq * L
    U = D // NDEV
    DS = ssm_d.shape[1]
    DP = pool_scale.shape[1]
    G, P, J = ssm_b_re.shape[1:]
    SL = G * P
    CH = GROUPS_PER_CHUNK * J
    CS = GROUPS_PER_CHUNK * P
    NCH = DS // CH
    NPG = len(POOL_WINDOWS)
    PG = DP // NPG
    FC = w_up.shape[2]
    NB = NDEV
    HB = NB // 2
    F2 = NB * FC
    dev = _slot(_mesh_pos())
    tm = _tile(N, 1024)
    tm2 = _tile(N, 512)

    x2 = x.reshape(N, D)
    tgt = loss_target.reshape(N, D)

    shards = [w_in[0], ssm_glu_w[0], pool_w[0], w_branch_ssm[0], w_branch_pool[0], w_out[0], w_up[0], w_down[0]]
    gathered = _comm_gather("comm_gather_weights",
                            [s.astype(BF16) for s in shards] + [pool_b[0], ffn_conv_w[0]])
    Win, Wglu, Wpool, Wbs, Wbp, Wout, Wup, Wdown, pool_b_all, conv_w_all = gathered
    Wglu = Wglu.reshape(DS, DS)
    Wpool = Wpool.transpose(1, 0, 2, 3).reshape(NPG, PG, PG)
    Wout = Wout.reshape(D, D)
    Wdown = Wdown.reshape(HB, FC, D)
    pool_b_full = pool_b_all.transpose(1, 0, 2).reshape(1, DP)
    conv_b_blk = ffn_conv_b.reshape(NB, 1, FC)

    lam_re, lam_im = ssm_lambda_re[0], ssm_lambda_im[0]
    log_step = ssm_log_step.reshape(G, 1)
    br_t = ssm_b_re[0].transpose(2, 0, 1)
    bi_t = ssm_b_im[0].transpose(2, 0, 1)
    pw_re3, pw_im3, bb_re, bb_im = _ssm_param_fwd(lam_re, lam_im, log_step, br_t, bi_t)
    pw_re, pw_im = pw_re3.reshape(SUBLANES, SL), pw_im3.reshape(SUBLANES, SL)
    pwf_re, pwf_im = pw_re[::-1], pw_im[::-1]
    WB = jnp.concatenate([_block_diag_in(bb_re, NCH), _block_diag_in(bb_im, NCH)], axis=2).astype(BF16)
    WCre = _block_diag_out(ssm_c_re[0], NCH).astype(BF16)
    WCim = _block_diag_out(-ssm_c_im[0], NCH).astype(BF16)

    a = _pre_norm(x2, norm_pre_mix)
    nq = 3 * NDEV
    (proj,) = _fused_matmul(
        "in_proj", (N // tm, nq, 1),
        [(a, BS((tm, D), lambda i, q, k: (i, 0)), Win, BS((None, D, U), lambda i, q, k: (q // 3, 0, q % 3)), "nn", 0)],
        [(tm, U)], [], [(_sds((N, 3 * D), F32), BS((tm, U), lambda i, q, k: (i, q)))],
        _store(lambda accs: accs))

    bu_re, bu_im = _fused_matmul(
        "ssm_in", (N // tm2, NCH, 1),
        [(proj, BS((tm2, CH), lambda i, c, k: (i, c)), WB, BS((None, CH, 2 * CS), lambda i, c, k: (c, 0, 0)), "nn", 0)],
        [(tm2, 2 * CS)], [],
        [(_sds((N, SL), F32), BS((tm2, CS), lambda i, c, k: (i, c)))] * 2,
        _store(lambda accs: (accs[0][:, :CS], accs[0][:, CS:])))
    s_re, s_im = _scan_fwd(bu_re, bu_im, pw_re, pw_im, nseq, L)

    def ssm_out_epi(ids, accs, ex, o):
        u_ref, d_ref = ex
        y0 = accs[0] + d_ref[...] * u_ref[...]
        o[0][...] = y0
        o[1][...] = _gelu(y0).astype(BF16)

    y0, y1 = _fused_matmul(
        "ssm_out", (N // tm2, NCH, 1),
        [(s_re, BS((tm2, CS), lambda i, c, k: (i, c)), WCre, BS((None, CS, CH), lambda i, c, k: (c, 0, 0)), "nn", 0),
         (s_im, BS((tm2, CS), lambda i, c, k: (i, c)), WCim, BS((None, CS, CH), lambda i, c, k: (c, 0, 0)), "nn", 0)],
        [(tm2, CH)],
        [(proj, BS((tm2, CH), lambda i, c, k: (i, c))), (ssm_d, BS((1, CH), lambda i, c, k: (0, c)))],
        [(_sds((N, DS), F32), BS((tm2, CH), lambda i, c, k: (i, c))),
         (_sds((N, DS), BF16), BS((tm2, CH), lambda i, c, k: (i, c)))],
        ssm_out_epi)

    tn_s = _tile(DS, 512)

    def glu_epi(ids, accs, ex, o):
        y0_ref, b_ref = ex
        zg = accs[0] + b_ref[...]
        o[0][...] = zg
        o[1][...] = (_gelu(y0_ref[...]) * _sigmoid(zg)).astype(BF16)

    zg, ys = _fused_matmul(
        "ssm_glu", (N // tm, DS // tn_s, 1),
        [(y1, BS((tm, DS), lambda i, j, k: (i, 0)), Wglu, BS((DS, tn_s), lambda i, j, k: (0, j)), "nn", 0)],
        [(tm, tn_s)],
        [(y0, BS((tm, tn_s), lambda i, j, k: (i, j))), (ssm_glu_b, BS((1, tn_s), lambda i, j, k: (0, j)))],
        [(_sds((N, DS), F32), BS((tm, tn_s), lambda i, j, k: (i, j))),
         (_sds((N, DS), BF16), BS((tm, tn_s), lambda i, j, k: (i, j)))],
        glu_epi)

    z = _pool_fwd(proj, DS, DP, nseq, L)

    def pool_mm_epi(ids, accs, ex, o):
        b_ref, sc_ref = ex
        q = accs[0] + b_ref[...]
        o[0][...] = q
        o[1][...] = (q * sc_ref[...]).astype(BF16)

    qp, yp = _fused_matmul(
        "pool_mm", (N // tm, NPG, 1),
        [(z, BS((tm, PG), lambda i, g, k: (i, g)), Wpool, BS((None, PG, PG), lambda i, g, k: (g, 0, 0)), "nn", 0)],
        [(tm, PG)],
        [(pool_b_full, BS((1, PG), lambda i, g, k: (0, g))), (pool_scale, BS((1, PG), lambda i, g, k: (0, g)))],
        [(_sds((N, DP), F32), BS((tm, PG), lambda i, g, k: (i, g))),
         (_sds((N, DP), BF16), BS((tm, PG), lambda i, g, k: (i, g)))],
        pool_mm_epi)

    gs_blk = BS((tm2, U), lambda i, q, k: (i, (DS + DP) // U + q))
    gp_blk = BS((tm2, U), lambda i, q, k: (i, (DS + DP + D) // U + q))
    out_blk = BS((tm2, U), lambda i, q, k: (i, q))

    def branch_epi(ids, accs, ex, o):
        gs_ref, gp_ref = ex
        o[0][...] = accs[0]
        o[1][...] = accs[1]
        o[2][...] = (_sigmoid(gs_ref[...]) * accs[0] + _sigmoid(gp_ref[...]) * accs[1]).astype(BF16)

    Ys, Yp, merged = _fused_matmul(
        "branch", (N // tm2, NDEV, 1),
        [(ys, BS((tm2, DS), lambda i, q, k: (i, 0)), Wbs, BS((None, DS, U), lambda i, q, k: (q, 0, 0)), "nn", 0),
         (yp, BS((tm2, DP), lambda i, q, k: (i, 0)), Wbp, BS((None, DP, U), lambda i, q, k: (q, 0, 0)), "nn", 1)],
        [(tm2, U), (tm2, U)],
        [(proj, gs_blk), (proj, gp_blk)],
        [(_sds((N, D), F32), out_blk), (_sds((N, D), F32), out_blk), (_sds((N, D), BF16), out_blk)],
        branch_epi)

    tn_d = _tile(D, 512)
    (o_mix,) = _fused_matmul(
        "out_proj", (N // tm, D // tn_d, 1),
        [(merged, BS((tm, D), lambda i, j, k: (i, 0)), Wout, BS((D, tn_d), lambda i, j, k: (0, j)), "nn", 0)],
        [(tm, tn_d)], [], [(_sds((N, D), F32), BS((tm, tn_d), lambda i, j, k: (i, j)))],
        _store(lambda accs: accs))
    h1, c = _mid_norm(x2, o_mix, norm_post_mix, norm_pre_ffn)

    tk_d = _tile(D, 1024)
    (up_pre,) = _fused_matmul(
        "ffn_up", (N // tm2, NB, D // tk_d),
        [(c, BS((tm2, tk_d), lambda i, j, k: (i, k)), Wup, BS((None, tk_d, FC), lambda i, j, k: (j, k, 0)), "nn", 0)],
        [(tm2, FC)], [], [(_sds((NB, N, FC), F32), BS((None, tm2, FC), lambda i, j, k: (j, i, 0)))],
        _store(lambda accs: accs))
    f = _gate_fwd(up_pre, conv_w_all, conv_b_blk, L)
    tn_d2 = _tile(D, 1024)
    (dn,) = _fused_matmul(
        "ffn_down", (N // tm2, D // tn_d2, HB),
        [(f, BS((None, tm2, FC), lambda i, j, k: (k, i, 0)), Wdown, BS((None, FC, tn_d2), lambda i, j, k: (k, 0, j)), "nn", 0)],
        [(tm2, tn_d2)], [], [(_sds((N, D), F32), BS((tm2, tn_d2), lambda i, j, k: (i, j)))],
        _store(lambda accs: accs))
    dh2, d_dn, lossv, dg4 = _post_ffn(h1, dn, tgt, norm_post_ffn)

    (df,) = _fused_matmul(
        "ffn_down_dx", (N // tm2, HB, D // tk_d),
        [(d_dn, BS((tm2, tk_d), lambda i, j, k: (i, k)), Wdown, BS((None, FC, tk_d), lambda i, j, k: (j, 0, k)), "nt", 0)],
        [(tm2, FC)], [], [(_sds((HB, N, FC), BF16), BS((None, tm2, FC), lambda i, j, k: (j, i, 0)))],
        _store(lambda accs: accs))
    tk_n = _tile(N, 1024)
    (gW_down,) = _fused_matmul(
        "ffn_down_dw", (HB, D // tn_d, N // tk_n),
        [(f, BS((None, tk_n, FC), lambda j, n, k: (j, k, 0)), d_dn, BS((tk_n, tn_d), lambda j, n, k: (k, n)), "tn", 0)],
        [(FC, tn_d)], [], [(_sds((HB, FC, D), BF16), BS((None, FC, tn_d), lambda j, n, k: (j, 0, n)))],
        _store(lambda accs: accs))
    dup_a, dup_b, dcw_a, dcw_b, dcb_a, dcb_b = _gate_bwd(up_pre, conv_w_all, conv_b_blk, df, L)
    dpre_a = _conv_bwd(dup_a, conv_w_all, L, 0)
    dpre_b = _conv_bwd(dup_b, conv_w_all, L, HB)
    (dc,) = _fused_matmul(
        "ffn_up_dx", (N // tm2, D // tn_d, HB),
        [(dpre_a, BS((None, tm2, FC), lambda i, j, k: (k, i, 0)), Wup, BS((None, tn_d, FC), lambda i, j, k: (k, j, 0)), "nt", 0),
         (dpre_b, BS((None, tm2, FC), lambda i, j, k: (k, i, 0)), Wup, BS((None, tn_d, FC), lambda i, j, k: (k + HB, j, 0)), "nt", 0)],
        [(tm2, tn_d)], [], [(_sds((N, D), F32), BS((tm2, tn_d), lambda i, j, k: (i, j)))],
        _store(lambda accs: accs))
    tk_n2 = _tile(N, 512)
    tm_d = _tile(D, 512)
    gW_up_a, gW_up_b = _fused_matmul(
        "ffn_up_dw", (HB, D // tm_d, N // tk_n2),
        [(c, BS((tk_n2, tm_d), lambda j, n, k: (k, n)), dpre_a, BS((None, tk_n2, FC), lambda j, n, k: (j, k, 0)), "tn", 0),
         (c, BS((tk_n2, tm_d), lambda j, n, k: (k, n)), dpre_b, BS((None, tk_n2, FC), lambda j, n, k: (j, k, 0)), "tn", 1)],
        [(tm_d, FC), (tm_d, FC)], [],
        [(_sds((HB, D, FC), BF16), BS((None, tm_d, FC), lambda j, n, k: (j, n, 0)))] * 2,
        _store(lambda accs: accs))
    gW_up = jnp.concatenate([gW_up_a, gW_up_b], axis=0)

    dh1, d_o, dg2, dg3 = _mid_bwd(dh2, dc, h1, o_mix, norm_post_mix, norm_pre_ffn)

    def dmerged_epi(ids, accs, ex, o):
        gs_ref, gp_ref, ys_ref, yp_ref = ex
        dm = accs[0]
        sg_s, sg_p = _sigmoid(gs_ref[...]), _sigmoid(gp_ref[...])
        o[0][...] = (dm * sg_s).astype(BF16)
        o[1][...] = (dm * sg_p).astype(BF16)
        o[2][...] = (dm * ys_ref[...] * sg_s * (1.0 - sg_s)).astype(BF16)
        o[3][...] = (dm * yp_ref[...] * sg_p * (1.0 - sg_p)).astype(BF16)

    dYs, dYp, dgs, dgp = _fused_matmul(
        "out_proj_dx", (N // tm2, NDEV, 1),
        [(d_o, BS((tm2, D), lambda i, q, k: (i, 0)), Wout, BS((U, D), lambda i, q, k: (q, 0)), "nt", 0)],
        [(tm2, U)],
        [(proj, gs_blk), (proj, gp_blk), (Ys, out_blk), (Yp, out_blk)],
        [(_sds((N, D), BF16), out_blk)] * 4,
        dmerged_epi)
    (gW_out,) = _fused_matmul(
        "out_proj_dw", (D // tm_d, D // tn_d, 1),
        [(merged, BS((N, tm_d), lambda i, j, k: (0, i)), d_o, BS((N, tn_d), lambda i, j, k: (0, j)), "tn", 0)],
        [(tm_d, tn_d)], [], [(_sds((D, D), BF16), BS((tm_d, tn_d), lambda i, j, k: (i, j)))],
        _store(lambda accs: accs))
    tm_s = _tile(DS, 512)
    gW_bs, gW_bp = _fused_matmul(
        "branch_dw", (DS // tm_s, NDEV, 1),
        [(ys, BS((N, tm_s), lambda i, q, k: (0, i)), dYs, BS((N, U), lambda i, q, k: (0, q)), "tn", 0),
         (yp, BS((N, tm_s), lambda i, q, k: (0, i)), dYp, BS((N, U), lambda i, q, k: (0, q)), "tn", 1)],
        [(tm_s, U), (tm_s, U)], [],
        [(_sds((NDEV, DS, U), BF16), BS((None, tm_s, U), lambda i, q, k: (q, i, 0)))] * 2,
        _store(lambda accs: accs))

    tn_p = _tile(PG, 512)

    def dyp_epi(ids, accs, ex, o):
        q_ref, sc_ref = ex
        first = ids[1] == 0
        dyp = accs[0]
        dq = dyp * sc_ref[...]
        o[0][...] = dq.astype(BF16)
        _rowsum_into(o[1], first, dyp * q_ref[...])
        _rowsum_into(o[2], first, dq)

    dq, d_pscale, d_pb = _fused_matmul(
        "branch_pool_dx", (DP // tn_p, N // tm, NDEV),
        [(dYp, BS((tm, U), lambda j, i, k: (i, k)), Wbp, BS((None, tn_p, U), lambda j, i, k: (k, j, 0)), "nt", 0)],
        [(tm, tn_p)],
        [(qp, BS((tm, tn_p), lambda j, i, k: (i, j))), (pool_scale, BS((1, tn_p), lambda j, i, k: (0, j)))],
        [(_sds((N, DP), BF16), BS((tm, tn_p), lambda j, i, k: (i, j))),
         (_sds((1, DP), F32), BS((1, tn_p), lambda j, i, k: (0, j))),
         (_sds((1, DP), F32), BS((1, tn_p), lambda j, i, k: (0, j)))],
        dyp_epi)
    (dz,) = _fused_matmul(
        "pool_mm_dx", (N // tm, NPG, 1),
        [(dq, BS((tm, PG), lambda i, g, k: (i, g)), Wpool, BS((None, PG, PG), lambda i, g, k: (g, 0, 0)), "nt", 0)],
        [(tm, PG)], [], [(_sds((N, DP), F32), BS((tm, PG), lambda i, g, k: (i, g)))],
        _store(lambda accs: accs))
    (gW_pool,) = _fused_matmul(
        "pool_mm_dw", (NPG, 1),
        [(z, BS((N, PG), lambda g, k: (0, g)), dq, BS((N, PG), lambda g, k: (0, g)), "tn", 0)],
        [(PG, PG)], [], [(_sds((NPG, PG, PG), BF16), BS((None, PG, PG), lambda g, k: (g, 0, 0)))],
        _store(lambda accs: accs))
    du_pool = _pool_bwd(dz, nseq, L)

    def dys_epi(ids, accs, ex, o):
        zg_ref, y0_ref = ex
        first = ids[1] == 0
        dys = accs[0]
        sg = _sigmoid(zg_ref[...])
        dzg = dys * _gelu(y0_ref[...]) * sg * (1.0 - sg)
        o[0][...] = dzg.astype(BF16)
        o[1][...] = dys * sg
        _rowsum_into(o[2], first, dzg)

    dzg, dy1_direct, d_glu_b = _fused_matmul(
        "branch_ssm_dx", (DS // tn_s, N // tm, NDEV),
        [(dYs, BS((tm, U), lambda j, i, k: (i, k)), Wbs, BS((None, tn_s, U), lambda j, i, k: (k, j, 0)), "nt", 0)],
        [(tm, tn_s)],
        [(zg, BS((tm, tn_s), lambda j, i, k: (i, j))), (y0, BS((tm, tn_s), lambda j, i, k: (i, j)))],
        [(_sds((N, DS), BF16), BS((tm, tn_s), lambda j, i, k: (i, j))),
         (_sds((N, DS), F32), BS((tm, tn_s), lambda j, i, k: (i, j))),
         (_sds((1, DS), F32), BS((1, tn_s), lambda j, i, k: (0, j)))],
        dys_epi)
    (gW_glu,) = _fused_matmul(
        "ssm_glu_dw", (DS // tm_s, DS // tn_s, 1),
        [(y1, BS((N, tm_s), lambda i, j, k: (0, i)), dzg, BS((N, tn_s), lambda i, j, k: (0, j)), "tn", 0)],
        [(tm_s, tn_s)], [], [(_sds((DS, DS), BF16), BS((tm_s, tn_s), lambda i, j, k: (i, j)))],
        _store(lambda accs: accs))

    tn_c = _tile(DS, CH)

    def dy0_epi(ids, accs, ex, o):
        d1_ref, y0_ref, u_ref = ex
        first = ids[1] == 0
        dy0 = (accs[0] + d1_ref[...]) * _gelu_grad(y0_ref[...])
        o[0][...] = dy0
        _rowsum_into(o[1], first, dy0 * u_ref[...])

    dy0, d_ssm_d = _fused_matmul(
        "ssm_glu_dx", (DS // tn_c, N // tm, 1),
        [(dzg, BS((tm, DS), lambda j, i, k: (i, 0)), Wglu, BS((tn_c, DS), lambda j, i, k: (j, 0)), "nt", 0)],
        [(tm, tn_c)],
        [(dy1_direct, BS((tm, tn_c), lambda j, i, k: (i, j))), (y0, BS((tm, tn_c), lambda j, i, k: (i, j))),
         (proj, BS((tm, tn_c), lambda j, i, k: (i, j)))],
        [(_sds((N, DS), F32), BS((tm, tn_c), lambda j, i, k: (i, j))),
         (_sds((1, DS), F32), BS((1, tn_c), lambda j, i, k: (0, j)))],
        dy0_epi)

    ds_re, ds_im = _fused_matmul(
        "ssm_out_dx", (N // tm2, NCH, 1),
        [(dy0, BS((tm2, CH), lambda i, c, k: (i, c)), WCre, BS((None, CS, CH), lambda i, c, k: (c, 0, 0)), "nt", 0),
         (dy0, BS((tm2, CH), lambda i, c, k: (i, c)), WCim, BS((None, CS, CH), lambda i, c, k: (c, 0, 0)), "nt", 1)],
        [(tm2, CS), (tm2, CS)], [],
        [(_sds((N, SL), F32), BS((tm2, CS), lambda i, c, k: (i, c)))] * 2,
        _store(lambda accs: accs))
    dWCre, dWCim = _fused_matmul(
        "ssm_out_dw", (NCH, N // tk_n),
        [(s_re, BS((tk_n, CS), lambda c, k: (k, c)), dy0, BS((tk_n, CH), lambda c, k: (k, c)), "tn", 0),
         (s_im, BS((tk_n, CS), lambda c, k: (k, c)), dy0, BS((tk_n, CH), lambda c, k: (k, c)), "tn", 1)],
        [(CS, CH), (CS, CH)], [],
        [(_sds((NCH, CS, CH), F32), BS((None, CS, CH), lambda c, k: (c, 0, 0)))] * 2,
        _store(lambda accs: accs))
    lam_r, lam_i, d_ab_re, d_ab_im = _scan_bwd(ds_re, ds_im, s_re, s_im, pw_re, pw_im, pwf_re, pwf_im, nseq, L)

    def du_epi(ids, accs, ex, o):
        dy0_ref, d_ref = ex
        o[0][...] = (accs[0] + dy0_ref[...] * d_ref[...]).astype(BF16)

    (du_ssm,) = _fused_matmul(
        "ssm_in_dx", (N // tm2, NCH, 1),
        [(lam_r, BS((tm2, CS), lambda i, c, k: (i, c)), WB, BS((None, CH, CS), lambda i, c, k: (c, 0, 0)), "nt", 0),
         (lam_i, BS((tm2, CS), lambda i, c, k: (i, c)), WB, BS((None, CH, CS), lambda i, c, k: (c, 0, 1)), "nt", 0)],
        [(tm2, CH)],
        [(dy0, BS((tm2, CH), lambda i, c, k: (i, c))), (ssm_d, BS((1, CH), lambda i, c, k: (0, c)))],
        [(_sds((N, DS), BF16), BS((tm2, CH), lambda i, c, k: (i, c)))],
        du_epi)
    dWBre, dWBim = _fused_matmul(
        "ssm_in_dw", (NCH, N // tk_n),
        [(proj, BS((tk_n, CH), lambda c, k: (k, c)), lam_r, BS((tk_n, CS), lambda c, k: (k, c)), "tn", 0),
         (proj, BS((tk_n, CH), lambda c, k: (k, c)), lam_i, BS((tk_n, CS), lambda c, k: (k, c)), "tn", 1)],
        [(CH, CS), (CH, CS)], [],
        [(_sds((NCH, CH, CS), F32), BS((None, CH, CS), lambda c, k: (c, 0, 0)))] * 2,
        _store(lambda accs: accs))
    d_bbr = _block_diag_in_grad(dWBre, J, G, P)
    d_bbi = _block_diag_in_grad(dWBim, J, G, P)
    d_lam_re, d_lam_im, d_log_step, d_br_t, d_bi_t = _ssm_param_bwd(
        lam_re, lam_im, log_step, br_t, bi_t,
        d_ab_re.reshape(nseq * SUBLANES, G, P), d_ab_im.reshape(nseq * SUBLANES, G, P), d_bbr, d_bbi)
    d_c_re = _block_diag_out_grad(dWCre, G, J, P)
    d_c_im = -_block_diag_out_grad(dWCim, G, J, P)

    dproj = jnp.concatenate([du_ssm, du_pool, dgs, dgp], axis=1)
    (da,) = _fused_matmul(
        "in_proj_dx", (N // tm, D // tn_d2, nq),
        [(dproj, BS((tm, U), lambda i, j, k: (i, k)), Win, BS((None, tn_d2, U), lambda i, j, k: (k // 3, j, k % 3)), "nt", 0)],
        [(tm, tn_d2)], [], [(_sds((N, D), F32), BS((tm, tn_d2), lambda i, j, k: (i, j)))],
        _store(lambda accs: accs))
    (gW_in,) = _fused_matmul(
        "in_proj_dw", (D // tm_d, nq, 1),
        [(a, BS((N, tm_d), lambda i, q, k: (0, i)), dproj, BS((N, U), lambda i, q, k: (0, q)), "tn", 0)],
        [(tm_d, U)], [], [(_sds((NDEV, D, 3 * U), BF16), BS((None, tm_d, U), lambda i, q, k: (q // 3, i, q % 3)))],
        _store(lambda accs: accs))
    grad_x, dg1 = _pre_bwd(x2, da, dh1, norm_pre_mix)

    d_conv_w = jnp.concatenate([dcw_a, dcw_b], axis=0).transpose(1, 0, 2).reshape(3, F2)
    d_conv_b = jnp.concatenate([dcb_a, dcb_b], axis=0).reshape(1, F2)
    small = {
        "norm_pre_mix": dg1, "norm_post_mix": dg2, "norm_pre_ffn": dg3, "norm_post_ffn": dg4,
        "ssm_lambda_re": d_lam_re[None], "ssm_lambda_im": d_lam_im[None], "ssm_log_step": d_log_step.reshape(1, G),
        "ssm_b_re": d_br_t.transpose(1, 2, 0)[None], "ssm_b_im": d_bi_t.transpose(1, 2, 0)[None],
        "ssm_c_re": d_c_re[None], "ssm_c_im": d_c_im[None],
        "ssm_d": d_ssm_d, "ssm_glu_b": d_glu_b, "pool_scale": d_pscale,
        "pool_b": d_pb.reshape(1, NPG, PG), "ffn_conv_w": d_conv_w[None], "ffn_conv_b": d_conv_b,
    }
    small_names = list(small)
    packed = _pack([lossv] + [small[n] for n in small_names])
    (small_all,) = _comm_gather("comm_gather_small", [packed])
    loss_rows = (D + SUBLANES * LANES - 1) // (SUBLANES * LANES) * SUBLANES
    total, loss = _small_sum(small_all, loss_rows, D)
    totals = dict(zip(small_names, _unpack(total, [lossv.shape] + [small[n].shape for n in small_names])[1:]))
    totals["pool_b"] = lax.dynamic_slice_in_dim(totals["pool_b"], dev * (PG // NDEV), PG // NDEV, axis=2)
    totals["ffn_conv_w"] = lax.dynamic_slice_in_dim(totals["ffn_conv_w"], dev * FC, FC, axis=2)
    sm_g = _pack([totals[n] for n in small_names])
    sm_w, sm_m, sm_v = (_pack([args[p + n] for n in small_names]) for p in ("", "m_", "v_"))
    _, sm_d, sm_nm, sm_nv = _adamw("adamw_small", sm_w, sm_m, sm_v, sm_g[None])
    shapes = [args[n].shape for n in small_names]
    res = {n: (totals[n], dl, nm, nv) for n, dl, nm, nv in
           zip(small_names, _unpack(sm_d, shapes), _unpack(sm_nm, shapes), _unpack(sm_nv, shapes))}

    big_names = ["w_in", "ssm_glu_w", "pool_w", "w_branch_ssm", "w_branch_pool", "w_out", "w_up", "w_down"]
    big_g = [gW_in, gW_glu.reshape(NDEV, DS // NDEV, DS),
             gW_pool.reshape(NPG, NDEV, PG // NDEV, PG).transpose(1, 0, 2, 3),
             gW_bs, gW_bp, gW_out.reshape(NDEV, U, D), gW_up, gW_down.reshape(NDEV, FC // 2, D)]
    received = _comm_exchange("comm_exchange_grads", big_g)
    for n, parts in zip(big_names, received):
        shape = args[n].shape
        cols = shape[-1]
        flat = lambda t: t.reshape(-1, cols)
        g, dl, nm, nv = _adamw("adamw_" + n, flat(args[n]), flat(args["m_" + n]), flat(args["v_" + n]),
                               parts.reshape(NDEV, -1, cols))
        res[n] = tuple(t.reshape(shape) for t in (g, dl, nm, nv))

    outs = [loss.reshape(()), grad_x.reshape(x.shape)]
    for k in range(4):
        outs += [res[n][k] for n in names]
    return tuple(outs)
```

```python
import functools
import math

import jax
import jax.numpy as jnp
from jax import lax
from jax.experimental import pallas as pl
from jax.experimental.pallas import tpu as pltpu

F32 = jnp.float32
BF16 = jnp.bfloat16
BS = pl.BlockSpec

NDEV = 8
SSM_GROUP = 16
SSM_STATE = 64
GROUPS_PER_CHUNK = 16
POOL_WINDOWS = (2, 4, 8, 16)
EPS = 1e-6
MIN_NEG_REAL = -1e-4
ADAM_LR, ADAM_B1, ADAM_B2, ADAM_EPS, ADAM_WD, ADAM_STEP = 0.001, 0.9, 0.999, 1e-08, 0.01, 10
LANES = 128
SUBLANES = 8
VMEM_LIMIT = 56 * 1024 * 1024

_DIMS = {"nn": (((1,), (0,)), ((), ())), "nt": (((1,), (1,)), ((), ())), "tn": (((0,), (0,)), ((), ()))}


def _tile(dim, pref, mult=LANES):
    if dim <= pref:
        return dim
    t = (pref // mult) * mult
    while t >= mult:
        if dim % t == 0:
            return t
        t -= mult
    return dim


def _pc(name, body, grid, ins, in_specs, outs, out_specs, scratch=(), deps=()):
    multi = isinstance(outs, (list, tuple))
    if deps:
        n_in, n_dep, inner = len(ins), len(deps), body

        def body(*refs):
            return inner(*refs[:n_in], *refs[n_in + n_dep:])

        ins = list(ins) + list(deps)
        in_specs = list(in_specs) + [BS(memory_space=pl.ANY)] * n_dep
    return pl.pallas_call(
        body, name=name, grid=grid, in_specs=list(in_specs),
        out_specs=list(out_specs) if multi else out_specs,
        out_shape=list(outs) if multi else outs, scratch_shapes=list(scratch),
        compiler_params=pltpu.CompilerParams(dimension_semantics=("arbitrary",) * len(grid),
                                             vmem_limit_bytes=VMEM_LIMIT),
    )(*ins)


def _sds(shape, dtype):
    return jax.ShapeDtypeStruct(tuple(shape), dtype)


def _gelu(x):
    k = math.sqrt(2.0 / math.pi)
    return 0.5 * x * (1.0 + jnp.tanh(k * (x + 0.044715 * (x * x * x))))


def _gelu_grad(x):
    k = math.sqrt(2.0 / math.pi)
    t = jnp.tanh(k * (x + 0.044715 * (x * x * x)))
    return 0.5 * (1.0 + t) + 0.5 * x * (1.0 - t * t) * (k * (1.0 + 3.0 * 0.044715 * x * x))


def _sigmoid(x):
    return jax.nn.sigmoid(x)


def _fused_matmul(name, grid, pairs, acc_shapes, extras, outs, epilogue, deps=()):
    n_p, n_e, n_o = len(pairs), len(extras), len(outs)
    rank = len(grid)
    nk = grid[-1]

    def body(*refs):
        ab = refs[:2 * n_p]
        ex = refs[2 * n_p:2 * n_p + n_e]
        o = refs[2 * n_p + n_e:2 * n_p + n_e + n_o]
        accs = refs[2 * n_p + n_e + n_o:]
        ids = [pl.program_id(d) for d in range(rank)]
        k = ids[-1]

        @pl.when(k == 0)
        def _():
            for acc in accs:
                acc[...] = jnp.zeros_like(acc)

        for p in range(n_p):
            a = ab[2 * p][...].astype(BF16)
            b = ab[2 * p + 1][...].astype(BF16)
            accs[pairs[p][5]][...] += lax.dot_general(a, b, _DIMS[pairs[p][4]], preferred_element_type=F32)

        @pl.when(k == nk - 1)
        def _():
            epilogue(ids, [acc[...] for acc in accs], ex, o)

    ins, in_specs = [], []
    for a, a_spec, b, b_spec, _, _ in pairs:
        ins += [a, b]
        in_specs += [a_spec, b_spec]
    for e, e_spec in extras:
        ins.append(e)
        in_specs.append(e_spec)
    res = _pc(name, body, grid, ins, in_specs, [s for s, _ in outs], [sp for _, sp in outs],
              scratch=[pltpu.VMEM(tuple(s), F32) for s in acc_shapes], deps=deps)
    return res


def _store(vals):
    def epilogue(ids, accs, ex, o):
        for r, v in zip(o, vals(accs)):
            r[...] = v.astype(r.dtype)
    return epilogue


def _rowsum_into(ref, first, v):
    s = jnp.sum(v, axis=0, keepdims=True)

    @pl.when(first)
    def _():
        ref[...] = s

    @pl.when(jnp.logical_not(first))
    def _():
        ref[...] += s


def _mesh_pos():
    return lax.axis_index("x"), lax.axis_index("y"), lax.axis_index("c")


def _slot(p):
    return 4 * p[0] + 2 * p[1] + p[2]


def _comm_gather(name, arrs):
    n = len(arrs)

    def body(*refs):
        ins, outs = refs[:n], refs[n:2 * n]
        send_sems, recv_sems, local_sems = refs[2 * n:]
        x, y, c = _mesh_pos()
        me, sib = (x, y, c), (x, y, 1 - c)
        chips = [(1 - x, y), (x, 1 - y), (1 - x, 1 - y)]

        def copy(r, k, block, to, src=None):
            dst = outs[r].at[_slot(block)]
            return pltpu.make_async_remote_copy(
                src_ref=dst if src is None else src, dst_ref=dst,
                send_sem=send_sems.at[r, k], recv_sem=recv_sems.at[r, k],
                device_id=to, device_id_type=pl.DeviceIdType.MESH)

        mine, first, passed = [], [], []
        for r in range(n):
            m = pltpu.make_async_copy(ins[r], outs[r].at[_slot(me)], local_sems.at[r])
            m.start()
            mine.append(m)
            f = [copy(r, 0, me, sib, src=ins[r])]
            f += [copy(r, 1 + j, me, (*chip, c), src=ins[r]) for j, chip in enumerate(chips)]
            for cp in f:
                cp.start()
            first.append(f)
        for r in range(n):
            ps = [copy(r, 4 + j, (*chip, c), sib) for j, chip in enumerate(chips)]
            for j, chip in enumerate(chips):
                copy(r, 1 + j, (*chip, c), me).wait_recv()
                ps[j].start()
            passed.append(ps)
        for r in range(n):
            copy(r, 0, sib, me).wait_recv()
            for j, chip in enumerate(chips):
                copy(r, 4 + j, (*chip, 1 - c), me).wait_recv()
            for cp in first[r] + passed[r]:
                cp.wait_send()
            mine[r].wait()

    any_spec = BS(memory_space=pl.ANY)
    return pl.pallas_call(
        body, name=name,
        in_specs=[any_spec] * n, out_specs=[any_spec] * n,
        out_shape=[_sds((NDEV,) + a.shape, a.dtype) for a in arrs],
        scratch_shapes=[pltpu.SemaphoreType.DMA((n, 7)), pltpu.SemaphoreType.DMA((n, 7)),
                        pltpu.SemaphoreType.DMA((n,))],
    )(*arrs)


_HBM = BS(memory_space=pltpu.HBM)
_SEM = BS(memory_space=pltpu.SEMAPHORE)
_ANY = BS(memory_space=pl.ANY)
_EFFECT = pltpu.SideEffectType.DATAFLOW_SIDE_EFFECTING


def _other_chips(x, y):
    return [(1 - x, y), (x, 1 - y), (1 - x, 1 - y)]


def _all_peers(x, y, c):
    peers = []
    for k in range(1, NDEV):
        kx, ky, kc = (k >> 2) & 1, (k >> 1) & 1, k & 1
        peers.append((1 - x if kx else x, 1 - y if ky else y, 1 - c if kc else c))
    return peers


def _gather_copies(src, land, send_sems, recv_sems, base):
    x, y, c = _mesh_pos()
    return [pltpu.make_async_remote_copy(
        src_ref=src, dst_ref=land.at[_slot((x, y, c))],
        send_sem=send_sems.at[base + k], recv_sem=recv_sems.at[base + k],
        device_id=(*chip, c), device_id_type=pl.DeviceIdType.MESH) for k, chip in enumerate(_other_chips(x, y))]


def _exchange_copies(src, land, send_sems, recv_sems, base):
    x, y, c = _mesh_pos()
    return [pltpu.make_async_remote_copy(
        src_ref=src.at[_slot(peer)], dst_ref=land.at[_slot((x, y, c))],
        send_sem=send_sems.at[base + k], recv_sem=recv_sems.at[base + k],
        device_id=peer, device_id_type=pl.DeviceIdType.MESH) for k, peer in enumerate(_all_peers(x, y, c))]


def _split_start(name, copies, ncopy, srcs, land_shapes, after):
    n = len(srcs)

    def body(*refs):
        src_refs, land_refs = refs[:n], refs[n:2 * n]
        send_sems, recv_sems = refs[2 * n + 1], refs[2 * n + 2]
        token = refs[-1]
        for r in range(n):
            for cp in copies(src_refs[r], land_refs[r], send_sems, recv_sems, r * ncopy):
                cp.start()
        token[...] = jnp.zeros_like(token)

    lands = [lax.empty(s.shape, s.dtype) for s in land_shapes]
    ins = [pltpu.with_memory_space_constraint(a, pltpu.HBM) for a in list(srcs) + lands]
    out_shape = ([pltpu.SemaphoreType.DMA((n * ncopy,)), pltpu.SemaphoreType.DMA((n * ncopy,))]
                 + [pltpu.HBM(a.shape, a.dtype) for a in list(srcs) + lands]
                 + [_sds((SUBLANES, LANES), F32)])
    res = pl.pallas_call(
        body, name=name, out_shape=out_shape,
        in_specs=[_HBM] * (2 * n) + [_ANY], out_specs=[_SEM, _SEM] + [_HBM] * (2 * n) + [BS(memory_space=pltpu.VMEM)],
        input_output_aliases={i: 2 + i for i in range(2 * n)},
        compiler_params=pltpu.CompilerParams(has_side_effects=_EFFECT),
    )(*ins, after)
    return res[0], res[1], list(res[2:2 + n]), list(res[2 + n:2 + 2 * n]), res[-1]


def _split_wait(name, copies, started, after):
    send_sems, recv_sems, srcs, lands, _ = started
    n = len(srcs)
    ncopy = send_sems.shape[0] // n

    def body(*refs):
        src_refs, land_refs = refs[:n], refs[n:2 * n]
        send_sems, recv_sems = refs[2 * n], refs[2 * n + 1]
        for r in range(n):
            for cp in copies(src_refs[r], land_refs[r], send_sems, recv_sems, r * ncopy):
                cp.wait_send()
                cp.wait_recv()

    res = pl.pallas_call(
        body, name=name, out_shape=[pltpu.HBM(a.shape, a.dtype) for a in srcs + lands],
        in_specs=[_HBM] * (2 * n) + [_SEM, _SEM, _ANY], out_specs=[_HBM] * (2 * n),
        input_output_aliases={i: i for i in range(2 * n)},
        compiler_params=pltpu.CompilerParams(has_side_effects=_EFFECT),
    )(*srcs, *lands, send_sems, recv_sems, after)
    return list(res[:n]), list(res[n:])


def _gather_d2d(name, shards, lands):
    n = len(shards)

    def body(*refs):
        shard_refs = refs[:n]
        land_refs = refs[2 * n:3 * n]
        send_sems, recv_sems, local_sems = refs[3 * n:]
        x, y, c = _mesh_pos()
        me, sib = (x, y, c), (x, y, 1 - c)
        chips = _other_chips(x, y)

        def copy(r, k, block, src=None):
            dst = land_refs[r].at[_slot(block)]
            return pltpu.make_async_remote_copy(
                src_ref=dst if src is None else src, dst_ref=dst,
                send_sem=send_sems.at[r, k], recv_sem=recv_sems.at[r, k],
                device_id=sib, device_id_type=pl.DeviceIdType.MESH)

        sent, mine = [], []
        for r in range(n):
            m = pltpu.make_async_copy(shard_refs[r], land_refs[r].at[_slot(me)], local_sems.at[r])
            m.start()
            mine.append(m)
            cps = [copy(r, 0, me, src=shard_refs[r])] + [copy(r, 1 + j, (*chip, c)) for j, chip in enumerate(chips)]
            for cp in cps:
                cp.start()
            sent += cps
        for r in range(n):
            copy(r, 0, sib).wait_recv()
            for j, chip in enumerate(chips):
                copy(r, 1 + j, (*chip, 1 - c)).wait_recv()
            mine[r].wait()
        for cp in sent:
            cp.wait_send()

    return pl.pallas_call(
        body, name=name,
        in_specs=[_ANY] * (2 * n), out_specs=[_ANY] * n,
        out_shape=[_sds(a.shape, a.dtype) for a in lands],
        input_output_aliases={n + i: i for i in range(n)},
        scratch_shapes=[pltpu.SemaphoreType.DMA((n, 4)), pltpu.SemaphoreType.DMA((n, 4)),
                        pltpu.SemaphoreType.DMA((n,))],
    )(*shards, *lands)


def _adamw(name, w, m, v, parts):
    R, C = w.shape
    S = parts.shape[0]
    tr = _tile(R, max(SUBLANES, (256 * 1024) // C), SUBLANES)

    def body(w_ref, m_ref, v_ref, p_ref, g_ref, d_ref, nm_ref, nv_ref):
        g = p_ref[0].astype(F32)
        for s in range(1, S):
            g = g + p_ref[s].astype(F32)
        m2 = ADAM_B1 * m_ref[...] + (1.0 - ADAM_B1) * g
        v2 = ADAM_B2 * v_ref[...] + (1.0 - ADAM_B2) * (g * g)
        m_hat = m2 / (1.0 - ADAM_B1 ** ADAM_STEP)
        v_hat = v2 / (1.0 - ADAM_B2 ** ADAM_STEP)
        g_ref[...] = g
        d_ref[...] = -ADAM_LR * (m_hat / (jnp.sqrt(v_hat) + ADAM_EPS) + ADAM_WD * w_ref[...])
        nm_ref[...] = m2
        nv_ref[...] = v2

    blk = BS((tr, C), lambda i: (i, 0))
    return _pc(name, body, (R // tr,), [w, m, v, parts],
               [blk, blk, blk, BS((S, tr, C), lambda i: (0, i, 0))],
               [_sds((R, C), F32)] * 4, [blk] * 4)


def _ssm_disc(lam_re, lam_im, log_step, br_t, bi_t):
    lr = jnp.minimum(lam_re, MIN_NEG_REAL)
    li = lam_im
    dt = jnp.exp(log_step)
    mag = jnp.exp(lr * dt)
    ang = li * dt
    ab_re = mag * jnp.cos(ang)
    ab_im = mag * jnp.sin(ang)
    nr = ab_re - 1.0
    ni = ab_im
    den = lr * lr + li * li
    f_re = (nr * lr + ni * li) / den
    f_im = (ni * lr - nr * li) / den
    bb_re = f_re[None] * br_t - f_im[None] * bi_t
    bb_im = f_re[None] * bi_t + f_im[None] * br_t
    return ab_re, ab_im, bb_re, bb_im


def _ssm_param_fwd(lam_re, lam_im, log_step, br_t, bi_t):
    G, P = lam_re.shape

    def body(lr_ref, li_ref, ls_ref, br_ref, bi_ref, pw_re_ref, pw_im_ref, bbr_ref, bbi_ref):
        ab_re, ab_im, bb_re, bb_im = _ssm_disc(lr_ref[...], li_ref[...], ls_ref[...], br_ref[...], bi_ref[...])
        bbr_ref[...] = bb_re
        bbi_ref[...] = bb_im
        pr, pi = ab_re, ab_im
        for r in range(SUBLANES):
            pw_re_ref[r] = pr
            pw_im_ref[r] = pi
            pr, pi = pr * ab_re - pi * ab_im, pr * ab_im + pi * ab_re

    full = lambda a: BS(a.shape, lambda i: (0,) * a.ndim)
    ins = [lam_re, lam_im, log_step, br_t, bi_t]
    outs = [_sds((SUBLANES, G, P), F32)] * 2 + [_sds(br_t.shape, F32)] * 2
    return _pc("ssm_param_fwd", body, (1,), ins, [full(a) for a in ins], outs, [full(o) for o in outs])


def _ssm_param_bwd(lam_re, lam_im, log_step, br_t, bi_t, d_ab_re, d_ab_im, d_bbr, d_bbi):
    def body(lr_ref, li_ref, ls_ref, br_ref, bi_ref, dar_ref, dai_ref, dbr_ref, dbi_ref,
             o_lr, o_li, o_ls, o_br, o_bi):
        prim = (lr_ref[...], li_ref[...], ls_ref[...], br_ref[...], bi_ref[...])
        _, vjp = jax.vjp(_ssm_disc, *prim)
        dar = dar_ref[0]
        dai = dai_ref[0]
        for k in range(1, dar_ref.shape[0]):
            dar = dar + dar_ref[k]
            dai = dai + dai_ref[k]
        g = vjp((dar, dai, dbr_ref[...], dbi_ref[...]))
        for r, v in zip((o_lr, o_li, o_ls, o_br, o_bi), g):
            r[...] = v

    full = lambda a: BS(a.shape, lambda i: (0,) * a.ndim)
    ins = [lam_re, lam_im, log_step, br_t, bi_t, d_ab_re, d_ab_im, d_bbr, d_bbi]
    outs = [_sds(a.shape, F32) for a in (lam_re, lam_im, log_step, br_t, bi_t)]
    return _pc("ssm_param_bwd", body, (1,), ins, [full(a) for a in ins], outs, [full(o) for o in outs])


def _bcast_row(ref, r, w):
    return jnp.broadcast_to(ref[pl.ds(r, 1), :], (SUBLANES, w))


def _pick_row(x, row, r):
    return jnp.broadcast_to(jnp.sum(jnp.where(row == r, x, 0.0), axis=0, keepdims=True), x.shape)


def _scan_fwd(bu_re, bu_im, pw_re, pw_im, nseq, L):
    N, SL = bu_re.shape
    W = _tile(SL, 256)

    def body(bre_ref, bim_ref, pre_ref, pim_ref, sre_ref, sim_ref):
        pre, pim = pre_ref[...], pim_ref[...]
        steps = [(k, _bcast_row(pre_ref, k - 1, W), _bcast_row(pim_ref, k - 1, W)) for k in (1, 2, 4)]
        row = lax.broadcasted_iota(jnp.int32, (SUBLANES, W), 0)

        def step(i, carry):
            cr, ci = carry
            r0 = pl.multiple_of(i * SUBLANES, SUBLANES)
            xr = bre_ref[pl.ds(r0, SUBLANES), :]
            xi = bim_ref[pl.ds(r0, SUBLANES), :]
            for k, ar, ai in steps:
                sr = pltpu.roll(xr, k, axis=0)
                si = pltpu.roll(xi, k, axis=0)
                keep = row >= k
                xr, xi = (xr + jnp.where(keep, ar * sr - ai * si, 0.0),
                          xi + jnp.where(keep, ar * si + ai * sr, 0.0))
            xr, xi = xr + (pre * cr - pim * ci), xi + (pre * ci + pim * cr)
            sre_ref[pl.ds(r0, SUBLANES), :] = xr
            sim_ref[pl.ds(r0, SUBLANES), :] = xi
            return _pick_row(xr, row, SUBLANES - 1), _pick_row(xi, row, SUBLANES - 1)

        zero = jnp.zeros((SUBLANES, W), F32)
        lax.fori_loop(0, L // SUBLANES, step, (zero, zero))

    blk = BS((L, W), lambda s, j: (s, j))
    pw = BS((SUBLANES, W), lambda s, j: (0, j))
    return _pc("ssm_scan_fwd", body, (nseq, SL // W), [bu_re, bu_im, pw_re, pw_im], [blk, blk, pw, pw],
               [_sds((N, SL), F32)] * 2, [blk, blk])


def _scan_bwd(ds_re, ds_im, s_re, s_im, pw_re, pw_im, pwf_re, pwf_im, nseq, L):
    N, SL = ds_re.shape
    W = _tile(SL, 256)
    nt = L // SUBLANES

    def body(dsr_ref, dsi_ref, sre_ref, sim_ref, pre_ref, pim_ref, fre_ref, fim_ref,
             lre_ref, lim_ref, dar_ref, dai_ref):
        fre, fim = fre_ref[...], -fim_ref[...]
        steps = [(k, _bcast_row(pre_ref, k - 1, W), -_bcast_row(pim_ref, k - 1, W)) for k in (1, 2, 4)]
        row = lax.broadcasted_iota(jnp.int32, (SUBLANES, W), 0)

        def step(ii, carry):
            cr, ci, acr, aci = carry
            i = nt - 1 - ii
            r0 = pl.multiple_of(i * SUBLANES, SUBLANES)
            xr = dsr_ref[pl.ds(r0, SUBLANES), :]
            xi = dsi_ref[pl.ds(r0, SUBLANES), :]
            for k, ar, ai in steps:
                sr = pltpu.roll(xr, SUBLANES - k, axis=0)
                si = pltpu.roll(xi, SUBLANES - k, axis=0)
                keep = row < SUBLANES - k
                xr, xi = (xr + jnp.where(keep, ar * sr - ai * si, 0.0),
                          xi + jnp.where(keep, ar * si + ai * sr, 0.0))
            xr, xi = xr + (fre * cr - fim * ci), xi + (fre * ci + fim * cr)
            lre_ref[pl.ds(r0, SUBLANES), :] = xr
            lim_ref[pl.ds(r0, SUBLANES), :] = xi
            p0 = pl.multiple_of(jnp.maximum(i - 1, 0) * SUBLANES, SUBLANES)
            has_prev = i > 0
            spr = jnp.where(row == 0,
                            jnp.where(has_prev, pltpu.roll(sre_ref[pl.ds(p0, SUBLANES), :], 1, axis=0), 0.0),
                            pltpu.roll(sre_ref[pl.ds(r0, SUBLANES), :], 1, axis=0))
            spi = jnp.where(row == 0,
                            jnp.where(has_prev, pltpu.roll(sim_ref[pl.ds(p0, SUBLANES), :], 1, axis=0), 0.0),
                            pltpu.roll(sim_ref[pl.ds(r0, SUBLANES), :], 1, axis=0))
            acr = acr + (xr * spr + xi * spi)
            aci = aci + (xi * spr - xr * spi)
            return _pick_row(xr, row, 0), _pick_row(xi, row, 0), acr, aci

        zero = jnp.zeros((SUBLANES, W), F32)
        _, _, acr, aci = lax.fori_loop(0, nt, step, (zero, zero, zero, zero))
        dar_ref[...] = acr
        dai_ref[...] = aci

    blk = BS((L, W), lambda s, j: (s, j))
    pw = BS((SUBLANES, W), lambda s, j: (0, j))
    da = BS((None, SUBLANES, W), lambda s, j: (s, 0, j))
    return _pc("ssm_scan_bwd", body, (nseq, SL // W),
               [ds_re, ds_im, s_re, s_im, pw_re, pw_im, pwf_re, pwf_im], [blk] * 4 + [pw] * 4,
               [_sds((N, SL), F32)] * 2 + [_sds((nseq, SUBLANES, SL), F32)] * 2, [blk, blk, da, da])


def _pool_select(g, vals):
    return jnp.where(g == 0, vals[0], jnp.where(g == 1, vals[1], jnp.where(g == 2, vals[2], vals[3])))


def _pool_fwd(proj, col0, DP, nseq, L):
    N = proj.shape[0]
    PG = DP // len(POOL_WINDOWS)
    W = _tile(PG, 256)

    def body(v_ref, z_ref):
        g = pl.program_id(1) // (PG // W)
        v = v_ref[...]
        row = lax.broadcasted_iota(jnp.int32, (L, W), 0)
        sums, s, k = [], v, 1
        for _ in POOL_WINDOWS:
            s = s + jnp.where(row >= k, pltpu.roll(s, k, axis=0), 0.0)
            sums.append(s)
            k *= 2
        win = _pool_select(g, [float(w) for w in POOL_WINDOWS])
        cnt = jnp.minimum((row + 1).astype(F32), win)
        z_ref[...] = (_pool_select(g, sums) / cnt - v).astype(z_ref.dtype)

    return _pc("pool_fwd", body, (nseq, DP // W), [proj], [BS((L, W), lambda s, j: (s, col0 // W + j))],
               _sds((N, DP), BF16), BS((L, W), lambda s, j: (s, j)))


def _pool_bwd(dz, nseq, L):
    N, DP = dz.shape
    PG = DP // len(POOL_WINDOWS)
    W = _tile(PG, 256)

    def body(dz_ref, dv_ref):
        g = pl.program_id(1) // (PG // W)
        d = dz_ref[...]
        row = lax.broadcasted_iota(jnp.int32, (L, W), 0)
        win = _pool_select(g, [float(w) for w in POOL_WINDOWS])
        s = d / jnp.minimum((row + 1).astype(F32), win)
        sums, k = [], 1
        for _ in POOL_WINDOWS:
            s = s + jnp.where(row < L - k, pltpu.roll(s, L - k, axis=0), 0.0)
            sums.append(s)
            k *= 2
        dv_ref[...] = (_pool_select(g, sums) - d).astype(dv_ref.dtype)

    blk = BS((L, W), lambda s, j: (s, j))
    return _pc("pool_bwd", body, (nseq, DP // W), [dz], [blk], _sds((N, DP), BF16), blk)


def _rstd(x):
    return lax.rsqrt(jnp.mean(x * x, axis=-1, keepdims=True) + EPS)


def _norm_bwd(dy, xhat, rstd, gain):
    t = dy * gain
    return rstd * (t - xhat * jnp.mean(t * xhat, axis=-1, keepdims=True))


def _pre_norm(x, g1):
    N, D = x.shape
    tr = _tile(N, 128, SUBLANES)

    def body(x_ref, g_ref, a_ref):
        xv = x_ref[...]
        a_ref[...] = (xv * _rstd(xv) * g_ref[...]).astype(a_ref.dtype)

    row = BS((tr, D), lambda i: (i, 0))
    vec = BS((1, D), lambda i: (0, 0))
    return _pc("pre_norm", body, (N // tr,), [x, g1], [row, vec], _sds((N, D), BF16), row)


def _mid_norm(x, o, g2, g3):
    N, D = x.shape
    tr = _tile(N, 128, SUBLANES)

    def body(x_ref, o_ref, g2_ref, g3_ref, h1_ref, c_ref):
        ov = o_ref[...]
        h1 = x_ref[...] + ov * _rstd(ov) * g2_ref[...]
        h1_ref[...] = h1
        c_ref[...] = (h1 * _rstd(h1) * g3_ref[...]).astype(c_ref.dtype)

    row = BS((tr, D), lambda i: (i, 0))
    vec = BS((1, D), lambda i: (0, 0))
    return _pc("mid_norm", body, (N // tr,), [x, o, g2, g3], [row, row, vec, vec],
               [_sds((N, D), F32), _sds((N, D), BF16)], [row, row])


def _post_ffn(h1, dn, tgt, g4):
    N, D = h1.shape
    tr = _tile(N, 128, SUBLANES)

    def body(h1_ref, dn_ref, t_ref, g_ref, dh2_ref, ddn_ref, lossv_ref, dg_ref):
        first = pl.program_id(0) == 0
        dnv = dn_ref[...]
        rstd = _rstd(dnv)
        xhat = dnv * rstd
        gain = g_ref[...]
        err = (h1_ref[...] + xhat * gain) - t_ref[...]
        dh2 = err / float(D)
        dh2_ref[...] = dh2
        ddn_ref[...] = _norm_bwd(dh2, xhat, rstd, gain).astype(ddn_ref.dtype)
        _rowsum_into(lossv_ref, first, err * err)
        _rowsum_into(dg_ref, first, dh2 * xhat)

    row = BS((tr, D), lambda i: (i, 0))
    vec = BS((1, D), lambda i: (0, 0))
    return _pc("post_ffn", body, (N // tr,), [h1, dn, tgt, g4], [row, row, row, vec],
               [_sds((N, D), F32), _sds((N, D), BF16), _sds((1, D), F32), _sds((1, D), F32)], [row, row, vec, vec])


def _mid_bwd(dh2, dc, h1, o, g2, g3, deps=()):
    N, D = h1.shape
    tr = _tile(N, 128, SUBLANES)

    def body(dh2_ref, dc_ref, h1_ref, o_ref, g2_ref, g3_ref, dh1_ref, do_ref, dg2_ref, dg3_ref):
        first = pl.program_id(0) == 0
        h1 = h1_ref[...]
        r3 = _rstd(h1)
        hc = h1 * r3
        dcv = dc_ref[...]
        dh1 = dh2_ref[...] + _norm_bwd(dcv, hc, r3, g3_ref[...])
        dh1_ref[...] = dh1
        ov = o_ref[...]
        r2 = _rstd(ov)
        ho = ov * r2
        do_ref[...] = _norm_bwd(dh1, ho, r2, g2_ref[...]).astype(do_ref.dtype)
        _rowsum_into(dg3_ref, first, dcv * hc)
        _rowsum_into(dg2_ref, first, dh1 * ho)

    row = BS((tr, D), lambda i: (i, 0))
    vec = BS((1, D), lambda i: (0, 0))
    return _pc("mid_bwd", body, (N // tr,), [dh2, dc, h1, o, g2, g3], [row] * 4 + [vec, vec],
               [_sds((N, D), F32), _sds((N, D), BF16), _sds((1, D), F32), _sds((1, D), F32)], [row, row, vec, vec],
               deps=deps)


def _pre_bwd(x, da, dh1, g1):
    N, D = x.shape
    tr = _tile(N, 128, SUBLANES)

    def body(x_ref, da_ref, dh1_ref, g_ref, dx_ref, dg_ref):
        first = pl.program_id(0) == 0
        xv = x_ref[...]
        r1 = _rstd(xv)
        xh = xv * r1
        dav = da_ref[...]
        dx_ref[...] = dh1_ref[...] + _norm_bwd(dav, xh, r1, g_ref[...])
        _rowsum_into(dg_ref, first, dav * xh)

    row = BS((tr, D), lambda i: (i, 0))
    vec = BS((1, D), lambda i: (0, 0))
    return _pc("pre_bwd", body, (N // tr,), [x, da, dh1, g1], [row, row, row, vec],
               [_sds((N, D), F32), _sds((1, D), F32)], [row, vec])


def _conv_rows(x_ref, halo_ref, first):
    x = x_ref[...]
    tr = x.shape[0]
    xx = jnp.concatenate([jnp.where(first, 0.0, halo_ref[...]), x], axis=0)
    x1 = pltpu.roll(xx, 1, axis=0)[SUBLANES:]
    x2 = pltpu.roll(xx, 2, axis=0)[SUBLANES:]
    del tr
    return x, x1, x2


def _conv_apply(rows, w_ref, b_ref):
    x, x1, x2 = rows
    return ((b_ref[...] + x2 * w_ref[pl.ds(0, 1), :]) + x1 * w_ref[pl.ds(1, 1), :]) + x * w_ref[pl.ds(2, 1), :]


def _gate_specs(N, FC, TR, half):
    tile = BS((None, TR, FC), lambda jj, i: (jj + half, i, 0))
    halo = BS((None, SUBLANES, FC), lambda jj, i: (jj + half, jnp.maximum(i * (TR // SUBLANES) - 1, 0), 0))
    cw = BS((None, 3, FC), lambda jj, i: (jj + half, 0, 0))
    cb = BS((None, 1, FC), lambda jj, i: (jj + half, 0, 0))
    return tile, halo, cw, cb


def _gate_fwd(up_pre, cw, cb, L):
    nb, N, FC = up_pre.shape
    half = nb // 2
    TR = _tile(L, 128, SUBLANES)

    def body(xa_ref, ha_ref, wa_ref, ba_ref, xb_ref, hb_ref, wb_ref, bb_ref, f_ref):
        first = (pl.program_id(1) % (L // TR)) == 0
        ua = _conv_apply(_conv_rows(xa_ref, ha_ref, first), wa_ref, ba_ref)
        ub = _conv_apply(_conv_rows(xb_ref, hb_ref, first), wb_ref, bb_ref)
        f_ref[...] = (_gelu(ua) * ub).astype(f_ref.dtype)

    sa, sb = _gate_specs(N, FC, TR, 0), _gate_specs(N, FC, TR, half)
    return _pc("gate_fwd", body, (half, N // TR), [up_pre, up_pre, cw, cb] * 2, list(sa) + list(sb),
               _sds((half, N, FC), BF16), BS((None, TR, FC), lambda jj, i: (jj, i, 0)))


def _gate_bwd(up_pre, cw, cb, df, L, deps=()):
    nb, N, FC = up_pre.shape
    half = nb // 2
    TR = _tile(L, 128, SUBLANES)

    def body(xa_ref, ha_ref, wa_ref, ba_ref, xb_ref, hb_ref, wb_ref, bb_ref, df_ref,
             da_ref, db_ref, dwa_ref, dwb_ref, dba_ref, dbb_ref):
        i = pl.program_id(1)
        first_row = i == 0
        first = (i % (L // TR)) == 0
        ra = _conv_rows(xa_ref, ha_ref, first)
        rb = _conv_rows(xb_ref, hb_ref, first)
        ua = _conv_apply(ra, wa_ref, ba_ref)
        ub = _conv_apply(rb, wb_ref, bb_ref)
        dfv = df_ref[...].astype(F32)
        dua = dfv * ub * _gelu_grad(ua)
        dub = dfv * _gelu(ua)
        da_ref[...] = dua
        db_ref[...] = dub
        for rows, du, dw_ref, dbias_ref in ((ra, dua, dwa_ref, dba_ref), (rb, dub, dwb_ref, dbb_ref)):
            x, x1, x2 = rows
            _rowsum_into(dbias_ref, first_row, du)
            for k, xs in enumerate((x2, x1, x)):
                _rowsum_into(dw_ref.at[pl.ds(k, 1), :], first_row, du * xs)

    sa, sb = _gate_specs(N, FC, TR, 0), _gate_specs(N, FC, TR, half)
    tile = BS((None, TR, FC), lambda jj, i: (jj, i, 0))
    dw = BS((None, 3, FC), lambda jj, i: (jj, 0, 0))
    dbias = BS((None, 1, FC), lambda jj, i: (jj, 0, 0))
    return _pc("gate_bwd", body, (half, N // TR), [up_pre, up_pre, cw, cb] * 2 + [df], list(sa) + list(sb) + [tile],
               [_sds((half, N, FC), F32)] * 2 + [_sds((half, 3, FC), F32)] * 2 + [_sds((half, 1, FC), F32)] * 2,
               [tile, tile, dw, dw, dbias, dbias], deps=deps)


def _conv_bwd(dup, cw, L, half):
    nb, N, FC = dup.shape
    TR = _tile(L, 128, SUBLANES)
    nrb = N // SUBLANES

    def body(x_ref, h_ref, w_ref, o_ref):
        last = ((pl.program_id(1) + 1) % (L // TR)) == 0
        x = x_ref[...]
        xx = jnp.concatenate([x, jnp.where(last, 0.0, h_ref[...])], axis=0)
        x1 = pltpu.roll(xx, TR + SUBLANES - 1, axis=0)[:TR]
        x2 = pltpu.roll(xx, TR + SUBLANES - 2, axis=0)[:TR]
        o_ref[...] = (x * w_ref[pl.ds(2, 1), :] + x1 * w_ref[pl.ds(1, 1), :] + x2 * w_ref[pl.ds(0, 1), :]
                      ).astype(o_ref.dtype)

    tile = BS((None, TR, FC), lambda jj, i: (jj, i, 0))
    halo = BS((None, SUBLANES, FC), lambda jj, i: (jj, jnp.minimum((i + 1) * (TR // SUBLANES), nrb - 1), 0))
    w = BS((None, 3, FC), lambda jj, i: (jj + half, 0, 0))
    return _pc("conv_bwd_%d" % half, body, (nb, N // TR), [dup, dup, cw], [tile, halo, w],
               _sds((nb, N, FC), BF16), tile)


def _pack(arrs):
    parts = []
    for a in arrs:
        flat = a.reshape(-1).astype(F32)
        pad = (-flat.shape[0]) % (SUBLANES * LANES)
        parts.append(jnp.pad(flat, (0, pad)))
    return jnp.concatenate(parts).reshape(-1, LANES)


def _unpack(packed, shapes):
    flat = packed.reshape(-1)
    out, off = [], 0
    for s in shapes:
        n = math.prod(s)
        out.append(flat[off:off + n].reshape(s))
        off += n + ((-n) % (SUBLANES * LANES))
    return out


def _small_sum(gathered, loss_rows, d_model):
    S, R, C = gathered.shape

    def body(p_ref, tot_ref, loss_ref):
        t = p_ref[0]
        for s in range(1, S):
            t = t + p_ref[s]
        tot_ref[...] = t
        loss_ref[...] = jnp.full((1, 1), 0.5 / d_model, F32) * jnp.sum(t[:loss_rows])

    return _pc("small_sum", body, (1,), [gathered], [BS((S, R, C), lambda i: (0, 0, 0))],
               [_sds((R, C), F32), _sds((1, 1), F32)], [BS((R, C), lambda i: (0, 0)), BS((1, 1), lambda i: (0, 0))])


def _block_diag_in(bb_t, nch):
    J, G, P = bb_t.shape
    gl = G // nch
    b = bb_t.reshape(J, nch, gl, P).transpose(1, 0, 2, 3)
    eye = jnp.eye(gl, dtype=F32)
    w = eye[None, :, None, :, None] * b[:, None, :, :, :]
    return w.reshape(nch, gl * J, gl * P)


def _block_diag_in_grad(dw, J, G, P):
    nch = dw.shape[0]
    gl = G // nch
    d = dw.reshape(nch, gl, J, gl, P)
    d = jnp.einsum("cgjgp->jcgp", d)
    return d.reshape(J, G, P)


def _block_diag_out(c, nch):
    G, J, P = c.shape
    gl = G // nch
    cc = c.reshape(nch, gl, J, P).transpose(0, 1, 3, 2)
    eye = jnp.eye(gl, dtype=F32)
    w = cc[:, :, :, None, :] * eye[None, :, None, :, None]
    return w.reshape(nch, gl * P, gl * J)


def _block_diag_out_grad(dw, G, J, P):
    nch = dw.shape[0]
    gl = G // nch
    d = dw.reshape(nch, gl, P, gl, J)
    d = jnp.einsum("cgpgj->cgjp", d)
    return d.reshape(G, J, P)


def kernel(x, norm_pre_mix, w_in, ssm_lambda_re, ssm_lambda_im, ssm_log_step, ssm_b_re, ssm_b_im, ssm_c_re, ssm_c_im, ssm_d, ssm_glu_w, ssm_glu_b, pool_w, pool_b, pool_scale, w_branch_ssm, w_branch_pool, w_out, norm_post_mix, norm_pre_ffn, w_up, ffn_conv_w, ffn_conv_b, w_down, norm_post_ffn, loss_target, m_norm_pre_mix, m_w_in, m_ssm_lambda_re, m_ssm_lambda_im, m_ssm_log_step, m_ssm_b_re, m_ssm_b_im, m_ssm_c_re, m_ssm_c_im, m_ssm_d, m_ssm_glu_w, m_ssm_glu_b, m_pool_w, m_pool_b, m_pool_scale, m_w_branch_ssm, m_w_branch_pool, m_w_out, m_norm_post_mix, m_norm_pre_ffn, m_w_up, m_ffn_conv_w, m_ffn_conv_b, m_w_down, m_norm_post_ffn, v_norm_pre_mix, v_w_in, v_ssm_lambda_re, v_ssm_lambda_im, v_ssm_log_step, v_ssm_b_re, v_ssm_b_im, v_ssm_c_re, v_ssm_c_im, v_ssm_d, v_ssm_glu_w, v_ssm_glu_b, v_pool_w, v_pool_b, v_pool_scale, v_w_branch_ssm, v_w_branch_pool, v_w_out, v_norm_post_mix, v_norm_pre_ffn, v_w_up, v_ffn_conv_w, v_ffn_conv_b, v_w_down, v_norm_post_ffn):
    args = dict(locals())
    names = ["norm_pre_mix", "w_in", "ssm_lambda_re", "ssm_lambda_im", "ssm_log_step", "ssm_b_re", "ssm_b_im",
             "ssm_c_re", "ssm_c_im", "ssm_d", "ssm_glu_w", "ssm_glu_b", "pool_w", "pool_b", "pool_scale",
             "w_branch_ssm", "w_branch_pool", "w_out", "norm_post_mix", "norm_pre_ffn", "w_up", "ffn_conv_w",
             "ffn_conv_b", "w_down", "norm_post_ffn"]

    nseq, L, D = x.shape
    N = nseq * L
    U = D // NDEV
    DS = ssm_d.shape[1]
    DP = pool_scale.shape[1]
    G, P, J = ssm_b_re.shape[1:]
    SL = G * P
    CH = GROUPS_PER_CHUNK * J
    CS = GROUPS_PER_CHUNK * P
    NCH = DS // CH
    NPG = len(POOL_WINDOWS)
    PG = DP // NPG
    FC = w_up.shape[2]
    NB = NDEV
    HB = NB // 2
    F2 = NB * FC
    dev = _slot(_mesh_pos())
    tm = _tile(N, 1024)
    tm2 = _tile(N, 512)

    x2 = x.reshape(N, D)
    tgt = loss_target.reshape(N, D)

    def bf(t):
        return t.astype(BF16)

    def g_start(tag, group, after):
        return _split_start("gather_start_" + tag, _gather_copies, 3, group,
                            [_sds((NDEV,) + s.shape, s.dtype) for s in group], after)

    def g_finish(tag, started, after):
        srcs, lands = _split_wait("gather_wait_" + tag, _gather_copies, started, after)
        return _gather_d2d("comm_gather_d2d_" + tag, srcs, lands)

    def x_start(tag, group):
        return _split_start("exchange_start_" + tag, _exchange_copies, NDEV - 1, group,
                            [_sds(g.shape, g.dtype) for g in group], group[0])

    def x_finish(tag, started, after):
        srcs, lands = _split_wait("exchange_wait_" + tag, _exchange_copies, started, after)
        own = [lax.dynamic_index_in_dim(s, dev, 0, keepdims=False) for s in srcs]
        return [lax.dynamic_update_index_in_dim(l, o, dev, 0) for l, o in zip(lands, own)]

    st_in = g_start("in", [bf(w_in[0])], x2)
    st_mix = g_start("mix", [bf(ssm_glu_w[0]), bf(pool_w[0]), pool_b[0], ffn_conv_w[0]], st_in[4])
    (Win,) = g_finish("in", st_in, st_mix[4])
    st_br = g_start("branch", [bf(w_branch_ssm[0]), bf(w_branch_pool[0]), bf(w_out[0])], Win)
    conv_b_blk = ffn_conv_b.reshape(NB, 1, FC)

    lam_re, lam_im = ssm_lambda_re[0], ssm_lambda_im[0]
    log_step = ssm_log_step.reshape(G, 1)
    br_t = ssm_b_re[0].transpose(2, 0, 1)
    bi_t = ssm_b_im[0].transpose(2, 0, 1)
    pw_re3, pw_im3, bb_re, bb_im = _ssm_param_fwd(lam_re, lam_im, log_step, br_t, bi_t)
    pw_re, pw_im = pw_re3.reshape(SUBLANES, SL), pw_im3.reshape(SUBLANES, SL)
    pwf_re, pwf_im = pw_re[::-1], pw_im[::-1]
    WB = jnp.concatenate([_block_diag_in(bb_re, NCH), _block_diag_in(bb_im, NCH)], axis=2).astype(BF16)
    WCre = _block_diag_out(ssm_c_re[0], NCH).astype(BF16)
    WCim = _block_diag_out(-ssm_c_im[0], NCH).astype(BF16)

    a = _pre_norm(x2, norm_pre_mix)
    nq = 3 * NDEV
    (proj,) = _fused_matmul(
        "in_proj", (N // tm, nq, 1),
        [(a, BS((tm, D), lambda i, q, k: (i, 0)), Win, BS((None, D, U), lambda i, q, k: (q // 3, 0, q % 3)), "nn", 0)],
        [(tm, U)], [], [(_sds((N, 3 * D), F32), BS((tm, U), lambda i, q, k: (i, q)))],
        _store(lambda accs: accs), deps=[st_br[4]])

    bu_re, bu_im = _fused_matmul(
        "ssm_in", (N // tm2, NCH, 1),
        [(proj, BS((tm2, CH), lambda i, c, k: (i, c)), WB, BS((None, CH, 2 * CS), lambda i, c, k: (c, 0, 0)), "nn", 0)],
        [(tm2, 2 * CS)], [],
        [(_sds((N, SL), F32), BS((tm2, CS), lambda i, c, k: (i, c)))] * 2,
        _store(lambda accs: (accs[0][:, :CS], accs[0][:, CS:])))
    s_re, s_im = _scan_fwd(bu_re, bu_im, pw_re, pw_im, nseq, L)

    def ssm_out_epi(ids, accs, ex, o):
        u_ref, d_ref = ex
        y0 = accs[0] + d_ref[...] * u_ref[...]
        o[0][...] = y0
        o[1][...] = _gelu(y0).astype(BF16)

    y0, y1 = _fused_matmul(
        "ssm_out", (N // tm2, NCH, 1),
        [(s_re, BS((tm2, CS), lambda i, c, k: (i, c)), WCre, BS((None, CS, CH), lambda i, c, k: (c, 0, 0)), "nn", 0),
         (s_im, BS((tm2, CS), lambda i, c, k: (i, c)), WCim, BS((None, CS, CH), lambda i, c, k: (c, 0, 0)), "nn", 0)],
        [(tm2, CH)],
        [(proj, BS((tm2, CH), lambda i, c, k: (i, c))), (ssm_d, BS((1, CH), lambda i, c, k: (0, c)))],
        [(_sds((N, DS), F32), BS((tm2, CH), lambda i, c, k: (i, c))),
         (_sds((N, DS), BF16), BS((tm2, CH), lambda i, c, k: (i, c)))],
        ssm_out_epi)

    Wglu, Wpool, pool_b_all, conv_w_all = g_finish("mix", st_mix, y1)
    st_up = g_start("up", [bf(w_up[0])], Wglu)
    Wglu = Wglu.reshape(DS, DS)
    Wpool = Wpool.transpose(1, 0, 2, 3).reshape(NPG, PG, PG)
    pool_b_full = pool_b_all.transpose(1, 0, 2).reshape(1, DP)
    tn_s = _tile(DS, 512)

    def glu_epi(ids, accs, ex, o):
        y0_ref, b_ref = ex
        zg = accs[0] + b_ref[...]
        o[0][...] = zg
        o[1][...] = (_gelu(y0_ref[...]) * _sigmoid(zg)).astype(BF16)

    zg, ys = _fused_matmul(
        "ssm_glu", (N // tm, DS // tn_s, 1),
        [(y1, BS((tm, DS), lambda i, j, k: (i, 0)), Wglu, BS((DS, tn_s), lambda i, j, k: (0, j)), "nn", 0)],
        [(tm, tn_s)],
        [(y0, BS((tm, tn_s), lambda i, j, k: (i, j))), (ssm_glu_b, BS((1, tn_s), lambda i, j, k: (0, j)))],
        [(_sds((N, DS), F32), BS((tm, tn_s), lambda i, j, k: (i, j))),
         (_sds((N, DS), BF16), BS((tm, tn_s), lambda i, j, k: (i, j)))],
        glu_epi, deps=[st_up[4]])

    z = _pool_fwd(proj, DS, DP, nseq, L)

    def pool_mm_epi(ids, accs, ex, o):
        b_ref, sc_ref = ex
        q = accs[0] + b_ref[...]
        o[0][...] = q
        o[1][...] = (q * sc_ref[...]).astype(BF16)

    qp, yp = _fused_matmul(
        "pool_mm", (N // tm, NPG, 1),
        [(z, BS((tm, PG), lambda i, g, k: (i, g)), Wpool, BS((None, PG, PG), lambda i, g, k: (g, 0, 0)), "nn", 0)],
        [(tm, PG)],
        [(pool_b_full, BS((1, PG), lambda i, g, k: (0, g))), (pool_scale, BS((1, PG), lambda i, g, k: (0, g)))],
        [(_sds((N, DP), F32), BS((tm, PG), lambda i, g, k: (i, g))),
         (_sds((N, DP), BF16), BS((tm, PG), lambda i, g, k: (i, g)))],
        pool_mm_epi)

    Wbs, Wbp, Wout = g_finish("branch", st_br, yp)
    st_down = g_start("down", [bf(w_down[0])], Wbs)
    Wout = Wout.reshape(D, D)
    gs_blk = BS((tm2, U), lambda i, q, k: (i, (DS + DP) // U + q))
    gp_blk = BS((tm2, U), lambda i, q, k: (i, (DS + DP + D) // U + q))
    out_blk = BS((tm2, U), lambda i, q, k: (i, q))

    def branch_epi(ids, accs, ex, o):
        gs_ref, gp_ref = ex
        o[0][...] = accs[0]
        o[1][...] = accs[1]
        o[2][...] = (_sigmoid(gs_ref[...]) * accs[0] + _sigmoid(gp_ref[...]) * accs[1]).astype(BF16)

    Ys, Yp, merged = _fused_matmul(
        "branch", (N // tm2, NDEV, 1),
        [(ys, BS((tm2, DS), lambda i, q, k: (i, 0)), Wbs, BS((None, DS, U), lambda i, q, k: (q, 0, 0)), "nn", 0),
         (yp, BS((tm2, DP), lambda i, q, k: (i, 0)), Wbp, BS((None, DP, U), lambda i, q, k: (q, 0, 0)), "nn", 1)],
        [(tm2, U), (tm2, U)],
        [(proj, gs_blk), (proj, gp_blk)],
        [(_sds((N, D), F32), out_blk), (_sds((N, D), F32), out_blk), (_sds((N, D), BF16), out_blk)],
        branch_epi, deps=[st_down[4]])

    tn_d = _tile(D, 512)
    (o_mix,) = _fused_matmul(
        "out_proj", (N // tm, D // tn_d, 1),
        [(merged, BS((tm, D), lambda i, j, k: (i, 0)), Wout, BS((D, tn_d), lambda i, j, k: (0, j)), "nn", 0)],
        [(tm, tn_d)], [], [(_sds((N, D), F32), BS((tm, tn_d), lambda i, j, k: (i, j)))],
        _store(lambda accs: accs))
    h1, c = _mid_norm(x2, o_mix, norm_post_mix, norm_pre_ffn)

    (Wup,) = g_finish("up", st_up, c)
    tk_d = _tile(D, 1024)
    (up_pre,) = _fused_matmul(
        "ffn_up", (N // tm2, NB, D // tk_d),
        [(c, BS((tm2, tk_d), lambda i, j, k: (i, k)), Wup, BS((None, tk_d, FC), lambda i, j, k: (j, k, 0)), "nn", 0)],
        [(tm2, FC)], [], [(_sds((NB, N, FC), F32), BS((None, tm2, FC), lambda i, j, k: (j, i, 0)))],
        _store(lambda accs: accs))
    f = _gate_fwd(up_pre, conv_w_all, conv_b_blk, L)
    (Wdown,) = g_finish("down", st_down, f)
    Wdown = Wdown.reshape(HB, FC, D)
    tn_d2 = _tile(D, 1024)
    (dn,) = _fused_matmul(
        "ffn_down", (N // tm2, D // tn_d2, HB),
        [(f, BS((None, tm2, FC), lambda i, j, k: (k, i, 0)), Wdown, BS((None, FC, tn_d2), lambda i, j, k: (k, 0, j)), "nn", 0)],
        [(tm2, tn_d2)], [], [(_sds((N, D), F32), BS((tm2, tn_d2), lambda i, j, k: (i, j)))],
        _store(lambda accs: accs))
    dh2, d_dn, lossv, dg4 = _post_ffn(h1, dn, tgt, norm_post_ffn)

    (df,) = _fused_matmul(
        "ffn_down_dx", (N // tm2, HB, D // tk_d),
        [(d_dn, BS((tm2, tk_d), lambda i, j, k: (i, k)), Wdown, BS((None, FC, tk_d), lambda i, j, k: (j, 0, k)), "nt", 0)],
        [(tm2, FC)], [], [(_sds((HB, N, FC), BF16), BS((None, tm2, FC), lambda i, j, k: (j, i, 0)))],
        _store(lambda accs: accs))
    tk_n = _tile(N, 1024)
    (gW_down,) = _fused_matmul(
        "ffn_down_dw", (HB, D // tn_d, N // tk_n),
        [(f, BS((None, tk_n, FC), lambda j, n, k: (j, k, 0)), d_dn, BS((tk_n, tn_d), lambda j, n, k: (k, n)), "tn", 0)],
        [(FC, tn_d)], [], [(_sds((HB, FC, D), BF16), BS((None, FC, tn_d), lambda j, n, k: (j, 0, n)))],
        _store(lambda accs: accs))
    x_down = x_start("down", [gW_down.reshape(NDEV, FC // 2, D)])
    dup_a, dup_b, dcw_a, dcw_b, dcb_a, dcb_b = _gate_bwd(up_pre, conv_w_all, conv_b_blk, df, L, deps=[x_down[4]])
    dpre_a = _conv_bwd(dup_a, conv_w_all, L, 0)
    dpre_b = _conv_bwd(dup_b, conv_w_all, L, HB)
    (dc,) = _fused_matmul(
        "ffn_up_dx", (N // tm2, D // tn_d, HB),
        [(dpre_a, BS((None, tm2, FC), lambda i, j, k: (k, i, 0)), Wup, BS((None, tn_d, FC), lambda i, j, k: (k, j, 0)), "nt", 0),
         (dpre_b, BS((None, tm2, FC), lambda i, j, k: (k, i, 0)), Wup, BS((None, tn_d, FC), lambda i, j, k: (k + HB, j, 0)), "nt", 0)],
        [(tm2, tn_d)], [], [(_sds((N, D), F32), BS((tm2, tn_d), lambda i, j, k: (i, j)))],
        _store(lambda accs: accs))
    tk_n2 = _tile(N, 512)
    tm_d = _tile(D, 512)
    gW_up_a, gW_up_b = _fused_matmul(
        "ffn_up_dw", (HB, D // tm_d, N // tk_n2),
        [(c, BS((tk_n2, tm_d), lambda j, n, k: (k, n)), dpre_a, BS((None, tk_n2, FC), lambda j, n, k: (j, k, 0)), "tn", 0),
         (c, BS((tk_n2, tm_d), lambda j, n, k: (k, n)), dpre_b, BS((None, tk_n2, FC), lambda j, n, k: (j, k, 0)), "tn", 1)],
        [(tm_d, FC), (tm_d, FC)], [],
        [(_sds((HB, D, FC), BF16), BS((None, tm_d, FC), lambda j, n, k: (j, n, 0)))] * 2,
        _store(lambda accs: accs))
    x_up = x_start("up", [jnp.concatenate([gW_up_a, gW_up_b], axis=0)])

    dh1, d_o, dg2, dg3 = _mid_bwd(dh2, dc, h1, o_mix, norm_post_mix, norm_pre_ffn, deps=[x_up[4]])

    def dmerged_epi(ids, accs, ex, o):
        gs_ref, gp_ref, ys_ref, yp_ref = ex
        dm = accs[0]
        sg_s, sg_p = _sigmoid(gs_ref[...]), _sigmoid(gp_ref[...])
        o[0][...] = (dm * sg_s).astype(BF16)
        o[1][...] = (dm * sg_p).astype(BF16)
        o[2][...] = (dm * ys_ref[...] * sg_s * (1.0 - sg_s)).astype(BF16)
        o[3][...] = (dm * yp_ref[...] * sg_p * (1.0 - sg_p)).astype(BF16)

    dYs, dYp, dgs, dgp = _fused_matmul(
        "out_proj_dx", (N // tm2, NDEV, 1),
        [(d_o, BS((tm2, D), lambda i, q, k: (i, 0)), Wout, BS((U, D), lambda i, q, k: (q, 0)), "nt", 0)],
        [(tm2, U)],
        [(proj, gs_blk), (proj, gp_blk), (Ys, out_blk), (Yp, out_blk)],
        [(_sds((N, D), BF16), out_blk)] * 4,
        dmerged_epi)
    (gW_out,) = _fused_matmul(
        "out_proj_dw", (D // tm_d, D // tn_d, 1),
        [(merged, BS((N, tm_d), lambda i, j, k: (0, i)), d_o, BS((N, tn_d), lambda i, j, k: (0, j)), "tn", 0)],
        [(tm_d, tn_d)], [], [(_sds((D, D), BF16), BS((tm_d, tn_d), lambda i, j, k: (i, j)))],
        _store(lambda accs: accs))
    tm_s = _tile(DS, 512)
    gW_bs, gW_bp = _fused_matmul(
        "branch_dw", (DS // tm_s, NDEV, 1),
        [(ys, BS((N, tm_s), lambda i, q, k: (0, i)), dYs, BS((N, U), lambda i, q, k: (0, q)), "tn", 0),
         (yp, BS((N, tm_s), lambda i, q, k: (0, i)), dYp, BS((N, U), lambda i, q, k: (0, q)), "tn", 1)],
        [(tm_s, U), (tm_s, U)], [],
        [(_sds((NDEV, DS, U), BF16), BS((None, tm_s, U), lambda i, q, k: (q, i, 0)))] * 2,
        _store(lambda accs: accs))
    x_br = x_start("branch", [gW_bs, gW_bp, gW_out.reshape(NDEV, U, D)])

    tn_p = _tile(PG, 512)

    def dyp_epi(ids, accs, ex, o):
        q_ref, sc_ref = ex
        first = ids[1] == 0
        dyp = accs[0]
        dq = dyp * sc_ref[...]
        o[0][...] = dq.astype(BF16)
        _rowsum_into(o[1], first, dyp * q_ref[...])
        _rowsum_into(o[2], first, dq)

    dq, d_pscale, d_pb = _fused_matmul(
        "branch_pool_dx", (DP // tn_p, N // tm, NDEV),
        [(dYp, BS((tm, U), lambda j, i, k: (i, k)), Wbp, BS((None, tn_p, U), lambda j, i, k: (k, j, 0)), "nt", 0)],
        [(tm, tn_p)],
        [(qp, BS((tm, tn_p), lambda j, i, k: (i, j))), (pool_scale, BS((1, tn_p), lambda j, i, k: (0, j)))],
        [(_sds((N, DP), BF16), BS((tm, tn_p), lambda j, i, k: (i, j))),
         (_sds((1, DP), F32), BS((1, tn_p), lambda j, i, k: (0, j))),
         (_sds((1, DP), F32), BS((1, tn_p), lambda j, i, k: (0, j)))],
        dyp_epi, deps=[x_br[4]])
    (dz,) = _fused_matmul(
        "pool_mm_dx", (N // tm, NPG, 1),
        [(dq, BS((tm, PG), lambda i, g, k: (i, g)), Wpool, BS((None, PG, PG), lambda i, g, k: (g, 0, 0)), "nt", 0)],
        [(tm, PG)], [], [(_sds((N, DP), F32), BS((tm, PG), lambda i, g, k: (i, g)))],
        _store(lambda accs: accs))
    (gW_pool,) = _fused_matmul(
        "pool_mm_dw", (NPG, 1),
        [(z, BS((N, PG), lambda g, k: (0, g)), dq, BS((N, PG), lambda g, k: (0, g)), "tn", 0)],
        [(PG, PG)], [], [(_sds((NPG, PG, PG), BF16), BS((None, PG, PG), lambda g, k: (g, 0, 0)))],
        _store(lambda accs: accs))
    du_pool = _pool_bwd(dz, nseq, L)

    def dys_epi(ids, accs, ex, o):
        zg_ref, y0_ref = ex
        first = ids[1] == 0
        dys = accs[0]
        sg = _sigmoid(zg_ref[...])
        dzg = dys * _gelu(y0_ref[...]) * sg * (1.0 - sg)
        o[0][...] = dzg.astype(BF16)
        o[1][...] = dys * sg
        _rowsum_into(o[2], first, dzg)

    dzg, dy1_direct, d_glu_b = _fused_matmul(
        "branch_ssm_dx", (DS // tn_s, N // tm, NDEV),
        [(dYs, BS((tm, U), lambda j, i, k: (i, k)), Wbs, BS((None, tn_s, U), lambda j, i, k: (k, j, 0)), "nt", 0)],
        [(tm, tn_s)],
        [(zg, BS((tm, tn_s), lambda j, i, k: (i, j))), (y0, BS((tm, tn_s), lambda j, i, k: (i, j)))],
        [(_sds((N, DS), BF16), BS((tm, tn_s), lambda j, i, k: (i, j))),
         (_sds((N, DS), F32), BS((tm, tn_s), lambda j, i, k: (i, j))),
         (_sds((1, DS), F32), BS((1, tn_s), lambda j, i, k: (0, j)))],
        dys_epi)
    (gW_glu,) = _fused_matmul(
        "ssm_glu_dw", (DS // tm_s, DS // tn_s, 1),
        [(y1, BS((N, tm_s), lambda i, j, k: (0, i)), dzg, BS((N, tn_s), lambda i, j, k: (0, j)), "tn", 0)],
        [(tm_s, tn_s)], [], [(_sds((DS, DS), BF16), BS((tm_s, tn_s), lambda i, j, k: (i, j)))],
        _store(lambda accs: accs))
    x_mix = x_start("mix", [gW_glu.reshape(NDEV, DS // NDEV, DS),
                            gW_pool.reshape(NPG, NDEV, PG // NDEV, PG).transpose(1, 0, 2, 3)])

    tn_c = _tile(DS, CH)

    def dy0_epi(ids, accs, ex, o):
        d1_ref, y0_ref, u_ref = ex
        first = ids[1] == 0
        dy0 = (accs[0] + d1_ref[...]) * _gelu_grad(y0_ref[...])
        o[0][...] = dy0
        _rowsum_into(o[1], first, dy0 * u_ref[...])

    dy0, d_ssm_d = _fused_matmul(
        "ssm_glu_dx", (DS // tn_c, N // tm, 1),
        [(dzg, BS((tm, DS), lambda j, i, k: (i, 0)), Wglu, BS((tn_c, DS), lambda j, i, k: (j, 0)), "nt", 0)],
        [(tm, tn_c)],
        [(dy1_direct, BS((tm, tn_c), lambda j, i, k: (i, j))), (y0, BS((tm, tn_c), lambda j, i, k: (i, j))),
         (proj, BS((tm, tn_c), lambda j, i, k: (i, j)))],
        [(_sds((N, DS), F32), BS((tm, tn_c), lambda j, i, k: (i, j))),
         (_sds((1, DS), F32), BS((1, tn_c), lambda j, i, k: (0, j)))],
        dy0_epi, deps=[x_mix[4]])

    ds_re, ds_im = _fused_matmul(
        "ssm_out_dx", (N // tm2, NCH, 1),
        [(dy0, BS((tm2, CH), lambda i, c, k: (i, c)), WCre, BS((None, CS, CH), lambda i, c, k: (c, 0, 0)), "nt", 0),
         (dy0, BS((tm2, CH), lambda i, c, k: (i, c)), WCim, BS((None, CS, CH), lambda i, c, k: (c, 0, 0)), "nt", 1)],
        [(tm2, CS), (tm2, CS)], [],
        [(_sds((N, SL), F32), BS((tm2, CS), lambda i, c, k: (i, c)))] * 2,
        _store(lambda accs: accs))
    dWCre, dWCim = _fused_matmul(
        "ssm_out_dw", (NCH, N // tk_n),
        [(s_re, BS((tk_n, CS), lambda c, k: (k, c)), dy0, BS((tk_n, CH), lambda c, k: (k, c)), "tn", 0),
         (s_im, BS((tk_n, CS), lambda c, k: (k, c)), dy0, BS((tk_n, CH), lambda c, k: (k, c)), "tn", 1)],
        [(CS, CH), (CS, CH)], [],
        [(_sds((NCH, CS, CH), F32), BS((None, CS, CH), lambda c, k: (c, 0, 0)))] * 2,
        _store(lambda accs: accs))
    lam_r, lam_i, d_ab_re, d_ab_im = _scan_bwd(ds_re, ds_im, s_re, s_im, pw_re, pw_im, pwf_re, pwf_im, nseq, L)

    def du_epi(ids, accs, ex, o):
        dy0_ref, d_ref = ex
        o[0][...] = (accs[0] + dy0_ref[...] * d_ref[...]).astype(BF16)

    (du_ssm,) = _fused_matmul(
        "ssm_in_dx", (N // tm2, NCH, 1),
        [(lam_r, BS((tm2, CS), lambda i, c, k: (i, c)), WB, BS((None, CH, CS), lambda i, c, k: (c, 0, 0)), "nt", 0),
         (lam_i, BS((tm2, CS), lambda i, c, k: (i, c)), WB, BS((None, CH, CS), lambda i, c, k: (c, 0, 1)), "nt", 0)],
        [(tm2, CH)],
        [(dy0, BS((tm2, CH), lambda i, c, k: (i, c))), (ssm_d, BS((1, CH), lambda i, c, k: (0, c)))],
        [(_sds((N, DS), BF16), BS((tm2, CH), lambda i, c, k: (i, c)))],
        du_epi)
    dWBre, dWBim = _fused_matmul(
        "ssm_in_dw", (NCH, N // tk_n),
        [(proj, BS((tk_n, CH), lambda c, k: (k, c)), lam_r, BS((tk_n, CS), lambda c, k: (k, c)), "tn", 0),
         (proj, BS((tk_n, CH), lambda c, k: (k, c)), lam_i, BS((tk_n, CS), lambda c, k: (k, c)), "tn", 1)],
        [(CH, CS), (CH, CS)], [],
        [(_sds((NCH, CH, CS), F32), BS((None, CH, CS), lambda c, k: (c, 0, 0)))] * 2,
        _store(lambda accs: accs))
    d_bbr = _block_diag_in_grad(dWBre, J, G, P)
    d_bbi = _block_diag_in_grad(dWBim, J, G, P)
    d_lam_re, d_lam_im, d_log_step, d_br_t, d_bi_t = _ssm_param_bwd(
        lam_re, lam_im, log_step, br_t, bi_t,
        d_ab_re.reshape(nseq * SUBLANES, G, P), d_ab_im.reshape(nseq * SUBLANES, G, P), d_bbr, d_bbi)
    d_c_re = _block_diag_out_grad(dWCre, G, J, P)
    d_c_im = -_block_diag_out_grad(dWCim, G, J, P)

    dproj = jnp.concatenate([du_ssm, du_pool, dgs, dgp], axis=1)
    (gW_in,) = _fused_matmul(
        "in_proj_dw", (D // tm_d, nq, 1),
        [(a, BS((N, tm_d), lambda i, q, k: (0, i)), dproj, BS((N, U), lambda i, q, k: (0, q)), "tn", 0)],
        [(tm_d, U)], [], [(_sds((NDEV, D, 3 * U), BF16), BS((None, tm_d, U), lambda i, q, k: (q // 3, i, q % 3)))],
        _store(lambda accs: accs))
    x_in = x_start("in", [gW_in])
    (da,) = _fused_matmul(
        "in_proj_dx", (N // tm, D // tn_d2, nq),
        [(dproj, BS((tm, U), lambda i, j, k: (i, k)), Win, BS((None, tn_d2, U), lambda i, j, k: (k // 3, j, k % 3)), "nt", 0)],
        [(tm, tn_d2)], [], [(_sds((N, D), F32), BS((tm, tn_d2), lambda i, j, k: (i, j)))],
        _store(lambda accs: accs), deps=[x_in[4]])
    grad_x, dg1 = _pre_bwd(x2, da, dh1, norm_pre_mix)

    d_conv_w = jnp.concatenate([dcw_a, dcw_b], axis=0).transpose(1, 0, 2).reshape(3, F2)
    d_conv_b = jnp.concatenate([dcb_a, dcb_b], axis=0).reshape(1, F2)
    small = {
        "norm_pre_mix": dg1, "norm_post_mix": dg2, "norm_pre_ffn": dg3, "norm_post_ffn": dg4,
        "ssm_lambda_re": d_lam_re[None], "ssm_lambda_im": d_lam_im[None], "ssm_log_step": d_log_step.reshape(1, G),
        "ssm_b_re": d_br_t.transpose(1, 2, 0)[None], "ssm_b_im": d_bi_t.transpose(1, 2, 0)[None],
        "ssm_c_re": d_c_re[None], "ssm_c_im": d_c_im[None],
        "ssm_d": d_ssm_d, "ssm_glu_b": d_glu_b, "pool_scale": d_pscale,
        "pool_b": d_pb.reshape(1, NPG, PG), "ffn_conv_w": d_conv_w[None], "ffn_conv_b": d_conv_b,
    }
    small_names = list(small)
    packed = _pack([lossv] + [small[n] for n in small_names])
    (small_all,) = _comm_gather("comm_gather_small", [packed])
    loss_rows = (D + SUBLANES * LANES - 1) // (SUBLANES * LANES) * SUBLANES
    total, loss = _small_sum(small_all, loss_rows, D)
    totals = dict(zip(small_names, _unpack(total, [lossv.shape] + [small[n].shape for n in small_names])[1:]))
    totals["pool_b"] = lax.dynamic_slice_in_dim(totals["pool_b"], dev * (PG // NDEV), PG // NDEV, axis=2)
    totals["ffn_conv_w"] = lax.dynamic_slice_in_dim(totals["ffn_conv_w"], dev * FC, FC, axis=2)
    sm_g = _pack([totals[n] for n in small_names])
    sm_w, sm_m, sm_v = (_pack([args[p + n] for n in small_names]) for p in ("", "m_", "v_"))
    _, sm_d, sm_nm, sm_nv = _adamw("adamw_small", sm_w, sm_m, sm_v, sm_g[None])
    shapes = [args[n].shape for n in small_names]
    res = {n: (totals[n], dl, nm, nv) for n, dl, nm, nv in
           zip(small_names, _unpack(sm_d, shapes), _unpack(sm_nm, shapes), _unpack(sm_nv, shapes))}

    after = sm_d
    for tag, started, group in (("down", x_down, ["w_down"]), ("up", x_up, ["w_up"]),
                                ("branch", x_br, ["w_branch_ssm", "w_branch_pool", "w_out"]),
                                ("mix", x_mix, ["ssm_glu_w", "pool_w"]), ("in", x_in, ["w_in"])):
        for n, parts in zip(group, x_finish(tag, started, after)):
            shape = args[n].shape
            cols = shape[-1]
            flat = lambda t: t.reshape(-1, cols)
            g, dl, nm, nv = _adamw("adamw_" + n, flat(args[n]), flat(args["m_" + n]), flat(args["v_" + n]),
                                   parts.reshape(NDEV, -1, cols))
            res[n] = tuple(t.reshape(shape) for t in (g, dl, nm, nv))
            after = g

    outs = [loss.reshape(()), grad_x.reshape(x.shape)]
    for k in range(4):
        outs += [res[n][k] for n in names]
    return tuple(outs)
```

```python
import functools
import math

import jax
import jax.numpy as jnp
from jax import lax
from jax.experimental import pallas as pl
from jax.experimental.pallas import tpu as pltpu

F32 = jnp.float32
BF16 = jnp.bfloat16
BS = pl.BlockSpec

NDEV = 8
SSM_GROUP = 16
SSM_STATE = 64
GROUPS_PER_CHUNK = 16
POOL_WINDOWS = (2, 4, 8, 16)
EPS = 1e-6
MIN_NEG_REAL = -1e-4
ADAM_LR, ADAM_B1, ADAM_B2, ADAM_EPS, ADAM_WD, ADAM_STEP = 0.001, 0.9, 0.999, 1e-08, 0.01, 10
LANES = 128
SUBLANES = 8
VMEM_LIMIT = 56 * 1024 * 1024

_DIMS = {"nn": (((1,), (0,)), ((), ())), "nt": (((1,), (1,)), ((), ())), "tn": (((0,), (0,)), ((), ()))}


def _tile(dim, pref, mult=LANES):
    if dim <= pref:
        return dim
    t = (pref // mult) * mult
    while t >= mult:
        if dim % t == 0:
            return t
        t -= mult
    return dim


def _pc(name, body, grid, ins, in_specs, outs, out_specs, scratch=(), deps=()):
    multi = isinstance(outs, (list, tuple))
    if deps:
        n_in, n_dep, inner = len(ins), len(deps), body

        def body(*refs):
            return inner(*refs[:n_in], *refs[n_in + n_dep:])

        ins = list(ins) + list(deps)
        in_specs = list(in_specs) + [BS(memory_space=pl.ANY)] * n_dep
    return pl.pallas_call(
        body, name=name, grid=grid, in_specs=list(in_specs),
        out_specs=list(out_specs) if multi else out_specs,
        out_shape=list(outs) if multi else outs, scratch_shapes=list(scratch),
        compiler_params=pltpu.CompilerParams(dimension_semantics=("arbitrary",) * len(grid),
                                             vmem_limit_bytes=VMEM_LIMIT),
    )(*ins)


def _sds(shape, dtype):
    return jax.ShapeDtypeStruct(tuple(shape), dtype)


def _gelu(x):
    k = math.sqrt(2.0 / math.pi)
    return 0.5 * x * (1.0 + jnp.tanh(k * (x + 0.044715 * (x * x * x))))


def _gelu_grad(x):
    k = math.sqrt(2.0 / math.pi)
    t = jnp.tanh(k * (x + 0.044715 * (x * x * x)))
    return 0.5 * (1.0 + t) + 0.5 * x * (1.0 - t * t) * (k * (1.0 + 3.0 * 0.044715 * x * x))


def _sigmoid(x):
    return jax.nn.sigmoid(x)


def _fused_matmul(name, grid, pairs, acc_shapes, extras, outs, epilogue, deps=()):
    n_p, n_e, n_o = len(pairs), len(extras), len(outs)
    rank = len(grid)
    nk = grid[-1]

    def body(*refs):
        ab = refs[:2 * n_p]
        ex = refs[2 * n_p:2 * n_p + n_e]
        o = refs[2 * n_p + n_e:2 * n_p + n_e + n_o]
        accs = refs[2 * n_p + n_e + n_o:]
        ids = [pl.program_id(d) for d in range(rank)]
        k = ids[-1]

        @pl.when(k == 0)
        def _():
            for acc in accs:
                acc[...] = jnp.zeros_like(acc)

        for p in range(n_p):
            a = ab[2 * p][...].astype(BF16)
            b = ab[2 * p + 1][...].astype(BF16)
            accs[pairs[p][5]][...] += lax.dot_general(a, b, _DIMS[pairs[p][4]], preferred_element_type=F32)

        @pl.when(k == nk - 1)
        def _():
            epilogue(ids, [acc[...] for acc in accs], ex, o)

    ins, in_specs = [], []
    for a, a_spec, b, b_spec, _, _ in pairs:
        ins += [a, b]
        in_specs += [a_spec, b_spec]
    for e, e_spec in extras:
        ins.append(e)
        in_specs.append(e_spec)
    res = _pc(name, body, grid, ins, in_specs, [s for s, _ in outs], [sp for _, sp in outs],
              scratch=[pltpu.VMEM(tuple(s), F32) for s in acc_shapes], deps=deps)
    return res


def _store(vals):
    def epilogue(ids, accs, ex, o):
        for r, v in zip(o, vals(accs)):
            r[...] = v.astype(r.dtype)
    return epilogue


def _rowsum_into(ref, first, v):
    s = jnp.sum(v, axis=0, keepdims=True)

    @pl.when(first)
    def _():
        ref[...] = s

    @pl.when(jnp.logical_not(first))
    def _():
        ref[...] += s


def _mesh_pos():
    return lax.axis_index("x"), lax.axis_index("y"), lax.axis_index("c")


def _slot(p):
    return 4 * p[0] + 2 * p[1] + p[2]


def _comm_gather(name, arrs):
    n = len(arrs)

    def body(*refs):
        ins, outs = refs[:n], refs[n:2 * n]
        send_sems, recv_sems, local_sems = refs[2 * n:]
        x, y, c = _mesh_pos()
        me, sib = (x, y, c), (x, y, 1 - c)
        chips = [(1 - x, y), (x, 1 - y), (1 - x, 1 - y)]

        def copy(r, k, block, to, src=None):
            dst = outs[r].at[_slot(block)]
            return pltpu.make_async_remote_copy(
                src_ref=dst if src is None else src, dst_ref=dst,
                send_sem=send_sems.at[r, k], recv_sem=recv_sems.at[r, k],
                device_id=to, device_id_type=pl.DeviceIdType.MESH)

        mine, first, passed = [], [], []
        for r in range(n):
            m = pltpu.make_async_copy(ins[r], outs[r].at[_slot(me)], local_sems.at[r])
            m.start()
            mine.append(m)
            f = [copy(r, 0, me, sib, src=ins[r])]
            f += [copy(r, 1 + j, me, (*chip, c), src=ins[r]) for j, chip in enumerate(chips)]
            for cp in f:
                cp.start()
            first.append(f)
        for r in range(n):
            ps = [copy(r, 4 + j, (*chip, c), sib) for j, chip in enumerate(chips)]
            for j, chip in enumerate(chips):
                copy(r, 1 + j, (*chip, c), me).wait_recv()
                ps[j].start()
            passed.append(ps)
        for r in range(n):
            copy(r, 0, sib, me).wait_recv()
            for j, chip in enumerate(chips):
                copy(r, 4 + j, (*chip, 1 - c), me).wait_recv()
            for cp in first[r] + passed[r]:
                cp.wait_send()
            mine[r].wait()

    any_spec = BS(memory_space=pl.ANY)
    return pl.pallas_call(
        body, name=name,
        in_specs=[any_spec] * n, out_specs=[any_spec] * n,
        out_shape=[_sds((NDEV,) + a.shape, a.dtype) for a in arrs],
        scratch_shapes=[pltpu.SemaphoreType.DMA((n, 7)), pltpu.SemaphoreType.DMA((n, 7)),
                        pltpu.SemaphoreType.DMA((n,))],
    )(*arrs)


_HBM = BS(memory_space=pltpu.HBM)
_SEM = BS(memory_space=pltpu.SEMAPHORE)
_ANY = BS(memory_space=pl.ANY)
_EFFECT = pltpu.SideEffectType.DATAFLOW_SIDE_EFFECTING


def _other_chips(x, y):
    return [(1 - x, y), (x, 1 - y), (1 - x, 1 - y)]


def _all_peers(x, y, c):
    peers = []
    for k in range(1, NDEV):
        kx, ky, kc = (k >> 2) & 1, (k >> 1) & 1, k & 1
        peers.append((1 - x if kx else x, 1 - y if ky else y, 1 - c if kc else c))
    return peers


def _gather_copies(src, land, send_sems, recv_sems, base):
    x, y, c = _mesh_pos()
    return [pltpu.make_async_remote_copy(
        src_ref=src, dst_ref=land.at[_slot((x, y, c))],
        send_sem=send_sems.at[base + k], recv_sem=recv_sems.at[base + k],
        device_id=(*chip, c), device_id_type=pl.DeviceIdType.MESH) for k, chip in enumerate(_other_chips(x, y))]


def _broadcast_copies(src, land, send_sems, recv_sems, base):
    x, y, c = _mesh_pos()
    return [pltpu.make_async_remote_copy(
        src_ref=src, dst_ref=land.at[_slot((x, y, c))],
        send_sem=send_sems.at[base + k], recv_sem=recv_sems.at[base + k],
        device_id=peer, device_id_type=pl.DeviceIdType.MESH) for k, peer in enumerate(_all_peers(x, y, c))]


def _exchange_copies(src, land, send_sems, recv_sems, base):
    x, y, c = _mesh_pos()
    return [pltpu.make_async_remote_copy(
        src_ref=src.at[_slot(peer)], dst_ref=land.at[_slot((x, y, c))],
        send_sem=send_sems.at[base + k], recv_sem=recv_sems.at[base + k],
        device_id=peer, device_id_type=pl.DeviceIdType.MESH) for k, peer in enumerate(_all_peers(x, y, c))]


def _split_start(name, copies, ncopy, srcs, land_shapes, after):
    n = len(srcs)

    def body(*refs):
        src_refs, land_refs = refs[:n], refs[n:2 * n]
        send_sems, recv_sems = refs[2 * n + 1], refs[2 * n + 2]
        token = refs[-1]
        for r in range(n):
            for cp in copies(src_refs[r], land_refs[r], send_sems, recv_sems, r * ncopy):
                cp.start()
        token[...] = jnp.zeros_like(token)

    lands = [lax.empty(s.shape, s.dtype) for s in land_shapes]
    ins = list(srcs) + [pltpu.with_memory_space_constraint(a, pltpu.HBM) for a in lands]
    out_shape = ([pltpu.SemaphoreType.DMA((n * ncopy,)), pltpu.SemaphoreType.DMA((n * ncopy,))]
                 + [pltpu.HBM(a.shape, a.dtype) for a in lands]
                 + [_sds((SUBLANES, LANES), F32)])
    res = pl.pallas_call(
        body, name=name, out_shape=out_shape,
        in_specs=[_HBM] * (2 * n) + [_ANY], out_specs=[_SEM, _SEM] + [_HBM] * n + [BS(memory_space=pltpu.VMEM)],
        input_output_aliases={n + i: 2 + i for i in range(n)},
        compiler_params=pltpu.CompilerParams(has_side_effects=_EFFECT),
    )(*ins, after)
    return res[0], res[1], list(srcs), list(res[2:2 + n]), res[-1]


def _split_wait(name, copies, started, after):
    send_sems, recv_sems, srcs, lands, _ = started
    n = len(srcs)
    ncopy = send_sems.shape[0] // n

    def body(*refs):
        src_refs, land_refs = refs[:n], refs[n:2 * n]
        send_sems, recv_sems = refs[2 * n], refs[2 * n + 1]
        for r in range(n):
            for cp in copies(src_refs[r], land_refs[r], send_sems, recv_sems, r * ncopy):
                cp.wait_send()
                cp.wait_recv()

    res = pl.pallas_call(
        body, name=name, out_shape=[pltpu.HBM(a.shape, a.dtype) for a in lands],
        in_specs=[_HBM] * (2 * n) + [_SEM, _SEM, _ANY], out_specs=[_HBM] * n,
        input_output_aliases={n + i: i for i in range(n)},
        compiler_params=pltpu.CompilerParams(has_side_effects=_EFFECT),
    )(*srcs, *lands, send_sems, recv_sems, after)
    return list(srcs), list(res)


D2D_STREAMS = 8
D2D_MIN_BYTES = 1 << 20


def _pieces(shape, dtype):
    itemsize = jnp.dtype(dtype).itemsize
    if math.prod(shape) * itemsize < D2D_MIN_BYTES:
        return [()]
    rows_unit = 1 if len(shape) > 2 else SUBLANES * (4 // itemsize)
    for k in (D2D_STREAMS, D2D_STREAMS // 2, D2D_STREAMS // 4):
        if k > 1 and shape[0] % (k * rows_unit) == 0:
            step = shape[0] // k
            return [(pl.ds(i * step, step),) for i in range(k)]
        if k > 1 and shape[-1] % (k * LANES) == 0:
            step = shape[-1] // k
            return [(slice(None),) * (len(shape) - 1) + (pl.ds(i * step, step),) for i in range(k)]
    return [()]


def _gather_d2d(name, shards, lands):
    n = len(shards)

    def body(*refs):
        shard_refs = refs[:n]
        land_refs = refs[2 * n:3 * n]
        send_sems, recv_sems, local_sems = refs[3 * n:]
        x, y, c = _mesh_pos()
        me, sib = (x, y, c), (x, y, 1 - c)
        chips = _other_chips(x, y)

        def copy(r, k, block, src=None, piece=()):
            dst = land_refs[r].at[_slot(block)]
            src = dst if src is None else src
            if piece:
                src, dst = src.at[piece], dst.at[piece]
            return pltpu.make_async_remote_copy(
                src_ref=src, dst_ref=dst, send_sem=send_sems.at[r, k], recv_sem=recv_sems.at[r, k],
                device_id=sib, device_id_type=pl.DeviceIdType.MESH)

        mine = []
        for r in range(n):
            m = pltpu.make_async_copy(shard_refs[r], land_refs[r].at[_slot(me)], local_sems.at[r])
            m.start()
            mine.append(m)
            for piece in _pieces(shards[r].shape, shards[r].dtype):
                copy(r, 0, me, src=shard_refs[r], piece=piece).start()
                for j, chip in enumerate(chips):
                    copy(r, 1 + j, (*chip, c), piece=piece).start()
        for r in range(n):
            copy(r, 0, sib).wait_recv()
            for j, chip in enumerate(chips):
                copy(r, 1 + j, (*chip, 1 - c)).wait_recv()
            mine[r].wait()
        for r in range(n):
            for k in range(4):
                copy(r, k, me).wait_send()

    return pl.pallas_call(
        body, name=name,
        in_specs=[_ANY] * (2 * n), out_specs=[_ANY] * n,
        out_shape=[_sds(a.shape, a.dtype) for a in lands],
        input_output_aliases={n + i: i for i in range(n)},
        scratch_shapes=[pltpu.SemaphoreType.DMA((n, 4)), pltpu.SemaphoreType.DMA((n, 4)),
                        pltpu.SemaphoreType.DMA((n,))],
    )(*shards, *lands)


def _adamw(name, w, m, v, parts):
    R, C = w.shape
    S = parts.shape[0]
    tr = _tile(R, max(SUBLANES, (256 * 1024) // C), SUBLANES)

    def body(w_ref, m_ref, v_ref, p_ref, g_ref, d_ref, nm_ref, nv_ref):
        g = p_ref[0].astype(F32)
        for s in range(1, S):
            g = g + p_ref[s].astype(F32)
        m2 = ADAM_B1 * m_ref[...] + (1.0 - ADAM_B1) * g
        v2 = ADAM_B2 * v_ref[...] + (1.0 - ADAM_B2) * (g * g)
        m_hat = m2 / (1.0 - ADAM_B1 ** ADAM_STEP)
        v_hat = v2 / (1.0 - ADAM_B2 ** ADAM_STEP)
        g_ref[...] = g
        d_ref[...] = -ADAM_LR * (m_hat / (jnp.sqrt(v_hat) + ADAM_EPS) + ADAM_WD * w_ref[...])
        nm_ref[...] = m2
        nv_ref[...] = v2

    blk = BS((tr, C), lambda i: (i, 0))
    return _pc(name, body, (R // tr,), [w, m, v, parts],
               [blk, blk, blk, BS((S, tr, C), lambda i: (0, i, 0))],
               [_sds((R, C), F32)] * 4, [blk] * 4)


def _ssm_disc(lam_re, lam_im, log_step, br_t, bi_t):
    lr = jnp.minimum(lam_re, MIN_NEG_REAL)
    li = lam_im
    dt = jnp.exp(log_step)
    mag = jnp.exp(lr * dt)
    ang = li * dt
    ab_re = mag * jnp.cos(ang)
    ab_im = mag * jnp.sin(ang)
    nr = ab_re - 1.0
    ni = ab_im
    den = lr * lr + li * li
    f_re = (nr * lr + ni * li) / den
    f_im = (ni * lr - nr * li) / den
    bb_re = f_re[None] * br_t - f_im[None] * bi_t
    bb_im = f_re[None] * bi_t + f_im[None] * br_t
    return ab_re, ab_im, bb_re, bb_im


def _ssm_param_fwd(lam_re, lam_im, log_step, br_t, bi_t):
    G, P = lam_re.shape

    def body(lr_ref, li_ref, ls_ref, br_ref, bi_ref, pw_re_ref, pw_im_ref, bbr_ref, bbi_ref):
        ab_re, ab_im, bb_re, bb_im = _ssm_disc(lr_ref[...], li_ref[...], ls_ref[...], br_ref[...], bi_ref[...])
        bbr_ref[...] = bb_re
        bbi_ref[...] = bb_im
        pr, pi = ab_re, ab_im
        for r in range(SUBLANES):
            pw_re_ref[r] = pr
            pw_im_ref[r] = pi
            pr, pi = pr * ab_re - pi * ab_im, pr * ab_im + pi * ab_re

    full = lambda a: BS(a.shape, lambda i: (0,) * a.ndim)
    ins = [lam_re, lam_im, log_step, br_t, bi_t]
    outs = [_sds((SUBLANES, G, P), F32)] * 2 + [_sds(br_t.shape, F32)] * 2
    return _pc("ssm_param_fwd", body, (1,), ins, [full(a) for a in ins], outs, [full(o) for o in outs])


def _ssm_param_bwd(lam_re, lam_im, log_step, br_t, bi_t, d_ab_re, d_ab_im, d_bbr, d_bbi):
    def body(lr_ref, li_ref, ls_ref, br_ref, bi_ref, dar_ref, dai_ref, dbr_ref, dbi_ref,
             o_lr, o_li, o_ls, o_br, o_bi):
        prim = (lr_ref[...], li_ref[...], ls_ref[...], br_ref[...], bi_ref[...])
        _, vjp = jax.vjp(_ssm_disc, *prim)
        dar = dar_ref[0]
        dai = dai_ref[0]
        for k in range(1, dar_ref.shape[0]):
            dar = dar + dar_ref[k]
            dai = dai + dai_ref[k]
        g = vjp((dar, dai, dbr_ref[...], dbi_ref[...]))
        for r, v in zip((o_lr, o_li, o_ls, o_br, o_bi), g):
            r[...] = v

    full = lambda a: BS(a.shape, lambda i: (0,) * a.ndim)
    ins = [lam_re, lam_im, log_step, br_t, bi_t, d_ab_re, d_ab_im, d_bbr, d_bbi]
    outs = [_sds(a.shape, F32) for a in (lam_re, lam_im, log_step, br_t, bi_t)]
    return _pc("ssm_param_bwd", body, (1,), ins, [full(a) for a in ins], outs, [full(o) for o in outs])


def _bcast_row(ref, r, w):
    return jnp.broadcast_to(ref[pl.ds(r, 1), :], (SUBLANES, w))


def _pick_row(x, row, r):
    return jnp.broadcast_to(jnp.sum(jnp.where(row == r, x, 0.0), axis=0, keepdims=True), x.shape)


def _scan_fwd(bu_re, bu_im, pw_re, pw_im, nseq, L):
    N, SL = bu_re.shape
    W = _tile(SL, 256)

    def body(bre_ref, bim_ref, pre_ref, pim_ref, sre_ref, sim_ref):
        pre, pim = pre_ref[...], pim_ref[...]
        steps = [(k, _bcast_row(pre_ref, k - 1, W), _bcast_row(pim_ref, k - 1, W)) for k in (1, 2, 4)]
        row = lax.broadcasted_iota(jnp.int32, (SUBLANES, W), 0)

        def step(i, carry):
            cr, ci = carry
            r0 = pl.multiple_of(i * SUBLANES, SUBLANES)
            xr = bre_ref[pl.ds(r0, SUBLANES), :]
            xi = bim_ref[pl.ds(r0, SUBLANES), :]
            for k, ar, ai in steps:
                sr = pltpu.roll(xr, k, axis=0)
                si = pltpu.roll(xi, k, axis=0)
                keep = row >= k
                xr, xi = (xr + jnp.where(keep, ar * sr - ai * si, 0.0),
                          xi + jnp.where(keep, ar * si + ai * sr, 0.0))
            xr, xi = xr + (pre * cr - pim * ci), xi + (pre * ci + pim * cr)
            sre_ref[pl.ds(r0, SUBLANES), :] = xr
            sim_ref[pl.ds(r0, SUBLANES), :] = xi
            return _pick_row(xr, row, SUBLANES - 1), _pick_row(xi, row, SUBLANES - 1)

        zero = jnp.zeros((SUBLANES, W), F32)
        lax.fori_loop(0, L // SUBLANES, step, (zero, zero))

    blk = BS((L, W), lambda s, j: (s, j))
    pw = BS((SUBLANES, W), lambda s, j: (0, j))
    return _pc("ssm_scan_fwd", body, (nseq, SL // W), [bu_re, bu_im, pw_re, pw_im], [blk, blk, pw, pw],
               [_sds((N, SL), F32)] * 2, [blk, blk])


def _scan_bwd(ds_re, ds_im, s_re, s_im, pw_re, pw_im, pwf_re, pwf_im, nseq, L):
    N, SL = ds_re.shape
    W = _tile(SL, 256)
    nt = L // SUBLANES

    def body(dsr_ref, dsi_ref, sre_ref, sim_ref, pre_ref, pim_ref, fre_ref, fim_ref,
             lre_ref, lim_ref, dar_ref, dai_ref):
        fre, fim = fre_ref[...], -fim_ref[...]
        steps = [(k, _bcast_row(pre_ref, k - 1, W), -_bcast_row(pim_ref, k - 1, W)) for k in (1, 2, 4)]
        row = lax.broadcasted_iota(jnp.int32, (SUBLANES, W), 0)

        def step(ii, carry):
            cr, ci, acr, aci = carry
            i = nt - 1 - ii
            r0 = pl.multiple_of(i * SUBLANES, SUBLANES)
            xr = dsr_ref[pl.ds(r0, SUBLANES), :]
            xi = dsi_ref[pl.ds(r0, SUBLANES), :]
            for k, ar, ai in steps:
                sr = pltpu.roll(xr, SUBLANES - k, axis=0)
                si = pltpu.roll(xi, SUBLANES - k, axis=0)
                keep = row < SUBLANES - k
                xr, xi = (xr + jnp.where(keep, ar * sr - ai * si, 0.0),
                          xi + jnp.where(keep, ar * si + ai * sr, 0.0))
            xr, xi = xr + (fre * cr - fim * ci), xi + (fre * ci + fim * cr)
            lre_ref[pl.ds(r0, SUBLANES), :] = xr
            lim_ref[pl.ds(r0, SUBLANES), :] = xi
            p0 = pl.multiple_of(jnp.maximum(i - 1, 0) * SUBLANES, SUBLANES)
            has_prev = i > 0
            spr = jnp.where(row == 0,
                            jnp.where(has_prev, pltpu.roll(sre_ref[pl.ds(p0, SUBLANES), :], 1, axis=0), 0.0),
                            pltpu.roll(sre_ref[pl.ds(r0, SUBLANES), :], 1, axis=0))
            spi = jnp.where(row == 0,
                            jnp.where(has_prev, pltpu.roll(sim_ref[pl.ds(p0, SUBLANES), :], 1, axis=0), 0.0),
                            pltpu.roll(sim_ref[pl.ds(r0, SUBLANES), :], 1, axis=0))
            acr = acr + (xr * spr + xi * spi)
            aci = aci + (xi * spr - xr * spi)
            return _pick_row(xr, row, 0), _pick_row(xi, row, 0), acr, aci

        zero = jnp.zeros((SUBLANES, W), F32)
        _, _, acr, aci = lax.fori_loop(0, nt, step, (zero, zero, zero, zero))
        dar_ref[...] = acr
        dai_ref[...] = aci

    blk = BS((L, W), lambda s, j: (s, j))
    pw = BS((SUBLANES, W), lambda s, j: (0, j))
    da = BS((None, SUBLANES, W), lambda s, j: (s, 0, j))
    return _pc("ssm_scan_bwd", body, (nseq, SL // W),
               [ds_re, ds_im, s_re, s_im, pw_re, pw_im, pwf_re, pwf_im], [blk] * 4 + [pw] * 4,
               [_sds((N, SL), F32)] * 2 + [_sds((nseq, SUBLANES, SL), F32)] * 2, [blk, blk, da, da])


def _pool_select(g, vals):
    return jnp.where(g == 0, vals[0], jnp.where(g == 1, vals[1], jnp.where(g == 2, vals[2], vals[3])))


def _pool_fwd(proj, col0, DP, nseq, L):
    N = proj.shape[0]
    PG = DP // len(POOL_WINDOWS)
    W = _tile(PG, 256)

    def body(v_ref, z_ref):
        g = pl.program_id(1) // (PG // W)
        v = v_ref[...]
        row = lax.broadcasted_iota(jnp.int32, (L, W), 0)
        sums, s, k = [], v, 1
        for _ in POOL_WINDOWS:
            s = s + jnp.where(row >= k, pltpu.roll(s, k, axis=0), 0.0)
            sums.append(s)
            k *= 2
        win = _pool_select(g, [float(w) for w in POOL_WINDOWS])
        cnt = jnp.minimum((row + 1).astype(F32), win)
        z_ref[...] = (_pool_select(g, sums) / cnt - v).astype(z_ref.dtype)

    return _pc("pool_fwd", body, (nseq, DP // W), [proj], [BS((L, W), lambda s, j: (s, col0 // W + j))],
               _sds((N, DP), BF16), BS((L, W), lambda s, j: (s, j)))


def _pool_bwd(dz, nseq, L):
    N, DP = dz.shape
    PG = DP // len(POOL_WINDOWS)
    W = _tile(PG, 256)

    def body(dz_ref, dv_ref):
        g = pl.program_id(1) // (PG // W)
        d = dz_ref[...]
        row = lax.broadcasted_iota(jnp.int32, (L, W), 0)
        win = _pool_select(g, [float(w) for w in POOL_WINDOWS])
        s = d / jnp.minimum((row + 1).astype(F32), win)
        sums, k = [], 1
        for _ in POOL_WINDOWS:
            s = s + jnp.where(row < L - k, pltpu.roll(s, L - k, axis=0), 0.0)
            sums.append(s)
            k *= 2
        dv_ref[...] = (_pool_select(g, sums) - d).astype(dv_ref.dtype)

    blk = BS((L, W), lambda s, j: (s, j))
    return _pc("pool_bwd", body, (nseq, DP // W), [dz], [blk], _sds((N, DP), BF16), blk)


def _rstd(x):
    return lax.rsqrt(jnp.mean(x * x, axis=-1, keepdims=True) + EPS)


def _norm_bwd(dy, xhat, rstd, gain):
    t = dy * gain
    return rstd * (t - xhat * jnp.mean(t * xhat, axis=-1, keepdims=True))


def _pre_norm(x, g1):
    N, D = x.shape
    tr = _tile(N, 128, SUBLANES)

    def body(x_ref, g_ref, a_ref):
        xv = x_ref[...]
        a_ref[...] = (xv * _rstd(xv) * g_ref[...]).astype(a_ref.dtype)

    row = BS((tr, D), lambda i: (i, 0))
    vec = BS((1, D), lambda i: (0, 0))
    return _pc("pre_norm", body, (N // tr,), [x, g1], [row, vec], _sds((N, D), BF16), row)


def _mid_norm(x, o, g2, g3):
    N, D = x.shape
    tr = _tile(N, 128, SUBLANES)

    def body(x_ref, o_ref, g2_ref, g3_ref, h1_ref, c_ref):
        ov = o_ref[...]
        h1 = x_ref[...] + ov * _rstd(ov) * g2_ref[...]
        h1_ref[...] = h1
        c_ref[...] = (h1 * _rstd(h1) * g3_ref[...]).astype(c_ref.dtype)

    row = BS((tr, D), lambda i: (i, 0))
    vec = BS((1, D), lambda i: (0, 0))
    return _pc("mid_norm", body, (N // tr,), [x, o, g2, g3], [row, row, vec, vec],
               [_sds((N, D), F32), _sds((N, D), BF16)], [row, row])


def _post_ffn(h1, dn, tgt, g4):
    N, D = h1.shape
    tr = _tile(N, 128, SUBLANES)

    def body(h1_ref, dn_ref, t_ref, g_ref, dh2_ref, ddn_ref, lossv_ref, dg_ref):
        first = pl.program_id(0) == 0
        dnv = dn_ref[...]
        rstd = _rstd(dnv)
        xhat = dnv * rstd
        gain = g_ref[...]
        err = (h1_ref[...] + xhat * gain) - t_ref[...]
        dh2 = err / float(D)
        dh2_ref[...] = dh2
        ddn_ref[...] = _norm_bwd(dh2, xhat, rstd, gain).astype(ddn_ref.dtype)
        _rowsum_into(lossv_ref, first, err * err)
        _rowsum_into(dg_ref, first, dh2 * xhat)

    row = BS((tr, D), lambda i: (i, 0))
    vec = BS((1, D), lambda i: (0, 0))
    return _pc("post_ffn", body, (N // tr,), [h1, dn, tgt, g4], [row, row, row, vec],
               [_sds((N, D), F32), _sds((N, D), BF16), _sds((1, D), F32), _sds((1, D), F32)], [row, row, vec, vec])


def _mid_bwd(dh2, dc, h1, o, g2, g3, deps=()):
    N, D = h1.shape
    tr = _tile(N, 128, SUBLANES)

    def body(dh2_ref, dc_ref, h1_ref, o_ref, g2_ref, g3_ref, dh1_ref, do_ref, dg2_ref, dg3_ref):
        first = pl.program_id(0) == 0
        h1 = h1_ref[...]
        r3 = _rstd(h1)
        hc = h1 * r3
        dcv = dc_ref[...]
        dh1 = dh2_ref[...] + _norm_bwd(dcv, hc, r3, g3_ref[...])
        dh1_ref[...] = dh1
        ov = o_ref[...]
        r2 = _rstd(ov)
        ho = ov * r2
        do_ref[...] = _norm_bwd(dh1, ho, r2, g2_ref[...]).astype(do_ref.dtype)
        _rowsum_into(dg3_ref, first, dcv * hc)
        _rowsum_into(dg2_ref, first, dh1 * ho)

    row = BS((tr, D), lambda i: (i, 0))
    vec = BS((1, D), lambda i: (0, 0))
    return _pc("mid_bwd", body, (N // tr,), [dh2, dc, h1, o, g2, g3], [row] * 4 + [vec, vec],
               [_sds((N, D), F32), _sds((N, D), BF16), _sds((1, D), F32), _sds((1, D), F32)], [row, row, vec, vec],
               deps=deps)


def _pre_bwd(x, da, dh1, g1):
    N, D = x.shape
    tr = _tile(N, 128, SUBLANES)

    def body(x_ref, da_ref, dh1_ref, g_ref, dx_ref, dg_ref):
        first = pl.program_id(0) == 0
        xv = x_ref[...]
        r1 = _rstd(xv)
        xh = xv * r1
        dav = da_ref[...]
        dx_ref[...] = dh1_ref[...] + _norm_bwd(dav, xh, r1, g_ref[...])
        _rowsum_into(dg_ref, first, dav * xh)

    row = BS((tr, D), lambda i: (i, 0))
    vec = BS((1, D), lambda i: (0, 0))
    return _pc("pre_bwd", body, (N // tr,), [x, da, dh1, g1], [row, row, row, vec],
               [_sds((N, D), F32), _sds((1, D), F32)], [row, vec])


def _conv_rows(x_ref, halo_ref, first):
    x = x_ref[...]
    tr = x.shape[0]
    xx = jnp.concatenate([jnp.where(first, 0.0, halo_ref[...]), x], axis=0)
    x1 = pltpu.roll(xx, 1, axis=0)[SUBLANES:]
    x2 = pltpu.roll(xx, 2, axis=0)[SUBLANES:]
    del tr
    return x, x1, x2


def _conv_apply(rows, w_ref, b_ref):
    x, x1, x2 = rows
    return ((b_ref[...] + x2 * w_ref[pl.ds(0, 1), :]) + x1 * w_ref[pl.ds(1, 1), :]) + x * w_ref[pl.ds(2, 1), :]


def _gate_specs(N, FC, TR, half):
    tile = BS((None, TR, FC), lambda jj, i: (jj + half, i, 0))
    halo = BS((None, SUBLANES, FC), lambda jj, i: (jj + half, jnp.maximum(i * (TR // SUBLANES) - 1, 0), 0))
    cw = BS((None, 3, FC), lambda jj, i: (jj + half, 0, 0))
    cb = BS((None, 1, FC), lambda jj, i: (jj + half, 0, 0))
    return tile, halo, cw, cb


def _gate_fwd(up_pre, cw, cb, L):
    nb, N, FC = up_pre.shape
    half = nb // 2
    TR = _tile(L, 128, SUBLANES)

    def body(xa_ref, ha_ref, wa_ref, ba_ref, xb_ref, hb_ref, wb_ref, bb_ref, f_ref):
        first = (pl.program_id(1) % (L // TR)) == 0
        ua = _conv_apply(_conv_rows(xa_ref, ha_ref, first), wa_ref, ba_ref)
        ub = _conv_apply(_conv_rows(xb_ref, hb_ref, first), wb_ref, bb_ref)
        f_ref[...] = (_gelu(ua) * ub).astype(f_ref.dtype)

    sa, sb = _gate_specs(N, FC, TR, 0), _gate_specs(N, FC, TR, half)
    return _pc("gate_fwd", body, (half, N // TR), [up_pre, up_pre, cw, cb] * 2, list(sa) + list(sb),
               _sds((half, N, FC), BF16), BS((None, TR, FC), lambda jj, i: (jj, i, 0)))


def _gate_bwd(up_pre, cw, cb, df, L, deps=()):
    nb, N, FC = up_pre.shape
    half = nb // 2
    TR = _tile(L, 128, SUBLANES)

    def body(xa_ref, ha_ref, wa_ref, ba_ref, xb_ref, hb_ref, wb_ref, bb_ref, df_ref, dup_ref, dw_ref, dbias_ref):
        i = pl.program_id(1)
        first_row = i == 0
        first = (i % (L // TR)) == 0
        ra = _conv_rows(xa_ref, ha_ref, first)
        rb = _conv_rows(xb_ref, hb_ref, first)
        ua = _conv_apply(ra, wa_ref, ba_ref)
        ub = _conv_apply(rb, wb_ref, bb_ref)
        dfv = df_ref[...].astype(F32)
        dua = dfv * ub * _gelu_grad(ua)
        dub = dfv * _gelu(ua)
        dup_ref[0] = dua
        dup_ref[1] = dub
        for h, (rows, du) in enumerate(((ra, dua), (rb, dub))):
            x, x1, x2 = rows
            _rowsum_into(dbias_ref.at[h], first_row, du)
            for k, xs in enumerate((x2, x1, x)):
                _rowsum_into(dw_ref.at[h, pl.ds(k, 1), :], first_row, du * xs)

    sa, sb = _gate_specs(N, FC, TR, 0), _gate_specs(N, FC, TR, half)
    tile = BS((None, TR, FC), lambda jj, i: (jj, i, 0))
    both = BS((2, None, TR, FC), lambda jj, i: (0, jj, i, 0))
    dw = BS((2, None, 3, FC), lambda jj, i: (0, jj, 0, 0))
    dbias = BS((2, None, 1, FC), lambda jj, i: (0, jj, 0, 0))
    return _pc("gate_bwd", body, (half, N // TR), [up_pre, up_pre, cw, cb] * 2 + [df], list(sa) + list(sb) + [tile],
               [_sds((2, half, N, FC), F32), _sds((2, half, 3, FC), F32), _sds((2, half, 1, FC), F32)],
               [both, dw, dbias], deps=deps)


def _conv_bwd(dup, cw, L):
    nb, N, FC = dup.shape
    TR = _tile(L, 128, SUBLANES)
    nrb = N // SUBLANES

    def body(x_ref, h_ref, w_ref, o_ref):
        last = ((pl.program_id(1) + 1) % (L // TR)) == 0
        x = x_ref[...]
        xx = jnp.concatenate([x, jnp.where(last, 0.0, h_ref[...])], axis=0)
        x1 = pltpu.roll(xx, TR + SUBLANES - 1, axis=0)[:TR]
        x2 = pltpu.roll(xx, TR + SUBLANES - 2, axis=0)[:TR]
        o_ref[...] = (x * w_ref[pl.ds(2, 1), :] + x1 * w_ref[pl.ds(1, 1), :] + x2 * w_ref[pl.ds(0, 1), :]
                      ).astype(o_ref.dtype)

    tile = BS((None, TR, FC), lambda jj, i: (jj, i, 0))
    halo = BS((None, SUBLANES, FC), lambda jj, i: (jj, jnp.minimum((i + 1) * (TR // SUBLANES), nrb - 1), 0))
    w = BS((None, 3, FC), lambda jj, i: (jj, 0, 0))
    return _pc("conv_bwd", body, (nb, N // TR), [dup, dup, cw], [tile, halo, w], _sds((nb, N, FC), BF16), tile)


def _pack(arrs):
    parts = []
    for a in arrs:
        flat = a.reshape(-1).astype(F32)
        pad = (-flat.shape[0]) % (SUBLANES * LANES)
        parts.append(jnp.pad(flat, (0, pad)))
    return jnp.concatenate(parts).reshape(-1, LANES)


def _unpack(packed, shapes):
    flat = packed.reshape(-1)
    out, off = [], 0
    for s in shapes:
        n = math.prod(s)
        out.append(flat[off:off + n].reshape(s))
        off += n + ((-n) % (SUBLANES * LANES))
    return out


def _small_sum(gathered, loss_rows, d_model):
    S, R, C = gathered.shape

    def body(p_ref, tot_ref, loss_ref):
        t = p_ref[0]
        for s in range(1, S):
            t = t + p_ref[s]
        tot_ref[...] = t
        loss_ref[...] = jnp.full((1, 1), 0.5 / d_model, F32) * jnp.sum(t[:loss_rows])

    return _pc("small_sum", body, (1,), [gathered], [BS((S, R, C), lambda i: (0, 0, 0))],
               [_sds((R, C), F32), _sds((1, 1), F32)], [BS((R, C), lambda i: (0, 0)), BS((1, 1), lambda i: (0, 0))])


def _block_diag_in(bb_t, nch):
    J, G, P = bb_t.shape
    gl = G // nch
    b = bb_t.reshape(J, nch, gl, P).transpose(1, 0, 2, 3)
    eye = jnp.eye(gl, dtype=F32)
    w = eye[None, :, None, :, None] * b[:, None, :, :, :]
    return w.reshape(nch, gl * J, gl * P)


def _block_diag_in_grad(dw, J, G, P):
    nch = dw.shape[0]
    gl = G // nch
    d = dw.reshape(nch, gl, J, gl, P)
    d = jnp.einsum("cgjgp->jcgp", d)
    return d.reshape(J, G, P)


def _block_diag_out(c, nch):
    G, J, P = c.shape
    gl = G // nch
    cc = c.reshape(nch, gl, J, P).transpose(0, 1, 3, 2)
    eye = jnp.eye(gl, dtype=F32)
    w = cc[:, :, :, None, :] * eye[None, :, None, :, None]
    return w.reshape(nch, gl * P, gl * J)


def _block_diag_out_grad(dw, G, J, P):
    nch = dw.shape[0]
    gl = G // nch
    d = dw.reshape(nch, gl, P, gl, J)
    d = jnp.einsum("cgpgj->cgjp", d)
    return d.reshape(G, J, P)


def kernel(x, norm_pre_mix, w_in, ssm_lambda_re, ssm_lambda_im, ssm_log_step, ssm_b_re, ssm_b_im, ssm_c_re, ssm_c_im, ssm_d, ssm_glu_w, ssm_glu_b, pool_w, pool_b, pool_scale, w_branch_ssm, w_branch_pool, w_out, norm_post_mix, norm_pre_ffn, w_up, ffn_conv_w, ffn_conv_b, w_down, norm_post_ffn, loss_target, m_norm_pre_mix, m_w_in, m_ssm_lambda_re, m_ssm_lambda_im, m_ssm_log_step, m_ssm_b_re, m_ssm_b_im, m_ssm_c_re, m_ssm_c_im, m_ssm_d, m_ssm_glu_w, m_ssm_glu_b, m_pool_w, m_pool_b, m_pool_scale, m_w_branch_ssm, m_w_branch_pool, m_w_out, m_norm_post_mix, m_norm_pre_ffn, m_w_up, m_ffn_conv_w, m_ffn_conv_b, m_w_down, m_norm_post_ffn, v_norm_pre_mix, v_w_in, v_ssm_lambda_re, v_ssm_lambda_im, v_ssm_log_step, v_ssm_b_re, v_ssm_b_im, v_ssm_c_re, v_ssm_c_im, v_ssm_d, v_ssm_glu_w, v_ssm_glu_b, v_pool_w, v_pool_b, v_pool_scale, v_w_branch_ssm, v_w_branch_pool, v_w_out, v_norm_post_mix, v_norm_pre_ffn, v_w_up, v_ffn_conv_w, v_ffn_conv_b, v_w_down, v_norm_post_ffn):
    args = dict(locals())
    names = ["norm_pre_mix", "w_in", "ssm_lambda_re", "ssm_lambda_im", "ssm_log_step", "ssm_b_re", "ssm_b_im",
             "ssm_c_re", "ssm_c_im", "ssm_d", "ssm_glu_w", "ssm_glu_b", "pool_w", "pool_b", "pool_scale",
             "w_branch_ssm", "w_branch_pool", "w_out", "norm_post_mix", "norm_pre_ffn", "w_up", "ffn_conv_w",
             "ffn_conv_b", "w_down", "norm_post_ffn"]

    nseq, L, D = x.shape
    N = nseq * L
    U = D // NDEV
    DS = ssm_d.shape[1]
    DP = pool_scale.shape[1]
    G, P, J = ssm_b_re.shape[1:]
    SL = G * P
    CH = GROUPS_PER_CHUNK * J
    CS = GROUPS_PER_CHUNK * P
    NCH = DS // CH
    NPG = len(POOL_WINDOWS)
    PG = DP // NPG
    FC = w_up.shape[2]
    NB = NDEV
    HB = NB // 2
    F2 = NB * FC
    dev = _slot(_mesh_pos())
    tm = _tile(N, 1024)
    tm2 = _tile(N, 512)

    x2 = x.reshape(N, D)
    tgt = loss_target.reshape(N, D)

    def bf(t):
        return t.astype(BF16)

    def g_start(tag, group, after):
        return _split_start("gather_start_" + tag, _gather_copies, 3, group,
                            [_sds((NDEV,) + s.shape, s.dtype) for s in group], after)

    def g_finish(tag, started, after):
        srcs, lands = _split_wait("gather_wait_" + tag, _gather_copies, started, after)
        return _gather_d2d("comm_gather_d2d_" + tag, srcs, lands)

    def x_start(tag, group):
        return _split_start("exchange_start_" + tag, _exchange_copies, NDEV - 1, group,
                            [_sds(g.shape, g.dtype) for g in group], group[0])

    def x_finish(tag, started, after):
        srcs, lands = _split_wait("exchange_wait_" + tag, _exchange_copies, started, after)
        own = [lax.dynamic_index_in_dim(s, dev, 0, keepdims=False) for s in srcs]
        return [lax.dynamic_update_index_in_dim(l, o, dev, 0) for l, o in zip(lands, own)]

    st_in = g_start("in", [bf(w_in[0])], x2)
    st_mix = g_start("mix", [bf(ssm_glu_w[0]), bf(pool_w[0]), pool_b[0], ffn_conv_w[0]], st_in[4])
    (Win,) = g_finish("in", st_in, st_mix[4])
    st_br = g_start("branch", [bf(w_branch_ssm[0]), bf(w_branch_pool[0]), bf(w_out[0])], Win)
    st_up = g_start("up", [bf(w_up[0])], st_br[4])
    conv_b_blk = ffn_conv_b.reshape(NB, 1, FC)

    lam_re, lam_im = ssm_lambda_re[0], ssm_lambda_im[0]
    log_step = ssm_log_step.reshape(G, 1)
    br_t = ssm_b_re[0].transpose(2, 0, 1)
    bi_t = ssm_b_im[0].transpose(2, 0, 1)
    pw_re3, pw_im3, bb_re, bb_im = _ssm_param_fwd(lam_re, lam_im, log_step, br_t, bi_t)
    pw_re, pw_im = pw_re3.reshape(SUBLANES, SL), pw_im3.reshape(SUBLANES, SL)
    pwf_re, pwf_im = pw_re[::-1], pw_im[::-1]
    WB = jnp.concatenate([_block_diag_in(bb_re, NCH), _block_diag_in(bb_im, NCH)], axis=2).astype(BF16)
    WCre = _block_diag_out(ssm_c_re[0], NCH).astype(BF16)
    WCim = _block_diag_out(-ssm_c_im[0], NCH).astype(BF16)

    a = _pre_norm(x2, norm_pre_mix)
    nq = 3 * NDEV
    (proj,) = _fused_matmul(
        "in_proj", (N // tm, nq, 1),
        [(a, BS((tm, D), lambda i, q, k: (i, 0)), Win, BS((None, D, U), lambda i, q, k: (q // 3, 0, q % 3)), "nn", 0)],
        [(tm, U)], [], [(_sds((N, 3 * D), F32), BS((tm, U), lambda i, q, k: (i, q)))],
        _store(lambda accs: accs), deps=[st_up[4]])

    bu_re, bu_im = _fused_matmul(
        "ssm_in", (N // tm2, NCH, 1),
        [(proj, BS((tm2, CH), lambda i, c, k: (i, c)), WB, BS((None, CH, 2 * CS), lambda i, c, k: (c, 0, 0)), "nn", 0)],
        [(tm2, 2 * CS)], [],
        [(_sds((N, SL), F32), BS((tm2, CS), lambda i, c, k: (i, c)))] * 2,
        _store(lambda accs: (accs[0][:, :CS], accs[0][:, CS:])))
    s_re, s_im = _scan_fwd(bu_re, bu_im, pw_re, pw_im, nseq, L)

    def ssm_out_epi(ids, accs, ex, o):
        u_ref, d_ref = ex
        y0 = accs[0] + d_ref[...] * u_ref[...]
        o[0][...] = y0
        o[1][...] = _gelu(y0).astype(BF16)

    y0, y1 = _fused_matmul(
        "ssm_out", (N // tm2, NCH, 1),
        [(s_re, BS((tm2, CS), lambda i, c, k: (i, c)), WCre, BS((None, CS, CH), lambda i, c, k: (c, 0, 0)), "nn", 0),
         (s_im, BS((tm2, CS), lambda i, c, k: (i, c)), WCim, BS((None, CS, CH), lambda i, c, k: (c, 0, 0)), "nn", 0)],
        [(tm2, CH)],
        [(proj, BS((tm2, CH), lambda i, c, k: (i, c))), (ssm_d, BS((1, CH), lambda i, c, k: (0, c)))],
        [(_sds((N, DS), F32), BS((tm2, CH), lambda i, c, k: (i, c))),
         (_sds((N, DS), BF16), BS((tm2, CH), lambda i, c, k: (i, c)))],
        ssm_out_epi)

    Wglu, Wpool, pool_b_all, conv_w_all = g_finish("mix", st_mix, y1)
    st_down = g_start("down", [bf(w_down[0])], Wglu)
    Wglu = Wglu.reshape(DS, DS)
    Wpool = Wpool.transpose(1, 0, 2, 3).reshape(NPG, PG, PG)
    pool_b_full = pool_b_all.transpose(1, 0, 2).reshape(1, DP)
    tn_s = _tile(DS, 512)

    def glu_epi(ids, accs, ex, o):
        y0_ref, b_ref = ex
        zg = accs[0] + b_ref[...]
        o[0][...] = zg
        o[1][...] = (_gelu(y0_ref[...]) * _sigmoid(zg)).astype(BF16)

    zg, ys = _fused_matmul(
        "ssm_glu", (N // tm, DS // tn_s, 1),
        [(y1, BS((tm, DS), lambda i, j, k: (i, 0)), Wglu, BS((DS, tn_s), lambda i, j, k: (0, j)), "nn", 0)],
        [(tm, tn_s)],
        [(y0, BS((tm, tn_s), lambda i, j, k: (i, j))), (ssm_glu_b, BS((1, tn_s), lambda i, j, k: (0, j)))],
        [(_sds((N, DS), F32), BS((tm, tn_s), lambda i, j, k: (i, j))),
         (_sds((N, DS), BF16), BS((tm, tn_s), lambda i, j, k: (i, j)))],
        glu_epi, deps=[st_down[4]])

    z = _pool_fwd(proj, DS, DP, nseq, L)

    def pool_mm_epi(ids, accs, ex, o):
        b_ref, sc_ref = ex
        q = accs[0] + b_ref[...]
        o[0][...] = q
        o[1][...] = (q * sc_ref[...]).astype(BF16)

    qp, yp = _fused_matmul(
        "pool_mm", (N // tm, NPG, 1),
        [(z, BS((tm, PG), lambda i, g, k: (i, g)), Wpool, BS((None, PG, PG), lambda i, g, k: (g, 0, 0)), "nn", 0)],
        [(tm, PG)],
        [(pool_b_full, BS((1, PG), lambda i, g, k: (0, g))), (pool_scale, BS((1, PG), lambda i, g, k: (0, g)))],
        [(_sds((N, DP), F32), BS((tm, PG), lambda i, g, k: (i, g))),
         (_sds((N, DP), BF16), BS((tm, PG), lambda i, g, k: (i, g)))],
        pool_mm_epi)

    Wbs, Wbp, Wout = g_finish("branch", st_br, yp)
    Wout = Wout.reshape(D, D)
    gs_blk = BS((tm2, U), lambda i, q, k: (i, (DS + DP) // U + q))
    gp_blk = BS((tm2, U), lambda i, q, k: (i, (DS + DP + D) // U + q))
    out_blk = BS((tm2, U), lambda i, q, k: (i, q))

    def branch_epi(ids, accs, ex, o):
        gs_ref, gp_ref = ex
        o[0][...] = accs[0]
        o[1][...] = accs[1]
        o[2][...] = (_sigmoid(gs_ref[...]) * accs[0] + _sigmoid(gp_ref[...]) * accs[1]).astype(BF16)

    Ys, Yp, merged = _fused_matmul(
        "branch", (N // tm2, NDEV, 1),
        [(ys, BS((tm2, DS), lambda i, q, k: (i, 0)), Wbs, BS((None, DS, U), lambda i, q, k: (q, 0, 0)), "nn", 0),
         (yp, BS((tm2, DP), lambda i, q, k: (i, 0)), Wbp, BS((None, DP, U), lambda i, q, k: (q, 0, 0)), "nn", 1)],
        [(tm2, U), (tm2, U)],
        [(proj, gs_blk), (proj, gp_blk)],
        [(_sds((N, D), F32), out_blk), (_sds((N, D), F32), out_blk), (_sds((N, D), BF16), out_blk)],
        branch_epi)

    tn_d = _tile(D, 512)
    (o_mix,) = _fused_matmul(
        "out_proj", (N // tm, D // tn_d, 1),
        [(merged, BS((tm, D), lambda i, j, k: (i, 0)), Wout, BS((D, tn_d), lambda i, j, k: (0, j)), "nn", 0)],
        [(tm, tn_d)], [], [(_sds((N, D), F32), BS((tm, tn_d), lambda i, j, k: (i, j)))],
        _store(lambda accs: accs))
    h1, c = _mid_norm(x2, o_mix, norm_post_mix, norm_pre_ffn)

    (Wup,) = g_finish("up", st_up, c)
    tk_d = _tile(D, 1024)
    tk_up = _tile(D, 2048)
    (up_pre,) = _fused_matmul(
        "ffn_up", (N // tm2, NB, D // tk_up),
        [(c, BS((tm2, tk_up), lambda i, j, k: (i, k)), Wup, BS((None, tk_up, FC), lambda i, j, k: (j, k, 0)), "nn", 0)],
        [(tm2, FC)], [], [(_sds((NB, N, FC), F32), BS((None, tm2, FC), lambda i, j, k: (j, i, 0)))],
        _store(lambda accs: accs))
    f = _gate_fwd(up_pre, conv_w_all, conv_b_blk, L)
    (Wdown,) = g_finish("down", st_down, f)
    Wdown = Wdown.reshape(HB, FC, D)
    tn_d2 = _tile(D, 1024)
    (dn,) = _fused_matmul(
        "ffn_down", (N // tm2, D // tn_d2, HB),
        [(f, BS((None, tm2, FC), lambda i, j, k: (k, i, 0)), Wdown, BS((None, FC, tn_d2), lambda i, j, k: (k, 0, j)), "nn", 0)],
        [(tm2, tn_d2)], [], [(_sds((N, D), F32), BS((tm2, tn_d2), lambda i, j, k: (i, j)))],
        _store(lambda accs: accs))
    dh2, d_dn, lossv, dg4 = _post_ffn(h1, dn, tgt, norm_post_ffn)

    (df,) = _fused_matmul(
        "ffn_down_dx", (N // tm2, HB, D // tk_d),
        [(d_dn, BS((tm2, tk_d), lambda i, j, k: (i, k)), Wdown, BS((None, FC, tk_d), lambda i, j, k: (j, 0, k)), "nt", 0)],
        [(tm2, FC)], [], [(_sds((HB, N, FC), BF16), BS((None, tm2, FC), lambda i, j, k: (j, i, 0)))],
        _store(lambda accs: accs))
    tk_n = _tile(N, 1024)
    (gW_down,) = _fused_matmul(
        "ffn_down_dw", (HB, D // tn_d, N // tk_n),
        [(f, BS((None, tk_n, FC), lambda j, n, k: (j, k, 0)), d_dn, BS((tk_n, tn_d), lambda j, n, k: (k, n)), "tn", 0)],
        [(FC, tn_d)], [], [(_sds((HB, FC, D), BF16), BS((None, FC, tn_d), lambda j, n, k: (j, 0, n)))],
        _store(lambda accs: accs))
    x_down = x_start("down", [gW_down.reshape(NDEV, FC // 2, D)])
    dup, dcw, dcb = _gate_bwd(up_pre, conv_w_all, conv_b_blk, df, L, deps=[x_down[4]])
    dpre = _conv_bwd(dup.reshape(NB, N, FC), conv_w_all, L)
    (dc,) = _fused_matmul(
        "ffn_up_dx", (N // tm2, D // tn_d2, NB),
        [(dpre, BS((None, tm2, FC), lambda i, j, k: (k, i, 0)), Wup, BS((None, tn_d2, FC), lambda i, j, k: (k, j, 0)), "nt", 0)],
        [(tm2, tn_d2)], [], [(_sds((N, D), F32), BS((tm2, tn_d2), lambda i, j, k: (i, j)))],
        _store(lambda accs: accs))
    tm_d = _tile(D, 512)
    (gW_up,) = _fused_matmul(
        "ffn_up_dw", (NB, D // tm_d, N // tk_n),
        [(c, BS((tk_n, tm_d), lambda j, n, k: (k, n)), dpre, BS((None, tk_n, FC), lambda j, n, k: (j, k, 0)), "tn", 0)],
        [(tm_d, FC)], [], [(_sds((NB, D, FC), BF16), BS((None, tm_d, FC), lambda j, n, k: (j, n, 0)))],
        _store(lambda accs: accs))
    x_up = x_start("up", [gW_up])

    dh1, d_o, dg2, dg3 = _mid_bwd(dh2, dc, h1, o_mix, norm_post_mix, norm_pre_ffn, deps=[x_up[4]])

    def dmerged_epi(ids, accs, ex, o):
        gs_ref, gp_ref, ys_ref, yp_ref = ex
        dm = accs[0]
        sg_s, sg_p = _sigmoid(gs_ref[...]), _sigmoid(gp_ref[...])
        o[0][...] = (dm * sg_s).astype(BF16)
        o[1][...] = (dm * sg_p).astype(BF16)
        o[2][...] = (dm * ys_ref[...] * sg_s * (1.0 - sg_s)).astype(BF16)
        o[3][...] = (dm * yp_ref[...] * sg_p * (1.0 - sg_p)).astype(BF16)

    dYs, dYp, dgs, dgp = _fused_matmul(
        "out_proj_dx", (N // tm2, NDEV, 1),
        [(d_o, BS((tm2, D), lambda i, q, k: (i, 0)), Wout, BS((U, D), lambda i, q, k: (q, 0)), "nt", 0)],
        [(tm2, U)],
        [(proj, gs_blk), (proj, gp_blk), (Ys, out_blk), (Yp, out_blk)],
        [(_sds((N, D), BF16), out_blk)] * 4,
        dmerged_epi)
    (gW_out,) = _fused_matmul(
        "out_proj_dw", (D // tm_d, D // tn_d, 1),
        [(merged, BS((N, tm_d), lambda i, j, k: (0, i)), d_o, BS((N, tn_d), lambda i, j, k: (0, j)), "tn", 0)],
        [(tm_d, tn_d)], [], [(_sds((D, D), BF16), BS((tm_d, tn_d), lambda i, j, k: (i, j)))],
        _store(lambda accs: accs))
    tm_s = _tile(DS, 512)
    gW_bs, gW_bp = _fused_matmul(
        "branch_dw", (DS // tm_s, NDEV, 1),
        [(ys, BS((N, tm_s), lambda i, q, k: (0, i)), dYs, BS((N, U), lambda i, q, k: (0, q)), "tn", 0),
         (yp, BS((N, tm_s), lambda i, q, k: (0, i)), dYp, BS((N, U), lambda i, q, k: (0, q)), "tn", 1)],
        [(tm_s, U), (tm_s, U)], [],
        [(_sds((NDEV, DS, U), BF16), BS((None, tm_s, U), lambda i, q, k: (q, i, 0)))] * 2,
        _store(lambda accs: accs))
    x_br = x_start("branch", [gW_bs, gW_bp, gW_out.reshape(NDEV, U, D)])

    tn_p = _tile(PG, 512)

    def dyp_epi(ids, accs, ex, o):
        q_ref, sc_ref = ex
        first = ids[1] == 0
        dyp = accs[0]
        dq = dyp * sc_ref[...]
        o[0][...] = dq.astype(BF16)
        _rowsum_into(o[1], first, dyp * q_ref[...])
        _rowsum_into(o[2], first, dq)

    dq, d_pscale, d_pb = _fused_matmul(
        "branch_pool_dx", (DP // tn_p, N // tm, NDEV),
        [(dYp, BS((tm, U), lambda j, i, k: (i, k)), Wbp, BS((None, tn_p, U), lambda j, i, k: (k, j, 0)), "nt", 0)],
        [(tm, tn_p)],
        [(qp, BS((tm, tn_p), lambda j, i, k: (i, j))), (pool_scale, BS((1, tn_p), lambda j, i, k: (0, j)))],
        [(_sds((N, DP), BF16), BS((tm, tn_p), lambda j, i, k: (i, j))),
         (_sds((1, DP), F32), BS((1, tn_p), lambda j, i, k: (0, j))),
         (_sds((1, DP), F32), BS((1, tn_p), lambda j, i, k: (0, j)))],
        dyp_epi, deps=[x_br[4]])
    (dz,) = _fused_matmul(
        "pool_mm_dx", (N // tm, NPG, 1),
        [(dq, BS((tm, PG), lambda i, g, k: (i, g)), Wpool, BS((None, PG, PG), lambda i, g, k: (g, 0, 0)), "nt", 0)],
        [(tm, PG)], [], [(_sds((N, DP), F32), BS((tm, PG), lambda i, g, k: (i, g)))],
        _store(lambda accs: accs))
    (gW_pool,) = _fused_matmul(
        "pool_mm_dw", (NPG, 1),
        [(z, BS((N, PG), lambda g, k: (0, g)), dq, BS((N, PG), lambda g, k: (0, g)), "tn", 0)],
        [(PG, PG)], [], [(_sds((NPG, PG, PG), BF16), BS((None, PG, PG), lambda g, k: (g, 0, 0)))],
        _store(lambda accs: accs))
    du_pool = _pool_bwd(dz, nseq, L)

    def dys_epi(ids, accs, ex, o):
        zg_ref, y0_ref = ex
        first = ids[1] == 0
        dys = accs[0]
        sg = _sigmoid(zg_ref[...])
        dzg = dys * _gelu(y0_ref[...]) * sg * (1.0 - sg)
        o[0][...] = dzg.astype(BF16)
        o[1][...] = dys * sg
        _rowsum_into(o[2], first, dzg)

    dzg, dy1_direct, d_glu_b = _fused_matmul(
        "branch_ssm_dx", (DS // tn_s, N // tm, NDEV),
        [(dYs, BS((tm, U), lambda j, i, k: (i, k)), Wbs, BS((None, tn_s, U), lambda j, i, k: (k, j, 0)), "nt", 0)],
        [(tm, tn_s)],
        [(zg, BS((tm, tn_s), lambda j, i, k: (i, j))), (y0, BS((tm, tn_s), lambda j, i, k: (i, j)))],
        [(_sds((N, DS), BF16), BS((tm, tn_s), lambda j, i, k: (i, j))),
         (_sds((N, DS), F32), BS((tm, tn_s), lambda j, i, k: (i, j))),
         (_sds((1, DS), F32), BS((1, tn_s), lambda j, i, k: (0, j)))],
        dys_epi)
    (gW_glu,) = _fused_matmul(
        "ssm_glu_dw", (DS // tm_s, DS // tn_s, 1),
        [(y1, BS((N, tm_s), lambda i, j, k: (0, i)), dzg, BS((N, tn_s), lambda i, j, k: (0, j)), "tn", 0)],
        [(tm_s, tn_s)], [], [(_sds((DS, DS), BF16), BS((tm_s, tn_s), lambda i, j, k: (i, j)))],
        _store(lambda accs: accs))
    x_mix = x_start("mix", [gW_glu.reshape(NDEV, DS // NDEV, DS),
                            gW_pool.reshape(NPG, NDEV, PG // NDEV, PG).transpose(1, 0, 2, 3)])

    tn_c = _tile(DS, CH)

    def dy0_epi(ids, accs, ex, o):
        d1_ref, y0_ref, u_ref = ex
        first = ids[1] == 0
        dy0 = (accs[0] + d1_ref[...]) * _gelu_grad(y0_ref[...])
        o[0][...] = dy0
        _rowsum_into(o[1], first, dy0 * u_ref[...])

    dy0, d_ssm_d = _fused_matmul(
        "ssm_glu_dx", (DS // tn_c, N // tm, 1),
        [(dzg, BS((tm, DS), lambda j, i, k: (i, 0)), Wglu, BS((tn_c, DS), lambda j, i, k: (j, 0)), "nt", 0)],
        [(tm, tn_c)],
        [(dy1_direct, BS((tm, tn_c), lambda j, i, k: (i, j))), (y0, BS((tm, tn_c), lambda j, i, k: (i, j))),
         (proj, BS((tm, tn_c), lambda j, i, k: (i, j)))],
        [(_sds((N, DS), F32), BS((tm, tn_c), lambda j, i, k: (i, j))),
         (_sds((1, DS), F32), BS((1, tn_c), lambda j, i, k: (0, j)))],
        dy0_epi, deps=[x_mix[4]])

    ds_re, ds_im = _fused_matmul(
        "ssm_out_dx", (N // tm2, NCH, 1),
        [(dy0, BS((tm2, CH), lambda i, c, k: (i, c)), WCre, BS((None, CS, CH), lambda i, c, k: (c, 0, 0)), "nt", 0),
         (dy0, BS((tm2, CH), lambda i, c, k: (i, c)), WCim, BS((None, CS, CH), lambda i, c, k: (c, 0, 0)), "nt", 1)],
        [(tm2, CS), (tm2, CS)], [],
        [(_sds((N, SL), F32), BS((tm2, CS), lambda i, c, k: (i, c)))] * 2,
        _store(lambda accs: accs))
    dWCre, dWCim = _fused_matmul(
        "ssm_out_dw", (NCH, N // tk_n),
        [(s_re, BS((tk_n, CS), lambda c, k: (k, c)), dy0, BS((tk_n, CH), lambda c, k: (k, c)), "tn", 0),
         (s_im, BS((tk_n, CS), lambda c, k: (k, c)), dy0, BS((tk_n, CH), lambda c, k: (k, c)), "tn", 1)],
        [(CS, CH), (CS, CH)], [],
        [(_sds((NCH, CS, CH), F32), BS((None, CS, CH), lambda c, k: (c, 0, 0)))] * 2,
        _store(lambda accs: accs))
    lam_r, lam_i, d_ab_re, d_ab_im = _scan_bwd(ds_re, ds_im, s_re, s_im, pw_re, pw_im, pwf_re, pwf_im, nseq, L)

    def du_epi(ids, accs, ex, o):
        dy0_ref, d_ref = ex
        o[0][...] = (accs[0] + dy0_ref[...] * d_ref[...]).astype(BF16)

    (du_ssm,) = _fused_matmul(
        "ssm_in_dx", (N // tm2, NCH, 1),
        [(lam_r, BS((tm2, CS), lambda i, c, k: (i, c)), WB, BS((None, CH, CS), lambda i, c, k: (c, 0, 0)), "nt", 0),
         (lam_i, BS((tm2, CS), lambda i, c, k: (i, c)), WB, BS((None, CH, CS), lambda i, c, k: (c, 0, 1)), "nt", 0)],
        [(tm2, CH)],
        [(dy0, BS((tm2, CH), lambda i, c, k: (i, c))), (ssm_d, BS((1, CH), lambda i, c, k: (0, c)))],
        [(_sds((N, DS), BF16), BS((tm2, CH), lambda i, c, k: (i, c)))],
        du_epi)
    dWBre, dWBim = _fused_matmul(
        "ssm_in_dw", (NCH, N // tk_n),
        [(proj, BS((tk_n, CH), lambda c, k: (k, c)), lam_r, BS((tk_n, CS), lambda c, k: (k, c)), "tn", 0),
         (proj, BS((tk_n, CH), lambda c, k: (k, c)), lam_i, BS((tk_n, CS), lambda c, k: (k, c)), "tn", 1)],
        [(CH, CS), (CH, CS)], [],
        [(_sds((NCH, CH, CS), F32), BS((None, CH, CS), lambda c, k: (c, 0, 0)))] * 2,
        _store(lambda accs: accs))
    d_bbr = _block_diag_in_grad(dWBre, J, G, P)
    d_bbi = _block_diag_in_grad(dWBim, J, G, P)
    d_lam_re, d_lam_im, d_log_step, d_br_t, d_bi_t = _ssm_param_bwd(
        lam_re, lam_im, log_step, br_t, bi_t,
        d_ab_re.reshape(nseq * SUBLANES, G, P), d_ab_im.reshape(nseq * SUBLANES, G, P), d_bbr, d_bbi)
    d_c_re = _block_diag_out_grad(dWCre, G, J, P)
    d_c_im = -_block_diag_out_grad(dWCim, G, J, P)

    dproj = jnp.concatenate([du_ssm, du_pool, dgs, dgp], axis=1)
    (gW_in,) = _fused_matmul(
        "in_proj_dw", (D // tm_d, nq, 1),
        [(a, BS((N, tm_d), lambda i, q, k: (0, i)), dproj, BS((N, U), lambda i, q, k: (0, q)), "tn", 0)],
        [(tm_d, U)], [], [(_sds((NDEV, D, 3 * U), BF16), BS((None, tm_d, U), lambda i, q, k: (q // 3, i, q % 3)))],
        _store(lambda accs: accs))
    x_in = x_start("in", [gW_in])
    (da,) = _fused_matmul(
        "in_proj_dx", (N // tm, D // tn_d2, NDEV),
        [(dproj, BS((tm, 3 * U), lambda i, j, k: (i, k)), Win, BS((None, tn_d2, 3 * U), lambda i, j, k: (k, j, 0)), "nt", 0)],
        [(tm, tn_d2)], [], [(_sds((N, D), F32), BS((tm, tn_d2), lambda i, j, k: (i, j)))],
        _store(lambda accs: accs), deps=[x_in[4]])
    grad_x, dg1 = _pre_bwd(x2, da, dh1, norm_pre_mix)

    d_conv_w = dcw.reshape(NB, 3, FC).transpose(1, 0, 2).reshape(3, F2)
    d_conv_b = dcb.reshape(1, F2)
    small = {
        "norm_pre_mix": dg1, "norm_post_mix": dg2, "norm_pre_ffn": dg3, "norm_post_ffn": dg4,
        "ssm_lambda_re": d_lam_re[None], "ssm_lambda_im": d_lam_im[None], "ssm_log_step": d_log_step.reshape(1, G),
        "ssm_b_re": d_br_t.transpose(1, 2, 0)[None], "ssm_b_im": d_bi_t.transpose(1, 2, 0)[None],
        "ssm_c_re": d_c_re[None], "ssm_c_im": d_c_im[None],
        "ssm_d": d_ssm_d, "ssm_glu_b": d_glu_b, "pool_scale": d_pscale,
        "pool_b": d_pb.reshape(1, NPG, PG), "ffn_conv_w": d_conv_w[None], "ffn_conv_b": d_conv_b,
    }
    small_names = list(small)
    packed = _pack([lossv] + [small[n] for n in small_names])
    st_small = _split_start("small_start", _broadcast_copies, NDEV - 1, [packed],
                            [_sds((NDEV,) + packed.shape, packed.dtype)], packed)

    res = {}
    after = st_small[4]
    for tag, started, group in (("down", x_down, ["w_down"]), ("up", x_up, ["w_up"]),
                                ("branch", x_br, ["w_branch_ssm", "w_branch_pool", "w_out"]),
                                ("mix", x_mix, ["ssm_glu_w", "pool_w"]), ("in", x_in, ["w_in"])):
        for n, parts in zip(group, x_finish(tag, started, after)):
            shape = args[n].shape
            cols = shape[-1]
            flat = lambda t: t.reshape(-1, cols)
            g, dl, nm, nv = _adamw("adamw_" + n, flat(args[n]), flat(args["m_" + n]), flat(args["v_" + n]),
                                   parts.reshape(NDEV, -1, cols))
            res[n] = tuple(t.reshape(shape) for t in (g, dl, nm, nv))
            after = g

    srcs, lands = _split_wait("small_wait", _broadcast_copies, st_small, after)
    small_all = lax.dynamic_update_index_in_dim(lands[0], srcs[0], dev, 0)
    loss_rows = (D + SUBLANES * LANES - 1) // (SUBLANES * LANES) * SUBLANES
    total, loss = _small_sum(small_all, loss_rows, D)
    totals = dict(zip(small_names, _unpack(total, [lossv.shape] + [small[n].shape for n in small_names])[1:]))
    totals["pool_b"] = lax.dynamic_slice_in_dim(totals["pool_b"], dev * (PG // NDEV), PG // NDEV, axis=2)
    totals["ffn_conv_w"] = lax.dynamic_slice_in_dim(totals["ffn_conv_w"], dev * FC, FC, axis=2)
    sm_g = _pack([totals[n] for n in small_names])
    sm_w, sm_m, sm_v = (_pack([args[p + n] for n in small_names]) for p in ("", "m_", "v_"))
    _, sm_d, sm_nm, sm_nv = _adamw("adamw_small", sm_w, sm_m, sm_v, sm_g[None])
    shapes = [args[n].shape for n in small_names]
    for n, dl, nm, nv in zip(small_names, _unpack(sm_d, shapes), _unpack(sm_nm, shapes), _unpack(sm_nv, shapes)):
        res[n] = (totals[n], dl, nm, nv)

    outs = [loss.reshape(()), grad_x.reshape(x.shape)]
    for k in range(4):
        outs += [res[n][k] for n in names]
    return tuple(outs)
```

```python
import functools
import math

import jax
import jax.numpy as jnp
from jax import lax
from jax.experimental import pallas as pl
from jax.experimental.pallas import tpu as pltpu

F32 = jnp.float32
BF16 = jnp.bfloat16
BS = pl.BlockSpec

NDEV = 8
SSM_GROUP = 16
SSM_STATE = 64
GROUPS_PER_CHUNK = 16
POOL_WINDOWS = (2, 4, 8, 16)
EPS = 1e-6
MIN_NEG_REAL = -1e-4
ADAM_LR, ADAM_B1, ADAM_B2, ADAM_EPS, ADAM_WD, ADAM_STEP = 0.001, 0.9, 0.999, 1e-08, 0.01, 10
LANES = 128
SUBLANES = 8
VMEM_LIMIT = 56 * 1024 * 1024

_DIMS = {"nn": (((1,), (0,)), ((), ())), "nt": (((1,), (1,)), ((), ())), "tn": (((0,), (0,)), ((), ()))}


def _tile(dim, pref, mult=LANES):
    if dim <= pref:
        return dim
    t = (pref // mult) * mult
    while t >= mult:
        if dim % t == 0:
            return t
        t -= mult
    return dim


def _pc(name, body, grid, ins, in_specs, outs, out_specs, scratch=(), deps=()):
    multi = isinstance(outs, (list, tuple))
    if deps:
        n_in, n_dep, inner = len(ins), len(deps), body

        def body(*refs):
            return inner(*refs[:n_in], *refs[n_in + n_dep:])

        ins = list(ins) + list(deps)
        in_specs = list(in_specs) + [BS(memory_space=pl.ANY)] * n_dep
    return pl.pallas_call(
        body, name=name, grid=grid, in_specs=list(in_specs),
        out_specs=list(out_specs) if multi else out_specs,
        out_shape=list(outs) if multi else outs, scratch_shapes=list(scratch),
        compiler_params=pltpu.CompilerParams(dimension_semantics=("arbitrary",) * len(grid),
                                             vmem_limit_bytes=VMEM_LIMIT),
    )(*ins)


def _sds(shape, dtype):
    return jax.ShapeDtypeStruct(tuple(shape), dtype)


def _gelu(x):
    k = math.sqrt(2.0 / math.pi)
    return 0.5 * x * (1.0 + jnp.tanh(k * (x + 0.044715 * (x * x * x))))


def _gelu_grad(x):
    k = math.sqrt(2.0 / math.pi)
    t = jnp.tanh(k * (x + 0.044715 * (x * x * x)))
    return 0.5 * (1.0 + t) + 0.5 * x * (1.0 - t * t) * (k * (1.0 + 3.0 * 0.044715 * x * x))


def _sigmoid(x):
    return jax.nn.sigmoid(x)


def _fused_matmul(name, grid, pairs, acc_shapes, extras, outs, epilogue, deps=()):
    n_p, n_e, n_o = len(pairs), len(extras), len(outs)
    rank = len(grid)
    nk = grid[-1]

    def body(*refs):
        ab = refs[:2 * n_p]
        ex = refs[2 * n_p:2 * n_p + n_e]
        o = refs[2 * n_p + n_e:2 * n_p + n_e + n_o]
        accs = refs[2 * n_p + n_e + n_o:]
        ids = [pl.program_id(d) for d in range(rank)]
        k = ids[-1]

        @pl.when(k == 0)
        def _():
            for acc in accs:
                acc[...] = jnp.zeros_like(acc)

        for p in range(n_p):
            a = ab[2 * p][...].astype(BF16)
            b = ab[2 * p + 1][...].astype(BF16)
            accs[pairs[p][5]][...] += lax.dot_general(a, b, _DIMS[pairs[p][4]], preferred_element_type=F32)

        @pl.when(k == nk - 1)
        def _():
            epilogue(ids, [acc[...] for acc in accs], ex, o)

    ins, in_specs = [], []
    for a, a_spec, b, b_spec, _, _ in pairs:
        ins += [a, b]
        in_specs += [a_spec, b_spec]
    for e, e_spec in extras:
        ins.append(e)
        in_specs.append(e_spec)
    res = _pc(name, body, grid, ins, in_specs, [s for s, _ in outs], [sp for _, sp in outs],
              scratch=[pltpu.VMEM(tuple(s), F32) for s in acc_shapes], deps=deps)
    return res


def _store(vals):
    def epilogue(ids, accs, ex, o):
        for r, v in zip(o, vals(accs)):
            r[...] = v.astype(r.dtype)
    return epilogue


def _rowsum_into(ref, first, v):
    s = jnp.sum(v, axis=0, keepdims=True)

    @pl.when(first)
    def _():
        ref[...] = s

    @pl.when(jnp.logical_not(first))
    def _():
        ref[...] += s


def _mesh_pos():
    return lax.axis_index("x"), lax.axis_index("y"), lax.axis_index("c")


def _slot(p):
    return 4 * p[0] + 2 * p[1] + p[2]


_HBM = BS(memory_space=pltpu.HBM)
_SEM = BS(memory_space=pltpu.SEMAPHORE)
_ANY = BS(memory_space=pl.ANY)
_EFFECT = pltpu.SideEffectType.DATAFLOW_SIDE_EFFECTING


def _other_chips(x, y):
    return [(1 - x, y), (x, 1 - y), (1 - x, 1 - y)]


def _all_peers(x, y, c):
    peers = []
    for k in range(1, NDEV):
        kx, ky, kc = (k >> 2) & 1, (k >> 1) & 1, k & 1
        peers.append((1 - x if kx else x, 1 - y if ky else y, 1 - c if kc else c))
    return peers


def _gather_copies(src, land, send_sems, recv_sems, base):
    x, y, c = _mesh_pos()
    return [pltpu.make_async_remote_copy(
        src_ref=src, dst_ref=land.at[_slot((x, y, c))],
        send_sem=send_sems.at[base + k], recv_sem=recv_sems.at[base + k],
        device_id=(*chip, c), device_id_type=pl.DeviceIdType.MESH) for k, chip in enumerate(_other_chips(x, y))]


def _d2d_copies(src, land, send_sems, recv_sems, base):
    x, y, c = _mesh_pos()
    blocks = [(x, y, c)] + [(*chip, c) for chip in _other_chips(x, y)]
    return [pltpu.make_async_remote_copy(
        src_ref=src if k == 0 else land.at[_slot(b)], dst_ref=land.at[_slot(b)],
        send_sem=send_sems.at[base + k], recv_sem=recv_sems.at[base + k],
        device_id=(x, y, 1 - c), device_id_type=pl.DeviceIdType.MESH) for k, b in enumerate(blocks)]


def _broadcast_copies(src, land, send_sems, recv_sems, base):
    x, y, c = _mesh_pos()
    return [pltpu.make_async_remote_copy(
        src_ref=src, dst_ref=land.at[_slot((x, y, c))],
        send_sem=send_sems.at[base + k], recv_sem=recv_sems.at[base + k],
        device_id=peer, device_id_type=pl.DeviceIdType.MESH) for k, peer in enumerate(_all_peers(x, y, c))]


def _exchange_copies(src, land, send_sems, recv_sems, base):
    x, y, c = _mesh_pos()
    return [pltpu.make_async_remote_copy(
        src_ref=src.at[_slot(peer)], dst_ref=land.at[_slot((x, y, c))],
        send_sem=send_sems.at[base + k], recv_sem=recv_sems.at[base + k],
        device_id=peer, device_id_type=pl.DeviceIdType.MESH) for k, peer in enumerate(_all_peers(x, y, c))]


def _split_start(name, copies, ncopy, srcs, land_shapes, after):
    n = len(srcs)

    def body(*refs):
        src_refs, land_refs = refs[:n], refs[n:2 * n]
        send_sems, recv_sems = refs[2 * n + 1], refs[2 * n + 2]
        token = refs[-1]
        for r in range(n):
            for cp in copies(src_refs[r], land_refs[r], send_sems, recv_sems, r * ncopy):
                cp.start()
        token[...] = jnp.zeros_like(token)

    lands = [s if isinstance(s, jax.Array) else lax.empty(s.shape, s.dtype) for s in land_shapes]
    ins = list(srcs) + [pltpu.with_memory_space_constraint(a, pltpu.HBM) for a in lands]
    out_shape = ([pltpu.SemaphoreType.DMA((n * ncopy,)), pltpu.SemaphoreType.DMA((n * ncopy,))]
                 + [pltpu.HBM(a.shape, a.dtype) for a in lands]
                 + [_sds((SUBLANES, LANES), F32)])
    res = pl.pallas_call(
        body, name=name, out_shape=out_shape,
        in_specs=[_HBM] * (2 * n) + [_ANY], out_specs=[_SEM, _SEM] + [_HBM] * n + [BS(memory_space=pltpu.VMEM)],
        input_output_aliases={n + i: 2 + i for i in range(n)},
        compiler_params=pltpu.CompilerParams(has_side_effects=_EFFECT),
    )(*ins, after)
    return res[0], res[1], list(srcs), list(res[2:2 + n]), res[-1]


def _split_wait(name, copies, started, after):
    send_sems, recv_sems, srcs, lands, _ = started
    n = len(srcs)
    ncopy = send_sems.shape[0] // n

    def body(*refs):
        src_refs, land_refs = refs[:n], refs[n:2 * n]
        send_sems, recv_sems = refs[2 * n], refs[2 * n + 1]
        for r in range(n):
            for cp in copies(src_refs[r], land_refs[r], send_sems, recv_sems, r * ncopy):
                cp.wait_send()
                cp.wait_recv()

    res = pl.pallas_call(
        body, name=name, out_shape=[pltpu.HBM(a.shape, a.dtype) for a in lands],
        in_specs=[_HBM] * (2 * n) + [_SEM, _SEM, _ANY], out_specs=[_HBM] * n,
        input_output_aliases={n + i: i for i in range(n)},
        compiler_params=pltpu.CompilerParams(has_side_effects=_EFFECT),
    )(*srcs, *lands, send_sems, recv_sems, after)
    return list(srcs), list(res)


def _adamw(name, w, m, v, parts):
    R, C = w.shape
    S = parts.shape[0]
    tr = _tile(R, max(SUBLANES, (256 * 1024) // C), SUBLANES)

    def body(w_ref, m_ref, v_ref, p_ref, g_ref, d_ref, nm_ref, nv_ref):
        g = p_ref[0].astype(F32)
        for s in range(1, S):
            g = g + p_ref[s].astype(F32)
        m2 = ADAM_B1 * m_ref[...] + (1.0 - ADAM_B1) * g
        v2 = ADAM_B2 * v_ref[...] + (1.0 - ADAM_B2) * (g * g)
        m_hat = m2 / (1.0 - ADAM_B1 ** ADAM_STEP)
        v_hat = v2 / (1.0 - ADAM_B2 ** ADAM_STEP)
        g_ref[...] = g
        d_ref[...] = -ADAM_LR * (m_hat / (jnp.sqrt(v_hat) + ADAM_EPS) + ADAM_WD * w_ref[...])
        nm_ref[...] = m2
        nv_ref[...] = v2

    blk = BS((tr, C), lambda i: (i, 0))
    return _pc(name, body, (R // tr,), [w, m, v, parts],
               [blk, blk, blk, BS((S, tr, C), lambda i: (0, i, 0))],
               [_sds((R, C), F32)] * 4, [blk] * 4)


def _ssm_disc(lam_re, lam_im, log_step, br_t, bi_t):
    lr = jnp.minimum(lam_re, MIN_NEG_REAL)
    li = lam_im
    dt = jnp.exp(log_step)
    mag = jnp.exp(lr * dt)
    ang = li * dt
    ab_re = mag * jnp.cos(ang)
    ab_im = mag * jnp.sin(ang)
    nr = ab_re - 1.0
    ni = ab_im
    den = lr * lr + li * li
    f_re = (nr * lr + ni * li) / den
    f_im = (ni * lr - nr * li) / den
    bb_re = f_re[None] * br_t - f_im[None] * bi_t
    bb_im = f_re[None] * bi_t + f_im[None] * br_t
    return ab_re, ab_im, bb_re, bb_im


def _ssm_param_fwd(lam_re, lam_im, log_step, br_t, bi_t):
    G, P = lam_re.shape

    def body(lr_ref, li_ref, ls_ref, br_ref, bi_ref, pw_re_ref, pw_im_ref, bbr_ref, bbi_ref):
        ab_re, ab_im, bb_re, bb_im = _ssm_disc(lr_ref[...], li_ref[...], ls_ref[...], br_ref[...], bi_ref[...])
        bbr_ref[...] = bb_re
        bbi_ref[...] = bb_im
        pr, pi = ab_re, ab_im
        for r in range(SUBLANES):
            pw_re_ref[r] = pr
            pw_im_ref[r] = pi
            pr, pi = pr * ab_re - pi * ab_im, pr * ab_im + pi * ab_re

    full = lambda a: BS(a.shape, lambda i: (0,) * a.ndim)
    ins = [lam_re, lam_im, log_step, br_t, bi_t]
    outs = [_sds((SUBLANES, G, P), F32)] * 2 + [_sds(br_t.shape, F32)] * 2
    return _pc("ssm_param_fwd", body, (1,), ins, [full(a) for a in ins], outs, [full(o) for o in outs])


def _ssm_param_bwd(lam_re, lam_im, log_step, br_t, bi_t, d_ab_re, d_ab_im, d_bbr, d_bbi):
    def body(lr_ref, li_ref, ls_ref, br_ref, bi_ref, dar_ref, dai_ref, dbr_ref, dbi_ref,
             o_lr, o_li, o_ls, o_br, o_bi):
        prim = (lr_ref[...], li_ref[...], ls_ref[...], br_ref[...], bi_ref[...])
        _, vjp = jax.vjp(_ssm_disc, *prim)
        dar = dar_ref[0]
        dai = dai_ref[0]
        for k in range(1, dar_ref.shape[0]):
            dar = dar + dar_ref[k]
            dai = dai + dai_ref[k]
        g = vjp((dar, dai, dbr_ref[...], dbi_ref[...]))
        for r, v in zip((o_lr, o_li, o_ls, o_br, o_bi), g):
            r[...] = v

    full = lambda a: BS(a.shape, lambda i: (0,) * a.ndim)
    ins = [lam_re, lam_im, log_step, br_t, bi_t, d_ab_re, d_ab_im, d_bbr, d_bbi]
    outs = [_sds(a.shape, F32) for a in (lam_re, lam_im, log_step, br_t, bi_t)]
    return _pc("ssm_param_bwd", body, (1,), ins, [full(a) for a in ins], outs, [full(o) for o in outs])


def _bcast_row(ref, r, w):
    return jnp.broadcast_to(ref[pl.ds(r, 1), :], (SUBLANES, w))


def _pick_row(x, row, r):
    return jnp.broadcast_to(jnp.sum(jnp.where(row == r, x, 0.0), axis=0, keepdims=True), x.shape)


def _scan_fwd(bu_re, bu_im, pw_re, pw_im, nseq, L):
    N, SL = bu_re.shape
    W = _tile(SL, 256)

    def body(bre_ref, bim_ref, pre_ref, pim_ref, sre_ref, sim_ref):
        pre, pim = pre_ref[...], pim_ref[...]
        steps = [(k, _bcast_row(pre_ref, k - 1, W), _bcast_row(pim_ref, k - 1, W)) for k in (1, 2, 4)]
        row = lax.broadcasted_iota(jnp.int32, (SUBLANES, W), 0)

        def step(i, carry):
            cr, ci = carry
            r0 = pl.multiple_of(i * SUBLANES, SUBLANES)
            xr = bre_ref[pl.ds(r0, SUBLANES), :]
            xi = bim_ref[pl.ds(r0, SUBLANES), :]
            for k, ar, ai in steps:
                sr = pltpu.roll(xr, k, axis=0)
                si = pltpu.roll(xi, k, axis=0)
                keep = row >= k
                xr, xi = (xr + jnp.where(keep, ar * sr - ai * si, 0.0),
                          xi + jnp.where(keep, ar * si + ai * sr, 0.0))
            xr, xi = xr + (pre * cr - pim * ci), xi + (pre * ci + pim * cr)
            sre_ref[pl.ds(r0, SUBLANES), :] = xr
            sim_ref[pl.ds(r0, SUBLANES), :] = xi
            return _pick_row(xr, row, SUBLANES - 1), _pick_row(xi, row, SUBLANES - 1)

        zero = jnp.zeros((SUBLANES, W), F32)
        lax.fori_loop(0, L // SUBLANES, step, (zero, zero))

    blk = BS((L, W), lambda s, j: (s, j))
    pw = BS((SUBLANES, W), lambda s, j: (0, j))
    return _pc("ssm_scan_fwd", body, (nseq, SL // W), [bu_re, bu_im, pw_re, pw_im], [blk, blk, pw, pw],
               [_sds((N, SL), F32)] * 2, [blk, blk])


def _scan_bwd(ds_re, ds_im, s_re, s_im, pw_re, pw_im, pwf_re, pwf_im, nseq, L):
    N, SL = ds_re.shape
    W = _tile(SL, 256)
    nt = L // SUBLANES

    def body(dsr_ref, dsi_ref, sre_ref, sim_ref, pre_ref, pim_ref, fre_ref, fim_ref,
             lre_ref, lim_ref, dar_ref, dai_ref):
        fre, fim = fre_ref[...], -fim_ref[...]
        steps = [(k, _bcast_row(pre_ref, k - 1, W), -_bcast_row(pim_ref, k - 1, W)) for k in (1, 2, 4)]
        row = lax.broadcasted_iota(jnp.int32, (SUBLANES, W), 0)

        def step(ii, carry):
            cr, ci, acr, aci = carry
            i = nt - 1 - ii
            r0 = pl.multiple_of(i * SUBLANES, SUBLANES)
            xr = dsr_ref[pl.ds(r0, SUBLANES), :]
            xi = dsi_ref[pl.ds(r0, SUBLANES), :]
            for k, ar, ai in steps:
                sr = pltpu.roll(xr, SUBLANES - k, axis=0)
                si = pltpu.roll(xi, SUBLANES - k, axis=0)
                keep = row < SUBLANES - k
                xr, xi = (xr + jnp.where(keep, ar * sr - ai * si, 0.0),
                          xi + jnp.where(keep, ar * si + ai * sr, 0.0))
            xr, xi = xr + (fre * cr - fim * ci), xi + (fre * ci + fim * cr)
            lre_ref[pl.ds(r0, SUBLANES), :] = xr
            lim_ref[pl.ds(r0, SUBLANES), :] = xi
            p0 = pl.multiple_of(jnp.maximum(i - 1, 0) * SUBLANES, SUBLANES)
            has_prev = i > 0
            spr = jnp.where(row == 0,
                            jnp.where(has_prev, pltpu.roll(sre_ref[pl.ds(p0, SUBLANES), :], 1, axis=0), 0.0),
                            pltpu.roll(sre_ref[pl.ds(r0, SUBLANES), :], 1, axis=0))
            spi = jnp.where(row == 0,
                            jnp.where(has_prev, pltpu.roll(sim_ref[pl.ds(p0, SUBLANES), :], 1, axis=0), 0.0),
                            pltpu.roll(sim_ref[pl.ds(r0, SUBLANES), :], 1, axis=0))
            acr = acr + (xr * spr + xi * spi)
            aci = aci + (xi * spr - xr * spi)
            return _pick_row(xr, row, 0), _pick_row(xi, row, 0), acr, aci

        zero = jnp.zeros((SUBLANES, W), F32)
        _, _, acr, aci = lax.fori_loop(0, nt, step, (zero, zero, zero, zero))
        dar_ref[...] = acr
        dai_ref[...] = aci

    blk = BS((L, W), lambda s, j: (s, j))
    pw = BS((SUBLANES, W), lambda s, j: (0, j))
    da = BS((None, SUBLANES, W), lambda s, j: (s, 0, j))
    return _pc("ssm_scan_bwd", body, (nseq, SL // W),
               [ds_re, ds_im, s_re, s_im, pw_re, pw_im, pwf_re, pwf_im], [blk] * 4 + [pw] * 4,
               [_sds((N, SL), F32)] * 2 + [_sds((nseq, SUBLANES, SL), F32)] * 2, [blk, blk, da, da])


def _pool_select(g, vals):
    return jnp.where(g == 0, vals[0], jnp.where(g == 1, vals[1], jnp.where(g == 2, vals[2], vals[3])))


def _pool_fwd(proj, col0, DP, nseq, L):
    N = proj.shape[0]
    PG = DP // len(POOL_WINDOWS)
    W = _tile(PG, 256)

    def body(v_ref, z_ref):
        g = pl.program_id(1) // (PG // W)
        v = v_ref[...]
        row = lax.broadcasted_iota(jnp.int32, (L, W), 0)
        sums, s, k = [], v, 1
        for _ in POOL_WINDOWS:
            s = s + jnp.where(row >= k, pltpu.roll(s, k, axis=0), 0.0)
            sums.append(s)
            k *= 2
        win = _pool_select(g, [float(w) for w in POOL_WINDOWS])
        cnt = jnp.minimum((row + 1).astype(F32), win)
        z_ref[...] = (_pool_select(g, sums) / cnt - v).astype(z_ref.dtype)

    return _pc("pool_fwd", body, (nseq, DP // W), [proj], [BS((L, W), lambda s, j: (s, col0 // W + j))],
               _sds((N, DP), BF16), BS((L, W), lambda s, j: (s, j)))


def _pool_bwd(dz, nseq, L):
    N, DP = dz.shape
    PG = DP // len(POOL_WINDOWS)
    W = _tile(PG, 256)

    def body(dz_ref, dv_ref):
        g = pl.program_id(1) // (PG // W)
        d = dz_ref[...]
        row = lax.broadcasted_iota(jnp.int32, (L, W), 0)
        win = _pool_select(g, [float(w) for w in POOL_WINDOWS])
        s = d / jnp.minimum((row + 1).astype(F32), win)
        sums, k = [], 1
        for _ in POOL_WINDOWS:
            s = s + jnp.where(row < L - k, pltpu.roll(s, L - k, axis=0), 0.0)
            sums.append(s)
            k *= 2
        dv_ref[...] = (_pool_select(g, sums) - d).astype(dv_ref.dtype)

    blk = BS((L, W), lambda s, j: (s, j))
    return _pc("pool_bwd", body, (nseq, DP // W), [dz], [blk], _sds((N, DP), BF16), blk)


def _rstd(x):
    return lax.rsqrt(jnp.mean(x * x, axis=-1, keepdims=True) + EPS)


def _norm_bwd(dy, xhat, rstd, gain):
    t = dy * gain
    return rstd * (t - xhat * jnp.mean(t * xhat, axis=-1, keepdims=True))


def _pre_norm(x, g1):
    N, D = x.shape
    tr = _tile(N, 128, SUBLANES)

    def body(x_ref, g_ref, a_ref):
        xv = x_ref[...]
        a_ref[...] = (xv * _rstd(xv) * g_ref[...]).astype(a_ref.dtype)

    row = BS((tr, D), lambda i: (i, 0))
    vec = BS((1, D), lambda i: (0, 0))
    return _pc("pre_norm", body, (N // tr,), [x, g1], [row, vec], _sds((N, D), BF16), row)


def _mid_norm(x, o, g2, g3):
    N, D = x.shape
    tr = _tile(N, 128, SUBLANES)

    def body(x_ref, o_ref, g2_ref, g3_ref, h1_ref, c_ref):
        ov = o_ref[...]
        h1 = x_ref[...] + ov * _rstd(ov) * g2_ref[...]
        h1_ref[...] = h1
        c_ref[...] = (h1 * _rstd(h1) * g3_ref[...]).astype(c_ref.dtype)

    row = BS((tr, D), lambda i: (i, 0))
    vec = BS((1, D), lambda i: (0, 0))
    return _pc("mid_norm", body, (N // tr,), [x, o, g2, g3], [row, row, vec, vec],
               [_sds((N, D), F32), _sds((N, D), BF16)], [row, row])


def _post_ffn(h1, dn, tgt, g4):
    N, D = h1.shape
    tr = _tile(N, 128, SUBLANES)

    def body(h1_ref, dn_ref, t_ref, g_ref, dh2_ref, ddn_ref, lossv_ref, dg_ref):
        first = pl.program_id(0) == 0
        dnv = dn_ref[...]
        rstd = _rstd(dnv)
        xhat = dnv * rstd
        gain = g_ref[...]
        err = (h1_ref[...] + xhat * gain) - t_ref[...]
        dh2 = err / float(D)
        dh2_ref[...] = dh2
        ddn_ref[...] = _norm_bwd(dh2, xhat, rstd, gain).astype(ddn_ref.dtype)
        _rowsum_into(lossv_ref, first, err * err)
        _rowsum_into(dg_ref, first, dh2 * xhat)

    row = BS((tr, D), lambda i: (i, 0))
    vec = BS((1, D), lambda i: (0, 0))
    return _pc("post_ffn", body, (N // tr,), [h1, dn, tgt, g4], [row, row, row, vec],
               [_sds((N, D), F32), _sds((N, D), BF16), _sds((1, D), F32), _sds((1, D), F32)], [row, row, vec, vec])


def _mid_bwd(dh2, dc, h1, o, g2, g3, deps=()):
    N, D = h1.shape
    tr = _tile(N, 128, SUBLANES)

    def body(dh2_ref, dc_ref, h1_ref, o_ref, g2_ref, g3_ref, dh1_ref, do_ref, dg2_ref, dg3_ref):
        first = pl.program_id(0) == 0
        h1 = h1_ref[...]
        r3 = _rstd(h1)
        hc = h1 * r3
        dcv = dc_ref[...]
        dh1 = dh2_ref[...] + _norm_bwd(dcv, hc, r3, g3_ref[...])
        dh1_ref[...] = dh1
        ov = o_ref[...]
        r2 = _rstd(ov)
        ho = ov * r2
        do_ref[...] = _norm_bwd(dh1, ho, r2, g2_ref[...]).astype(do_ref.dtype)
        _rowsum_into(dg3_ref, first, dcv * hc)
        _rowsum_into(dg2_ref, first, dh1 * ho)

    row = BS((tr, D), lambda i: (i, 0))
    vec = BS((1, D), lambda i: (0, 0))
    return _pc("mid_bwd", body, (N // tr,), [dh2, dc, h1, o, g2, g3], [row] * 4 + [vec, vec],
               [_sds((N, D), F32), _sds((N, D), BF16), _sds((1, D), F32), _sds((1, D), F32)], [row, row, vec, vec],
               deps=deps)


def _pre_bwd(x, da, dh1, g1):
    N, D = x.shape
    tr = _tile(N, 128, SUBLANES)

    def body(x_ref, da_ref, dh1_ref, g_ref, dx_ref, dg_ref):
        first = pl.program_id(0) == 0
        xv = x_ref[...]
        r1 = _rstd(xv)
        xh = xv * r1
        dav = da_ref[...]
        dx_ref[...] = dh1_ref[...] + _norm_bwd(dav, xh, r1, g_ref[...])
        _rowsum_into(dg_ref, first, dav * xh)

    row = BS((tr, D), lambda i: (i, 0))
    vec = BS((1, D), lambda i: (0, 0))
    return _pc("pre_bwd", body, (N // tr,), [x, da, dh1, g1], [row, row, row, vec],
               [_sds((N, D), F32), _sds((1, D), F32)], [row, vec])


def _conv_rows(x_ref, halo_ref, first):
    x = x_ref[...]
    tr = x.shape[0]
    xx = jnp.concatenate([jnp.where(first, 0.0, halo_ref[...]), x], axis=0)
    x1 = pltpu.roll(xx, 1, axis=0)[SUBLANES:]
    x2 = pltpu.roll(xx, 2, axis=0)[SUBLANES:]
    del tr
    return x, x1, x2


def _conv_apply(rows, w_ref, b_ref):
    x, x1, x2 = rows
    return ((b_ref[...] + x2 * w_ref[pl.ds(0, 1), :]) + x1 * w_ref[pl.ds(1, 1), :]) + x * w_ref[pl.ds(2, 1), :]


def _gate_specs(N, FC, TR, half):
    tile = BS((None, TR, FC), lambda jj, i: (jj + half, i, 0))
    halo = BS((None, SUBLANES, FC), lambda jj, i: (jj + half, jnp.maximum(i * (TR // SUBLANES) - 1, 0), 0))
    cw = BS((None, 3, FC), lambda jj, i: (jj + half, 0, 0))
    cb = BS((None, 1, FC), lambda jj, i: (jj + half, 0, 0))
    return tile, halo, cw, cb


def _gate_fwd(up_pre, cw, cb, L):
    nb, N, FC = up_pre.shape
    half = nb // 2
    TR = _tile(L, 128, SUBLANES)

    def body(xa_ref, ha_ref, wa_ref, ba_ref, xb_ref, hb_ref, wb_ref, bb_ref, f_ref):
        first = (pl.program_id(1) % (L // TR)) == 0
        ua = _conv_apply(_conv_rows(xa_ref, ha_ref, first), wa_ref, ba_ref)
        ub = _conv_apply(_conv_rows(xb_ref, hb_ref, first), wb_ref, bb_ref)
        f_ref[...] = (_gelu(ua) * ub).astype(f_ref.dtype)

    sa, sb = _gate_specs(N, FC, TR, 0), _gate_specs(N, FC, TR, half)
    return _pc("gate_fwd", body, (half, N // TR), [up_pre, up_pre, cw, cb] * 2, list(sa) + list(sb),
               _sds((half, N, FC), BF16), BS((None, TR, FC), lambda jj, i: (jj, i, 0)))


def _gate_bwd(up_pre, cw, cb, df, L, deps=()):
    nb, N, FC = up_pre.shape
    half = nb // 2
    TR = _tile(L, 128, SUBLANES)

    def body(xa_ref, ha_ref, wa_ref, ba_ref, xb_ref, hb_ref, wb_ref, bb_ref, df_ref, dup_ref, dw_ref, dbias_ref):
        i = pl.program_id(1)
        first_row = i == 0
        first = (i % (L // TR)) == 0
        ra = _conv_rows(xa_ref, ha_ref, first)
        rb = _conv_rows(xb_ref, hb_ref, first)
        ua = _conv_apply(ra, wa_ref, ba_ref)
        ub = _conv_apply(rb, wb_ref, bb_ref)
        dfv = df_ref[...].astype(F32)
        dua = dfv * ub * _gelu_grad(ua)
        dub = dfv * _gelu(ua)
        dup_ref[0] = dua
        dup_ref[1] = dub
        for h, (rows, du) in enumerate(((ra, dua), (rb, dub))):
            x, x1, x2 = rows
            _rowsum_into(dbias_ref.at[h], first_row, du)
            for k, xs in enumerate((x2, x1, x)):
                _rowsum_into(dw_ref.at[h, pl.ds(k, 1), :], first_row, du * xs)

    sa, sb = _gate_specs(N, FC, TR, 0), _gate_specs(N, FC, TR, half)
    tile = BS((None, TR, FC), lambda jj, i: (jj, i, 0))
    both = BS((2, None, TR, FC), lambda jj, i: (0, jj, i, 0))
    dw = BS((2, None, 3, FC), lambda jj, i: (0, jj, 0, 0))
    dbias = BS((2, None, 1, FC), lambda jj, i: (0, jj, 0, 0))
    return _pc("gate_bwd", body, (half, N // TR), [up_pre, up_pre, cw, cb] * 2 + [df], list(sa) + list(sb) + [tile],
               [_sds((2, half, N, FC), F32), _sds((2, half, 3, FC), F32), _sds((2, half, 1, FC), F32)],
               [both, dw, dbias], deps=deps)


def _conv_bwd(dup, cw, L):
    nb, N, FC = dup.shape
    TR = _tile(L, 128, SUBLANES)
    nrb = N // SUBLANES

    def body(x_ref, h_ref, w_ref, o_ref):
        last = ((pl.program_id(1) + 1) % (L // TR)) == 0
        x = x_ref[...]
        xx = jnp.concatenate([x, jnp.where(last, 0.0, h_ref[...])], axis=0)
        x1 = pltpu.roll(xx, TR + SUBLANES - 1, axis=0)[:TR]
        x2 = pltpu.roll(xx, TR + SUBLANES - 2, axis=0)[:TR]
        o_ref[...] = (x * w_ref[pl.ds(2, 1), :] + x1 * w_ref[pl.ds(1, 1), :] + x2 * w_ref[pl.ds(0, 1), :]
                      ).astype(o_ref.dtype)

    tile = BS((None, TR, FC), lambda jj, i: (jj, i, 0))
    halo = BS((None, SUBLANES, FC), lambda jj, i: (jj, jnp.minimum((i + 1) * (TR // SUBLANES), nrb - 1), 0))
    w = BS((None, 3, FC), lambda jj, i: (jj, 0, 0))
    return _pc("conv_bwd", body, (nb, N // TR), [dup, dup, cw], [tile, halo, w], _sds((nb, N, FC), BF16), tile)


def _pack(arrs):
    parts = []
    for a in arrs:
        flat = a.reshape(-1).astype(F32)
        pad = (-flat.shape[0]) % (SUBLANES * LANES)
        parts.append(jnp.pad(flat, (0, pad)))
    return jnp.concatenate(parts).reshape(-1, LANES)


def _unpack(packed, shapes):
    flat = packed.reshape(-1)
    out, off = [], 0
    for s in shapes:
        n = math.prod(s)
        out.append(flat[off:off + n].reshape(s))
        off += n + ((-n) % (SUBLANES * LANES))
    return out


def _small_sum(gathered, loss_rows, d_model):
    S, R, C = gathered.shape

    def body(p_ref, tot_ref, loss_ref):
        t = p_ref[0]
        for s in range(1, S):
            t = t + p_ref[s]
        tot_ref[...] = t
        loss_ref[...] = jnp.full((1, 1), 0.5 / d_model, F32) * jnp.sum(t[:loss_rows])

    return _pc("small_sum", body, (1,), [gathered], [BS((S, R, C), lambda i: (0, 0, 0))],
               [_sds((R, C), F32), _sds((1, 1), F32)], [BS((R, C), lambda i: (0, 0)), BS((1, 1), lambda i: (0, 0))])


def _block_diag_in(bb_t, nch):
    J, G, P = bb_t.shape
    gl = G // nch
    b = bb_t.reshape(J, nch, gl, P).transpose(1, 0, 2, 3)
    eye = jnp.eye(gl, dtype=F32)
    w = eye[None, :, None, :, None] * b[:, None, :, :, :]
    return w.reshape(nch, gl * J, gl * P)


def _block_diag_in_grad(dw, J, G, P):
    nch = dw.shape[0]
    gl = G // nch
    d = dw.reshape(nch, gl, J, gl, P)
    d = jnp.einsum("cgjgp->jcgp", d)
    return d.reshape(J, G, P)


def _block_diag_out(c, nch):
    G, J, P = c.shape
    gl = G // nch
    cc = c.reshape(nch, gl, J, P).transpose(0, 1, 3, 2)
    eye = jnp.eye(gl, dtype=F32)
    w = cc[:, :, :, None, :] * eye[None, :, None, :, None]
    return w.reshape(nch, gl * P, gl * J)


def _block_diag_out_grad(dw, G, J, P):
    nch = dw.shape[0]
    gl = G // nch
    d = dw.reshape(nch, gl, P, gl, J)
    d = jnp.einsum("cgpgj->cgjp", d)
    return d.reshape(G, J, P)


def kernel(x, norm_pre_mix, w_in, ssm_lambda_re, ssm_lambda_im, ssm_log_step, ssm_b_re, ssm_b_im, ssm_c_re, ssm_c_im, ssm_d, ssm_glu_w, ssm_glu_b, pool_w, pool_b, pool_scale, w_branch_ssm, w_branch_pool, w_out, norm_post_mix, norm_pre_ffn, w_up, ffn_conv_w, ffn_conv_b, w_down, norm_post_ffn, loss_target, m_norm_pre_mix, m_w_in, m_ssm_lambda_re, m_ssm_lambda_im, m_ssm_log_step, m_ssm_b_re, m_ssm_b_im, m_ssm_c_re, m_ssm_c_im, m_ssm_d, m_ssm_glu_w, m_ssm_glu_b, m_pool_w, m_pool_b, m_pool_scale, m_w_branch_ssm, m_w_branch_pool, m_w_out, m_norm_post_mix, m_norm_pre_ffn, m_w_up, m_ffn_conv_w, m_ffn_conv_b, m_w_down, m_norm_post_ffn, v_norm_pre_mix, v_w_in, v_ssm_lambda_re, v_ssm_lambda_im, v_ssm_log_step, v_ssm_b_re, v_ssm_b_im, v_ssm_c_re, v_ssm_c_im, v_ssm_d, v_ssm_glu_w, v_ssm_glu_b, v_pool_w, v_pool_b, v_pool_scale, v_w_branch_ssm, v_w_branch_pool, v_w_out, v_norm_post_mix, v_norm_pre_ffn, v_w_up, v_ffn_conv_w, v_ffn_conv_b, v_w_down, v_norm_post_ffn):
    args = dict(locals())
    names = ["norm_pre_mix", "w_in", "ssm_lambda_re", "ssm_lambda_im", "ssm_log_step", "ssm_b_re", "ssm_b_im",
             "ssm_c_re", "ssm_c_im", "ssm_d", "ssm_glu_w", "ssm_glu_b", "pool_w", "pool_b", "pool_scale",
             "w_branch_ssm", "w_branch_pool", "w_out", "norm_post_mix", "norm_pre_ffn", "w_up", "ffn_conv_w",
             "ffn_conv_b", "w_down", "norm_post_ffn"]

    nseq, L, D = x.shape
    N = nseq * L
    U = D // NDEV
    DS = ssm_d.shape[1]
    DP = pool_scale.shape[1]
    G, P, J = ssm_b_re.shape[1:]
    SL = G * P
    CH = GROUPS_PER_CHUNK * J
    CS = GROUPS_PER_CHUNK * P
    NCH = DS // CH
    NPG = len(POOL_WINDOWS)
    PG = DP // NPG
    FC = w_up.shape[2]
    NB = NDEV
    HB = NB // 2
    F2 = NB * FC
    dev = _slot(_mesh_pos())
    tm = _tile(N, 1024)
    tm2 = _tile(N, 512)

    x2 = x.reshape(N, D)
    tgt = loss_target.reshape(N, D)

    def bf(t):
        return t.astype(BF16)

    def g_start(tag, group, after):
        return _split_start("gather_start_" + tag, _gather_copies, 3, group,
                            [_sds((NDEV,) + s.shape, s.dtype) for s in group], after)

    def g_land(tag, started, after):
        srcs, lands = _split_wait("gather_wait_" + tag, _gather_copies, started, after)
        return _split_start("d2d_start_" + tag, _d2d_copies, 4, srcs, lands, lands[0])

    def g_finish(tag, d2d, after):
        srcs, lands = _split_wait("d2d_wait_" + tag, _d2d_copies, d2d, after)
        return [lax.dynamic_update_index_in_dim(l, s, dev, 0) for l, s in zip(lands, srcs)]

    def x_start(tag, group):
        return _split_start("exchange_start_" + tag, _exchange_copies, NDEV - 1, group,
                            [_sds(g.shape, g.dtype) for g in group], group[0])

    def x_finish(tag, started, after):
        srcs, lands = _split_wait("exchange_wait_" + tag, _exchange_copies, started, after)
        own = [lax.dynamic_index_in_dim(s, dev, 0, keepdims=False) for s in srcs]
        return [lax.dynamic_update_index_in_dim(l, o, dev, 0) for l, o in zip(lands, own)]

    st_in = g_start("in", [bf(w_in[0])], x2)
    st_mix = g_start("mix", [bf(ssm_glu_w[0]), bf(pool_w[0]), pool_b[0], ffn_conv_w[0]], st_in[4])
    d_in = g_land("in", st_in, st_mix[4])
    st_br = g_start("branch", [bf(w_branch_ssm[0]), bf(w_branch_pool[0]), bf(w_out[0])], d_in[4])
    st_up = g_start("up", [bf(w_up[0])], st_br[4])
    st_down = g_start("down", [bf(w_down[0])], st_up[4])
    d_mix = g_land("mix", st_mix, st_down[4])
    (Win,) = g_finish("in", d_in, d_mix[4])
    conv_b_blk = ffn_conv_b.reshape(NB, 1, FC)

    lam_re, lam_im = ssm_lambda_re[0], ssm_lambda_im[0]
    log_step = ssm_log_step.reshape(G, 1)
    br_t = ssm_b_re[0].transpose(2, 0, 1)
    bi_t = ssm_b_im[0].transpose(2, 0, 1)
    pw_re3, pw_im3, bb_re, bb_im = _ssm_param_fwd(lam_re, lam_im, log_step, br_t, bi_t)
    pw_re, pw_im = pw_re3.reshape(SUBLANES, SL), pw_im3.reshape(SUBLANES, SL)
    pwf_re, pwf_im = pw_re[::-1], pw_im[::-1]
    WB = jnp.concatenate([_block_diag_in(bb_re, NCH), _block_diag_in(bb_im, NCH)], axis=2).astype(BF16)
    WCre = _block_diag_out(ssm_c_re[0], NCH).astype(BF16)
    WCim = _block_diag_out(-ssm_c_im[0], NCH).astype(BF16)

    a = _pre_norm(x2, norm_pre_mix)
    nq = 3 * NDEV
    (proj,) = _fused_matmul(
        "in_proj", (N // tm, nq, 1),
        [(a, BS((tm, D), lambda i, q, k: (i, 0)), Win, BS((None, D, U), lambda i, q, k: (q // 3, 0, q % 3)), "nn", 0)],
        [(tm, U)], [], [(_sds((N, 3 * D), F32), BS((tm, U), lambda i, q, k: (i, q)))],
        _store(lambda accs: accs))
    d_br = g_land("branch", st_br, proj)

    bu_re, bu_im = _fused_matmul(
        "ssm_in", (N // tm2, NCH, 1),
        [(proj, BS((tm2, CH), lambda i, c, k: (i, c)), WB, BS((None, CH, 2 * CS), lambda i, c, k: (c, 0, 0)), "nn", 0)],
        [(tm2, 2 * CS)], [],
        [(_sds((N, SL), F32), BS((tm2, CS), lambda i, c, k: (i, c)))] * 2,
        _store(lambda accs: (accs[0][:, :CS], accs[0][:, CS:])), deps=[d_br[4]])
    s_re, s_im = _scan_fwd(bu_re, bu_im, pw_re, pw_im, nseq, L)

    def ssm_out_epi(ids, accs, ex, o):
        u_ref, d_ref = ex
        y0 = accs[0] + d_ref[...] * u_ref[...]
        o[0][...] = y0
        o[1][...] = _gelu(y0).astype(BF16)

    y0, y1 = _fused_matmul(
        "ssm_out", (N // tm2, NCH, 1),
        [(s_re, BS((tm2, CS), lambda i, c, k: (i, c)), WCre, BS((None, CS, CH), lambda i, c, k: (c, 0, 0)), "nn", 0),
         (s_im, BS((tm2, CS), lambda i, c, k: (i, c)), WCim, BS((None, CS, CH), lambda i, c, k: (c, 0, 0)), "nn", 0)],
        [(tm2, CH)],
        [(proj, BS((tm2, CH), lambda i, c, k: (i, c))), (ssm_d, BS((1, CH), lambda i, c, k: (0, c)))],
        [(_sds((N, DS), F32), BS((tm2, CH), lambda i, c, k: (i, c))),
         (_sds((N, DS), BF16), BS((tm2, CH), lambda i, c, k: (i, c)))],
        ssm_out_epi)

    Wglu, Wpool, pool_b_all, conv_w_all = g_finish("mix", d_mix, y1)
    Wglu = Wglu.reshape(DS, DS)
    Wpool = Wpool.transpose(1, 0, 2, 3).reshape(NPG, PG, PG)
    pool_b_full = pool_b_all.transpose(1, 0, 2).reshape(1, DP)
    tn_s = _tile(DS, 512)

    def glu_epi(ids, accs, ex, o):
        y0_ref, b_ref = ex
        zg = accs[0] + b_ref[...]
        o[0][...] = zg
        o[1][...] = (_gelu(y0_ref[...]) * _sigmoid(zg)).astype(BF16)

    zg, ys = _fused_matmul(
        "ssm_glu", (N // tm, DS // tn_s, 1),
        [(y1, BS((tm, DS), lambda i, j, k: (i, 0)), Wglu, BS((DS, tn_s), lambda i, j, k: (0, j)), "nn", 0)],
        [(tm, tn_s)],
        [(y0, BS((tm, tn_s), lambda i, j, k: (i, j))), (ssm_glu_b, BS((1, tn_s), lambda i, j, k: (0, j)))],
        [(_sds((N, DS), F32), BS((tm, tn_s), lambda i, j, k: (i, j))),
         (_sds((N, DS), BF16), BS((tm, tn_s), lambda i, j, k: (i, j)))],
        glu_epi)

    z = _pool_fwd(proj, DS, DP, nseq, L)

    def pool_mm_epi(ids, accs, ex, o):
        b_ref, sc_ref = ex
        q = accs[0] + b_ref[...]
        o[0][...] = q
        o[1][...] = (q * sc_ref[...]).astype(BF16)

    qp, yp = _fused_matmul(
        "pool_mm", (N // tm, NPG, 1),
        [(z, BS((tm, PG), lambda i, g, k: (i, g)), Wpool, BS((None, PG, PG), lambda i, g, k: (g, 0, 0)), "nn", 0)],
        [(tm, PG)],
        [(pool_b_full, BS((1, PG), lambda i, g, k: (0, g))), (pool_scale, BS((1, PG), lambda i, g, k: (0, g)))],
        [(_sds((N, DP), F32), BS((tm, PG), lambda i, g, k: (i, g))),
         (_sds((N, DP), BF16), BS((tm, PG), lambda i, g, k: (i, g)))],
        pool_mm_epi)

    Wbs, Wbp, Wout = g_finish("branch", d_br, yp)
    d_up = g_land("up", st_up, Wbs)
    Wout = Wout.reshape(D, D)
    gs_blk = BS((tm2, U), lambda i, q, k: (i, (DS + DP) // U + q))
    gp_blk = BS((tm2, U), lambda i, q, k: (i, (DS + DP + D) // U + q))
    out_blk = BS((tm2, U), lambda i, q, k: (i, q))

    def branch_epi(ids, accs, ex, o):
        gs_ref, gp_ref = ex
        o[0][...] = accs[0]
        o[1][...] = accs[1]
        o[2][...] = (_sigmoid(gs_ref[...]) * accs[0] + _sigmoid(gp_ref[...]) * accs[1]).astype(BF16)

    Ys, Yp, merged = _fused_matmul(
        "branch", (N // tm2, NDEV, 1),
        [(ys, BS((tm2, DS), lambda i, q, k: (i, 0)), Wbs, BS((None, DS, U), lambda i, q, k: (q, 0, 0)), "nn", 0),
         (yp, BS((tm2, DP), lambda i, q, k: (i, 0)), Wbp, BS((None, DP, U), lambda i, q, k: (q, 0, 0)), "nn", 1)],
        [(tm2, U), (tm2, U)],
        [(proj, gs_blk), (proj, gp_blk)],
        [(_sds((N, D), F32), out_blk), (_sds((N, D), F32), out_blk), (_sds((N, D), BF16), out_blk)],
        branch_epi, deps=[d_up[4]])

    tn_d = _tile(D, 512)
    (o_mix,) = _fused_matmul(
        "out_proj", (N // tm, D // tn_d, 1),
        [(merged, BS((tm, D), lambda i, j, k: (i, 0)), Wout, BS((D, tn_d), lambda i, j, k: (0, j)), "nn", 0)],
        [(tm, tn_d)], [], [(_sds((N, D), F32), BS((tm, tn_d), lambda i, j, k: (i, j)))],
        _store(lambda accs: accs))
    h1, c = _mid_norm(x2, o_mix, norm_post_mix, norm_pre_ffn)

    (Wup,) = g_finish("up", d_up, c)
    d_down = g_land("down", st_down, Wup)
    tk_d = _tile(D, 1024)
    tk_up = _tile(D, 2048)
    (up_pre,) = _fused_matmul(
        "ffn_up", (N // tm2, NB, D // tk_up),
        [(c, BS((tm2, tk_up), lambda i, j, k: (i, k)), Wup, BS((None, tk_up, FC), lambda i, j, k: (j, k, 0)), "nn", 0)],
        [(tm2, FC)], [], [(_sds((NB, N, FC), F32), BS((None, tm2, FC), lambda i, j, k: (j, i, 0)))],
        _store(lambda accs: accs), deps=[d_down[4]])
    f = _gate_fwd(up_pre, conv_w_all, conv_b_blk, L)
    (Wdown,) = g_finish("down", d_down, f)
    Wdown = Wdown.reshape(HB, FC, D)
    tn_d2 = _tile(D, 1024)
    (dn,) = _fused_matmul(
        "ffn_down", (N // tm2, D // tn_d2, HB),
        [(f, BS((None, tm2, FC), lambda i, j, k: (k, i, 0)), Wdown, BS((None, FC, tn_d2), lambda i, j, k: (k, 0, j)), "nn", 0)],
        [(tm2, tn_d2)], [], [(_sds((N, D), F32), BS((tm2, tn_d2), lambda i, j, k: (i, j)))],
        _store(lambda accs: accs))
    dh2, d_dn, lossv, dg4 = _post_ffn(h1, dn, tgt, norm_post_ffn)

    (df,) = _fused_matmul(
        "ffn_down_dx", (N // tm2, HB, D // tk_d),
        [(d_dn, BS((tm2, tk_d), lambda i, j, k: (i, k)), Wdown, BS((None, FC, tk_d), lambda i, j, k: (j, 0, k)), "nt", 0)],
        [(tm2, FC)], [], [(_sds((HB, N, FC), BF16), BS((None, tm2, FC), lambda i, j, k: (j, i, 0)))],
        _store(lambda accs: accs))
    tk_n = _tile(N, 1024)
    (gW_down,) = _fused_matmul(
        "ffn_down_dw", (HB, D // tn_d, N // tk_n),
        [(f, BS((None, tk_n, FC), lambda j, n, k: (j, k, 0)), d_dn, BS((tk_n, tn_d), lambda j, n, k: (k, n)), "tn", 0)],
        [(FC, tn_d)], [], [(_sds((HB, FC, D), BF16), BS((None, FC, tn_d), lambda j, n, k: (j, 0, n)))],
        _store(lambda accs: accs))
    x_down = x_start("down", [gW_down.reshape(NDEV, FC // 2, D)])
    dup, dcw, dcb = _gate_bwd(up_pre, conv_w_all, conv_b_blk, df, L, deps=[x_down[4]])
    dpre = _conv_bwd(dup.reshape(NB, N, FC), conv_w_all, L)
    (dc,) = _fused_matmul(
        "ffn_up_dx", (N // tm2, D // tn_d2, NB),
        [(dpre, BS((None, tm2, FC), lambda i, j, k: (k, i, 0)), Wup, BS((None, tn_d2, FC), lambda i, j, k: (k, j, 0)), "nt", 0)],
        [(tm2, tn_d2)], [], [(_sds((N, D), F32), BS((tm2, tn_d2), lambda i, j, k: (i, j)))],
        _store(lambda accs: accs))
    tm_d = _tile(D, 512)
    (gW_up,) = _fused_matmul(
        "ffn_up_dw", (NB, D // tm_d, N // tk_n),
        [(c, BS((tk_n, tm_d), lambda j, n, k: (k, n)), dpre, BS((None, tk_n, FC), lambda j, n, k: (j, k, 0)), "tn", 0)],
        [(tm_d, FC)], [], [(_sds((NB, D, FC), BF16), BS((None, tm_d, FC), lambda j, n, k: (j, n, 0)))],
        _store(lambda accs: accs))
    x_up = x_start("up", [gW_up])

    dh1, d_o, dg2, dg3 = _mid_bwd(dh2, dc, h1, o_mix, norm_post_mix, norm_pre_ffn, deps=[x_up[4]])

    def dmerged_epi(ids, accs, ex, o):
        gs_ref, gp_ref, ys_ref, yp_ref = ex
        dm = accs[0]
        sg_s, sg_p = _sigmoid(gs_ref[...]), _sigmoid(gp_ref[...])
        o[0][...] = (dm * sg_s).astype(BF16)
        o[1][...] = (dm * sg_p).astype(BF16)
        o[2][...] = (dm * ys_ref[...] * sg_s * (1.0 - sg_s)).astype(BF16)
        o[3][...] = (dm * yp_ref[...] * sg_p * (1.0 - sg_p)).astype(BF16)

    dYs, dYp, dgs, dgp = _fused_matmul(
        "out_proj_dx", (N // tm2, NDEV, 1),
        [(d_o, BS((tm2, D), lambda i, q, k: (i, 0)), Wout, BS((U, D), lambda i, q, k: (q, 0)), "nt", 0)],
        [(tm2, U)],
        [(proj, gs_blk), (proj, gp_blk), (Ys, out_blk), (Yp, out_blk)],
        [(_sds((N, D), BF16), out_blk)] * 4,
        dmerged_epi)
    (gW_out,) = _fused_matmul(
        "out_proj_dw", (D // tm_d, D // tn_d, 1),
        [(merged, BS((N, tm_d), lambda i, j, k: (0, i)), d_o, BS((N, tn_d), lambda i, j, k: (0, j)), "tn", 0)],
        [(tm_d, tn_d)], [], [(_sds((D, D), BF16), BS((tm_d, tn_d), lambda i, j, k: (i, j)))],
        _store(lambda accs: accs))
    tm_s = _tile(DS, 512)
    gW_bs, gW_bp = _fused_matmul(
        "branch_dw", (DS // tm_s, NDEV, 1),
        [(ys, BS((N, tm_s), lambda i, q, k: (0, i)), dYs, BS((N, U), lambda i, q, k: (0, q)), "tn", 0),
         (yp, BS((N, tm_s), lambda i, q, k: (0, i)), dYp, BS((N, U), lambda i, q, k: (0, q)), "tn", 1)],
        [(tm_s, U), (tm_s, U)], [],
        [(_sds((NDEV, DS, U), BF16), BS((None, tm_s, U), lambda i, q, k: (q, i, 0)))] * 2,
        _store(lambda accs: accs))
    x_br = x_start("branch", [gW_bs, gW_bp, gW_out.reshape(NDEV, U, D)])

    tn_p = _tile(PG, 512)

    def dyp_epi(ids, accs, ex, o):
        q_ref, sc_ref = ex
        first = ids[1] == 0
        dyp = accs[0]
        dq = dyp * sc_ref[...]
        o[0][...] = dq.astype(BF16)
        _rowsum_into(o[1], first, dyp * q_ref[...])
        _rowsum_into(o[2], first, dq)

    dq, d_pscale, d_pb = _fused_matmul(
        "branch_pool_dx", (DP // tn_p, N // tm, NDEV),
        [(dYp, BS((tm, U), lambda j, i, k: (i, k)), Wbp, BS((None, tn_p, U), lambda j, i, k: (k, j, 0)), "nt", 0)],
        [(tm, tn_p)],
        [(qp, BS((tm, tn_p), lambda j, i, k: (i, j))), (pool_scale, BS((1, tn_p), lambda j, i, k: (0, j)))],
        [(_sds((N, DP), BF16), BS((tm, tn_p), lambda j, i, k: (i, j))),
         (_sds((1, DP), F32), BS((1, tn_p), lambda j, i, k: (0, j))),
         (_sds((1, DP), F32), BS((1, tn_p), lambda j, i, k: (0, j)))],
        dyp_epi, deps=[x_br[4]])
    (dz,) = _fused_matmul(
        "pool_mm_dx", (N // tm, NPG, 1),
        [(dq, BS((tm, PG), lambda i, g, k: (i, g)), Wpool, BS((None, PG, PG), lambda i, g, k: (g, 0, 0)), "nt", 0)],
        [(tm, PG)], [], [(_sds((N, DP), F32), BS((tm, PG), lambda i, g, k: (i, g)))],
        _store(lambda accs: accs))
    (gW_pool,) = _fused_matmul(
        "pool_mm_dw", (NPG, 1),
        [(z, BS((N, PG), lambda g, k: (0, g)), dq, BS((N, PG), lambda g, k: (0, g)), "tn", 0)],
        [(PG, PG)], [], [(_sds((NPG, PG, PG), BF16), BS((None, PG, PG), lambda g, k: (g, 0, 0)))],
        _store(lambda accs: accs))
    du_pool = _pool_bwd(dz, nseq, L)

    def dys_epi(ids, accs, ex, o):
        zg_ref, y0_ref = ex
        first = ids[1] == 0
        dys = accs[0]
        sg = _sigmoid(zg_ref[...])
        dzg = dys * _gelu(y0_ref[...]) * sg * (1.0 - sg)
        o[0][...] = dzg.astype(BF16)
        o[1][...] = dys * sg
        _rowsum_into(o[2], first, dzg)

    dzg, dy1_direct, d_glu_b = _fused_matmul(
        "branch_ssm_dx", (DS // tn_s, N // tm, NDEV),
        [(dYs, BS((tm, U), lambda j, i, k: (i, k)), Wbs, BS((None, tn_s, U), lambda j, i, k: (k, j, 0)), "nt", 0)],
        [(tm, tn_s)],
        [(zg, BS((tm, tn_s), lambda j, i, k: (i, j))), (y0, BS((tm, tn_s), lambda j, i, k: (i, j)))],
        [(_sds((N, DS), BF16), BS((tm, tn_s), lambda j, i, k: (i, j))),
         (_sds((N, DS), F32), BS((tm, tn_s), lambda j, i, k: (i, j))),
         (_sds((1, DS), F32), BS((1, tn_s), lambda j, i, k: (0, j)))],
        dys_epi)
    (gW_glu,) = _fused_matmul(
        "ssm_glu_dw", (DS // tm_s, DS // tn_s, 1),
        [(y1, BS((N, tm_s), lambda i, j, k: (0, i)), dzg, BS((N, tn_s), lambda i, j, k: (0, j)), "tn", 0)],
        [(tm_s, tn_s)], [], [(_sds((DS, DS), BF16), BS((tm_s, tn_s), lambda i, j, k: (i, j)))],
        _store(lambda accs: accs))
    x_mix = x_start("mix", [gW_glu.reshape(NDEV, DS // NDEV, DS),
                            gW_pool.reshape(NPG, NDEV, PG // NDEV, PG).transpose(1, 0, 2, 3)])

    tn_c = _tile(DS, CH)

    def dy0_epi(ids, accs, ex, o):
        d1_ref, y0_ref, u_ref = ex
        first = ids[1] == 0
        dy0 = (accs[0] + d1_ref[...]) * _gelu_grad(y0_ref[...])
        o[0][...] = dy0
        _rowsum_into(o[1], first, dy0 * u_ref[...])

    dy0, d_ssm_d = _fused_matmul(
        "ssm_glu_dx", (DS // tn_c, N // tm, 1),
        [(dzg, BS((tm, DS), lambda j, i, k: (i, 0)), Wglu, BS((tn_c, DS), lambda j, i, k: (j, 0)), "nt", 0)],
        [(tm, tn_c)],
        [(dy1_direct, BS((tm, tn_c), lambda j, i, k: (i, j))), (y0, BS((tm, tn_c), lambda j, i, k: (i, j))),
         (proj, BS((tm, tn_c), lambda j, i, k: (i, j)))],
        [(_sds((N, DS), F32), BS((tm, tn_c), lambda j, i, k: (i, j))),
         (_sds((1, DS), F32), BS((1, tn_c), lambda j, i, k: (0, j)))],
        dy0_epi, deps=[x_mix[4]])

    ds_re, ds_im = _fused_matmul(
        "ssm_out_dx", (N // tm2, NCH, 1),
        [(dy0, BS((tm2, CH), lambda i, c, k: (i, c)), WCre, BS((None, CS, CH), lambda i, c, k: (c, 0, 0)), "nt", 0),
         (dy0, BS((tm2, CH), lambda i, c, k: (i, c)), WCim, BS((None, CS, CH), lambda i, c, k: (c, 0, 0)), "nt", 1)],
        [(tm2, CS), (tm2, CS)], [],
        [(_sds((N, SL), F32), BS((tm2, CS), lambda i, c, k: (i, c)))] * 2,
        _store(lambda accs: accs))
    dWCre, dWCim = _fused_matmul(
        "ssm_out_dw", (NCH, N // tk_n),
        [(s_re, BS((tk_n, CS), lambda c, k: (k, c)), dy0, BS((tk_n, CH), lambda c, k: (k, c)), "tn", 0),
         (s_im, BS((tk_n, CS), lambda c, k: (k, c)), dy0, BS((tk_n, CH), lambda c, k: (k, c)), "tn", 1)],
        [(CS, CH), (CS, CH)], [],
        [(_sds((NCH, CS, CH), F32), BS((None, CS, CH), lambda c, k: (c, 0, 0)))] * 2,
        _store(lambda accs: accs))
    lam_r, lam_i, d_ab_re, d_ab_im = _scan_bwd(ds_re, ds_im, s_re, s_im, pw_re, pw_im, pwf_re, pwf_im, nseq, L)

    def du_epi(ids, accs, ex, o):
        dy0_ref, d_ref = ex
        o[0][...] = (accs[0] + dy0_ref[...] * d_ref[...]).astype(BF16)

    (du_ssm,) = _fused_matmul(
        "ssm_in_dx", (N // tm2, NCH, 1),
        [(lam_r, BS((tm2, CS), lambda i, c, k: (i, c)), WB, BS((None, CH, CS), lambda i, c, k: (c, 0, 0)), "nt", 0),
         (lam_i, BS((tm2, CS), lambda i, c, k: (i, c)), WB, BS((None, CH, CS), lambda i, c, k: (c, 0, 1)), "nt", 0)],
        [(tm2, CH)],
        [(dy0, BS((tm2, CH), lambda i, c, k: (i, c))), (ssm_d, BS((1, CH), lambda i, c, k: (0, c)))],
        [(_sds((N, DS), BF16), BS((tm2, CH), lambda i, c, k: (i, c)))],
        du_epi)
    dWBre, dWBim = _fused_matmul(
        "ssm_in_dw", (NCH, N // tk_n),
        [(proj, BS((tk_n, CH), lambda c, k: (k, c)), lam_r, BS((tk_n, CS), lambda c, k: (k, c)), "tn", 0),
         (proj, BS((tk_n, CH), lambda c, k: (k, c)), lam_i, BS((tk_n, CS), lambda c, k: (k, c)), "tn", 1)],
        [(CH, CS), (CH, CS)], [],
        [(_sds((NCH, CH, CS), F32), BS((None, CH, CS), lambda c, k: (c, 0, 0)))] * 2,
        _store(lambda accs: accs))
    d_bbr = _block_diag_in_grad(dWBre, J, G, P)
    d_bbi = _block_diag_in_grad(dWBim, J, G, P)
    d_lam_re, d_lam_im, d_log_step, d_br_t, d_bi_t = _ssm_param_bwd(
        lam_re, lam_im, log_step, br_t, bi_t,
        d_ab_re.reshape(nseq * SUBLANES, G, P), d_ab_im.reshape(nseq * SUBLANES, G, P), d_bbr, d_bbi)
    d_c_re = _block_diag_out_grad(dWCre, G, J, P)
    d_c_im = -_block_diag_out_grad(dWCim, G, J, P)

    dproj = jnp.concatenate([du_ssm, du_pool, dgs, dgp], axis=1)
    (gW_in,) = _fused_matmul(
        "in_proj_dw", (D // tm_d, nq, 1),
        [(a, BS((N, tm_d), lambda i, q, k: (0, i)), dproj, BS((N, U), lambda i, q, k: (0, q)), "tn", 0)],
        [(tm_d, U)], [], [(_sds((NDEV, D, 3 * U), BF16), BS((None, tm_d, U), lambda i, q, k: (q // 3, i, q % 3)))],
        _store(lambda accs: accs))
    x_in = x_start("in", [gW_in])
    (da,) = _fused_matmul(
        "in_proj_dx", (N // tm, D // tn_d2, NDEV),
        [(dproj, BS((tm, 3 * U), lambda i, j, k: (i, k)), Win, BS((None, tn_d2, 3 * U), lambda i, j, k: (k, j, 0)), "nt", 0)],
        [(tm, tn_d2)], [], [(_sds((N, D), F32), BS((tm, tn_d2), lambda i, j, k: (i, j)))],
        _store(lambda accs: accs), deps=[x_in[4]])
    grad_x, dg1 = _pre_bwd(x2, da, dh1, norm_pre_mix)

    d_conv_w = dcw.reshape(NB, 3, FC).transpose(1, 0, 2).reshape(3, F2)
    d_conv_b = dcb.reshape(1, F2)
    small = {
        "norm_pre_mix": dg1, "norm_post_mix": dg2, "norm_pre_ffn": dg3, "norm_post_ffn": dg4,
        "ssm_lambda_re": d_lam_re[None], "ssm_lambda_im": d_lam_im[None], "ssm_log_step": d_log_step.reshape(1, G),
        "ssm_b_re": d_br_t.transpose(1, 2, 0)[None], "ssm_b_im": d_bi_t.transpose(1, 2, 0)[None],
        "ssm_c_re": d_c_re[None], "ssm_c_im": d_c_im[None],
        "ssm_d": d_ssm_d, "ssm_glu_b": d_glu_b, "pool_scale": d_pscale,
        "pool_b": d_pb.reshape(1, NPG, PG), "ffn_conv_w": d_conv_w[None], "ffn_conv_b": d_conv_b,
    }
    small_names = list(small)
    packed = _pack([lossv] + [small[n] for n in small_names])
    st_small = _split_start("small_start", _broadcast_copies, NDEV - 1, [packed],
                            [_sds((NDEV,) + packed.shape, packed.dtype)], packed)

    res = {}
    after = st_small[4]
    for tag, started, group in (("down", x_down, ["w_down"]), ("up", x_up, ["w_up"]),
                                ("branch", x_br, ["w_branch_ssm", "w_branch_pool", "w_out"]),
                                ("mix", x_mix, ["ssm_glu_w", "pool_w"]), ("in", x_in, ["w_in"])):
        for n, parts in zip(group, x_finish(tag, started, after)):
            shape = args[n].shape
            cols = shape[-1]
            flat = lambda t: t.reshape(-1, cols)
            g, dl, nm, nv = _adamw("adamw_" + n, flat(args[n]), flat(args["m_" + n]), flat(args["v_" + n]),
                                   parts.reshape(NDEV, -1, cols))
            res[n] = tuple(t.reshape(shape) for t in (g, dl, nm, nv))
            after = g

    srcs, lands = _split_wait("small_wait", _broadcast_copies, st_small, after)
    small_all = lax.dynamic_update_index_in_dim(lands[0], srcs[0], dev, 0)
    loss_rows = (D + SUBLANES * LANES - 1) // (SUBLANES * LANES) * SUBLANES
    total, loss = _small_sum(small_all, loss_rows, D)
    totals = dict(zip(small_names, _unpack(total, [lossv.shape] + [small[n].shape for n in small_names])[1:]))
    totals["pool_b"] = lax.dynamic_slice_in_dim(totals["pool_b"], dev * (PG // NDEV), PG // NDEV, axis=2)
    totals["ffn_conv_w"] = lax.dynamic_slice_in_dim(totals["ffn_conv_w"], dev * FC, FC, axis=2)
    sm_g = _pack([totals[n] for n in small_names])
    sm_w, sm_m, sm_v = (_pack([args[p + n] for n in small_names]) for p in ("", "m_", "v_"))
    _, sm_d, sm_nm, sm_nv = _adamw("adamw_small", sm_w, sm_m, sm_v, sm_g[None])
    shapes = [args[n].shape for n in small_names]
    for n, dl, nm, nv in zip(small_names, _unpack(sm_d, shapes), _unpack(sm_nm, shapes), _unpack(sm_nv, shapes)):
        res[n] = (totals[n], dl, nm, nv)

    outs = [loss.reshape(()), grad_x.reshape(x.shape)]
    for k in range(4):
        outs += [res[n][k] for n in names]
    return tuple(outs)
```

```python
import functools
import math

import jax
import jax.numpy as jnp
from jax import lax
from jax.experimental import pallas as pl
from jax.experimental.pallas import tpu as pltpu

F32 = jnp.float32
BF16 = jnp.bfloat16
BS = pl.BlockSpec

NDEV = 8
SSM_GROUP = 16
SSM_STATE = 64
GROUPS_PER_CHUNK = 16
SCAN_UNROLL = 4
POOL_WINDOWS = (2, 4, 8, 16)
EPS = 1e-6
MIN_NEG_REAL = -1e-4
ADAM_LR, ADAM_B1, ADAM_B2, ADAM_EPS, ADAM_WD, ADAM_STEP = 0.001, 0.9, 0.999, 1e-08, 0.01, 10
LANES = 128
SUBLANES = 8
VMEM_LIMIT = 56 * 1024 * 1024

_DIMS = {"nn": (((1,), (0,)), ((), ())), "nt": (((1,), (1,)), ((), ())), "tn": (((0,), (0,)), ((), ()))}


def _tile(dim, pref, mult=LANES):
    if dim <= pref:
        return dim
    t = (pref // mult) * mult
    while t >= mult:
        if dim % t == 0:
            return t
        t -= mult
    return dim


def _pc(name, body, grid, ins, in_specs, outs, out_specs, scratch=(), deps=()):
    multi = isinstance(outs, (list, tuple))
    if deps:
        n_in, n_dep, inner = len(ins), len(deps), body

        def body(*refs):
            return inner(*refs[:n_in], *refs[n_in + n_dep:])

        ins = list(ins) + list(deps)
        in_specs = list(in_specs) + [BS(memory_space=pl.ANY)] * n_dep
    return pl.pallas_call(
        body, name=name, grid=grid, in_specs=list(in_specs),
        out_specs=list(out_specs) if multi else out_specs,
        out_shape=list(outs) if multi else outs, scratch_shapes=list(scratch),
        compiler_params=pltpu.CompilerParams(dimension_semantics=("arbitrary",) * len(grid),
                                             vmem_limit_bytes=VMEM_LIMIT),
    )(*ins)


def _sds(shape, dtype):
    return jax.ShapeDtypeStruct(tuple(shape), dtype)


def _gelu(x):
    k = math.sqrt(2.0 / math.pi)
    return 0.5 * x * (1.0 + jnp.tanh(k * (x + 0.044715 * (x * x * x))))


def _gelu_grad(x):
    k = math.sqrt(2.0 / math.pi)
    t = jnp.tanh(k * (x + 0.044715 * (x * x * x)))
    return 0.5 * (1.0 + t) + 0.5 * x * (1.0 - t * t) * (k * (1.0 + 3.0 * 0.044715 * x * x))


def _sigmoid(x):
    return jax.nn.sigmoid(x)


def _fused_matmul(name, grid, pairs, acc_shapes, extras, outs, epilogue, deps=()):
    n_p, n_e, n_o = len(pairs), len(extras), len(outs)
    rank = len(grid)
    nk = grid[-1]

    def body(*refs):
        ab = refs[:2 * n_p]
        ex = refs[2 * n_p:2 * n_p + n_e]
        o = refs[2 * n_p + n_e:2 * n_p + n_e + n_o]
        accs = refs[2 * n_p + n_e + n_o:]
        ids = [pl.program_id(d) for d in range(rank)]
        k = ids[-1]

        @pl.when(k == 0)
        def _():
            for acc in accs:
                acc[...] = jnp.zeros_like(acc)

        for p in range(n_p):
            a = ab[2 * p][...].astype(BF16)
            b = ab[2 * p + 1][...].astype(BF16)
            accs[pairs[p][5]][...] += lax.dot_general(a, b, _DIMS[pairs[p][4]], preferred_element_type=F32)

        @pl.when(k == nk - 1)
        def _():
            epilogue(ids, [acc[...] for acc in accs], ex, o)

    ins, in_specs = [], []
    for a, a_spec, b, b_spec, _, _ in pairs:
        ins += [a, b]
        in_specs += [a_spec, b_spec]
    for e, e_spec in extras:
        ins.append(e)
        in_specs.append(e_spec)
    res = _pc(name, body, grid, ins, in_specs, [s for s, _ in outs], [sp for _, sp in outs],
              scratch=[pltpu.VMEM(tuple(s), F32) for s in acc_shapes], deps=deps)
    return res


def _store(vals):
    def epilogue(ids, accs, ex, o):
        for r, v in zip(o, vals(accs)):
            r[...] = v.astype(r.dtype)
    return epilogue


def _rowsum_into(ref, first, v):
    s = jnp.sum(v, axis=0, keepdims=True)

    @pl.when(first)
    def _():
        ref[...] = s

    @pl.when(jnp.logical_not(first))
    def _():
        ref[...] += s


def _mesh_pos():
    return lax.axis_index("x"), lax.axis_index("y"), lax.axis_index("c")


def _slot(p):
    return 4 * p[0] + 2 * p[1] + p[2]


_HBM = BS(memory_space=pltpu.HBM)
_SEM = BS(memory_space=pltpu.SEMAPHORE)
_ANY = BS(memory_space=pl.ANY)
_EFFECT = pltpu.SideEffectType.DATAFLOW_SIDE_EFFECTING


def _other_chips(x, y):
    return [(1 - x, y), (x, 1 - y), (1 - x, 1 - y)]


def _all_peers(x, y, c):
    peers = []
    for k in range(1, NDEV):
        kx, ky, kc = (k >> 2) & 1, (k >> 1) & 1, k & 1
        peers.append((1 - x if kx else x, 1 - y if ky else y, 1 - c if kc else c))
    return peers


def _gather_copies(src, land, send_sems, recv_sems, base):
    x, y, c = _mesh_pos()
    return [pltpu.make_async_remote_copy(
        src_ref=src, dst_ref=land.at[_slot((x, y, c))],
        send_sem=send_sems.at[base + k], recv_sem=recv_sems.at[base + k],
        device_id=(*chip, c), device_id_type=pl.DeviceIdType.MESH) for k, chip in enumerate(_other_chips(x, y))]


def _d2d_copies(src, land, send_sems, recv_sems, base):
    x, y, c = _mesh_pos()
    blocks = [(x, y, c)] + [(*chip, c) for chip in _other_chips(x, y)]
    return [pltpu.make_async_remote_copy(
        src_ref=src if k == 0 else land.at[_slot(b)], dst_ref=land.at[_slot(b)],
        send_sem=send_sems.at[base + k], recv_sem=recv_sems.at[base + k],
        device_id=(x, y, 1 - c), device_id_type=pl.DeviceIdType.MESH) for k, b in enumerate(blocks)]


def _broadcast_copies(src, land, send_sems, recv_sems, base):
    x, y, c = _mesh_pos()
    return [pltpu.make_async_remote_copy(
        src_ref=src, dst_ref=land.at[_slot((x, y, c))],
        send_sem=send_sems.at[base + k], recv_sem=recv_sems.at[base + k],
        device_id=peer, device_id_type=pl.DeviceIdType.MESH) for k, peer in enumerate(_all_peers(x, y, c))]


def _exchange_copies(src, land, send_sems, recv_sems, base):
    x, y, c = _mesh_pos()
    return [pltpu.make_async_remote_copy(
        src_ref=src.at[_slot(peer)], dst_ref=land.at[_slot((x, y, c))],
        send_sem=send_sems.at[base + k], recv_sem=recv_sems.at[base + k],
        device_id=peer, device_id_type=pl.DeviceIdType.MESH) for k, peer in enumerate(_all_peers(x, y, c))]


def _split_start(name, copies, ncopy, srcs, land_shapes, after):
    n = len(srcs)

    def body(*refs):
        src_refs, land_refs = refs[:n], refs[n:2 * n]
        send_sems, recv_sems = refs[2 * n + 1], refs[2 * n + 2]
        token = refs[-1]
        for r in range(n):
            for cp in copies(src_refs[r], land_refs[r], send_sems, recv_sems, r * ncopy):
                cp.start()
        token[...] = jnp.zeros_like(token)

    lands = [s if isinstance(s, jax.Array) else pltpu.with_memory_space_constraint(lax.empty(s.shape, s.dtype), pltpu.HBM)
             for s in land_shapes]
    ins = list(srcs) + lands
    out_shape = ([pltpu.SemaphoreType.DMA((n * ncopy,)), pltpu.SemaphoreType.DMA((n * ncopy,))]
                 + [pltpu.HBM(a.shape, a.dtype) for a in lands]
                 + [_sds((SUBLANES, LANES), F32)])
    res = pl.pallas_call(
        body, name=name, out_shape=out_shape,
        in_specs=[_HBM] * (2 * n) + [_ANY], out_specs=[_SEM, _SEM] + [_HBM] * n + [BS(memory_space=pltpu.VMEM)],
        input_output_aliases={n + i: 2 + i for i in range(n)},
        compiler_params=pltpu.CompilerParams(has_side_effects=_EFFECT),
    )(*ins, after)
    return res[0], res[1], list(srcs), list(res[2:2 + n]), res[-1]


def _split_wait(name, copies, started, after):
    send_sems, recv_sems, srcs, lands, _ = started
    n = len(srcs)
    ncopy = send_sems.shape[0] // n
    after = list(after) if isinstance(after, (list, tuple)) else [after]

    def body(*refs):
        src_refs, land_refs = refs[:n], refs[n:2 * n]
        send_sems, recv_sems = refs[2 * n], refs[2 * n + 1]
        for r in range(n):
            for cp in copies(src_refs[r], land_refs[r], send_sems, recv_sems, r * ncopy):
                cp.wait_send()
                cp.wait_recv()

    res = pl.pallas_call(
        body, name=name, out_shape=[pltpu.HBM(a.shape, a.dtype) for a in lands],
        in_specs=[_HBM] * (2 * n) + [_SEM, _SEM] + [_ANY] * len(after), out_specs=[_HBM] * n,
        input_output_aliases={n + i: i for i in range(n)},
        compiler_params=pltpu.CompilerParams(has_side_effects=_EFFECT),
    )(*srcs, *lands, send_sems, recv_sems, *after)
    return list(srcs), list(res)


def _adamw(name, w, m, v, parts):
    R, C = w.shape
    S = parts.shape[0]
    tr = _tile(R, max(SUBLANES, (256 * 1024) // C), SUBLANES)

    def body(w_ref, m_ref, v_ref, p_ref, g_ref, d_ref, nm_ref, nv_ref):
        g = p_ref[0].astype(F32)
        for s in range(1, S):
            g = g + p_ref[s].astype(F32)
        m2 = ADAM_B1 * m_ref[...] + (1.0 - ADAM_B1) * g
        v2 = ADAM_B2 * v_ref[...] + (1.0 - ADAM_B2) * (g * g)
        m_hat = m2 / (1.0 - ADAM_B1 ** ADAM_STEP)
        v_hat = v2 / (1.0 - ADAM_B2 ** ADAM_STEP)
        g_ref[...] = g
        d_ref[...] = -ADAM_LR * (m_hat / (jnp.sqrt(v_hat) + ADAM_EPS) + ADAM_WD * w_ref[...])
        nm_ref[...] = m2
        nv_ref[...] = v2

    blk = BS((tr, C), lambda i: (i, 0))
    return _pc(name, body, (R // tr,), [w, m, v, parts],
               [blk, blk, blk, BS((S, tr, C), lambda i: (0, i, 0))],
               [_sds((R, C), F32)] * 4, [blk] * 4)


def _ssm_disc(lam_re, lam_im, log_step, br_t, bi_t):
    lr = jnp.minimum(lam_re, MIN_NEG_REAL)
    li = lam_im
    dt = jnp.exp(log_step)
    mag = jnp.exp(lr * dt)
    ang = li * dt
    ab_re = mag * jnp.cos(ang)
    ab_im = mag * jnp.sin(ang)
    nr = ab_re - 1.0
    ni = ab_im
    den = lr * lr + li * li
    f_re = (nr * lr + ni * li) / den
    f_im = (ni * lr - nr * li) / den
    bb_re = f_re[None] * br_t - f_im[None] * bi_t
    bb_im = f_re[None] * bi_t + f_im[None] * br_t
    return ab_re, ab_im, bb_re, bb_im


def _ssm_param_fwd(lam_re, lam_im, log_step, br_t, bi_t):
    G, P = lam_re.shape

    def body(lr_ref, li_ref, ls_ref, br_ref, bi_ref, pw_re_ref, pw_im_ref, bbr_ref, bbi_ref):
        ab_re, ab_im, bb_re, bb_im = _ssm_disc(lr_ref[...], li_ref[...], ls_ref[...], br_ref[...], bi_ref[...])
        bbr_ref[...] = bb_re
        bbi_ref[...] = bb_im
        pr, pi = ab_re, ab_im
        for r in range(SUBLANES):
            pw_re_ref[r] = pr
            pw_im_ref[r] = pi
            pr, pi = pr * ab_re - pi * ab_im, pr * ab_im + pi * ab_re

    full = lambda a: BS(a.shape, lambda i: (0,) * a.ndim)
    ins = [lam_re, lam_im, log_step, br_t, bi_t]
    outs = [_sds((SUBLANES, G, P), F32)] * 2 + [_sds(br_t.shape, F32)] * 2
    return _pc("ssm_param_fwd", body, (1,), ins, [full(a) for a in ins], outs, [full(o) for o in outs])


def _ssm_param_bwd(lam_re, lam_im, log_step, br_t, bi_t, d_ab_re, d_ab_im, d_bbr, d_bbi):
    def body(lr_ref, li_ref, ls_ref, br_ref, bi_ref, dar_ref, dai_ref, dbr_ref, dbi_ref,
             o_lr, o_li, o_ls, o_br, o_bi):
        prim = (lr_ref[...], li_ref[...], ls_ref[...], br_ref[...], bi_ref[...])
        _, vjp = jax.vjp(_ssm_disc, *prim)
        dar = dar_ref[0]
        dai = dai_ref[0]
        for k in range(1, dar_ref.shape[0]):
            dar = dar + dar_ref[k]
            dai = dai + dai_ref[k]
        g = vjp((dar, dai, dbr_ref[...], dbi_ref[...]))
        for r, v in zip((o_lr, o_li, o_ls, o_br, o_bi), g):
            r[...] = v

    full = lambda a: BS(a.shape, lambda i: (0,) * a.ndim)
    ins = [lam_re, lam_im, log_step, br_t, bi_t, d_ab_re, d_ab_im, d_bbr, d_bbi]
    outs = [_sds(a.shape, F32) for a in (lam_re, lam_im, log_step, br_t, bi_t)]
    return _pc("ssm_param_bwd", body, (1,), ins, [full(a) for a in ins], outs, [full(o) for o in outs])


def _bcast_row(ref, r, w):
    return jnp.broadcast_to(ref[pl.ds(r, 1), :], (SUBLANES, w))


def _pick_row(x, row, r):
    return jnp.broadcast_to(jnp.sum(jnp.where(row == r, x, 0.0), axis=0, keepdims=True), x.shape)


def _scan_fwd(bu_re, bu_im, pw_re, pw_im, nseq, L):
    N, SL = bu_re.shape
    W = _tile(SL, 256)
    unroll = math.gcd(L // SUBLANES, SCAN_UNROLL)

    def body(bre_ref, bim_ref, pre_ref, pim_ref, sre_ref, sim_ref):
        pre, pim = pre_ref[...], pim_ref[...]
        steps = [(k, _bcast_row(pre_ref, k - 1, W), _bcast_row(pim_ref, k - 1, W)) for k in (1, 2, 4)]
        row = lax.broadcasted_iota(jnp.int32, (SUBLANES, W), 0)

        def step(i, carry):
            cr, ci = carry
            r0 = pl.multiple_of(i * SUBLANES, SUBLANES)
            xr = bre_ref[pl.ds(r0, SUBLANES), :]
            xi = bim_ref[pl.ds(r0, SUBLANES), :]
            for k, ar, ai in steps:
                sr = pltpu.roll(xr, k, axis=0)
                si = pltpu.roll(xi, k, axis=0)
                keep = row >= k
                xr, xi = (xr + jnp.where(keep, ar * sr - ai * si, 0.0),
                          xi + jnp.where(keep, ar * si + ai * sr, 0.0))
            xr, xi = xr + (pre * cr - pim * ci), xi + (pre * ci + pim * cr)
            sre_ref[pl.ds(r0, SUBLANES), :] = xr
            sim_ref[pl.ds(r0, SUBLANES), :] = xi
            return _pick_row(xr, row, SUBLANES - 1), _pick_row(xi, row, SUBLANES - 1)

        def group(g, carry):
            for u in range(unroll):
                carry = step(g * unroll + u, carry)
            return carry

        zero = jnp.zeros((SUBLANES, W), F32)
        lax.fori_loop(0, L // SUBLANES // unroll, group, (zero, zero))

    blk = BS((L, W), lambda s, j: (s, j))
    pw = BS((SUBLANES, W), lambda s, j: (0, j))
    return _pc("ssm_scan_fwd", body, (nseq, SL // W), [bu_re, bu_im, pw_re, pw_im], [blk, blk, pw, pw],
               [_sds((N, SL), F32)] * 2, [blk, blk])


def _scan_bwd(ds_re, ds_im, s_re, s_im, pw_re, pw_im, pwf_re, pwf_im, nseq, L):
    N, SL = ds_re.shape
    W = _tile(SL, 256)
    nt = L // SUBLANES
    unroll = math.gcd(nt, SCAN_UNROLL)

    def body(dsr_ref, dsi_ref, sre_ref, sim_ref, pre_ref, pim_ref, fre_ref, fim_ref,
             lre_ref, lim_ref, dar_ref, dai_ref):
        fre, fim = fre_ref[...], -fim_ref[...]
        steps = [(k, _bcast_row(pre_ref, k - 1, W), -_bcast_row(pim_ref, k - 1, W)) for k in (1, 2, 4)]
        row = lax.broadcasted_iota(jnp.int32, (SUBLANES, W), 0)

        def step(ii, carry):
            cr, ci, acr, aci = carry
            i = nt - 1 - ii
            r0 = pl.multiple_of(i * SUBLANES, SUBLANES)
            xr = dsr_ref[pl.ds(r0, SUBLANES), :]
            xi = dsi_ref[pl.ds(r0, SUBLANES), :]
            for k, ar, ai in steps:
                sr = pltpu.roll(xr, SUBLANES - k, axis=0)
                si = pltpu.roll(xi, SUBLANES - k, axis=0)
                keep = row < SUBLANES - k
                xr, xi = (xr + jnp.where(keep, ar * sr - ai * si, 0.0),
                          xi + jnp.where(keep, ar * si + ai * sr, 0.0))
            xr, xi = xr + (fre * cr - fim * ci), xi + (fre * ci + fim * cr)
            lre_ref[pl.ds(r0, SUBLANES), :] = xr
            lim_ref[pl.ds(r0, SUBLANES), :] = xi
            p0 = pl.multiple_of(jnp.maximum(i - 1, 0) * SUBLANES, SUBLANES)
            has_prev = i > 0
            spr = jnp.where(row == 0,
                            jnp.where(has_prev, pltpu.roll(sre_ref[pl.ds(p0, SUBLANES), :], 1, axis=0), 0.0),
                            pltpu.roll(sre_ref[pl.ds(r0, SUBLANES), :], 1, axis=0))
            spi = jnp.where(row == 0,
                            jnp.where(has_prev, pltpu.roll(sim_ref[pl.ds(p0, SUBLANES), :], 1, axis=0), 0.0),
                            pltpu.roll(sim_ref[pl.ds(r0, SUBLANES), :], 1, axis=0))
            acr = acr + (xr * spr + xi * spi)
            aci = aci + (xi * spr - xr * spi)
            return _pick_row(xr, row, 0), _pick_row(xi, row, 0), acr, aci

        def group(g, carry):
            for u in range(unroll):
                carry = step(g * unroll + u, carry)
            return carry

        zero = jnp.zeros((SUBLANES, W), F32)
        _, _, acr, aci = lax.fori_loop(0, nt // unroll, group, (zero, zero, zero, zero))
        dar_ref[...] = acr
        dai_ref[...] = aci

    blk = BS((L, W), lambda s, j: (s, j))
    pw = BS((SUBLANES, W), lambda s, j: (0, j))
    da = BS((None, SUBLANES, W), lambda s, j: (s, 0, j))
    return _pc("ssm_scan_bwd", body, (nseq, SL // W),
               [ds_re, ds_im, s_re, s_im, pw_re, pw_im, pwf_re, pwf_im], [blk] * 4 + [pw] * 4,
               [_sds((N, SL), F32)] * 2 + [_sds((nseq, SUBLANES, SL), F32)] * 2, [blk, blk, da, da])


def _pool_select(g, vals):
    return jnp.where(g == 0, vals[0], jnp.where(g == 1, vals[1], jnp.where(g == 2, vals[2], vals[3])))


def _pool_fwd(proj, col0, DP, nseq, L):
    N = proj.shape[0]
    PG = DP // len(POOL_WINDOWS)
    W = _tile(PG, 256)

    def body(v_ref, z_ref):
        g = pl.program_id(1) // (PG // W)
        v = v_ref[...]
        row = lax.broadcasted_iota(jnp.int32, (L, W), 0)
        sums, s, k = [], v, 1
        for _ in POOL_WINDOWS:
            s = s + jnp.where(row >= k, pltpu.roll(s, k, axis=0), 0.0)
            sums.append(s)
            k *= 2
        win = _pool_select(g, [float(w) for w in POOL_WINDOWS])
        cnt = jnp.minimum((row + 1).astype(F32), win)
        z_ref[...] = (_pool_select(g, sums) / cnt - v).astype(z_ref.dtype)

    return _pc("pool_fwd", body, (nseq, DP // W), [proj], [BS((L, W), lambda s, j: (s, col0 // W + j))],
               _sds((N, DP), BF16), BS((L, W), lambda s, j: (s, j)))


def _pool_bwd(dz, nseq, L):
    N, DP = dz.shape
    PG = DP // len(POOL_WINDOWS)
    W = _tile(PG, 256)

    def body(dz_ref, dv_ref):
        g = pl.program_id(1) // (PG // W)
        d = dz_ref[...]
        row = lax.broadcasted_iota(jnp.int32, (L, W), 0)
        win = _pool_select(g, [float(w) for w in POOL_WINDOWS])
        s = d / jnp.minimum((row + 1).astype(F32), win)
        sums, k = [], 1
        for _ in POOL_WINDOWS:
            s = s + jnp.where(row < L - k, pltpu.roll(s, L - k, axis=0), 0.0)
            sums.append(s)
            k *= 2
        dv_ref[...] = (_pool_select(g, sums) - d).astype(dv_ref.dtype)

    blk = BS((L, W), lambda s, j: (s, j))
    return _pc("pool_bwd", body, (nseq, DP // W), [dz], [blk], _sds((N, DP), BF16), blk)


def _rstd(x):
    return lax.rsqrt(jnp.mean(x * x, axis=-1, keepdims=True) + EPS)


def _norm_bwd(dy, xhat, rstd, gain):
    t = dy * gain
    return rstd * (t - xhat * jnp.mean(t * xhat, axis=-1, keepdims=True))


def _pre_norm(x, g1):
    N, D = x.shape
    tr = _tile(N, 128, SUBLANES)

    def body(x_ref, g_ref, a_ref):
        xv = x_ref[...]
        a_ref[...] = (xv * _rstd(xv) * g_ref[...]).astype(a_ref.dtype)

    row = BS((tr, D), lambda i: (i, 0))
    vec = BS((1, D), lambda i: (0, 0))
    return _pc("pre_norm", body, (N // tr,), [x, g1], [row, vec], _sds((N, D), BF16), row)


def _mid_norm(x, o, g2, g3):
    N, D = x.shape
    tr = _tile(N, 128, SUBLANES)

    def body(x_ref, o_ref, g2_ref, g3_ref, h1_ref, c_ref):
        ov = o_ref[...]
        h1 = x_ref[...] + ov * _rstd(ov) * g2_ref[...]
        h1_ref[...] = h1
        c_ref[...] = (h1 * _rstd(h1) * g3_ref[...]).astype(c_ref.dtype)

    row = BS((tr, D), lambda i: (i, 0))
    vec = BS((1, D), lambda i: (0, 0))
    return _pc("mid_norm", body, (N // tr,), [x, o, g2, g3], [row, row, vec, vec],
               [_sds((N, D), F32), _sds((N, D), BF16)], [row, row])


def _post_ffn(h1, dn, tgt, g4):
    N, D = h1.shape
    tr = _tile(N, 128, SUBLANES)

    def body(h1_ref, dn_ref, t_ref, g_ref, dh2_ref, ddn_ref, lossv_ref, dg_ref):
        first = pl.program_id(0) == 0
        dnv = dn_ref[...]
        rstd = _rstd(dnv)
        xhat = dnv * rstd
        gain = g_ref[...]
        err = (h1_ref[...] + xhat * gain) - t_ref[...]
        dh2 = err / float(D)
        dh2_ref[...] = dh2
        ddn_ref[...] = _norm_bwd(dh2, xhat, rstd, gain).astype(ddn_ref.dtype)
        _rowsum_into(lossv_ref, first, err * err)
        _rowsum_into(dg_ref, first, dh2 * xhat)

    row = BS((tr, D), lambda i: (i, 0))
    vec = BS((1, D), lambda i: (0, 0))
    return _pc("post_ffn", body, (N // tr,), [h1, dn, tgt, g4], [row, row, row, vec],
               [_sds((N, D), F32), _sds((N, D), BF16), _sds((1, D), F32), _sds((1, D), F32)], [row, row, vec, vec])


def _mid_bwd(dh2, dc, h1, o, g2, g3, deps=()):
    N, D = h1.shape
    tr = _tile(N, 128, SUBLANES)

    def body(dh2_ref, dc_ref, h1_ref, o_ref, g2_ref, g3_ref, dh1_ref, do_ref, dg2_ref, dg3_ref):
        first = pl.program_id(0) == 0
        h1 = h1_ref[...]
        r3 = _rstd(h1)
        hc = h1 * r3
        dcv = dc_ref[...]
        dh1 = dh2_ref[...] + _norm_bwd(dcv, hc, r3, g3_ref[...])
        dh1_ref[...] = dh1
        ov = o_ref[...]
        r2 = _rstd(ov)
        ho = ov * r2
        do_ref[...] = _norm_bwd(dh1, ho, r2, g2_ref[...]).astype(do_ref.dtype)
        _rowsum_into(dg3_ref, first, dcv * hc)
        _rowsum_into(dg2_ref, first, dh1 * ho)

    row = BS((tr, D), lambda i: (i, 0))
    vec = BS((1, D), lambda i: (0, 0))
    return _pc("mid_bwd", body, (N // tr,), [dh2, dc, h1, o, g2, g3], [row] * 4 + [vec, vec],
               [_sds((N, D), F32), _sds((N, D), BF16), _sds((1, D), F32), _sds((1, D), F32)], [row, row, vec, vec],
               deps=deps)


def _pre_bwd(x, da, dh1, g1):
    N, D = x.shape
    tr = _tile(N, 128, SUBLANES)

    def body(x_ref, da_ref, dh1_ref, g_ref, dx_ref, dg_ref):
        first = pl.program_id(0) == 0
        xv = x_ref[...]
        r1 = _rstd(xv)
        xh = xv * r1
        dav = da_ref[...]
        dx_ref[...] = dh1_ref[...] + _norm_bwd(dav, xh, r1, g_ref[...])
        _rowsum_into(dg_ref, first, dav * xh)

    row = BS((tr, D), lambda i: (i, 0))
    vec = BS((1, D), lambda i: (0, 0))
    return _pc("pre_bwd", body, (N // tr,), [x, da, dh1, g1], [row, row, row, vec],
               [_sds((N, D), F32), _sds((1, D), F32)], [row, vec])


def _conv_rows(x_ref, halo_ref, first):
    x = x_ref[...]
    tr = x.shape[0]
    xx = jnp.concatenate([jnp.where(first, 0.0, halo_ref[...]), x], axis=0)
    x1 = pltpu.roll(xx, 1, axis=0)[SUBLANES:]
    x2 = pltpu.roll(xx, 2, axis=0)[SUBLANES:]
    del tr
    return x, x1, x2


def _conv_apply(rows, w_ref, b_ref):
    x, x1, x2 = rows
    return ((b_ref[...] + x2 * w_ref[pl.ds(0, 1), :]) + x1 * w_ref[pl.ds(1, 1), :]) + x * w_ref[pl.ds(2, 1), :]


def _gate_specs(N, FC, TR, half):
    tile = BS((None, TR, FC), lambda jj, i: (jj + half, i, 0))
    halo = BS((None, SUBLANES, FC), lambda jj, i: (jj + half, jnp.maximum(i * (TR // SUBLANES) - 1, 0), 0))
    cw = BS((None, 3, FC), lambda jj, i: (jj + half, 0, 0))
    cb = BS((None, 1, FC), lambda jj, i: (jj + half, 0, 0))
    return tile, halo, cw, cb


def _gate_fwd(up_pre, cw, cb, L):
    nb, N, FC = up_pre.shape
    half = nb // 2
    TR = _tile(L, 128, SUBLANES)

    def body(xa_ref, ha_ref, wa_ref, ba_ref, xb_ref, hb_ref, wb_ref, bb_ref, f_ref):
        first = (pl.program_id(1) % (L // TR)) == 0
        ua = _conv_apply(_conv_rows(xa_ref, ha_ref, first), wa_ref, ba_ref)
        ub = _conv_apply(_conv_rows(xb_ref, hb_ref, first), wb_ref, bb_ref)
        f_ref[...] = (_gelu(ua) * ub).astype(f_ref.dtype)

    sa, sb = _gate_specs(N, FC, TR, 0), _gate_specs(N, FC, TR, half)
    return _pc("gate_fwd", body, (half, N // TR), [up_pre, up_pre, cw, cb] * 2, list(sa) + list(sb),
               _sds((half, N, FC), BF16), BS((None, TR, FC), lambda jj, i: (jj, i, 0)))


def _gate_bwd(up_pre, cw, cb, df, L, deps=()):
    nb, N, FC = up_pre.shape
    half = nb // 2
    TR = _tile(L, 128, SUBLANES)

    def body(xa_ref, ha_ref, wa_ref, ba_ref, xb_ref, hb_ref, wb_ref, bb_ref, df_ref, dup_ref, dw_ref, dbias_ref):
        i = pl.program_id(1)
        first_row = i == 0
        first = (i % (L // TR)) == 0
        ra = _conv_rows(xa_ref, ha_ref, first)
        rb = _conv_rows(xb_ref, hb_ref, first)
        ua = _conv_apply(ra, wa_ref, ba_ref)
        ub = _conv_apply(rb, wb_ref, bb_ref)
        dfv = df_ref[...].astype(F32)
        dua = dfv * ub * _gelu_grad(ua)
        dub = dfv * _gelu(ua)
        dup_ref[0] = dua
        dup_ref[1] = dub
        for h, (rows, du) in enumerate(((ra, dua), (rb, dub))):
            x, x1, x2 = rows
            _rowsum_into(dbias_ref.at[h], first_row, du)
            for k, xs in enumerate((x2, x1, x)):
                _rowsum_into(dw_ref.at[h, pl.ds(k, 1), :], first_row, du * xs)

    sa, sb = _gate_specs(N, FC, TR, 0), _gate_specs(N, FC, TR, half)
    tile = BS((None, TR, FC), lambda jj, i: (jj, i, 0))
    both = BS((2, None, TR, FC), lambda jj, i: (0, jj, i, 0))
    dw = BS((2, None, 3, FC), lambda jj, i: (0, jj, 0, 0))
    dbias = BS((2, None, 1, FC), lambda jj, i: (0, jj, 0, 0))
    return _pc("gate_bwd", body, (half, N // TR), [up_pre, up_pre, cw, cb] * 2 + [df], list(sa) + list(sb) + [tile],
               [_sds((2, half, N, FC), F32), _sds((2, half, 3, FC), F32), _sds((2, half, 1, FC), F32)],
               [both, dw, dbias], deps=deps)


def _conv_bwd(dup, cw, L):
    nb, N, FC = dup.shape
    TR = _tile(L, 128, SUBLANES)
    nrb = N // SUBLANES

    def body(x_ref, h_ref, w_ref, o_ref):
        last = ((pl.program_id(1) + 1) % (L // TR)) == 0
        x = x_ref[...]
        xx = jnp.concatenate([x, jnp.where(last, 0.0, h_ref[...])], axis=0)
        x1 = pltpu.roll(xx, TR + SUBLANES - 1, axis=0)[:TR]
        x2 = pltpu.roll(xx, TR + SUBLANES - 2, axis=0)[:TR]
        o_ref[...] = (x * w_ref[pl.ds(2, 1), :] + x1 * w_ref[pl.ds(1, 1), :] + x2 * w_ref[pl.ds(0, 1), :]
                      ).astype(o_ref.dtype)

    tile = BS((None, TR, FC), lambda jj, i: (jj, i, 0))
    halo = BS((None, SUBLANES, FC), lambda jj, i: (jj, jnp.minimum((i + 1) * (TR // SUBLANES), nrb - 1), 0))
    w = BS((None, 3, FC), lambda jj, i: (jj, 0, 0))
    return _pc("conv_bwd", body, (nb, N // TR), [dup, dup, cw], [tile, halo, w], _sds((nb, N, FC), BF16), tile)


def _pack(arrs):
    parts = []
    for a in arrs:
        flat = a.reshape(-1).astype(F32)
        pad = (-flat.shape[0]) % (SUBLANES * LANES)
        parts.append(jnp.pad(flat, (0, pad)))
    return jnp.concatenate(parts).reshape(-1, LANES)


def _unpack(packed, shapes):
    flat = packed.reshape(-1)
    out, off = [], 0
    for s in shapes:
        n = math.prod(s)
        out.append(flat[off:off + n].reshape(s))
        off += n + ((-n) % (SUBLANES * LANES))
    return out


def _small_sum(gathered, loss_rows, d_model):
    S, R, C = gathered.shape

    def body(p_ref, tot_ref, loss_ref):
        t = p_ref[0]
        for s in range(1, S):
            t = t + p_ref[s]
        tot_ref[...] = t
        loss_ref[...] = jnp.full((1, 1), 0.5 / d_model, F32) * jnp.sum(t[:loss_rows])

    return _pc("small_sum", body, (1,), [gathered], [BS((S, R, C), lambda i: (0, 0, 0))],
               [_sds((R, C), F32), _sds((1, 1), F32)], [BS((R, C), lambda i: (0, 0)), BS((1, 1), lambda i: (0, 0))])


def _block_diag_in(bb_t, nch):
    J, G, P = bb_t.shape
    gl = G // nch
    b = bb_t.reshape(J, nch, gl, P).transpose(1, 0, 2, 3)
    eye = jnp.eye(gl, dtype=F32)
    w = eye[None, :, None, :, None] * b[:, None, :, :, :]
    return w.reshape(nch, gl * J, gl * P)


def _block_diag_in_grad(dw, J, G, P):
    nch = dw.shape[0]
    gl = G // nch
    d = dw.reshape(nch, gl, J, gl, P)
    d = jnp.einsum("cgjgp->jcgp", d)
    return d.reshape(J, G, P)


def _block_diag_out(c, nch):
    G, J, P = c.shape
    gl = G // nch
    cc = c.reshape(nch, gl, J, P).transpose(0, 1, 3, 2)
    eye = jnp.eye(gl, dtype=F32)
    w = cc[:, :, :, None, :] * eye[None, :, None, :, None]
    return w.reshape(nch, gl * P, gl * J)


def _block_diag_out_grad(dw, G, J, P):
    nch = dw.shape[0]
    gl = G // nch
    d = dw.reshape(nch, gl, P, gl, J)
    d = jnp.einsum("cgpgj->cgjp", d)
    return d.reshape(G, J, P)


def kernel(x, norm_pre_mix, w_in, ssm_lambda_re, ssm_lambda_im, ssm_log_step, ssm_b_re, ssm_b_im, ssm_c_re, ssm_c_im, ssm_d, ssm_glu_w, ssm_glu_b, pool_w, pool_b, pool_scale, w_branch_ssm, w_branch_pool, w_out, norm_post_mix, norm_pre_ffn, w_up, ffn_conv_w, ffn_conv_b, w_down, norm_post_ffn, loss_target, m_norm_pre_mix, m_w_in, m_ssm_lambda_re, m_ssm_lambda_im, m_ssm_log_step, m_ssm_b_re, m_ssm_b_im, m_ssm_c_re, m_ssm_c_im, m_ssm_d, m_ssm_glu_w, m_ssm_glu_b, m_pool_w, m_pool_b, m_pool_scale, m_w_branch_ssm, m_w_branch_pool, m_w_out, m_norm_post_mix, m_norm_pre_ffn, m_w_up, m_ffn_conv_w, m_ffn_conv_b, m_w_down, m_norm_post_ffn, v_norm_pre_mix, v_w_in, v_ssm_lambda_re, v_ssm_lambda_im, v_ssm_log_step, v_ssm_b_re, v_ssm_b_im, v_ssm_c_re, v_ssm_c_im, v_ssm_d, v_ssm_glu_w, v_ssm_glu_b, v_pool_w, v_pool_b, v_pool_scale, v_w_branch_ssm, v_w_branch_pool, v_w_out, v_norm_post_mix, v_norm_pre_ffn, v_w_up, v_ffn_conv_w, v_ffn_conv_b, v_w_down, v_norm_post_ffn):
    args = dict(locals())
    names = ["norm_pre_mix", "w_in", "ssm_lambda_re", "ssm_lambda_im", "ssm_log_step", "ssm_b_re", "ssm_b_im",
             "ssm_c_re", "ssm_c_im", "ssm_d", "ssm_glu_w", "ssm_glu_b", "pool_w", "pool_b", "pool_scale",
             "w_branch_ssm", "w_branch_pool", "w_out", "norm_post_mix", "norm_pre_ffn", "w_up", "ffn_conv_w",
             "ffn_conv_b", "w_down", "norm_post_ffn"]

    nseq, L, D = x.shape
    N = nseq * L
    U = D // NDEV
    DS = ssm_d.shape[1]
    DP = pool_scale.shape[1]
    G, P, J = ssm_b_re.shape[1:]
    SL = G * P
    CH = GROUPS_PER_CHUNK * J
    CS = GROUPS_PER_CHUNK * P
    NCH = DS // CH
    NPG = len(POOL_WINDOWS)
    PG = DP // NPG
    FC = w_up.shape[2]
    NB = NDEV
    HB = NB // 2
    F2 = NB * FC
    dev = _slot(_mesh_pos())
    tm = _tile(N, 1024)
    tm2 = _tile(N, 512)

    x2 = x.reshape(N, D)
    tgt = loss_target.reshape(N, D)

    def bf(t):
        return t.astype(BF16)

    def g_start(tag, group, after):
        return _split_start("gather_start_" + tag, _gather_copies, 3, group,
                            [_sds((NDEV,) + s.shape, s.dtype) for s in group], after)

    def g_land(tag, started, after):
        srcs, lands = _split_wait("gather_wait_" + tag, _gather_copies, started, after)
        return _split_start("d2d_start_" + tag, _d2d_copies, 4, srcs, lands, srcs[0])

    def g_finish(tag, d2d, after):
        srcs, lands = _split_wait("d2d_wait_" + tag, _d2d_copies, d2d, after)
        return [lax.dynamic_update_index_in_dim(l, s, dev, 0) for l, s in zip(lands, srcs)]

    def x_start(tag, group):
        return _split_start("exchange_start_" + tag, _exchange_copies, NDEV - 1, group,
                            [_sds(g.shape, g.dtype) for g in group], group[0])

    def x_finish(tag, started, after):
        srcs, lands = _split_wait("exchange_wait_" + tag, _exchange_copies, started, after)
        own = [lax.dynamic_index_in_dim(s, dev, 0, keepdims=False) for s in srcs]
        return [lax.dynamic_update_index_in_dim(l, o, dev, 0) for l, o in zip(lands, own)]

    st_in = g_start("in", [bf(w_in[0])], x2)
    st_mix = g_start("mix", [bf(ssm_glu_w[0]), bf(pool_w[0]), pool_b[0], ffn_conv_w[0]], st_in[4])
    conv_b_blk = ffn_conv_b.reshape(NB, 1, FC)

    lam_re, lam_im = ssm_lambda_re[0], ssm_lambda_im[0]
    log_step = ssm_log_step.reshape(G, 1)
    br_t = ssm_b_re[0].transpose(2, 0, 1)
    bi_t = ssm_b_im[0].transpose(2, 0, 1)
    pw_re3, pw_im3, bb_re, bb_im = _ssm_param_fwd(lam_re, lam_im, log_step, br_t, bi_t)
    pw_re, pw_im = pw_re3.reshape(SUBLANES, SL), pw_im3.reshape(SUBLANES, SL)
    pwf_re, pwf_im = pw_re[::-1], pw_im[::-1]
    WB = jnp.concatenate([_block_diag_in(bb_re, NCH), _block_diag_in(bb_im, NCH)], axis=2).astype(BF16)
    WCre = _block_diag_out(ssm_c_re[0], NCH).astype(BF16)
    WCim = _block_diag_out(-ssm_c_im[0], NCH).astype(BF16)
    a = _pre_norm(x2, norm_pre_mix)
    small_names = ["norm_pre_mix", "norm_post_mix", "norm_pre_ffn", "norm_post_ffn", "ssm_lambda_re", "ssm_lambda_im",
                   "ssm_log_step", "ssm_b_re", "ssm_b_im", "ssm_c_re", "ssm_c_im", "ssm_d", "ssm_glu_b", "pool_scale",
                   "pool_b", "ffn_conv_w", "ffn_conv_b"]
    sm_w, sm_m, sm_v = (_pack([args[p + n] for n in small_names]) for p in ("", "m_", "v_"))
    g_br = [bf(w_branch_ssm[0]), bf(w_branch_pool[0]), bf(w_out[0])]
    g_up, g_down = [bf(w_up[0])], [bf(w_down[0])]
    early = [WB, WCre, WCim, pwf_re, pwf_im, a, sm_w, sm_m, sm_v] + g_br + g_up + g_down

    d_in = g_land("in", st_in, [st_mix[4]] + early)
    st_br = g_start("branch", g_br, d_in[4])
    st_up = g_start("up", g_up, st_br[4])
    st_down = g_start("down", g_down, st_up[4])
    d_mix = g_land("mix", st_mix, st_down[4])
    (Win,) = g_finish("in", d_in, d_mix[4])

    nq = 3 * NDEV
    (proj,) = _fused_matmul(
        "in_proj", (N // tm, nq, 1),
        [(a, BS((tm, D), lambda i, q, k: (i, 0)), Win, BS((None, D, U), lambda i, q, k: (q // 3, 0, q % 3)), "nn", 0)],
        [(tm, U)], [], [(_sds((N, 3 * D), F32), BS((tm, U), lambda i, q, k: (i, q)))],
        _store(lambda accs: accs))
    d_br = g_land("branch", st_br, proj)

    bu_re, bu_im = _fused_matmul(
        "ssm_in", (N // tm2, NCH, 1),
        [(proj, BS((tm2, CH), lambda i, c, k: (i, c)), WB, BS((None, CH, 2 * CS), lambda i, c, k: (c, 0, 0)), "nn", 0)],
        [(tm2, 2 * CS)], [],
        [(_sds((N, SL), F32), BS((tm2, CS), lambda i, c, k: (i, c)))] * 2,
        _store(lambda accs: (accs[0][:, :CS], accs[0][:, CS:])), deps=[d_br[4]])
    s_re, s_im = _scan_fwd(bu_re, bu_im, pw_re, pw_im, nseq, L)

    def ssm_out_epi(ids, accs, ex, o):
        u_ref, d_ref = ex
        y0 = accs[0] + d_ref[...] * u_ref[...]
        o[0][...] = y0
        o[1][...] = _gelu(y0).astype(BF16)

    y0, y1 = _fused_matmul(
        "ssm_out", (N // tm2, NCH, 1),
        [(s_re, BS((tm2, CS), lambda i, c, k: (i, c)), WCre, BS((None, CS, CH), lambda i, c, k: (c, 0, 0)), "nn", 0),
         (s_im, BS((tm2, CS), lambda i, c, k: (i, c)), WCim, BS((None, CS, CH), lambda i, c, k: (c, 0, 0)), "nn", 0)],
        [(tm2, CH)],
        [(proj, BS((tm2, CH), lambda i, c, k: (i, c))), (ssm_d, BS((1, CH), lambda i, c, k: (0, c)))],
        [(_sds((N, DS), F32), BS((tm2, CH), lambda i, c, k: (i, c))),
         (_sds((N, DS), BF16), BS((tm2, CH), lambda i, c, k: (i, c)))],
        ssm_out_epi)

    Wglu, Wpool, pool_b_all, conv_w_all = g_finish("mix", d_mix, y1)
    Wglu = Wglu.reshape(DS, DS)
    Wpool = Wpool.transpose(1, 0, 2, 3).reshape(NPG, PG, PG)
    pool_b_full = pool_b_all.transpose(1, 0, 2).reshape(1, DP)
    tn_s = _tile(DS, 512)

    def glu_epi(ids, accs, ex, o):
        y0_ref, b_ref = ex
        zg = accs[0] + b_ref[...]
        o[0][...] = zg
        o[1][...] = (_gelu(y0_ref[...]) * _sigmoid(zg)).astype(BF16)

    zg, ys = _fused_matmul(
        "ssm_glu", (N // tm, DS // tn_s, 1),
        [(y1, BS((tm, DS), lambda i, j, k: (i, 0)), Wglu, BS((DS, tn_s), lambda i, j, k: (0, j)), "nn", 0)],
        [(tm, tn_s)],
        [(y0, BS((tm, tn_s), lambda i, j, k: (i, j))), (ssm_glu_b, BS((1, tn_s), lambda i, j, k: (0, j)))],
        [(_sds((N, DS), F32), BS((tm, tn_s), lambda i, j, k: (i, j))),
         (_sds((N, DS), BF16), BS((tm, tn_s), lambda i, j, k: (i, j)))],
        glu_epi)

    z = _pool_fwd(proj, DS, DP, nseq, L)

    def pool_mm_epi(ids, accs, ex, o):
        b_ref, sc_ref = ex
        q = accs[0] + b_ref[...]
        o[0][...] = q
        o[1][...] = (q * sc_ref[...]).astype(BF16)

    qp, yp = _fused_matmul(
        "pool_mm", (N // tm, NPG, 1),
        [(z, BS((tm, PG), lambda i, g, k: (i, g)), Wpool, BS((None, PG, PG), lambda i, g, k: (g, 0, 0)), "nn", 0)],
        [(tm, PG)],
        [(pool_b_full, BS((1, PG), lambda i, g, k: (0, g))), (pool_scale, BS((1, PG), lambda i, g, k: (0, g)))],
        [(_sds((N, DP), F32), BS((tm, PG), lambda i, g, k: (i, g))),
         (_sds((N, DP), BF16), BS((tm, PG), lambda i, g, k: (i, g)))],
        pool_mm_epi)

    Wbs, Wbp, Wout = g_finish("branch", d_br, yp)
    d_up = g_land("up", st_up, Wbs)
    Wout = Wout.reshape(D, D)
    gs_blk = BS((tm2, U), lambda i, q, k: (i, (DS + DP) // U + q))
    gp_blk = BS((tm2, U), lambda i, q, k: (i, (DS + DP + D) // U + q))
    out_blk = BS((tm2, U), lambda i, q, k: (i, q))

    def branch_epi(ids, accs, ex, o):
        gs_ref, gp_ref = ex
        o[0][...] = accs[0]
        o[1][...] = accs[1]
        o[2][...] = (_sigmoid(gs_ref[...]) * accs[0] + _sigmoid(gp_ref[...]) * accs[1]).astype(BF16)

    Ys, Yp, merged = _fused_matmul(
        "branch", (N // tm2, NDEV, 1),
        [(ys, BS((tm2, DS), lambda i, q, k: (i, 0)), Wbs, BS((None, DS, U), lambda i, q, k: (q, 0, 0)), "nn", 0),
         (yp, BS((tm2, DP), lambda i, q, k: (i, 0)), Wbp, BS((None, DP, U), lambda i, q, k: (q, 0, 0)), "nn", 1)],
        [(tm2, U), (tm2, U)],
        [(proj, gs_blk), (proj, gp_blk)],
        [(_sds((N, D), F32), out_blk), (_sds((N, D), F32), out_blk), (_sds((N, D), BF16), out_blk)],
        branch_epi, deps=[d_up[4]])

    tn_d = _tile(D, 512)
    (o_mix,) = _fused_matmul(
        "out_proj", (N // tm, D // tn_d, 1),
        [(merged, BS((tm, D), lambda i, j, k: (i, 0)), Wout, BS((D, tn_d), lambda i, j, k: (0, j)), "nn", 0)],
        [(tm, tn_d)], [], [(_sds((N, D), F32), BS((tm, tn_d), lambda i, j, k: (i, j)))],
        _store(lambda accs: accs))
    h1, c = _mid_norm(x2, o_mix, norm_post_mix, norm_pre_ffn)

    (Wup,) = g_finish("up", d_up, c)
    d_down = g_land("down", st_down, Wup)
    tk_d = _tile(D, 1024)
    tk_up = _tile(D, 2048)
    (up_pre,) = _fused_matmul(
        "ffn_up", (N // tm2, NB, D // tk_up),
        [(c, BS((tm2, tk_up), lambda i, j, k: (i, k)), Wup, BS((None, tk_up, FC), lambda i, j, k: (j, k, 0)), "nn", 0)],
        [(tm2, FC)], [], [(_sds((NB, N, FC), F32), BS((None, tm2, FC), lambda i, j, k: (j, i, 0)))],
        _store(lambda accs: accs), deps=[d_down[4]])
    f = _gate_fwd(up_pre, conv_w_all, conv_b_blk, L)
    (Wdown,) = g_finish("down", d_down, f)
    Wdown = Wdown.reshape(HB, FC, D)
    tn_d2 = _tile(D, 1024)
    (dn,) = _fused_matmul(
        "ffn_down", (N // tm2, D // tn_d2, HB),
        [(f, BS((None, tm2, FC), lambda i, j, k: (k, i, 0)), Wdown, BS((None, FC, tn_d2), lambda i, j, k: (k, 0, j)), "nn", 0)],
        [(tm2, tn_d2)], [], [(_sds((N, D), F32), BS((tm2, tn_d2), lambda i, j, k: (i, j)))],
        _store(lambda accs: accs))
    dh2, d_dn, lossv, dg4 = _post_ffn(h1, dn, tgt, norm_post_ffn)

    (df,) = _fused_matmul(
        "ffn_down_dx", (N // tm2, HB, D // tk_d),
        [(d_dn, BS((tm2, tk_d), lambda i, j, k: (i, k)), Wdown, BS((None, FC, tk_d), lambda i, j, k: (j, 0, k)), "nt", 0)],
        [(tm2, FC)], [], [(_sds((HB, N, FC), BF16), BS((None, tm2, FC), lambda i, j, k: (j, i, 0)))],
        _store(lambda accs: accs))
    tk_n = _tile(N, 1024)
    (gW_down,) = _fused_matmul(
        "ffn_down_dw", (HB, D // tn_d, N // tk_n),
        [(f, BS((None, tk_n, FC), lambda j, n, k: (j, k, 0)), d_dn, BS((tk_n, tn_d), lambda j, n, k: (k, n)), "tn", 0)],
        [(FC, tn_d)], [], [(_sds((HB, FC, D), BF16), BS((None, FC, tn_d), lambda j, n, k: (j, 0, n)))],
        _store(lambda accs: accs))
    x_down = x_start("down", [gW_down.reshape(NDEV, FC // 2, D)])
    dup, dcw, dcb = _gate_bwd(up_pre, conv_w_all, conv_b_blk, df, L, deps=[x_down[4]])
    dpre = _conv_bwd(dup.reshape(NB, N, FC), conv_w_all, L)
    (dc,) = _fused_matmul(
        "ffn_up_dx", (N // tm2, D // tn_d2, NB),
        [(dpre, BS((None, tm2, FC), lambda i, j, k: (k, i, 0)), Wup, BS((None, tn_d2, FC), lambda i, j, k: (k, j, 0)), "nt", 0)],
        [(tm2, tn_d2)], [], [(_sds((N, D), F32), BS((tm2, tn_d2), lambda i, j, k: (i, j)))],
        _store(lambda accs: accs))
    tm_d = _tile(D, 512)
    (gW_up,) = _fused_matmul(
        "ffn_up_dw", (NB, D // tm_d, N // tk_n),
        [(c, BS((tk_n, tm_d), lambda j, n, k: (k, n)), dpre, BS((None, tk_n, FC), lambda j, n, k: (j, k, 0)), "tn", 0)],
        [(tm_d, FC)], [], [(_sds((NB, D, FC), BF16), BS((None, tm_d, FC), lambda j, n, k: (j, n, 0)))],
        _store(lambda accs: accs))
    x_up = x_start("up", [gW_up])

    dh1, d_o, dg2, dg3 = _mid_bwd(dh2, dc, h1, o_mix, norm_post_mix, norm_pre_ffn, deps=[x_up[4]])

    def dmerged_epi(ids, accs, ex, o):
        gs_ref, gp_ref, ys_ref, yp_ref = ex
        dm = accs[0]
        sg_s, sg_p = _sigmoid(gs_ref[...]), _sigmoid(gp_ref[...])
        o[0][...] = (dm * sg_s).astype(BF16)
        o[1][...] = (dm * sg_p).astype(BF16)
        o[2][...] = (dm * ys_ref[...] * sg_s * (1.0 - sg_s)).astype(BF16)
        o[3][...] = (dm * yp_ref[...] * sg_p * (1.0 - sg_p)).astype(BF16)

    dYs, dYp, dgs, dgp = _fused_matmul(
        "out_proj_dx", (N // tm2, NDEV, 1),
        [(d_o, BS((tm2, D), lambda i, q, k: (i, 0)), Wout, BS((U, D), lambda i, q, k: (q, 0)), "nt", 0)],
        [(tm2, U)],
        [(proj, gs_blk), (proj, gp_blk), (Ys, out_blk), (Yp, out_blk)],
        [(_sds((N, D), BF16), out_blk)] * 4,
        dmerged_epi)
    (gW_out,) = _fused_matmul(
        "out_proj_dw", (D // tm_d, D // tn_d, 1),
        [(merged, BS((N, tm_d), lambda i, j, k: (0, i)), d_o, BS((N, tn_d), lambda i, j, k: (0, j)), "tn", 0)],
        [(tm_d, tn_d)], [], [(_sds((D, D), BF16), BS((tm_d, tn_d), lambda i, j, k: (i, j)))],
        _store(lambda accs: accs))
    tm_s = _tile(DS, 512)
    gW_bs, gW_bp = _fused_matmul(
        "branch_dw", (DS // tm_s, NDEV, 1),
        [(ys, BS((N, tm_s), lambda i, q, k: (0, i)), dYs, BS((N, U), lambda i, q, k: (0, q)), "tn", 0),
         (yp, BS((N, tm_s), lambda i, q, k: (0, i)), dYp, BS((N, U), lambda i, q, k: (0, q)), "tn", 1)],
        [(tm_s, U), (tm_s, U)], [],
        [(_sds((NDEV, DS, U), BF16), BS((None, tm_s, U), lambda i, q, k: (q, i, 0)))] * 2,
        _store(lambda accs: accs))
    x_br = x_start("branch", [gW_bs, gW_bp, gW_out.reshape(NDEV, U, D)])

    tn_p = _tile(PG, 512)

    def dyp_epi(ids, accs, ex, o):
        q_ref, sc_ref = ex
        first = ids[1] == 0
        dyp = accs[0]
        dq = dyp * sc_ref[...]
        o[0][...] = dq.astype(BF16)
        _rowsum_into(o[1], first, dyp * q_ref[...])
        _rowsum_into(o[2], first, dq)

    dq, d_pscale, d_pb = _fused_matmul(
        "branch_pool_dx", (DP // tn_p, N // tm, NDEV),
        [(dYp, BS((tm, U), lambda j, i, k: (i, k)), Wbp, BS((None, tn_p, U), lambda j, i, k: (k, j, 0)), "nt", 0)],
        [(tm, tn_p)],
        [(qp, BS((tm, tn_p), lambda j, i, k: (i, j))), (pool_scale, BS((1, tn_p), lambda j, i, k: (0, j)))],
        [(_sds((N, DP), BF16), BS((tm, tn_p), lambda j, i, k: (i, j))),
         (_sds((1, DP), F32), BS((1, tn_p), lambda j, i, k: (0, j))),
         (_sds((1, DP), F32), BS((1, tn_p), lambda j, i, k: (0, j)))],
        dyp_epi, deps=[x_br[4]])
    (dz,) = _fused_matmul(
        "pool_mm_dx", (N // tm, NPG, 1),
        [(dq, BS((tm, PG), lambda i, g, k: (i, g)), Wpool, BS((None, PG, PG), lambda i, g, k: (g, 0, 0)), "nt", 0)],
        [(tm, PG)], [], [(_sds((N, DP), F32), BS((tm, PG), lambda i, g, k: (i, g)))],
        _store(lambda accs: accs))
    (gW_pool,) = _fused_matmul(
        "pool_mm_dw", (NPG, 1),
        [(z, BS((N, PG), lambda g, k: (0, g)), dq, BS((N, PG), lambda g, k: (0, g)), "tn", 0)],
        [(PG, PG)], [], [(_sds((NPG, PG, PG), BF16), BS((None, PG, PG), lambda g, k: (g, 0, 0)))],
        _store(lambda accs: accs))
    du_pool = _pool_bwd(dz, nseq, L)

    def dys_epi(ids, accs, ex, o):
        zg_ref, y0_ref = ex
        first = ids[1] == 0
        dys = accs[0]
        sg = _sigmoid(zg_ref[...])
        dzg = dys * _gelu(y0_ref[...]) * sg * (1.0 - sg)
        o[0][...] = dzg.astype(BF16)
        o[1][...] = dys * sg
        _rowsum_into(o[2], first, dzg)

    dzg, dy1_direct, d_glu_b = _fused_matmul(
        "branch_ssm_dx", (DS // tn_s, N // tm, NDEV),
        [(dYs, BS((tm, U), lambda j, i, k: (i, k)), Wbs, BS((None, tn_s, U), lambda j, i, k: (k, j, 0)), "nt", 0)],
        [(tm, tn_s)],
        [(zg, BS((tm, tn_s), lambda j, i, k: (i, j))), (y0, BS((tm, tn_s), lambda j, i, k: (i, j)))],
        [(_sds((N, DS), BF16), BS((tm, tn_s), lambda j, i, k: (i, j))),
         (_sds((N, DS), F32), BS((tm, tn_s), lambda j, i, k: (i, j))),
         (_sds((1, DS), F32), BS((1, tn_s), lambda j, i, k: (0, j)))],
        dys_epi)
    (gW_glu,) = _fused_matmul(
        "ssm_glu_dw", (DS // tm_s, DS // tn_s, 1),
        [(y1, BS((N, tm_s), lambda i, j, k: (0, i)), dzg, BS((N, tn_s), lambda i, j, k: (0, j)), "tn", 0)],
        [(tm_s, tn_s)], [], [(_sds((DS, DS), BF16), BS((tm_s, tn_s), lambda i, j, k: (i, j)))],
        _store(lambda accs: accs))
    x_mix = x_start("mix", [gW_glu.reshape(NDEV, DS // NDEV, DS),
                            gW_pool.reshape(NPG, NDEV, PG // NDEV, PG).transpose(1, 0, 2, 3)])

    tn_c = _tile(DS, CH)

    def dy0_epi(ids, accs, ex, o):
        d1_ref, y0_ref, u_ref = ex
        first = ids[1] == 0
        dy0 = (accs[0] + d1_ref[...]) * _gelu_grad(y0_ref[...])
        o[0][...] = dy0
        _rowsum_into(o[1], first, dy0 * u_ref[...])

    dy0, d_ssm_d = _fused_matmul(
        "ssm_glu_dx", (DS // tn_c, N // tm, 1),
        [(dzg, BS((tm, DS), lambda j, i, k: (i, 0)), Wglu, BS((tn_c, DS), lambda j, i, k: (j, 0)), "nt", 0)],
        [(tm, tn_c)],
        [(dy1_direct, BS((tm, tn_c), lambda j, i, k: (i, j))), (y0, BS((tm, tn_c), lambda j, i, k: (i, j))),
         (proj, BS((tm, tn_c), lambda j, i, k: (i, j)))],
        [(_sds((N, DS), F32), BS((tm, tn_c), lambda j, i, k: (i, j))),
         (_sds((1, DS), F32), BS((1, tn_c), lambda j, i, k: (0, j)))],
        dy0_epi, deps=[x_mix[4]])

    ds_re, ds_im = _fused_matmul(
        "ssm_out_dx", (N // tm2, NCH, 1),
        [(dy0, BS((tm2, CH), lambda i, c, k: (i, c)), WCre, BS((None, CS, CH), lambda i, c, k: (c, 0, 0)), "nt", 0),
         (dy0, BS((tm2, CH), lambda i, c, k: (i, c)), WCim, BS((None, CS, CH), lambda i, c, k: (c, 0, 0)), "nt", 1)],
        [(tm2, CS), (tm2, CS)], [],
        [(_sds((N, SL), F32), BS((tm2, CS), lambda i, c, k: (i, c)))] * 2,
        _store(lambda accs: accs))
    lam_r, lam_i, d_ab_re, d_ab_im = _scan_bwd(ds_re, ds_im, s_re, s_im, pw_re, pw_im, pwf_re, pwf_im, nseq, L)

    def du_epi(ids, accs, ex, o):
        dy0_ref, d_ref = ex
        o[0][...] = (accs[0] + dy0_ref[...] * d_ref[...]).astype(BF16)

    (du_ssm,) = _fused_matmul(
        "ssm_in_dx", (N // tm2, NCH, 1),
        [(lam_r, BS((tm2, CS), lambda i, c, k: (i, c)), WB, BS((None, CH, CS), lambda i, c, k: (c, 0, 0)), "nt", 0),
         (lam_i, BS((tm2, CS), lambda i, c, k: (i, c)), WB, BS((None, CH, CS), lambda i, c, k: (c, 0, 1)), "nt", 0)],
        [(tm2, CH)],
        [(dy0, BS((tm2, CH), lambda i, c, k: (i, c))), (ssm_d, BS((1, CH), lambda i, c, k: (0, c)))],
        [(_sds((N, DS), BF16), BS((tm2, CH), lambda i, c, k: (i, c)))],
        du_epi)
    dproj = jnp.concatenate([du_ssm, du_pool, dgs, dgp], axis=1)
    (gW_in,) = _fused_matmul(
        "in_proj_dw", (D // tm_d, nq, 1),
        [(a, BS((N, tm_d), lambda i, q, k: (0, i)), dproj, BS((N, U), lambda i, q, k: (0, q)), "tn", 0)],
        [(tm_d, U)], [], [(_sds((NDEV, D, 3 * U), BF16), BS((None, tm_d, U), lambda i, q, k: (q // 3, i, q % 3)))],
        _store(lambda accs: accs))
    x_in = x_start("in", [gW_in])
    (da,) = _fused_matmul(
        "in_proj_dx", (N // tm, D // tn_d2, NDEV),
        [(dproj, BS((tm, 3 * U), lambda i, j, k: (i, k)), Win, BS((None, tn_d2, 3 * U), lambda i, j, k: (k, j, 0)), "nt", 0)],
        [(tm, tn_d2)], [], [(_sds((N, D), F32), BS((tm, tn_d2), lambda i, j, k: (i, j)))],
        _store(lambda accs: accs), deps=[x_in[4]])
    grad_x, dg1 = _pre_bwd(x2, da, dh1, norm_pre_mix)

    dWCre, dWCim = _fused_matmul(
        "ssm_out_dw", (NCH, N // tk_n),
        [(s_re, BS((tk_n, CS), lambda c, k: (k, c)), dy0, BS((tk_n, CH), lambda c, k: (k, c)), "tn", 0),
         (s_im, BS((tk_n, CS), lambda c, k: (k, c)), dy0, BS((tk_n, CH), lambda c, k: (k, c)), "tn", 1)],
        [(CS, CH), (CS, CH)], [],
        [(_sds((NCH, CS, CH), F32), BS((None, CS, CH), lambda c, k: (c, 0, 0)))] * 2,
        _store(lambda accs: accs), deps=[x_in[4]])
    dWBre, dWBim = _fused_matmul(
        "ssm_in_dw", (NCH, N // tk_n),
        [(proj, BS((tk_n, CH), lambda c, k: (k, c)), lam_r, BS((tk_n, CS), lambda c, k: (k, c)), "tn", 0),
         (proj, BS((tk_n, CH), lambda c, k: (k, c)), lam_i, BS((tk_n, CS), lambda c, k: (k, c)), "tn", 1)],
        [(CH, CS), (CH, CS)], [],
        [(_sds((NCH, CH, CS), F32), BS((None, CH, CS), lambda c, k: (c, 0, 0)))] * 2,
        _store(lambda accs: accs), deps=[x_in[4]])
    d_bbr = _block_diag_in_grad(dWBre, J, G, P)
    d_bbi = _block_diag_in_grad(dWBim, J, G, P)
    d_lam_re, d_lam_im, d_log_step, d_br_t, d_bi_t = _ssm_param_bwd(
        lam_re, lam_im, log_step, br_t, bi_t,
        d_ab_re.reshape(nseq * SUBLANES, G, P), d_ab_im.reshape(nseq * SUBLANES, G, P), d_bbr, d_bbi)
    d_c_re = _block_diag_out_grad(dWCre, G, J, P)
    d_c_im = -_block_diag_out_grad(dWCim, G, J, P)

    d_conv_w = dcw.reshape(NB, 3, FC).transpose(1, 0, 2).reshape(3, F2)
    d_conv_b = dcb.reshape(1, F2)
    small = {
        "norm_pre_mix": dg1, "norm_post_mix": dg2, "norm_pre_ffn": dg3, "norm_post_ffn": dg4,
        "ssm_lambda_re": d_lam_re[None], "ssm_lambda_im": d_lam_im[None], "ssm_log_step": d_log_step.reshape(1, G),
        "ssm_b_re": d_br_t.transpose(1, 2, 0)[None], "ssm_b_im": d_bi_t.transpose(1, 2, 0)[None],
        "ssm_c_re": d_c_re[None], "ssm_c_im": d_c_im[None],
        "ssm_d": d_ssm_d, "ssm_glu_b": d_glu_b, "pool_scale": d_pscale,
        "pool_b": d_pb.reshape(1, NPG, PG), "ffn_conv_w": d_conv_w[None], "ffn_conv_b": d_conv_b,
    }
    assert list(small) == small_names
    packed = _pack([lossv] + [small[n] for n in small_names])
    st_small = _split_start("small_start", _broadcast_copies, NDEV - 1, [packed],
                            [_sds((NDEV,) + packed.shape, packed.dtype)], packed)

    res = {}
    after = st_small[4]
    for tag, started, group in (("down", x_down, ["w_down"]), ("up", x_up, ["w_up"]),
                                ("branch", x_br, ["w_branch_ssm", "w_branch_pool", "w_out"]),
                                ("mix", x_mix, ["ssm_glu_w", "pool_w"]), ("in", x_in, ["w_in"])):
        for n, parts in zip(group, x_finish(tag, started, after)):
            shape = args[n].shape
            cols = shape[-1]
            flat = lambda t: t.reshape(-1, cols)
            g, dl, nm, nv = _adamw("adamw_" + n, flat(args[n]), flat(args["m_" + n]), flat(args["v_" + n]),
                                   parts.reshape(NDEV, -1, cols))
            res[n] = tuple(t.reshape(shape) for t in (g, dl, nm, nv))
            after = g

    srcs, lands = _split_wait("small_wait", _broadcast_copies, st_small, after)
    small_all = lax.dynamic_update_index_in_dim(lands[0], srcs[0], dev, 0)
    loss_rows = (D + SUBLANES * LANES - 1) // (SUBLANES * LANES) * SUBLANES
    total, loss = _small_sum(small_all, loss_rows, D)
    totals = dict(zip(small_names, _unpack(total, [lossv.shape] + [small[n].shape for n in small_names])[1:]))
    totals["pool_b"] = lax.dynamic_slice_in_dim(totals["pool_b"], dev * (PG // NDEV), PG // NDEV, axis=2)
    totals["ffn_conv_w"] = lax.dynamic_slice_in_dim(totals["ffn_conv_w"], dev * FC, FC, axis=2)
    sm_g = _pack([totals[n] for n in small_names])
    _, sm_d, sm_nm, sm_nv = _adamw("adamw_small", sm_w, sm_m, sm_v, sm_g[None])
    shapes = [args[n].shape for n in small_names]
    for n, dl, nm, nv in zip(small_names, _unpack(sm_d, shapes), _unpack(sm_nm, shapes), _unpack(sm_nv, shapes)):
        res[n] = (totals[n], dl, nm, nv)

    outs = [loss.reshape(()), grad_x.reshape(x.shape)]
    for k in range(4):
        outs += [res[n][k] for n in names]
    return tuple(outs)
```

```python
import functools
import math

import jax
import jax.numpy as jnp
from jax import lax
from jax.experimental import pallas as pl
from jax.experimental.pallas import tpu as pltpu

F32 = jnp.float32
BF16 = jnp.bfloat16
BS = pl.BlockSpec

NDEV = 8
SSM_GROUP = 16
SSM_STATE = 64
GROUPS_PER_CHUNK = 16
SCAN_UNROLL = 4
POOL_WINDOWS = (2, 4, 8, 16)
EPS = 1e-6
MIN_NEG_REAL = -1e-4
ADAM_LR, ADAM_B1, ADAM_B2, ADAM_EPS, ADAM_WD, ADAM_STEP = 0.001, 0.9, 0.999, 1e-08, 0.01, 10
LANES = 128
SUBLANES = 8
VMEM_LIMIT = 56 * 1024 * 1024

_DIMS = {"nn": (((1,), (0,)), ((), ())), "nt": (((1,), (1,)), ((), ())), "tn": (((0,), (0,)), ((), ()))}


def _tile(dim, pref, mult=LANES):
    if dim <= pref:
        return dim
    t = (pref // mult) * mult
    while t >= mult:
        if dim % t == 0:
            return t
        t -= mult
    return dim


def _pc(name, body, grid, ins, in_specs, outs, out_specs, scratch=(), deps=()):
    multi = isinstance(outs, (list, tuple))
    if deps:
        n_in, n_dep, inner = len(ins), len(deps), body

        def body(*refs):
            return inner(*refs[:n_in], *refs[n_in + n_dep:])

        ins = list(ins) + list(deps)
        in_specs = list(in_specs) + [BS(memory_space=pl.ANY)] * n_dep
    return pl.pallas_call(
        body, name=name, grid=grid, in_specs=list(in_specs),
        out_specs=list(out_specs) if multi else out_specs,
        out_shape=list(outs) if multi else outs, scratch_shapes=list(scratch),
        compiler_params=pltpu.CompilerParams(dimension_semantics=("arbitrary",) * len(grid),
                                             vmem_limit_bytes=VMEM_LIMIT),
    )(*ins)


def _sds(shape, dtype):
    return jax.ShapeDtypeStruct(tuple(shape), dtype)


def _gelu(x):
    k = math.sqrt(2.0 / math.pi)
    return 0.5 * x * (1.0 + jnp.tanh(k * (x + 0.044715 * (x * x * x))))


def _gelu_grad(x):
    k = math.sqrt(2.0 / math.pi)
    t = jnp.tanh(k * (x + 0.044715 * (x * x * x)))
    return 0.5 * (1.0 + t) + 0.5 * x * (1.0 - t * t) * (k * (1.0 + 3.0 * 0.044715 * x * x))


def _sigmoid(x):
    return jax.nn.sigmoid(x)


def _fused_matmul(name, grid, pairs, acc_shapes, extras, outs, epilogue, deps=()):
    n_p, n_e, n_o = len(pairs), len(extras), len(outs)
    rank = len(grid)
    nk = grid[-1]

    def body(*refs):
        ab = refs[:2 * n_p]
        ex = refs[2 * n_p:2 * n_p + n_e]
        o = refs[2 * n_p + n_e:2 * n_p + n_e + n_o]
        accs = refs[2 * n_p + n_e + n_o:]
        ids = [pl.program_id(d) for d in range(rank)]
        k = ids[-1]

        @pl.when(k == 0)
        def _():
            for acc in accs:
                acc[...] = jnp.zeros_like(acc)

        for p in range(n_p):
            a = ab[2 * p][...].astype(BF16)
            b = ab[2 * p + 1][...].astype(BF16)
            accs[pairs[p][5]][...] += lax.dot_general(a, b, _DIMS[pairs[p][4]], preferred_element_type=F32)

        @pl.when(k == nk - 1)
        def _():
            epilogue(ids, [acc[...] for acc in accs], ex, o)

    ins, in_specs = [], []
    for a, a_spec, b, b_spec, _, _ in pairs:
        ins += [a, b]
        in_specs += [a_spec, b_spec]
    for e, e_spec in extras:
        ins.append(e)
        in_specs.append(e_spec)
    res = _pc(name, body, grid, ins, in_specs, [s for s, _ in outs], [sp for _, sp in outs],
              scratch=[pltpu.VMEM(tuple(s), F32) for s in acc_shapes], deps=deps)
    return res


def _store(vals):
    def epilogue(ids, accs, ex, o):
        for r, v in zip(o, vals(accs)):
            r[...] = v.astype(r.dtype)
    return epilogue


def _rowsum_into(ref, first, v):
    s = jnp.sum(v, axis=0, keepdims=True)

    @pl.when(first)
    def _():
        ref[...] = s

    @pl.when(jnp.logical_not(first))
    def _():
        ref[...] += s


def _mesh_pos():
    return lax.axis_index("x"), lax.axis_index("y"), lax.axis_index("c")


def _slot(p):
    return 4 * p[0] + 2 * p[1] + p[2]


_HBM = BS(memory_space=pltpu.HBM)
_SEM = BS(memory_space=pltpu.SEMAPHORE)
_ANY = BS(memory_space=pl.ANY)
_EFFECT = pltpu.SideEffectType.DATAFLOW_SIDE_EFFECTING


def _other_chips(x, y):
    return [(1 - x, y), (x, 1 - y), (1 - x, 1 - y)]


def _all_peers(x, y, c):
    peers = []
    for k in range(1, NDEV):
        kx, ky, kc = (k >> 2) & 1, (k >> 1) & 1, k & 1
        peers.append((1 - x if kx else x, 1 - y if ky else y, 1 - c if kc else c))
    return peers


def _gather_copies(src, land, send_sems, recv_sems, base):
    x, y, c = _mesh_pos()
    return [pltpu.make_async_remote_copy(
        src_ref=src, dst_ref=land.at[_slot((x, y, c))],
        send_sem=send_sems.at[base + k], recv_sem=recv_sems.at[base + k],
        device_id=(*chip, c), device_id_type=pl.DeviceIdType.MESH) for k, chip in enumerate(_other_chips(x, y))]


def _d2d_copies(src, land, send_sems, recv_sems, base):
    x, y, c = _mesh_pos()
    blocks = [(x, y, c)] + [(*chip, c) for chip in _other_chips(x, y)]
    return [pltpu.make_async_remote_copy(
        src_ref=src if k == 0 else land.at[_slot(b)], dst_ref=land.at[_slot(b)],
        send_sem=send_sems.at[base + k], recv_sem=recv_sems.at[base + k],
        device_id=(x, y, 1 - c), device_id_type=pl.DeviceIdType.MESH) for k, b in enumerate(blocks)]


def _broadcast_copies(src, land, send_sems, recv_sems, base):
    x, y, c = _mesh_pos()
    return [pltpu.make_async_remote_copy(
        src_ref=src, dst_ref=land.at[_slot((x, y, c))],
        send_sem=send_sems.at[base + k], recv_sem=recv_sems.at[base + k],
        device_id=peer, device_id_type=pl.DeviceIdType.MESH) for k, peer in enumerate(_all_peers(x, y, c))]


def _exchange_copies(src, land, send_sems, recv_sems, base):
    x, y, c = _mesh_pos()
    return [pltpu.make_async_remote_copy(
        src_ref=src.at[_slot(peer)], dst_ref=land.at[_slot((x, y, c))],
        send_sem=send_sems.at[base + k], recv_sem=recv_sems.at[base + k],
        device_id=peer, device_id_type=pl.DeviceIdType.MESH) for k, peer in enumerate(_all_peers(x, y, c))]


def _split_start(name, copies, ncopy, srcs, land_shapes, after):
    n = len(srcs)

    def body(*refs):
        src_refs, land_refs = refs[:n], refs[n:2 * n]
        send_sems, recv_sems = refs[2 * n + 1], refs[2 * n + 2]
        token = refs[-1]
        for r in range(n):
            for cp in copies(src_refs[r], land_refs[r], send_sems, recv_sems, r * ncopy):
                cp.start()
        token[...] = jnp.zeros_like(token)

    lands = [s if isinstance(s, jax.Array) else pltpu.with_memory_space_constraint(lax.empty(s.shape, s.dtype), pltpu.HBM)
             for s in land_shapes]
    ins = list(srcs) + lands
    out_shape = ([pltpu.SemaphoreType.DMA((n * ncopy,)), pltpu.SemaphoreType.DMA((n * ncopy,))]
                 + [pltpu.HBM(a.shape, a.dtype) for a in lands]
                 + [_sds((SUBLANES, LANES), F32)])
    res = pl.pallas_call(
        body, name=name, out_shape=out_shape,
        in_specs=[_HBM] * (2 * n) + [_ANY], out_specs=[_SEM, _SEM] + [_HBM] * n + [BS(memory_space=pltpu.VMEM)],
        input_output_aliases={n + i: 2 + i for i in range(n)},
        compiler_params=pltpu.CompilerParams(has_side_effects=_EFFECT),
    )(*ins, after)
    return res[0], res[1], list(srcs), list(res[2:2 + n]), res[-1]


def _split_wait(name, copies, started, after):
    send_sems, recv_sems, srcs, lands, _ = started
    n = len(srcs)
    ncopy = send_sems.shape[0] // n
    after = list(after) if isinstance(after, (list, tuple)) else [after]

    def body(*refs):
        src_refs, land_refs = refs[:n], refs[n:2 * n]
        send_sems, recv_sems = refs[2 * n], refs[2 * n + 1]
        for r in range(n):
            for cp in copies(src_refs[r], land_refs[r], send_sems, recv_sems, r * ncopy):
                cp.wait_send()
                cp.wait_recv()

    res = pl.pallas_call(
        body, name=name, out_shape=[pltpu.HBM(a.shape, a.dtype) for a in lands],
        in_specs=[_HBM] * (2 * n) + [_SEM, _SEM] + [_ANY] * len(after), out_specs=[_HBM] * n,
        input_output_aliases={n + i: i for i in range(n)},
        compiler_params=pltpu.CompilerParams(has_side_effects=_EFFECT),
    )(*srcs, *lands, send_sems, recv_sems, *after)
    return list(srcs), list(res)


def _adamw(name, w, m, v, parts):
    R, C = w.shape
    S = parts.shape[0]
    tr = _tile(R, max(SUBLANES, (256 * 1024) // C), SUBLANES)

    def body(w_ref, m_ref, v_ref, p_ref, g_ref, d_ref, nm_ref, nv_ref):
        g = p_ref[0].astype(F32)
        for s in range(1, S):
            g = g + p_ref[s].astype(F32)
        m2 = ADAM_B1 * m_ref[...] + (1.0 - ADAM_B1) * g
        v2 = ADAM_B2 * v_ref[...] + (1.0 - ADAM_B2) * (g * g)
        m_hat = m2 / (1.0 - ADAM_B1 ** ADAM_STEP)
        v_hat = v2 / (1.0 - ADAM_B2 ** ADAM_STEP)
        g_ref[...] = g
        d_ref[...] = -ADAM_LR * (m_hat / (jnp.sqrt(v_hat) + ADAM_EPS) + ADAM_WD * w_ref[...])
        nm_ref[...] = m2
        nv_ref[...] = v2

    blk = BS((tr, C), lambda i: (i, 0))
    return _pc(name, body, (R // tr,), [w, m, v, parts],
               [blk, blk, blk, BS((S, tr, C), lambda i: (0, i, 0))],
               [_sds((R, C), F32)] * 4, [blk] * 4)


def _ssm_disc(lam_re, lam_im, log_step, br_t, bi_t):
    lr = jnp.minimum(lam_re, MIN_NEG_REAL)
    li = lam_im
    dt = jnp.exp(log_step)
    mag = jnp.exp(lr * dt)
    ang = li * dt
    ab_re = mag * jnp.cos(ang)
    ab_im = mag * jnp.sin(ang)
    nr = ab_re - 1.0
    ni = ab_im
    den = lr * lr + li * li
    f_re = (nr * lr + ni * li) / den
    f_im = (ni * lr - nr * li) / den
    bb_re = f_re[None] * br_t - f_im[None] * bi_t
    bb_im = f_re[None] * bi_t + f_im[None] * br_t
    return ab_re, ab_im, bb_re, bb_im


def _ssm_param_fwd(lam_re, lam_im, log_step, br_t, bi_t):
    G, P = lam_re.shape

    def body(lr_ref, li_ref, ls_ref, br_ref, bi_ref, pw_re_ref, pw_im_ref, bbr_ref, bbi_ref):
        ab_re, ab_im, bb_re, bb_im = _ssm_disc(lr_ref[...], li_ref[...], ls_ref[...], br_ref[...], bi_ref[...])
        bbr_ref[...] = bb_re
        bbi_ref[...] = bb_im
        pr, pi = ab_re, ab_im
        for r in range(SUBLANES):
            pw_re_ref[r] = pr
            pw_im_ref[r] = pi
            pr, pi = pr * ab_re - pi * ab_im, pr * ab_im + pi * ab_re

    full = lambda a: BS(a.shape, lambda i: (0,) * a.ndim)
    ins = [lam_re, lam_im, log_step, br_t, bi_t]
    outs = [_sds((SUBLANES, G, P), F32)] * 2 + [_sds(br_t.shape, F32)] * 2
    return _pc("ssm_param_fwd", body, (1,), ins, [full(a) for a in ins], outs, [full(o) for o in outs])


def _ssm_param_bwd(lam_re, lam_im, log_step, br_t, bi_t, d_ab_re, d_ab_im, d_bbr, d_bbi):
    def body(lr_ref, li_ref, ls_ref, br_ref, bi_ref, dar_ref, dai_ref, dbr_ref, dbi_ref,
             o_lr, o_li, o_ls, o_br, o_bi):
        prim = (lr_ref[...], li_ref[...], ls_ref[...], br_ref[...], bi_ref[...])
        _, vjp = jax.vjp(_ssm_disc, *prim)
        dar = dar_ref[0]
        dai = dai_ref[0]
        for k in range(1, dar_ref.shape[0]):
            dar = dar + dar_ref[k]
            dai = dai + dai_ref[k]
        g = vjp((dar, dai, dbr_ref[...], dbi_ref[...]))
        for r, v in zip((o_lr, o_li, o_ls, o_br, o_bi), g):
            r[...] = v

    full = lambda a: BS(a.shape, lambda i: (0,) * a.ndim)
    ins = [lam_re, lam_im, log_step, br_t, bi_t, d_ab_re, d_ab_im, d_bbr, d_bbi]
    outs = [_sds(a.shape, F32) for a in (lam_re, lam_im, log_step, br_t, bi_t)]
    return _pc("ssm_param_bwd", body, (1,), ins, [full(a) for a in ins], outs, [full(o) for o in outs])


def _bcast_row(ref, r, w):
    return jnp.broadcast_to(ref[pl.ds(r, 1), :], (SUBLANES, w))


def _pick_row(x, row, r):
    return jnp.broadcast_to(jnp.sum(jnp.where(row == r, x, 0.0), axis=0, keepdims=True), x.shape)


def _scan_fwd(bu_re, bu_im, pw_re, pw_im, nseq, L):
    N, SL = bu_re.shape
    W = _tile(SL, 256)
    unroll = math.gcd(L // SUBLANES, SCAN_UNROLL)

    def body(bre_ref, bim_ref, pre_ref, pim_ref, sre_ref, sim_ref):
        pre, pim = pre_ref[...], pim_ref[...]
        steps = [(k, _bcast_row(pre_ref, k - 1, W), _bcast_row(pim_ref, k - 1, W)) for k in (1, 2, 4)]
        row = lax.broadcasted_iota(jnp.int32, (SUBLANES, W), 0)

        def step(i, carry):
            cr, ci = carry
            r0 = pl.multiple_of(i * SUBLANES, SUBLANES)
            xr = bre_ref[pl.ds(r0, SUBLANES), :]
            xi = bim_ref[pl.ds(r0, SUBLANES), :]
            for k, ar, ai in steps:
                sr = pltpu.roll(xr, k, axis=0)
                si = pltpu.roll(xi, k, axis=0)
                keep = row >= k
                xr, xi = (xr + jnp.where(keep, ar * sr - ai * si, 0.0),
                          xi + jnp.where(keep, ar * si + ai * sr, 0.0))
            xr, xi = xr + (pre * cr - pim * ci), xi + (pre * ci + pim * cr)
            sre_ref[pl.ds(r0, SUBLANES), :] = xr
            sim_ref[pl.ds(r0, SUBLANES), :] = xi
            return _pick_row(xr, row, SUBLANES - 1), _pick_row(xi, row, SUBLANES - 1)

        def group(g, carry):
            for u in range(unroll):
                carry = step(g * unroll + u, carry)
            return carry

        zero = jnp.zeros((SUBLANES, W), F32)
        lax.fori_loop(0, L // SUBLANES // unroll, group, (zero, zero))

    blk = BS((L, W), lambda s, j: (s, j))
    pw = BS((SUBLANES, W), lambda s, j: (0, j))
    return _pc("ssm_scan_fwd", body, (nseq, SL // W), [bu_re, bu_im, pw_re, pw_im], [blk, blk, pw, pw],
               [_sds((N, SL), F32)] * 2, [blk, blk])


def _scan_bwd(ds_re, ds_im, s_re, s_im, pw_re, pw_im, pwf_re, pwf_im, nseq, L):
    N, SL = ds_re.shape
    W = _tile(SL, 256)
    nt = L // SUBLANES
    unroll = math.gcd(nt, SCAN_UNROLL)

    def body(dsr_ref, dsi_ref, sre_ref, sim_ref, pre_ref, pim_ref, fre_ref, fim_ref,
             lre_ref, lim_ref, dar_ref, dai_ref):
        fre, fim = fre_ref[...], -fim_ref[...]
        steps = [(k, _bcast_row(pre_ref, k - 1, W), -_bcast_row(pim_ref, k - 1, W)) for k in (1, 2, 4)]
        row = lax.broadcasted_iota(jnp.int32, (SUBLANES, W), 0)

        def step(ii, carry):
            cr, ci, acr, aci = carry
            i = nt - 1 - ii
            r0 = pl.multiple_of(i * SUBLANES, SUBLANES)
            xr = dsr_ref[pl.ds(r0, SUBLANES), :]
            xi = dsi_ref[pl.ds(r0, SUBLANES), :]
            for k, ar, ai in steps:
                sr = pltpu.roll(xr, SUBLANES - k, axis=0)
                si = pltpu.roll(xi, SUBLANES - k, axis=0)
                keep = row < SUBLANES - k
                xr, xi = (xr + jnp.where(keep, ar * sr - ai * si, 0.0),
                          xi + jnp.where(keep, ar * si + ai * sr, 0.0))
            xr, xi = xr + (fre * cr - fim * ci), xi + (fre * ci + fim * cr)
            lre_ref[pl.ds(r0, SUBLANES), :] = xr
            lim_ref[pl.ds(r0, SUBLANES), :] = xi
            p0 = pl.multiple_of(jnp.maximum(i - 1, 0) * SUBLANES, SUBLANES)
            has_prev = i > 0
            spr = jnp.where(row == 0,
                            jnp.where(has_prev, pltpu.roll(sre_ref[pl.ds(p0, SUBLANES), :], 1, axis=0), 0.0),
                            pltpu.roll(sre_ref[pl.ds(r0, SUBLANES), :], 1, axis=0))
            spi = jnp.where(row == 0,
                            jnp.where(has_prev, pltpu.roll(sim_ref[pl.ds(p0, SUBLANES), :], 1, axis=0), 0.0),
                            pltpu.roll(sim_ref[pl.ds(r0, SUBLANES), :], 1, axis=0))
            acr = acr + (xr * spr + xi * spi)
            aci = aci + (xi * spr - xr * spi)
            return _pick_row(xr, row, 0), _pick_row(xi, row, 0), acr, aci

        def group(g, carry):
            for u in range(unroll):
                carry = step(g * unroll + u, carry)
            return carry

        zero = jnp.zeros((SUBLANES, W), F32)
        _, _, acr, aci = lax.fori_loop(0, nt // unroll, group, (zero, zero, zero, zero))
        dar_ref[...] = acr
        dai_ref[...] = aci

    blk = BS((L, W), lambda s, j: (s, j))
    pw = BS((SUBLANES, W), lambda s, j: (0, j))
    da = BS((None, SUBLANES, W), lambda s, j: (s, 0, j))
    return _pc("ssm_scan_bwd", body, (nseq, SL // W),
               [ds_re, ds_im, s_re, s_im, pw_re, pw_im, pwf_re, pwf_im], [blk] * 4 + [pw] * 4,
               [_sds((N, SL), F32)] * 2 + [_sds((nseq, SUBLANES, SL), F32)] * 2, [blk, blk, da, da])


def _pool_select(g, vals):
    return jnp.where(g == 0, vals[0], jnp.where(g == 1, vals[1], jnp.where(g == 2, vals[2], vals[3])))


def _pool_fwd(proj, col0, DP, nseq, L):
    N = proj.shape[0]
    PG = DP // len(POOL_WINDOWS)
    W = _tile(PG, 256)

    def body(v_ref, z_ref):
        g = pl.program_id(1) // (PG // W)
        v = v_ref[...]
        row = lax.broadcasted_iota(jnp.int32, (L, W), 0)
        sums, s, k = [], v, 1
        for _ in POOL_WINDOWS:
            s = s + jnp.where(row >= k, pltpu.roll(s, k, axis=0), 0.0)
            sums.append(s)
            k *= 2
        win = _pool_select(g, [float(w) for w in POOL_WINDOWS])
        cnt = jnp.minimum((row + 1).astype(F32), win)
        z_ref[...] = (_pool_select(g, sums) / cnt - v).astype(z_ref.dtype)

    return _pc("pool_fwd", body, (nseq, DP // W), [proj], [BS((L, W), lambda s, j: (s, col0 // W + j))],
               _sds((N, DP), BF16), BS((L, W), lambda s, j: (s, j)))


def _pool_bwd(dz, nseq, L):
    N, DP = dz.shape
    PG = DP // len(POOL_WINDOWS)
    W = _tile(PG, 256)

    def body(dz_ref, dv_ref):
        g = pl.program_id(1) // (PG // W)
        d = dz_ref[...]
        row = lax.broadcasted_iota(jnp.int32, (L, W), 0)
        win = _pool_select(g, [float(w) for w in POOL_WINDOWS])
        s = d / jnp.minimum((row + 1).astype(F32), win)
        sums, k = [], 1
        for _ in POOL_WINDOWS:
            s = s + jnp.where(row < L - k, pltpu.roll(s, L - k, axis=0), 0.0)
            sums.append(s)
            k *= 2
        dv_ref[...] = (_pool_select(g, sums) - d).astype(dv_ref.dtype)

    blk = BS((L, W), lambda s, j: (s, j))
    return _pc("pool_bwd", body, (nseq, DP // W), [dz], [blk], _sds((N, DP), BF16), blk)


def _rstd(x):
    return lax.rsqrt(jnp.mean(x * x, axis=-1, keepdims=True) + EPS)


def _norm_bwd(dy, xhat, rstd, gain):
    t = dy * gain
    return rstd * (t - xhat * jnp.mean(t * xhat, axis=-1, keepdims=True))


def _pre_norm(x, g1):
    N, D = x.shape
    tr = _tile(N, 128, SUBLANES)

    def body(x_ref, g_ref, a_ref):
        xv = x_ref[...]
        a_ref[...] = (xv * _rstd(xv) * g_ref[...]).astype(a_ref.dtype)

    row = BS((tr, D), lambda i: (i, 0))
    vec = BS((1, D), lambda i: (0, 0))
    return _pc("pre_norm", body, (N // tr,), [x, g1], [row, vec], _sds((N, D), BF16), row)


def _mid_norm(x, o, g2, g3):
    N, D = x.shape
    tr = _tile(N, 128, SUBLANES)

    def body(x_ref, o_ref, g2_ref, g3_ref, h1_ref, c_ref):
        ov = o_ref[...]
        h1 = x_ref[...] + ov * _rstd(ov) * g2_ref[...]
        h1_ref[...] = h1
        c_ref[...] = (h1 * _rstd(h1) * g3_ref[...]).astype(c_ref.dtype)

    row = BS((tr, D), lambda i: (i, 0))
    vec = BS((1, D), lambda i: (0, 0))
    return _pc("mid_norm", body, (N // tr,), [x, o, g2, g3], [row, row, vec, vec],
               [_sds((N, D), F32), _sds((N, D), BF16)], [row, row])


def _post_ffn(h1, dn, tgt, g4):
    N, D = h1.shape
    tr = _tile(N, 128, SUBLANES)

    def body(h1_ref, dn_ref, t_ref, g_ref, dh2_ref, ddn_ref, lossv_ref, dg_ref):
        first = pl.program_id(0) == 0
        dnv = dn_ref[...]
        rstd = _rstd(dnv)
        xhat = dnv * rstd
        gain = g_ref[...]
        err = (h1_ref[...] + xhat * gain) - t_ref[...]
        dh2 = err / float(D)
        dh2_ref[...] = dh2
        ddn_ref[...] = _norm_bwd(dh2, xhat, rstd, gain).astype(ddn_ref.dtype)
        _rowsum_into(lossv_ref, first, err * err)
        _rowsum_into(dg_ref, first, dh2 * xhat)

    row = BS((tr, D), lambda i: (i, 0))
    vec = BS((1, D), lambda i: (0, 0))
    return _pc("post_ffn", body, (N // tr,), [h1, dn, tgt, g4], [row, row, row, vec],
               [_sds((N, D), F32), _sds((N, D), BF16), _sds((1, D), F32), _sds((1, D), F32)], [row, row, vec, vec])


def _mid_bwd(dh2, dc, h1, o, g2, g3, deps=()):
    N, D = h1.shape
    tr = _tile(N, 128, SUBLANES)

    def body(dh2_ref, dc_ref, h1_ref, o_ref, g2_ref, g3_ref, dh1_ref, do_ref, dg2_ref, dg3_ref):
        first = pl.program_id(0) == 0
        h1 = h1_ref[...]
        r3 = _rstd(h1)
        hc = h1 * r3
        dcv = dc_ref[...]
        dh1 = dh2_ref[...] + _norm_bwd(dcv, hc, r3, g3_ref[...])
        dh1_ref[...] = dh1
        ov = o_ref[...]
        r2 = _rstd(ov)
        ho = ov * r2
        do_ref[...] = _norm_bwd(dh1, ho, r2, g2_ref[...]).astype(do_ref.dtype)
        _rowsum_into(dg3_ref, first, dcv * hc)
        _rowsum_into(dg2_ref, first, dh1 * ho)

    row = BS((tr, D), lambda i: (i, 0))
    vec = BS((1, D), lambda i: (0, 0))
    return _pc("mid_bwd", body, (N // tr,), [dh2, dc, h1, o, g2, g3], [row] * 4 + [vec, vec],
               [_sds((N, D), F32), _sds((N, D), BF16), _sds((1, D), F32), _sds((1, D), F32)], [row, row, vec, vec],
               deps=deps)


def _pre_bwd(x, da, dh1, g1):
    N, D = x.shape
    tr = _tile(N, 128, SUBLANES)

    def body(x_ref, da_ref, dh1_ref, g_ref, dx_ref, dg_ref):
        first = pl.program_id(0) == 0
        xv = x_ref[...]
        r1 = _rstd(xv)
        xh = xv * r1
        dav = da_ref[...]
        dx_ref[...] = dh1_ref[...] + _norm_bwd(dav, xh, r1, g_ref[...])
        _rowsum_into(dg_ref, first, dav * xh)

    row = BS((tr, D), lambda i: (i, 0))
    vec = BS((1, D), lambda i: (0, 0))
    return _pc("pre_bwd", body, (N // tr,), [x, da, dh1, g1], [row, row, row, vec],
               [_sds((N, D), F32), _sds((1, D), F32)], [row, vec])


def _conv_rows(x_ref, halo_ref, first):
    x = x_ref[...]
    tr = x.shape[0]
    xx = jnp.concatenate([jnp.where(first, 0.0, halo_ref[...]), x], axis=0)
    x1 = pltpu.roll(xx, 1, axis=0)[SUBLANES:]
    x2 = pltpu.roll(xx, 2, axis=0)[SUBLANES:]
    del tr
    return x, x1, x2


def _conv_apply(rows, w_ref, b_ref):
    x, x1, x2 = rows
    return ((b_ref[...] + x2 * w_ref[pl.ds(0, 1), :]) + x1 * w_ref[pl.ds(1, 1), :]) + x * w_ref[pl.ds(2, 1), :]


def _gate_specs(N, FC, TR, half):
    tile = BS((None, TR, FC), lambda jj, i: (jj + half, i, 0))
    halo = BS((None, SUBLANES, FC), lambda jj, i: (jj + half, jnp.maximum(i * (TR // SUBLANES) - 1, 0), 0))
    cw = BS((None, 3, FC), lambda jj, i: (jj + half, 0, 0))
    cb = BS((None, 1, FC), lambda jj, i: (jj + half, 0, 0))
    return tile, halo, cw, cb


def _gate_fwd(up_pre, cw, cb, L):
    nb, N, FC = up_pre.shape
    half = nb // 2
    TR = _tile(L, 128, SUBLANES)

    def body(xa_ref, ha_ref, wa_ref, ba_ref, xb_ref, hb_ref, wb_ref, bb_ref, f_ref):
        first = (pl.program_id(1) % (L // TR)) == 0
        ua = _conv_apply(_conv_rows(xa_ref, ha_ref, first), wa_ref, ba_ref)
        ub = _conv_apply(_conv_rows(xb_ref, hb_ref, first), wb_ref, bb_ref)
        f_ref[...] = (_gelu(ua) * ub).astype(f_ref.dtype)

    sa, sb = _gate_specs(N, FC, TR, 0), _gate_specs(N, FC, TR, half)
    return _pc("gate_fwd", body, (half, N // TR), [up_pre, up_pre, cw, cb] * 2, list(sa) + list(sb),
               _sds((half, N, FC), BF16), BS((None, TR, FC), lambda jj, i: (jj, i, 0)))


def _gate_bwd(up_pre, cw, cb, df, L, deps=()):
    nb, N, FC = up_pre.shape
    half = nb // 2
    TR = _tile(L, 128, SUBLANES)

    def body(xa_ref, ha_ref, wa_ref, ba_ref, xb_ref, hb_ref, wb_ref, bb_ref, df_ref, dup_ref, dw_ref, dbias_ref):
        i = pl.program_id(1)
        first_row = i == 0
        first = (i % (L // TR)) == 0
        ra = _conv_rows(xa_ref, ha_ref, first)
        rb = _conv_rows(xb_ref, hb_ref, first)
        ua = _conv_apply(ra, wa_ref, ba_ref)
        ub = _conv_apply(rb, wb_ref, bb_ref)
        dfv = df_ref[...].astype(F32)
        dua = dfv * ub * _gelu_grad(ua)
        dub = dfv * _gelu(ua)
        dup_ref[0] = dua.astype(dup_ref.dtype)
        dup_ref[1] = dub.astype(dup_ref.dtype)
        for h, (rows, du) in enumerate(((ra, dua), (rb, dub))):
            x, x1, x2 = rows
            _rowsum_into(dbias_ref.at[h], first_row, du)
            for k, xs in enumerate((x2, x1, x)):
                _rowsum_into(dw_ref.at[h, pl.ds(k, 1), :], first_row, du * xs)

    sa, sb = _gate_specs(N, FC, TR, 0), _gate_specs(N, FC, TR, half)
    tile = BS((None, TR, FC), lambda jj, i: (jj, i, 0))
    both = BS((2, None, TR, FC), lambda jj, i: (0, jj, i, 0))
    dw = BS((2, None, 3, FC), lambda jj, i: (0, jj, 0, 0))
    dbias = BS((2, None, 1, FC), lambda jj, i: (0, jj, 0, 0))
    return _pc("gate_bwd", body, (half, N // TR), [up_pre, up_pre, cw, cb] * 2 + [df], list(sa) + list(sb) + [tile],
               [_sds((2, half, N, FC), BF16), _sds((2, half, 3, FC), F32), _sds((2, half, 1, FC), F32)],
               [both, dw, dbias], deps=deps)


def _conv_bwd(dup, cw, L):
    nb, N, FC = dup.shape
    TR = _tile(L, 128, 2 * SUBLANES)
    HR = 2 * SUBLANES
    nrb = N // HR

    def body(x_ref, h_ref, w_ref, o_ref):
        last = ((pl.program_id(1) + 1) % (L // TR)) == 0
        x = x_ref[...].astype(F32)
        xx = jnp.concatenate([x, jnp.where(last, 0.0, h_ref[...].astype(F32))], axis=0)
        x1 = pltpu.roll(xx, TR + HR - 1, axis=0)[:TR]
        x2 = pltpu.roll(xx, TR + HR - 2, axis=0)[:TR]
        o_ref[...] = (x * w_ref[pl.ds(2, 1), :] + x1 * w_ref[pl.ds(1, 1), :] + x2 * w_ref[pl.ds(0, 1), :]
                      ).astype(o_ref.dtype)

    tile = BS((None, TR, FC), lambda jj, i: (jj, i, 0))
    halo = BS((None, HR, FC), lambda jj, i: (jj, jnp.minimum((i + 1) * (TR // HR), nrb - 1), 0))
    w = BS((None, 3, FC), lambda jj, i: (jj, 0, 0))
    return _pc("conv_bwd", body, (nb, N // TR), [dup, dup, cw], [tile, halo, w], _sds((nb, N, FC), BF16), tile)


def _pack(arrs):
    parts = []
    for a in arrs:
        flat = a.reshape(-1).astype(F32)
        pad = (-flat.shape[0]) % (SUBLANES * LANES)
        parts.append(jnp.pad(flat, (0, pad)))
    return jnp.concatenate(parts).reshape(-1, LANES)


def _unpack(packed, shapes):
    flat = packed.reshape(-1)
    out, off = [], 0
    for s in shapes:
        n = math.prod(s)
        out.append(flat[off:off + n].reshape(s))
        off += n + ((-n) % (SUBLANES * LANES))
    return out


def _small_sum(gathered, loss_rows, d_model):
    S, R, C = gathered.shape

    def body(p_ref, tot_ref, loss_ref):
        t = p_ref[0]
        for s in range(1, S):
            t = t + p_ref[s]
        tot_ref[...] = t
        loss_ref[...] = jnp.full((1, 1), 0.5 / d_model, F32) * jnp.sum(t[:loss_rows])

    return _pc("small_sum", body, (1,), [gathered], [BS((S, R, C), lambda i: (0, 0, 0))],
               [_sds((R, C), F32), _sds((1, 1), F32)], [BS((R, C), lambda i: (0, 0)), BS((1, 1), lambda i: (0, 0))])


def _block_diag_in(bb_t, nch):
    J, G, P = bb_t.shape
    gl = G // nch
    b = bb_t.reshape(J, nch, gl, P).transpose(1, 0, 2, 3)
    eye = jnp.eye(gl, dtype=F32)
    w = eye[None, :, None, :, None] * b[:, None, :, :, :]
    return w.reshape(nch, gl * J, gl * P)


def _block_diag_in_grad(dw, J, G, P):
    nch = dw.shape[0]
    gl = G // nch
    d = dw.reshape(nch, gl, J, gl, P)
    d = jnp.einsum("cgjgp->jcgp", d)
    return d.reshape(J, G, P)


def _block_diag_out(c, nch):
    G, J, P = c.shape
    gl = G // nch
    cc = c.reshape(nch, gl, J, P).transpose(0, 1, 3, 2)
    eye = jnp.eye(gl, dtype=F32)
    w = cc[:, :, :, None, :] * eye[None, :, None, :, None]
    return w.reshape(nch, gl * P, gl * J)


def _block_diag_out_grad(dw, G, J, P):
    nch = dw.shape[0]
    gl = G // nch
    d = dw.reshape(nch, gl, P, gl, J)
    d = jnp.einsum("cgpgj->cgjp", d)
    return d.reshape(G, J, P)


def kernel(x, norm_pre_mix, w_in, ssm_lambda_re, ssm_lambda_im, ssm_log_step, ssm_b_re, ssm_b_im, ssm_c_re, ssm_c_im, ssm_d, ssm_glu_w, ssm_glu_b, pool_w, pool_b, pool_scale, w_branch_ssm, w_branch_pool, w_out, norm_post_mix, norm_pre_ffn, w_up, ffn_conv_w, ffn_conv_b, w_down, norm_post_ffn, loss_target, m_norm_pre_mix, m_w_in, m_ssm_lambda_re, m_ssm_lambda_im, m_ssm_log_step, m_ssm_b_re, m_ssm_b_im, m_ssm_c_re, m_ssm_c_im, m_ssm_d, m_ssm_glu_w, m_ssm_glu_b, m_pool_w, m_pool_b, m_pool_scale, m_w_branch_ssm, m_w_branch_pool, m_w_out, m_norm_post_mix, m_norm_pre_ffn, m_w_up, m_ffn_conv_w, m_ffn_conv_b, m_w_down, m_norm_post_ffn, v_norm_pre_mix, v_w_in, v_ssm_lambda_re, v_ssm_lambda_im, v_ssm_log_step, v_ssm_b_re, v_ssm_b_im, v_ssm_c_re, v_ssm_c_im, v_ssm_d, v_ssm_glu_w, v_ssm_glu_b, v_pool_w, v_pool_b, v_pool_scale, v_w_branch_ssm, v_w_branch_pool, v_w_out, v_norm_post_mix, v_norm_pre_ffn, v_w_up, v_ffn_conv_w, v_ffn_conv_b, v_w_down, v_norm_post_ffn):
    args = dict(locals())
    names = ["norm_pre_mix", "w_in", "ssm_lambda_re", "ssm_lambda_im", "ssm_log_step", "ssm_b_re", "ssm_b_im",
             "ssm_c_re", "ssm_c_im", "ssm_d", "ssm_glu_w", "ssm_glu_b", "pool_w", "pool_b", "pool_scale",
             "w_branch_ssm", "w_branch_pool", "w_out", "norm_post_mix", "norm_pre_ffn", "w_up", "ffn_conv_w",
             "ffn_conv_b", "w_down", "norm_post_ffn"]

    nseq, L, D = x.shape
    N = nseq * L
    U = D // NDEV
    DS = ssm_d.shape[1]
    DP = pool_scale.shape[1]
    G, P, J = ssm_b_re.shape[1:]
    SL = G * P
    CH = GROUPS_PER_CHUNK * J
    CS = GROUPS_PER_CHUNK * P
    NCH = DS // CH
    NPG = len(POOL_WINDOWS)
    PG = DP // NPG
    FC = w_up.shape[2]
    NB = NDEV
    HB = NB // 2
    F2 = NB * FC
    dev = _slot(_mesh_pos())
    tm = _tile(N, 1024)
    tm2 = _tile(N, 512)

    x2 = x.reshape(N, D)
    tgt = loss_target.reshape(N, D)

    def bf(t):
        return t.astype(BF16)

    def g_start(tag, group, after):
        return _split_start("gather_start_" + tag, _gather_copies, 3, group,
                            [_sds((NDEV,) + s.shape, s.dtype) for s in group], after)

    def g_land(tag, started, after):
        srcs, lands = _split_wait("gather_wait_" + tag, _gather_copies, started, after)
        return _split_start("d2d_start_" + tag, _d2d_copies, 4, srcs, lands, srcs[0])

    def g_finish(tag, d2d, after):
        srcs, lands = _split_wait("d2d_wait_" + tag, _d2d_copies, d2d, after)
        return [lax.dynamic_update_index_in_dim(l, s, dev, 0) for l, s in zip(lands, srcs)]

    def x_start(tag, group):
        return _split_start("exchange_start_" + tag, _exchange_copies, NDEV - 1, group,
                            [_sds(g.shape, g.dtype) for g in group], group[0])

    def x_finish(tag, started, after):
        srcs, lands = _split_wait("exchange_wait_" + tag, _exchange_copies, started, after)
        own = [lax.dynamic_index_in_dim(s, dev, 0, keepdims=False) for s in srcs]
        return [lax.dynamic_update_index_in_dim(l, o, dev, 0) for l, o in zip(lands, own)]

    st_in = g_start("in", [bf(w_in[0])], x2)
    st_mix = g_start("mix", [bf(ssm_glu_w[0]), bf(pool_w[0]), pool_b[0], ffn_conv_w[0]], st_in[4])
    conv_b_blk = ffn_conv_b.reshape(NB, 1, FC)

    (_, x2e, w_branch_ssm, w_branch_pool, w_out, w_up, w_down, ssm_lambda_re, ssm_lambda_im, ssm_log_step,
     ssm_b_re, ssm_b_im, ssm_c_re, ssm_c_im) = lax.optimization_barrier(
        (st_mix[4], x2, w_branch_ssm, w_branch_pool, w_out, w_up, w_down, ssm_lambda_re, ssm_lambda_im, ssm_log_step,
         ssm_b_re, ssm_b_im, ssm_c_re, ssm_c_im))
    lam_re, lam_im = ssm_lambda_re[0], ssm_lambda_im[0]
    log_step = ssm_log_step.reshape(G, 1)
    br_t = ssm_b_re[0].transpose(2, 0, 1)
    bi_t = ssm_b_im[0].transpose(2, 0, 1)
    pw_re3, pw_im3, bb_re, bb_im = _ssm_param_fwd(lam_re, lam_im, log_step, br_t, bi_t)
    pw_re, pw_im = pw_re3.reshape(SUBLANES, SL), pw_im3.reshape(SUBLANES, SL)
    pwf_re, pwf_im = pw_re[::-1], pw_im[::-1]
    WB = jnp.concatenate([_block_diag_in(bb_re, NCH), _block_diag_in(bb_im, NCH)], axis=2).astype(BF16)
    WCre = _block_diag_out(ssm_c_re[0], NCH).astype(BF16)
    WCim = _block_diag_out(-ssm_c_im[0], NCH).astype(BF16)
    a = _pre_norm(x2e, norm_pre_mix)
    small_names = ["norm_pre_mix", "norm_post_mix", "norm_pre_ffn", "norm_post_ffn", "ssm_lambda_re", "ssm_lambda_im",
                   "ssm_log_step", "ssm_b_re", "ssm_b_im", "ssm_c_re", "ssm_c_im", "ssm_d", "ssm_glu_b", "pool_scale",
                   "pool_b", "ffn_conv_w", "ffn_conv_b"]
    _, small_in = lax.optimization_barrier(
        (st_mix[4], [[args[p + n] for n in small_names] for p in ("", "m_", "v_")]))
    sm_w, sm_m, sm_v = (_pack(group) for group in small_in)
    g_br = [bf(w_branch_ssm[0]), bf(w_branch_pool[0]), bf(w_out[0])]
    g_up, g_down = [bf(w_up[0])], [bf(w_down[0])]
    early = [WB, WCre, WCim, pwf_re, pwf_im, a, sm_w, sm_m, sm_v] + g_br + g_up + g_down

    d_in = g_land("in", st_in, [st_mix[4]] + early)
    st_br = g_start("branch", g_br, d_in[4])
    st_up = g_start("up", g_up, st_br[4])
    st_down = g_start("down", g_down, st_up[4])
    d_mix = g_land("mix", st_mix, st_down[4])
    (Win,) = g_finish("in", d_in, d_mix[4])

    nq = 3 * NDEV
    (proj,) = _fused_matmul(
        "in_proj", (N // tm, nq, 1),
        [(a, BS((tm, D), lambda i, q, k: (i, 0)), Win, BS((None, D, U), lambda i, q, k: (q // 3, 0, q % 3)), "nn", 0)],
        [(tm, U)], [], [(_sds((N, 3 * D), F32), BS((tm, U), lambda i, q, k: (i, q)))],
        _store(lambda accs: accs))
    d_br = g_land("branch", st_br, proj)

    bu_re, bu_im = _fused_matmul(
        "ssm_in", (N // tm2, NCH, 1),
        [(proj, BS((tm2, CH), lambda i, c, k: (i, c)), WB, BS((None, CH, 2 * CS), lambda i, c, k: (c, 0, 0)), "nn", 0)],
        [(tm2, 2 * CS)], [],
        [(_sds((N, SL), F32), BS((tm2, CS), lambda i, c, k: (i, c)))] * 2,
        _store(lambda accs: (accs[0][:, :CS], accs[0][:, CS:])), deps=[d_br[4]])
    s_re, s_im = _scan_fwd(bu_re, bu_im, pw_re, pw_im, nseq, L)

    def ssm_out_epi(ids, accs, ex, o):
        u_ref, d_ref = ex
        y0 = accs[0] + d_ref[...] * u_ref[...]
        o[0][...] = y0
        o[1][...] = _gelu(y0).astype(BF16)

    y0, y1 = _fused_matmul(
        "ssm_out", (N // tm2, NCH, 1),
        [(s_re, BS((tm2, CS), lambda i, c, k: (i, c)), WCre, BS((None, CS, CH), lambda i, c, k: (c, 0, 0)), "nn", 0),
         (s_im, BS((tm2, CS), lambda i, c, k: (i, c)), WCim, BS((None, CS, CH), lambda i, c, k: (c, 0, 0)), "nn", 0)],
        [(tm2, CH)],
        [(proj, BS((tm2, CH), lambda i, c, k: (i, c))), (ssm_d, BS((1, CH), lambda i, c, k: (0, c)))],
        [(_sds((N, DS), F32), BS((tm2, CH), lambda i, c, k: (i, c))),
         (_sds((N, DS), BF16), BS((tm2, CH), lambda i, c, k: (i, c)))],
        ssm_out_epi)

    Wglu, Wpool, pool_b_all, conv_w_all = g_finish("mix", d_mix, y1)
    Wglu = Wglu.reshape(DS, DS)
    Wpool = Wpool.transpose(1, 0, 2, 3).reshape(NPG, PG, PG)
    pool_b_full = pool_b_all.transpose(1, 0, 2).reshape(1, DP)
    tn_s = _tile(DS, 512)

    def glu_epi(ids, accs, ex, o):
        y0_ref, b_ref = ex
        zg = accs[0] + b_ref[...]
        o[0][...] = zg
        o[1][...] = (_gelu(y0_ref[...]) * _sigmoid(zg)).astype(BF16)

    zg, ys = _fused_matmul(
        "ssm_glu", (N // tm, DS // tn_s, 1),
        [(y1, BS((tm, DS), lambda i, j, k: (i, 0)), Wglu, BS((DS, tn_s), lambda i, j, k: (0, j)), "nn", 0)],
        [(tm, tn_s)],
        [(y0, BS((tm, tn_s), lambda i, j, k: (i, j))), (ssm_glu_b, BS((1, tn_s), lambda i, j, k: (0, j)))],
        [(_sds((N, DS), F32), BS((tm, tn_s), lambda i, j, k: (i, j))),
         (_sds((N, DS), BF16), BS((tm, tn_s), lambda i, j, k: (i, j)))],
        glu_epi)

    z = _pool_fwd(proj, DS, DP, nseq, L)

    def pool_mm_epi(ids, accs, ex, o):
        b_ref, sc_ref = ex
        q = accs[0] + b_ref[...]
        o[0][...] = q
        o[1][...] = (q * sc_ref[...]).astype(BF16)

    qp, yp = _fused_matmul(
        "pool_mm", (N // tm, NPG, 1),
        [(z, BS((tm, PG), lambda i, g, k: (i, g)), Wpool, BS((None, PG, PG), lambda i, g, k: (g, 0, 0)), "nn", 0)],
        [(tm, PG)],
        [(pool_b_full, BS((1, PG), lambda i, g, k: (0, g))), (pool_scale, BS((1, PG), lambda i, g, k: (0, g)))],
        [(_sds((N, DP), F32), BS((tm, PG), lambda i, g, k: (i, g))),
         (_sds((N, DP), BF16), BS((tm, PG), lambda i, g, k: (i, g)))],
        pool_mm_epi)

    Wbs, Wbp, Wout = g_finish("branch", d_br, yp)
    Wout = Wout.reshape(D, D)
    gs_blk = BS((tm2, U), lambda i, q, k: (i, (DS + DP) // U + q))
    gp_blk = BS((tm2, U), lambda i, q, k: (i, (DS + DP + D) // U + q))
    out_blk = BS((tm2, U), lambda i, q, k: (i, q))

    def branch_epi(ids, accs, ex, o):
        gs_ref, gp_ref = ex
        o[0][...] = accs[0]
        o[1][...] = accs[1]
        o[2][...] = (_sigmoid(gs_ref[...]) * accs[0] + _sigmoid(gp_ref[...]) * accs[1]).astype(BF16)

    Ys, Yp, merged = _fused_matmul(
        "branch", (N // tm2, NDEV, 1),
        [(ys, BS((tm2, DS), lambda i, q, k: (i, 0)), Wbs, BS((None, DS, U), lambda i, q, k: (q, 0, 0)), "nn", 0),
         (yp, BS((tm2, DP), lambda i, q, k: (i, 0)), Wbp, BS((None, DP, U), lambda i, q, k: (q, 0, 0)), "nn", 1)],
        [(tm2, U), (tm2, U)],
        [(proj, gs_blk), (proj, gp_blk)],
        [(_sds((N, D), F32), out_blk), (_sds((N, D), F32), out_blk), (_sds((N, D), BF16), out_blk)],
        branch_epi)
    d_up = g_land("up", st_up, merged)

    tn_d = _tile(D, 512)
    (o_mix,) = _fused_matmul(
        "out_proj", (N // tm, D // tn_d, 1),
        [(merged, BS((tm, D), lambda i, j, k: (i, 0)), Wout, BS((D, tn_d), lambda i, j, k: (0, j)), "nn", 0)],
        [(tm, tn_d)], [], [(_sds((N, D), F32), BS((tm, tn_d), lambda i, j, k: (i, j)))],
        _store(lambda accs: accs), deps=[d_up[4]])
    h1, c = _mid_norm(x2, o_mix, norm_post_mix, norm_pre_ffn)

    (Wup,) = g_finish("up", d_up, c)
    d_down = g_land("down", st_down, Wup)
    tk_d = _tile(D, 1024)
    tk_up = _tile(D, 2048)
    (up_pre,) = _fused_matmul(
        "ffn_up", (N // tm2, NB, D // tk_up),
        [(c, BS((tm2, tk_up), lambda i, j, k: (i, k)), Wup, BS((None, tk_up, FC), lambda i, j, k: (j, k, 0)), "nn", 0)],
        [(tm2, FC)], [], [(_sds((NB, N, FC), F32), BS((None, tm2, FC), lambda i, j, k: (j, i, 0)))],
        _store(lambda accs: accs), deps=[d_down[4]])
    f = _gate_fwd(up_pre, conv_w_all, conv_b_blk, L)
    (Wdown,) = g_finish("down", d_down, f)
    Wdown = Wdown.reshape(HB, FC, D)
    tn_d2 = _tile(D, 1024)
    (dn,) = _fused_matmul(
        "ffn_down", (N // tm2, D // tn_d2, HB),
        [(f, BS((None, tm2, FC), lambda i, j, k: (k, i, 0)), Wdown, BS((None, FC, tn_d2), lambda i, j, k: (k, 0, j)), "nn", 0)],
        [(tm2, tn_d2)], [], [(_sds((N, D), F32), BS((tm2, tn_d2), lambda i, j, k: (i, j)))],
        _store(lambda accs: accs))
    dh2, d_dn, lossv, dg4 = _post_ffn(h1, dn, tgt, norm_post_ffn)

    (df,) = _fused_matmul(
        "ffn_down_dx", (N // tm2, HB, D // tk_d),
        [(d_dn, BS((tm2, tk_d), lambda i, j, k: (i, k)), Wdown, BS((None, FC, tk_d), lambda i, j, k: (j, 0, k)), "nt", 0)],
        [(tm2, FC)], [], [(_sds((HB, N, FC), BF16), BS((None, tm2, FC), lambda i, j, k: (j, i, 0)))],
        _store(lambda accs: accs))
    tk_n = _tile(N, 1024)
    (gW_down,) = _fused_matmul(
        "ffn_down_dw", (HB, D // tn_d, N // tk_n),
        [(f, BS((None, tk_n, FC), lambda j, n, k: (j, k, 0)), d_dn, BS((tk_n, tn_d), lambda j, n, k: (k, n)), "tn", 0)],
        [(FC, tn_d)], [], [(_sds((HB, FC, D), BF16), BS((None, FC, tn_d), lambda j, n, k: (j, 0, n)))],
        _store(lambda accs: accs))
    x_down = x_start("down", [gW_down.reshape(NDEV, FC // 2, D)])
    dup, dcw, dcb = _gate_bwd(up_pre, conv_w_all, conv_b_blk, df, L, deps=[x_down[4]])
    dpre = _conv_bwd(dup.reshape(NB, N, FC), conv_w_all, L)
    (dc,) = _fused_matmul(
        "ffn_up_dx", (N // tm2, D // tn_d2, NB),
        [(dpre, BS((None, tm2, FC), lambda i, j, k: (k, i, 0)), Wup, BS((None, tn_d2, FC), lambda i, j, k: (k, j, 0)), "nt", 0)],
        [(tm2, tn_d2)], [], [(_sds((N, D), F32), BS((tm2, tn_d2), lambda i, j, k: (i, j)))],
        _store(lambda accs: accs))
    tm_d = _tile(D, 512)
    (gW_up,) = _fused_matmul(
        "ffn_up_dw", (NB, D // tm_d, N // tk_n),
        [(c, BS((tk_n, tm_d), lambda j, n, k: (k, n)), dpre, BS((None, tk_n, FC), lambda j, n, k: (j, k, 0)), "tn", 0)],
        [(tm_d, FC)], [], [(_sds((NB, D, FC), BF16), BS((None, tm_d, FC), lambda j, n, k: (j, n, 0)))],
        _store(lambda accs: accs))
    x_up = x_start("up", [gW_up])

    dh1, d_o, dg2, dg3 = _mid_bwd(dh2, dc, h1, o_mix, norm_post_mix, norm_pre_ffn, deps=[x_up[4]])

    def dmerged_epi(ids, accs, ex, o):
        gs_ref, gp_ref, ys_ref, yp_ref = ex
        dm = accs[0]
        sg_s, sg_p = _sigmoid(gs_ref[...]), _sigmoid(gp_ref[...])
        o[0][...] = (dm * sg_s).astype(BF16)
        o[1][...] = (dm * sg_p).astype(BF16)
        o[2][...] = (dm * ys_ref[...] * sg_s * (1.0 - sg_s)).astype(BF16)
        o[3][...] = (dm * yp_ref[...] * sg_p * (1.0 - sg_p)).astype(BF16)

    dYs, dYp, dgs, dgp = _fused_matmul(
        "out_proj_dx", (N // tm2, NDEV, 1),
        [(d_o, BS((tm2, D), lambda i, q, k: (i, 0)), Wout, BS((U, D), lambda i, q, k: (q, 0)), "nt", 0)],
        [(tm2, U)],
        [(proj, gs_blk), (proj, gp_blk), (Ys, out_blk), (Yp, out_blk)],
        [(_sds((N, D), BF16), out_blk)] * 4,
        dmerged_epi)
    (gW_out,) = _fused_matmul(
        "out_proj_dw", (D // tm_d, D // tn_d, 1),
        [(merged, BS((N, tm_d), lambda i, j, k: (0, i)), d_o, BS((N, tn_d), lambda i, j, k: (0, j)), "tn", 0)],
        [(tm_d, tn_d)], [], [(_sds((D, D), BF16), BS((tm_d, tn_d), lambda i, j, k: (i, j)))],
        _store(lambda accs: accs))
    tm_s = _tile(DS, 512)
    gW_bs, gW_bp = _fused_matmul(
        "branch_dw", (DS // tm_s, NDEV, 1),
        [(ys, BS((N, tm_s), lambda i, q, k: (0, i)), dYs, BS((N, U), lambda i, q, k: (0, q)), "tn", 0),
         (yp, BS((N, tm_s), lambda i, q, k: (0, i)), dYp, BS((N, U), lambda i, q, k: (0, q)), "tn", 1)],
        [(tm_s, U), (tm_s, U)], [],
        [(_sds((NDEV, DS, U), BF16), BS((None, tm_s, U), lambda i, q, k: (q, i, 0)))] * 2,
        _store(lambda accs: accs))
    x_br = x_start("branch", [gW_bs, gW_bp, gW_out.reshape(NDEV, U, D)])

    tn_p = _tile(PG, 512)

    def dyp_epi(ids, accs, ex, o):
        q_ref, sc_ref = ex
        first = ids[1] == 0
        dyp = accs[0]
        dq = dyp * sc_ref[...]
        o[0][...] = dq.astype(BF16)
        _rowsum_into(o[1], first, dyp * q_ref[...])
        _rowsum_into(o[2], first, dq)

    dq, d_pscale, d_pb = _fused_matmul(
        "branch_pool_dx", (DP // tn_p, N // tm, NDEV),
        [(dYp, BS((tm, U), lambda j, i, k: (i, k)), Wbp, BS((None, tn_p, U), lambda j, i, k: (k, j, 0)), "nt", 0)],
        [(tm, tn_p)],
        [(qp, BS((tm, tn_p), lambda j, i, k: (i, j))), (pool_scale, BS((1, tn_p), lambda j, i, k: (0, j)))],
        [(_sds((N, DP), BF16), BS((tm, tn_p), lambda j, i, k: (i, j))),
         (_sds((1, DP), F32), BS((1, tn_p), lambda j, i, k: (0, j))),
         (_sds((1, DP), F32), BS((1, tn_p), lambda j, i, k: (0, j)))],
        dyp_epi, deps=[x_br[4]])
    (dz,) = _fused_matmul(
        "pool_mm_dx", (N // tm, NPG, 1),
        [(dq, BS((tm, PG), lambda i, g, k: (i, g)), Wpool, BS((None, PG, PG), lambda i, g, k: (g, 0, 0)), "nt", 0)],
        [(tm, PG)], [], [(_sds((N, DP), F32), BS((tm, PG), lambda i, g, k: (i, g)))],
        _store(lambda accs: accs))
    (gW_pool,) = _fused_matmul(
        "pool_mm_dw", (NPG, 1),
        [(z, BS((N, PG), lambda g, k: (0, g)), dq, BS((N, PG), lambda g, k: (0, g)), "tn", 0)],
        [(PG, PG)], [], [(_sds((NPG, PG, PG), BF16), BS((None, PG, PG), lambda g, k: (g, 0, 0)))],
        _store(lambda accs: accs))
    du_pool = _pool_bwd(dz, nseq, L)

    def dys_epi(ids, accs, ex, o):
        zg_ref, y0_ref = ex
        first = ids[1] == 0
        dys = accs[0]
        sg = _sigmoid(zg_ref[...])
        dzg = dys * _gelu(y0_ref[...]) * sg * (1.0 - sg)
        o[0][...] = dzg.astype(BF16)
        o[1][...] = dys * sg
        _rowsum_into(o[2], first, dzg)

    dzg, dy1_direct, d_glu_b = _fused_matmul(
        "branch_ssm_dx", (DS // tn_s, N // tm, NDEV),
        [(dYs, BS((tm, U), lambda j, i, k: (i, k)), Wbs, BS((None, tn_s, U), lambda j, i, k: (k, j, 0)), "nt", 0)],
        [(tm, tn_s)],
        [(zg, BS((tm, tn_s), lambda j, i, k: (i, j))), (y0, BS((tm, tn_s), lambda j, i, k: (i, j)))],
        [(_sds((N, DS), BF16), BS((tm, tn_s), lambda j, i, k: (i, j))),
         (_sds((N, DS), F32), BS((tm, tn_s), lambda j, i, k: (i, j))),
         (_sds((1, DS), F32), BS((1, tn_s), lambda j, i, k: (0, j)))],
        dys_epi)
    (gW_glu,) = _fused_matmul(
        "ssm_glu_dw", (DS // tm_s, DS // tn_s, 1),
        [(y1, BS((N, tm_s), lambda i, j, k: (0, i)), dzg, BS((N, tn_s), lambda i, j, k: (0, j)), "tn", 0)],
        [(tm_s, tn_s)], [], [(_sds((DS, DS), BF16), BS((tm_s, tn_s), lambda i, j, k: (i, j)))],
        _store(lambda accs: accs))
    x_mix = x_start("mix", [gW_glu.reshape(NDEV, DS // NDEV, DS),
                            gW_pool.reshape(NPG, NDEV, PG // NDEV, PG).transpose(1, 0, 2, 3)])

    tn_c = _tile(DS, CH)

    def dy0_epi(ids, accs, ex, o):
        d1_ref, y0_ref, u_ref = ex
        first = ids[1] == 0
        dy0 = (accs[0] + d1_ref[...]) * _gelu_grad(y0_ref[...])
        o[0][...] = dy0
        _rowsum_into(o[1], first, dy0 * u_ref[...])

    dy0, d_ssm_d = _fused_matmul(
        "ssm_glu_dx", (DS // tn_c, N // tm, 1),
        [(dzg, BS((tm, DS), lambda j, i, k: (i, 0)), Wglu, BS((tn_c, DS), lambda j, i, k: (j, 0)), "nt", 0)],
        [(tm, tn_c)],
        [(dy1_direct, BS((tm, tn_c), lambda j, i, k: (i, j))), (y0, BS((tm, tn_c), lambda j, i, k: (i, j))),
         (proj, BS((tm, tn_c), lambda j, i, k: (i, j)))],
        [(_sds((N, DS), F32), BS((tm, tn_c), lambda j, i, k: (i, j))),
         (_sds((1, DS), F32), BS((1, tn_c), lambda j, i, k: (0, j)))],
        dy0_epi, deps=[x_mix[4]])

    ds_re, ds_im = _fused_matmul(
        "ssm_out_dx", (N // tm2, NCH, 1),
        [(dy0, BS((tm2, CH), lambda i, c, k: (i, c)), WCre, BS((None, CS, CH), lambda i, c, k: (c, 0, 0)), "nt", 0),
         (dy0, BS((tm2, CH), lambda i, c, k: (i, c)), WCim, BS((None, CS, CH), lambda i, c, k: (c, 0, 0)), "nt", 1)],
        [(tm2, CS), (tm2, CS)], [],
        [(_sds((N, SL), F32), BS((tm2, CS), lambda i, c, k: (i, c)))] * 2,
        _store(lambda accs: accs))
    lam_r, lam_i, d_ab_re, d_ab_im = _scan_bwd(ds_re, ds_im, s_re, s_im, pw_re, pw_im, pwf_re, pwf_im, nseq, L)

    def du_epi(ids, accs, ex, o):
        dy0_ref, d_ref = ex
        o[0][...] = (accs[0] + dy0_ref[...] * d_ref[...]).astype(BF16)

    (du_ssm,) = _fused_matmul(
        "ssm_in_dx", (N // tm2, NCH, 1),
        [(lam_r, BS((tm2, CS), lambda i, c, k: (i, c)), WB, BS((None, CH, CS), lambda i, c, k: (c, 0, 0)), "nt", 0),
         (lam_i, BS((tm2, CS), lambda i, c, k: (i, c)), WB, BS((None, CH, CS), lambda i, c, k: (c, 0, 1)), "nt", 0)],
        [(tm2, CH)],
        [(dy0, BS((tm2, CH), lambda i, c, k: (i, c))), (ssm_d, BS((1, CH), lambda i, c, k: (0, c)))],
        [(_sds((N, DS), BF16), BS((tm2, CH), lambda i, c, k: (i, c)))],
        du_epi)
    dproj = jnp.concatenate([du_ssm, du_pool, dgs, dgp], axis=1)
    (gW_in,) = _fused_matmul(
        "in_proj_dw", (D // tm_d, nq, 1),
        [(a, BS((N, tm_d), lambda i, q, k: (0, i)), dproj, BS((N, U), lambda i, q, k: (0, q)), "tn", 0)],
        [(tm_d, U)], [], [(_sds((NDEV, D, 3 * U), BF16), BS((None, tm_d, U), lambda i, q, k: (q // 3, i, q % 3)))],
        _store(lambda accs: accs))
    x_in = x_start("in", [gW_in])
    (da,) = _fused_matmul(
        "in_proj_dx", (N // tm, D // tn_d2, NDEV),
        [(dproj, BS((tm, 3 * U), lambda i, j, k: (i, k)), Win, BS((None, tn_d2, 3 * U), lambda i, j, k: (k, j, 0)), "nt", 0)],
        [(tm, tn_d2)], [], [(_sds((N, D), F32), BS((tm, tn_d2), lambda i, j, k: (i, j)))],
        _store(lambda accs: accs), deps=[x_in[4]])
    grad_x, dg1 = _pre_bwd(x2, da, dh1, norm_pre_mix)

    dWCre, dWCim = _fused_matmul(
        "ssm_out_dw", (NCH, N // tk_n),
        [(s_re, BS((tk_n, CS), lambda c, k: (k, c)), dy0, BS((tk_n, CH), lambda c, k: (k, c)), "tn", 0),
         (s_im, BS((tk_n, CS), lambda c, k: (k, c)), dy0, BS((tk_n, CH), lambda c, k: (k, c)), "tn", 1)],
        [(CS, CH), (CS, CH)], [],
        [(_sds((NCH, CS, CH), F32), BS((None, CS, CH), lambda c, k: (c, 0, 0)))] * 2,
        _store(lambda accs: accs), deps=[x_in[4]])
    dWBre, dWBim = _fused_matmul(
        "ssm_in_dw", (NCH, N // tk_n),
        [(proj, BS((tk_n, CH), lambda c, k: (k, c)), lam_r, BS((tk_n, CS), lambda c, k: (k, c)), "tn", 0),
         (proj, BS((tk_n, CH), lambda c, k: (k, c)), lam_i, BS((tk_n, CS), lambda c, k: (k, c)), "tn", 1)],
        [(CH, CS), (CH, CS)], [],
        [(_sds((NCH, CH, CS), F32), BS((None, CH, CS), lambda c, k: (c, 0, 0)))] * 2,
        _store(lambda accs: accs), deps=[x_in[4]])
    d_bbr = _block_diag_in_grad(dWBre, J, G, P)
    d_bbi = _block_diag_in_grad(dWBim, J, G, P)
    d_lam_re, d_lam_im, d_log_step, d_br_t, d_bi_t = _ssm_param_bwd(
        lam_re, lam_im, log_step, br_t, bi_t,
        d_ab_re.reshape(nseq * SUBLANES, G, P), d_ab_im.reshape(nseq * SUBLANES, G, P), d_bbr, d_bbi)
    d_c_re = _block_diag_out_grad(dWCre, G, J, P)
    d_c_im = -_block_diag_out_grad(dWCim, G, J, P)

    d_conv_w = dcw.reshape(NB, 3, FC).transpose(1, 0, 2).reshape(3, F2)
    d_conv_b = dcb.reshape(1, F2)
    small = {
        "norm_pre_mix": dg1, "norm_post_mix": dg2, "norm_pre_ffn": dg3, "norm_post_ffn": dg4,
        "ssm_lambda_re": d_lam_re[None], "ssm_lambda_im": d_lam_im[None], "ssm_log_step": d_log_step.reshape(1, G),
        "ssm_b_re": d_br_t.transpose(1, 2, 0)[None], "ssm_b_im": d_bi_t.transpose(1, 2, 0)[None],
        "ssm_c_re": d_c_re[None], "ssm_c_im": d_c_im[None],
        "ssm_d": d_ssm_d, "ssm_glu_b": d_glu_b, "pool_scale": d_pscale,
        "pool_b": d_pb.reshape(1, NPG, PG), "ffn_conv_w": d_conv_w[None], "ffn_conv_b": d_conv_b,
    }
    assert list(small) == small_names
    packed = _pack([lossv] + [small[n] for n in small_names])
    st_small = _split_start("small_start", _broadcast_copies, NDEV - 1, [packed],
                            [_sds((NDEV,) + packed.shape, packed.dtype)], packed)

    res = {}
    after = st_small[4]
    for tag, started, group in (("down", x_down, ["w_down"]), ("up", x_up, ["w_up"]),
                                ("branch", x_br, ["w_branch_ssm", "w_branch_pool", "w_out"]),
                                ("mix", x_mix, ["ssm_glu_w", "pool_w"]), ("in", x_in, ["w_in"])):
        for n, parts in zip(group, x_finish(tag, started, after)):
            shape = args[n].shape
            cols = shape[-1]
            flat = lambda t: t.reshape(-1, cols)
            g, dl, nm, nv = _adamw("adamw_" + n, flat(args[n]), flat(args["m_" + n]), flat(args["v_" + n]),
                                   parts.reshape(NDEV, -1, cols))
            res[n] = tuple(t.reshape(shape) for t in (g, dl, nm, nv))
            after = g

    srcs, lands = _split_wait("small_wait", _broadcast_copies, st_small, after)
    small_all = lax.dynamic_update_index_in_dim(lands[0], srcs[0], dev, 0)
    loss_rows = (D + SUBLANES * LANES - 1) // (SUBLANES * LANES) * SUBLANES
    total, loss = _small_sum(small_all, loss_rows, D)
    totals = dict(zip(small_names, _unpack(total, [lossv.shape] + [small[n].shape for n in small_names])[1:]))
    totals["pool_b"] = lax.dynamic_slice_in_dim(totals["pool_b"], dev * (PG // NDEV), PG // NDEV, axis=2)
    totals["ffn_conv_w"] = lax.dynamic_slice_in_dim(totals["ffn_conv_w"], dev * FC, FC, axis=2)
    sm_g = _pack([totals[n] for n in small_names])
    _, sm_d, sm_nm, sm_nv = _adamw("adamw_small", sm_w, sm_m, sm_v, sm_g[None])
    shapes = [args[n].shape for n in small_names]
    for n, dl, nm, nv in zip(small_names, _unpack(sm_d, shapes), _unpack(sm_nm, shapes), _unpack(sm_nv, shapes)):
        res[n] = (totals[n], dl, nm, nv)

    outs = [loss.reshape(()), grad_x.reshape(x.shape)]
    for k in range(4):
        outs += [res[n][k] for n in names]
    return tuple(outs)
```

```python
import functools
import math

import jax
import jax.numpy as jnp
from jax import lax
from jax.experimental import pallas as pl
from jax.experimental.pallas import tpu as pltpu

F32 = jnp.float32
BF16 = jnp.bfloat16
BS = pl.BlockSpec

NDEV = 8
SSM_GROUP = 16
SSM_STATE = 64
GROUPS_PER_CHUNK = 16
SCAN_UNROLL = 8
POOL_WINDOWS = (2, 4, 8, 16)
EPS = 1e-6
MIN_NEG_REAL = -1e-4
ADAM_LR, ADAM_B1, ADAM_B2, ADAM_EPS, ADAM_WD, ADAM_STEP = 0.001, 0.9, 0.999, 1e-08, 0.01, 10
LANES = 128
SUBLANES = 8
VMEM_LIMIT = 56 * 1024 * 1024

_DIMS = {"nn": (((1,), (0,)), ((), ())), "nt": (((1,), (1,)), ((), ())), "tn": (((0,), (0,)), ((), ()))}


def _tile(dim, pref, mult=LANES):
    if dim <= pref:
        return dim
    t = (pref // mult) * mult
    while t >= mult:
        if dim % t == 0:
            return t
        t -= mult
    return dim


def _pc(name, body, grid, ins, in_specs, outs, out_specs, scratch=(), deps=()):
    multi = isinstance(outs, (list, tuple))
    if deps:
        n_in, n_dep, inner = len(ins), len(deps), body

        def body(*refs):
            return inner(*refs[:n_in], *refs[n_in + n_dep:])

        ins = list(ins) + list(deps)
        in_specs = list(in_specs) + [BS(memory_space=pl.ANY)] * n_dep
    return pl.pallas_call(
        body, name=name, grid=grid, in_specs=list(in_specs),
        out_specs=list(out_specs) if multi else out_specs,
        out_shape=list(outs) if multi else outs, scratch_shapes=list(scratch),
        compiler_params=pltpu.CompilerParams(dimension_semantics=("arbitrary",) * len(grid),
                                             vmem_limit_bytes=VMEM_LIMIT),
    )(*ins)


def _sds(shape, dtype):
    return jax.ShapeDtypeStruct(tuple(shape), dtype)


def _gelu(x):
    k = math.sqrt(2.0 / math.pi)
    return 0.5 * x * (1.0 + jnp.tanh(k * (x + 0.044715 * (x * x * x))))


def _gelu_grad(x):
    k = math.sqrt(2.0 / math.pi)
    t = jnp.tanh(k * (x + 0.044715 * (x * x * x)))
    return 0.5 * (1.0 + t) + 0.5 * x * (1.0 - t * t) * (k * (1.0 + 3.0 * 0.044715 * x * x))


def _sigmoid(x):
    return jax.nn.sigmoid(x)


def _fused_matmul(name, grid, pairs, acc_shapes, extras, outs, epilogue, deps=()):
    n_p, n_e, n_o = len(pairs), len(extras), len(outs)
    rank = len(grid)
    nk = grid[-1]

    def body(*refs):
        ab = refs[:2 * n_p]
        ex = refs[2 * n_p:2 * n_p + n_e]
        o = refs[2 * n_p + n_e:2 * n_p + n_e + n_o]
        accs = refs[2 * n_p + n_e + n_o:]
        ids = [pl.program_id(d) for d in range(rank)]
        k = ids[-1]

        @pl.when(k == 0)
        def _():
            for acc in accs:
                acc[...] = jnp.zeros_like(acc)

        for p in range(n_p):
            a = ab[2 * p][...].astype(BF16)
            mode = pairs[p][4]
            if mode == "nt_cat":
                b_ref = ab[2 * p + 1]
                b = jnp.concatenate([b_ref[q].astype(BF16) for q in range(b_ref.shape[0])], axis=1)
                mode = "nt"
            else:
                b = ab[2 * p + 1][...].astype(BF16)
            accs[pairs[p][5]][...] += lax.dot_general(a, b, _DIMS[mode], preferred_element_type=F32)

        @pl.when(k == nk - 1)
        def _():
            epilogue(ids, [acc[...] for acc in accs], ex, o)

    ins, in_specs = [], []
    for a, a_spec, b, b_spec, _, _ in pairs:
        ins += [a, b]
        in_specs += [a_spec, b_spec]
    for e, e_spec in extras:
        ins.append(e)
        in_specs.append(e_spec)
    res = _pc(name, body, grid, ins, in_specs, [s for s, _ in outs], [sp for _, sp in outs],
              scratch=[pltpu.VMEM(tuple(s), F32) for s in acc_shapes], deps=deps)
    return res


def _store(vals):
    def epilogue(ids, accs, ex, o):
        for r, v in zip(o, vals(accs)):
            r[...] = v.astype(r.dtype)
    return epilogue


def _rowsum_into(ref, first, v):
    s = jnp.sum(v, axis=0, keepdims=True)

    @pl.when(first)
    def _():
        ref[...] = s

    @pl.when(jnp.logical_not(first))
    def _():
        ref[...] += s


def _mesh_pos():
    return lax.axis_index("x"), lax.axis_index("y"), lax.axis_index("c")


def _slot(p):
    return 4 * p[0] + 2 * p[1] + p[2]


_HBM = BS(memory_space=pltpu.HBM)
_SEM = BS(memory_space=pltpu.SEMAPHORE)
_ANY = BS(memory_space=pl.ANY)
_EFFECT = pltpu.SideEffectType.DATAFLOW_SIDE_EFFECTING


def _other_chips(x, y):
    return [(1 - x, y), (x, 1 - y), (1 - x, 1 - y)]


def _all_peers(x, y, c):
    peers = []
    for k in range(1, NDEV):
        kx, ky, kc = (k >> 2) & 1, (k >> 1) & 1, k & 1
        peers.append((1 - x if kx else x, 1 - y if ky else y, 1 - c if kc else c))
    return peers


def _gather_copies(src, land, send_sems, recv_sems, base):
    x, y, c = _mesh_pos()
    return [pltpu.make_async_remote_copy(
        src_ref=src, dst_ref=land.at[_slot((x, y, c))],
        send_sem=send_sems.at[base + k], recv_sem=recv_sems.at[base + k],
        device_id=(*chip, c), device_id_type=pl.DeviceIdType.MESH) for k, chip in enumerate(_other_chips(x, y))]


def _d2d_copies(src, land, send_sems, recv_sems, base):
    x, y, c = _mesh_pos()
    blocks = [(x, y, c)] + [(*chip, c) for chip in _other_chips(x, y)]
    return [pltpu.make_async_remote_copy(
        src_ref=src if k == 0 else land.at[_slot(b)], dst_ref=land.at[_slot(b)],
        send_sem=send_sems.at[base + k], recv_sem=recv_sems.at[base + k],
        device_id=(x, y, 1 - c), device_id_type=pl.DeviceIdType.MESH) for k, b in enumerate(blocks)]


def _broadcast_copies(src, land, send_sems, recv_sems, base):
    x, y, c = _mesh_pos()
    return [pltpu.make_async_remote_copy(
        src_ref=src, dst_ref=land.at[_slot((x, y, c))],
        send_sem=send_sems.at[base + k], recv_sem=recv_sems.at[base + k],
        device_id=peer, device_id_type=pl.DeviceIdType.MESH) for k, peer in enumerate(_all_peers(x, y, c))]


def _exchange_copies(src, land, send_sems, recv_sems, base):
    x, y, c = _mesh_pos()
    return [pltpu.make_async_remote_copy(
        src_ref=src.at[_slot(peer)], dst_ref=land.at[_slot((x, y, c))],
        send_sem=send_sems.at[base + k], recv_sem=recv_sems.at[base + k],
        device_id=peer, device_id_type=pl.DeviceIdType.MESH) for k, peer in enumerate(_all_peers(x, y, c))]


def _split_start(name, copies, ncopy, srcs, land_shapes, after):
    n = len(srcs)

    def body(*refs):
        src_refs, land_refs = refs[:n], refs[n:2 * n]
        send_sems, recv_sems = refs[2 * n + 1], refs[2 * n + 2]
        token = refs[-1]
        for r in range(n):
            for cp in copies(src_refs[r], land_refs[r], send_sems, recv_sems, r * ncopy):
                cp.start()
        token[...] = jnp.zeros_like(token)

    lands = [s if isinstance(s, jax.Array) else pltpu.with_memory_space_constraint(lax.empty(s.shape, s.dtype), pltpu.HBM)
             for s in land_shapes]
    ins = list(srcs) + lands
    out_shape = ([pltpu.SemaphoreType.DMA((n * ncopy,)), pltpu.SemaphoreType.DMA((n * ncopy,))]
                 + [pltpu.HBM(a.shape, a.dtype) for a in lands]
                 + [_sds((SUBLANES, LANES), F32)])
    res = pl.pallas_call(
        body, name=name, out_shape=out_shape,
        in_specs=[_HBM] * (2 * n) + [_ANY], out_specs=[_SEM, _SEM] + [_HBM] * n + [BS(memory_space=pltpu.VMEM)],
        input_output_aliases={n + i: 2 + i for i in range(n)},
        compiler_params=pltpu.CompilerParams(has_side_effects=_EFFECT),
    )(*ins, after)
    return res[0], res[1], list(srcs), list(res[2:2 + n]), res[-1]


def _split_wait(name, copies, started, after):
    send_sems, recv_sems, srcs, lands, _ = started
    n = len(srcs)
    ncopy = send_sems.shape[0] // n
    after = list(after) if isinstance(after, (list, tuple)) else [after]

    def body(*refs):
        src_refs, land_refs = refs[:n], refs[n:2 * n]
        send_sems, recv_sems = refs[2 * n], refs[2 * n + 1]
        for r in range(n):
            for cp in copies(src_refs[r], land_refs[r], send_sems, recv_sems, r * ncopy):
                cp.wait_send()
                cp.wait_recv()

    res = pl.pallas_call(
        body, name=name, out_shape=[pltpu.HBM(a.shape, a.dtype) for a in lands],
        in_specs=[_HBM] * (2 * n) + [_SEM, _SEM] + [_ANY] * len(after), out_specs=[_HBM] * n,
        input_output_aliases={n + i: i for i in range(n)},
        compiler_params=pltpu.CompilerParams(has_side_effects=_EFFECT),
    )(*srcs, *lands, send_sems, recv_sems, *after)
    return list(srcs), list(res)


def _adamw(name, w, m, v, parts):
    R, C = w.shape
    S = parts.shape[0]
    tr = _tile(R, max(SUBLANES, (256 * 1024) // C), SUBLANES)

    def body(w_ref, m_ref, v_ref, p_ref, g_ref, d_ref, nm_ref, nv_ref):
        g = p_ref[0].astype(F32)
        for s in range(1, S):
            g = g + p_ref[s].astype(F32)
        m2 = ADAM_B1 * m_ref[...] + (1.0 - ADAM_B1) * g
        v2 = ADAM_B2 * v_ref[...] + (1.0 - ADAM_B2) * (g * g)
        m_hat = m2 / (1.0 - ADAM_B1 ** ADAM_STEP)
        v_hat = v2 / (1.0 - ADAM_B2 ** ADAM_STEP)
        g_ref[...] = g
        d_ref[...] = -ADAM_LR * (m_hat / (jnp.sqrt(v_hat) + ADAM_EPS) + ADAM_WD * w_ref[...])
        nm_ref[...] = m2
        nv_ref[...] = v2

    blk = BS((tr, C), lambda i: (i, 0))
    return _pc(name, body, (R // tr,), [w, m, v, parts],
               [blk, blk, blk, BS((S, tr, C), lambda i: (0, i, 0))],
               [_sds((R, C), F32)] * 4, [blk] * 4)


def _ssm_disc(lam_re, lam_im, log_step, br_t, bi_t):
    lr = jnp.minimum(lam_re, MIN_NEG_REAL)
    li = lam_im
    dt = jnp.exp(log_step)
    mag = jnp.exp(lr * dt)
    ang = li * dt
    ab_re = mag * jnp.cos(ang)
    ab_im = mag * jnp.sin(ang)
    nr = ab_re - 1.0
    ni = ab_im
    den = lr * lr + li * li
    f_re = (nr * lr + ni * li) / den
    f_im = (ni * lr - nr * li) / den
    bb_re = f_re[None] * br_t - f_im[None] * bi_t
    bb_im = f_re[None] * bi_t + f_im[None] * br_t
    return ab_re, ab_im, bb_re, bb_im


def _ssm_param_fwd(lam_re, lam_im, log_step, br_t, bi_t):
    G, P = lam_re.shape

    def body(lr_ref, li_ref, ls_ref, br_ref, bi_ref, pw_re_ref, pw_im_ref, pwf_re_ref, pwf_im_ref, bbr_ref, bbi_ref):
        ab_re, ab_im, bb_re, bb_im = _ssm_disc(lr_ref[...], li_ref[...], ls_ref[...], br_ref[...], bi_ref[...])
        bbr_ref[...] = bb_re
        bbi_ref[...] = bb_im
        pr, pi = ab_re, ab_im
        for r in range(SUBLANES):
            pw_re_ref[r] = pr
            pw_im_ref[r] = pi
            pwf_re_ref[SUBLANES - 1 - r] = pr
            pwf_im_ref[SUBLANES - 1 - r] = pi
            pr, pi = pr * ab_re - pi * ab_im, pr * ab_im + pi * ab_re

    full = lambda a: BS(a.shape, lambda i: (0,) * a.ndim)
    ins = [lam_re, lam_im, log_step, br_t, bi_t]
    outs = [_sds((SUBLANES, G, P), F32)] * 4 + [_sds(br_t.shape, F32)] * 2
    return _pc("ssm_param_fwd", body, (1,), ins, [full(a) for a in ins], outs, [full(o) for o in outs])


def _ssm_param_bwd(lam_re, lam_im, log_step, br_t, bi_t, d_ab_re, d_ab_im, d_bbr, d_bbi):
    def body(lr_ref, li_ref, ls_ref, br_ref, bi_ref, dar_ref, dai_ref, dbr_ref, dbi_ref,
             o_lr, o_li, o_ls, o_br, o_bi):
        prim = (lr_ref[...], li_ref[...], ls_ref[...], br_ref[...], bi_ref[...])
        _, vjp = jax.vjp(_ssm_disc, *prim)
        dar = dar_ref[0]
        dai = dai_ref[0]
        for k in range(1, dar_ref.shape[0]):
            dar = dar + dar_ref[k]
            dai = dai + dai_ref[k]
        g = vjp((dar, dai, dbr_ref[...], dbi_ref[...]))
        for r, v in zip((o_lr, o_li, o_ls, o_br, o_bi), g):
            r[...] = v

    full = lambda a: BS(a.shape, lambda i: (0,) * a.ndim)
    ins = [lam_re, lam_im, log_step, br_t, bi_t, d_ab_re, d_ab_im, d_bbr, d_bbi]
    outs = [_sds(a.shape, F32) for a in (lam_re, lam_im, log_step, br_t, bi_t)]
    return _pc("ssm_param_bwd", body, (1,), ins, [full(a) for a in ins], outs, [full(o) for o in outs])


def _bcast_row(ref, r, w):
    return jnp.broadcast_to(ref[pl.ds(r, 1), :], (SUBLANES, w))


def _pick_row(x, row, r):
    return jnp.broadcast_to(jnp.sum(jnp.where(row == r, x, 0.0), axis=0, keepdims=True), x.shape)


def _scan_fwd(bu_re, bu_im, pw_re, pw_im, nseq, L):
    N, SL = bu_re.shape
    W = _tile(SL, 256)
    unroll = math.gcd(L // SUBLANES, SCAN_UNROLL)

    def body(bre_ref, bim_ref, pre_ref, pim_ref, sre_ref, sim_ref):
        pre, pim = pre_ref[...], pim_ref[...]
        steps = [(k, _bcast_row(pre_ref, k - 1, W), _bcast_row(pim_ref, k - 1, W)) for k in (1, 2, 4)]
        row = lax.broadcasted_iota(jnp.int32, (SUBLANES, W), 0)

        def step(i, carry):
            cr, ci = carry
            r0 = pl.multiple_of(i * SUBLANES, SUBLANES)
            xr = bre_ref[pl.ds(r0, SUBLANES), :]
            xi = bim_ref[pl.ds(r0, SUBLANES), :]
            for k, ar, ai in steps:
                sr = pltpu.roll(xr, k, axis=0)
                si = pltpu.roll(xi, k, axis=0)
                keep = row >= k
                xr, xi = (xr + jnp.where(keep, ar * sr - ai * si, 0.0),
                          xi + jnp.where(keep, ar * si + ai * sr, 0.0))
            xr, xi = xr + (pre * cr - pim * ci), xi + (pre * ci + pim * cr)
            sre_ref[pl.ds(r0, SUBLANES), :] = xr
            sim_ref[pl.ds(r0, SUBLANES), :] = xi
            return _pick_row(xr, row, SUBLANES - 1), _pick_row(xi, row, SUBLANES - 1)

        def group(g, carry):
            for u in range(unroll):
                carry = step(g * unroll + u, carry)
            return carry

        zero = jnp.zeros((SUBLANES, W), F32)
        lax.fori_loop(0, L // SUBLANES // unroll, group, (zero, zero))

    blk = BS((L, W), lambda s, j: (s, j))
    pw = BS((SUBLANES, W), lambda s, j: (0, j))
    return _pc("ssm_scan_fwd", body, (nseq, SL // W), [bu_re, bu_im, pw_re, pw_im], [blk, blk, pw, pw],
               [_sds((N, SL), F32)] * 2, [blk, blk])


def _scan_bwd(ds_re, ds_im, s_re, s_im, pw_re, pw_im, pwf_re, pwf_im, nseq, L):
    N, SL = ds_re.shape
    W = _tile(SL, 256)
    nt = L // SUBLANES
    unroll = math.gcd(nt, SCAN_UNROLL)

    def body(dsr_ref, dsi_ref, sre_ref, sim_ref, pre_ref, pim_ref, fre_ref, fim_ref,
             lre_ref, lim_ref, dar_ref, dai_ref):
        fre, fim = fre_ref[...], -fim_ref[...]
        steps = [(k, _bcast_row(pre_ref, k - 1, W), -_bcast_row(pim_ref, k - 1, W)) for k in (1, 2, 4)]
        row = lax.broadcasted_iota(jnp.int32, (SUBLANES, W), 0)

        def step(ii, carry):
            cr, ci, acr, aci = carry
            i = nt - 1 - ii
            r0 = pl.multiple_of(i * SUBLANES, SUBLANES)
            xr = dsr_ref[pl.ds(r0, SUBLANES), :]
            xi = dsi_ref[pl.ds(r0, SUBLANES), :]
            for k, ar, ai in steps:
                sr = pltpu.roll(xr, SUBLANES - k, axis=0)
                si = pltpu.roll(xi, SUBLANES - k, axis=0)
                keep = row < SUBLANES - k
                xr, xi = (xr + jnp.where(keep, ar * sr - ai * si, 0.0),
                          xi + jnp.where(keep, ar * si + ai * sr, 0.0))
            xr, xi = xr + (fre * cr - fim * ci), xi + (fre * ci + fim * cr)
            lre_ref[pl.ds(r0, SUBLANES), :] = xr
            lim_ref[pl.ds(r0, SUBLANES), :] = xi
            p0 = pl.multiple_of(jnp.maximum(i - 1, 0) * SUBLANES, SUBLANES)
            has_prev = i > 0
            spr = jnp.where(row == 0,
                            jnp.where(has_prev, pltpu.roll(sre_ref[pl.ds(p0, SUBLANES), :], 1, axis=0), 0.0),
                            pltpu.roll(sre_ref[pl.ds(r0, SUBLANES), :], 1, axis=0))
            spi = jnp.where(row == 0,
                            jnp.where(has_prev, pltpu.roll(sim_ref[pl.ds(p0, SUBLANES), :], 1, axis=0), 0.0),
                            pltpu.roll(sim_ref[pl.ds(r0, SUBLANES), :], 1, axis=0))
            acr = acr + (xr * spr + xi * spi)
            aci = aci + (xi * spr - xr * spi)
            return _pick_row(xr, row, 0), _pick_row(xi, row, 0), acr, aci

        def group(g, carry):
            for u in range(unroll):
                carry = step(g * unroll + u, carry)
            return carry

        zero = jnp.zeros((SUBLANES, W), F32)
        _, _, acr, aci = lax.fori_loop(0, nt // unroll, group, (zero, zero, zero, zero))
        dar_ref[...] = acr
        dai_ref[...] = aci

    blk = BS((L, W), lambda s, j: (s, j))
    pw = BS((SUBLANES, W), lambda s, j: (0, j))
    da = BS((None, SUBLANES, W), lambda s, j: (s, 0, j))
    return _pc("ssm_scan_bwd", body, (nseq, SL // W),
               [ds_re, ds_im, s_re, s_im, pw_re, pw_im, pwf_re, pwf_im], [blk] * 4 + [pw] * 4,
               [_sds((N, SL), F32)] * 2 + [_sds((nseq, SUBLANES, SL), F32)] * 2, [blk, blk, da, da])


def _pool_select(g, vals):
    return jnp.where(g == 0, vals[0], jnp.where(g == 1, vals[1], jnp.where(g == 2, vals[2], vals[3])))


def _pool_fwd(proj, col0, DP, nseq, L):
    N = proj.shape[0]
    PG = DP // len(POOL_WINDOWS)
    W = _tile(PG, 256)

    def body(v_ref, z_ref):
        g = pl.program_id(1) // (PG // W)
        v = v_ref[...]
        row = lax.broadcasted_iota(jnp.int32, (L, W), 0)
        sums, s, k = [], v, 1
        for _ in POOL_WINDOWS:
            s = s + jnp.where(row >= k, pltpu.roll(s, k, axis=0), 0.0)
            sums.append(s)
            k *= 2
        win = _pool_select(g, [float(w) for w in POOL_WINDOWS])
        cnt = jnp.minimum((row + 1).astype(F32), win)
        z_ref[...] = (_pool_select(g, sums) / cnt - v).astype(z_ref.dtype)

    return _pc("pool_fwd", body, (nseq, DP // W), [proj], [BS((L, W), lambda s, j: (s, col0 // W + j))],
               _sds((N, DP), BF16), BS((L, W), lambda s, j: (s, j)))


def _pool_bwd(dz, nseq, L):
    N, DP = dz.shape
    PG = DP // len(POOL_WINDOWS)
    W = _tile(PG, 256)

    def body(dz_ref, dv_ref):
        g = pl.program_id(1) // (PG // W)
        d = dz_ref[...]
        row = lax.broadcasted_iota(jnp.int32, (L, W), 0)
        win = _pool_select(g, [float(w) for w in POOL_WINDOWS])
        s = d / jnp.minimum((row + 1).astype(F32), win)
        sums, k = [], 1
        for _ in POOL_WINDOWS:
            s = s + jnp.where(row < L - k, pltpu.roll(s, L - k, axis=0), 0.0)
            sums.append(s)
            k *= 2
        dv_ref[...] = (_pool_select(g, sums) - d).astype(dv_ref.dtype)

    blk = BS((L, W), lambda s, j: (s, j))
    return _pc("pool_bwd", body, (nseq, DP // W), [dz], [blk], _sds((N, DP), BF16), blk)


def _rstd(x):
    return lax.rsqrt(jnp.mean(x * x, axis=-1, keepdims=True) + EPS)


def _norm_bwd(dy, xhat, rstd, gain):
    t = dy * gain
    return rstd * (t - xhat * jnp.mean(t * xhat, axis=-1, keepdims=True))


def _pre_norm(x, g1):
    N, D = x.shape
    tr = _tile(N, 128, SUBLANES)

    def body(x_ref, g_ref, a_ref):
        xv = x_ref[...]
        a_ref[...] = (xv * _rstd(xv) * g_ref[...]).astype(a_ref.dtype)

    row = BS((tr, D), lambda i: (i, 0))
    vec = BS((1, D), lambda i: (0, 0))
    return _pc("pre_norm", body, (N // tr,), [x, g1], [row, vec], _sds((N, D), BF16), row)


def _mid_norm(x, o, g2, g3):
    N, D = x.shape
    tr = _tile(N, 128, SUBLANES)

    def body(x_ref, o_ref, g2_ref, g3_ref, h1_ref, c_ref):
        ov = o_ref[...]
        h1 = x_ref[...] + ov * _rstd(ov) * g2_ref[...]
        h1_ref[...] = h1
        c_ref[...] = (h1 * _rstd(h1) * g3_ref[...]).astype(c_ref.dtype)

    row = BS((tr, D), lambda i: (i, 0))
    vec = BS((1, D), lambda i: (0, 0))
    return _pc("mid_norm", body, (N // tr,), [x, o, g2, g3], [row, row, vec, vec],
               [_sds((N, D), F32), _sds((N, D), BF16)], [row, row])


def _post_ffn(h1, dn, tgt, g4):
    N, D = h1.shape
    tr = _tile(N, 128, SUBLANES)

    def body(h1_ref, dn_ref, t_ref, g_ref, dh2_ref, ddn_ref, lossv_ref, dg_ref):
        first = pl.program_id(0) == 0
        dnv = dn_ref[...]
        rstd = _rstd(dnv)
        xhat = dnv * rstd
        gain = g_ref[...]
        err = (h1_ref[...] + xhat * gain) - t_ref[...]
        dh2 = err / float(D)
        dh2_ref[...] = dh2
        ddn_ref[...] = _norm_bwd(dh2, xhat, rstd, gain).astype(ddn_ref.dtype)
        _rowsum_into(lossv_ref, first, err * err)
        _rowsum_into(dg_ref, first, dh2 * xhat)

    row = BS((tr, D), lambda i: (i, 0))
    vec = BS((1, D), lambda i: (0, 0))
    return _pc("post_ffn", body, (N // tr,), [h1, dn, tgt, g4], [row, row, row, vec],
               [_sds((N, D), F32), _sds((N, D), BF16), _sds((1, D), F32), _sds((1, D), F32)], [row, row, vec, vec])


def _mid_bwd(dh2, dc, h1, o, g2, g3, deps=()):
    N, D = h1.shape
    tr = _tile(N, 128, SUBLANES)

    def body(dh2_ref, dc_ref, h1_ref, o_ref, g2_ref, g3_ref, dh1_ref, do_ref, dg2_ref, dg3_ref):
        first = pl.program_id(0) == 0
        h1 = h1_ref[...]
        r3 = _rstd(h1)
        hc = h1 * r3
        dcv = dc_ref[...]
        dh1 = dh2_ref[...] + _norm_bwd(dcv, hc, r3, g3_ref[...])
        dh1_ref[...] = dh1
        ov = o_ref[...]
        r2 = _rstd(ov)
        ho = ov * r2
        do_ref[...] = _norm_bwd(dh1, ho, r2, g2_ref[...]).astype(do_ref.dtype)
        _rowsum_into(dg3_ref, first, dcv * hc)
        _rowsum_into(dg2_ref, first, dh1 * ho)

    row = BS((tr, D), lambda i: (i, 0))
    vec = BS((1, D), lambda i: (0, 0))
    return _pc("mid_bwd", body, (N // tr,), [dh2, dc, h1, o, g2, g3], [row] * 4 + [vec, vec],
               [_sds((N, D), F32), _sds((N, D), BF16), _sds((1, D), F32), _sds((1, D), F32)], [row, row, vec, vec],
               deps=deps)


def _pre_bwd(x, da, dh1, g1):
    N, D = x.shape
    tr = _tile(N, 128, SUBLANES)

    def body(x_ref, da_ref, dh1_ref, g_ref, dx_ref, dg_ref):
        first = pl.program_id(0) == 0
        xv = x_ref[...]
        r1 = _rstd(xv)
        xh = xv * r1
        dav = da_ref[...]
        dx_ref[...] = dh1_ref[...] + _norm_bwd(dav, xh, r1, g_ref[...])
        _rowsum_into(dg_ref, first, dav * xh)

    row = BS((tr, D), lambda i: (i, 0))
    vec = BS((1, D), lambda i: (0, 0))
    return _pc("pre_bwd", body, (N // tr,), [x, da, dh1, g1], [row, row, row, vec],
               [_sds((N, D), F32), _sds((1, D), F32)], [row, vec])


def _conv_rows(x_ref, halo_ref, first):
    x = x_ref[...]
    tr = x.shape[0]
    xx = jnp.concatenate([jnp.where(first, 0.0, halo_ref[...]), x], axis=0)
    x1 = pltpu.roll(xx, 1, axis=0)[SUBLANES:]
    x2 = pltpu.roll(xx, 2, axis=0)[SUBLANES:]
    del tr
    return x, x1, x2


def _conv_apply(rows, w_ref, b_ref):
    x, x1, x2 = rows
    return ((b_ref[...] + x2 * w_ref[pl.ds(0, 1), :]) + x1 * w_ref[pl.ds(1, 1), :]) + x * w_ref[pl.ds(2, 1), :]


def _gate_specs(N, FC, TR, half):
    tile = BS((None, TR, FC), lambda jj, i: (jj + half, i, 0))
    halo = BS((None, SUBLANES, FC), lambda jj, i: (jj + half, jnp.maximum(i * (TR // SUBLANES) - 1, 0), 0))
    cw = BS((None, 3, FC), lambda jj, i: (jj + half, 0, 0))
    cb = BS((None, 1, FC), lambda jj, i: (jj + half, 0, 0))
    return tile, halo, cw, cb


def _gate_fwd(up_pre, cw, cb, L):
    nb, N, FC = up_pre.shape
    half = nb // 2
    TR = _tile(L, 128, SUBLANES)

    def body(xa_ref, ha_ref, wa_ref, ba_ref, xb_ref, hb_ref, wb_ref, bb_ref, f_ref):
        first = (pl.program_id(1) % (L // TR)) == 0
        ua = _conv_apply(_conv_rows(xa_ref, ha_ref, first), wa_ref, ba_ref)
        ub = _conv_apply(_conv_rows(xb_ref, hb_ref, first), wb_ref, bb_ref)
        f_ref[...] = (_gelu(ua) * ub).astype(f_ref.dtype)

    sa, sb = _gate_specs(N, FC, TR, 0), _gate_specs(N, FC, TR, half)
    return _pc("gate_fwd", body, (half, N // TR), [up_pre, up_pre, cw, cb] * 2, list(sa) + list(sb),
               _sds((half, N, FC), BF16), BS((None, TR, FC), lambda jj, i: (jj, i, 0)))


def _gate_bwd(up_pre, cw, cb, df, L, deps=()):
    nb, N, FC = up_pre.shape
    half = nb // 2
    TR = _tile(L, 128, SUBLANES)

    def body(xa_ref, ha_ref, wa_ref, ba_ref, xb_ref, hb_ref, wb_ref, bb_ref, df_ref, dup_ref, dw_ref, dbias_ref):
        i = pl.program_id(1)
        first_row = i == 0
        first = (i % (L // TR)) == 0
        ra = _conv_rows(xa_ref, ha_ref, first)
        rb = _conv_rows(xb_ref, hb_ref, first)
        ua = _conv_apply(ra, wa_ref, ba_ref)
        ub = _conv_apply(rb, wb_ref, bb_ref)
        dfv = df_ref[...].astype(F32)
        dua = dfv * ub * _gelu_grad(ua)
        dub = dfv * _gelu(ua)
        dup_ref[0] = dua.astype(dup_ref.dtype)
        dup_ref[1] = dub.astype(dup_ref.dtype)
        for h, (rows, du) in enumerate(((ra, dua), (rb, dub))):
            x, x1, x2 = rows
            _rowsum_into(dbias_ref.at[h], first_row, du)
            for k, xs in enumerate((x2, x1, x)):
                _rowsum_into(dw_ref.at[h, pl.ds(k, 1), :], first_row, du * xs)

    sa, sb = _gate_specs(N, FC, TR, 0), _gate_specs(N, FC, TR, half)
    tile = BS((None, TR, FC), lambda jj, i: (jj, i, 0))
    both = BS((2, None, TR, FC), lambda jj, i: (0, jj, i, 0))
    dw = BS((2, None, 3, FC), lambda jj, i: (0, jj, 0, 0))
    dbias = BS((2, None, 1, FC), lambda jj, i: (0, jj, 0, 0))
    return _pc("gate_bwd", body, (half, N // TR), [up_pre, up_pre, cw, cb] * 2 + [df], list(sa) + list(sb) + [tile],
               [_sds((2, half, N, FC), BF16), _sds((2, half, 3, FC), F32), _sds((2, half, 1, FC), F32)],
               [both, dw, dbias], deps=deps)


def _conv_bwd(dup, cw, L):
    nb, N, FC = dup.shape
    TR = _tile(L, 128, 2 * SUBLANES)
    HR = 2 * SUBLANES
    nrb = N // HR

    def body(x_ref, h_ref, w_ref, o_ref):
        last = ((pl.program_id(1) + 1) % (L // TR)) == 0
        x = x_ref[...].astype(F32)
        xx = jnp.concatenate([x, jnp.where(last, 0.0, h_ref[...].astype(F32))], axis=0)
        x1 = pltpu.roll(xx, TR + HR - 1, axis=0)[:TR]
        x2 = pltpu.roll(xx, TR + HR - 2, axis=0)[:TR]
        o_ref[...] = (x * w_ref[pl.ds(2, 1), :] + x1 * w_ref[pl.ds(1, 1), :] + x2 * w_ref[pl.ds(0, 1), :]
                      ).astype(o_ref.dtype)

    tile = BS((None, TR, FC), lambda jj, i: (jj, i, 0))
    halo = BS((None, HR, FC), lambda jj, i: (jj, jnp.minimum((i + 1) * (TR // HR), nrb - 1), 0))
    w = BS((None, 3, FC), lambda jj, i: (jj, 0, 0))
    return _pc("conv_bwd", body, (nb, N // TR), [dup, dup, cw], [tile, halo, w], _sds((nb, N, FC), BF16), tile)


def _pack(arrs):
    parts = []
    for a in arrs:
        flat = a.reshape(-1).astype(F32)
        pad = (-flat.shape[0]) % (SUBLANES * LANES)
        parts.append(jnp.pad(flat, (0, pad)))
    return jnp.concatenate(parts).reshape(-1, LANES)


def _unpack(packed, shapes):
    flat = packed.reshape(-1)
    out, off = [], 0
    for s in shapes:
        n = math.prod(s)
        out.append(flat[off:off + n].reshape(s))
        off += n + ((-n) % (SUBLANES * LANES))
    return out


def _small_sum(gathered, loss_rows, d_model):
    S, R, C = gathered.shape

    def body(p_ref, tot_ref, loss_ref):
        t = p_ref[0]
        for s in range(1, S):
            t = t + p_ref[s]
        tot_ref[...] = t
        loss_ref[...] = jnp.full((1, 1), 0.5 / d_model, F32) * jnp.sum(t[:loss_rows])

    return _pc("small_sum", body, (1,), [gathered], [BS((S, R, C), lambda i: (0, 0, 0))],
               [_sds((R, C), F32), _sds((1, 1), F32)], [BS((R, C), lambda i: (0, 0)), BS((1, 1), lambda i: (0, 0))])


def _block_diag_in(bb_t, nch):
    J, G, P = bb_t.shape
    gl = G // nch
    b = bb_t.reshape(J, nch, gl, P).transpose(1, 0, 2, 3)
    eye = jnp.eye(gl, dtype=F32)
    w = eye[None, :, None, :, None] * b[:, None, :, :, :]
    return w.reshape(nch, gl * J, gl * P)


def _block_diag_in_grad(dw, J, G, P):
    nch = dw.shape[0]
    gl = G // nch
    d = dw.reshape(nch, gl, J, gl, P)
    d = jnp.einsum("cgjgp->jcgp", d)
    return d.reshape(J, G, P)


def _block_diag_out(c, nch):
    G, J, P = c.shape
    gl = G // nch
    cc = c.reshape(nch, gl, J, P).transpose(0, 1, 3, 2)
    eye = jnp.eye(gl, dtype=F32)
    w = cc[:, :, :, None, :] * eye[None, :, None, :, None]
    return w.reshape(nch, gl * P, gl * J)


def _block_diag_out_grad(dw, G, J, P):
    nch = dw.shape[0]
    gl = G // nch
    d = dw.reshape(nch, gl, P, gl, J)
    d = jnp.einsum("cgpgj->cgjp", d)
    return d.reshape(G, J, P)


def kernel(x, norm_pre_mix, w_in, ssm_lambda_re, ssm_lambda_im, ssm_log_step, ssm_b_re, ssm_b_im, ssm_c_re, ssm_c_im, ssm_d, ssm_glu_w, ssm_glu_b, pool_w, pool_b, pool_scale, w_branch_ssm, w_branch_pool, w_out, norm_post_mix, norm_pre_ffn, w_up, ffn_conv_w, ffn_conv_b, w_down, norm_post_ffn, loss_target, m_norm_pre_mix, m_w_in, m_ssm_lambda_re, m_ssm_lambda_im, m_ssm_log_step, m_ssm_b_re, m_ssm_b_im, m_ssm_c_re, m_ssm_c_im, m_ssm_d, m_ssm_glu_w, m_ssm_glu_b, m_pool_w, m_pool_b, m_pool_scale, m_w_branch_ssm, m_w_branch_pool, m_w_out, m_norm_post_mix, m_norm_pre_ffn, m_w_up, m_ffn_conv_w, m_ffn_conv_b, m_w_down, m_norm_post_ffn, v_norm_pre_mix, v_w_in, v_ssm_lambda_re, v_ssm_lambda_im, v_ssm_log_step, v_ssm_b_re, v_ssm_b_im, v_ssm_c_re, v_ssm_c_im, v_ssm_d, v_ssm_glu_w, v_ssm_glu_b, v_pool_w, v_pool_b, v_pool_scale, v_w_branch_ssm, v_w_branch_pool, v_w_out, v_norm_post_mix, v_norm_pre_ffn, v_w_up, v_ffn_conv_w, v_ffn_conv_b, v_w_down, v_norm_post_ffn):
    args = dict(locals())
    names = ["norm_pre_mix", "w_in", "ssm_lambda_re", "ssm_lambda_im", "ssm_log_step", "ssm_b_re", "ssm_b_im",
             "ssm_c_re", "ssm_c_im", "ssm_d", "ssm_glu_w", "ssm_glu_b", "pool_w", "pool_b", "pool_scale",
             "w_branch_ssm", "w_branch_pool", "w_out", "norm_post_mix", "norm_pre_ffn", "w_up", "ffn_conv_w",
             "ffn_conv_b", "w_down", "norm_post_ffn"]

    nseq, L, D = x.shape
    N = nseq * L
    U = D // NDEV
    DS = ssm_d.shape[1]
    DP = pool_scale.shape[1]
    G, P, J = ssm_b_re.shape[1:]
    SL = G * P
    CH = GROUPS_PER_CHUNK * J
    CS = GROUPS_PER_CHUNK * P
    NCH = DS // CH
    NPG = len(POOL_WINDOWS)
    PG = DP // NPG
    FC = w_up.shape[2]
    NB = NDEV
    HB = NB // 2
    F2 = NB * FC
    dev = _slot(_mesh_pos())
    tm = _tile(N, 1024)
    tm2 = _tile(N, 512)

    x2 = x.reshape(N, D)
    tgt = loss_target.reshape(N, D)

    def bf(t):
        return t.astype(BF16)

    def g_start(tag, group, after):
        return _split_start("gather_start_" + tag, _gather_copies, 3, group,
                            [_sds((NDEV,) + s.shape, s.dtype) for s in group], after)

    def g_land(tag, started, after):
        srcs, lands = _split_wait("gather_wait_" + tag, _gather_copies, started, after)
        return _split_start("d2d_start_" + tag, _d2d_copies, 4, srcs, lands, srcs[0])

    def g_finish(tag, d2d, after):
        srcs, lands = _split_wait("d2d_wait_" + tag, _d2d_copies, d2d, after)
        return [lax.dynamic_update_index_in_dim(l, s, dev, 0) for l, s in zip(lands, srcs)]

    def x_start(tag, group):
        return _split_start("exchange_start_" + tag, _exchange_copies, NDEV - 1, group,
                            [_sds(g.shape, g.dtype) for g in group], group[0])

    def x_finish(tag, started, after):
        srcs, lands = _split_wait("exchange_wait_" + tag, _exchange_copies, started, after)
        own = [lax.dynamic_index_in_dim(s, dev, 0, keepdims=False) for s in srcs]
        return [lax.dynamic_update_index_in_dim(l, o, dev, 0) for l, o in zip(lands, own)]

    st_in = g_start("in", [bf(w_in[0])], x2)
    st_mix = g_start("mix", [bf(ssm_glu_w[0]), bf(pool_w[0]), pool_b[0], ffn_conv_w[0]], st_in[4])
    conv_b_blk = ffn_conv_b.reshape(NB, 1, FC)

    (_, x2e, w_branch_ssm, w_branch_pool, w_out, w_up, w_down, ssm_lambda_re, ssm_lambda_im, ssm_log_step,
     ssm_b_re, ssm_b_im, ssm_c_re, ssm_c_im) = lax.optimization_barrier(
        (st_mix[4], x2, w_branch_ssm, w_branch_pool, w_out, w_up, w_down, ssm_lambda_re, ssm_lambda_im, ssm_log_step,
         ssm_b_re, ssm_b_im, ssm_c_re, ssm_c_im))
    lam_re, lam_im = ssm_lambda_re[0], ssm_lambda_im[0]
    log_step = ssm_log_step.reshape(G, 1)
    br_t = ssm_b_re[0].transpose(2, 0, 1)
    bi_t = ssm_b_im[0].transpose(2, 0, 1)
    pw_re3, pw_im3, pwf_re3, pwf_im3, bb_re, bb_im = _ssm_param_fwd(lam_re, lam_im, log_step, br_t, bi_t)
    pw_re, pw_im = pw_re3.reshape(SUBLANES, SL), pw_im3.reshape(SUBLANES, SL)
    pwf_re, pwf_im = pwf_re3.reshape(SUBLANES, SL), pwf_im3.reshape(SUBLANES, SL)
    WB = jnp.concatenate([_block_diag_in(bb_re, NCH), _block_diag_in(bb_im, NCH)], axis=2).astype(BF16)
    WCre = _block_diag_out(ssm_c_re[0], NCH).astype(BF16)
    WCim = _block_diag_out(-ssm_c_im[0], NCH).astype(BF16)
    a = _pre_norm(x2e, norm_pre_mix)
    small_names = ["norm_pre_mix", "norm_post_mix", "norm_pre_ffn", "norm_post_ffn", "ssm_lambda_re", "ssm_lambda_im",
                   "ssm_log_step", "ssm_b_re", "ssm_b_im", "ssm_c_re", "ssm_c_im", "ssm_d", "ssm_glu_b", "pool_scale",
                   "pool_b", "ffn_conv_w", "ffn_conv_b"]
    _, small_in = lax.optimization_barrier(
        (st_mix[4], [[args[p + n] for n in small_names] for p in ("", "m_", "v_")]))
    sm_w, sm_m, sm_v = (_pack(group) for group in small_in)
    g_br = [bf(w_branch_ssm[0]), bf(w_branch_pool[0]), bf(w_out[0])]
    g_up, g_down = [bf(w_up[0].T)], [bf(w_down[0])]
    early = [WB, WCre, WCim, pwf_re, pwf_im, a, sm_w, sm_m, sm_v] + g_br + g_up + g_down

    d_in = g_land("in", st_in, [st_mix[4]] + early)
    st_br = g_start("branch", g_br, d_in[4])
    st_up = g_start("up", g_up, st_br[4])
    st_down = g_start("down", g_down, st_up[4])
    d_mix = g_land("mix", st_mix, st_down[4])
    (Win,) = g_finish("in", d_in, d_mix[4])

    nq = 3 * NDEV
    (proj,) = _fused_matmul(
        "in_proj", (N // tm, nq, 1),
        [(a, BS((tm, D), lambda i, q, k: (i, 0)), Win, BS((None, D, U), lambda i, q, k: (q // 3, 0, q % 3)), "nn", 0)],
        [(tm, U)], [], [(_sds((N, 3 * D), F32), BS((tm, U), lambda i, q, k: (i, q)))],
        _store(lambda accs: accs))
    d_br = g_land("branch", st_br, proj)

    bu_re, bu_im = _fused_matmul(
        "ssm_in", (N // tm2, NCH, 1),
        [(proj, BS((tm2, CH), lambda i, c, k: (i, c)), WB, BS((None, CH, 2 * CS), lambda i, c, k: (c, 0, 0)), "nn", 0)],
        [(tm2, 2 * CS)], [],
        [(_sds((N, SL), F32), BS((tm2, CS), lambda i, c, k: (i, c)))] * 2,
        _store(lambda accs: (accs[0][:, :CS], accs[0][:, CS:])), deps=[d_br[4]])
    s_re, s_im = _scan_fwd(bu_re, bu_im, pw_re, pw_im, nseq, L)

    def ssm_out_epi(ids, accs, ex, o):
        u_ref, d_ref = ex
        y0 = accs[0] + d_ref[...] * u_ref[...]
        o[0][...] = y0
        o[1][...] = _gelu(y0).astype(BF16)

    y0, y1 = _fused_matmul(
        "ssm_out", (N // tm2, NCH, 1),
        [(s_re, BS((tm2, CS), lambda i, c, k: (i, c)), WCre, BS((None, CS, CH), lambda i, c, k: (c, 0, 0)), "nn", 0),
         (s_im, BS((tm2, CS), lambda i, c, k: (i, c)), WCim, BS((None, CS, CH), lambda i, c, k: (c, 0, 0)), "nn", 0)],
        [(tm2, CH)],
        [(proj, BS((tm2, CH), lambda i, c, k: (i, c))), (ssm_d, BS((1, CH), lambda i, c, k: (0, c)))],
        [(_sds((N, DS), F32), BS((tm2, CH), lambda i, c, k: (i, c))),
         (_sds((N, DS), BF16), BS((tm2, CH), lambda i, c, k: (i, c)))],
        ssm_out_epi)

    Wglu, Wpool, pool_b_all, conv_w_all = g_finish("mix", d_mix, y1)
    Wglu = Wglu.reshape(DS, DS)
    Wpool = Wpool.transpose(1, 0, 2, 3).reshape(NPG, PG, PG)
    pool_b_full = pool_b_all.transpose(1, 0, 2).reshape(1, DP)
    tn_s = _tile(DS, 512)

    def glu_epi(ids, accs, ex, o):
        y0_ref, b_ref = ex
        zg = accs[0] + b_ref[...]
        o[0][...] = zg
        o[1][...] = (_gelu(y0_ref[...]) * _sigmoid(zg)).astype(BF16)

    zg, ys = _fused_matmul(
        "ssm_glu", (N // tm, DS // tn_s, 1),
        [(y1, BS((tm, DS), lambda i, j, k: (i, 0)), Wglu, BS((DS, tn_s), lambda i, j, k: (0, j)), "nn", 0)],
        [(tm, tn_s)],
        [(y0, BS((tm, tn_s), lambda i, j, k: (i, j))), (ssm_glu_b, BS((1, tn_s), lambda i, j, k: (0, j)))],
        [(_sds((N, DS), F32), BS((tm, tn_s), lambda i, j, k: (i, j))),
         (_sds((N, DS), BF16), BS((tm, tn_s), lambda i, j, k: (i, j)))],
        glu_epi)

    z = _pool_fwd(proj, DS, DP, nseq, L)

    def pool_mm_epi(ids, accs, ex, o):
        b_ref, sc_ref = ex
        q = accs[0] + b_ref[...]
        o[0][...] = q
        o[1][...] = (q * sc_ref[...]).astype(BF16)

    qp, yp = _fused_matmul(
        "pool_mm", (N // tm, NPG, 1),
        [(z, BS((tm, PG), lambda i, g, k: (i, g)), Wpool, BS((None, PG, PG), lambda i, g, k: (g, 0, 0)), "nn", 0)],
        [(tm, PG)],
        [(pool_b_full, BS((1, PG), lambda i, g, k: (0, g))), (pool_scale, BS((1, PG), lambda i, g, k: (0, g)))],
        [(_sds((N, DP), F32), BS((tm, PG), lambda i, g, k: (i, g))),
         (_sds((N, DP), BF16), BS((tm, PG), lambda i, g, k: (i, g)))],
        pool_mm_epi)

    Wbs, Wbp, Wout = g_finish("branch", d_br, yp)
    Wout = Wout.reshape(D, D)
    gs_blk = BS((tm2, U), lambda i, q, k: (i, (DS + DP) // U + q))
    gp_blk = BS((tm2, U), lambda i, q, k: (i, (DS + DP + D) // U + q))
    out_blk = BS((tm2, U), lambda i, q, k: (i, q))

    def branch_epi(ids, accs, ex, o):
        gs_ref, gp_ref = ex
        o[0][...] = accs[0]
        o[1][...] = accs[1]
        o[2][...] = (_sigmoid(gs_ref[...]) * accs[0] + _sigmoid(gp_ref[...]) * accs[1]).astype(BF16)

    Ys, Yp, merged = _fused_matmul(
        "branch", (N // tm2, NDEV, 1),
        [(ys, BS((tm2, DS), lambda i, q, k: (i, 0)), Wbs, BS((None, DS, U), lambda i, q, k: (q, 0, 0)), "nn", 0),
         (yp, BS((tm2, DP), lambda i, q, k: (i, 0)), Wbp, BS((None, DP, U), lambda i, q, k: (q, 0, 0)), "nn", 1)],
        [(tm2, U), (tm2, U)],
        [(proj, gs_blk), (proj, gp_blk)],
        [(_sds((N, D), F32), out_blk), (_sds((N, D), F32), out_blk), (_sds((N, D), BF16), out_blk)],
        branch_epi)
    d_up = g_land("up", st_up, merged)

    tn_d = _tile(D, 512)
    (o_mix,) = _fused_matmul(
        "out_proj", (N // tm, D // tn_d, 1),
        [(merged, BS((tm, D), lambda i, j, k: (i, 0)), Wout, BS((D, tn_d), lambda i, j, k: (0, j)), "nn", 0)],
        [(tm, tn_d)], [], [(_sds((N, D), F32), BS((tm, tn_d), lambda i, j, k: (i, j)))],
        _store(lambda accs: accs), deps=[d_up[4]])
    h1, c = _mid_norm(x2, o_mix, norm_post_mix, norm_pre_ffn)

    (Wup,) = g_finish("up", d_up, c)
    d_down = g_land("down", st_down, Wup)
    tk_d = _tile(D, 1024)
    tk_up = _tile(D, 2048)
    (up_pre,) = _fused_matmul(
        "ffn_up", (N // tm2, NB, D // tk_up),
        [(c, BS((tm2, tk_up), lambda i, j, k: (i, k)), Wup, BS((None, FC, tk_up), lambda i, j, k: (j, 0, k)), "nt", 0)],
        [(tm2, FC)], [], [(_sds((NB, N, FC), F32), BS((None, tm2, FC), lambda i, j, k: (j, i, 0)))],
        _store(lambda accs: accs), deps=[d_down[4]])
    f = _gate_fwd(up_pre, conv_w_all, conv_b_blk, L)
    (Wdown,) = g_finish("down", d_down, f)
    Wdown = Wdown.reshape(HB, FC, D)
    tn_d2 = _tile(D, 1024)
    (dn,) = _fused_matmul(
        "ffn_down", (N // tm2, D // tn_d2, HB),
        [(f, BS((None, tm2, FC), lambda i, j, k: (k, i, 0)), Wdown, BS((None, FC, tn_d2), lambda i, j, k: (k, 0, j)), "nn", 0)],
        [(tm2, tn_d2)], [], [(_sds((N, D), F32), BS((tm2, tn_d2), lambda i, j, k: (i, j)))],
        _store(lambda accs: accs))
    dh2, d_dn, lossv, dg4 = _post_ffn(h1, dn, tgt, norm_post_ffn)

    (df,) = _fused_matmul(
        "ffn_down_dx", (N // tm2, HB, D // tk_d),
        [(d_dn, BS((tm2, tk_d), lambda i, j, k: (i, k)), Wdown, BS((None, FC, tk_d), lambda i, j, k: (j, 0, k)), "nt", 0)],
        [(tm2, FC)], [], [(_sds((HB, N, FC), BF16), BS((None, tm2, FC), lambda i, j, k: (j, i, 0)))],
        _store(lambda accs: accs))
    tk_n = _tile(N, 1024)
    (gW_down,) = _fused_matmul(
        "ffn_down_dw", (HB, D // tn_d, N // tk_n),
        [(f, BS((None, tk_n, FC), lambda j, n, k: (j, k, 0)), d_dn, BS((tk_n, tn_d), lambda j, n, k: (k, n)), "tn", 0)],
        [(FC, tn_d)], [], [(_sds((HB, FC, D), BF16), BS((None, FC, tn_d), lambda j, n, k: (j, 0, n)))],
        _store(lambda accs: accs))
    x_down = x_start("down", [gW_down.reshape(NDEV, FC // 2, D)])
    dup, dcw, dcb = _gate_bwd(up_pre, conv_w_all, conv_b_blk, df, L, deps=[x_down[4]])
    dpre = _conv_bwd(dup.reshape(NB, N, FC), conv_w_all, L)
    (dc,) = _fused_matmul(
        "ffn_up_dx", (N // tm2, D // tn_d2, NB),
        [(dpre, BS((None, tm2, FC), lambda i, j, k: (k, i, 0)), Wup, BS((None, FC, tn_d2), lambda i, j, k: (k, 0, j)), "nn", 0)],
        [(tm2, tn_d2)], [], [(_sds((N, D), F32), BS((tm2, tn_d2), lambda i, j, k: (i, j)))],
        _store(lambda accs: accs))
    tm_d = _tile(D, 512)
    (gW_up,) = _fused_matmul(
        "ffn_up_dw", (NB, D // tm_d, N // tk_n),
        [(dpre, BS((None, tk_n, FC), lambda j, n, k: (j, k, 0)), c, BS((tk_n, tm_d), lambda j, n, k: (k, n)), "tn", 0)],
        [(FC, tm_d)], [], [(_sds((NB, FC, D), BF16), BS((None, FC, tm_d), lambda j, n, k: (j, 0, n)))],
        _store(lambda accs: accs))
    x_up = x_start("up", [gW_up])

    dh1, d_o, dg2, dg3 = _mid_bwd(dh2, dc, h1, o_mix, norm_post_mix, norm_pre_ffn, deps=[x_up[4]])

    def dmerged_epi(ids, accs, ex, o):
        gs_ref, gp_ref, ys_ref, yp_ref = ex
        dm = accs[0]
        sg_s, sg_p = _sigmoid(gs_ref[...]), _sigmoid(gp_ref[...])
        o[0][...] = (dm * sg_s).astype(BF16)
        o[1][...] = (dm * sg_p).astype(BF16)
        o[2][...] = (dm * ys_ref[...] * sg_s * (1.0 - sg_s)).astype(BF16)
        o[3][...] = (dm * yp_ref[...] * sg_p * (1.0 - sg_p)).astype(BF16)

    dYs, dYp, dgs, dgp = _fused_matmul(
        "out_proj_dx", (N // tm2, NDEV, 1),
        [(d_o, BS((tm2, D), lambda i, q, k: (i, 0)), Wout, BS((U, D), lambda i, q, k: (q, 0)), "nt", 0)],
        [(tm2, U)],
        [(proj, gs_blk), (proj, gp_blk), (Ys, out_blk), (Yp, out_blk)],
        [(_sds((N, D), BF16), out_blk)] * 4,
        dmerged_epi)
    (gW_out,) = _fused_matmul(
        "out_proj_dw", (D // tm_d, D // tn_d, 1),
        [(merged, BS((N, tm_d), lambda i, j, k: (0, i)), d_o, BS((N, tn_d), lambda i, j, k: (0, j)), "tn", 0)],
        [(tm_d, tn_d)], [], [(_sds((D, D), BF16), BS((tm_d, tn_d), lambda i, j, k: (i, j)))],
        _store(lambda accs: accs))
    tm_s = _tile(DS, 512)
    gW_bs, gW_bp = _fused_matmul(
        "branch_dw", (DS // tm_s, NDEV, 1),
        [(ys, BS((N, tm_s), lambda i, q, k: (0, i)), dYs, BS((N, U), lambda i, q, k: (0, q)), "tn", 0),
         (yp, BS((N, tm_s), lambda i, q, k: (0, i)), dYp, BS((N, U), lambda i, q, k: (0, q)), "tn", 1)],
        [(tm_s, U), (tm_s, U)], [],
        [(_sds((NDEV, DS, U), BF16), BS((None, tm_s, U), lambda i, q, k: (q, i, 0)))] * 2,
        _store(lambda accs: accs))
    x_br = x_start("branch", [gW_bs, gW_bp, gW_out.reshape(NDEV, U, D)])

    tn_p = _tile(PG, 512)

    def dyp_epi(ids, accs, ex, o):
        q_ref, sc_ref = ex
        first = ids[1] == 0
        dyp = accs[0]
        dq = dyp * sc_ref[...]
        o[0][...] = dq.astype(BF16)
        _rowsum_into(o[1], first, dyp * q_ref[...])
        _rowsum_into(o[2], first, dq)

    dq, d_pscale, d_pb = _fused_matmul(
        "branch_pool_dx", (DP // tn_p, N // tm, 1),
        [(dYp, BS((tm, D), lambda j, i, k: (i, 0)), Wbp, BS((NDEV, tn_p, U), lambda j, i, k: (0, j, 0)), "nt_cat", 0)],
        [(tm, tn_p)],
        [(qp, BS((tm, tn_p), lambda j, i, k: (i, j))), (pool_scale, BS((1, tn_p), lambda j, i, k: (0, j)))],
        [(_sds((N, DP), BF16), BS((tm, tn_p), lambda j, i, k: (i, j))),
         (_sds((1, DP), F32), BS((1, tn_p), lambda j, i, k: (0, j))),
         (_sds((1, DP), F32), BS((1, tn_p), lambda j, i, k: (0, j)))],
        dyp_epi, deps=[x_br[4]])
    (dz,) = _fused_matmul(
        "pool_mm_dx", (N // tm, NPG, 1),
        [(dq, BS((tm, PG), lambda i, g, k: (i, g)), Wpool, BS((None, PG, PG), lambda i, g, k: (g, 0, 0)), "nt", 0)],
        [(tm, PG)], [], [(_sds((N, DP), F32), BS((tm, PG), lambda i, g, k: (i, g)))],
        _store(lambda accs: accs))
    (gW_pool,) = _fused_matmul(
        "pool_mm_dw", (NPG, 1),
        [(z, BS((N, PG), lambda g, k: (0, g)), dq, BS((N, PG), lambda g, k: (0, g)), "tn", 0)],
        [(PG, PG)], [], [(_sds((NPG, PG, PG), BF16), BS((None, PG, PG), lambda g, k: (g, 0, 0)))],
        _store(lambda accs: accs))
    du_pool = _pool_bwd(dz, nseq, L)

    def dys_epi(ids, accs, ex, o):
        zg_ref, y0_ref = ex
        first = ids[1] == 0
        dys = accs[0]
        sg = _sigmoid(zg_ref[...])
        dzg = dys * _gelu(y0_ref[...]) * sg * (1.0 - sg)
        o[0][...] = dzg.astype(BF16)
        o[1][...] = dys * sg
        _rowsum_into(o[2], first, dzg)

    dzg, dy1_direct, d_glu_b = _fused_matmul(
        "branch_ssm_dx", (DS // tn_s, N // tm, 1),
        [(dYs, BS((tm, D), lambda j, i, k: (i, 0)), Wbs, BS((NDEV, tn_s, U), lambda j, i, k: (0, j, 0)), "nt_cat", 0)],
        [(tm, tn_s)],
        [(zg, BS((tm, tn_s), lambda j, i, k: (i, j))), (y0, BS((tm, tn_s), lambda j, i, k: (i, j)))],
        [(_sds((N, DS), BF16), BS((tm, tn_s), lambda j, i, k: (i, j))),
         (_sds((N, DS), F32), BS((tm, tn_s), lambda j, i, k: (i, j))),
         (_sds((1, DS), F32), BS((1, tn_s), lambda j, i, k: (0, j)))],
        dys_epi)
    (gW_glu,) = _fused_matmul(
        "ssm_glu_dw", (DS // tm_s, DS // tn_s, 1),
        [(y1, BS((N, tm_s), lambda i, j, k: (0, i)), dzg, BS((N, tn_s), lambda i, j, k: (0, j)), "tn", 0)],
        [(tm_s, tn_s)], [], [(_sds((DS, DS), BF16), BS((tm_s, tn_s), lambda i, j, k: (i, j)))],
        _store(lambda accs: accs))
    x_mix = x_start("mix", [gW_glu.reshape(NDEV, DS // NDEV, DS),
                            gW_pool.reshape(NPG, NDEV, PG // NDEV, PG).transpose(1, 0, 2, 3)])

    tn_c = _tile(DS, CH)

    def dy0_epi(ids, accs, ex, o):
        d1_ref, y0_ref, u_ref = ex
        first = ids[1] == 0
        dy0 = (accs[0] + d1_ref[...]) * _gelu_grad(y0_ref[...])
        o[0][...] = dy0
        _rowsum_into(o[1], first, dy0 * u_ref[...])

    dy0, d_ssm_d = _fused_matmul(
        "ssm_glu_dx", (DS // tn_c, N // tm, 1),
        [(dzg, BS((tm, DS), lambda j, i, k: (i, 0)), Wglu, BS((tn_c, DS), lambda j, i, k: (j, 0)), "nt", 0)],
        [(tm, tn_c)],
        [(dy1_direct, BS((tm, tn_c), lambda j, i, k: (i, j))), (y0, BS((tm, tn_c), lambda j, i, k: (i, j))),
         (proj, BS((tm, tn_c), lambda j, i, k: (i, j)))],
        [(_sds((N, DS), F32), BS((tm, tn_c), lambda j, i, k: (i, j))),
         (_sds((1, DS), F32), BS((1, tn_c), lambda j, i, k: (0, j)))],
        dy0_epi, deps=[x_mix[4]])

    ds_re, ds_im = _fused_matmul(
        "ssm_out_dx", (N // tm2, NCH, 1),
        [(dy0, BS((tm2, CH), lambda i, c, k: (i, c)), WCre, BS((None, CS, CH), lambda i, c, k: (c, 0, 0)), "nt", 0),
         (dy0, BS((tm2, CH), lambda i, c, k: (i, c)), WCim, BS((None, CS, CH), lambda i, c, k: (c, 0, 0)), "nt", 1)],
        [(tm2, CS), (tm2, CS)], [],
        [(_sds((N, SL), F32), BS((tm2, CS), lambda i, c, k: (i, c)))] * 2,
        _store(lambda accs: accs))
    lam_r, lam_i, d_ab_re, d_ab_im = _scan_bwd(ds_re, ds_im, s_re, s_im, pw_re, pw_im, pwf_re, pwf_im, nseq, L)

    def du_epi(ids, accs, ex, o):
        dy0_ref, d_ref = ex
        o[0][...] = (accs[0] + dy0_ref[...] * d_ref[...]).astype(BF16)

    (du_ssm,) = _fused_matmul(
        "ssm_in_dx", (N // tm2, NCH, 1),
        [(lam_r, BS((tm2, CS), lambda i, c, k: (i, c)), WB, BS((None, CH, CS), lambda i, c, k: (c, 0, 0)), "nt", 0),
         (lam_i, BS((tm2, CS), lambda i, c, k: (i, c)), WB, BS((None, CH, CS), lambda i, c, k: (c, 0, 1)), "nt", 0)],
        [(tm2, CH)],
        [(dy0, BS((tm2, CH), lambda i, c, k: (i, c))), (ssm_d, BS((1, CH), lambda i, c, k: (0, c)))],
        [(_sds((N, DS), BF16), BS((tm2, CH), lambda i, c, k: (i, c)))],
        du_epi)
    dproj = jnp.concatenate([du_ssm, du_pool, dgs, dgp], axis=1)
    (gW_in,) = _fused_matmul(
        "in_proj_dw", (D // tm_d, nq, 1),
        [(a, BS((N, tm_d), lambda i, q, k: (0, i)), dproj, BS((N, U), lambda i, q, k: (0, q)), "tn", 0)],
        [(tm_d, U)], [], [(_sds((NDEV, D, 3 * U), BF16), BS((None, tm_d, U), lambda i, q, k: (q // 3, i, q % 3)))],
        _store(lambda accs: accs))
    x_in = x_start("in", [gW_in])
    (da,) = _fused_matmul(
        "in_proj_dx", (N // tm, D // tn_d2, NDEV),
        [(dproj, BS((tm, 3 * U), lambda i, j, k: (i, k)), Win, BS((None, tn_d2, 3 * U), lambda i, j, k: (k, j, 0)), "nt", 0)],
        [(tm, tn_d2)], [], [(_sds((N, D), F32), BS((tm, tn_d2), lambda i, j, k: (i, j)))],
        _store(lambda accs: accs), deps=[x_in[4]])
    grad_x, dg1 = _pre_bwd(x2, da, dh1, norm_pre_mix)

    dWCre, dWCim = _fused_matmul(
        "ssm_out_dw", (NCH, N // tk_n),
        [(s_re, BS((tk_n, CS), lambda c, k: (k, c)), dy0, BS((tk_n, CH), lambda c, k: (k, c)), "tn", 0),
         (s_im, BS((tk_n, CS), lambda c, k: (k, c)), dy0, BS((tk_n, CH), lambda c, k: (k, c)), "tn", 1)],
        [(CS, CH), (CS, CH)], [],
        [(_sds((NCH, CS, CH), F32), BS((None, CS, CH), lambda c, k: (c, 0, 0)))] * 2,
        _store(lambda accs: accs), deps=[x_in[4]])
    dWBre, dWBim = _fused_matmul(
        "ssm_in_dw", (NCH, N // tk_n),
        [(proj, BS((tk_n, CH), lambda c, k: (k, c)), lam_r, BS((tk_n, CS), lambda c, k: (k, c)), "tn", 0),
         (proj, BS((tk_n, CH), lambda c, k: (k, c)), lam_i, BS((tk_n, CS), lambda c, k: (k, c)), "tn", 1)],
        [(CH, CS), (CH, CS)], [],
        [(_sds((NCH, CH, CS), F32), BS((None, CH, CS), lambda c, k: (c, 0, 0)))] * 2,
        _store(lambda accs: accs), deps=[x_in[4]])
    d_bbr = _block_diag_in_grad(dWBre, J, G, P)
    d_bbi = _block_diag_in_grad(dWBim, J, G, P)
    d_lam_re, d_lam_im, d_log_step, d_br_t, d_bi_t = _ssm_param_bwd(
        lam_re, lam_im, log_step, br_t, bi_t,
        d_ab_re.reshape(nseq * SUBLANES, G, P), d_ab_im.reshape(nseq * SUBLANES, G, P), d_bbr, d_bbi)
    d_c_re = _block_diag_out_grad(dWCre, G, J, P)
    d_c_im = -_block_diag_out_grad(dWCim, G, J, P)

    d_conv_w = dcw.reshape(NB, 3, FC).transpose(1, 0, 2).reshape(3, F2)
    d_conv_b = dcb.reshape(1, F2)
    small = {
        "norm_pre_mix": dg1, "norm_post_mix": dg2, "norm_pre_ffn": dg3, "norm_post_ffn": dg4,
        "ssm_lambda_re": d_lam_re[None], "ssm_lambda_im": d_lam_im[None], "ssm_log_step": d_log_step.reshape(1, G),
        "ssm_b_re": d_br_t.transpose(1, 2, 0)[None], "ssm_b_im": d_bi_t.transpose(1, 2, 0)[None],
        "ssm_c_re": d_c_re[None], "ssm_c_im": d_c_im[None],
        "ssm_d": d_ssm_d, "ssm_glu_b": d_glu_b, "pool_scale": d_pscale,
        "pool_b": d_pb.reshape(1, NPG, PG), "ffn_conv_w": d_conv_w[None], "ffn_conv_b": d_conv_b,
    }
    assert list(small) == small_names
    packed = _pack([lossv] + [small[n] for n in small_names])
    st_small = _split_start("small_start", _broadcast_copies, NDEV - 1, [packed],
                            [_sds((NDEV,) + packed.shape, packed.dtype)], packed)

    res = {}
    after = st_small[4]
    for tag, started, group in (("down", x_down, ["w_down"]), ("up", x_up, ["w_up"]),
                                ("branch", x_br, ["w_branch_ssm", "w_branch_pool", "w_out"]),
                                ("mix", x_mix, ["ssm_glu_w", "pool_w"]), ("in", x_in, ["w_in"])):
        for n, parts in zip(group, x_finish(tag, started, after)):
            shape = args[n].shape
            if n == "w_up":
                flat, back = (lambda t: t[0].T), (lambda t: t.T[None])
            else:
                flat, back = (lambda t: t.reshape(-1, shape[-1])), (lambda t: t.reshape(shape))
            w2 = flat(args[n])
            g, dl, nm, nv = _adamw("adamw_" + n, w2, flat(args["m_" + n]), flat(args["v_" + n]),
                                   parts.reshape((NDEV,) + w2.shape))
            res[n] = tuple(back(t) for t in (g, dl, nm, nv))
            after = g

    srcs, lands = _split_wait("small_wait", _broadcast_copies, st_small, after)
    small_all = lax.dynamic_update_index_in_dim(lands[0], srcs[0], dev, 0)
    loss_rows = (D + SUBLANES * LANES - 1) // (SUBLANES * LANES) * SUBLANES
    total, loss = _small_sum(small_all, loss_rows, D)
    totals = dict(zip(small_names, _unpack(total, [lossv.shape] + [small[n].shape for n in small_names])[1:]))
    totals["pool_b"] = lax.dynamic_slice_in_dim(totals["pool_b"], dev * (PG // NDEV), PG // NDEV, axis=2)
    totals["ffn_conv_w"] = lax.dynamic_slice_in_dim(totals["ffn_conv_w"], dev * FC, FC, axis=2)
    sm_g = _pack([totals[n] for n in small_names])
    _, sm_d, sm_nm, sm_nv = _adamw("adamw_small", sm_w, sm_m, sm_v, sm_g[None])
    shapes = [args[n].shape for n in small_names]
    for n, dl, nm, nv in zip(small_names, _unpack(sm_d, shapes), _unpack(sm_nm, shapes), _unpack(sm_nv, shapes)):
        res[n] = (totals[n], dl, nm, nv)

    outs = [loss.reshape(()), grad_x.reshape(x.shape)]
    for k in range(4):
        outs += [res[n][k] for n in names]
    return tuple(outs)
```

```python
import functools
import math

import jax
import jax.numpy as jnp
from jax import lax
from jax.experimental import pallas as pl
from jax.experimental.pallas import tpu as pltpu

F32 = jnp.float32
BF16 = jnp.bfloat16
BS = pl.BlockSpec

NDEV = 8
SSM_GROUP = 16
SSM_STATE = 64
GROUPS_PER_CHUNK = 16
SCAN_UNROLL = 8
POOL_WINDOWS = (2, 4, 8, 16)
EPS = 1e-6
MIN_NEG_REAL = -1e-4
ADAM_LR, ADAM_B1, ADAM_B2, ADAM_EPS, ADAM_WD, ADAM_STEP = 0.001, 0.9, 0.999, 1e-08, 0.01, 10
LANES = 128
SUBLANES = 8
VMEM_LIMIT = 56 * 1024 * 1024

_DIMS = {"nn": (((1,), (0,)), ((), ())), "nt": (((1,), (1,)), ((), ())), "tn": (((0,), (0,)), ((), ()))}


def _tile(dim, pref, mult=LANES):
    if dim <= pref:
        return dim
    t = (pref // mult) * mult
    while t >= mult:
        if dim % t == 0:
            return t
        t -= mult
    return dim


def _pc(name, body, grid, ins, in_specs, outs, out_specs, scratch=(), deps=()):
    multi = isinstance(outs, (list, tuple))
    if deps:
        n_in, n_dep, inner = len(ins), len(deps), body

        def body(*refs):
            return inner(*refs[:n_in], *refs[n_in + n_dep:])

        ins = list(ins) + list(deps)
        in_specs = list(in_specs) + [BS(memory_space=pl.ANY)] * n_dep
    return pl.pallas_call(
        body, name=name, grid=grid, in_specs=list(in_specs),
        out_specs=list(out_specs) if multi else out_specs,
        out_shape=list(outs) if multi else outs, scratch_shapes=list(scratch),
        compiler_params=pltpu.CompilerParams(dimension_semantics=("arbitrary",) * len(grid),
                                             vmem_limit_bytes=VMEM_LIMIT),
    )(*ins)


def _sds(shape, dtype):
    return jax.ShapeDtypeStruct(tuple(shape), dtype)


def _gelu(x):
    k = math.sqrt(2.0 / math.pi)
    return 0.5 * x * (1.0 + jnp.tanh(k * (x + 0.044715 * (x * x * x))))


def _gelu_grad(x):
    k = math.sqrt(2.0 / math.pi)
    t = jnp.tanh(k * (x + 0.044715 * (x * x * x)))
    return 0.5 * (1.0 + t) + 0.5 * x * (1.0 - t * t) * (k * (1.0 + 3.0 * 0.044715 * x * x))


def _sigmoid(x):
    return jax.nn.sigmoid(x)


def _fused_matmul(name, grid, pairs, acc_shapes, extras, outs, epilogue, deps=()):
    n_p, n_e, n_o = len(pairs), len(extras), len(outs)
    rank = len(grid)
    nk = grid[-1]

    def body(*refs):
        ab = refs[:2 * n_p]
        ex = refs[2 * n_p:2 * n_p + n_e]
        o = refs[2 * n_p + n_e:2 * n_p + n_e + n_o]
        accs = refs[2 * n_p + n_e + n_o:]
        ids = [pl.program_id(d) for d in range(rank)]
        k = ids[-1]

        @pl.when(k == 0)
        def _():
            for acc in accs:
                acc[...] = jnp.zeros_like(acc)

        for p in range(n_p):
            a = ab[2 * p][...].astype(BF16)
            mode = pairs[p][4]
            if mode == "nt_cat":
                b_ref = ab[2 * p + 1]
                b = jnp.concatenate([b_ref[q].astype(BF16) for q in range(b_ref.shape[0])], axis=1)
                mode = "nt"
            else:
                b = ab[2 * p + 1][...].astype(BF16)
            accs[pairs[p][5]][...] += lax.dot_general(a, b, _DIMS[mode], preferred_element_type=F32)

        @pl.when(k == nk - 1)
        def _():
            epilogue(ids, [acc[...] for acc in accs], ex, o)

    ins, in_specs = [], []
    for a, a_spec, b, b_spec, _, _ in pairs:
        ins += [a, b]
        in_specs += [a_spec, b_spec]
    for e, e_spec in extras:
        ins.append(e)
        in_specs.append(e_spec)
    res = _pc(name, body, grid, ins, in_specs, [s for s, _ in outs], [sp for _, sp in outs],
              scratch=[pltpu.VMEM(tuple(s), F32) for s in acc_shapes], deps=deps)
    return res


def _store(vals):
    def epilogue(ids, accs, ex, o):
        for r, v in zip(o, vals(accs)):
            r[...] = v.astype(r.dtype)
    return epilogue


def _rowsum_into(ref, first, v):
    s = jnp.sum(v, axis=0, keepdims=True)

    @pl.when(first)
    def _():
        ref[...] = s

    @pl.when(jnp.logical_not(first))
    def _():
        ref[...] += s


def _mesh_pos():
    return lax.axis_index("x"), lax.axis_index("y"), lax.axis_index("c")


def _slot(p):
    return 4 * p[0] + 2 * p[1] + p[2]


_HBM = BS(memory_space=pltpu.HBM)
_SEM = BS(memory_space=pltpu.SEMAPHORE)
_ANY = BS(memory_space=pl.ANY)
_EFFECT = pltpu.SideEffectType.DATAFLOW_SIDE_EFFECTING


def _other_chips(x, y):
    return [(1 - x, y), (x, 1 - y), (1 - x, 1 - y)]


def _all_peers(x, y, c):
    peers = []
    for k in range(1, NDEV):
        kx, ky, kc = (k >> 2) & 1, (k >> 1) & 1, k & 1
        peers.append((1 - x if kx else x, 1 - y if ky else y, 1 - c if kc else c))
    return peers


def _gather_copies(src, land, send_sems, recv_sems, base):
    x, y, c = _mesh_pos()
    return [pltpu.make_async_remote_copy(
        src_ref=src, dst_ref=land.at[_slot((x, y, c))],
        send_sem=send_sems.at[base + k], recv_sem=recv_sems.at[base + k],
        device_id=(*chip, c), device_id_type=pl.DeviceIdType.MESH) for k, chip in enumerate(_other_chips(x, y))]


def _d2d_copies(src, land, send_sems, recv_sems, base):
    x, y, c = _mesh_pos()
    blocks = [(x, y, c)] + [(*chip, c) for chip in _other_chips(x, y)]
    return [pltpu.make_async_remote_copy(
        src_ref=src if k == 0 else land.at[_slot(b)], dst_ref=land.at[_slot(b)],
        send_sem=send_sems.at[base + k], recv_sem=recv_sems.at[base + k],
        device_id=(x, y, 1 - c), device_id_type=pl.DeviceIdType.MESH) for k, b in enumerate(blocks)]


def _broadcast_copies(src, land, send_sems, recv_sems, base):
    x, y, c = _mesh_pos()
    return [pltpu.make_async_remote_copy(
        src_ref=src, dst_ref=land.at[_slot((x, y, c))],
        send_sem=send_sems.at[base + k], recv_sem=recv_sems.at[base + k],
        device_id=peer, device_id_type=pl.DeviceIdType.MESH) for k, peer in enumerate(_all_peers(x, y, c))]


def _exchange_copies(src, land, send_sems, recv_sems, base):
    x, y, c = _mesh_pos()
    return [pltpu.make_async_remote_copy(
        src_ref=src.at[_slot(peer)], dst_ref=land.at[_slot((x, y, c))],
        send_sem=send_sems.at[base + k], recv_sem=recv_sems.at[base + k],
        device_id=peer, device_id_type=pl.DeviceIdType.MESH) for k, peer in enumerate(_all_peers(x, y, c))]


def _split_start(name, copies, ncopy, srcs, land_shapes, after):
    n = len(srcs)

    def body(*refs):
        src_refs, land_refs = refs[:n], refs[n:2 * n]
        send_sems, recv_sems = refs[2 * n + 1], refs[2 * n + 2]
        token = refs[-1]
        for r in range(n):
            for cp in copies(src_refs[r], land_refs[r], send_sems, recv_sems, r * ncopy):
                cp.start()
        token[...] = jnp.zeros_like(token)

    lands = [s if isinstance(s, jax.Array) else pltpu.with_memory_space_constraint(lax.empty(s.shape, s.dtype), pltpu.HBM)
             for s in land_shapes]
    ins = list(srcs) + lands
    out_shape = ([pltpu.SemaphoreType.DMA((n * ncopy,)), pltpu.SemaphoreType.DMA((n * ncopy,))]
                 + [pltpu.HBM(a.shape, a.dtype) for a in lands]
                 + [_sds((SUBLANES, LANES), F32)])
    res = pl.pallas_call(
        body, name=name, out_shape=out_shape,
        in_specs=[_HBM] * (2 * n) + [_ANY], out_specs=[_SEM, _SEM] + [_HBM] * n + [BS(memory_space=pltpu.VMEM)],
        input_output_aliases={n + i: 2 + i for i in range(n)},
        compiler_params=pltpu.CompilerParams(has_side_effects=_EFFECT),
    )(*ins, after)
    return res[0], res[1], list(srcs), list(res[2:2 + n]), res[-1]


def _split_wait(name, copies, started, after):
    send_sems, recv_sems, srcs, lands, _ = started
    n = len(srcs)
    ncopy = send_sems.shape[0] // n
    after = list(after) if isinstance(after, (list, tuple)) else [after]

    def body(*refs):
        src_refs, land_refs = refs[:n], refs[n:2 * n]
        send_sems, recv_sems = refs[2 * n], refs[2 * n + 1]
        for r in range(n):
            for cp in copies(src_refs[r], land_refs[r], send_sems, recv_sems, r * ncopy):
                cp.wait_send()
                cp.wait_recv()

    res = pl.pallas_call(
        body, name=name, out_shape=[pltpu.HBM(a.shape, a.dtype) for a in lands],
        in_specs=[_HBM] * (2 * n) + [_SEM, _SEM] + [_ANY] * len(after), out_specs=[_HBM] * n,
        input_output_aliases={n + i: i for i in range(n)},
        compiler_params=pltpu.CompilerParams(has_side_effects=_EFFECT),
    )(*srcs, *lands, send_sems, recv_sems, *after)
    return list(srcs), list(res)


def _adamw(name, w, m, v, parts):
    R, C = w.shape
    S = parts.shape[0]
    tr = _tile(R, max(SUBLANES, (256 * 1024) // C), SUBLANES)

    def body(w_ref, m_ref, v_ref, p_ref, g_ref, d_ref, nm_ref, nv_ref):
        g = p_ref[0].astype(F32)
        for s in range(1, S):
            g = g + p_ref[s].astype(F32)
        m2 = ADAM_B1 * m_ref[...] + (1.0 - ADAM_B1) * g
        v2 = ADAM_B2 * v_ref[...] + (1.0 - ADAM_B2) * (g * g)
        m_hat = m2 / (1.0 - ADAM_B1 ** ADAM_STEP)
        v_hat = v2 / (1.0 - ADAM_B2 ** ADAM_STEP)
        g_ref[...] = g
        d_ref[...] = -ADAM_LR * (m_hat / (jnp.sqrt(v_hat) + ADAM_EPS) + ADAM_WD * w_ref[...])
        nm_ref[...] = m2
        nv_ref[...] = v2

    blk = BS((tr, C), lambda i: (i, 0))
    return _pc(name, body, (R // tr,), [w, m, v, parts],
               [blk, blk, blk, BS((S, tr, C), lambda i: (0, i, 0))],
               [_sds((R, C), F32)] * 4, [blk] * 4)


def _ssm_disc(lam_re, lam_im, log_step, br_t, bi_t):
    lr = jnp.minimum(lam_re, MIN_NEG_REAL)
    li = lam_im
    dt = jnp.exp(log_step)
    mag = jnp.exp(lr * dt)
    ang = li * dt
    ab_re = mag * jnp.cos(ang)
    ab_im = mag * jnp.sin(ang)
    nr = ab_re - 1.0
    ni = ab_im
    den = lr * lr + li * li
    f_re = (nr * lr + ni * li) / den
    f_im = (ni * lr - nr * li) / den
    bb_re = f_re[None] * br_t - f_im[None] * bi_t
    bb_im = f_re[None] * bi_t + f_im[None] * br_t
    return ab_re, ab_im, bb_re, bb_im


def _ssm_param_fwd(lam_re, lam_im, log_step, br_t, bi_t):
    G, P = lam_re.shape

    def body(lr_ref, li_ref, ls_ref, br_ref, bi_ref, pw_re_ref, pw_im_ref, pwf_re_ref, pwf_im_ref, bbr_ref, bbi_ref):
        ab_re, ab_im, bb_re, bb_im = _ssm_disc(lr_ref[...], li_ref[...], ls_ref[...], br_ref[...], bi_ref[...])
        bbr_ref[...] = bb_re
        bbi_ref[...] = bb_im
        pr, pi = ab_re, ab_im
        for r in range(SUBLANES):
            pw_re_ref[r] = pr
            pw_im_ref[r] = pi
            pwf_re_ref[SUBLANES - 1 - r] = pr
            pwf_im_ref[SUBLANES - 1 - r] = pi
            pr, pi = pr * ab_re - pi * ab_im, pr * ab_im + pi * ab_re

    full = lambda a: BS(a.shape, lambda i: (0,) * a.ndim)
    ins = [lam_re, lam_im, log_step, br_t, bi_t]
    outs = [_sds((SUBLANES, G, P), F32)] * 4 + [_sds(br_t.shape, F32)] * 2
    return _pc("ssm_param_fwd", body, (1,), ins, [full(a) for a in ins], outs, [full(o) for o in outs])


def _ssm_param_bwd(lam_re, lam_im, log_step, br_t, bi_t, d_ab_re, d_ab_im, d_bbr, d_bbi):
    def body(lr_ref, li_ref, ls_ref, br_ref, bi_ref, dar_ref, dai_ref, dbr_ref, dbi_ref,
             o_lr, o_li, o_ls, o_br, o_bi):
        prim = (lr_ref[...], li_ref[...], ls_ref[...], br_ref[...], bi_ref[...])
        _, vjp = jax.vjp(_ssm_disc, *prim)
        dar = dar_ref[0]
        dai = dai_ref[0]
        for k in range(1, dar_ref.shape[0]):
            dar = dar + dar_ref[k]
            dai = dai + dai_ref[k]
        g = vjp((dar, dai, dbr_ref[...], dbi_ref[...]))
        for r, v in zip((o_lr, o_li, o_ls, o_br, o_bi), g):
            r[...] = v

    full = lambda a: BS(a.shape, lambda i: (0,) * a.ndim)
    ins = [lam_re, lam_im, log_step, br_t, bi_t, d_ab_re, d_ab_im, d_bbr, d_bbi]
    outs = [_sds(a.shape, F32) for a in (lam_re, lam_im, log_step, br_t, bi_t)]
    return _pc("ssm_param_bwd", body, (1,), ins, [full(a) for a in ins], outs, [full(o) for o in outs])


def _bcast_row(ref, r, w):
    return jnp.broadcast_to(ref[pl.ds(r, 1), :], (SUBLANES, w))


def _pick_row(x, row, r):
    return jnp.broadcast_to(jnp.sum(jnp.where(row == r, x, 0.0), axis=0, keepdims=True), x.shape)


def _scan_fwd(bu_re, bu_im, pw_re, pw_im, nseq, L):
    N, SL = bu_re.shape
    W = _tile(SL, 256)
    unroll = math.gcd(L // SUBLANES, SCAN_UNROLL)

    def body(bre_ref, bim_ref, pre_ref, pim_ref, sre_ref, sim_ref):
        pre, pim = pre_ref[...], pim_ref[...]
        steps = [(k, _bcast_row(pre_ref, k - 1, W), _bcast_row(pim_ref, k - 1, W)) for k in (1, 2, 4)]
        row = lax.broadcasted_iota(jnp.int32, (SUBLANES, W), 0)

        def step(i, carry):
            cr, ci = carry
            r0 = pl.multiple_of(i * SUBLANES, SUBLANES)
            xr = bre_ref[pl.ds(r0, SUBLANES), :]
            xi = bim_ref[pl.ds(r0, SUBLANES), :]
            for k, ar, ai in steps:
                sr = pltpu.roll(xr, k, axis=0)
                si = pltpu.roll(xi, k, axis=0)
                keep = row >= k
                xr, xi = (xr + jnp.where(keep, ar * sr - ai * si, 0.0),
                          xi + jnp.where(keep, ar * si + ai * sr, 0.0))
            xr, xi = xr + (pre * cr - pim * ci), xi + (pre * ci + pim * cr)
            sre_ref[pl.ds(r0, SUBLANES), :] = xr
            sim_ref[pl.ds(r0, SUBLANES), :] = xi
            return _pick_row(xr, row, SUBLANES - 1), _pick_row(xi, row, SUBLANES - 1)

        def group(g, carry):
            for u in range(unroll):
                carry = step(g * unroll + u, carry)
            return carry

        zero = jnp.zeros((SUBLANES, W), F32)
        lax.fori_loop(0, L // SUBLANES // unroll, group, (zero, zero))

    blk = BS((L, W), lambda s, j: (s, j))
    pw = BS((SUBLANES, W), lambda s, j: (0, j))
    return _pc("ssm_scan_fwd", body, (nseq, SL // W), [bu_re, bu_im, pw_re, pw_im], [blk, blk, pw, pw],
               [_sds((N, SL), F32)] * 2, [blk, blk])


def _scan_bwd(ds_re, ds_im, s_re, s_im, pw_re, pw_im, pwf_re, pwf_im, nseq, L):
    N, SL = ds_re.shape
    W = _tile(SL, 256)
    nt = L // SUBLANES
    unroll = math.gcd(nt, SCAN_UNROLL)

    def body(dsr_ref, dsi_ref, sre_ref, sim_ref, pre_ref, pim_ref, fre_ref, fim_ref,
             lre_ref, lim_ref, dar_ref, dai_ref):
        fre, fim = fre_ref[...], -fim_ref[...]
        steps = [(k, _bcast_row(pre_ref, k - 1, W), -_bcast_row(pim_ref, k - 1, W)) for k in (1, 2, 4)]
        row = lax.broadcasted_iota(jnp.int32, (SUBLANES, W), 0)

        def step(ii, carry):
            cr, ci, acr, aci = carry
            i = nt - 1 - ii
            r0 = pl.multiple_of(i * SUBLANES, SUBLANES)
            xr = dsr_ref[pl.ds(r0, SUBLANES), :]
            xi = dsi_ref[pl.ds(r0, SUBLANES), :]
            for k, ar, ai in steps:
                sr = pltpu.roll(xr, SUBLANES - k, axis=0)
                si = pltpu.roll(xi, SUBLANES - k, axis=0)
                keep = row < SUBLANES - k
                xr, xi = (xr + jnp.where(keep, ar * sr - ai * si, 0.0),
                          xi + jnp.where(keep, ar * si + ai * sr, 0.0))
            xr, xi = xr + (fre * cr - fim * ci), xi + (fre * ci + fim * cr)
            lre_ref[pl.ds(r0, SUBLANES), :] = xr
            lim_ref[pl.ds(r0, SUBLANES), :] = xi
            p0 = pl.multiple_of(jnp.maximum(i - 1, 0) * SUBLANES, SUBLANES)
            has_prev = i > 0
            spr = jnp.where(row == 0,
                            jnp.where(has_prev, pltpu.roll(sre_ref[pl.ds(p0, SUBLANES), :], 1, axis=0), 0.0),
                            pltpu.roll(sre_ref[pl.ds(r0, SUBLANES), :], 1, axis=0))
            spi = jnp.where(row == 0,
                            jnp.where(has_prev, pltpu.roll(sim_ref[pl.ds(p0, SUBLANES), :], 1, axis=0), 0.0),
                            pltpu.roll(sim_ref[pl.ds(r0, SUBLANES), :], 1, axis=0))
            acr = acr + (xr * spr + xi * spi)
            aci = aci + (xi * spr - xr * spi)
            return _pick_row(xr, row, 0), _pick_row(xi, row, 0), acr, aci

        def group(g, carry):
            for u in range(unroll):
                carry = step(g * unroll + u, carry)
            return carry

        zero = jnp.zeros((SUBLANES, W), F32)
        _, _, acr, aci = lax.fori_loop(0, nt // unroll, group, (zero, zero, zero, zero))
        dar_ref[...] = acr
        dai_ref[...] = aci

    blk = BS((L, W), lambda s, j: (s, j))
    pw = BS((SUBLANES, W), lambda s, j: (0, j))
    da = BS((None, SUBLANES, W), lambda s, j: (s, 0, j))
    return _pc("ssm_scan_bwd", body, (nseq, SL // W),
               [ds_re, ds_im, s_re, s_im, pw_re, pw_im, pwf_re, pwf_im], [blk] * 4 + [pw] * 4,
               [_sds((N, SL), F32)] * 2 + [_sds((nseq, SUBLANES, SL), F32)] * 2, [blk, blk, da, da])


def _pool_select(g, vals):
    return jnp.where(g == 0, vals[0], jnp.where(g == 1, vals[1], jnp.where(g == 2, vals[2], vals[3])))


def _pool_fwd(proj, col0, DP, nseq, L):
    N = proj.shape[0]
    PG = DP // len(POOL_WINDOWS)
    W = _tile(PG, 256)

    def body(v_ref, z_ref):
        g = pl.program_id(1) // (PG // W)
        v = v_ref[...]
        row = lax.broadcasted_iota(jnp.int32, (L, W), 0)
        sums, s, k = [], v, 1
        for _ in POOL_WINDOWS:
            s = s + jnp.where(row >= k, pltpu.roll(s, k, axis=0), 0.0)
            sums.append(s)
            k *= 2
        win = _pool_select(g, [float(w) for w in POOL_WINDOWS])
        cnt = jnp.minimum((row + 1).astype(F32), win)
        z_ref[...] = (_pool_select(g, sums) / cnt - v).astype(z_ref.dtype)

    return _pc("pool_fwd", body, (nseq, DP // W), [proj], [BS((L, W), lambda s, j: (s, col0 // W + j))],
               _sds((N, DP), BF16), BS((L, W), lambda s, j: (s, j)))


def _pool_bwd(dz, nseq, L):
    N, DP = dz.shape
    PG = DP // len(POOL_WINDOWS)
    W = _tile(PG, 256)

    def body(dz_ref, dv_ref):
        g = pl.program_id(1) // (PG // W)
        d = dz_ref[...]
        row = lax.broadcasted_iota(jnp.int32, (L, W), 0)
        win = _pool_select(g, [float(w) for w in POOL_WINDOWS])
        s = d / jnp.minimum((row + 1).astype(F32), win)
        sums, k = [], 1
        for _ in POOL_WINDOWS:
            s = s + jnp.where(row < L - k, pltpu.roll(s, L - k, axis=0), 0.0)
            sums.append(s)
            k *= 2
        dv_ref[...] = (_pool_select(g, sums) - d).astype(dv_ref.dtype)

    blk = BS((L, W), lambda s, j: (s, j))
    return _pc("pool_bwd", body, (nseq, DP // W), [dz], [blk], _sds((N, DP), BF16), blk)


def _rstd(x):
    return lax.rsqrt(jnp.mean(x * x, axis=-1, keepdims=True) + EPS)


def _norm_bwd(dy, xhat, rstd, gain):
    t = dy * gain
    return rstd * (t - xhat * jnp.mean(t * xhat, axis=-1, keepdims=True))


def _pre_norm(x, g1):
    N, D = x.shape
    tr = _tile(N, 128, SUBLANES)

    def body(x_ref, g_ref, a_ref):
        xv = x_ref[...]
        a_ref[...] = (xv * _rstd(xv) * g_ref[...]).astype(a_ref.dtype)

    row = BS((tr, D), lambda i: (i, 0))
    vec = BS((1, D), lambda i: (0, 0))
    return _pc("pre_norm", body, (N // tr,), [x, g1], [row, vec], _sds((N, D), BF16), row)


def _mid_norm(x, o, g2, g3):
    N, D = x.shape
    tr = _tile(N, 128, SUBLANES)

    def body(x_ref, o_ref, g2_ref, g3_ref, h1_ref, c_ref):
        ov = o_ref[...]
        h1 = x_ref[...] + ov * _rstd(ov) * g2_ref[...]
        h1_ref[...] = h1
        c_ref[...] = (h1 * _rstd(h1) * g3_ref[...]).astype(c_ref.dtype)

    row = BS((tr, D), lambda i: (i, 0))
    vec = BS((1, D), lambda i: (0, 0))
    return _pc("mid_norm", body, (N // tr,), [x, o, g2, g3], [row, row, vec, vec],
               [_sds((N, D), F32), _sds((N, D), BF16)], [row, row])


def _post_ffn(h1, dn, tgt, g4):
    N, D = h1.shape
    tr = _tile(N, 128, SUBLANES)

    def body(h1_ref, dn_ref, t_ref, g_ref, dh2_ref, ddn_ref, lossv_ref, dg_ref):
        first = pl.program_id(0) == 0
        dnv = dn_ref[...]
        rstd = _rstd(dnv)
        xhat = dnv * rstd
        gain = g_ref[...]
        err = (h1_ref[...] + xhat * gain) - t_ref[...]
        dh2 = err / float(D)
        dh2_ref[...] = dh2
        ddn_ref[...] = _norm_bwd(dh2, xhat, rstd, gain).astype(ddn_ref.dtype)
        _rowsum_into(lossv_ref, first, err * err)
        _rowsum_into(dg_ref, first, dh2 * xhat)

    row = BS((tr, D), lambda i: (i, 0))
    vec = BS((1, D), lambda i: (0, 0))
    return _pc("post_ffn", body, (N // tr,), [h1, dn, tgt, g4], [row, row, row, vec],
               [_sds((N, D), F32), _sds((N, D), BF16), _sds((1, D), F32), _sds((1, D), F32)], [row, row, vec, vec])


def _mid_bwd(dh2, dc, h1, o, g2, g3, deps=()):
    N, D = h1.shape
    tr = _tile(N, 128, SUBLANES)

    def body(dh2_ref, dc_ref, h1_ref, o_ref, g2_ref, g3_ref, dh1_ref, do_ref, dg2_ref, dg3_ref):
        first = pl.program_id(0) == 0
        h1 = h1_ref[...]
        r3 = _rstd(h1)
        hc = h1 * r3
        dcv = dc_ref[...]
        dh1 = dh2_ref[...] + _norm_bwd(dcv, hc, r3, g3_ref[...])
        dh1_ref[...] = dh1
        ov = o_ref[...]
        r2 = _rstd(ov)
        ho = ov * r2
        do_ref[...] = _norm_bwd(dh1, ho, r2, g2_ref[...]).astype(do_ref.dtype)
        _rowsum_into(dg3_ref, first, dcv * hc)
        _rowsum_into(dg2_ref, first, dh1 * ho)

    row = BS((tr, D), lambda i: (i, 0))
    vec = BS((1, D), lambda i: (0, 0))
    return _pc("mid_bwd", body, (N // tr,), [dh2, dc, h1, o, g2, g3], [row] * 4 + [vec, vec],
               [_sds((N, D), F32), _sds((N, D), BF16), _sds((1, D), F32), _sds((1, D), F32)], [row, row, vec, vec],
               deps=deps)


def _pre_bwd(x, da, dh1, g1):
    N, D = x.shape
    tr = _tile(N, 128, SUBLANES)

    def body(x_ref, da_ref, dh1_ref, g_ref, dx_ref, dg_ref):
        first = pl.program_id(0) == 0
        xv = x_ref[...]
        r1 = _rstd(xv)
        xh = xv * r1
        dav = da_ref[...]
        dx_ref[...] = dh1_ref[...] + _norm_bwd(dav, xh, r1, g_ref[...])
        _rowsum_into(dg_ref, first, dav * xh)

    row = BS((tr, D), lambda i: (i, 0))
    vec = BS((1, D), lambda i: (0, 0))
    return _pc("pre_bwd", body, (N // tr,), [x, da, dh1, g1], [row, row, row, vec],
               [_sds((N, D), F32), _sds((1, D), F32)], [row, vec])


def _conv_rows(x_ref, halo_ref, first):
    x = x_ref[...]
    xx = jnp.concatenate([jnp.where(first, 0.0, halo_ref[...]), x], axis=0)
    x1 = pltpu.roll(xx, 1, axis=0)[SUBLANES:]
    x2 = pltpu.roll(xx, 2, axis=0)[SUBLANES:]
    return x, x1, x2


def _conv_apply(rows, w_ref, b_ref):
    x, x1, x2 = rows
    return ((b_ref[...] + x2 * w_ref[pl.ds(0, 1), :]) + x1 * w_ref[pl.ds(1, 1), :]) + x * w_ref[pl.ds(2, 1), :]


def _gate_specs(N, FC, TR, half):
    tile = BS((None, TR, FC), lambda jj, i: (jj + half, i, 0))
    halo = BS((None, SUBLANES, FC), lambda jj, i: (jj + half, jnp.maximum(i * (TR // SUBLANES) - 1, 0), 0))
    cw = BS((None, 3, FC), lambda jj, i: (jj + half, 0, 0))
    cb = BS((None, 1, FC), lambda jj, i: (jj + half, 0, 0))
    return tile, halo, cw, cb


def _gate_fwd(up_pre, cw, cb, L, deps=()):
    nb, N, FC = up_pre.shape
    half = nb // 2
    TR = _tile(L, 128, SUBLANES)

    def body(xa_ref, ha_ref, wa_ref, ba_ref, xb_ref, hb_ref, wb_ref, bb_ref, f_ref):
        first = (pl.program_id(1) % (L // TR)) == 0
        ua = _conv_apply(_conv_rows(xa_ref, ha_ref, first), wa_ref, ba_ref)
        ub = _conv_apply(_conv_rows(xb_ref, hb_ref, first), wb_ref, bb_ref)
        f_ref[...] = (_gelu(ua) * ub).astype(f_ref.dtype)

    sa, sb = _gate_specs(N, FC, TR, 0), _gate_specs(N, FC, TR, half)
    return _pc("gate_fwd", body, (half, N // TR), [up_pre, up_pre, cw, cb] * 2, list(sa) + list(sb),
               _sds((half, N, FC), BF16), BS((None, TR, FC), lambda jj, i: (jj, i, 0)), deps=deps)


def _gate_bwd(up_pre, cw, cb, df, L, deps=()):
    nb, N, FC = up_pre.shape
    half = nb // 2
    TR = _tile(L, 128, SUBLANES)

    def body(xa_ref, ha_ref, wa_ref, ba_ref, xb_ref, hb_ref, wb_ref, bb_ref, df_ref, dup_ref, dw_ref, dbias_ref):
        i = pl.program_id(1)
        first_row = i == 0
        first = (i % (L // TR)) == 0
        ra = _conv_rows(xa_ref, ha_ref, first)
        rb = _conv_rows(xb_ref, hb_ref, first)
        ua = _conv_apply(ra, wa_ref, ba_ref)
        ub = _conv_apply(rb, wb_ref, bb_ref)
        dfv = df_ref[...].astype(F32)
        dua = dfv * ub * _gelu_grad(ua)
        dub = dfv * _gelu(ua)
        dup_ref[0] = dua.astype(dup_ref.dtype)
        dup_ref[1] = dub.astype(dup_ref.dtype)
        for h, (rows, du) in enumerate(((ra, dua), (rb, dub))):
            x, x1, x2 = rows
            _rowsum_into(dbias_ref.at[h], first_row, du)
            for k, xs in enumerate((x2, x1, x)):
                _rowsum_into(dw_ref.at[h, pl.ds(k, 1), :], first_row, du * xs)

    sa, sb = _gate_specs(N, FC, TR, 0), _gate_specs(N, FC, TR, half)
    tile = BS((None, TR, FC), lambda jj, i: (jj, i, 0))
    both = BS((2, None, TR, FC), lambda jj, i: (0, jj, i, 0))
    dw = BS((2, None, 3, FC), lambda jj, i: (0, jj, 0, 0))
    dbias = BS((2, None, 1, FC), lambda jj, i: (0, jj, 0, 0))
    return _pc("gate_bwd", body, (half, N // TR), [up_pre, up_pre, cw, cb] * 2 + [df], list(sa) + list(sb) + [tile],
               [_sds((2, half, N, FC), BF16), _sds((2, half, 3, FC), F32), _sds((2, half, 1, FC), F32)],
               [both, dw, dbias], deps=deps)


def _conv_bwd(dup, cw, L):
    nb, N, FC = dup.shape
    TR = _tile(L, 128, 2 * SUBLANES)
    HR = 2 * SUBLANES
    nrb = N // HR

    def body(x_ref, h_ref, w_ref, o_ref):
        last = ((pl.program_id(1) + 1) % (L // TR)) == 0
        x = x_ref[...].astype(F32)
        xx = jnp.concatenate([x, jnp.where(last, 0.0, h_ref[...].astype(F32))], axis=0)
        x1 = pltpu.roll(xx, TR + HR - 1, axis=0)[:TR]
        x2 = pltpu.roll(xx, TR + HR - 2, axis=0)[:TR]
        o_ref[...] = (x * w_ref[pl.ds(2, 1), :] + x1 * w_ref[pl.ds(1, 1), :] + x2 * w_ref[pl.ds(0, 1), :]
                      ).astype(o_ref.dtype)

    tile = BS((None, TR, FC), lambda jj, i: (jj, i, 0))
    halo = BS((None, HR, FC), lambda jj, i: (jj, jnp.minimum((i + 1) * (TR // HR), nrb - 1), 0))
    w = BS((None, 3, FC), lambda jj, i: (jj, 0, 0))
    return _pc("conv_bwd", body, (nb, N // TR), [dup, dup, cw], [tile, halo, w], _sds((nb, N, FC), BF16), tile)


def _pack(arrs):
    parts = []
    for a in arrs:
        flat = a.reshape(-1).astype(F32)
        pad = (-flat.shape[0]) % (SUBLANES * LANES)
        parts.append(jnp.pad(flat, (0, pad)))
    return jnp.concatenate(parts).reshape(-1, LANES)


def _unpack(packed, shapes):
    flat = packed.reshape(-1)
    out, off = [], 0
    for s in shapes:
        n = math.prod(s)
        out.append(flat[off:off + n].reshape(s))
        off += n + ((-n) % (SUBLANES * LANES))
    return out


def _small_sum(gathered, loss_rows, d_model):
    S, R, C = gathered.shape

    def body(p_ref, tot_ref, loss_ref):
        t = p_ref[0]
        for s in range(1, S):
            t = t + p_ref[s]
        tot_ref[...] = t
        loss_ref[...] = jnp.full((1, 1), 0.5 / d_model, F32) * jnp.sum(t[:loss_rows])

    return _pc("small_sum", body, (1,), [gathered], [BS((S, R, C), lambda i: (0, 0, 0))],
               [_sds((R, C), F32), _sds((1, 1), F32)], [BS((R, C), lambda i: (0, 0)), BS((1, 1), lambda i: (0, 0))])


def _block_diag_in(bb_t, nch):
    J, G, P = bb_t.shape
    gl = G // nch
    b = bb_t.reshape(J, nch, gl, P).transpose(1, 0, 2, 3)
    eye = jnp.eye(gl, dtype=F32)
    w = eye[None, :, None, :, None] * b[:, None, :, :, :]
    return w.reshape(nch, gl * J, gl * P)


def _block_diag_in_grad(dw, J, G, P):
    nch = dw.shape[0]
    gl = G // nch
    d = dw.reshape(nch, gl, J, gl, P)
    d = jnp.einsum("cgjgp->jcgp", d)
    return d.reshape(J, G, P)


def _block_diag_out(c, nch):
    G, J, P = c.shape
    gl = G // nch
    cc = c.reshape(nch, gl, J, P).transpose(0, 1, 3, 2)
    eye = jnp.eye(gl, dtype=F32)
    w = cc[:, :, :, None, :] * eye[None, :, None, :, None]
    return w.reshape(nch, gl * P, gl * J)


def _block_diag_out_grad(dw, G, J, P):
    nch = dw.shape[0]
    gl = G // nch
    d = dw.reshape(nch, gl, P, gl, J)
    d = jnp.einsum("cgpgj->cgjp", d)
    return d.reshape(G, J, P)


def kernel(x, norm_pre_mix, w_in, ssm_lambda_re, ssm_lambda_im, ssm_log_step, ssm_b_re, ssm_b_im, ssm_c_re, ssm_c_im, ssm_d, ssm_glu_w, ssm_glu_b, pool_w, pool_b, pool_scale, w_branch_ssm, w_branch_pool, w_out, norm_post_mix, norm_pre_ffn, w_up, ffn_conv_w, ffn_conv_b, w_down, norm_post_ffn, loss_target, m_norm_pre_mix, m_w_in, m_ssm_lambda_re, m_ssm_lambda_im, m_ssm_log_step, m_ssm_b_re, m_ssm_b_im, m_ssm_c_re, m_ssm_c_im, m_ssm_d, m_ssm_glu_w, m_ssm_glu_b, m_pool_w, m_pool_b, m_pool_scale, m_w_branch_ssm, m_w_branch_pool, m_w_out, m_norm_post_mix, m_norm_pre_ffn, m_w_up, m_ffn_conv_w, m_ffn_conv_b, m_w_down, m_norm_post_ffn, v_norm_pre_mix, v_w_in, v_ssm_lambda_re, v_ssm_lambda_im, v_ssm_log_step, v_ssm_b_re, v_ssm_b_im, v_ssm_c_re, v_ssm_c_im, v_ssm_d, v_ssm_glu_w, v_ssm_glu_b, v_pool_w, v_pool_b, v_pool_scale, v_w_branch_ssm, v_w_branch_pool, v_w_out, v_norm_post_mix, v_norm_pre_ffn, v_w_up, v_ffn_conv_w, v_ffn_conv_b, v_w_down, v_norm_post_ffn):
    args = dict(locals())
    names = ["norm_pre_mix", "w_in", "ssm_lambda_re", "ssm_lambda_im", "ssm_log_step", "ssm_b_re", "ssm_b_im",
             "ssm_c_re", "ssm_c_im", "ssm_d", "ssm_glu_w", "ssm_glu_b", "pool_w", "pool_b", "pool_scale",
             "w_branch_ssm", "w_branch_pool", "w_out", "norm_post_mix", "norm_pre_ffn", "w_up", "ffn_conv_w",
             "ffn_conv_b", "w_down", "norm_post_ffn"]

    nseq, L, D = x.shape
    N = nseq * L
    U = D // NDEV
    DS = ssm_d.shape[1]
    DP = pool_scale.shape[1]
    G, P, J = ssm_b_re.shape[1:]
    SL = G * P
    CH = GROUPS_PER_CHUNK * J
    CS = GROUPS_PER_CHUNK * P
    NCH = DS // CH
    NPG = len(POOL_WINDOWS)
    PG = DP // NPG
    FC = w_up.shape[2]
    NB = NDEV
    HB = NB // 2
    F2 = NB * FC
    dev = _slot(_mesh_pos())
    tm = _tile(N, 1024)
    tm2 = _tile(N, 512)

    x2 = x.reshape(N, D)
    tgt = loss_target.reshape(N, D)

    def bf(t):
        return t.astype(BF16)

    def g_start(tag, group, after):
        return _split_start("gather_start_" + tag, _gather_copies, 3, group,
                            [_sds((NDEV,) + s.shape, s.dtype) for s in group], after)

    def g_land(tag, started, after):
        srcs, lands = _split_wait("gather_wait_" + tag, _gather_copies, started, after)
        return _split_start("d2d_start_" + tag, _d2d_copies, 4, srcs, lands, srcs[0])

    def g_finish(tag, d2d, after):
        srcs, lands = _split_wait("d2d_wait_" + tag, _d2d_copies, d2d, after)
        return [lax.dynamic_update_index_in_dim(l, s, dev, 0) for l, s in zip(lands, srcs)]

    def x_start(tag, group):
        return _split_start("exchange_start_" + tag, _exchange_copies, NDEV - 1, group,
                            [_sds(g.shape, g.dtype) for g in group], group[0])

    def x_finish(tag, started, after):
        srcs, lands = _split_wait("exchange_wait_" + tag, _exchange_copies, started, after)
        own = [lax.dynamic_index_in_dim(s, dev, 0, keepdims=False) for s in srcs]
        return [lax.dynamic_update_index_in_dim(l, o, dev, 0) for l, o in zip(lands, own)]

    st_in = g_start("in", [bf(w_in[0])], x2)
    st_mix = g_start("mix", [bf(ssm_glu_w[0]), bf(pool_w[0]), pool_b[0], ffn_conv_w[0]], st_in[4])
    conv_b_blk = ffn_conv_b.reshape(NB, 1, FC)

    (_, x2e, w_branch_ssm, w_branch_pool, w_out, w_up, w_down, ssm_lambda_re, ssm_lambda_im, ssm_log_step,
     ssm_b_re, ssm_b_im, ssm_c_re, ssm_c_im) = lax.optimization_barrier(
        (st_mix[4], x2, w_branch_ssm, w_branch_pool, w_out, w_up, w_down, ssm_lambda_re, ssm_lambda_im, ssm_log_step,
         ssm_b_re, ssm_b_im, ssm_c_re, ssm_c_im))
    lam_re, lam_im = ssm_lambda_re[0], ssm_lambda_im[0]
    log_step = ssm_log_step.reshape(G, 1)
    br_t = ssm_b_re[0].transpose(2, 0, 1)
    bi_t = ssm_b_im[0].transpose(2, 0, 1)
    pw_re3, pw_im3, pwf_re3, pwf_im3, bb_re, bb_im = _ssm_param_fwd(lam_re, lam_im, log_step, br_t, bi_t)
    pw_re, pw_im = pw_re3.reshape(SUBLANES, SL), pw_im3.reshape(SUBLANES, SL)
    pwf_re, pwf_im = pwf_re3.reshape(SUBLANES, SL), pwf_im3.reshape(SUBLANES, SL)
    WB = jnp.concatenate([_block_diag_in(bb_re, NCH), _block_diag_in(bb_im, NCH)], axis=2).astype(BF16)
    WCre = _block_diag_out(ssm_c_re[0], NCH).astype(BF16)
    WCim = _block_diag_out(-ssm_c_im[0], NCH).astype(BF16)
    a = _pre_norm(x2e, norm_pre_mix)
    small_names = ["norm_pre_mix", "norm_post_mix", "norm_pre_ffn", "norm_post_ffn", "ssm_lambda_re", "ssm_lambda_im",
                   "ssm_log_step", "ssm_b_re", "ssm_b_im", "ssm_c_re", "ssm_c_im", "ssm_d", "ssm_glu_b", "pool_scale",
                   "pool_b", "ffn_conv_w", "ffn_conv_b"]
    _, small_in = lax.optimization_barrier(
        (st_mix[4], [[args[p + n] for n in small_names] for p in ("", "m_", "v_")]))
    sm_w, sm_m, sm_v = (_pack(group) for group in small_in)
    g_br = [bf(w_branch_ssm[0]), bf(w_branch_pool[0]), bf(w_out[0])]
    g_up, g_down = [bf(w_up[0].T)], [bf(w_down[0])]
    early = [WB, WCre, WCim, pwf_re, pwf_im, a, sm_w, sm_m, sm_v] + g_br + g_up + g_down

    d_in = g_land("in", st_in, [st_mix[4]] + early)
    st_br = g_start("branch", g_br, d_in[4])
    st_up = g_start("up", g_up, st_br[4])
    st_down = g_start("down", g_down, st_up[4])
    (Win,) = g_finish("in", d_in, st_down[4])

    nq = 3 * NDEV
    (proj,) = _fused_matmul(
        "in_proj", (N // tm, nq, 1),
        [(a, BS((tm, D), lambda i, q, k: (i, 0)), Win, BS((None, D, U), lambda i, q, k: (q // 3, 0, q % 3)), "nn", 0)],
        [(tm, U)], [], [(_sds((N, 3 * D), F32), BS((tm, U), lambda i, q, k: (i, q)))],
        _store(lambda accs: accs))
    d_mix = g_land("mix", st_mix, proj)
    d_br = g_land("branch", st_br, d_mix[4])

    bu_re, bu_im = _fused_matmul(
        "ssm_in", (N // tm2, NCH, 1),
        [(proj, BS((tm2, CH), lambda i, c, k: (i, c)), WB, BS((None, CH, 2 * CS), lambda i, c, k: (c, 0, 0)), "nn", 0)],
        [(tm2, 2 * CS)], [],
        [(_sds((N, SL), F32), BS((tm2, CS), lambda i, c, k: (i, c)))] * 2,
        _store(lambda accs: (accs[0][:, :CS], accs[0][:, CS:])), deps=[d_br[4]])
    s_re, s_im = _scan_fwd(bu_re, bu_im, pw_re, pw_im, nseq, L)

    def ssm_out_epi(ids, accs, ex, o):
        u_ref, d_ref = ex
        y0 = accs[0] + d_ref[...] * u_ref[...]
        o[0][...] = y0
        o[1][...] = _gelu(y0).astype(BF16)

    y0, y1 = _fused_matmul(
        "ssm_out", (N // tm2, NCH, 1),
        [(s_re, BS((tm2, CS), lambda i, c, k: (i, c)), WCre, BS((None, CS, CH), lambda i, c, k: (c, 0, 0)), "nn", 0),
         (s_im, BS((tm2, CS), lambda i, c, k: (i, c)), WCim, BS((None, CS, CH), lambda i, c, k: (c, 0, 0)), "nn", 0)],
        [(tm2, CH)],
        [(proj, BS((tm2, CH), lambda i, c, k: (i, c))), (ssm_d, BS((1, CH), lambda i, c, k: (0, c)))],
        [(_sds((N, DS), F32), BS((tm2, CH), lambda i, c, k: (i, c))),
         (_sds((N, DS), BF16), BS((tm2, CH), lambda i, c, k: (i, c)))],
        ssm_out_epi)

    Wglu, Wpool, pool_b_all, conv_w_all = g_finish("mix", d_mix, y1)
    Wglu = Wglu.reshape(DS, DS)
    Wpool = Wpool.transpose(1, 0, 2, 3).reshape(NPG, PG, PG)
    pool_b_full = pool_b_all.transpose(1, 0, 2).reshape(1, DP)
    tn_s = _tile(DS, 512)

    def glu_epi(ids, accs, ex, o):
        y0_ref, b_ref = ex
        zg = accs[0] + b_ref[...]
        o[0][...] = zg
        o[1][...] = (_gelu(y0_ref[...]) * _sigmoid(zg)).astype(BF16)

    zg, ys = _fused_matmul(
        "ssm_glu", (N // tm, DS // tn_s, 1),
        [(y1, BS((tm, DS), lambda i, j, k: (i, 0)), Wglu, BS((DS, tn_s), lambda i, j, k: (0, j)), "nn", 0)],
        [(tm, tn_s)],
        [(y0, BS((tm, tn_s), lambda i, j, k: (i, j))), (ssm_glu_b, BS((1, tn_s), lambda i, j, k: (0, j)))],
        [(_sds((N, DS), F32), BS((tm, tn_s), lambda i, j, k: (i, j))),
         (_sds((N, DS), BF16), BS((tm, tn_s), lambda i, j, k: (i, j)))],
        glu_epi)

    z = _pool_fwd(proj, DS, DP, nseq, L)

    def pool_mm_epi(ids, accs, ex, o):
        b_ref, sc_ref = ex
        q = accs[0] + b_ref[...]
        o[0][...] = q
        o[1][...] = (q * sc_ref[...]).astype(BF16)

    qp, yp = _fused_matmul(
        "pool_mm", (N // tm, NPG, 1),
        [(z, BS((tm, PG), lambda i, g, k: (i, g)), Wpool, BS((None, PG, PG), lambda i, g, k: (g, 0, 0)), "nn", 0)],
        [(tm, PG)],
        [(pool_b_full, BS((1, PG), lambda i, g, k: (0, g))), (pool_scale, BS((1, PG), lambda i, g, k: (0, g)))],
        [(_sds((N, DP), F32), BS((tm, PG), lambda i, g, k: (i, g))),
         (_sds((N, DP), BF16), BS((tm, PG), lambda i, g, k: (i, g)))],
        pool_mm_epi)

    Wbs, Wbp, Wout = g_finish("branch", d_br, yp)
    Wout = Wout.reshape(D, D)
    gs_blk = BS((tm2, U), lambda i, q, k: (i, (DS + DP) // U + q))
    gp_blk = BS((tm2, U), lambda i, q, k: (i, (DS + DP + D) // U + q))
    out_blk = BS((tm2, U), lambda i, q, k: (i, q))

    def branch_epi(ids, accs, ex, o):
        gs_ref, gp_ref = ex
        o[0][...] = accs[0]
        o[1][...] = accs[1]
        o[2][...] = (_sigmoid(gs_ref[...]) * accs[0] + _sigmoid(gp_ref[...]) * accs[1]).astype(BF16)

    Ys, Yp, merged = _fused_matmul(
        "branch", (N // tm2, NDEV, 1),
        [(ys, BS((tm2, DS), lambda i, q, k: (i, 0)), Wbs, BS((None, DS, U), lambda i, q, k: (q, 0, 0)), "nn", 0),
         (yp, BS((tm2, DP), lambda i, q, k: (i, 0)), Wbp, BS((None, DP, U), lambda i, q, k: (q, 0, 0)), "nn", 1)],
        [(tm2, U), (tm2, U)],
        [(proj, gs_blk), (proj, gp_blk)],
        [(_sds((N, D), F32), out_blk), (_sds((N, D), F32), out_blk), (_sds((N, D), BF16), out_blk)],
        branch_epi)
    d_up = g_land("up", st_up, merged)

    tn_d = _tile(D, 512)
    (o_mix,) = _fused_matmul(
        "out_proj", (N // tm, D // tn_d, 1),
        [(merged, BS((tm, D), lambda i, j, k: (i, 0)), Wout, BS((D, tn_d), lambda i, j, k: (0, j)), "nn", 0)],
        [(tm, tn_d)], [], [(_sds((N, D), F32), BS((tm, tn_d), lambda i, j, k: (i, j)))],
        _store(lambda accs: accs), deps=[d_up[4]])
    h1, c = _mid_norm(x2, o_mix, norm_post_mix, norm_pre_ffn)

    (Wup,) = g_finish("up", d_up, c)
    tk_d = _tile(D, 1024)
    tk_up = _tile(D, 2048)
    (up_pre,) = _fused_matmul(
        "ffn_up", (N // tm2, NB, D // tk_up),
        [(c, BS((tm2, tk_up), lambda i, j, k: (i, k)), Wup, BS((None, FC, tk_up), lambda i, j, k: (j, 0, k)), "nt", 0)],
        [(tm2, FC)], [], [(_sds((NB, N, FC), F32), BS((None, tm2, FC), lambda i, j, k: (j, i, 0)))],
        _store(lambda accs: accs))
    d_down = g_land("down", st_down, up_pre)
    f = _gate_fwd(up_pre, conv_w_all, conv_b_blk, L, deps=[d_down[4]])
    (Wdown,) = g_finish("down", d_down, f)
    Wdown = Wdown.reshape(HB, FC, D)
    tn_d2 = _tile(D, 1024)
    (dn,) = _fused_matmul(
        "ffn_down", (N // tm2, D // tn_d2, HB),
        [(f, BS((None, tm2, FC), lambda i, j, k: (k, i, 0)), Wdown, BS((None, FC, tn_d2), lambda i, j, k: (k, 0, j)), "nn", 0)],
        [(tm2, tn_d2)], [], [(_sds((N, D), F32), BS((tm2, tn_d2), lambda i, j, k: (i, j)))],
        _store(lambda accs: accs))
    dh2, d_dn, lossv, dg4 = _post_ffn(h1, dn, tgt, norm_post_ffn)

    (df,) = _fused_matmul(
        "ffn_down_dx", (N // tm2, HB, D // tk_d),
        [(d_dn, BS((tm2, tk_d), lambda i, j, k: (i, k)), Wdown, BS((None, FC, tk_d), lambda i, j, k: (j, 0, k)), "nt", 0)],
        [(tm2, FC)], [], [(_sds((HB, N, FC), BF16), BS((None, tm2, FC), lambda i, j, k: (j, i, 0)))],
        _store(lambda accs: accs))
    tk_n = _tile(N, 1024)
    (gW_down,) = _fused_matmul(
        "ffn_down_dw", (HB, D // tn_d, N // tk_n),
        [(f, BS((None, tk_n, FC), lambda j, n, k: (j, k, 0)), d_dn, BS((tk_n, tn_d), lambda j, n, k: (k, n)), "tn", 0)],
        [(FC, tn_d)], [], [(_sds((HB, FC, D), BF16), BS((None, FC, tn_d), lambda j, n, k: (j, 0, n)))],
        _store(lambda accs: accs))
    x_down = x_start("down", [gW_down.reshape(NDEV, FC // 2, D)])
    dup, dcw, dcb = _gate_bwd(up_pre, conv_w_all, conv_b_blk, df, L, deps=[x_down[4]])
    dpre = _conv_bwd(dup.reshape(NB, N, FC), conv_w_all, L)
    (dc,) = _fused_matmul(
        "ffn_up_dx", (N // tm2, D // tn_d2, NB),
        [(dpre, BS((None, tm2, FC), lambda i, j, k: (k, i, 0)), Wup, BS((None, FC, tn_d2), lambda i, j, k: (k, 0, j)), "nn", 0)],
        [(tm2, tn_d2)], [], [(_sds((N, D), F32), BS((tm2, tn_d2), lambda i, j, k: (i, j)))],
        _store(lambda accs: accs))
    tm_d = _tile(D, 512)
    (gW_up,) = _fused_matmul(
        "ffn_up_dw", (NB, D // tm_d, N // tk_n),
        [(dpre, BS((None, tk_n, FC), lambda j, n, k: (j, k, 0)), c, BS((tk_n, tm_d), lambda j, n, k: (k, n)), "tn", 0)],
        [(FC, tm_d)], [], [(_sds((NB, FC, D), BF16), BS((None, FC, tm_d), lambda j, n, k: (j, 0, n)))],
        _store(lambda accs: accs))
    x_up = x_start("up", [gW_up])

    dh1, d_o, dg2, dg3 = _mid_bwd(dh2, dc, h1, o_mix, norm_post_mix, norm_pre_ffn, deps=[x_up[4]])

    def dmerged_epi(ids, accs, ex, o):
        gs_ref, gp_ref, ys_ref, yp_ref = ex
        dm = accs[0]
        sg_s, sg_p = _sigmoid(gs_ref[...]), _sigmoid(gp_ref[...])
        o[0][...] = (dm * sg_s).astype(BF16)
        o[1][...] = (dm * sg_p).astype(BF16)
        o[2][...] = (dm * ys_ref[...] * sg_s * (1.0 - sg_s)).astype(BF16)
        o[3][...] = (dm * yp_ref[...] * sg_p * (1.0 - sg_p)).astype(BF16)

    dYs, dYp, dgs, dgp = _fused_matmul(
        "out_proj_dx", (N // tm2, NDEV, 1),
        [(d_o, BS((tm2, D), lambda i, q, k: (i, 0)), Wout, BS((U, D), lambda i, q, k: (q, 0)), "nt", 0)],
        [(tm2, U)],
        [(proj, gs_blk), (proj, gp_blk), (Ys, out_blk), (Yp, out_blk)],
        [(_sds((N, D), BF16), out_blk)] * 4,
        dmerged_epi)
    (gW_out,) = _fused_matmul(
        "out_proj_dw", (D // tm_d, D // tn_d, 1),
        [(merged, BS((N, tm_d), lambda i, j, k: (0, i)), d_o, BS((N, tn_d), lambda i, j, k: (0, j)), "tn", 0)],
        [(tm_d, tn_d)], [], [(_sds((D, D), BF16), BS((tm_d, tn_d), lambda i, j, k: (i, j)))],
        _store(lambda accs: accs))
    tm_s = _tile(DS, 512)
    gW_bs, gW_bp = _fused_matmul(
        "branch_dw", (DS // tm_s, NDEV, 1),
        [(ys, BS((N, tm_s), lambda i, q, k: (0, i)), dYs, BS((N, U), lambda i, q, k: (0, q)), "tn", 0),
         (yp, BS((N, tm_s), lambda i, q, k: (0, i)), dYp, BS((N, U), lambda i, q, k: (0, q)), "tn", 1)],
        [(tm_s, U), (tm_s, U)], [],
        [(_sds((NDEV, DS, U), BF16), BS((None, tm_s, U), lambda i, q, k: (q, i, 0)))] * 2,
        _store(lambda accs: accs))
    x_br = x_start("branch", [gW_bs, gW_bp, gW_out.reshape(NDEV, U, D)])

    tn_p = _tile(PG, 512)

    def dyp_epi(ids, accs, ex, o):
        q_ref, sc_ref = ex
        first = ids[1] == 0
        dyp = accs[0]
        dq = dyp * sc_ref[...]
        o[0][...] = dq.astype(BF16)
        _rowsum_into(o[1], first, dyp * q_ref[...])
        _rowsum_into(o[2], first, dq)

    dq, d_pscale, d_pb = _fused_matmul(
        "branch_pool_dx", (DP // tn_p, N // tm, 1),
        [(dYp, BS((tm, D), lambda j, i, k: (i, 0)), Wbp, BS((NDEV, tn_p, U), lambda j, i, k: (0, j, 0)), "nt_cat", 0)],
        [(tm, tn_p)],
        [(qp, BS((tm, tn_p), lambda j, i, k: (i, j))), (pool_scale, BS((1, tn_p), lambda j, i, k: (0, j)))],
        [(_sds((N, DP), BF16), BS((tm, tn_p), lambda j, i, k: (i, j))),
         (_sds((1, DP), F32), BS((1, tn_p), lambda j, i, k: (0, j))),
         (_sds((1, DP), F32), BS((1, tn_p), lambda j, i, k: (0, j)))],
        dyp_epi, deps=[x_br[4]])
    (dz,) = _fused_matmul(
        "pool_mm_dx", (N // tm, NPG, 1),
        [(dq, BS((tm, PG), lambda i, g, k: (i, g)), Wpool, BS((None, PG, PG), lambda i, g, k: (g, 0, 0)), "nt", 0)],
        [(tm, PG)], [], [(_sds((N, DP), F32), BS((tm, PG), lambda i, g, k: (i, g)))],
        _store(lambda accs: accs))
    (gW_pool,) = _fused_matmul(
        "pool_mm_dw", (NPG, 1),
        [(z, BS((N, PG), lambda g, k: (0, g)), dq, BS((N, PG), lambda g, k: (0, g)), "tn", 0)],
        [(PG, PG)], [], [(_sds((NPG, PG, PG), BF16), BS((None, PG, PG), lambda g, k: (g, 0, 0)))],
        _store(lambda accs: accs))
    du_pool = _pool_bwd(dz, nseq, L)

    def dys_epi(ids, accs, ex, o):
        zg_ref, y0_ref = ex
        first = ids[1] == 0
        dys = accs[0]
        sg = _sigmoid(zg_ref[...])
        dzg = dys * _gelu(y0_ref[...]) * sg * (1.0 - sg)
        o[0][...] = dzg.astype(BF16)
        o[1][...] = dys * sg
        _rowsum_into(o[2], first, dzg)

    dzg, dy1_direct, d_glu_b = _fused_matmul(
        "branch_ssm_dx", (DS // tn_s, N // tm, 1),
        [(dYs, BS((tm, D), lambda j, i, k: (i, 0)), Wbs, BS((NDEV, tn_s, U), lambda j, i, k: (0, j, 0)), "nt_cat", 0)],
        [(tm, tn_s)],
        [(zg, BS((tm, tn_s), lambda j, i, k: (i, j))), (y0, BS((tm, tn_s), lambda j, i, k: (i, j)))],
        [(_sds((N, DS), BF16), BS((tm, tn_s), lambda j, i, k: (i, j))),
         (_sds((N, DS), F32), BS((tm, tn_s), lambda j, i, k: (i, j))),
         (_sds((1, DS), F32), BS((1, tn_s), lambda j, i, k: (0, j)))],
        dys_epi)
    (gW_glu,) = _fused_matmul(
        "ssm_glu_dw", (DS // tm_s, DS // tn_s, 1),
        [(y1, BS((N, tm_s), lambda i, j, k: (0, i)), dzg, BS((N, tn_s), lambda i, j, k: (0, j)), "tn", 0)],
        [(tm_s, tn_s)], [], [(_sds((DS, DS), BF16), BS((tm_s, tn_s), lambda i, j, k: (i, j)))],
        _store(lambda accs: accs))
    x_mix = x_start("mix", [gW_glu.reshape(NDEV, DS // NDEV, DS),
                            gW_pool.reshape(NPG, NDEV, PG // NDEV, PG).transpose(1, 0, 2, 3)])

    tn_c = _tile(DS, CH)

    def dy0_epi(ids, accs, ex, o):
        d1_ref, y0_ref, u_ref = ex
        first = ids[1] == 0
        dy0 = (accs[0] + d1_ref[...]) * _gelu_grad(y0_ref[...])
        o[0][...] = dy0
        _rowsum_into(o[1], first, dy0 * u_ref[...])

    dy0, d_ssm_d = _fused_matmul(
        "ssm_glu_dx", (DS // tn_c, N // tm, 1),
        [(dzg, BS((tm, DS), lambda j, i, k: (i, 0)), Wglu, BS((tn_c, DS), lambda j, i, k: (j, 0)), "nt", 0)],
        [(tm, tn_c)],
        [(dy1_direct, BS((tm, tn_c), lambda j, i, k: (i, j))), (y0, BS((tm, tn_c), lambda j, i, k: (i, j))),
         (proj, BS((tm, tn_c), lambda j, i, k: (i, j)))],
        [(_sds((N, DS), F32), BS((tm, tn_c), lambda j, i, k: (i, j))),
         (_sds((1, DS), F32), BS((1, tn_c), lambda j, i, k: (0, j)))],
        dy0_epi, deps=[x_mix[4]])

    ds_re, ds_im = _fused_matmul(
        "ssm_out_dx", (N // tm2, NCH, 1),
        [(dy0, BS((tm2, CH), lambda i, c, k: (i, c)), WCre, BS((None, CS, CH), lambda i, c, k: (c, 0, 0)), "nt", 0),
         (dy0, BS((tm2, CH), lambda i, c, k: (i, c)), WCim, BS((None, CS, CH), lambda i, c, k: (c, 0, 0)), "nt", 1)],
        [(tm2, CS), (tm2, CS)], [],
        [(_sds((N, SL), F32), BS((tm2, CS), lambda i, c, k: (i, c)))] * 2,
        _store(lambda accs: accs))
    lam_r, lam_i, d_ab_re, d_ab_im = _scan_bwd(ds_re, ds_im, s_re, s_im, pw_re, pw_im, pwf_re, pwf_im, nseq, L)

    def du_epi(ids, accs, ex, o):
        dy0_ref, d_ref = ex
        o[0][...] = (accs[0] + dy0_ref[...] * d_ref[...]).astype(BF16)

    (du_ssm,) = _fused_matmul(
        "ssm_in_dx", (N // tm2, NCH, 1),
        [(lam_r, BS((tm2, CS), lambda i, c, k: (i, c)), WB, BS((None, CH, CS), lambda i, c, k: (c, 0, 0)), "nt", 0),
         (lam_i, BS((tm2, CS), lambda i, c, k: (i, c)), WB, BS((None, CH, CS), lambda i, c, k: (c, 0, 1)), "nt", 0)],
        [(tm2, CH)],
        [(dy0, BS((tm2, CH), lambda i, c, k: (i, c))), (ssm_d, BS((1, CH), lambda i, c, k: (0, c)))],
        [(_sds((N, DS), BF16), BS((tm2, CH), lambda i, c, k: (i, c)))],
        du_epi)
    dproj = jnp.concatenate([du_ssm, du_pool, dgs, dgp], axis=1)
    (gW_in,) = _fused_matmul(
        "in_proj_dw", (D // tm_d, nq, 1),
        [(a, BS((N, tm_d), lambda i, q, k: (0, i)), dproj, BS((N, U), lambda i, q, k: (0, q)), "tn", 0)],
        [(tm_d, U)], [], [(_sds((NDEV, D, 3 * U), BF16), BS((None, tm_d, U), lambda i, q, k: (q // 3, i, q % 3)))],
        _store(lambda accs: accs))
    x_in = x_start("in", [gW_in])
    (da,) = _fused_matmul(
        "in_proj_dx", (N // tm, D // tn_d2, NDEV),
        [(dproj, BS((tm, 3 * U), lambda i, j, k: (i, k)), Win, BS((None, tn_d2, 3 * U), lambda i, j, k: (k, j, 0)), "nt", 0)],
        [(tm, tn_d2)], [], [(_sds((N, D), F32), BS((tm, tn_d2), lambda i, j, k: (i, j)))],
        _store(lambda accs: accs), deps=[x_in[4]])
    grad_x, dg1 = _pre_bwd(x2, da, dh1, norm_pre_mix)

    dWCre, dWCim = _fused_matmul(
        "ssm_out_dw", (NCH, N // tk_n),
        [(s_re, BS((tk_n, CS), lambda c, k: (k, c)), dy0, BS((tk_n, CH), lambda c, k: (k, c)), "tn", 0),
         (s_im, BS((tk_n, CS), lambda c, k: (k, c)), dy0, BS((tk_n, CH), lambda c, k: (k, c)), "tn", 1)],
        [(CS, CH), (CS, CH)], [],
        [(_sds((NCH, CS, CH), F32), BS((None, CS, CH), lambda c, k: (c, 0, 0)))] * 2,
        _store(lambda accs: accs), deps=[x_in[4]])
    dWBre, dWBim = _fused_matmul(
        "ssm_in_dw", (NCH, N // tk_n),
        [(proj, BS((tk_n, CH), lambda c, k: (k, c)), lam_r, BS((tk_n, CS), lambda c, k: (k, c)), "tn", 0),
         (proj, BS((tk_n, CH), lambda c, k: (k, c)), lam_i, BS((tk_n, CS), lambda c, k: (k, c)), "tn", 1)],
        [(CH, CS), (CH, CS)], [],
        [(_sds((NCH, CH, CS), F32), BS((None, CH, CS), lambda c, k: (c, 0, 0)))] * 2,
        _store(lambda accs: accs), deps=[x_in[4]])
    d_bbr = _block_diag_in_grad(dWBre, J, G, P)
    d_bbi = _block_diag_in_grad(dWBim, J, G, P)
    d_lam_re, d_lam_im, d_log_step, d_br_t, d_bi_t = _ssm_param_bwd(
        lam_re, lam_im, log_step, br_t, bi_t,
        d_ab_re.reshape(nseq * SUBLANES, G, P), d_ab_im.reshape(nseq * SUBLANES, G, P), d_bbr, d_bbi)
    d_c_re = _block_diag_out_grad(dWCre, G, J, P)
    d_c_im = -_block_diag_out_grad(dWCim, G, J, P)

    d_conv_w = dcw.reshape(NB, 3, FC).transpose(1, 0, 2).reshape(3, F2)
    d_conv_b = dcb.reshape(1, F2)
    small = {
        "norm_pre_mix": dg1, "norm_post_mix": dg2, "norm_pre_ffn": dg3, "norm_post_ffn": dg4,
        "ssm_lambda_re": d_lam_re[None], "ssm_lambda_im": d_lam_im[None], "ssm_log_step": d_log_step.reshape(1, G),
        "ssm_b_re": d_br_t.transpose(1, 2, 0)[None], "ssm_b_im": d_bi_t.transpose(1, 2, 0)[None],
        "ssm_c_re": d_c_re[None], "ssm_c_im": d_c_im[None],
        "ssm_d": d_ssm_d, "ssm_glu_b": d_glu_b, "pool_scale": d_pscale,
        "pool_b": d_pb.reshape(1, NPG, PG), "ffn_conv_w": d_conv_w[None], "ffn_conv_b": d_conv_b,
    }
    assert list(small) == small_names
    packed = _pack([lossv] + [small[n] for n in small_names])
    st_small = _split_start("small_start", _broadcast_copies, NDEV - 1, [packed],
                            [_sds((NDEV,) + packed.shape, packed.dtype)], packed)

    res = {}
    after = st_small[4]
    for tag, started, group in (("down", x_down, ["w_down"]), ("up", x_up, ["w_up"]),
                                ("branch", x_br, ["w_branch_ssm", "w_branch_pool", "w_out"]),
                                ("mix", x_mix, ["ssm_glu_w", "pool_w"]), ("in", x_in, ["w_in"])):
        for n, parts in zip(group, x_finish(tag, started, after)):
            shape = args[n].shape
            if n == "w_up":
                flat, back = (lambda t: t[0].T), (lambda t: t.T[None])
            else:
                flat, back = (lambda t: t.reshape(-1, shape[-1])), (lambda t: t.reshape(shape))
            w2 = flat(args[n])
            g, dl, nm, nv = _adamw("adamw_" + n, w2, flat(args["m_" + n]), flat(args["v_" + n]),
                                   parts.reshape((NDEV,) + w2.shape))
            res[n] = tuple(back(t) for t in (g, dl, nm, nv))
            after = g

    srcs, lands = _split_wait("small_wait", _broadcast_copies, st_small, after)
    small_all = lax.dynamic_update_index_in_dim(lands[0], srcs[0], dev, 0)
    loss_rows = (D + SUBLANES * LANES - 1) // (SUBLANES * LANES) * SUBLANES
    total, loss = _small_sum(small_all, loss_rows, D)
    totals = dict(zip(small_names, _unpack(total, [lossv.shape] + [small[n].shape for n in small_names])[1:]))
    totals["pool_b"] = lax.dynamic_slice_in_dim(totals["pool_b"], dev * (PG // NDEV), PG // NDEV, axis=2)
    totals["ffn_conv_w"] = lax.dynamic_slice_in_dim(totals["ffn_conv_w"], dev * FC, FC, axis=2)
    sm_g = _pack([totals[n] for n in small_names])
    _, sm_d, sm_nm, sm_nv = _adamw("adamw_small", sm_w, sm_m, sm_v, sm_g[None])
    shapes = [args[n].shape for n in small_names]
    for n, dl, nm, nv in zip(small_names, _unpack(sm_d, shapes), _unpack(sm_nm, shapes), _unpack(sm_nv, shapes)):
        res[n] = (totals[n], dl, nm, nv)

    outs = [loss.reshape(()), grad_x.reshape(x.shape)]
    for k in range(4):
        outs += [res[n][k] for n in names]
    return tuple(outs)
```

```python
import functools
import math

import jax
import jax.numpy as jnp
from jax import lax
from jax.experimental import pallas as pl
from jax.experimental.pallas import tpu as pltpu

F32 = jnp.float32
BF16 = jnp.bfloat16
BS = pl.BlockSpec

NDEV = 8
SSM_GROUP = 16
SSM_STATE = 64
GROUPS_PER_CHUNK = 16
SCAN_UNROLL = 8
POOL_WINDOWS = (2, 4, 8, 16)
EPS = 1e-6
MIN_NEG_REAL = -1e-4
ADAM_LR, ADAM_B1, ADAM_B2, ADAM_EPS, ADAM_WD, ADAM_STEP = 0.001, 0.9, 0.999, 1e-08, 0.01, 10
LANES = 128
SUBLANES = 8
VMEM_LIMIT = 56 * 1024 * 1024

_DIMS = {"nn": (((1,), (0,)), ((), ())), "nt": (((1,), (1,)), ((), ())), "tn": (((0,), (0,)), ((), ()))}


def _tile(dim, pref, mult=LANES):
    if dim <= pref:
        return dim
    t = (pref // mult) * mult
    while t >= mult:
        if dim % t == 0:
            return t
        t -= mult
    return dim


def _pc(name, body, grid, ins, in_specs, outs, out_specs, scratch=(), deps=()):
    multi = isinstance(outs, (list, tuple))
    if deps:
        n_in, n_dep, inner = len(ins), len(deps), body

        def body(*refs):
            return inner(*refs[:n_in], *refs[n_in + n_dep:])

        ins = list(ins) + list(deps)
        in_specs = list(in_specs) + [BS(memory_space=pl.ANY)] * n_dep
    return pl.pallas_call(
        body, name=name, grid=grid, in_specs=list(in_specs),
        out_specs=list(out_specs) if multi else out_specs,
        out_shape=list(outs) if multi else outs, scratch_shapes=list(scratch),
        compiler_params=pltpu.CompilerParams(dimension_semantics=("arbitrary",) * len(grid),
                                             vmem_limit_bytes=VMEM_LIMIT),
    )(*ins)


def _sds(shape, dtype):
    return jax.ShapeDtypeStruct(tuple(shape), dtype)


def _gelu(x):
    k = math.sqrt(2.0 / math.pi)
    return 0.5 * x * (1.0 + jnp.tanh(k * (x + 0.044715 * (x * x * x))))


def _gelu_grad(x):
    k = math.sqrt(2.0 / math.pi)
    t = jnp.tanh(k * (x + 0.044715 * (x * x * x)))
    return 0.5 * (1.0 + t) + 0.5 * x * (1.0 - t * t) * (k * (1.0 + 3.0 * 0.044715 * x * x))


def _sigmoid(x):
    return jax.nn.sigmoid(x)


def _fused_matmul(name, grid, pairs, acc_shapes, extras, outs, epilogue, deps=()):
    n_p, n_e, n_o = len(pairs), len(extras), len(outs)
    rank = len(grid)
    nk = grid[-1]

    def body(*refs):
        ab = refs[:2 * n_p]
        ex = refs[2 * n_p:2 * n_p + n_e]
        o = refs[2 * n_p + n_e:2 * n_p + n_e + n_o]
        accs = refs[2 * n_p + n_e + n_o:]
        ids = [pl.program_id(d) for d in range(rank)]
        k = ids[-1]

        @pl.when(k == 0)
        def _():
            for acc in accs:
                acc[...] = jnp.zeros_like(acc)

        for p in range(n_p):
            a = ab[2 * p][...].astype(BF16)
            mode = pairs[p][4]
            if mode == "nt_cat":
                b_ref = ab[2 * p + 1]
                b = jnp.concatenate([b_ref[q].astype(BF16) for q in range(b_ref.shape[0])], axis=1)
                mode = "nt"
            else:
                b = ab[2 * p + 1][...].astype(BF16)
            accs[pairs[p][5]][...] += lax.dot_general(a, b, _DIMS[mode], preferred_element_type=F32)

        @pl.when(k == nk - 1)
        def _():
            epilogue(ids, [acc[...] for acc in accs], ex, o)

    ins, in_specs = [], []
    for a, a_spec, b, b_spec, _, _ in pairs:
        ins += [a, b]
        in_specs += [a_spec, b_spec]
    for e, e_spec in extras:
        ins.append(e)
        in_specs.append(e_spec)
    res = _pc(name, body, grid, ins, in_specs, [s for s, _ in outs], [sp for _, sp in outs],
              scratch=[pltpu.VMEM(tuple(s), F32) for s in acc_shapes], deps=deps)
    return res


def _store(vals):
    def epilogue(ids, accs, ex, o):
        for r, v in zip(o, vals(accs)):
            r[...] = v.astype(r.dtype)
    return epilogue


def _rowsum_into(ref, first, v):
    s = jnp.sum(v, axis=0, keepdims=True)

    @pl.when(first)
    def _():
        ref[...] = s

    @pl.when(jnp.logical_not(first))
    def _():
        ref[...] += s


def _mesh_pos():
    return lax.axis_index("x"), lax.axis_index("y"), lax.axis_index("c")


def _slot(p):
    return 4 * p[0] + 2 * p[1] + p[2]


_HBM = BS(memory_space=pltpu.HBM)
_SEM = BS(memory_space=pltpu.SEMAPHORE)
_ANY = BS(memory_space=pl.ANY)
_EFFECT = pltpu.SideEffectType.DATAFLOW_SIDE_EFFECTING


def _other_chips(x, y):
    return [(1 - x, y), (x, 1 - y), (1 - x, 1 - y)]


def _all_peers(x, y, c):
    peers = []
    for k in range(1, NDEV):
        kx, ky, kc = (k >> 2) & 1, (k >> 1) & 1, k & 1
        peers.append((1 - x if kx else x, 1 - y if ky else y, 1 - c if kc else c))
    return peers


def _gather_copies(src, land, send_sems, recv_sems, base):
    x, y, c = _mesh_pos()
    return [pltpu.make_async_remote_copy(
        src_ref=src, dst_ref=land.at[_slot((x, y, c))],
        send_sem=send_sems.at[base + k], recv_sem=recv_sems.at[base + k],
        device_id=(*chip, c), device_id_type=pl.DeviceIdType.MESH) for k, chip in enumerate(_other_chips(x, y))]


def _d2d_copies(src, land, send_sems, recv_sems, base):
    x, y, c = _mesh_pos()
    blocks = [(x, y, c)] + [(*chip, c) for chip in _other_chips(x, y)]
    return [pltpu.make_async_remote_copy(
        src_ref=src if k == 0 else land.at[_slot(b)], dst_ref=land.at[_slot(b)],
        send_sem=send_sems.at[base + k], recv_sem=recv_sems.at[base + k],
        device_id=(x, y, 1 - c), device_id_type=pl.DeviceIdType.MESH) for k, b in enumerate(blocks)]


def _broadcast_copies(src, land, send_sems, recv_sems, base):
    x, y, c = _mesh_pos()
    return [pltpu.make_async_remote_copy(
        src_ref=src, dst_ref=land.at[_slot((x, y, c))],
        send_sem=send_sems.at[base + k], recv_sem=recv_sems.at[base + k],
        device_id=peer, device_id_type=pl.DeviceIdType.MESH) for k, peer in enumerate(_all_peers(x, y, c))]


def _exchange_copies(src, land, send_sems, recv_sems, base):
    x, y, c = _mesh_pos()
    return [pltpu.make_async_remote_copy(
        src_ref=src.at[_slot(peer)], dst_ref=land.at[_slot((x, y, c))],
        send_sem=send_sems.at[base + k], recv_sem=recv_sems.at[base + k],
        device_id=peer, device_id_type=pl.DeviceIdType.MESH) for k, peer in enumerate(_all_peers(x, y, c))]


def _split_start(name, copies, ncopy, srcs, land_shapes, after):
    n = len(srcs)

    def body(*refs):
        src_refs, land_refs = refs[:n], refs[n:2 * n]
        send_sems, recv_sems = refs[2 * n + 1], refs[2 * n + 2]
        token = refs[-1]
        for r in range(n):
            for cp in copies(src_refs[r], land_refs[r], send_sems, recv_sems, r * ncopy):
                cp.start()
        token[...] = jnp.zeros_like(token)

    lands = [s if isinstance(s, jax.Array) else pltpu.with_memory_space_constraint(lax.empty(s.shape, s.dtype), pltpu.HBM)
             for s in land_shapes]
    ins = list(srcs) + lands
    out_shape = ([pltpu.SemaphoreType.DMA((n * ncopy,)), pltpu.SemaphoreType.DMA((n * ncopy,))]
                 + [pltpu.HBM(a.shape, a.dtype) for a in lands]
                 + [_sds((SUBLANES, LANES), F32)])
    res = pl.pallas_call(
        body, name=name, out_shape=out_shape,
        in_specs=[_HBM] * (2 * n) + [_ANY], out_specs=[_SEM, _SEM] + [_HBM] * n + [BS(memory_space=pltpu.VMEM)],
        input_output_aliases={n + i: 2 + i for i in range(n)},
        compiler_params=pltpu.CompilerParams(has_side_effects=_EFFECT),
    )(*ins, after)
    return res[0], res[1], list(srcs), list(res[2:2 + n]), res[-1]


def _split_wait(name, copies, started, after):
    send_sems, recv_sems, srcs, lands, _ = started
    n = len(srcs)
    ncopy = send_sems.shape[0] // n
    after = list(after) if isinstance(after, (list, tuple)) else [after]

    def body(*refs):
        src_refs, land_refs = refs[:n], refs[n:2 * n]
        send_sems, recv_sems = refs[2 * n], refs[2 * n + 1]
        for r in range(n):
            for cp in copies(src_refs[r], land_refs[r], send_sems, recv_sems, r * ncopy):
                cp.wait_send()
                cp.wait_recv()

    res = pl.pallas_call(
        body, name=name, out_shape=[pltpu.HBM(a.shape, a.dtype) for a in lands],
        in_specs=[_HBM] * (2 * n) + [_SEM, _SEM] + [_ANY] * len(after), out_specs=[_HBM] * n,
        input_output_aliases={n + i: i for i in range(n)},
        compiler_params=pltpu.CompilerParams(has_side_effects=_EFFECT),
    )(*srcs, *lands, send_sems, recv_sems, *after)
    return list(srcs), list(res)


def _adamw(name, w, m, v, parts):
    R, C = w.shape
    S = parts.shape[0]
    tr = _tile(R, max(SUBLANES, (256 * 1024) // C), SUBLANES)

    def body(w_ref, m_ref, v_ref, p_ref, g_ref, d_ref, nm_ref, nv_ref):
        g = p_ref[0].astype(F32)
        for s in range(1, S):
            g = g + p_ref[s].astype(F32)
        m2 = ADAM_B1 * m_ref[...] + (1.0 - ADAM_B1) * g
        v2 = ADAM_B2 * v_ref[...] + (1.0 - ADAM_B2) * (g * g)
        m_hat = m2 / (1.0 - ADAM_B1 ** ADAM_STEP)
        v_hat = v2 / (1.0 - ADAM_B2 ** ADAM_STEP)
        g_ref[...] = g
        d_ref[...] = -ADAM_LR * (m_hat / (jnp.sqrt(v_hat) + ADAM_EPS) + ADAM_WD * w_ref[...])
        nm_ref[...] = m2
        nv_ref[...] = v2

    blk = BS((tr, C), lambda i: (i, 0))
    return _pc(name, body, (R // tr,), [w, m, v, parts],
               [blk, blk, blk, BS((S, tr, C), lambda i: (0, i, 0))],
               [_sds((R, C), F32)] * 4, [blk] * 4)


def _ssm_disc(lam_re, lam_im, log_step, br_t, bi_t):
    lr = jnp.minimum(lam_re, MIN_NEG_REAL)
    li = lam_im
    dt = jnp.exp(log_step)
    mag = jnp.exp(lr * dt)
    ang = li * dt
    ab_re = mag * jnp.cos(ang)
    ab_im = mag * jnp.sin(ang)
    nr = ab_re - 1.0
    ni = ab_im
    den = lr * lr + li * li
    f_re = (nr * lr + ni * li) / den
    f_im = (ni * lr - nr * li) / den
    bb_re = f_re[None] * br_t - f_im[None] * bi_t
    bb_im = f_re[None] * bi_t + f_im[None] * br_t
    return ab_re, ab_im, bb_re, bb_im


def _ssm_param_fwd(lam_re, lam_im, log_step, br_t, bi_t):
    G, P = lam_re.shape

    def body(lr_ref, li_ref, ls_ref, br_ref, bi_ref, pw_re_ref, pw_im_ref, pwf_re_ref, pwf_im_ref, bbr_ref, bbi_ref):
        ab_re, ab_im, bb_re, bb_im = _ssm_disc(lr_ref[...], li_ref[...], ls_ref[...], br_ref[...], bi_ref[...])
        bbr_ref[...] = bb_re
        bbi_ref[...] = bb_im
        pr, pi = ab_re, ab_im
        for r in range(SUBLANES):
            pw_re_ref[r] = pr
            pw_im_ref[r] = pi
            pwf_re_ref[SUBLANES - 1 - r] = pr
            pwf_im_ref[SUBLANES - 1 - r] = pi
            pr, pi = pr * ab_re - pi * ab_im, pr * ab_im + pi * ab_re

    full = lambda a: BS(a.shape, lambda i: (0,) * a.ndim)
    ins = [lam_re, lam_im, log_step, br_t, bi_t]
    outs = [_sds((SUBLANES, G, P), F32)] * 4 + [_sds(br_t.shape, F32)] * 2
    return _pc("ssm_param_fwd", body, (1,), ins, [full(a) for a in ins], outs, [full(o) for o in outs])


def _ssm_param_bwd(lam_re, lam_im, log_step, br_t, bi_t, d_ab_re, d_ab_im, d_bbr, d_bbi):
    def body(lr_ref, li_ref, ls_ref, br_ref, bi_ref, dar_ref, dai_ref, dbr_ref, dbi_ref,
             o_lr, o_li, o_ls, o_br, o_bi):
        prim = (lr_ref[...], li_ref[...], ls_ref[...], br_ref[...], bi_ref[...])
        _, vjp = jax.vjp(_ssm_disc, *prim)
        dar = dar_ref[0]
        dai = dai_ref[0]
        for k in range(1, dar_ref.shape[0]):
            dar = dar + dar_ref[k]
            dai = dai + dai_ref[k]
        g = vjp((dar, dai, dbr_ref[...], dbi_ref[...]))
        for r, v in zip((o_lr, o_li, o_ls, o_br, o_bi), g):
            r[...] = v

    full = lambda a: BS(a.shape, lambda i: (0,) * a.ndim)
    ins = [lam_re, lam_im, log_step, br_t, bi_t, d_ab_re, d_ab_im, d_bbr, d_bbi]
    outs = [_sds(a.shape, F32) for a in (lam_re, lam_im, log_step, br_t, bi_t)]
    return _pc("ssm_param_bwd", body, (1,), ins, [full(a) for a in ins], outs, [full(o) for o in outs])


def _bcast_row(ref, r, w):
    return jnp.broadcast_to(ref[pl.ds(r, 1), :], (SUBLANES, w))


def _pick_row(x, row, r):
    return jnp.broadcast_to(jnp.sum(jnp.where(row == r, x, 0.0), axis=0, keepdims=True), x.shape)


def _scan_fwd(bu_re, bu_im, pw_re, pw_im, nseq, L):
    N, SL = bu_re.shape
    W = _tile(SL, 256)
    unroll = math.gcd(L // SUBLANES, SCAN_UNROLL)

    def body(bre_ref, bim_ref, pre_ref, pim_ref, sre_ref, sim_ref):
        pre, pim = pre_ref[...], pim_ref[...]
        steps = [(k, _bcast_row(pre_ref, k - 1, W), _bcast_row(pim_ref, k - 1, W)) for k in (1, 2, 4)]
        row = lax.broadcasted_iota(jnp.int32, (SUBLANES, W), 0)

        def step(i, carry):
            cr, ci = carry
            r0 = pl.multiple_of(i * SUBLANES, SUBLANES)
            xr = bre_ref[pl.ds(r0, SUBLANES), :]
            xi = bim_ref[pl.ds(r0, SUBLANES), :]
            for k, ar, ai in steps:
                sr = pltpu.roll(xr, k, axis=0)
                si = pltpu.roll(xi, k, axis=0)
                keep = row >= k
                xr, xi = (xr + jnp.where(keep, ar * sr - ai * si, 0.0),
                          xi + jnp.where(keep, ar * si + ai * sr, 0.0))
            xr, xi = xr + (pre * cr - pim * ci), xi + (pre * ci + pim * cr)
            sre_ref[pl.ds(r0, SUBLANES), :] = xr
            sim_ref[pl.ds(r0, SUBLANES), :] = xi
            return _pick_row(xr, row, SUBLANES - 1), _pick_row(xi, row, SUBLANES - 1)

        def group(g, carry):
            for u in range(unroll):
                carry = step(g * unroll + u, carry)
            return carry

        zero = jnp.zeros((SUBLANES, W), F32)
        lax.fori_loop(0, L // SUBLANES // unroll, group, (zero, zero))

    blk = BS((L, W), lambda s, j: (s, j))
    pw = BS((SUBLANES, W), lambda s, j: (0, j))
    return _pc("ssm_scan_fwd", body, (nseq, SL // W), [bu_re, bu_im, pw_re, pw_im], [blk, blk, pw, pw],
               [_sds((N, SL), F32)] * 2, [blk, blk])


def _scan_bwd(ds_re, ds_im, s_re, s_im, pw_re, pw_im, pwf_re, pwf_im, nseq, L):
    N, SL = ds_re.shape
    W = _tile(SL, 256)
    nt = L // SUBLANES
    unroll = math.gcd(nt, SCAN_UNROLL)

    def body(dsr_ref, dsi_ref, sre_ref, sim_ref, pre_ref, pim_ref, fre_ref, fim_ref,
             lre_ref, lim_ref, dar_ref, dai_ref):
        fre, fim = fre_ref[...], -fim_ref[...]
        steps = [(k, _bcast_row(pre_ref, k - 1, W), -_bcast_row(pim_ref, k - 1, W)) for k in (1, 2, 4)]
        row = lax.broadcasted_iota(jnp.int32, (SUBLANES, W), 0)

        def step(ii, carry):
            cr, ci, acr, aci = carry
            i = nt - 1 - ii
            r0 = pl.multiple_of(i * SUBLANES, SUBLANES)
            xr = dsr_ref[pl.ds(r0, SUBLANES), :]
            xi = dsi_ref[pl.ds(r0, SUBLANES), :]
            for k, ar, ai in steps:
                sr = pltpu.roll(xr, SUBLANES - k, axis=0)
                si = pltpu.roll(xi, SUBLANES - k, axis=0)
                keep = row < SUBLANES - k
                xr, xi = (xr + jnp.where(keep, ar * sr - ai * si, 0.0),
                          xi + jnp.where(keep, ar * si + ai * sr, 0.0))
            xr, xi = xr + (fre * cr - fim * ci), xi + (fre * ci + fim * cr)
            lre_ref[pl.ds(r0, SUBLANES), :] = xr
            lim_ref[pl.ds(r0, SUBLANES), :] = xi
            p0 = pl.multiple_of(jnp.maximum(i - 1, 0) * SUBLANES, SUBLANES)
            has_prev = i > 0
            spr = jnp.where(row == 0,
                            jnp.where(has_prev, pltpu.roll(sre_ref[pl.ds(p0, SUBLANES), :], 1, axis=0), 0.0),
                            pltpu.roll(sre_ref[pl.ds(r0, SUBLANES), :], 1, axis=0))
            spi = jnp.where(row == 0,
                            jnp.where(has_prev, pltpu.roll(sim_ref[pl.ds(p0, SUBLANES), :], 1, axis=0), 0.0),
                            pltpu.roll(sim_ref[pl.ds(r0, SUBLANES), :], 1, axis=0))
            acr = acr + (xr * spr + xi * spi)
            aci = aci + (xi * spr - xr * spi)
            return _pick_row(xr, row, 0), _pick_row(xi, row, 0), acr, aci

        def group(g, carry):
            for u in range(unroll):
                carry = step(g * unroll + u, carry)
            return carry

        zero = jnp.zeros((SUBLANES, W), F32)
        _, _, acr, aci = lax.fori_loop(0, nt // unroll, group, (zero, zero, zero, zero))
        dar_ref[...] = acr
        dai_ref[...] = aci

    blk = BS((L, W), lambda s, j: (s, j))
    pw = BS((SUBLANES, W), lambda s, j: (0, j))
    da = BS((None, SUBLANES, W), lambda s, j: (s, 0, j))
    return _pc("ssm_scan_bwd", body, (nseq, SL // W),
               [ds_re, ds_im, s_re, s_im, pw_re, pw_im, pwf_re, pwf_im], [blk] * 4 + [pw] * 4,
               [_sds((N, SL), F32)] * 2 + [_sds((nseq, SUBLANES, SL), F32)] * 2, [blk, blk, da, da])


def _pool_select(g, vals):
    return jnp.where(g == 0, vals[0], jnp.where(g == 1, vals[1], jnp.where(g == 2, vals[2], vals[3])))


def _pool_fwd(proj, col0, DP, nseq, L):
    N = proj.shape[0]
    PG = DP // len(POOL_WINDOWS)
    W = _tile(PG, 256)

    def body(v_ref, z_ref):
        g = pl.program_id(1) // (PG // W)
        v = v_ref[...]
        row = lax.broadcasted_iota(jnp.int32, (L, W), 0)
        sums, s, k = [], v, 1
        for _ in POOL_WINDOWS:
            s = s + jnp.where(row >= k, pltpu.roll(s, k, axis=0), 0.0)
            sums.append(s)
            k *= 2
        win = _pool_select(g, [float(w) for w in POOL_WINDOWS])
        cnt = jnp.minimum((row + 1).astype(F32), win)
        z_ref[...] = (_pool_select(g, sums) / cnt - v).astype(z_ref.dtype)

    return _pc("pool_fwd", body, (nseq, DP // W), [proj], [BS((L, W), lambda s, j: (s, col0 // W + j))],
               _sds((N, DP), BF16), BS((L, W), lambda s, j: (s, j)))


def _pool_bwd(dz, nseq, L):
    N, DP = dz.shape
    PG = DP // len(POOL_WINDOWS)
    W = _tile(PG, 256)

    def body(dz_ref, dv_ref):
        g = pl.program_id(1) // (PG // W)
        d = dz_ref[...]
        row = lax.broadcasted_iota(jnp.int32, (L, W), 0)
        win = _pool_select(g, [float(w) for w in POOL_WINDOWS])
        s = d / jnp.minimum((row + 1).astype(F32), win)
        sums, k = [], 1
        for _ in POOL_WINDOWS:
            s = s + jnp.where(row < L - k, pltpu.roll(s, L - k, axis=0), 0.0)
            sums.append(s)
            k *= 2
        dv_ref[...] = (_pool_select(g, sums) - d).astype(dv_ref.dtype)

    blk = BS((L, W), lambda s, j: (s, j))
    return _pc("pool_bwd", body, (nseq, DP // W), [dz], [blk], _sds((N, DP), BF16), blk)


def _rstd(x):
    return lax.rsqrt(jnp.mean(x * x, axis=-1, keepdims=True) + EPS)


def _norm_bwd(dy, xhat, rstd, gain):
    t = dy * gain
    return rstd * (t - xhat * jnp.mean(t * xhat, axis=-1, keepdims=True))


def _pre_norm(x, g1):
    N, D = x.shape
    tr = _tile(N, 128, SUBLANES)

    def body(x_ref, g_ref, a_ref):
        xv = x_ref[...]
        a_ref[...] = (xv * _rstd(xv) * g_ref[...]).astype(a_ref.dtype)

    row = BS((tr, D), lambda i: (i, 0))
    vec = BS((1, D), lambda i: (0, 0))
    return _pc("pre_norm", body, (N // tr,), [x, g1], [row, vec], _sds((N, D), BF16), row)


def _mid_norm(x, o, g2, g3):
    N, D = x.shape
    tr = _tile(N, 128, SUBLANES)

    def body(x_ref, o_ref, g2_ref, g3_ref, h1_ref, c_ref):
        ov = o_ref[...]
        h1 = x_ref[...] + ov * _rstd(ov) * g2_ref[...]
        h1_ref[...] = h1
        c_ref[...] = (h1 * _rstd(h1) * g3_ref[...]).astype(c_ref.dtype)

    row = BS((tr, D), lambda i: (i, 0))
    vec = BS((1, D), lambda i: (0, 0))
    return _pc("mid_norm", body, (N // tr,), [x, o, g2, g3], [row, row, vec, vec],
               [_sds((N, D), F32), _sds((N, D), BF16)], [row, row])


def _post_ffn(h1, dn, tgt, g4):
    N, D = h1.shape
    tr = _tile(N, 128, SUBLANES)

    def body(h1_ref, dn_ref, t_ref, g_ref, dh2_ref, ddn_ref, lossv_ref, dg_ref):
        first = pl.program_id(0) == 0
        dnv = dn_ref[...]
        rstd = _rstd(dnv)
        xhat = dnv * rstd
        gain = g_ref[...]
        err = (h1_ref[...] + xhat * gain) - t_ref[...]
        dh2 = err / float(D)
        dh2_ref[...] = dh2
        ddn_ref[...] = _norm_bwd(dh2, xhat, rstd, gain).astype(ddn_ref.dtype)
        _rowsum_into(lossv_ref, first, err * err)
        _rowsum_into(dg_ref, first, dh2 * xhat)

    row = BS((tr, D), lambda i: (i, 0))
    vec = BS((1, D), lambda i: (0, 0))
    return _pc("post_ffn", body, (N // tr,), [h1, dn, tgt, g4], [row, row, row, vec],
               [_sds((N, D), F32), _sds((N, D), BF16), _sds((1, D), F32), _sds((1, D), F32)], [row, row, vec, vec])


def _mid_bwd(dh2, dc, h1, o, g2, g3, deps=()):
    N, D = h1.shape
    tr = _tile(N, 128, SUBLANES)

    def body(dh2_ref, dc_ref, h1_ref, o_ref, g2_ref, g3_ref, dh1_ref, do_ref, dg2_ref, dg3_ref):
        first = pl.program_id(0) == 0
        h1 = h1_ref[...]
        r3 = _rstd(h1)
        hc = h1 * r3
        dcv = dc_ref[...]
        dh1 = dh2_ref[...] + _norm_bwd(dcv, hc, r3, g3_ref[...])
        dh1_ref[...] = dh1
        ov = o_ref[...]
        r2 = _rstd(ov)
        ho = ov * r2
        do_ref[...] = _norm_bwd(dh1, ho, r2, g2_ref[...]).astype(do_ref.dtype)
        _rowsum_into(dg3_ref, first, dcv * hc)
        _rowsum_into(dg2_ref, first, dh1 * ho)

    row = BS((tr, D), lambda i: (i, 0))
    vec = BS((1, D), lambda i: (0, 0))
    return _pc("mid_bwd", body, (N // tr,), [dh2, dc, h1, o, g2, g3], [row] * 4 + [vec, vec],
               [_sds((N, D), F32), _sds((N, D), BF16), _sds((1, D), F32), _sds((1, D), F32)], [row, row, vec, vec],
               deps=deps)


def _pre_bwd(x, da, dh1, g1):
    N, D = x.shape
    tr = _tile(N, 128, SUBLANES)

    def body(x_ref, da_ref, dh1_ref, g_ref, dx_ref, dg_ref):
        first = pl.program_id(0) == 0
        xv = x_ref[...]
        r1 = _rstd(xv)
        xh = xv * r1
        dav = da_ref[...]
        dx_ref[...] = dh1_ref[...] + _norm_bwd(dav, xh, r1, g_ref[...])
        _rowsum_into(dg_ref, first, dav * xh)

    row = BS((tr, D), lambda i: (i, 0))
    vec = BS((1, D), lambda i: (0, 0))
    return _pc("pre_bwd", body, (N // tr,), [x, da, dh1, g1], [row, row, row, vec],
               [_sds((N, D), F32), _sds((1, D), F32)], [row, vec])


def _conv_rows(x_ref, halo_ref, first):
    x = x_ref[...]
    xx = jnp.concatenate([jnp.where(first, 0.0, halo_ref[...]), x], axis=0)
    x1 = pltpu.roll(xx, 1, axis=0)[SUBLANES:]
    x2 = pltpu.roll(xx, 2, axis=0)[SUBLANES:]
    return x, x1, x2


def _conv_apply(rows, w_ref, b_ref):
    x, x1, x2 = rows
    return ((b_ref[...] + x2 * w_ref[pl.ds(0, 1), :]) + x1 * w_ref[pl.ds(1, 1), :]) + x * w_ref[pl.ds(2, 1), :]


def _gate_specs(N, FC, TR, half):
    tile = BS((None, TR, FC), lambda jj, i: (jj + half, i, 0))
    halo = BS((None, SUBLANES, FC), lambda jj, i: (jj + half, jnp.maximum(i * (TR // SUBLANES) - 1, 0), 0))
    cw = BS((None, 3, FC), lambda jj, i: (jj + half, 0, 0))
    cb = BS((None, 1, FC), lambda jj, i: (jj + half, 0, 0))
    return tile, halo, cw, cb


def _gate_fwd(up_pre, cw, cb, L, deps=()):
    nb, N, FC = up_pre.shape
    half = nb // 2
    TR = _tile(L, 128, SUBLANES)

    def body(xa_ref, ha_ref, wa_ref, ba_ref, xb_ref, hb_ref, wb_ref, bb_ref, f_ref):
        first = (pl.program_id(1) % (L // TR)) == 0
        ua = _conv_apply(_conv_rows(xa_ref, ha_ref, first), wa_ref, ba_ref)
        ub = _conv_apply(_conv_rows(xb_ref, hb_ref, first), wb_ref, bb_ref)
        f_ref[...] = (_gelu(ua) * ub).astype(f_ref.dtype)

    sa, sb = _gate_specs(N, FC, TR, 0), _gate_specs(N, FC, TR, half)
    return _pc("gate_fwd", body, (half, N // TR), [up_pre, up_pre, cw, cb] * 2, list(sa) + list(sb),
               _sds((half, N, FC), BF16), BS((None, TR, FC), lambda jj, i: (jj, i, 0)), deps=deps)


def _gate_bwd(up_pre, cw, cb, df, L, deps=()):
    nb, N, FC = up_pre.shape
    half = nb // 2
    TR = _tile(L, 128, SUBLANES)

    def body(xa_ref, ha_ref, wa_ref, ba_ref, xb_ref, hb_ref, wb_ref, bb_ref, df_ref, dup_ref, dw_ref, dbias_ref):
        i = pl.program_id(1)
        first_row = i == 0
        first = (i % (L // TR)) == 0
        ra = _conv_rows(xa_ref, ha_ref, first)
        rb = _conv_rows(xb_ref, hb_ref, first)
        ua = _conv_apply(ra, wa_ref, ba_ref)
        ub = _conv_apply(rb, wb_ref, bb_ref)
        dfv = df_ref[...].astype(F32)
        dua = dfv * ub * _gelu_grad(ua)
        dub = dfv * _gelu(ua)
        dup_ref[0] = dua.astype(dup_ref.dtype)
        dup_ref[1] = dub.astype(dup_ref.dtype)
        for h, (rows, du) in enumerate(((ra, dua), (rb, dub))):
            x, x1, x2 = rows
            _rowsum_into(dbias_ref.at[h], first_row, du)
            for k, xs in enumerate((x2, x1, x)):
                _rowsum_into(dw_ref.at[h, pl.ds(k, 1), :], first_row, du * xs)

    sa, sb = _gate_specs(N, FC, TR, 0), _gate_specs(N, FC, TR, half)
    tile = BS((None, TR, FC), lambda jj, i: (jj, i, 0))
    both = BS((2, None, TR, FC), lambda jj, i: (0, jj, i, 0))
    dw = BS((2, None, 3, FC), lambda jj, i: (0, jj, 0, 0))
    dbias = BS((2, None, 1, FC), lambda jj, i: (0, jj, 0, 0))
    return _pc("gate_bwd", body, (half, N // TR), [up_pre, up_pre, cw, cb] * 2 + [df], list(sa) + list(sb) + [tile],
               [_sds((2, half, N, FC), BF16), _sds((2, half, 3, FC), F32), _sds((2, half, 1, FC), F32)],
               [both, dw, dbias], deps=deps)


def _conv_bwd(dup, cw, L):
    nb, N, FC = dup.shape
    TR = _tile(L, 128, 2 * SUBLANES)
    HR = 2 * SUBLANES
    nrb = N // HR

    def body(x_ref, h_ref, w_ref, o_ref):
        last = ((pl.program_id(1) + 1) % (L // TR)) == 0
        x = x_ref[...].astype(F32)
        xx = jnp.concatenate([x, jnp.where(last, 0.0, h_ref[...].astype(F32))], axis=0)
        x1 = pltpu.roll(xx, TR + HR - 1, axis=0)[:TR]
        x2 = pltpu.roll(xx, TR + HR - 2, axis=0)[:TR]
        o_ref[...] = (x * w_ref[pl.ds(2, 1), :] + x1 * w_ref[pl.ds(1, 1), :] + x2 * w_ref[pl.ds(0, 1), :]
                      ).astype(o_ref.dtype)

    tile = BS((None, TR, FC), lambda jj, i: (jj, i, 0))
    halo = BS((None, HR, FC), lambda jj, i: (jj, jnp.minimum((i + 1) * (TR // HR), nrb - 1), 0))
    w = BS((None, 3, FC), lambda jj, i: (jj, 0, 0))
    return _pc("conv_bwd", body, (nb, N // TR), [dup, dup, cw], [tile, halo, w], _sds((nb, N, FC), BF16), tile)


def _pack(arrs):
    parts = []
    for a in arrs:
        flat = a.reshape(-1).astype(F32)
        pad = (-flat.shape[0]) % (SUBLANES * LANES)
        parts.append(jnp.pad(flat, (0, pad)))
    return jnp.concatenate(parts).reshape(-1, LANES)


def _unpack(packed, shapes):
    flat = packed.reshape(-1)
    out, off = [], 0
    for s in shapes:
        n = math.prod(s)
        out.append(flat[off:off + n].reshape(s))
        off += n + ((-n) % (SUBLANES * LANES))
    return out


def _small_sum(gathered, gathered_bf, loss_rows, d_model):
    S, R, C = gathered.shape
    R2 = gathered_bf.shape[1]

    def body(p_ref, q_ref, tot_ref, tot2_ref, loss_ref):
        t = p_ref[0]
        u = q_ref[0].astype(F32)
        for s in range(1, S):
            t = t + p_ref[s]
            u = u + q_ref[s].astype(F32)
        tot_ref[...] = t
        tot2_ref[...] = u
        loss_ref[...] = jnp.full((1, 1), 0.5 / d_model, F32) * jnp.sum(t[:loss_rows])

    return _pc("small_sum", body, (1,), [gathered, gathered_bf],
               [BS((S, R, C), lambda i: (0, 0, 0)), BS((S, R2, C), lambda i: (0, 0, 0))],
               [_sds((R, C), F32), _sds((R2, C), F32), _sds((1, 1), F32)],
               [BS((R, C), lambda i: (0, 0)), BS((R2, C), lambda i: (0, 0)), BS((1, 1), lambda i: (0, 0))])


def _block_diag_in(bb_t, nch):
    J, G, P = bb_t.shape
    gl = G // nch
    b = bb_t.reshape(J, nch, gl, P).transpose(1, 0, 2, 3)
    eye = jnp.eye(gl, dtype=F32)
    w = eye[None, :, None, :, None] * b[:, None, :, :, :]
    return w.reshape(nch, gl * J, gl * P)


def _block_diag_in_grad(dw, J, G, P):
    nch = dw.shape[0]
    gl = G // nch
    d = dw.reshape(nch, gl, J, gl, P)
    d = jnp.einsum("cgjgp->jcgp", d)
    return d.reshape(J, G, P)


def _block_diag_out(c, nch):
    G, J, P = c.shape
    gl = G // nch
    cc = c.reshape(nch, gl, J, P).transpose(0, 1, 3, 2)
    eye = jnp.eye(gl, dtype=F32)
    w = cc[:, :, :, None, :] * eye[None, :, None, :, None]
    return w.reshape(nch, gl * P, gl * J)


def _block_diag_out_grad(dw, G, J, P):
    nch = dw.shape[0]
    gl = G // nch
    d = dw.reshape(nch, gl, P, gl, J)
    d = jnp.einsum("cgpgj->cgjp", d)
    return d.reshape(G, J, P)


def kernel(x, norm_pre_mix, w_in, ssm_lambda_re, ssm_lambda_im, ssm_log_step, ssm_b_re, ssm_b_im, ssm_c_re, ssm_c_im, ssm_d, ssm_glu_w, ssm_glu_b, pool_w, pool_b, pool_scale, w_branch_ssm, w_branch_pool, w_out, norm_post_mix, norm_pre_ffn, w_up, ffn_conv_w, ffn_conv_b, w_down, norm_post_ffn, loss_target, m_norm_pre_mix, m_w_in, m_ssm_lambda_re, m_ssm_lambda_im, m_ssm_log_step, m_ssm_b_re, m_ssm_b_im, m_ssm_c_re, m_ssm_c_im, m_ssm_d, m_ssm_glu_w, m_ssm_glu_b, m_pool_w, m_pool_b, m_pool_scale, m_w_branch_ssm, m_w_branch_pool, m_w_out, m_norm_post_mix, m_norm_pre_ffn, m_w_up, m_ffn_conv_w, m_ffn_conv_b, m_w_down, m_norm_post_ffn, v_norm_pre_mix, v_w_in, v_ssm_lambda_re, v_ssm_lambda_im, v_ssm_log_step, v_ssm_b_re, v_ssm_b_im, v_ssm_c_re, v_ssm_c_im, v_ssm_d, v_ssm_glu_w, v_ssm_glu_b, v_pool_w, v_pool_b, v_pool_scale, v_w_branch_ssm, v_w_branch_pool, v_w_out, v_norm_post_mix, v_norm_pre_ffn, v_w_up, v_ffn_conv_w, v_ffn_conv_b, v_w_down, v_norm_post_ffn):
    args = dict(locals())
    names = ["norm_pre_mix", "w_in", "ssm_lambda_re", "ssm_lambda_im", "ssm_log_step", "ssm_b_re", "ssm_b_im",
             "ssm_c_re", "ssm_c_im", "ssm_d", "ssm_glu_w", "ssm_glu_b", "pool_w", "pool_b", "pool_scale",
             "w_branch_ssm", "w_branch_pool", "w_out", "norm_post_mix", "norm_pre_ffn", "w_up", "ffn_conv_w",
             "ffn_conv_b", "w_down", "norm_post_ffn"]

    nseq, L, D = x.shape
    N = nseq * L
    U = D // NDEV
    DS = ssm_d.shape[1]
    DP = pool_scale.shape[1]
    G, P, J = ssm_b_re.shape[1:]
    SL = G * P
    CH = GROUPS_PER_CHUNK * J
    CS = GROUPS_PER_CHUNK * P
    NCH = DS // CH
    NPG = len(POOL_WINDOWS)
    PG = DP // NPG
    FC = w_up.shape[2]
    NB = NDEV
    HB = NB // 2
    F2 = NB * FC
    dev = _slot(_mesh_pos())
    tm = _tile(N, 1024)
    tm2 = _tile(N, 512)

    x2 = x.reshape(N, D)
    tgt = loss_target.reshape(N, D)

    def bf(t):
        return t.astype(BF16)

    def g_start(tag, group, after):
        return _split_start("gather_start_" + tag, _gather_copies, 3, group,
                            [_sds((NDEV,) + s.shape, s.dtype) for s in group], after)

    def g_land(tag, started, after):
        srcs, lands = _split_wait("gather_wait_" + tag, _gather_copies, started, after)
        return _split_start("d2d_start_" + tag, _d2d_copies, 4, srcs, lands, srcs[0])

    def g_finish(tag, d2d, after):
        srcs, lands = _split_wait("d2d_wait_" + tag, _d2d_copies, d2d, after)
        return [lax.dynamic_update_index_in_dim(l, s, dev, 0) for l, s in zip(lands, srcs)]

    def x_start(tag, group):
        return _split_start("exchange_start_" + tag, _exchange_copies, NDEV - 1, group,
                            [_sds(g.shape, g.dtype) for g in group], group[0])

    def x_finish(tag, started, after):
        srcs, lands = _split_wait("exchange_wait_" + tag, _exchange_copies, started, after)
        own = [lax.dynamic_index_in_dim(s, dev, 0, keepdims=False) for s in srcs]
        return [lax.dynamic_update_index_in_dim(l, o, dev, 0) for l, o in zip(lands, own)]

    st_in = g_start("in", [bf(w_in[0])], x2)
    st_mix = g_start("mix", [bf(ssm_glu_w[0]), bf(pool_w[0]), pool_b[0], ffn_conv_w[0]], st_in[4])
    conv_b_blk = ffn_conv_b.reshape(NB, 1, FC)

    (_, x2e, w_branch_ssm, w_branch_pool, w_out, w_up, w_down, ssm_lambda_re, ssm_lambda_im, ssm_log_step,
     ssm_b_re, ssm_b_im, ssm_c_re, ssm_c_im) = lax.optimization_barrier(
        (st_mix[4], x2, w_branch_ssm, w_branch_pool, w_out, w_up, w_down, ssm_lambda_re, ssm_lambda_im, ssm_log_step,
         ssm_b_re, ssm_b_im, ssm_c_re, ssm_c_im))
    lam_re, lam_im = ssm_lambda_re[0], ssm_lambda_im[0]
    log_step = ssm_log_step.reshape(G, 1)
    br_t = ssm_b_re[0].transpose(2, 0, 1)
    bi_t = ssm_b_im[0].transpose(2, 0, 1)
    pw_re3, pw_im3, pwf_re3, pwf_im3, bb_re, bb_im = _ssm_param_fwd(lam_re, lam_im, log_step, br_t, bi_t)
    pw_re, pw_im = pw_re3.reshape(SUBLANES, SL), pw_im3.reshape(SUBLANES, SL)
    pwf_re, pwf_im = pwf_re3.reshape(SUBLANES, SL), pwf_im3.reshape(SUBLANES, SL)
    WB = jnp.concatenate([_block_diag_in(bb_re, NCH), _block_diag_in(bb_im, NCH)], axis=2).astype(BF16)
    WCre = _block_diag_out(ssm_c_re[0], NCH).astype(BF16)
    WCim = _block_diag_out(-ssm_c_im[0], NCH).astype(BF16)
    a = _pre_norm(x2e, norm_pre_mix)
    small_names = ["norm_pre_mix", "norm_post_mix", "norm_pre_ffn", "norm_post_ffn", "ssm_lambda_re", "ssm_lambda_im",
                   "ssm_log_step", "ssm_b_re", "ssm_b_im", "ssm_c_re", "ssm_c_im", "ssm_d", "ssm_glu_b", "pool_scale",
                   "pool_b", "ffn_conv_w", "ffn_conv_b"]
    _, small_in = lax.optimization_barrier(
        (st_mix[4], [[args[p + n] for n in small_names] for p in ("", "m_", "v_")]))
    sm_w, sm_m, sm_v = (_pack(group) for group in small_in)
    g_br = [bf(w_branch_ssm[0]), bf(w_branch_pool[0]), bf(w_out[0])]
    g_up, g_down = [bf(w_up[0].T)], [bf(w_down[0])]
    early = [WB, WCre, WCim, pwf_re, pwf_im, a, sm_w, sm_m, sm_v] + g_br + g_up + g_down

    d_in = g_land("in", st_in, [st_mix[4]] + early)
    st_br = g_start("branch", g_br, d_in[4])
    st_up = g_start("up", g_up, st_br[4])
    st_down = g_start("down", g_down, st_up[4])
    (Win,) = g_finish("in", d_in, st_down[4])

    nq = 3 * NDEV
    (proj,) = _fused_matmul(
        "in_proj", (N // tm, nq, 1),
        [(a, BS((tm, D), lambda i, q, k: (i, 0)), Win, BS((None, D, U), lambda i, q, k: (q // 3, 0, q % 3)), "nn", 0)],
        [(tm, U)], [], [(_sds((N, 3 * D), F32), BS((tm, U), lambda i, q, k: (i, q)))],
        _store(lambda accs: accs))
    d_mix = g_land("mix", st_mix, proj)
    d_br = g_land("branch", st_br, d_mix[4])

    bu_re, bu_im = _fused_matmul(
        "ssm_in", (N // tm2, NCH, 1),
        [(proj, BS((tm2, CH), lambda i, c, k: (i, c)), WB, BS((None, CH, 2 * CS), lambda i, c, k: (c, 0, 0)), "nn", 0)],
        [(tm2, 2 * CS)], [],
        [(_sds((N, SL), F32), BS((tm2, CS), lambda i, c, k: (i, c)))] * 2,
        _store(lambda accs: (accs[0][:, :CS], accs[0][:, CS:])), deps=[d_br[4]])
    s_re, s_im = _scan_fwd(bu_re, bu_im, pw_re, pw_im, nseq, L)

    def ssm_out_epi(ids, accs, ex, o):
        u_ref, d_ref = ex
        y0 = accs[0] + d_ref[...] * u_ref[...]
        o[0][...] = y0
        o[1][...] = _gelu(y0).astype(BF16)

    y0, y1 = _fused_matmul(
        "ssm_out", (N // tm2, NCH, 1),
        [(s_re, BS((tm2, CS), lambda i, c, k: (i, c)), WCre, BS((None, CS, CH), lambda i, c, k: (c, 0, 0)), "nn", 0),
         (s_im, BS((tm2, CS), lambda i, c, k: (i, c)), WCim, BS((None, CS, CH), lambda i, c, k: (c, 0, 0)), "nn", 0)],
        [(tm2, CH)],
        [(proj, BS((tm2, CH), lambda i, c, k: (i, c))), (ssm_d, BS((1, CH), lambda i, c, k: (0, c)))],
        [(_sds((N, DS), F32), BS((tm2, CH), lambda i, c, k: (i, c))),
         (_sds((N, DS), BF16), BS((tm2, CH), lambda i, c, k: (i, c)))],
        ssm_out_epi)

    Wglu, Wpool, pool_b_all, conv_w_all = g_finish("mix", d_mix, y1)
    Wglu = Wglu.reshape(DS, DS)
    Wpool = Wpool.transpose(1, 0, 2, 3).reshape(NPG, PG, PG)
    pool_b_full = pool_b_all.transpose(1, 0, 2).reshape(1, DP)
    tn_s = _tile(DS, 512)

    def glu_epi(ids, accs, ex, o):
        y0_ref, b_ref = ex
        zg = accs[0] + b_ref[...]
        o[0][...] = zg
        o[1][...] = (_gelu(y0_ref[...]) * _sigmoid(zg)).astype(BF16)

    zg, ys = _fused_matmul(
        "ssm_glu", (N // tm, DS // tn_s, 1),
        [(y1, BS((tm, DS), lambda i, j, k: (i, 0)), Wglu, BS((DS, tn_s), lambda i, j, k: (0, j)), "nn", 0)],
        [(tm, tn_s)],
        [(y0, BS((tm, tn_s), lambda i, j, k: (i, j))), (ssm_glu_b, BS((1, tn_s), lambda i, j, k: (0, j)))],
        [(_sds((N, DS), F32), BS((tm, tn_s), lambda i, j, k: (i, j))),
         (_sds((N, DS), BF16), BS((tm, tn_s), lambda i, j, k: (i, j)))],
        glu_epi)

    z = _pool_fwd(proj, DS, DP, nseq, L)

    def pool_mm_epi(ids, accs, ex, o):
        b_ref, sc_ref = ex
        q = accs[0] + b_ref[...]
        o[0][...] = q
        o[1][...] = (q * sc_ref[...]).astype(BF16)

    qp, yp = _fused_matmul(
        "pool_mm", (N // tm, NPG, 1),
        [(z, BS((tm, PG), lambda i, g, k: (i, g)), Wpool, BS((None, PG, PG), lambda i, g, k: (g, 0, 0)), "nn", 0)],
        [(tm, PG)],
        [(pool_b_full, BS((1, PG), lambda i, g, k: (0, g))), (pool_scale, BS((1, PG), lambda i, g, k: (0, g)))],
        [(_sds((N, DP), F32), BS((tm, PG), lambda i, g, k: (i, g))),
         (_sds((N, DP), BF16), BS((tm, PG), lambda i, g, k: (i, g)))],
        pool_mm_epi)

    Wbs, Wbp, Wout = g_finish("branch", d_br, yp)
    Wout = Wout.reshape(D, D)
    gs_blk = BS((tm2, U), lambda i, q, k: (i, (DS + DP) // U + q))
    gp_blk = BS((tm2, U), lambda i, q, k: (i, (DS + DP + D) // U + q))
    out_blk = BS((tm2, U), lambda i, q, k: (i, q))

    def branch_epi(ids, accs, ex, o):
        gs_ref, gp_ref = ex
        o[0][...] = accs[0].astype(BF16)
        o[1][...] = accs[1].astype(BF16)
        o[2][...] = (_sigmoid(gs_ref[...]) * accs[0] + _sigmoid(gp_ref[...]) * accs[1]).astype(BF16)

    Ys, Yp, merged = _fused_matmul(
        "branch", (N // tm2, NDEV, 1),
        [(ys, BS((tm2, DS), lambda i, q, k: (i, 0)), Wbs, BS((None, DS, U), lambda i, q, k: (q, 0, 0)), "nn", 0),
         (yp, BS((tm2, DP), lambda i, q, k: (i, 0)), Wbp, BS((None, DP, U), lambda i, q, k: (q, 0, 0)), "nn", 1)],
        [(tm2, U), (tm2, U)],
        [(proj, gs_blk), (proj, gp_blk)],
        [(_sds((N, D), BF16), out_blk), (_sds((N, D), BF16), out_blk), (_sds((N, D), BF16), out_blk)],
        branch_epi)
    d_up = g_land("up", st_up, merged)

    tn_d = _tile(D, 512)
    (o_mix,) = _fused_matmul(
        "out_proj", (N // tm, D // tn_d, 1),
        [(merged, BS((tm, D), lambda i, j, k: (i, 0)), Wout, BS((D, tn_d), lambda i, j, k: (0, j)), "nn", 0)],
        [(tm, tn_d)], [], [(_sds((N, D), F32), BS((tm, tn_d), lambda i, j, k: (i, j)))],
        _store(lambda accs: accs), deps=[d_up[4]])
    h1, c = _mid_norm(x2, o_mix, norm_post_mix, norm_pre_ffn)

    (Wup,) = g_finish("up", d_up, c)
    tk_d = _tile(D, 1024)
    tk_up = _tile(D, 2048)
    (up_pre,) = _fused_matmul(
        "ffn_up", (N // tm2, NB, D // tk_up),
        [(c, BS((tm2, tk_up), lambda i, j, k: (i, k)), Wup, BS((None, FC, tk_up), lambda i, j, k: (j, 0, k)), "nt", 0)],
        [(tm2, FC)], [], [(_sds((NB, N, FC), F32), BS((None, tm2, FC), lambda i, j, k: (j, i, 0)))],
        _store(lambda accs: accs))
    d_down = g_land("down", st_down, up_pre)
    f = _gate_fwd(up_pre, conv_w_all, conv_b_blk, L, deps=[d_down[4]])
    (Wdown,) = g_finish("down", d_down, f)
    Wdown = Wdown.reshape(HB, FC, D)
    tn_d2 = _tile(D, 1024)
    (dn,) = _fused_matmul(
        "ffn_down", (N // tm2, D // tn_d2, HB),
        [(f, BS((None, tm2, FC), lambda i, j, k: (k, i, 0)), Wdown, BS((None, FC, tn_d2), lambda i, j, k: (k, 0, j)), "nn", 0)],
        [(tm2, tn_d2)], [], [(_sds((N, D), F32), BS((tm2, tn_d2), lambda i, j, k: (i, j)))],
        _store(lambda accs: accs))
    dh2, d_dn, lossv, dg4 = _post_ffn(h1, dn, tgt, norm_post_ffn)

    (df,) = _fused_matmul(
        "ffn_down_dx", (N // tm2, HB, D // tk_d),
        [(d_dn, BS((tm2, tk_d), lambda i, j, k: (i, k)), Wdown, BS((None, FC, tk_d), lambda i, j, k: (j, 0, k)), "nt", 0)],
        [(tm2, FC)], [], [(_sds((HB, N, FC), BF16), BS((None, tm2, FC), lambda i, j, k: (j, i, 0)))],
        _store(lambda accs: accs))
    tk_n = _tile(N, 1024)
    (gW_down,) = _fused_matmul(
        "ffn_down_dw", (HB, D // tn_d, N // tk_n),
        [(f, BS((None, tk_n, FC), lambda j, n, k: (j, k, 0)), d_dn, BS((tk_n, tn_d), lambda j, n, k: (k, n)), "tn", 0)],
        [(FC, tn_d)], [], [(_sds((HB, FC, D), BF16), BS((None, FC, tn_d), lambda j, n, k: (j, 0, n)))],
        _store(lambda accs: accs))
    x_down = x_start("down", [gW_down.reshape(NDEV, FC // 2, D)])
    dup, dcw, dcb = _gate_bwd(up_pre, conv_w_all, conv_b_blk, df, L, deps=[x_down[4]])
    dpre = _conv_bwd(dup.reshape(NB, N, FC), conv_w_all, L)
    (dc,) = _fused_matmul(
        "ffn_up_dx", (N // tm2, D // tn_d2, NB),
        [(dpre, BS((None, tm2, FC), lambda i, j, k: (k, i, 0)), Wup, BS((None, FC, tn_d2), lambda i, j, k: (k, 0, j)), "nn", 0)],
        [(tm2, tn_d2)], [], [(_sds((N, D), F32), BS((tm2, tn_d2), lambda i, j, k: (i, j)))],
        _store(lambda accs: accs))
    tm_d = _tile(D, 512)
    (gW_up,) = _fused_matmul(
        "ffn_up_dw", (NB, D // tm_d, N // tk_n),
        [(dpre, BS((None, tk_n, FC), lambda j, n, k: (j, k, 0)), c, BS((tk_n, tm_d), lambda j, n, k: (k, n)), "tn", 0)],
        [(FC, tm_d)], [], [(_sds((NB, FC, D), BF16), BS((None, FC, tm_d), lambda j, n, k: (j, 0, n)))],
        _store(lambda accs: accs))
    x_up = x_start("up", [gW_up])

    dh1, d_o, dg2, dg3 = _mid_bwd(dh2, dc, h1, o_mix, norm_post_mix, norm_pre_ffn, deps=[x_up[4]])

    def dmerged_epi(ids, accs, ex, o):
        gs_ref, gp_ref, ys_ref, yp_ref = ex
        dm = accs[0]
        sg_s, sg_p = _sigmoid(gs_ref[...]), _sigmoid(gp_ref[...])
        o[0][...] = (dm * sg_s).astype(BF16)
        o[1][...] = (dm * sg_p).astype(BF16)
        o[2][...] = (dm * ys_ref[...].astype(F32) * sg_s * (1.0 - sg_s)).astype(BF16)
        o[3][...] = (dm * yp_ref[...].astype(F32) * sg_p * (1.0 - sg_p)).astype(BF16)

    dYs, dYp, dgs, dgp = _fused_matmul(
        "out_proj_dx", (N // tm2, NDEV, 1),
        [(d_o, BS((tm2, D), lambda i, q, k: (i, 0)), Wout, BS((U, D), lambda i, q, k: (q, 0)), "nt", 0)],
        [(tm2, U)],
        [(proj, gs_blk), (proj, gp_blk), (Ys, out_blk), (Yp, out_blk)],
        [(_sds((N, D), BF16), out_blk)] * 4,
        dmerged_epi)
    (gW_out,) = _fused_matmul(
        "out_proj_dw", (D // tm_d, D // tn_d, 1),
        [(merged, BS((N, tm_d), lambda i, j, k: (0, i)), d_o, BS((N, tn_d), lambda i, j, k: (0, j)), "tn", 0)],
        [(tm_d, tn_d)], [], [(_sds((D, D), BF16), BS((tm_d, tn_d), lambda i, j, k: (i, j)))],
        _store(lambda accs: accs))
    tm_s = _tile(DS, 512)
    gW_bs, gW_bp = _fused_matmul(
        "branch_dw", (DS // tm_s, NDEV, 1),
        [(ys, BS((N, tm_s), lambda i, q, k: (0, i)), dYs, BS((N, U), lambda i, q, k: (0, q)), "tn", 0),
         (yp, BS((N, tm_s), lambda i, q, k: (0, i)), dYp, BS((N, U), lambda i, q, k: (0, q)), "tn", 1)],
        [(tm_s, U), (tm_s, U)], [],
        [(_sds((NDEV, DS, U), BF16), BS((None, tm_s, U), lambda i, q, k: (q, i, 0)))] * 2,
        _store(lambda accs: accs))
    x_br = x_start("branch", [gW_bs, gW_bp, gW_out.reshape(NDEV, U, D)])

    tn_p = _tile(PG, 512)

    def dyp_epi(ids, accs, ex, o):
        q_ref, sc_ref = ex
        first = ids[1] == 0
        dyp = accs[0]
        dq = dyp * sc_ref[...]
        o[0][...] = dq.astype(BF16)
        _rowsum_into(o[1], first, dyp * q_ref[...])
        _rowsum_into(o[2], first, dq)

    dq, d_pscale, d_pb = _fused_matmul(
        "branch_pool_dx", (DP // tn_p, N // tm, 1),
        [(dYp, BS((tm, D), lambda j, i, k: (i, 0)), Wbp, BS((NDEV, tn_p, U), lambda j, i, k: (0, j, 0)), "nt_cat", 0)],
        [(tm, tn_p)],
        [(qp, BS((tm, tn_p), lambda j, i, k: (i, j))), (pool_scale, BS((1, tn_p), lambda j, i, k: (0, j)))],
        [(_sds((N, DP), BF16), BS((tm, tn_p), lambda j, i, k: (i, j))),
         (_sds((1, DP), F32), BS((1, tn_p), lambda j, i, k: (0, j))),
         (_sds((1, DP), F32), BS((1, tn_p), lambda j, i, k: (0, j)))],
        dyp_epi, deps=[x_br[4]])
    (dz,) = _fused_matmul(
        "pool_mm_dx", (N // tm, NPG, 1),
        [(dq, BS((tm, PG), lambda i, g, k: (i, g)), Wpool, BS((None, PG, PG), lambda i, g, k: (g, 0, 0)), "nt", 0)],
        [(tm, PG)], [], [(_sds((N, DP), F32), BS((tm, PG), lambda i, g, k: (i, g)))],
        _store(lambda accs: accs))
    (gW_pool,) = _fused_matmul(
        "pool_mm_dw", (NPG, 1),
        [(z, BS((N, PG), lambda g, k: (0, g)), dq, BS((N, PG), lambda g, k: (0, g)), "tn", 0)],
        [(PG, PG)], [], [(_sds((NPG, PG, PG), BF16), BS((None, PG, PG), lambda g, k: (g, 0, 0)))],
        _store(lambda accs: accs))
    du_pool = _pool_bwd(dz, nseq, L)

    def dys_epi(ids, accs, ex, o):
        zg_ref, y0_ref = ex
        first = ids[1] == 0
        dys = accs[0]
        sg = _sigmoid(zg_ref[...])
        dzg = dys * _gelu(y0_ref[...]) * sg * (1.0 - sg)
        o[0][...] = dzg.astype(BF16)
        o[1][...] = dys * sg
        _rowsum_into(o[2], first, dzg)

    dzg, dy1_direct, d_glu_b = _fused_matmul(
        "branch_ssm_dx", (DS // tn_s, N // tm, 1),
        [(dYs, BS((tm, D), lambda j, i, k: (i, 0)), Wbs, BS((NDEV, tn_s, U), lambda j, i, k: (0, j, 0)), "nt_cat", 0)],
        [(tm, tn_s)],
        [(zg, BS((tm, tn_s), lambda j, i, k: (i, j))), (y0, BS((tm, tn_s), lambda j, i, k: (i, j)))],
        [(_sds((N, DS), BF16), BS((tm, tn_s), lambda j, i, k: (i, j))),
         (_sds((N, DS), F32), BS((tm, tn_s), lambda j, i, k: (i, j))),
         (_sds((1, DS), F32), BS((1, tn_s), lambda j, i, k: (0, j)))],
        dys_epi)
    (gW_glu,) = _fused_matmul(
        "ssm_glu_dw", (DS // tm_s, DS // tn_s, 1),
        [(y1, BS((N, tm_s), lambda i, j, k: (0, i)), dzg, BS((N, tn_s), lambda i, j, k: (0, j)), "tn", 0)],
        [(tm_s, tn_s)], [], [(_sds((DS, DS), BF16), BS((tm_s, tn_s), lambda i, j, k: (i, j)))],
        _store(lambda accs: accs))
    x_mix = x_start("mix", [gW_glu.reshape(NDEV, DS // NDEV, DS),
                            gW_pool.reshape(NPG, NDEV, PG // NDEV, PG).transpose(1, 0, 2, 3)])

    tn_c = _tile(DS, CH)

    def dy0_epi(ids, accs, ex, o):
        d1_ref, y0_ref, u_ref = ex
        first = ids[1] == 0
        dy0 = (accs[0] + d1_ref[...]) * _gelu_grad(y0_ref[...])
        o[0][...] = dy0
        _rowsum_into(o[1], first, dy0 * u_ref[...])

    dy0, d_ssm_d = _fused_matmul(
        "ssm_glu_dx", (DS // tn_c, N // tm, 1),
        [(dzg, BS((tm, DS), lambda j, i, k: (i, 0)), Wglu, BS((tn_c, DS), lambda j, i, k: (j, 0)), "nt", 0)],
        [(tm, tn_c)],
        [(dy1_direct, BS((tm, tn_c), lambda j, i, k: (i, j))), (y0, BS((tm, tn_c), lambda j, i, k: (i, j))),
         (proj, BS((tm, tn_c), lambda j, i, k: (i, j)))],
        [(_sds((N, DS), F32), BS((tm, tn_c), lambda j, i, k: (i, j))),
         (_sds((1, DS), F32), BS((1, tn_c), lambda j, i, k: (0, j)))],
        dy0_epi, deps=[x_mix[4]])

    ds_re, ds_im = _fused_matmul(
        "ssm_out_dx", (N // tm2, NCH, 1),
        [(dy0, BS((tm2, CH), lambda i, c, k: (i, c)), WCre, BS((None, CS, CH), lambda i, c, k: (c, 0, 0)), "nt", 0),
         (dy0, BS((tm2, CH), lambda i, c, k: (i, c)), WCim, BS((None, CS, CH), lambda i, c, k: (c, 0, 0)), "nt", 1)],
        [(tm2, CS), (tm2, CS)], [],
        [(_sds((N, SL), F32), BS((tm2, CS), lambda i, c, k: (i, c)))] * 2,
        _store(lambda accs: accs))
    lam_r, lam_i, d_ab_re, d_ab_im = _scan_bwd(ds_re, ds_im, s_re, s_im, pw_re, pw_im, pwf_re, pwf_im, nseq, L)

    def du_epi(ids, accs, ex, o):
        dy0_ref, d_ref = ex
        o[0][...] = (accs[0] + dy0_ref[...] * d_ref[...]).astype(BF16)

    (du_ssm,) = _fused_matmul(
        "ssm_in_dx", (N // tm2, NCH, 1),
        [(lam_r, BS((tm2, CS), lambda i, c, k: (i, c)), WB, BS((None, CH, CS), lambda i, c, k: (c, 0, 0)), "nt", 0),
         (lam_i, BS((tm2, CS), lambda i, c, k: (i, c)), WB, BS((None, CH, CS), lambda i, c, k: (c, 0, 1)), "nt", 0)],
        [(tm2, CH)],
        [(dy0, BS((tm2, CH), lambda i, c, k: (i, c))), (ssm_d, BS((1, CH), lambda i, c, k: (0, c)))],
        [(_sds((N, DS), BF16), BS((tm2, CH), lambda i, c, k: (i, c)))],
        du_epi)
    dproj = jnp.concatenate([du_ssm, du_pool, dgs, dgp], axis=1)
    (gW_in,) = _fused_matmul(
        "in_proj_dw", (D // tm_d, nq, 1),
        [(a, BS((N, tm_d), lambda i, q, k: (0, i)), dproj, BS((N, U), lambda i, q, k: (0, q)), "tn", 0)],
        [(tm_d, U)], [], [(_sds((NDEV, D, 3 * U), BF16), BS((None, tm_d, U), lambda i, q, k: (q // 3, i, q % 3)))],
        _store(lambda accs: accs))
    x_in = x_start("in", [gW_in])
    (da,) = _fused_matmul(
        "in_proj_dx", (N // tm, D // tn_d2, NDEV),
        [(dproj, BS((tm, 3 * U), lambda i, j, k: (i, k)), Win, BS((None, tn_d2, 3 * U), lambda i, j, k: (k, j, 0)), "nt", 0)],
        [(tm, tn_d2)], [], [(_sds((N, D), F32), BS((tm, tn_d2), lambda i, j, k: (i, j)))],
        _store(lambda accs: accs), deps=[x_in[4]])
    grad_x, dg1 = _pre_bwd(x2, da, dh1, norm_pre_mix)

    dWCre, dWCim = _fused_matmul(
        "ssm_out_dw", (NCH, N // tk_n),
        [(s_re, BS((tk_n, CS), lambda c, k: (k, c)), dy0, BS((tk_n, CH), lambda c, k: (k, c)), "tn", 0),
         (s_im, BS((tk_n, CS), lambda c, k: (k, c)), dy0, BS((tk_n, CH), lambda c, k: (k, c)), "tn", 1)],
        [(CS, CH), (CS, CH)], [],
        [(_sds((NCH, CS, CH), F32), BS((None, CS, CH), lambda c, k: (c, 0, 0)))] * 2,
        _store(lambda accs: accs), deps=[x_in[4]])
    dWBre, dWBim = _fused_matmul(
        "ssm_in_dw", (NCH, N // tk_n),
        [(proj, BS((tk_n, CH), lambda c, k: (k, c)), lam_r, BS((tk_n, CS), lambda c, k: (k, c)), "tn", 0),
         (proj, BS((tk_n, CH), lambda c, k: (k, c)), lam_i, BS((tk_n, CS), lambda c, k: (k, c)), "tn", 1)],
        [(CH, CS), (CH, CS)], [],
        [(_sds((NCH, CH, CS), F32), BS((None, CH, CS), lambda c, k: (c, 0, 0)))] * 2,
        _store(lambda accs: accs), deps=[x_in[4]])
    d_bbr = _block_diag_in_grad(dWBre, J, G, P)
    d_bbi = _block_diag_in_grad(dWBim, J, G, P)
    d_lam_re, d_lam_im, d_log_step, d_br_t, d_bi_t = _ssm_param_bwd(
        lam_re, lam_im, log_step, br_t, bi_t,
        d_ab_re.reshape(nseq * SUBLANES, G, P), d_ab_im.reshape(nseq * SUBLANES, G, P), d_bbr, d_bbi)
    d_c_re = _block_diag_out_grad(dWCre, G, J, P)
    d_c_im = -_block_diag_out_grad(dWCim, G, J, P)

    d_conv_w = dcw.reshape(NB, 3, FC).transpose(1, 0, 2).reshape(3, F2)
    d_conv_b = dcb.reshape(1, F2)
    small = {
        "norm_pre_mix": dg1, "norm_post_mix": dg2, "norm_pre_ffn": dg3, "norm_post_ffn": dg4,
        "ssm_lambda_re": d_lam_re[None], "ssm_lambda_im": d_lam_im[None], "ssm_log_step": d_log_step.reshape(1, G),
        "ssm_b_re": d_br_t.transpose(1, 2, 0)[None], "ssm_b_im": d_bi_t.transpose(1, 2, 0)[None],
        "ssm_c_re": d_c_re[None], "ssm_c_im": d_c_im[None],
        "ssm_d": d_ssm_d, "ssm_glu_b": d_glu_b, "pool_scale": d_pscale,
        "pool_b": d_pb.reshape(1, NPG, PG), "ffn_conv_w": d_conv_w[None], "ffn_conv_b": d_conv_b,
    }
    assert list(small) == small_names
    wide = ["ssm_b_re", "ssm_b_im", "ssm_c_re", "ssm_c_im"]
    narrow = [n for n in small_names if n not in wide]
    packed = _pack([lossv] + [small[n] for n in narrow])
    packed_bf = _pack([small[n] for n in wide]).astype(BF16)
    st_small = _split_start("small_start", _broadcast_copies, NDEV - 1, [packed, packed_bf],
                            [_sds((NDEV,) + p.shape, p.dtype) for p in (packed, packed_bf)], packed)

    res = {}
    after = st_small[4]
    for tag, started, group in (("down", x_down, ["w_down"]), ("up", x_up, ["w_up"]),
                                ("branch", x_br, ["w_branch_ssm", "w_branch_pool", "w_out"]),
                                ("mix", x_mix, ["ssm_glu_w", "pool_w"]), ("in", x_in, ["w_in"])):
        for n, parts in zip(group, x_finish(tag, started, after)):
            shape = args[n].shape
            if n == "w_up":
                flat, back = (lambda t: t[0].T), (lambda t: t.T[None])
            else:
                flat, back = (lambda t: t.reshape(-1, shape[-1])), (lambda t: t.reshape(shape))
            w2 = flat(args[n])
            g, dl, nm, nv = _adamw("adamw_" + n, w2, flat(args["m_" + n]), flat(args["v_" + n]),
                                   parts.reshape((NDEV,) + w2.shape))
            res[n] = tuple(back(t) for t in (g, dl, nm, nv))
            after = g

    srcs, lands = _split_wait("small_wait", _broadcast_copies, st_small, after)
    small_all, small_all_bf = (lax.dynamic_update_index_in_dim(l, s, dev, 0) for l, s in zip(lands, srcs))
    loss_rows = (D + SUBLANES * LANES - 1) // (SUBLANES * LANES) * SUBLANES
    total, total_wide, loss = _small_sum(small_all, small_all_bf, loss_rows, D)
    totals = dict(zip(narrow, _unpack(total, [lossv.shape] + [small[n].shape for n in narrow])[1:]))
    totals.update(zip(wide, _unpack(total_wide, [small[n].shape for n in wide])))
    totals["pool_b"] = lax.dynamic_slice_in_dim(totals["pool_b"], dev * (PG // NDEV), PG // NDEV, axis=2)
    totals["ffn_conv_w"] = lax.dynamic_slice_in_dim(totals["ffn_conv_w"], dev * FC, FC, axis=2)
    sm_g = _pack([totals[n] for n in small_names])
    _, sm_d, sm_nm, sm_nv = _adamw("adamw_small", sm_w, sm_m, sm_v, sm_g[None])
    shapes = [args[n].shape for n in small_names]
    for n, dl, nm, nv in zip(small_names, _unpack(sm_d, shapes), _unpack(sm_nm, shapes), _unpack(sm_nv, shapes)):
        res[n] = (totals[n], dl, nm, nv)

    outs = [loss.reshape(()), grad_x.reshape(x.shape)]
    for k in range(4):
        outs += [res[n][k] for n in names]
    return tuple(outs)
```

```python
import functools
import math

import jax
import jax.numpy as jnp
from jax import lax
from jax.experimental import pallas as pl
from jax.experimental.pallas import tpu as pltpu

F32 = jnp.float32
BF16 = jnp.bfloat16
BS = pl.BlockSpec

NDEV = 8
SSM_GROUP = 16
SSM_STATE = 64
GROUPS_PER_CHUNK = 16
SCAN_UNROLL = 8
POOL_WINDOWS = (2, 4, 8, 16)
EPS = 1e-6
MIN_NEG_REAL = -1e-4
ADAM_LR, ADAM_B1, ADAM_B2, ADAM_EPS, ADAM_WD, ADAM_STEP = 0.001, 0.9, 0.999, 1e-08, 0.01, 10
LANES = 128
SUBLANES = 8
VMEM_LIMIT = 56 * 1024 * 1024

_DIMS = {"nn": (((1,), (0,)), ((), ())), "nt": (((1,), (1,)), ((), ())), "tn": (((0,), (0,)), ((), ()))}


def _tile(dim, pref, mult=LANES):
    if dim <= pref:
        return dim
    t = (pref // mult) * mult
    while t >= mult:
        if dim % t == 0:
            return t
        t -= mult
    return dim


def _pc(name, body, grid, ins, in_specs, outs, out_specs, scratch=(), deps=()):
    multi = isinstance(outs, (list, tuple))
    if deps:
        n_in, n_dep, inner = len(ins), len(deps), body

        def body(*refs):
            return inner(*refs[:n_in], *refs[n_in + n_dep:])

        ins = list(ins) + list(deps)
        in_specs = list(in_specs) + [BS(memory_space=pl.ANY)] * n_dep
    return pl.pallas_call(
        body, name=name, grid=grid, in_specs=list(in_specs),
        out_specs=list(out_specs) if multi else out_specs,
        out_shape=list(outs) if multi else outs, scratch_shapes=list(scratch),
        compiler_params=pltpu.CompilerParams(dimension_semantics=("arbitrary",) * len(grid),
                                             vmem_limit_bytes=VMEM_LIMIT),
    )(*ins)


def _sds(shape, dtype):
    return jax.ShapeDtypeStruct(tuple(shape), dtype)


def _gelu(x):
    k = math.sqrt(2.0 / math.pi)
    return 0.5 * x * (1.0 + jnp.tanh(k * (x + 0.044715 * (x * x * x))))


def _gelu_grad(x):
    k = math.sqrt(2.0 / math.pi)
    t = jnp.tanh(k * (x + 0.044715 * (x * x * x)))
    return 0.5 * (1.0 + t) + 0.5 * x * (1.0 - t * t) * (k * (1.0 + 3.0 * 0.044715 * x * x))


def _sigmoid(x):
    return jax.nn.sigmoid(x)


def _fused_matmul(name, grid, pairs, acc_shapes, extras, outs, epilogue, deps=()):
    n_p, n_e, n_o = len(pairs), len(extras), len(outs)
    rank = len(grid)
    nk = grid[-1]
    if nk == 1:
        acc_shapes = []

    def body(*refs):
        ab = refs[:2 * n_p]
        ex = refs[2 * n_p:2 * n_p + n_e]
        o = refs[2 * n_p + n_e:2 * n_p + n_e + n_o]
        accs = refs[2 * n_p + n_e + n_o:]
        ids = [pl.program_id(d) for d in range(rank)]
        k = ids[-1]

        def products():
            sums = {}
            for p in range(n_p):
                a = ab[2 * p][...].astype(BF16)
                mode = pairs[p][4]
                if mode == "nt_cat":
                    b_ref = ab[2 * p + 1]
                    b = jnp.concatenate([b_ref[q].astype(BF16) for q in range(b_ref.shape[0])], axis=1)
                    mode = "nt"
                else:
                    b = ab[2 * p + 1][...].astype(BF16)
                d = lax.dot_general(a, b, _DIMS[mode], preferred_element_type=F32)
                sums[pairs[p][5]] = d if pairs[p][5] not in sums else sums[pairs[p][5]] + d
            return [sums[i] for i in range(len(sums))]

        if nk == 1:
            epilogue(ids, products(), ex, o)
            return

        @pl.when(k == 0)
        def _():
            for acc, s in zip(accs, products()):
                acc[...] = s

        if nk > 2:
            @pl.when(jnp.logical_and(k > 0, k < nk - 1))
            def _():
                for acc, s in zip(accs, products()):
                    acc[...] += s

        @pl.when(k == nk - 1)
        def _():
            epilogue(ids, [acc[...] + s for acc, s in zip(accs, products())], ex, o)

    ins, in_specs = [], []
    for a, a_spec, b, b_spec, _, _ in pairs:
        ins += [a, b]
        in_specs += [a_spec, b_spec]
    for e, e_spec in extras:
        ins.append(e)
        in_specs.append(e_spec)
    res = _pc(name, body, grid, ins, in_specs, [s for s, _ in outs], [sp for _, sp in outs],
              scratch=[pltpu.VMEM(tuple(s), F32) for s in acc_shapes], deps=deps)
    return res


def _store(vals):
    def epilogue(ids, accs, ex, o):
        for r, v in zip(o, vals(accs)):
            r[...] = v.astype(r.dtype)
    return epilogue


def _rowsum_into(ref, first, v):
    s = jnp.sum(v, axis=0, keepdims=True)

    @pl.when(first)
    def _():
        ref[...] = s

    @pl.when(jnp.logical_not(first))
    def _():
        ref[...] += s


def _mesh_pos():
    return lax.axis_index("x"), lax.axis_index("y"), lax.axis_index("c")


def _slot(p):
    return 4 * p[0] + 2 * p[1] + p[2]


_HBM = BS(memory_space=pltpu.HBM)
_SEM = BS(memory_space=pltpu.SEMAPHORE)
_ANY = BS(memory_space=pl.ANY)
_EFFECT = pltpu.SideEffectType.DATAFLOW_SIDE_EFFECTING


def _other_chips(x, y):
    return [(1 - x, y), (x, 1 - y), (1 - x, 1 - y)]


def _all_peers(x, y, c):
    peers = []
    for k in range(1, NDEV):
        kx, ky, kc = (k >> 2) & 1, (k >> 1) & 1, k & 1
        peers.append((1 - x if kx else x, 1 - y if ky else y, 1 - c if kc else c))
    return peers


def _gather_copies(src, land, send_sems, recv_sems, base):
    x, y, c = _mesh_pos()
    return [pltpu.make_async_remote_copy(
        src_ref=src, dst_ref=land.at[_slot((x, y, c))],
        send_sem=send_sems.at[base + k], recv_sem=recv_sems.at[base + k],
        device_id=(*chip, c), device_id_type=pl.DeviceIdType.MESH) for k, chip in enumerate(_other_chips(x, y))]


def _d2d_copies(src, land, send_sems, recv_sems, base):
    x, y, c = _mesh_pos()
    blocks = [(x, y, c)] + [(*chip, c) for chip in _other_chips(x, y)]
    return [pltpu.make_async_remote_copy(
        src_ref=src if k == 0 else land.at[_slot(b)], dst_ref=land.at[_slot(b)],
        send_sem=send_sems.at[base + k], recv_sem=recv_sems.at[base + k],
        device_id=(x, y, 1 - c), device_id_type=pl.DeviceIdType.MESH) for k, b in enumerate(blocks)]


def _broadcast_copies(src, land, send_sems, recv_sems, base):
    x, y, c = _mesh_pos()
    return [pltpu.make_async_remote_copy(
        src_ref=src, dst_ref=land.at[_slot((x, y, c))],
        send_sem=send_sems.at[base + k], recv_sem=recv_sems.at[base + k],
        device_id=peer, device_id_type=pl.DeviceIdType.MESH) for k, peer in enumerate(_all_peers(x, y, c))]


def _exchange_copies(src, land, send_sems, recv_sems, base):
    x, y, c = _mesh_pos()
    return [pltpu.make_async_remote_copy(
        src_ref=src.at[_slot(peer)], dst_ref=land.at[_slot((x, y, c))],
        send_sem=send_sems.at[base + k], recv_sem=recv_sems.at[base + k],
        device_id=peer, device_id_type=pl.DeviceIdType.MESH) for k, peer in enumerate(_all_peers(x, y, c))]


def _split_start(name, copies, ncopy, srcs, land_shapes, after):
    n = len(srcs)

    def body(*refs):
        src_refs, land_refs = refs[:n], refs[n:2 * n]
        send_sems, recv_sems = refs[2 * n + 1], refs[2 * n + 2]
        token = refs[-1]
        for r in range(n):
            for cp in copies(src_refs[r], land_refs[r], send_sems, recv_sems, r * ncopy):
                cp.start()
        token[...] = jnp.zeros_like(token)

    lands = [s if isinstance(s, jax.Array) else pltpu.with_memory_space_constraint(lax.empty(s.shape, s.dtype), pltpu.HBM)
             for s in land_shapes]
    ins = list(srcs) + lands
    out_shape = ([pltpu.SemaphoreType.DMA((n * ncopy,)), pltpu.SemaphoreType.DMA((n * ncopy,))]
                 + [pltpu.HBM(a.shape, a.dtype) for a in lands]
                 + [_sds((SUBLANES, LANES), F32)])
    res = pl.pallas_call(
        body, name=name, out_shape=out_shape,
        in_specs=[_HBM] * (2 * n) + [_ANY], out_specs=[_SEM, _SEM] + [_HBM] * n + [BS(memory_space=pltpu.VMEM)],
        input_output_aliases={n + i: 2 + i for i in range(n)},
        compiler_params=pltpu.CompilerParams(has_side_effects=_EFFECT),
    )(*ins, after)
    return res[0], res[1], list(srcs), list(res[2:2 + n]), res[-1]


def _split_wait(name, copies, started, after):
    send_sems, recv_sems, srcs, lands, _ = started
    n = len(srcs)
    ncopy = send_sems.shape[0] // n
    after = list(after) if isinstance(after, (list, tuple)) else [after]

    def body(*refs):
        src_refs, land_refs = refs[:n], refs[n:2 * n]
        send_sems, recv_sems = refs[2 * n], refs[2 * n + 1]
        for r in range(n):
            for cp in copies(src_refs[r], land_refs[r], send_sems, recv_sems, r * ncopy):
                cp.wait_send()
                cp.wait_recv()

    res = pl.pallas_call(
        body, name=name, out_shape=[pltpu.HBM(a.shape, a.dtype) for a in lands],
        in_specs=[_HBM] * (2 * n) + [_SEM, _SEM] + [_ANY] * len(after), out_specs=[_HBM] * n,
        input_output_aliases={n + i: i for i in range(n)},
        compiler_params=pltpu.CompilerParams(has_side_effects=_EFFECT),
    )(*srcs, *lands, send_sems, recv_sems, *after)
    return list(srcs), list(res)


def _adamw(name, w, m, v, parts):
    R, C = w.shape
    S = parts.shape[0]
    tr = _tile(R, max(SUBLANES, (256 * 1024) // C), SUBLANES)

    def body(w_ref, m_ref, v_ref, p_ref, g_ref, d_ref, nm_ref, nv_ref):
        g = p_ref[0].astype(F32)
        for s in range(1, S):
            g = g + p_ref[s].astype(F32)
        m2 = ADAM_B1 * m_ref[...] + (1.0 - ADAM_B1) * g
        v2 = ADAM_B2 * v_ref[...] + (1.0 - ADAM_B2) * (g * g)
        m_hat = m2 / (1.0 - ADAM_B1 ** ADAM_STEP)
        v_hat = v2 / (1.0 - ADAM_B2 ** ADAM_STEP)
        g_ref[...] = g
        d_ref[...] = -ADAM_LR * (m_hat / (jnp.sqrt(v_hat) + ADAM_EPS) + ADAM_WD * w_ref[...])
        nm_ref[...] = m2
        nv_ref[...] = v2

    blk = BS((tr, C), lambda i: (i, 0))
    return _pc(name, body, (R // tr,), [w, m, v, parts],
               [blk, blk, blk, BS((S, tr, C), lambda i: (0, i, 0))],
               [_sds((R, C), F32)] * 4, [blk] * 4)


def _ssm_disc(lam_re, lam_im, log_step, br_t, bi_t):
    lr = jnp.minimum(lam_re, MIN_NEG_REAL)
    li = lam_im
    dt = jnp.exp(log_step)
    mag = jnp.exp(lr * dt)
    ang = li * dt
    ab_re = mag * jnp.cos(ang)
    ab_im = mag * jnp.sin(ang)
    nr = ab_re - 1.0
    ni = ab_im
    den = lr * lr + li * li
    f_re = (nr * lr + ni * li) / den
    f_im = (ni * lr - nr * li) / den
    bb_re = f_re[None] * br_t - f_im[None] * bi_t
    bb_im = f_re[None] * bi_t + f_im[None] * br_t
    return ab_re, ab_im, bb_re, bb_im


def _ssm_param_fwd(lam_re, lam_im, log_step, br_t, bi_t):
    G, P = lam_re.shape

    def body(lr_ref, li_ref, ls_ref, br_ref, bi_ref, pw_re_ref, pw_im_ref, pwf_re_ref, pwf_im_ref, bbr_ref, bbi_ref):
        ab_re, ab_im, bb_re, bb_im = _ssm_disc(lr_ref[...], li_ref[...], ls_ref[...], br_ref[...], bi_ref[...])
        bbr_ref[...] = bb_re
        bbi_ref[...] = bb_im
        pr, pi = ab_re, ab_im
        for r in range(SUBLANES):
            pw_re_ref[r] = pr
            pw_im_ref[r] = pi
            pwf_re_ref[SUBLANES - 1 - r] = pr
            pwf_im_ref[SUBLANES - 1 - r] = pi
            pr, pi = pr * ab_re - pi * ab_im, pr * ab_im + pi * ab_re

    full = lambda a: BS(a.shape, lambda i: (0,) * a.ndim)
    ins = [lam_re, lam_im, log_step, br_t, bi_t]
    outs = [_sds((SUBLANES, G, P), F32)] * 4 + [_sds(br_t.shape, F32)] * 2
    return _pc("ssm_param_fwd", body, (1,), ins, [full(a) for a in ins], outs, [full(o) for o in outs])


def _ssm_param_bwd(lam_re, lam_im, log_step, br_t, bi_t, d_ab_re, d_ab_im, d_bbr, d_bbi):
    def body(lr_ref, li_ref, ls_ref, br_ref, bi_ref, dar_ref, dai_ref, dbr_ref, dbi_ref,
             o_lr, o_li, o_ls, o_br, o_bi):
        prim = (lr_ref[...], li_ref[...], ls_ref[...], br_ref[...], bi_ref[...])
        _, vjp = jax.vjp(_ssm_disc, *prim)
        dar = dar_ref[0]
        dai = dai_ref[0]
        for k in range(1, dar_ref.shape[0]):
            dar = dar + dar_ref[k]
            dai = dai + dai_ref[k]
        g = vjp((dar, dai, dbr_ref[...], dbi_ref[...]))
        for r, v in zip((o_lr, o_li, o_ls, o_br, o_bi), g):
            r[...] = v

    full = lambda a: BS(a.shape, lambda i: (0,) * a.ndim)
    ins = [lam_re, lam_im, log_step, br_t, bi_t, d_ab_re, d_ab_im, d_bbr, d_bbi]
    outs = [_sds(a.shape, F32) for a in (lam_re, lam_im, log_step, br_t, bi_t)]
    return _pc("ssm_param_bwd", body, (1,), ins, [full(a) for a in ins], outs, [full(o) for o in outs])


def _bcast_row(ref, r, w):
    return jnp.broadcast_to(ref[pl.ds(r, 1), :], (SUBLANES, w))


def _pick_row(x, row, r):
    return jnp.broadcast_to(jnp.sum(jnp.where(row == r, x, 0.0), axis=0, keepdims=True), x.shape)


def _scan_fwd(bu_re, bu_im, pw_re, pw_im, nseq, L):
    N, SL = bu_re.shape
    W = _tile(SL, 256)
    unroll = math.gcd(L // SUBLANES, SCAN_UNROLL)

    def body(bre_ref, bim_ref, pre_ref, pim_ref, sre_ref, sim_ref):
        pre, pim = pre_ref[...], pim_ref[...]
        steps = [(k, _bcast_row(pre_ref, k - 1, W), _bcast_row(pim_ref, k - 1, W)) for k in (1, 2, 4)]
        row = lax.broadcasted_iota(jnp.int32, (SUBLANES, W), 0)

        def step(i, carry):
            cr, ci = carry
            r0 = pl.multiple_of(i * SUBLANES, SUBLANES)
            xr = bre_ref[pl.ds(r0, SUBLANES), :]
            xi = bim_ref[pl.ds(r0, SUBLANES), :]
            for k, ar, ai in steps:
                sr = pltpu.roll(xr, k, axis=0)
                si = pltpu.roll(xi, k, axis=0)
                keep = row >= k
                xr, xi = (xr + jnp.where(keep, ar * sr - ai * si, 0.0),
                          xi + jnp.where(keep, ar * si + ai * sr, 0.0))
            xr, xi = xr + (pre * cr - pim * ci), xi + (pre * ci + pim * cr)
            sre_ref[pl.ds(r0, SUBLANES), :] = xr
            sim_ref[pl.ds(r0, SUBLANES), :] = xi
            return _pick_row(xr, row, SUBLANES - 1), _pick_row(xi, row, SUBLANES - 1)

        def group(g, carry):
            for u in range(unroll):
                carry = step(g * unroll + u, carry)
            return carry

        zero = jnp.zeros((SUBLANES, W), F32)
        lax.fori_loop(0, L // SUBLANES // unroll, group, (zero, zero))

    blk = BS((L, W), lambda s, j: (s, j))
    pw = BS((SUBLANES, W), lambda s, j: (0, j))
    return _pc("ssm_scan_fwd", body, (nseq, SL // W), [bu_re, bu_im, pw_re, pw_im], [blk, blk, pw, pw],
               [_sds((N, SL), F32)] * 2, [blk, blk])


def _scan_bwd(ds_re, ds_im, s_re, s_im, pw_re, pw_im, pwf_re, pwf_im, nseq, L):
    N, SL = ds_re.shape
    W = _tile(SL, 256)
    nt = L // SUBLANES
    unroll = math.gcd(nt, SCAN_UNROLL)

    def body(dsr_ref, dsi_ref, sre_ref, sim_ref, pre_ref, pim_ref, fre_ref, fim_ref,
             lre_ref, lim_ref, dar_ref, dai_ref):
        fre, fim = fre_ref[...], -fim_ref[...]
        steps = [(k, _bcast_row(pre_ref, k - 1, W), -_bcast_row(pim_ref, k - 1, W)) for k in (1, 2, 4)]
        row = lax.broadcasted_iota(jnp.int32, (SUBLANES, W), 0)

        def step(ii, carry):
            cr, ci, acr, aci = carry
            i = nt - 1 - ii
            r0 = pl.multiple_of(i * SUBLANES, SUBLANES)
            xr = dsr_ref[pl.ds(r0, SUBLANES), :]
            xi = dsi_ref[pl.ds(r0, SUBLANES), :]
            for k, ar, ai in steps:
                sr = pltpu.roll(xr, SUBLANES - k, axis=0)
                si = pltpu.roll(xi, SUBLANES - k, axis=0)
                keep = row < SUBLANES - k
                xr, xi = (xr + jnp.where(keep, ar * sr - ai * si, 0.0),
                          xi + jnp.where(keep, ar * si + ai * sr, 0.0))
            xr, xi = xr + (fre * cr - fim * ci), xi + (fre * ci + fim * cr)
            lre_ref[pl.ds(r0, SUBLANES), :] = xr
            lim_ref[pl.ds(r0, SUBLANES), :] = xi
            p0 = pl.multiple_of(jnp.maximum(i - 1, 0) * SUBLANES, SUBLANES)
            has_prev = i > 0
            spr = jnp.where(row == 0,
                            jnp.where(has_prev, pltpu.roll(sre_ref[pl.ds(p0, SUBLANES), :], 1, axis=0), 0.0),
                            pltpu.roll(sre_ref[pl.ds(r0, SUBLANES), :], 1, axis=0))
            spi = jnp.where(row == 0,
                            jnp.where(has_prev, pltpu.roll(sim_ref[pl.ds(p0, SUBLANES), :], 1, axis=0), 0.0),
                            pltpu.roll(sim_ref[pl.ds(r0, SUBLANES), :], 1, axis=0))
            acr = acr + (xr * spr + xi * spi)
            aci = aci + (xi * spr - xr * spi)
            return _pick_row(xr, row, 0), _pick_row(xi, row, 0), acr, aci

        def group(g, carry):
            for u in range(unroll):
                carry = step(g * unroll + u, carry)
            return carry

        zero = jnp.zeros((SUBLANES, W), F32)
        _, _, acr, aci = lax.fori_loop(0, nt // unroll, group, (zero, zero, zero, zero))
        dar_ref[...] = acr
        dai_ref[...] = aci

    blk = BS((L, W), lambda s, j: (s, j))
    pw = BS((SUBLANES, W), lambda s, j: (0, j))
    da = BS((None, SUBLANES, W), lambda s, j: (s, 0, j))
    return _pc("ssm_scan_bwd", body, (nseq, SL // W),
               [ds_re, ds_im, s_re, s_im, pw_re, pw_im, pwf_re, pwf_im], [blk] * 4 + [pw] * 4,
               [_sds((N, SL), F32)] * 2 + [_sds((nseq, SUBLANES, SL), F32)] * 2, [blk, blk, da, da])


def _pool_select(g, vals):
    return jnp.where(g == 0, vals[0], jnp.where(g == 1, vals[1], jnp.where(g == 2, vals[2], vals[3])))


def _pool_fwd(proj, col0, DP, nseq, L):
    N = proj.shape[0]
    PG = DP // len(POOL_WINDOWS)
    W = _tile(PG, 256)

    def body(v_ref, z_ref):
        g = pl.program_id(1) // (PG // W)
        v = v_ref[...]
        row = lax.broadcasted_iota(jnp.int32, (L, W), 0)
        sums, s, k = [], v, 1
        for _ in POOL_WINDOWS:
            s = s + jnp.where(row >= k, pltpu.roll(s, k, axis=0), 0.0)
            sums.append(s)
            k *= 2
        win = _pool_select(g, [float(w) for w in POOL_WINDOWS])
        cnt = jnp.minimum((row + 1).astype(F32), win)
        z_ref[...] = (_pool_select(g, sums) / cnt - v).astype(z_ref.dtype)

    return _pc("pool_fwd", body, (nseq, DP // W), [proj], [BS((L, W), lambda s, j: (s, col0 // W + j))],
               _sds((N, DP), BF16), BS((L, W), lambda s, j: (s, j)))


def _pool_bwd(dz, nseq, L):
    N, DP = dz.shape
    PG = DP // len(POOL_WINDOWS)
    W = _tile(PG, 256)

    def body(dz_ref, dv_ref):
        g = pl.program_id(1) // (PG // W)
        d = dz_ref[...]
        row = lax.broadcasted_iota(jnp.int32, (L, W), 0)
        win = _pool_select(g, [float(w) for w in POOL_WINDOWS])
        s = d / jnp.minimum((row + 1).astype(F32), win)
        sums, k = [], 1
        for _ in POOL_WINDOWS:
            s = s + jnp.where(row < L - k, pltpu.roll(s, L - k, axis=0), 0.0)
            sums.append(s)
            k *= 2
        dv_ref[...] = (_pool_select(g, sums) - d).astype(dv_ref.dtype)

    blk = BS((L, W), lambda s, j: (s, j))
    return _pc("pool_bwd", body, (nseq, DP // W), [dz], [blk], _sds((N, DP), BF16), blk)


def _rstd(x):
    return lax.rsqrt(jnp.mean(x * x, axis=-1, keepdims=True) + EPS)


def _norm_bwd(dy, xhat, rstd, gain):
    t = dy * gain
    return rstd * (t - xhat * jnp.mean(t * xhat, axis=-1, keepdims=True))


def _pre_norm(x, g1):
    N, D = x.shape
    tr = _tile(N, 128, SUBLANES)

    def body(x_ref, g_ref, a_ref):
        xv = x_ref[...]
        a_ref[...] = (xv * _rstd(xv) * g_ref[...]).astype(a_ref.dtype)

    row = BS((tr, D), lambda i: (i, 0))
    vec = BS((1, D), lambda i: (0, 0))
    return _pc("pre_norm", body, (N // tr,), [x, g1], [row, vec], _sds((N, D), BF16), row)


def _mid_norm(x, o, g2, g3):
    N, D = x.shape
    tr = _tile(N, 128, SUBLANES)

    def body(x_ref, o_ref, g2_ref, g3_ref, h1_ref, c_ref):
        ov = o_ref[...]
        h1 = x_ref[...] + ov * _rstd(ov) * g2_ref[...]
        h1_ref[...] = h1
        c_ref[...] = (h1 * _rstd(h1) * g3_ref[...]).astype(c_ref.dtype)

    row = BS((tr, D), lambda i: (i, 0))
    vec = BS((1, D), lambda i: (0, 0))
    return _pc("mid_norm", body, (N // tr,), [x, o, g2, g3], [row, row, vec, vec],
               [_sds((N, D), F32), _sds((N, D), BF16)], [row, row])


def _post_ffn(h1, dn, tgt, g4):
    N, D = h1.shape
    tr = _tile(N, 128, SUBLANES)

    def body(h1_ref, dn_ref, t_ref, g_ref, dh2_ref, ddn_ref, lossv_ref, dg_ref):
        first = pl.program_id(0) == 0
        dnv = dn_ref[...]
        rstd = _rstd(dnv)
        xhat = dnv * rstd
        gain = g_ref[...]
        err = (h1_ref[...] + xhat * gain) - t_ref[...]
        dh2 = err / float(D)
        dh2_ref[...] = dh2
        ddn_ref[...] = _norm_bwd(dh2, xhat, rstd, gain).astype(ddn_ref.dtype)
        _rowsum_into(lossv_ref, first, err * err)
        _rowsum_into(dg_ref, first, dh2 * xhat)

    row = BS((tr, D), lambda i: (i, 0))
    vec = BS((1, D), lambda i: (0, 0))
    return _pc("post_ffn", body, (N // tr,), [h1, dn, tgt, g4], [row, row, row, vec],
               [_sds((N, D), F32), _sds((N, D), BF16), _sds((1, D), F32), _sds((1, D), F32)], [row, row, vec, vec])


def _mid_bwd(dh2, dc, h1, o, g2, g3, deps=()):
    N, D = h1.shape
    tr = _tile(N, 128, SUBLANES)

    def body(dh2_ref, dc_ref, h1_ref, o_ref, g2_ref, g3_ref, dh1_ref, do_ref, dg2_ref, dg3_ref):
        first = pl.program_id(0) == 0
        h1 = h1_ref[...]
        r3 = _rstd(h1)
        hc = h1 * r3
        dcv = dc_ref[...]
        dh1 = dh2_ref[...] + _norm_bwd(dcv, hc, r3, g3_ref[...])
        dh1_ref[...] = dh1
        ov = o_ref[...]
        r2 = _rstd(ov)
        ho = ov * r2
        do_ref[...] = _norm_bwd(dh1, ho, r2, g2_ref[...]).astype(do_ref.dtype)
        _rowsum_into(dg3_ref, first, dcv * hc)
        _rowsum_into(dg2_ref, first, dh1 * ho)

    row = BS((tr, D), lambda i: (i, 0))
    vec = BS((1, D), lambda i: (0, 0))
    return _pc("mid_bwd", body, (N // tr,), [dh2, dc, h1, o, g2, g3], [row] * 4 + [vec, vec],
               [_sds((N, D), F32), _sds((N, D), BF16), _sds((1, D), F32), _sds((1, D), F32)], [row, row, vec, vec],
               deps=deps)


def _pre_bwd(x, da, dh1, g1):
    N, D = x.shape
    tr = _tile(N, 128, SUBLANES)

    def body(x_ref, da_ref, dh1_ref, g_ref, dx_ref, dg_ref):
        first = pl.program_id(0) == 0
        xv = x_ref[...]
        r1 = _rstd(xv)
        xh = xv * r1
        dav = da_ref[...]
        dx_ref[...] = dh1_ref[...] + _norm_bwd(dav, xh, r1, g_ref[...])
        _rowsum_into(dg_ref, first, dav * xh)

    row = BS((tr, D), lambda i: (i, 0))
    vec = BS((1, D), lambda i: (0, 0))
    return _pc("pre_bwd", body, (N // tr,), [x, da, dh1, g1], [row, row, row, vec],
               [_sds((N, D), F32), _sds((1, D), F32)], [row, vec])


def _conv_rows(x_ref, halo_ref, first):
    x = x_ref[...]
    xx = jnp.concatenate([jnp.where(first, 0.0, halo_ref[...]), x], axis=0)
    x1 = pltpu.roll(xx, 1, axis=0)[SUBLANES:]
    x2 = pltpu.roll(xx, 2, axis=0)[SUBLANES:]
    return x, x1, x2


def _conv_apply(rows, w_ref, b_ref):
    x, x1, x2 = rows
    return ((b_ref[...] + x2 * w_ref[pl.ds(0, 1), :]) + x1 * w_ref[pl.ds(1, 1), :]) + x * w_ref[pl.ds(2, 1), :]


def _gate_specs(N, FC, TR, half):
    tile = BS((None, TR, FC), lambda jj, i: (jj + half, i, 0))
    halo = BS((None, SUBLANES, FC), lambda jj, i: (jj + half, jnp.maximum(i * (TR // SUBLANES) - 1, 0), 0))
    cw = BS((None, 3, FC), lambda jj, i: (jj + half, 0, 0))
    cb = BS((None, 1, FC), lambda jj, i: (jj + half, 0, 0))
    return tile, halo, cw, cb


def _gate_fwd(up_pre, cw, cb, L, deps=()):
    nb, N, FC = up_pre.shape
    half = nb // 2
    TR = _tile(L, 128, SUBLANES)

    def body(xa_ref, ha_ref, wa_ref, ba_ref, xb_ref, hb_ref, wb_ref, bb_ref, f_ref):
        first = (pl.program_id(1) % (L // TR)) == 0
        ua = _conv_apply(_conv_rows(xa_ref, ha_ref, first), wa_ref, ba_ref)
        ub = _conv_apply(_conv_rows(xb_ref, hb_ref, first), wb_ref, bb_ref)
        f_ref[...] = (_gelu(ua) * ub).astype(f_ref.dtype)

    sa, sb = _gate_specs(N, FC, TR, 0), _gate_specs(N, FC, TR, half)
    return _pc("gate_fwd", body, (half, N // TR), [up_pre, up_pre, cw, cb] * 2, list(sa) + list(sb),
               _sds((half, N, FC), BF16), BS((None, TR, FC), lambda jj, i: (jj, i, 0)), deps=deps)


def _gate_bwd(up_pre, cw, cb, df, L, deps=()):
    nb, N, FC = up_pre.shape
    half = nb // 2
    TR = _tile(L, 128, SUBLANES)

    def body(xa_ref, ha_ref, wa_ref, ba_ref, xb_ref, hb_ref, wb_ref, bb_ref, df_ref, dup_ref, dw_ref, dbias_ref):
        i = pl.program_id(1)
        first_row = i == 0
        first = (i % (L // TR)) == 0
        ra = _conv_rows(xa_ref, ha_ref, first)
        rb = _conv_rows(xb_ref, hb_ref, first)
        ua = _conv_apply(ra, wa_ref, ba_ref)
        ub = _conv_apply(rb, wb_ref, bb_ref)
        dfv = df_ref[...].astype(F32)
        dua = dfv * ub * _gelu_grad(ua)
        dub = dfv * _gelu(ua)
        dup_ref[0] = dua.astype(dup_ref.dtype)
        dup_ref[1] = dub.astype(dup_ref.dtype)
        for h, (rows, du) in enumerate(((ra, dua), (rb, dub))):
            x, x1, x2 = rows
            _rowsum_into(dbias_ref.at[h], first_row, du)
            for k, xs in enumerate((x2, x1, x)):
                _rowsum_into(dw_ref.at[h, pl.ds(k, 1), :], first_row, du * xs)

    sa, sb = _gate_specs(N, FC, TR, 0), _gate_specs(N, FC, TR, half)
    tile = BS((None, TR, FC), lambda jj, i: (jj, i, 0))
    both = BS((2, None, TR, FC), lambda jj, i: (0, jj, i, 0))
    dw = BS((2, None, 3, FC), lambda jj, i: (0, jj, 0, 0))
    dbias = BS((2, None, 1, FC), lambda jj, i: (0, jj, 0, 0))
    return _pc("gate_bwd", body, (half, N // TR), [up_pre, up_pre, cw, cb] * 2 + [df], list(sa) + list(sb) + [tile],
               [_sds((2, half, N, FC), BF16), _sds((2, half, 3, FC), F32), _sds((2, half, 1, FC), F32)],
               [both, dw, dbias], deps=deps)


def _conv_bwd(dup, cw, L):
    nb, N, FC = dup.shape
    TR = _tile(L, 128, 2 * SUBLANES)
    HR = 2 * SUBLANES
    nrb = N // HR

    def body(x_ref, h_ref, w_ref, o_ref):
        last = ((pl.program_id(1) + 1) % (L // TR)) == 0
        x = x_ref[...].astype(F32)
        xx = jnp.concatenate([x, jnp.where(last, 0.0, h_ref[...].astype(F32))], axis=0)
        x1 = pltpu.roll(xx, TR + HR - 1, axis=0)[:TR]
        x2 = pltpu.roll(xx, TR + HR - 2, axis=0)[:TR]
        o_ref[...] = (x * w_ref[pl.ds(2, 1), :] + x1 * w_ref[pl.ds(1, 1), :] + x2 * w_ref[pl.ds(0, 1), :]
                      ).astype(o_ref.dtype)

    tile = BS((None, TR, FC), lambda jj, i: (jj, i, 0))
    halo = BS((None, HR, FC), lambda jj, i: (jj, jnp.minimum((i + 1) * (TR // HR), nrb - 1), 0))
    w = BS((None, 3, FC), lambda jj, i: (jj, 0, 0))
    return _pc("conv_bwd", body, (nb, N // TR), [dup, dup, cw], [tile, halo, w], _sds((nb, N, FC), BF16), tile)


def _pack(arrs):
    parts = []
    for a in arrs:
        flat = a.reshape(-1).astype(F32)
        pad = (-flat.shape[0]) % (SUBLANES * LANES)
        parts.append(jnp.pad(flat, (0, pad)))
    return jnp.concatenate(parts).reshape(-1, LANES)


def _unpack(packed, shapes):
    flat = packed.reshape(-1)
    out, off = [], 0
    for s in shapes:
        n = math.prod(s)
        out.append(flat[off:off + n].reshape(s))
        off += n + ((-n) % (SUBLANES * LANES))
    return out


def _small_sum(gathered, gathered_bf, loss_rows, d_model):
    S, R, C = gathered.shape
    R2 = gathered_bf.shape[1]

    def body(p_ref, q_ref, tot_ref, tot2_ref, loss_ref):
        t = p_ref[0]
        u = q_ref[0].astype(F32)
        for s in range(1, S):
            t = t + p_ref[s]
            u = u + q_ref[s].astype(F32)
        tot_ref[...] = t
        tot2_ref[...] = u
        loss_ref[...] = jnp.full((1, 1), 0.5 / d_model, F32) * jnp.sum(t[:loss_rows])

    return _pc("small_sum", body, (1,), [gathered, gathered_bf],
               [BS((S, R, C), lambda i: (0, 0, 0)), BS((S, R2, C), lambda i: (0, 0, 0))],
               [_sds((R, C), F32), _sds((R2, C), F32), _sds((1, 1), F32)],
               [BS((R, C), lambda i: (0, 0)), BS((R2, C), lambda i: (0, 0)), BS((1, 1), lambda i: (0, 0))])


def _block_diag_in(bb_t, nch):
    J, G, P = bb_t.shape
    gl = G // nch
    b = bb_t.reshape(J, nch, gl, P).transpose(1, 0, 2, 3)
    eye = jnp.eye(gl, dtype=F32)
    w = eye[None, :, None, :, None] * b[:, None, :, :, :]
    return w.reshape(nch, gl * J, gl * P)


def _block_diag_in_grad(dw, J, G, P):
    nch = dw.shape[0]
    gl = G // nch
    d = dw.reshape(nch, gl, J, gl, P)
    d = jnp.einsum("cgjgp->jcgp", d)
    return d.reshape(J, G, P)


def _block_diag_out(c, nch):
    G, J, P = c.shape
    gl = G // nch
    cc = c.reshape(nch, gl, J, P).transpose(0, 1, 3, 2)
    eye = jnp.eye(gl, dtype=F32)
    w = cc[:, :, :, None, :] * eye[None, :, None, :, None]
    return w.reshape(nch, gl * P, gl * J)


def _block_diag_out_grad(dw, G, J, P):
    nch = dw.shape[0]
    gl = G // nch
    d = dw.reshape(nch, gl, P, gl, J)
    d = jnp.einsum("cgpgj->cgjp", d)
    return d.reshape(G, J, P)


def kernel(x, norm_pre_mix, w_in, ssm_lambda_re, ssm_lambda_im, ssm_log_step, ssm_b_re, ssm_b_im, ssm_c_re, ssm_c_im, ssm_d, ssm_glu_w, ssm_glu_b, pool_w, pool_b, pool_scale, w_branch_ssm, w_branch_pool, w_out, norm_post_mix, norm_pre_ffn, w_up, ffn_conv_w, ffn_conv_b, w_down, norm_post_ffn, loss_target, m_norm_pre_mix, m_w_in, m_ssm_lambda_re, m_ssm_lambda_im, m_ssm_log_step, m_ssm_b_re, m_ssm_b_im, m_ssm_c_re, m_ssm_c_im, m_ssm_d, m_ssm_glu_w, m_ssm_glu_b, m_pool_w, m_pool_b, m_pool_scale, m_w_branch_ssm, m_w_branch_pool, m_w_out, m_norm_post_mix, m_norm_pre_ffn, m_w_up, m_ffn_conv_w, m_ffn_conv_b, m_w_down, m_norm_post_ffn, v_norm_pre_mix, v_w_in, v_ssm_lambda_re, v_ssm_lambda_im, v_ssm_log_step, v_ssm_b_re, v_ssm_b_im, v_ssm_c_re, v_ssm_c_im, v_ssm_d, v_ssm_glu_w, v_ssm_glu_b, v_pool_w, v_pool_b, v_pool_scale, v_w_branch_ssm, v_w_branch_pool, v_w_out, v_norm_post_mix, v_norm_pre_ffn, v_w_up, v_ffn_conv_w, v_ffn_conv_b, v_w_down, v_norm_post_ffn):
    args = dict(locals())
    names = ["norm_pre_mix", "w_in", "ssm_lambda_re", "ssm_lambda_im", "ssm_log_step", "ssm_b_re", "ssm_b_im",
             "ssm_c_re", "ssm_c_im", "ssm_d", "ssm_glu_w", "ssm_glu_b", "pool_w", "pool_b", "pool_scale",
             "w_branch_ssm", "w_branch_pool", "w_out", "norm_post_mix", "norm_pre_ffn", "w_up", "ffn_conv_w",
             "ffn_conv_b", "w_down", "norm_post_ffn"]

    nseq, L, D = x.shape
    N = nseq * L
    U = D // NDEV
    DS = ssm_d.shape[1]
    DP = pool_scale.shape[1]
    G, P, J = ssm_b_re.shape[1:]
    SL = G * P
    CH = GROUPS_PER_CHUNK * J
    CS = GROUPS_PER_CHUNK * P
    NCH = DS // CH
    NPG = len(POOL_WINDOWS)
    PG = DP // NPG
    FC = w_up.shape[2]
    NB = NDEV
    HB = NB // 2
    F2 = NB * FC
    dev = _slot(_mesh_pos())
    tm = _tile(N, 1024)
    tm2 = _tile(N, 512)

    x2 = x.reshape(N, D)
    tgt = loss_target.reshape(N, D)

    def bf(t):
        return t.astype(BF16)

    def g_start(tag, group, after):
        return _split_start("gather_start_" + tag, _gather_copies, 3, group,
                            [_sds((NDEV,) + s.shape, s.dtype) for s in group], after)

    def g_land(tag, started, after):
        srcs, lands = _split_wait("gather_wait_" + tag, _gather_copies, started, after)
        return _split_start("d2d_start_" + tag, _d2d_copies, 4, srcs, lands, srcs[0])

    def g_finish(tag, d2d, after):
        srcs, lands = _split_wait("d2d_wait_" + tag, _d2d_copies, d2d, after)
        return [lax.dynamic_update_index_in_dim(l, s, dev, 0) for l, s in zip(lands, srcs)]

    def x_start(tag, group):
        return _split_start("exchange_start_" + tag, _exchange_copies, NDEV - 1, group,
                            [_sds(g.shape, g.dtype) for g in group], group[0])

    def x_finish(tag, started, after):
        srcs, lands = _split_wait("exchange_wait_" + tag, _exchange_copies, started, after)
        own = [lax.dynamic_index_in_dim(s, dev, 0, keepdims=False) for s in srcs]
        return [lax.dynamic_update_index_in_dim(l, o, dev, 0) for l, o in zip(lands, own)]

    st_in = g_start("in", [bf(w_in[0])], x2)
    st_mix = g_start("mix", [bf(ssm_glu_w[0]), bf(pool_w[0]), pool_b[0], ffn_conv_w[0]], st_in[4])
    conv_b_blk = ffn_conv_b.reshape(NB, 1, FC)

    (_, x2e, w_branch_ssm, w_branch_pool, w_out, w_up, w_down, ssm_lambda_re, ssm_lambda_im, ssm_log_step,
     ssm_b_re, ssm_b_im, ssm_c_re, ssm_c_im) = lax.optimization_barrier(
        (st_mix[4], x2, w_branch_ssm, w_branch_pool, w_out, w_up, w_down, ssm_lambda_re, ssm_lambda_im, ssm_log_step,
         ssm_b_re, ssm_b_im, ssm_c_re, ssm_c_im))
    lam_re, lam_im = ssm_lambda_re[0], ssm_lambda_im[0]
    log_step = ssm_log_step.reshape(G, 1)
    br_t = ssm_b_re[0].transpose(2, 0, 1)
    bi_t = ssm_b_im[0].transpose(2, 0, 1)
    pw_re3, pw_im3, pwf_re3, pwf_im3, bb_re, bb_im = _ssm_param_fwd(lam_re, lam_im, log_step, br_t, bi_t)
    pw_re, pw_im = pw_re3.reshape(SUBLANES, SL), pw_im3.reshape(SUBLANES, SL)
    pwf_re, pwf_im = pwf_re3.reshape(SUBLANES, SL), pwf_im3.reshape(SUBLANES, SL)
    WB = jnp.concatenate([_block_diag_in(bb_re, NCH), _block_diag_in(bb_im, NCH)], axis=2).astype(BF16)
    WCre = _block_diag_out(ssm_c_re[0], NCH).astype(BF16)
    WCim = _block_diag_out(-ssm_c_im[0], NCH).astype(BF16)
    a = _pre_norm(x2e, norm_pre_mix)
    small_names = ["norm_pre_mix", "norm_post_mix", "norm_pre_ffn", "norm_post_ffn", "ssm_lambda_re", "ssm_lambda_im",
                   "ssm_log_step", "ssm_b_re", "ssm_b_im", "ssm_c_re", "ssm_c_im", "ssm_d", "ssm_glu_b", "pool_scale",
                   "pool_b", "ffn_conv_w", "ffn_conv_b"]
    _, small_in = lax.optimization_barrier(
        (st_mix[4], [[args[p + n] for n in small_names] for p in ("", "m_", "v_")]))
    sm_w, sm_m, sm_v = (_pack(group) for group in small_in)
    g_br = [bf(w_branch_ssm[0]), bf(w_branch_pool[0]), bf(w_out[0])]
    g_up, g_down = [bf(w_up[0].T)], [bf(w_down[0])]
    early = [WB, WCre, WCim, pwf_re, pwf_im, a, sm_w, sm_m, sm_v] + g_br + g_up + g_down

    d_in = g_land("in", st_in, [st_mix[4]] + early)
    st_br = g_start("branch", g_br, d_in[4])
    st_up = g_start("up", g_up, st_br[4])
    st_down = g_start("down", g_down, st_up[4])
    (Win,) = g_finish("in", d_in, st_down[4])

    nq = 3 * NDEV
    (proj,) = _fused_matmul(
        "in_proj", (N // tm, nq, 1),
        [(a, BS((tm, D), lambda i, q, k: (i, 0)), Win, BS((None, D, U), lambda i, q, k: (q // 3, 0, q % 3)), "nn", 0)],
        [(tm, U)], [], [(_sds((N, 3 * D), F32), BS((tm, U), lambda i, q, k: (i, q)))],
        _store(lambda accs: accs))
    d_mix = g_land("mix", st_mix, proj)
    d_br = g_land("branch", st_br, d_mix[4])

    bu_re, bu_im = _fused_matmul(
        "ssm_in", (N // tm2, NCH, 1),
        [(proj, BS((tm2, CH), lambda i, c, k: (i, c)), WB, BS((None, CH, 2 * CS), lambda i, c, k: (c, 0, 0)), "nn", 0)],
        [(tm2, 2 * CS)], [],
        [(_sds((N, SL), F32), BS((tm2, CS), lambda i, c, k: (i, c)))] * 2,
        _store(lambda accs: (accs[0][:, :CS], accs[0][:, CS:])), deps=[d_br[4]])
    s_re, s_im = _scan_fwd(bu_re, bu_im, pw_re, pw_im, nseq, L)

    def ssm_out_epi(ids, accs, ex, o):
        u_ref, d_ref = ex
        y0 = accs[0] + d_ref[...] * u_ref[...]
        o[0][...] = y0
        o[1][...] = _gelu(y0).astype(BF16)

    y0, y1 = _fused_matmul(
        "ssm_out", (N // tm2, NCH, 1),
        [(s_re, BS((tm2, CS), lambda i, c, k: (i, c)), WCre, BS((None, CS, CH), lambda i, c, k: (c, 0, 0)), "nn", 0),
         (s_im, BS((tm2, CS), lambda i, c, k: (i, c)), WCim, BS((None, CS, CH), lambda i, c, k: (c, 0, 0)), "nn", 0)],
        [(tm2, CH)],
        [(proj, BS((tm2, CH), lambda i, c, k: (i, c))), (ssm_d, BS((1, CH), lambda i, c, k: (0, c)))],
        [(_sds((N, DS), F32), BS((tm2, CH), lambda i, c, k: (i, c))),
         (_sds((N, DS), BF16), BS((tm2, CH), lambda i, c, k: (i, c)))],
        ssm_out_epi)

    Wglu, Wpool, pool_b_all, conv_w_all = g_finish("mix", d_mix, y1)
    Wglu = Wglu.reshape(DS, DS)
    Wpool = Wpool.transpose(1, 0, 2, 3).reshape(NPG, PG, PG)
    pool_b_full = pool_b_all.transpose(1, 0, 2).reshape(1, DP)
    tn_s = _tile(DS, 512)

    def glu_epi(ids, accs, ex, o):
        y0_ref, b_ref = ex
        zg = accs[0] + b_ref[...]
        o[0][...] = zg
        o[1][...] = (_gelu(y0_ref[...]) * _sigmoid(zg)).astype(BF16)

    zg, ys = _fused_matmul(
        "ssm_glu", (N // tm, DS // tn_s, 1),
        [(y1, BS((tm, DS), lambda i, j, k: (i, 0)), Wglu, BS((DS, tn_s), lambda i, j, k: (0, j)), "nn", 0)],
        [(tm, tn_s)],
        [(y0, BS((tm, tn_s), lambda i, j, k: (i, j))), (ssm_glu_b, BS((1, tn_s), lambda i, j, k: (0, j)))],
        [(_sds((N, DS), F32), BS((tm, tn_s), lambda i, j, k: (i, j))),
         (_sds((N, DS), BF16), BS((tm, tn_s), lambda i, j, k: (i, j)))],
        glu_epi)

    z = _pool_fwd(proj, DS, DP, nseq, L)

    def pool_mm_epi(ids, accs, ex, o):
        b_ref, sc_ref = ex
        q = accs[0] + b_ref[...]
        o[0][...] = q
        o[1][...] = (q * sc_ref[...]).astype(BF16)

    qp, yp = _fused_matmul(
        "pool_mm", (N // tm, NPG, 1),
        [(z, BS((tm, PG), lambda i, g, k: (i, g)), Wpool, BS((None, PG, PG), lambda i, g, k: (g, 0, 0)), "nn", 0)],
        [(tm, PG)],
        [(pool_b_full, BS((1, PG), lambda i, g, k: (0, g))), (pool_scale, BS((1, PG), lambda i, g, k: (0, g)))],
        [(_sds((N, DP), F32), BS((tm, PG), lambda i, g, k: (i, g))),
         (_sds((N, DP), BF16), BS((tm, PG), lambda i, g, k: (i, g)))],
        pool_mm_epi)

    Wbs, Wbp, Wout = g_finish("branch", d_br, yp)
    Wout = Wout.reshape(D, D)
    gs_blk = BS((tm2, U), lambda i, q, k: (i, (DS + DP) // U + q))
    gp_blk = BS((tm2, U), lambda i, q, k: (i, (DS + DP + D) // U + q))
    out_blk = BS((tm2, U), lambda i, q, k: (i, q))

    def branch_epi(ids, accs, ex, o):
        gs_ref, gp_ref = ex
        o[0][...] = accs[0].astype(BF16)
        o[1][...] = accs[1].astype(BF16)
        o[2][...] = (_sigmoid(gs_ref[...]) * accs[0] + _sigmoid(gp_ref[...]) * accs[1]).astype(BF16)

    Ys, Yp, merged = _fused_matmul(
        "branch", (N // tm2, NDEV, 1),
        [(ys, BS((tm2, DS), lambda i, q, k: (i, 0)), Wbs, BS((None, DS, U), lambda i, q, k: (q, 0, 0)), "nn", 0),
         (yp, BS((tm2, DP), lambda i, q, k: (i, 0)), Wbp, BS((None, DP, U), lambda i, q, k: (q, 0, 0)), "nn", 1)],
        [(tm2, U), (tm2, U)],
        [(proj, gs_blk), (proj, gp_blk)],
        [(_sds((N, D), BF16), out_blk), (_sds((N, D), BF16), out_blk), (_sds((N, D), BF16), out_blk)],
        branch_epi)
    d_up = g_land("up", st_up, merged)

    tn_d = _tile(D, 512)
    (o_mix,) = _fused_matmul(
        "out_proj", (N // tm, D // tn_d, 1),
        [(merged, BS((tm, D), lambda i, j, k: (i, 0)), Wout, BS((D, tn_d), lambda i, j, k: (0, j)), "nn", 0)],
        [(tm, tn_d)], [], [(_sds((N, D), F32), BS((tm, tn_d), lambda i, j, k: (i, j)))],
        _store(lambda accs: accs), deps=[d_up[4]])
    h1, c = _mid_norm(x2, o_mix, norm_post_mix, norm_pre_ffn)

    (Wup,) = g_finish("up", d_up, c)
    tk_d = _tile(D, 1024)
    tk_up = _tile(D, 2048)
    (up_pre,) = _fused_matmul(
        "ffn_up", (N // tm2, NB, D // tk_up),
        [(c, BS((tm2, tk_up), lambda i, j, k: (i, k)), Wup, BS((None, FC, tk_up), lambda i, j, k: (j, 0, k)), "nt", 0)],
        [(tm2, FC)], [], [(_sds((NB, N, FC), F32), BS((None, tm2, FC), lambda i, j, k: (j, i, 0)))],
        _store(lambda accs: accs))
    d_down = g_land("down", st_down, up_pre)
    f = _gate_fwd(up_pre, conv_w_all, conv_b_blk, L, deps=[d_down[4]])
    (Wdown,) = g_finish("down", d_down, f)
    Wdown = Wdown.reshape(HB, FC, D)
    tn_d2 = _tile(D, 1024)
    (dn,) = _fused_matmul(
        "ffn_down", (N // tm2, D // tn_d2, HB),
        [(f, BS((None, tm2, FC), lambda i, j, k: (k, i, 0)), Wdown, BS((None, FC, tn_d2), lambda i, j, k: (k, 0, j)), "nn", 0)],
        [(tm2, tn_d2)], [], [(_sds((N, D), F32), BS((tm2, tn_d2), lambda i, j, k: (i, j)))],
        _store(lambda accs: accs))
    dh2, d_dn, lossv, dg4 = _post_ffn(h1, dn, tgt, norm_post_ffn)

    (df,) = _fused_matmul(
        "ffn_down_dx", (N // tm2, HB, D // tk_d),
        [(d_dn, BS((tm2, tk_d), lambda i, j, k: (i, k)), Wdown, BS((None, FC, tk_d), lambda i, j, k: (j, 0, k)), "nt", 0)],
        [(tm2, FC)], [], [(_sds((HB, N, FC), BF16), BS((None, tm2, FC), lambda i, j, k: (j, i, 0)))],
        _store(lambda accs: accs))
    tk_n = _tile(N, 1024)
    (gW_down,) = _fused_matmul(
        "ffn_down_dw", (HB, D // tn_d, N // tk_n),
        [(f, BS((None, tk_n, FC), lambda j, n, k: (j, k, 0)), d_dn, BS((tk_n, tn_d), lambda j, n, k: (k, n)), "tn", 0)],
        [(FC, tn_d)], [], [(_sds((HB, FC, D), BF16), BS((None, FC, tn_d), lambda j, n, k: (j, 0, n)))],
        _store(lambda accs: accs))
    x_down = x_start("down", [gW_down.reshape(NDEV, FC // 2, D)])
    dup, dcw, dcb = _gate_bwd(up_pre, conv_w_all, conv_b_blk, df, L, deps=[x_down[4]])
    dpre = _conv_bwd(dup.reshape(NB, N, FC), conv_w_all, L)
    (dc,) = _fused_matmul(
        "ffn_up_dx", (N // tm2, D // tn_d2, NB),
        [(dpre, BS((None, tm2, FC), lambda i, j, k: (k, i, 0)), Wup, BS((None, FC, tn_d2), lambda i, j, k: (k, 0, j)), "nn", 0)],
        [(tm2, tn_d2)], [], [(_sds((N, D), F32), BS((tm2, tn_d2), lambda i, j, k: (i, j)))],
        _store(lambda accs: accs))
    tm_d = _tile(D, 512)
    (gW_up,) = _fused_matmul(
        "ffn_up_dw", (NB, D // tm_d, N // tk_n),
        [(dpre, BS((None, tk_n, FC), lambda j, n, k: (j, k, 0)), c, BS((tk_n, tm_d), lambda j, n, k: (k, n)), "tn", 0)],
        [(FC, tm_d)], [], [(_sds((NB, FC, D), BF16), BS((None, FC, tm_d), lambda j, n, k: (j, 0, n)))],
        _store(lambda accs: accs))
    x_up = x_start("up", [gW_up])

    dh1, d_o, dg2, dg3 = _mid_bwd(dh2, dc, h1, o_mix, norm_post_mix, norm_pre_ffn, deps=[x_up[4]])

    def dmerged_epi(ids, accs, ex, o):
        gs_ref, gp_ref, ys_ref, yp_ref = ex
        dm = accs[0]
        sg_s, sg_p = _sigmoid(gs_ref[...]), _sigmoid(gp_ref[...])
        o[0][...] = (dm * sg_s).astype(BF16)
        o[1][...] = (dm * sg_p).astype(BF16)
        o[2][...] = (dm * ys_ref[...].astype(F32) * sg_s * (1.0 - sg_s)).astype(BF16)
        o[3][...] = (dm * yp_ref[...].astype(F32) * sg_p * (1.0 - sg_p)).astype(BF16)

    dYs, dYp, dgs, dgp = _fused_matmul(
        "out_proj_dx", (N // tm2, NDEV, 1),
        [(d_o, BS((tm2, D), lambda i, q, k: (i, 0)), Wout, BS((U, D), lambda i, q, k: (q, 0)), "nt", 0)],
        [(tm2, U)],
        [(proj, gs_blk), (proj, gp_blk), (Ys, out_blk), (Yp, out_blk)],
        [(_sds((N, D), BF16), out_blk)] * 4,
        dmerged_epi)
    (gW_out,) = _fused_matmul(
        "out_proj_dw", (D // tm_d, D // tn_d, 1),
        [(merged, BS((N, tm_d), lambda i, j, k: (0, i)), d_o, BS((N, tn_d), lambda i, j, k: (0, j)), "tn", 0)],
        [(tm_d, tn_d)], [], [(_sds((D, D), BF16), BS((tm_d, tn_d), lambda i, j, k: (i, j)))],
        _store(lambda accs: accs))
    tm_s = _tile(DS, 512)
    gW_bs, gW_bp = _fused_matmul(
        "branch_dw", (DS // tm_s, NDEV, 1),
        [(ys, BS((N, tm_s), lambda i, q, k: (0, i)), dYs, BS((N, U), lambda i, q, k: (0, q)), "tn", 0),
         (yp, BS((N, tm_s), lambda i, q, k: (0, i)), dYp, BS((N, U), lambda i, q, k: (0, q)), "tn", 1)],
        [(tm_s, U), (tm_s, U)], [],
        [(_sds((NDEV, DS, U), BF16), BS((None, tm_s, U), lambda i, q, k: (q, i, 0)))] * 2,
        _store(lambda accs: accs))
    x_br = x_start("branch", [gW_bs, gW_bp, gW_out.reshape(NDEV, U, D)])

    tn_p = _tile(PG, 512)

    def dyp_epi(ids, accs, ex, o):
        q_ref, sc_ref = ex
        first = ids[1] == 0
        dyp = accs[0]
        dq = dyp * sc_ref[...]
        o[0][...] = dq.astype(BF16)
        _rowsum_into(o[1], first, dyp * q_ref[...])
        _rowsum_into(o[2], first, dq)

    dq, d_pscale, d_pb = _fused_matmul(
        "branch_pool_dx", (DP // tn_p, N // tm, 1),
        [(dYp, BS((tm, D), lambda j, i, k: (i, 0)), Wbp, BS((NDEV, tn_p, U), lambda j, i, k: (0, j, 0)), "nt_cat", 0)],
        [(tm, tn_p)],
        [(qp, BS((tm, tn_p), lambda j, i, k: (i, j))), (pool_scale, BS((1, tn_p), lambda j, i, k: (0, j)))],
        [(_sds((N, DP), BF16), BS((tm, tn_p), lambda j, i, k: (i, j))),
         (_sds((1, DP), F32), BS((1, tn_p), lambda j, i, k: (0, j))),
         (_sds((1, DP), F32), BS((1, tn_p), lambda j, i, k: (0, j)))],
        dyp_epi, deps=[x_br[4]])
    (dz,) = _fused_matmul(
        "pool_mm_dx", (N // tm, NPG, 1),
        [(dq, BS((tm, PG), lambda i, g, k: (i, g)), Wpool, BS((None, PG, PG), lambda i, g, k: (g, 0, 0)), "nt", 0)],
        [(tm, PG)], [], [(_sds((N, DP), F32), BS((tm, PG), lambda i, g, k: (i, g)))],
        _store(lambda accs: accs))
    (gW_pool,) = _fused_matmul(
        "pool_mm_dw", (NPG, 1),
        [(z, BS((N, PG), lambda g, k: (0, g)), dq, BS((N, PG), lambda g, k: (0, g)), "tn", 0)],
        [(PG, PG)], [], [(_sds((NPG, PG, PG), BF16), BS((None, PG, PG), lambda g, k: (g, 0, 0)))],
        _store(lambda accs: accs))
    du_pool = _pool_bwd(dz, nseq, L)

    def dys_epi(ids, accs, ex, o):
        zg_ref, y0_ref = ex
        first = ids[1] == 0
        dys = accs[0]
        sg = _sigmoid(zg_ref[...])
        dzg = dys * _gelu(y0_ref[...]) * sg * (1.0 - sg)
        o[0][...] = dzg.astype(BF16)
        o[1][...] = dys * sg
        _rowsum_into(o[2], first, dzg)

    dzg, dy1_direct, d_glu_b = _fused_matmul(
        "branch_ssm_dx", (DS // tn_s, N // tm, 1),
        [(dYs, BS((tm, D), lambda j, i, k: (i, 0)), Wbs, BS((NDEV, tn_s, U), lambda j, i, k: (0, j, 0)), "nt_cat", 0)],
        [(tm, tn_s)],
        [(zg, BS((tm, tn_s), lambda j, i, k: (i, j))), (y0, BS((tm, tn_s), lambda j, i, k: (i, j)))],
        [(_sds((N, DS), BF16), BS((tm, tn_s), lambda j, i, k: (i, j))),
         (_sds((N, DS), F32), BS((tm, tn_s), lambda j, i, k: (i, j))),
         (_sds((1, DS), F32), BS((1, tn_s), lambda j, i, k: (0, j)))],
        dys_epi)
    (gW_glu,) = _fused_matmul(
        "ssm_glu_dw", (DS // tm_s, DS // tn_s, 1),
        [(y1, BS((N, tm_s), lambda i, j, k: (0, i)), dzg, BS((N, tn_s), lambda i, j, k: (0, j)), "tn", 0)],
        [(tm_s, tn_s)], [], [(_sds((DS, DS), BF16), BS((tm_s, tn_s), lambda i, j, k: (i, j)))],
        _store(lambda accs: accs))
    x_mix = x_start("mix", [gW_glu.reshape(NDEV, DS // NDEV, DS),
                            gW_pool.reshape(NPG, NDEV, PG // NDEV, PG).transpose(1, 0, 2, 3)])

    tn_c = _tile(DS, CH)

    def dy0_epi(ids, accs, ex, o):
        d1_ref, y0_ref, u_ref = ex
        first = ids[1] == 0
        dy0 = (accs[0] + d1_ref[...]) * _gelu_grad(y0_ref[...])
        o[0][...] = dy0
        _rowsum_into(o[1], first, dy0 * u_ref[...])

    dy0, d_ssm_d = _fused_matmul(
        "ssm_glu_dx", (DS // tn_c, N // tm, 1),
        [(dzg, BS((tm, DS), lambda j, i, k: (i, 0)), Wglu, BS((tn_c, DS), lambda j, i, k: (j, 0)), "nt", 0)],
        [(tm, tn_c)],
        [(dy1_direct, BS((tm, tn_c), lambda j, i, k: (i, j))), (y0, BS((tm, tn_c), lambda j, i, k: (i, j))),
         (proj, BS((tm, tn_c), lambda j, i, k: (i, j)))],
        [(_sds((N, DS), F32), BS((tm, tn_c), lambda j, i, k: (i, j))),
         (_sds((1, DS), F32), BS((1, tn_c), lambda j, i, k: (0, j)))],
        dy0_epi, deps=[x_mix[4]])

    ds_re, ds_im = _fused_matmul(
        "ssm_out_dx", (N // tm2, NCH, 1),
        [(dy0, BS((tm2, CH), lambda i, c, k: (i, c)), WCre, BS((None, CS, CH), lambda i, c, k: (c, 0, 0)), "nt", 0),
         (dy0, BS((tm2, CH), lambda i, c, k: (i, c)), WCim, BS((None, CS, CH), lambda i, c, k: (c, 0, 0)), "nt", 1)],
        [(tm2, CS), (tm2, CS)], [],
        [(_sds((N, SL), F32), BS((tm2, CS), lambda i, c, k: (i, c)))] * 2,
        _store(lambda accs: accs))
    lam_r, lam_i, d_ab_re, d_ab_im = _scan_bwd(ds_re, ds_im, s_re, s_im, pw_re, pw_im, pwf_re, pwf_im, nseq, L)

    def du_epi(ids, accs, ex, o):
        dy0_ref, d_ref = ex
        o[0][...] = (accs[0] + dy0_ref[...] * d_ref[...]).astype(BF16)

    (du_ssm,) = _fused_matmul(
        "ssm_in_dx", (N // tm2, NCH, 1),
        [(lam_r, BS((tm2, CS), lambda i, c, k: (i, c)), WB, BS((None, CH, CS), lambda i, c, k: (c, 0, 0)), "nt", 0),
         (lam_i, BS((tm2, CS), lambda i, c, k: (i, c)), WB, BS((None, CH, CS), lambda i, c, k: (c, 0, 1)), "nt", 0)],
        [(tm2, CH)],
        [(dy0, BS((tm2, CH), lambda i, c, k: (i, c))), (ssm_d, BS((1, CH), lambda i, c, k: (0, c)))],
        [(_sds((N, DS), BF16), BS((tm2, CH), lambda i, c, k: (i, c)))],
        du_epi)
    dproj = jnp.concatenate([du_ssm, du_pool, dgs, dgp], axis=1)
    (gW_in,) = _fused_matmul(
        "in_proj_dw", (D // tm_d, nq, 1),
        [(a, BS((N, tm_d), lambda i, q, k: (0, i)), dproj, BS((N, U), lambda i, q, k: (0, q)), "tn", 0)],
        [(tm_d, U)], [], [(_sds((NDEV, D, 3 * U), BF16), BS((None, tm_d, U), lambda i, q, k: (q // 3, i, q % 3)))],
        _store(lambda accs: accs))
    x_in = x_start("in", [gW_in])
    (da,) = _fused_matmul(
        "in_proj_dx", (N // tm, D // tn_d2, NDEV),
        [(dproj, BS((tm, 3 * U), lambda i, j, k: (i, k)), Win, BS((None, tn_d2, 3 * U), lambda i, j, k: (k, j, 0)), "nt", 0)],
        [(tm, tn_d2)], [], [(_sds((N, D), F32), BS((tm, tn_d2), lambda i, j, k: (i, j)))],
        _store(lambda accs: accs), deps=[x_in[4]])
    grad_x, dg1 = _pre_bwd(x2, da, dh1, norm_pre_mix)

    dWCre, dWCim = _fused_matmul(
        "ssm_out_dw", (NCH, N // tk_n),
        [(s_re, BS((tk_n, CS), lambda c, k: (k, c)), dy0, BS((tk_n, CH), lambda c, k: (k, c)), "tn", 0),
         (s_im, BS((tk_n, CS), lambda c, k: (k, c)), dy0, BS((tk_n, CH), lambda c, k: (k, c)), "tn", 1)],
        [(CS, CH), (CS, CH)], [],
        [(_sds((NCH, CS, CH), F32), BS((None, CS, CH), lambda c, k: (c, 0, 0)))] * 2,
        _store(lambda accs: accs), deps=[x_in[4]])
    dWBre, dWBim = _fused_matmul(
        "ssm_in_dw", (NCH, N // tk_n),
        [(proj, BS((tk_n, CH), lambda c, k: (k, c)), lam_r, BS((tk_n, CS), lambda c, k: (k, c)), "tn", 0),
         (proj, BS((tk_n, CH), lambda c, k: (k, c)), lam_i, BS((tk_n, CS), lambda c, k: (k, c)), "tn", 1)],
        [(CH, CS), (CH, CS)], [],
        [(_sds((NCH, CH, CS), F32), BS((None, CH, CS), lambda c, k: (c, 0, 0)))] * 2,
        _store(lambda accs: accs), deps=[x_in[4]])
    d_bbr = _block_diag_in_grad(dWBre, J, G, P)
    d_bbi = _block_diag_in_grad(dWBim, J, G, P)
    d_lam_re, d_lam_im, d_log_step, d_br_t, d_bi_t = _ssm_param_bwd(
        lam_re, lam_im, log_step, br_t, bi_t,
        d_ab_re.reshape(nseq * SUBLANES, G, P), d_ab_im.reshape(nseq * SUBLANES, G, P), d_bbr, d_bbi)
    d_c_re = _block_diag_out_grad(dWCre, G, J, P)
    d_c_im = -_block_diag_out_grad(dWCim, G, J, P)

    d_conv_w = dcw.reshape(NB, 3, FC).transpose(1, 0, 2).reshape(3, F2)
    d_conv_b = dcb.reshape(1, F2)
    small = {
        "norm_pre_mix": dg1, "norm_post_mix": dg2, "norm_pre_ffn": dg3, "norm_post_ffn": dg4,
        "ssm_lambda_re": d_lam_re[None], "ssm_lambda_im": d_lam_im[None], "ssm_log_step": d_log_step.reshape(1, G),
        "ssm_b_re": d_br_t.transpose(1, 2, 0)[None], "ssm_b_im": d_bi_t.transpose(1, 2, 0)[None],
        "ssm_c_re": d_c_re[None], "ssm_c_im": d_c_im[None],
        "ssm_d": d_ssm_d, "ssm_glu_b": d_glu_b, "pool_scale": d_pscale,
        "pool_b": d_pb.reshape(1, NPG, PG), "ffn_conv_w": d_conv_w[None], "ffn_conv_b": d_conv_b,
    }
    assert list(small) == small_names
    wide = ["ssm_b_re", "ssm_b_im", "ssm_c_re", "ssm_c_im"]
    narrow = [n for n in small_names if n not in wide]
    packed = _pack([lossv] + [small[n] for n in narrow])
    packed_bf = _pack([small[n] for n in wide]).astype(BF16)
    st_small = _split_start("small_start", _broadcast_copies, NDEV - 1, [packed, packed_bf],
                            [_sds((NDEV,) + p.shape, p.dtype) for p in (packed, packed_bf)], packed)

    res = {}
    after = st_small[4]
    for tag, started, group in (("down", x_down, ["w_down"]), ("up", x_up, ["w_up"]),
                                ("branch", x_br, ["w_branch_ssm", "w_branch_pool", "w_out"]),
                                ("mix", x_mix, ["ssm_glu_w", "pool_w"]), ("in", x_in, ["w_in"])):
        for n, parts in zip(group, x_finish(tag, started, after)):
            shape = args[n].shape
            if n == "w_up":
                flat, back = (lambda t: t[0].T), (lambda t: t.T[None])
            else:
                flat, back = (lambda t: t.reshape(-1, shape[-1])), (lambda t: t.reshape(shape))
            w2 = flat(args[n])
            g, dl, nm, nv = _adamw("adamw_" + n, w2, flat(args["m_" + n]), flat(args["v_" + n]),
                                   parts.reshape((NDEV,) + w2.shape))
            res[n] = tuple(back(t) for t in (g, dl, nm, nv))
            after = g

    srcs, lands = _split_wait("small_wait", _broadcast_copies, st_small, after)
    small_all, small_all_bf = (lax.dynamic_update_index_in_dim(l, s, dev, 0) for l, s in zip(lands, srcs))
    loss_rows = (D + SUBLANES * LANES - 1) // (SUBLANES * LANES) * SUBLANES
    total, total_wide, loss = _small_sum(small_all, small_all_bf, loss_rows, D)
    totals = dict(zip(narrow, _unpack(total, [lossv.shape] + [small[n].shape for n in narrow])[1:]))
    totals.update(zip(wide, _unpack(total_wide, [small[n].shape for n in wide])))
    totals["pool_b"] = lax.dynamic_slice_in_dim(totals["pool_b"], dev * (PG // NDEV), PG // NDEV, axis=2)
    totals["ffn_conv_w"] = lax.dynamic_slice_in_dim(totals["ffn_conv_w"], dev * FC, FC, axis=2)
    sm_g = _pack([totals[n] for n in small_names])
    _, sm_d, sm_nm, sm_nv = _adamw("adamw_small", sm_w, sm_m, sm_v, sm_g[None])
    shapes = [args[n].shape for n in small_names]
    for n, dl, nm, nv in zip(small_names, _unpack(sm_d, shapes), _unpack(sm_nm, shapes), _unpack(sm_nv, shapes)):
        res[n] = (totals[n], dl, nm, nv)

    outs = [loss.reshape(()), grad_x.reshape(x.shape)]
    for k in range(4):
        outs += [res[n][k] for n in names]
    return tuple(outs)
```

```python
import functools
import math

import jax
import jax.numpy as jnp
from jax import lax
from jax.experimental import pallas as pl
from jax.experimental.pallas import tpu as pltpu

F32 = jnp.float32
BF16 = jnp.bfloat16
BS = pl.BlockSpec

NDEV = 8
SSM_GROUP = 16
SSM_STATE = 64
GROUPS_PER_CHUNK = 16
SCAN_UNROLL = 8
POOL_WINDOWS = (2, 4, 8, 16)
EPS = 1e-6
MIN_NEG_REAL = -1e-4
ADAM_LR, ADAM_B1, ADAM_B2, ADAM_EPS, ADAM_WD, ADAM_STEP = 0.001, 0.9, 0.999, 1e-08, 0.01, 10
LANES = 128
SUBLANES = 8
VMEM_LIMIT = 56 * 1024 * 1024

_DIMS = {"nn": (((1,), (0,)), ((), ())), "nt": (((1,), (1,)), ((), ())), "tn": (((0,), (0,)), ((), ()))}


def _tile(dim, pref, mult=LANES):
    if dim <= pref:
        return dim
    t = (pref // mult) * mult
    while t >= mult:
        if dim % t == 0:
            return t
        t -= mult
    return dim


def _pc(name, body, grid, ins, in_specs, outs, out_specs, scratch=(), deps=()):
    multi = isinstance(outs, (list, tuple))
    if deps:
        n_in, n_dep, inner = len(ins), len(deps), body

        def body(*refs):
            return inner(*refs[:n_in], *refs[n_in + n_dep:])

        ins = list(ins) + list(deps)
        in_specs = list(in_specs) + [BS(memory_space=pl.ANY)] * n_dep
    return pl.pallas_call(
        body, name=name, grid=grid, in_specs=list(in_specs),
        out_specs=list(out_specs) if multi else out_specs,
        out_shape=list(outs) if multi else outs, scratch_shapes=list(scratch),
        compiler_params=pltpu.CompilerParams(dimension_semantics=("arbitrary",) * len(grid),
                                             vmem_limit_bytes=VMEM_LIMIT),
    )(*ins)


def _sds(shape, dtype):
    return jax.ShapeDtypeStruct(tuple(shape), dtype)


def _gelu(x):
    k = math.sqrt(2.0 / math.pi)
    return 0.5 * x * (1.0 + jnp.tanh(k * (x + 0.044715 * (x * x * x))))


def _gelu_grad(x):
    k = math.sqrt(2.0 / math.pi)
    t = jnp.tanh(k * (x + 0.044715 * (x * x * x)))
    return 0.5 * (1.0 + t) + 0.5 * x * (1.0 - t * t) * (k * (1.0 + 3.0 * 0.044715 * x * x))


def _sigmoid(x):
    return jax.nn.sigmoid(x)


def _fused_matmul(name, grid, pairs, acc_shapes, extras, outs, epilogue, deps=()):
    n_p, n_e, n_o = len(pairs), len(extras), len(outs)
    rank = len(grid)
    nk = grid[-1]
    if nk == 1:
        acc_shapes = []

    def body(*refs):
        ab = refs[:2 * n_p]
        ex = refs[2 * n_p:2 * n_p + n_e]
        o = refs[2 * n_p + n_e:2 * n_p + n_e + n_o]
        accs = refs[2 * n_p + n_e + n_o:]
        ids = [pl.program_id(d) for d in range(rank)]
        k = ids[-1]

        def products():
            sums = {}
            for p in range(n_p):
                a = ab[2 * p][...].astype(BF16)
                mode = pairs[p][4]
                if mode == "nt_cat":
                    b_ref = ab[2 * p + 1]
                    b = jnp.concatenate([b_ref[q].astype(BF16) for q in range(b_ref.shape[0])], axis=1)
                    mode = "nt"
                else:
                    b = ab[2 * p + 1][...].astype(BF16)
                d = lax.dot_general(a, b, _DIMS[mode], preferred_element_type=F32)
                sums[pairs[p][5]] = d if pairs[p][5] not in sums else sums[pairs[p][5]] + d
            return [sums[i] for i in range(len(sums))]

        if nk == 1:
            epilogue(ids, products(), ex, o)
            return

        @pl.when(k == 0)
        def _():
            for acc, s in zip(accs, products()):
                acc[...] = s

        if nk > 2:
            @pl.when(jnp.logical_and(k > 0, k < nk - 1))
            def _():
                for acc, s in zip(accs, products()):
                    acc[...] += s

        @pl.when(k == nk - 1)
        def _():
            epilogue(ids, [acc[...] + s for acc, s in zip(accs, products())], ex, o)

    ins, in_specs = [], []
    for a, a_spec, b, b_spec, _, _ in pairs:
        ins += [a, b]
        in_specs += [a_spec, b_spec]
    for e, e_spec in extras:
        ins.append(e)
        in_specs.append(e_spec)
    res = _pc(name, body, grid, ins, in_specs, [s for s, _ in outs], [sp for _, sp in outs],
              scratch=[pltpu.VMEM(tuple(s), F32) for s in acc_shapes], deps=deps)
    return res


def _store(vals):
    def epilogue(ids, accs, ex, o):
        for r, v in zip(o, vals(accs)):
            r[...] = v.astype(r.dtype)
    return epilogue


def _rowsum_into(ref, first, v):
    s = jnp.sum(v, axis=0, keepdims=True)

    @pl.when(first)
    def _():
        ref[...] = s

    @pl.when(jnp.logical_not(first))
    def _():
        ref[...] += s


def _mesh_pos():
    return lax.axis_index("x"), lax.axis_index("y"), lax.axis_index("c")


def _slot(p):
    return 4 * p[0] + 2 * p[1] + p[2]


_HBM = BS(memory_space=pltpu.HBM)
_SEM = BS(memory_space=pltpu.SEMAPHORE)
_ANY = BS(memory_space=pl.ANY)
_EFFECT = pltpu.SideEffectType.DATAFLOW_SIDE_EFFECTING


def _other_chips(x, y):
    return [(1 - x, y), (x, 1 - y), (1 - x, 1 - y)]


def _all_peers(x, y, c):
    peers = []
    for k in range(1, NDEV):
        kx, ky, kc = (k >> 2) & 1, (k >> 1) & 1, k & 1
        peers.append((1 - x if kx else x, 1 - y if ky else y, 1 - c if kc else c))
    return peers


def _gather_copies(src, land, send_sems, recv_sems, base):
    x, y, c = _mesh_pos()
    return [pltpu.make_async_remote_copy(
        src_ref=src, dst_ref=land.at[_slot((x, y, c))],
        send_sem=send_sems.at[base + k], recv_sem=recv_sems.at[base + k],
        device_id=(*chip, c), device_id_type=pl.DeviceIdType.MESH) for k, chip in enumerate(_other_chips(x, y))]


def _d2d_copies(src, land, send_sems, recv_sems, base):
    x, y, c = _mesh_pos()
    blocks = [(x, y, c)] + [(*chip, c) for chip in _other_chips(x, y)]
    return [pltpu.make_async_remote_copy(
        src_ref=src if k == 0 else land.at[_slot(b)], dst_ref=land.at[_slot(b)],
        send_sem=send_sems.at[base + k], recv_sem=recv_sems.at[base + k],
        device_id=(x, y, 1 - c), device_id_type=pl.DeviceIdType.MESH) for k, b in enumerate(blocks)]


def _broadcast_copies(src, land, send_sems, recv_sems, base):
    x, y, c = _mesh_pos()
    return [pltpu.make_async_remote_copy(
        src_ref=src, dst_ref=land.at[_slot((x, y, c))],
        send_sem=send_sems.at[base + k], recv_sem=recv_sems.at[base + k],
        device_id=peer, device_id_type=pl.DeviceIdType.MESH) for k, peer in enumerate(_all_peers(x, y, c))]


def _exchange_copies(src, land, send_sems, recv_sems, base):
    x, y, c = _mesh_pos()
    return [pltpu.make_async_remote_copy(
        src_ref=src.at[_slot(peer)], dst_ref=land.at[_slot((x, y, c))],
        send_sem=send_sems.at[base + k], recv_sem=recv_sems.at[base + k],
        device_id=peer, device_id_type=pl.DeviceIdType.MESH) for k, peer in enumerate(_all_peers(x, y, c))]


def _split_start(name, copies, ncopy, srcs, land_shapes, after):
    n = len(srcs)

    def body(*refs):
        src_refs, land_refs = refs[:n], refs[n:2 * n]
        send_sems, recv_sems = refs[2 * n + 1], refs[2 * n + 2]
        token = refs[-1]
        for r in range(n):
            for cp in copies(src_refs[r], land_refs[r], send_sems, recv_sems, r * ncopy):
                cp.start()
        token[...] = jnp.zeros_like(token)

    lands = [s if isinstance(s, jax.Array) else pltpu.with_memory_space_constraint(lax.empty(s.shape, s.dtype), pltpu.HBM)
             for s in land_shapes]
    ins = list(srcs) + lands
    out_shape = ([pltpu.SemaphoreType.DMA((n * ncopy,)), pltpu.SemaphoreType.DMA((n * ncopy,))]
                 + [pltpu.HBM(a.shape, a.dtype) for a in lands]
                 + [_sds((SUBLANES, LANES), F32)])
    res = pl.pallas_call(
        body, name=name, out_shape=out_shape,
        in_specs=[_HBM] * (2 * n) + [_ANY], out_specs=[_SEM, _SEM] + [_HBM] * n + [BS(memory_space=pltpu.VMEM)],
        input_output_aliases={n + i: 2 + i for i in range(n)},
        compiler_params=pltpu.CompilerParams(has_side_effects=_EFFECT),
    )(*ins, after)
    return res[0], res[1], list(srcs), list(res[2:2 + n]), res[-1]


def _split_wait(name, copies, started, after):
    send_sems, recv_sems, srcs, lands, _ = started
    n = len(srcs)
    ncopy = send_sems.shape[0] // n
    after = list(after) if isinstance(after, (list, tuple)) else [after]

    def body(*refs):
        src_refs, land_refs = refs[:n], refs[n:2 * n]
        send_sems, recv_sems = refs[2 * n], refs[2 * n + 1]
        for r in range(n):
            for cp in copies(src_refs[r], land_refs[r], send_sems, recv_sems, r * ncopy):
                cp.wait_send()
                cp.wait_recv()

    res = pl.pallas_call(
        body, name=name, out_shape=[pltpu.HBM(a.shape, a.dtype) for a in lands],
        in_specs=[_HBM] * (2 * n) + [_SEM, _SEM] + [_ANY] * len(after), out_specs=[_HBM] * n,
        input_output_aliases={n + i: i for i in range(n)},
        compiler_params=pltpu.CompilerParams(has_side_effects=_EFFECT),
    )(*srcs, *lands, send_sems, recv_sems, *after)
    return list(srcs), list(res)


def _adamw(name, w, m, v, parts):
    R, C = w.shape
    S = parts.shape[0]
    tr = _tile(R, max(SUBLANES, (256 * 1024) // C), SUBLANES)

    def body(w_ref, m_ref, v_ref, p_ref, g_ref, d_ref, nm_ref, nv_ref):
        g = p_ref[0].astype(F32)
        for s in range(1, S):
            g = g + p_ref[s].astype(F32)
        m2 = ADAM_B1 * m_ref[...] + (1.0 - ADAM_B1) * g
        v2 = ADAM_B2 * v_ref[...] + (1.0 - ADAM_B2) * (g * g)
        m_hat = m2 / (1.0 - ADAM_B1 ** ADAM_STEP)
        v_hat = v2 / (1.0 - ADAM_B2 ** ADAM_STEP)
        g_ref[...] = g
        d_ref[...] = -ADAM_LR * (m_hat / (jnp.sqrt(v_hat) + ADAM_EPS) + ADAM_WD * w_ref[...])
        nm_ref[...] = m2
        nv_ref[...] = v2

    blk = BS((tr, C), lambda i: (i, 0))
    return _pc(name, body, (R // tr,), [w, m, v, parts],
               [blk, blk, blk, BS((S, tr, C), lambda i: (0, i, 0))],
               [_sds((R, C), F32)] * 4, [blk] * 4)


def _ssm_disc(lam_re, lam_im, log_step, br_t, bi_t):
    lr = jnp.minimum(lam_re, MIN_NEG_REAL)
    li = lam_im
    dt = jnp.exp(log_step)
    mag = jnp.exp(lr * dt)
    ang = li * dt
    ab_re = mag * jnp.cos(ang)
    ab_im = mag * jnp.sin(ang)
    nr = ab_re - 1.0
    ni = ab_im
    den = lr * lr + li * li
    f_re = (nr * lr + ni * li) / den
    f_im = (ni * lr - nr * li) / den
    bb_re = f_re[None] * br_t - f_im[None] * bi_t
    bb_im = f_re[None] * bi_t + f_im[None] * br_t
    return ab_re, ab_im, bb_re, bb_im


def _ssm_param_fwd(lam_re, lam_im, log_step, br_t, bi_t):
    G, P = lam_re.shape

    def body(lr_ref, li_ref, ls_ref, br_ref, bi_ref, pw_re_ref, pw_im_ref, pwf_re_ref, pwf_im_ref, bbr_ref, bbi_ref):
        ab_re, ab_im, bb_re, bb_im = _ssm_disc(lr_ref[...], li_ref[...], ls_ref[...], br_ref[...], bi_ref[...])
        bbr_ref[...] = bb_re
        bbi_ref[...] = bb_im
        pr, pi = ab_re, ab_im
        for r in range(SUBLANES):
            pw_re_ref[r] = pr
            pw_im_ref[r] = pi
            pwf_re_ref[SUBLANES - 1 - r] = pr
            pwf_im_ref[SUBLANES - 1 - r] = pi
            pr, pi = pr * ab_re - pi * ab_im, pr * ab_im + pi * ab_re

    full = lambda a: BS(a.shape, lambda i: (0,) * a.ndim)
    ins = [lam_re, lam_im, log_step, br_t, bi_t]
    outs = [_sds((SUBLANES, G, P), F32)] * 4 + [_sds(br_t.shape, F32)] * 2
    return _pc("ssm_param_fwd", body, (1,), ins, [full(a) for a in ins], outs, [full(o) for o in outs])


def _ssm_param_bwd(lam_re, lam_im, log_step, br_t, bi_t, d_ab_re, d_ab_im, d_bbr, d_bbi):
    def body(lr_ref, li_ref, ls_ref, br_ref, bi_ref, dar_ref, dai_ref, dbr_ref, dbi_ref,
             o_lr, o_li, o_ls, o_br, o_bi):
        prim = (lr_ref[...], li_ref[...], ls_ref[...], br_ref[...], bi_ref[...])
        _, vjp = jax.vjp(_ssm_disc, *prim)
        dar = dar_ref[0]
        dai = dai_ref[0]
        for k in range(1, dar_ref.shape[0]):
            dar = dar + dar_ref[k]
            dai = dai + dai_ref[k]
        g = vjp((dar, dai, dbr_ref[...], dbi_ref[...]))
        for r, v in zip((o_lr, o_li, o_ls, o_br, o_bi), g):
            r[...] = v

    full = lambda a: BS(a.shape, lambda i: (0,) * a.ndim)
    ins = [lam_re, lam_im, log_step, br_t, bi_t, d_ab_re, d_ab_im, d_bbr, d_bbi]
    outs = [_sds(a.shape, F32) for a in (lam_re, lam_im, log_step, br_t, bi_t)]
    return _pc("ssm_param_bwd", body, (1,), ins, [full(a) for a in ins], outs, [full(o) for o in outs])


def _bcast_row(ref, r, w):
    return jnp.broadcast_to(ref[pl.ds(r, 1), :], (SUBLANES, w))


def _pick_row(x, row, r):
    return jnp.broadcast_to(jnp.sum(jnp.where(row == r, x, 0.0), axis=0, keepdims=True), x.shape)


def _scan_fwd(bu_re, bu_im, pw_re, pw_im, nseq, L):
    N, SL = bu_re.shape
    W = _tile(SL, 256)
    unroll = math.gcd(L // SUBLANES, SCAN_UNROLL)

    def body(bre_ref, bim_ref, pre_ref, pim_ref, sre_ref, sim_ref):
        pre, pim = pre_ref[...], pim_ref[...]
        steps = [(k, _bcast_row(pre_ref, k - 1, W), _bcast_row(pim_ref, k - 1, W)) for k in (1, 2, 4)]
        row = lax.broadcasted_iota(jnp.int32, (SUBLANES, W), 0)

        def step(i, carry):
            cr, ci = carry
            r0 = pl.multiple_of(i * SUBLANES, SUBLANES)
            xr = bre_ref[pl.ds(r0, SUBLANES), :]
            xi = bim_ref[pl.ds(r0, SUBLANES), :]
            for k, ar, ai in steps:
                sr = pltpu.roll(xr, k, axis=0)
                si = pltpu.roll(xi, k, axis=0)
                keep = row >= k
                xr, xi = (xr + jnp.where(keep, ar * sr - ai * si, 0.0),
                          xi + jnp.where(keep, ar * si + ai * sr, 0.0))
            xr, xi = xr + (pre * cr - pim * ci), xi + (pre * ci + pim * cr)
            sre_ref[pl.ds(r0, SUBLANES), :] = xr
            sim_ref[pl.ds(r0, SUBLANES), :] = xi
            return _pick_row(xr, row, SUBLANES - 1), _pick_row(xi, row, SUBLANES - 1)

        def group(g, carry):
            for u in range(unroll):
                carry = step(g * unroll + u, carry)
            return carry

        zero = jnp.zeros((SUBLANES, W), F32)
        lax.fori_loop(0, L // SUBLANES // unroll, group, (zero, zero))

    blk = BS((L, W), lambda s, j: (s, j))
    pw = BS((SUBLANES, W), lambda s, j: (0, j))
    return _pc("ssm_scan_fwd", body, (nseq, SL // W), [bu_re, bu_im, pw_re, pw_im], [blk, blk, pw, pw],
               [_sds((N, SL), F32)] * 2, [blk, blk])


def _scan_bwd(ds_re, ds_im, s_re, s_im, pw_re, pw_im, pwf_re, pwf_im, nseq, L):
    N, SL = ds_re.shape
    W = _tile(SL, 256)
    nt = L // SUBLANES
    unroll = math.gcd(nt, SCAN_UNROLL)

    def body(dsr_ref, dsi_ref, sre_ref, sim_ref, pre_ref, pim_ref, fre_ref, fim_ref,
             lre_ref, lim_ref, dar_ref, dai_ref):
        fre, fim = fre_ref[...], -fim_ref[...]
        steps = [(k, _bcast_row(pre_ref, k - 1, W), -_bcast_row(pim_ref, k - 1, W)) for k in (1, 2, 4)]
        row = lax.broadcasted_iota(jnp.int32, (SUBLANES, W), 0)

        def step(ii, carry):
            cr, ci, acr, aci = carry
            i = nt - 1 - ii
            r0 = pl.multiple_of(i * SUBLANES, SUBLANES)
            xr = dsr_ref[pl.ds(r0, SUBLANES), :]
            xi = dsi_ref[pl.ds(r0, SUBLANES), :]
            for k, ar, ai in steps:
                sr = pltpu.roll(xr, SUBLANES - k, axis=0)
                si = pltpu.roll(xi, SUBLANES - k, axis=0)
                keep = row < SUBLANES - k
                xr, xi = (xr + jnp.where(keep, ar * sr - ai * si, 0.0),
                          xi + jnp.where(keep, ar * si + ai * sr, 0.0))
            xr, xi = xr + (fre * cr - fim * ci), xi + (fre * ci + fim * cr)
            lre_ref[pl.ds(r0, SUBLANES), :] = xr
            lim_ref[pl.ds(r0, SUBLANES), :] = xi
            p0 = pl.multiple_of(jnp.maximum(i - 1, 0) * SUBLANES, SUBLANES)
            has_prev = i > 0
            spr = jnp.where(row == 0,
                            jnp.where(has_prev, pltpu.roll(sre_ref[pl.ds(p0, SUBLANES), :], 1, axis=0), 0.0),
                            pltpu.roll(sre_ref[pl.ds(r0, SUBLANES), :], 1, axis=0))
            spi = jnp.where(row == 0,
                            jnp.where(has_prev, pltpu.roll(sim_ref[pl.ds(p0, SUBLANES), :], 1, axis=0), 0.0),
                            pltpu.roll(sim_ref[pl.ds(r0, SUBLANES), :], 1, axis=0))
            acr = acr + (xr * spr + xi * spi)
            aci = aci + (xi * spr - xr * spi)
            return _pick_row(xr, row, 0), _pick_row(xi, row, 0), acr, aci

        def group(g, carry):
            for u in range(unroll):
                carry = step(g * unroll + u, carry)
            return carry

        zero = jnp.zeros((SUBLANES, W), F32)
        _, _, acr, aci = lax.fori_loop(0, nt // unroll, group, (zero, zero, zero, zero))
        dar_ref[...] = acr
        dai_ref[...] = aci

    blk = BS((L, W), lambda s, j: (s, j))
    pw = BS((SUBLANES, W), lambda s, j: (0, j))
    da = BS((None, SUBLANES, W), lambda s, j: (s, 0, j))
    return _pc("ssm_scan_bwd", body, (nseq, SL // W),
               [ds_re, ds_im, s_re, s_im, pw_re, pw_im, pwf_re, pwf_im], [blk] * 4 + [pw] * 4,
               [_sds((N, SL), F32)] * 2 + [_sds((nseq, SUBLANES, SL), F32)] * 2, [blk, blk, da, da])


def _pool_select(g, vals):
    return jnp.where(g == 0, vals[0], jnp.where(g == 1, vals[1], jnp.where(g == 2, vals[2], vals[3])))


def _pool_fwd(proj, col0, DP, nseq, L):
    N = proj.shape[0]
    PG = DP // len(POOL_WINDOWS)
    W = _tile(PG, 256)

    def body(v_ref, z_ref):
        g = pl.program_id(1) // (PG // W)
        v = v_ref[...]
        row = lax.broadcasted_iota(jnp.int32, (L, W), 0)
        sums, s, k = [], v, 1
        for _ in POOL_WINDOWS:
            s = s + jnp.where(row >= k, pltpu.roll(s, k, axis=0), 0.0)
            sums.append(s)
            k *= 2
        win = _pool_select(g, [float(w) for w in POOL_WINDOWS])
        cnt = jnp.minimum((row + 1).astype(F32), win)
        z_ref[...] = (_pool_select(g, sums) / cnt - v).astype(z_ref.dtype)

    return _pc("pool_fwd", body, (nseq, DP // W), [proj], [BS((L, W), lambda s, j: (s, col0 // W + j))],
               _sds((N, DP), BF16), BS((L, W), lambda s, j: (s, j)))


def _pool_bwd(dz, nseq, L):
    N, DP = dz.shape
    PG = DP // len(POOL_WINDOWS)
    W = _tile(PG, 256)

    def body(dz_ref, dv_ref):
        g = pl.program_id(1) // (PG // W)
        d = dz_ref[...]
        row = lax.broadcasted_iota(jnp.int32, (L, W), 0)
        win = _pool_select(g, [float(w) for w in POOL_WINDOWS])
        s = d / jnp.minimum((row + 1).astype(F32), win)
        sums, k = [], 1
        for _ in POOL_WINDOWS:
            s = s + jnp.where(row < L - k, pltpu.roll(s, L - k, axis=0), 0.0)
            sums.append(s)
            k *= 2
        dv_ref[...] = (_pool_select(g, sums) - d).astype(dv_ref.dtype)

    blk = BS((L, W), lambda s, j: (s, j))
    return _pc("pool_bwd", body, (nseq, DP // W), [dz], [blk], _sds((N, DP), BF16), blk)


def _rstd(x):
    return lax.rsqrt(jnp.mean(x * x, axis=-1, keepdims=True) + EPS)


def _norm_bwd(dy, xhat, rstd, gain):
    t = dy * gain
    return rstd * (t - xhat * jnp.mean(t * xhat, axis=-1, keepdims=True))


def _pre_norm(x, g1):
    N, D = x.shape
    tr = _tile(N, 128, SUBLANES)

    def body(x_ref, g_ref, a_ref):
        xv = x_ref[...]
        a_ref[...] = (xv * _rstd(xv) * g_ref[...]).astype(a_ref.dtype)

    row = BS((tr, D), lambda i: (i, 0))
    vec = BS((1, D), lambda i: (0, 0))
    return _pc("pre_norm", body, (N // tr,), [x, g1], [row, vec], _sds((N, D), BF16), row)


def _mid_norm(x, o, g2, g3):
    N, D = x.shape
    tr = _tile(N, 128, SUBLANES)

    def body(x_ref, o_ref, g2_ref, g3_ref, h1_ref, c_ref):
        ov = o_ref[...]
        h1 = x_ref[...] + ov * _rstd(ov) * g2_ref[...]
        h1_ref[...] = h1
        c_ref[...] = (h1 * _rstd(h1) * g3_ref[...]).astype(c_ref.dtype)

    row = BS((tr, D), lambda i: (i, 0))
    vec = BS((1, D), lambda i: (0, 0))
    return _pc("mid_norm", body, (N // tr,), [x, o, g2, g3], [row, row, vec, vec],
               [_sds((N, D), F32), _sds((N, D), BF16)], [row, row])


def _post_ffn(h1, dn, tgt, g4):
    N, D = h1.shape
    tr = _tile(N, 128, SUBLANES)

    def body(h1_ref, dn_ref, t_ref, g_ref, dh2_ref, ddn_ref, lossv_ref, dg_ref):
        first = pl.program_id(0) == 0
        dnv = dn_ref[...]
        rstd = _rstd(dnv)
        xhat = dnv * rstd
        gain = g_ref[...]
        err = (h1_ref[...] + xhat * gain) - t_ref[...]
        dh2 = err / float(D)
        dh2_ref[...] = dh2
        ddn_ref[...] = _norm_bwd(dh2, xhat, rstd, gain).astype(ddn_ref.dtype)
        _rowsum_into(lossv_ref, first, err * err)
        _rowsum_into(dg_ref, first, dh2 * xhat)

    row = BS((tr, D), lambda i: (i, 0))
    vec = BS((1, D), lambda i: (0, 0))
    return _pc("post_ffn", body, (N // tr,), [h1, dn, tgt, g4], [row, row, row, vec],
               [_sds((N, D), F32), _sds((N, D), BF16), _sds((1, D), F32), _sds((1, D), F32)], [row, row, vec, vec])


def _mid_bwd(dh2, dc, h1, o, g2, g3, deps=()):
    N, D = h1.shape
    tr = _tile(N, 128, SUBLANES)

    def body(dh2_ref, dc_ref, h1_ref, o_ref, g2_ref, g3_ref, dh1_ref, do_ref, dg2_ref, dg3_ref):
        first = pl.program_id(0) == 0
        h1 = h1_ref[...]
        r3 = _rstd(h1)
        hc = h1 * r3
        dcv = dc_ref[...]
        dh1 = dh2_ref[...] + _norm_bwd(dcv, hc, r3, g3_ref[...])
        dh1_ref[...] = dh1
        ov = o_ref[...]
        r2 = _rstd(ov)
        ho = ov * r2
        do_ref[...] = _norm_bwd(dh1, ho, r2, g2_ref[...]).astype(do_ref.dtype)
        _rowsum_into(dg3_ref, first, dcv * hc)
        _rowsum_into(dg2_ref, first, dh1 * ho)

    row = BS((tr, D), lambda i: (i, 0))
    vec = BS((1, D), lambda i: (0, 0))
    return _pc("mid_bwd", body, (N // tr,), [dh2, dc, h1, o, g2, g3], [row] * 4 + [vec, vec],
               [_sds((N, D), F32), _sds((N, D), BF16), _sds((1, D), F32), _sds((1, D), F32)], [row, row, vec, vec],
               deps=deps)


def _pre_bwd(x, da, dh1, g1):
    N, D = x.shape
    tr = _tile(N, 128, SUBLANES)

    def body(x_ref, da_ref, dh1_ref, g_ref, dx_ref, dg_ref):
        first = pl.program_id(0) == 0
        xv = x_ref[...]
        r1 = _rstd(xv)
        xh = xv * r1
        dav = da_ref[...]
        dx_ref[...] = dh1_ref[...] + _norm_bwd(dav, xh, r1, g_ref[...])
        _rowsum_into(dg_ref, first, dav * xh)

    row = BS((tr, D), lambda i: (i, 0))
    vec = BS((1, D), lambda i: (0, 0))
    return _pc("pre_bwd", body, (N // tr,), [x, da, dh1, g1], [row, row, row, vec],
               [_sds((N, D), F32), _sds((1, D), F32)], [row, vec])


def _conv_rows(x_ref, halo_ref, first):
    x = x_ref[...]
    xx = jnp.concatenate([jnp.where(first, 0.0, halo_ref[...]), x], axis=0)
    x1 = pltpu.roll(xx, 1, axis=0)[SUBLANES:]
    x2 = pltpu.roll(xx, 2, axis=0)[SUBLANES:]
    return x, x1, x2


def _conv_apply(rows, w_ref, b_ref):
    x, x1, x2 = rows
    return ((b_ref[...] + x2 * w_ref[pl.ds(0, 1), :]) + x1 * w_ref[pl.ds(1, 1), :]) + x * w_ref[pl.ds(2, 1), :]


def _gate_specs(N, FC, TR, half):
    tile = BS((None, TR, FC), lambda jj, i: (jj + half, i, 0))
    halo = BS((None, SUBLANES, FC), lambda jj, i: (jj + half, jnp.maximum(i * (TR // SUBLANES) - 1, 0), 0))
    cw = BS((None, 3, FC), lambda jj, i: (jj + half, 0, 0))
    cb = BS((None, 1, FC), lambda jj, i: (jj + half, 0, 0))
    return tile, halo, cw, cb


def _gate_fwd(up_pre, cw, cb, L, deps=()):
    nb, N, FC = up_pre.shape
    half = nb // 2
    TR = _tile(L, 128, SUBLANES)

    def body(xa_ref, ha_ref, wa_ref, ba_ref, xb_ref, hb_ref, wb_ref, bb_ref, f_ref):
        first = (pl.program_id(1) % (L // TR)) == 0
        ua = _conv_apply(_conv_rows(xa_ref, ha_ref, first), wa_ref, ba_ref)
        ub = _conv_apply(_conv_rows(xb_ref, hb_ref, first), wb_ref, bb_ref)
        f_ref[...] = (_gelu(ua) * ub).astype(f_ref.dtype)

    sa, sb = _gate_specs(N, FC, TR, 0), _gate_specs(N, FC, TR, half)
    return _pc("gate_fwd", body, (half, N // TR), [up_pre, up_pre, cw, cb] * 2, list(sa) + list(sb),
               _sds((half, N, FC), BF16), BS((None, TR, FC), lambda jj, i: (jj, i, 0)), deps=deps)


def _gate_bwd(up_pre, cw, cb, df, L, deps=()):
    nb, N, FC = up_pre.shape
    half = nb // 2
    TR = _tile(L, 128, SUBLANES)

    def body(xa_ref, ha_ref, wa_ref, ba_ref, xb_ref, hb_ref, wb_ref, bb_ref, df_ref, dup_ref, dw_ref, dbias_ref):
        i = pl.program_id(1)
        first_row = i == 0
        first = (i % (L // TR)) == 0
        ra = _conv_rows(xa_ref, ha_ref, first)
        rb = _conv_rows(xb_ref, hb_ref, first)
        ua = _conv_apply(ra, wa_ref, ba_ref)
        ub = _conv_apply(rb, wb_ref, bb_ref)
        dfv = df_ref[...].astype(F32)
        dua = dfv * ub * _gelu_grad(ua)
        dub = dfv * _gelu(ua)
        dup_ref[0] = dua.astype(dup_ref.dtype)
        dup_ref[1] = dub.astype(dup_ref.dtype)
        for h, (rows, du) in enumerate(((ra, dua), (rb, dub))):
            x, x1, x2 = rows
            _rowsum_into(dbias_ref.at[h], first_row, du)
            for k, xs in enumerate((x2, x1, x)):
                _rowsum_into(dw_ref.at[h, pl.ds(k, 1), :], first_row, du * xs)

    sa, sb = _gate_specs(N, FC, TR, 0), _gate_specs(N, FC, TR, half)
    tile = BS((None, TR, FC), lambda jj, i: (jj, i, 0))
    both = BS((2, None, TR, FC), lambda jj, i: (0, jj, i, 0))
    dw = BS((2, None, 3, FC), lambda jj, i: (0, jj, 0, 0))
    dbias = BS((2, None, 1, FC), lambda jj, i: (0, jj, 0, 0))
    return _pc("gate_bwd", body, (half, N // TR), [up_pre, up_pre, cw, cb] * 2 + [df], list(sa) + list(sb) + [tile],
               [_sds((2, half, N, FC), BF16), _sds((2, half, 3, FC), F32), _sds((2, half, 1, FC), F32)],
               [both, dw, dbias], deps=deps)


def _conv_bwd(dup, cw, L):
    nb, N, FC = dup.shape
    TR = _tile(L, 128, 2 * SUBLANES)
    HR = 2 * SUBLANES
    nrb = N // HR

    def body(x_ref, h_ref, w_ref, o_ref):
        last = ((pl.program_id(1) + 1) % (L // TR)) == 0
        x = x_ref[...].astype(F32)
        xx = jnp.concatenate([x, jnp.where(last, 0.0, h_ref[...].astype(F32))], axis=0)
        x1 = pltpu.roll(xx, TR + HR - 1, axis=0)[:TR]
        x2 = pltpu.roll(xx, TR + HR - 2, axis=0)[:TR]
        o_ref[...] = (x * w_ref[pl.ds(2, 1), :] + x1 * w_ref[pl.ds(1, 1), :] + x2 * w_ref[pl.ds(0, 1), :]
                      ).astype(o_ref.dtype)

    tile = BS((None, TR, FC), lambda jj, i: (jj, i, 0))
    halo = BS((None, HR, FC), lambda jj, i: (jj, jnp.minimum((i + 1) * (TR // HR), nrb - 1), 0))
    w = BS((None, 3, FC), lambda jj, i: (jj, 0, 0))
    return _pc("conv_bwd", body, (nb, N // TR), [dup, dup, cw], [tile, halo, w], _sds((nb, N, FC), BF16), tile)


def _pack(arrs):
    parts = []
    for a in arrs:
        flat = a.reshape(-1).astype(F32)
        pad = (-flat.shape[0]) % (SUBLANES * LANES)
        parts.append(jnp.pad(flat, (0, pad)))
    return jnp.concatenate(parts).reshape(-1, LANES)


def _unpack(packed, shapes):
    flat = packed.reshape(-1)
    out, off = [], 0
    for s in shapes:
        n = math.prod(s)
        out.append(flat[off:off + n].reshape(s))
        off += n + ((-n) % (SUBLANES * LANES))
    return out


def _small_sum(gathered, gathered_bf, loss_rows, d_model):
    S, R, C = gathered.shape
    R2 = gathered_bf.shape[1]

    def body(p_ref, q_ref, tot_ref, tot2_ref, loss_ref):
        t = p_ref[0]
        u = q_ref[0].astype(F32)
        for s in range(1, S):
            t = t + p_ref[s]
            u = u + q_ref[s].astype(F32)
        tot_ref[...] = t
        tot2_ref[...] = u
        loss_ref[...] = jnp.full((1, 1), 0.5 / d_model, F32) * jnp.sum(t[:loss_rows])

    return _pc("small_sum", body, (1,), [gathered, gathered_bf],
               [BS((S, R, C), lambda i: (0, 0, 0)), BS((S, R2, C), lambda i: (0, 0, 0))],
               [_sds((R, C), F32), _sds((R2, C), F32), _sds((1, 1), F32)],
               [BS((R, C), lambda i: (0, 0)), BS((R2, C), lambda i: (0, 0)), BS((1, 1), lambda i: (0, 0))])


def _block_diag_in(bb_t, nch):
    J, G, P = bb_t.shape
    gl = G // nch
    b = bb_t.reshape(J, nch, gl, P).transpose(1, 0, 2, 3)
    eye = jnp.eye(gl, dtype=F32)
    w = eye[None, :, None, :, None] * b[:, None, :, :, :]
    return w.reshape(nch, gl * J, gl * P)


def _block_diag_in_grad(dw, J, G, P):
    nch = dw.shape[0]
    gl = G // nch
    d = dw.reshape(nch, gl, J, gl, P)
    d = jnp.einsum("cgjgp->jcgp", d)
    return d.reshape(J, G, P)


def _block_diag_out(c, nch):
    G, J, P = c.shape
    gl = G // nch
    cc = c.reshape(nch, gl, J, P).transpose(0, 1, 3, 2)
    eye = jnp.eye(gl, dtype=F32)
    w = cc[:, :, :, None, :] * eye[None, :, None, :, None]
    return w.reshape(nch, gl * P, gl * J)


def _block_diag_out_grad(dw, G, J, P):
    nch = dw.shape[0]
    gl = G // nch
    d = dw.reshape(nch, gl, P, gl, J)
    d = jnp.einsum("cgpgj->cgjp", d)
    return d.reshape(G, J, P)


def kernel(x, norm_pre_mix, w_in, ssm_lambda_re, ssm_lambda_im, ssm_log_step, ssm_b_re, ssm_b_im, ssm_c_re, ssm_c_im, ssm_d, ssm_glu_w, ssm_glu_b, pool_w, pool_b, pool_scale, w_branch_ssm, w_branch_pool, w_out, norm_post_mix, norm_pre_ffn, w_up, ffn_conv_w, ffn_conv_b, w_down, norm_post_ffn, loss_target, m_norm_pre_mix, m_w_in, m_ssm_lambda_re, m_ssm_lambda_im, m_ssm_log_step, m_ssm_b_re, m_ssm_b_im, m_ssm_c_re, m_ssm_c_im, m_ssm_d, m_ssm_glu_w, m_ssm_glu_b, m_pool_w, m_pool_b, m_pool_scale, m_w_branch_ssm, m_w_branch_pool, m_w_out, m_norm_post_mix, m_norm_pre_ffn, m_w_up, m_ffn_conv_w, m_ffn_conv_b, m_w_down, m_norm_post_ffn, v_norm_pre_mix, v_w_in, v_ssm_lambda_re, v_ssm_lambda_im, v_ssm_log_step, v_ssm_b_re, v_ssm_b_im, v_ssm_c_re, v_ssm_c_im, v_ssm_d, v_ssm_glu_w, v_ssm_glu_b, v_pool_w, v_pool_b, v_pool_scale, v_w_branch_ssm, v_w_branch_pool, v_w_out, v_norm_post_mix, v_norm_pre_ffn, v_w_up, v_ffn_conv_w, v_ffn_conv_b, v_w_down, v_norm_post_ffn):
    args = dict(locals())
    names = ["norm_pre_mix", "w_in", "ssm_lambda_re", "ssm_lambda_im", "ssm_log_step", "ssm_b_re", "ssm_b_im",
             "ssm_c_re", "ssm_c_im", "ssm_d", "ssm_glu_w", "ssm_glu_b", "pool_w", "pool_b", "pool_scale",
             "w_branch_ssm", "w_branch_pool", "w_out", "norm_post_mix", "norm_pre_ffn", "w_up", "ffn_conv_w",
             "ffn_conv_b", "w_down", "norm_post_ffn"]

    nseq, L, D = x.shape
    N = nseq * L
    U = D // NDEV
    DS = ssm_d.shape[1]
    DP = pool_scale.shape[1]
    G, P, J = ssm_b_re.shape[1:]
    SL = G * P
    CH = GROUPS_PER_CHUNK * J
    CS = GROUPS_PER_CHUNK * P
    NCH = DS // CH
    NPG = len(POOL_WINDOWS)
    PG = DP // NPG
    FC = w_up.shape[2]
    NB = NDEV
    HB = NB // 2
    F2 = NB * FC
    dev = _slot(_mesh_pos())
    tm = _tile(N, 1024)
    tm2 = _tile(N, 512)

    x2 = x.reshape(N, D)
    tgt = loss_target.reshape(N, D)

    def bf(t):
        return t.astype(BF16)

    def g_start(tag, group, after):
        return _split_start("gather_start_" + tag, _gather_copies, 3, group,
                            [_sds((NDEV,) + s.shape, s.dtype) for s in group], after)

    def g_land(tag, started, after):
        srcs, lands = _split_wait("gather_wait_" + tag, _gather_copies, started, after)
        return _split_start("d2d_start_" + tag, _d2d_copies, 4, srcs, lands, srcs[0])

    def g_finish(tag, d2d, after):
        srcs, lands = _split_wait("d2d_wait_" + tag, _d2d_copies, d2d, after)
        return [lax.dynamic_update_index_in_dim(l, s, dev, 0) for l, s in zip(lands, srcs)]

    def x_start(tag, group):
        return _split_start("exchange_start_" + tag, _exchange_copies, NDEV - 1, group,
                            [_sds(g.shape, g.dtype) for g in group], group[0])

    def x_finish(tag, started, after):
        srcs, lands = _split_wait("exchange_wait_" + tag, _exchange_copies, started, after)
        own = [lax.dynamic_index_in_dim(s, dev, 0, keepdims=False) for s in srcs]
        return [lax.dynamic_update_index_in_dim(l, o, dev, 0) for l, o in zip(lands, own)]

    st_in = g_start("in", [bf(w_in[0])], x2)
    st_mix = g_start("mix", [bf(ssm_glu_w[0]), bf(pool_w[0]), pool_b[0], ffn_conv_w[0]], st_in[4])
    conv_b_blk = ffn_conv_b.reshape(NB, 1, FC)

    (_, x2e, w_branch_ssm, w_branch_pool, w_out, w_up, w_down, ssm_lambda_re, ssm_lambda_im, ssm_log_step,
     ssm_b_re, ssm_b_im, ssm_c_re, ssm_c_im) = lax.optimization_barrier(
        (st_mix[4], x2, w_branch_ssm, w_branch_pool, w_out, w_up, w_down, ssm_lambda_re, ssm_lambda_im, ssm_log_step,
         ssm_b_re, ssm_b_im, ssm_c_re, ssm_c_im))
    lam_re, lam_im = ssm_lambda_re[0], ssm_lambda_im[0]
    log_step = ssm_log_step.reshape(G, 1)
    br_t = ssm_b_re[0].transpose(2, 0, 1)
    bi_t = ssm_b_im[0].transpose(2, 0, 1)
    pw_re3, pw_im3, pwf_re3, pwf_im3, bb_re, bb_im = _ssm_param_fwd(lam_re, lam_im, log_step, br_t, bi_t)
    pw_re, pw_im = pw_re3.reshape(SUBLANES, SL), pw_im3.reshape(SUBLANES, SL)
    pwf_re, pwf_im = pwf_re3.reshape(SUBLANES, SL), pwf_im3.reshape(SUBLANES, SL)
    WB = jnp.concatenate([_block_diag_in(bb_re, NCH), _block_diag_in(bb_im, NCH)], axis=2).astype(BF16)
    WCre = _block_diag_out(ssm_c_re[0], NCH).astype(BF16)
    WCim = _block_diag_out(-ssm_c_im[0], NCH).astype(BF16)
    a = _pre_norm(x2e, norm_pre_mix)
    small_names = ["norm_pre_mix", "norm_post_mix", "norm_pre_ffn", "norm_post_ffn", "ssm_lambda_re", "ssm_lambda_im",
                   "ssm_log_step", "ssm_b_re", "ssm_b_im", "ssm_c_re", "ssm_c_im", "ssm_d", "ssm_glu_b", "pool_scale",
                   "pool_b", "ffn_conv_w", "ffn_conv_b"]
    _, small_in = lax.optimization_barrier(
        (st_mix[4], [[args[p + n] for n in small_names] for p in ("", "m_", "v_")]))
    sm_w, sm_m, sm_v = (_pack(group) for group in small_in)
    g_br = [bf(w_branch_ssm[0]), bf(w_branch_pool[0]), bf(w_out[0])]
    g_up, g_down = [bf(w_up[0].T)], [bf(w_down[0])]
    early = [WB, WCre, WCim, pwf_re, pwf_im, a, sm_w, sm_m, sm_v] + g_br + g_up + g_down

    d_in = g_land("in", st_in, [st_mix[4]] + early)
    st_br = g_start("branch", g_br, d_in[4])
    st_up = g_start("up", g_up, st_br[4])
    st_down = g_start("down", g_down, st_up[4])
    (Win,) = g_finish("in", d_in, st_down[4])

    nq = 3 * NDEV
    (proj,) = _fused_matmul(
        "in_proj", (N // tm, nq, 1),
        [(a, BS((tm, D), lambda i, q, k: (i, 0)), Win, BS((None, D, U), lambda i, q, k: (q // 3, 0, q % 3)), "nn", 0)],
        [(tm, U)], [], [(_sds((N, 3 * D), F32), BS((tm, U), lambda i, q, k: (i, q)))],
        _store(lambda accs: accs))
    d_mix = g_land("mix", st_mix, proj)
    d_br = g_land("branch", st_br, d_mix[4])

    bu_re, bu_im = _fused_matmul(
        "ssm_in", (N // tm2, NCH, 1),
        [(proj, BS((tm2, CH), lambda i, c, k: (i, c)), WB, BS((None, CH, 2 * CS), lambda i, c, k: (c, 0, 0)), "nn", 0)],
        [(tm2, 2 * CS)], [],
        [(_sds((N, SL), F32), BS((tm2, CS), lambda i, c, k: (i, c)))] * 2,
        _store(lambda accs: (accs[0][:, :CS], accs[0][:, CS:])), deps=[d_br[4]])
    s_re, s_im = _scan_fwd(bu_re, bu_im, pw_re, pw_im, nseq, L)

    def ssm_out_epi(ids, accs, ex, o):
        u_ref, d_ref = ex
        y0 = accs[0] + d_ref[...] * u_ref[...]
        o[0][...] = y0
        o[1][...] = _gelu(y0).astype(BF16)

    y0, y1 = _fused_matmul(
        "ssm_out", (N // tm2, NCH, 1),
        [(s_re, BS((tm2, CS), lambda i, c, k: (i, c)), WCre, BS((None, CS, CH), lambda i, c, k: (c, 0, 0)), "nn", 0),
         (s_im, BS((tm2, CS), lambda i, c, k: (i, c)), WCim, BS((None, CS, CH), lambda i, c, k: (c, 0, 0)), "nn", 0)],
        [(tm2, CH)],
        [(proj, BS((tm2, CH), lambda i, c, k: (i, c))), (ssm_d, BS((1, CH), lambda i, c, k: (0, c)))],
        [(_sds((N, DS), F32), BS((tm2, CH), lambda i, c, k: (i, c))),
         (_sds((N, DS), BF16), BS((tm2, CH), lambda i, c, k: (i, c)))],
        ssm_out_epi)

    Wglu, Wpool, pool_b_all, conv_w_all = g_finish("mix", d_mix, y1)
    Wglu = Wglu.reshape(DS, DS)
    Wpool = Wpool.transpose(1, 0, 2, 3).reshape(NPG, PG, PG)
    pool_b_full = pool_b_all.transpose(1, 0, 2).reshape(1, DP)
    tn_s = _tile(DS, 512)

    def glu_epi(ids, accs, ex, o):
        y0_ref, b_ref = ex
        zg = accs[0] + b_ref[...]
        o[0][...] = zg
        o[1][...] = (_gelu(y0_ref[...]) * _sigmoid(zg)).astype(BF16)

    zg, ys = _fused_matmul(
        "ssm_glu", (N // tm, DS // tn_s, 1),
        [(y1, BS((tm, DS), lambda i, j, k: (i, 0)), Wglu, BS((DS, tn_s), lambda i, j, k: (0, j)), "nn", 0)],
        [(tm, tn_s)],
        [(y0, BS((tm, tn_s), lambda i, j, k: (i, j))), (ssm_glu_b, BS((1, tn_s), lambda i, j, k: (0, j)))],
        [(_sds((N, DS), F32), BS((tm, tn_s), lambda i, j, k: (i, j))),
         (_sds((N, DS), BF16), BS((tm, tn_s), lambda i, j, k: (i, j)))],
        glu_epi)

    z = _pool_fwd(proj, DS, DP, nseq, L)

    def pool_mm_epi(ids, accs, ex, o):
        b_ref, sc_ref = ex
        q = accs[0] + b_ref[...]
        o[0][...] = q
        o[1][...] = (q * sc_ref[...]).astype(BF16)

    qp, yp = _fused_matmul(
        "pool_mm", (N // tm, NPG, 1),
        [(z, BS((tm, PG), lambda i, g, k: (i, g)), Wpool, BS((None, PG, PG), lambda i, g, k: (g, 0, 0)), "nn", 0)],
        [(tm, PG)],
        [(pool_b_full, BS((1, PG), lambda i, g, k: (0, g))), (pool_scale, BS((1, PG), lambda i, g, k: (0, g)))],
        [(_sds((N, DP), F32), BS((tm, PG), lambda i, g, k: (i, g))),
         (_sds((N, DP), BF16), BS((tm, PG), lambda i, g, k: (i, g)))],
        pool_mm_epi)

    Wbs, Wbp, Wout = g_finish("branch", d_br, yp)
    Wout = Wout.reshape(D, D)
    gs_blk = BS((tm2, U), lambda i, q, k: (i, (DS + DP) // U + q))
    gp_blk = BS((tm2, U), lambda i, q, k: (i, (DS + DP + D) // U + q))
    out_blk = BS((tm2, U), lambda i, q, k: (i, q))

    def branch_epi(ids, accs, ex, o):
        gs_ref, gp_ref = ex
        o[0][...] = accs[0].astype(BF16)
        o[1][...] = accs[1].astype(BF16)
        o[2][...] = (_sigmoid(gs_ref[...]) * accs[0] + _sigmoid(gp_ref[...]) * accs[1]).astype(BF16)

    Ys, Yp, merged = _fused_matmul(
        "branch", (N // tm2, NDEV, 1),
        [(ys, BS((tm2, DS), lambda i, q, k: (i, 0)), Wbs, BS((None, DS, U), lambda i, q, k: (q, 0, 0)), "nn", 0),
         (yp, BS((tm2, DP), lambda i, q, k: (i, 0)), Wbp, BS((None, DP, U), lambda i, q, k: (q, 0, 0)), "nn", 1)],
        [(tm2, U), (tm2, U)],
        [(proj, gs_blk), (proj, gp_blk)],
        [(_sds((N, D), BF16), out_blk), (_sds((N, D), BF16), out_blk), (_sds((N, D), BF16), out_blk)],
        branch_epi)
    d_up = g_land("up", st_up, merged)

    tn_d = _tile(D, 512)
    (o_mix,) = _fused_matmul(
        "out_proj", (N // tm, D // tn_d, 1),
        [(merged, BS((tm, D), lambda i, j, k: (i, 0)), Wout, BS((D, tn_d), lambda i, j, k: (0, j)), "nn", 0)],
        [(tm, tn_d)], [], [(_sds((N, D), F32), BS((tm, tn_d), lambda i, j, k: (i, j)))],
        _store(lambda accs: accs), deps=[d_up[4]])
    h1, c = _mid_norm(x2, o_mix, norm_post_mix, norm_pre_ffn)

    (Wup,) = g_finish("up", d_up, c)
    tk_up = _tile(D, 2048)
    (up_pre,) = _fused_matmul(
        "ffn_up", (N // tm2, NB, D // tk_up),
        [(c, BS((tm2, tk_up), lambda i, j, k: (i, k)), Wup, BS((None, FC, tk_up), lambda i, j, k: (j, 0, k)), "nt", 0)],
        [(tm2, FC)], [], [(_sds((NB, N, FC), F32), BS((None, tm2, FC), lambda i, j, k: (j, i, 0)))],
        _store(lambda accs: accs))
    d_down = g_land("down", st_down, up_pre)
    f = _gate_fwd(up_pre, conv_w_all, conv_b_blk, L, deps=[d_down[4]])
    (Wdown,) = g_finish("down", d_down, f)
    Wdown = Wdown.reshape(HB, FC, D)
    tn_d2 = _tile(D, 1024)
    (dn,) = _fused_matmul(
        "ffn_down", (N // tm2, D // tn_d2, HB // 2),
        [(f, BS((None, tm2, FC), lambda i, j, k: (2 * k, i, 0)), Wdown, BS((None, FC, tn_d2), lambda i, j, k: (2 * k, 0, j)), "nn", 0),
         (f, BS((None, tm2, FC), lambda i, j, k: (2 * k + 1, i, 0)), Wdown, BS((None, FC, tn_d2), lambda i, j, k: (2 * k + 1, 0, j)), "nn", 0)],
        [(tm2, tn_d2)], [], [(_sds((N, D), F32), BS((tm2, tn_d2), lambda i, j, k: (i, j)))],
        _store(lambda accs: accs))
    dh2, d_dn, lossv, dg4 = _post_ffn(h1, dn, tgt, norm_post_ffn)

    tk_d = _tile(D, 2048)
    (df,) = _fused_matmul(
        "ffn_down_dx", (N // tm2, HB, D // tk_d),
        [(d_dn, BS((tm2, tk_d), lambda i, j, k: (i, k)), Wdown, BS((None, FC, tk_d), lambda i, j, k: (j, 0, k)), "nt", 0)],
        [(tm2, FC)], [], [(_sds((HB, N, FC), BF16), BS((None, tm2, FC), lambda i, j, k: (j, i, 0)))],
        _store(lambda accs: accs))
    tk_n = _tile(N, 2048)
    (gW_down,) = _fused_matmul(
        "ffn_down_dw", (HB, D // tn_d, N // tk_n),
        [(f, BS((None, tk_n, FC), lambda j, n, k: (j, k, 0)), d_dn, BS((tk_n, tn_d), lambda j, n, k: (k, n)), "tn", 0)],
        [(FC, tn_d)], [], [(_sds((HB, FC, D), BF16), BS((None, FC, tn_d), lambda j, n, k: (j, 0, n)))],
        _store(lambda accs: accs))
    x_down = x_start("down", [gW_down.reshape(NDEV, FC // 2, D)])
    dup, dcw, dcb = _gate_bwd(up_pre, conv_w_all, conv_b_blk, df, L, deps=[x_down[4]])
    dpre = _conv_bwd(dup.reshape(NB, N, FC), conv_w_all, L)
    (dc,) = _fused_matmul(
        "ffn_up_dx", (N // tm2, D // tn_d2, NB // 2),
        [(dpre, BS((None, tm2, FC), lambda i, j, k: (2 * k, i, 0)), Wup, BS((None, FC, tn_d2), lambda i, j, k: (2 * k, 0, j)), "nn", 0),
         (dpre, BS((None, tm2, FC), lambda i, j, k: (2 * k + 1, i, 0)), Wup, BS((None, FC, tn_d2), lambda i, j, k: (2 * k + 1, 0, j)), "nn", 0)],
        [(tm2, tn_d2)], [], [(_sds((N, D), F32), BS((tm2, tn_d2), lambda i, j, k: (i, j)))],
        _store(lambda accs: accs))
    tm_d = _tile(D, 512)
    (gW_up,) = _fused_matmul(
        "ffn_up_dw", (NB, D // tm_d, N // tk_n),
        [(dpre, BS((None, tk_n, FC), lambda j, n, k: (j, k, 0)), c, BS((tk_n, tm_d), lambda j, n, k: (k, n)), "tn", 0)],
        [(FC, tm_d)], [], [(_sds((NB, FC, D), BF16), BS((None, FC, tm_d), lambda j, n, k: (j, 0, n)))],
        _store(lambda accs: accs))
    x_up = x_start("up", [gW_up])

    dh1, d_o, dg2, dg3 = _mid_bwd(dh2, dc, h1, o_mix, norm_post_mix, norm_pre_ffn, deps=[x_up[4]])

    def dmerged_epi(ids, accs, ex, o):
        gs_ref, gp_ref, ys_ref, yp_ref = ex
        dm = accs[0]
        sg_s, sg_p = _sigmoid(gs_ref[...]), _sigmoid(gp_ref[...])
        o[0][...] = (dm * sg_s).astype(BF16)
        o[1][...] = (dm * sg_p).astype(BF16)
        o[2][...] = (dm * ys_ref[...].astype(F32) * sg_s * (1.0 - sg_s)).astype(BF16)
        o[3][...] = (dm * yp_ref[...].astype(F32) * sg_p * (1.0 - sg_p)).astype(BF16)

    dYs, dYp, dgs, dgp = _fused_matmul(
        "out_proj_dx", (N // tm2, NDEV, 1),
        [(d_o, BS((tm2, D), lambda i, q, k: (i, 0)), Wout, BS((U, D), lambda i, q, k: (q, 0)), "nt", 0)],
        [(tm2, U)],
        [(proj, gs_blk), (proj, gp_blk), (Ys, out_blk), (Yp, out_blk)],
        [(_sds((N, D), BF16), out_blk)] * 4,
        dmerged_epi)
    (gW_out,) = _fused_matmul(
        "out_proj_dw", (D // tm_d, D // tn_d, 1),
        [(merged, BS((N, tm_d), lambda i, j, k: (0, i)), d_o, BS((N, tn_d), lambda i, j, k: (0, j)), "tn", 0)],
        [(tm_d, tn_d)], [], [(_sds((D, D), BF16), BS((tm_d, tn_d), lambda i, j, k: (i, j)))],
        _store(lambda accs: accs))
    tm_s = _tile(DS, 512)
    gW_bs, gW_bp = _fused_matmul(
        "branch_dw", (DS // tm_s, NDEV, 1),
        [(ys, BS((N, tm_s), lambda i, q, k: (0, i)), dYs, BS((N, U), lambda i, q, k: (0, q)), "tn", 0),
         (yp, BS((N, tm_s), lambda i, q, k: (0, i)), dYp, BS((N, U), lambda i, q, k: (0, q)), "tn", 1)],
        [(tm_s, U), (tm_s, U)], [],
        [(_sds((NDEV, DS, U), BF16), BS((None, tm_s, U), lambda i, q, k: (q, i, 0)))] * 2,
        _store(lambda accs: accs))
    x_br = x_start("branch", [gW_bs, gW_bp, gW_out.reshape(NDEV, U, D)])

    tn_p = _tile(PG, 512)

    def dyp_epi(ids, accs, ex, o):
        q_ref, sc_ref = ex
        first = ids[1] == 0
        dyp = accs[0]
        dq = dyp * sc_ref[...]
        o[0][...] = dq.astype(BF16)
        _rowsum_into(o[1], first, dyp * q_ref[...])
        _rowsum_into(o[2], first, dq)

    dq, d_pscale, d_pb = _fused_matmul(
        "branch_pool_dx", (DP // tn_p, N // tm, 1),
        [(dYp, BS((tm, D), lambda j, i, k: (i, 0)), Wbp, BS((NDEV, tn_p, U), lambda j, i, k: (0, j, 0)), "nt_cat", 0)],
        [(tm, tn_p)],
        [(qp, BS((tm, tn_p), lambda j, i, k: (i, j))), (pool_scale, BS((1, tn_p), lambda j, i, k: (0, j)))],
        [(_sds((N, DP), BF16), BS((tm, tn_p), lambda j, i, k: (i, j))),
         (_sds((1, DP), F32), BS((1, tn_p), lambda j, i, k: (0, j))),
         (_sds((1, DP), F32), BS((1, tn_p), lambda j, i, k: (0, j)))],
        dyp_epi, deps=[x_br[4]])
    (dz,) = _fused_matmul(
        "pool_mm_dx", (N // tm, NPG, 1),
        [(dq, BS((tm, PG), lambda i, g, k: (i, g)), Wpool, BS((None, PG, PG), lambda i, g, k: (g, 0, 0)), "nt", 0)],
        [(tm, PG)], [], [(_sds((N, DP), F32), BS((tm, PG), lambda i, g, k: (i, g)))],
        _store(lambda accs: accs))
    (gW_pool,) = _fused_matmul(
        "pool_mm_dw", (NPG, 1),
        [(z, BS((N, PG), lambda g, k: (0, g)), dq, BS((N, PG), lambda g, k: (0, g)), "tn", 0)],
        [(PG, PG)], [], [(_sds((NPG, PG, PG), BF16), BS((None, PG, PG), lambda g, k: (g, 0, 0)))],
        _store(lambda accs: accs))
    du_pool = _pool_bwd(dz, nseq, L)

    def dys_epi(ids, accs, ex, o):
        zg_ref, y0_ref = ex
        first = ids[1] == 0
        dys = accs[0]
        sg = _sigmoid(zg_ref[...])
        dzg = dys * _gelu(y0_ref[...]) * sg * (1.0 - sg)
        o[0][...] = dzg.astype(BF16)
        o[1][...] = dys * sg
        _rowsum_into(o[2], first, dzg)

    dzg, dy1_direct, d_glu_b = _fused_matmul(
        "branch_ssm_dx", (DS // tn_s, N // tm, 1),
        [(dYs, BS((tm, D), lambda j, i, k: (i, 0)), Wbs, BS((NDEV, tn_s, U), lambda j, i, k: (0, j, 0)), "nt_cat", 0)],
        [(tm, tn_s)],
        [(zg, BS((tm, tn_s), lambda j, i, k: (i, j))), (y0, BS((tm, tn_s), lambda j, i, k: (i, j)))],
        [(_sds((N, DS), BF16), BS((tm, tn_s), lambda j, i, k: (i, j))),
         (_sds((N, DS), F32), BS((tm, tn_s), lambda j, i, k: (i, j))),
         (_sds((1, DS), F32), BS((1, tn_s), lambda j, i, k: (0, j)))],
        dys_epi)
    (gW_glu,) = _fused_matmul(
        "ssm_glu_dw", (DS // tm_s, DS // tn_s, 1),
        [(y1, BS((N, tm_s), lambda i, j, k: (0, i)), dzg, BS((N, tn_s), lambda i, j, k: (0, j)), "tn", 0)],
        [(tm_s, tn_s)], [], [(_sds((DS, DS), BF16), BS((tm_s, tn_s), lambda i, j, k: (i, j)))],
        _store(lambda accs: accs))
    x_mix = x_start("mix", [gW_glu.reshape(NDEV, DS // NDEV, DS),
                            gW_pool.reshape(NPG, NDEV, PG // NDEV, PG).transpose(1, 0, 2, 3)])

    tn_c = _tile(DS, CH)

    def dy0_epi(ids, accs, ex, o):
        d1_ref, y0_ref, u_ref = ex
        first = ids[1] == 0
        dy0 = (accs[0] + d1_ref[...]) * _gelu_grad(y0_ref[...])
        o[0][...] = dy0
        _rowsum_into(o[1], first, dy0 * u_ref[...])

    dy0, d_ssm_d = _fused_matmul(
        "ssm_glu_dx", (DS // tn_c, N // tm, 1),
        [(dzg, BS((tm, DS), lambda j, i, k: (i, 0)), Wglu, BS((tn_c, DS), lambda j, i, k: (j, 0)), "nt", 0)],
        [(tm, tn_c)],
        [(dy1_direct, BS((tm, tn_c), lambda j, i, k: (i, j))), (y0, BS((tm, tn_c), lambda j, i, k: (i, j))),
         (proj, BS((tm, tn_c), lambda j, i, k: (i, j)))],
        [(_sds((N, DS), F32), BS((tm, tn_c), lambda j, i, k: (i, j))),
         (_sds((1, DS), F32), BS((1, tn_c), lambda j, i, k: (0, j)))],
        dy0_epi, deps=[x_mix[4]])

    ds_re, ds_im = _fused_matmul(
        "ssm_out_dx", (N // tm2, NCH, 1),
        [(dy0, BS((tm2, CH), lambda i, c, k: (i, c)), WCre, BS((None, CS, CH), lambda i, c, k: (c, 0, 0)), "nt", 0),
         (dy0, BS((tm2, CH), lambda i, c, k: (i, c)), WCim, BS((None, CS, CH), lambda i, c, k: (c, 0, 0)), "nt", 1)],
        [(tm2, CS), (tm2, CS)], [],
        [(_sds((N, SL), F32), BS((tm2, CS), lambda i, c, k: (i, c)))] * 2,
        _store(lambda accs: accs))
    lam_r, lam_i, d_ab_re, d_ab_im = _scan_bwd(ds_re, ds_im, s_re, s_im, pw_re, pw_im, pwf_re, pwf_im, nseq, L)

    def du_epi(ids, accs, ex, o):
        dy0_ref, d_ref = ex
        o[0][...] = (accs[0] + dy0_ref[...] * d_ref[...]).astype(BF16)

    (du_ssm,) = _fused_matmul(
        "ssm_in_dx", (N // tm2, NCH, 1),
        [(lam_r, BS((tm2, CS), lambda i, c, k: (i, c)), WB, BS((None, CH, CS), lambda i, c, k: (c, 0, 0)), "nt", 0),
         (lam_i, BS((tm2, CS), lambda i, c, k: (i, c)), WB, BS((None, CH, CS), lambda i, c, k: (c, 0, 1)), "nt", 0)],
        [(tm2, CH)],
        [(dy0, BS((tm2, CH), lambda i, c, k: (i, c))), (ssm_d, BS((1, CH), lambda i, c, k: (0, c)))],
        [(_sds((N, DS), BF16), BS((tm2, CH), lambda i, c, k: (i, c)))],
        du_epi)
    dproj = jnp.concatenate([du_ssm, du_pool, dgs, dgp], axis=1)
    (gW_in,) = _fused_matmul(
        "in_proj_dw", (D // tm_d, nq, 1),
        [(a, BS((N, tm_d), lambda i, q, k: (0, i)), dproj, BS((N, U), lambda i, q, k: (0, q)), "tn", 0)],
        [(tm_d, U)], [], [(_sds((NDEV, D, 3 * U), BF16), BS((None, tm_d, U), lambda i, q, k: (q // 3, i, q % 3)))],
        _store(lambda accs: accs))
    x_in = x_start("in", [gW_in])
    (da,) = _fused_matmul(
        "in_proj_dx", (N // tm, D // tn_d2, NDEV // 2),
        [(dproj, BS((tm, 6 * U), lambda i, j, k: (i, k)), Win, BS((2, tn_d2, 3 * U), lambda i, j, k: (k, j, 0)), "nt_cat", 0)],
        [(tm, tn_d2)], [], [(_sds((N, D), F32), BS((tm, tn_d2), lambda i, j, k: (i, j)))],
        _store(lambda accs: accs), deps=[x_in[4]])
    grad_x, dg1 = _pre_bwd(x2, da, dh1, norm_pre_mix)

    tk_n = _tile(N, 1024)
    dWCre, dWCim = _fused_matmul(
        "ssm_out_dw", (NCH, N // tk_n),
        [(s_re, BS((tk_n, CS), lambda c, k: (k, c)), dy0, BS((tk_n, CH), lambda c, k: (k, c)), "tn", 0),
         (s_im, BS((tk_n, CS), lambda c, k: (k, c)), dy0, BS((tk_n, CH), lambda c, k: (k, c)), "tn", 1)],
        [(CS, CH), (CS, CH)], [],
        [(_sds((NCH, CS, CH), F32), BS((None, CS, CH), lambda c, k: (c, 0, 0)))] * 2,
        _store(lambda accs: accs), deps=[x_in[4]])
    dWBre, dWBim = _fused_matmul(
        "ssm_in_dw", (NCH, N // tk_n),
        [(proj, BS((tk_n, CH), lambda c, k: (k, c)), lam_r, BS((tk_n, CS), lambda c, k: (k, c)), "tn", 0),
         (proj, BS((tk_n, CH), lambda c, k: (k, c)), lam_i, BS((tk_n, CS), lambda c, k: (k, c)), "tn", 1)],
        [(CH, CS), (CH, CS)], [],
        [(_sds((NCH, CH, CS), F32), BS((None, CH, CS), lambda c, k: (c, 0, 0)))] * 2,
        _store(lambda accs: accs), deps=[x_in[4]])
    d_bbr = _block_diag_in_grad(dWBre, J, G, P)
    d_bbi = _block_diag_in_grad(dWBim, J, G, P)
    d_lam_re, d_lam_im, d_log_step, d_br_t, d_bi_t = _ssm_param_bwd(
        lam_re, lam_im, log_step, br_t, bi_t,
        d_ab_re.reshape(nseq * SUBLANES, G, P), d_ab_im.reshape(nseq * SUBLANES, G, P), d_bbr, d_bbi)
    d_c_re = _block_diag_out_grad(dWCre, G, J, P)
    d_c_im = -_block_diag_out_grad(dWCim, G, J, P)

    d_conv_w = dcw.reshape(NB, 3, FC).transpose(1, 0, 2).reshape(3, F2)
    d_conv_b = dcb.reshape(1, F2)
    small = {
        "norm_pre_mix": dg1, "norm_post_mix": dg2, "norm_pre_ffn": dg3, "norm_post_ffn": dg4,
        "ssm_lambda_re": d_lam_re[None], "ssm_lambda_im": d_lam_im[None], "ssm_log_step": d_log_step.reshape(1, G),
        "ssm_b_re": d_br_t.transpose(1, 2, 0)[None], "ssm_b_im": d_bi_t.transpose(1, 2, 0)[None],
        "ssm_c_re": d_c_re[None], "ssm_c_im": d_c_im[None],
        "ssm_d": d_ssm_d, "ssm_glu_b": d_glu_b, "pool_scale": d_pscale,
        "pool_b": d_pb.reshape(1, NPG, PG), "ffn_conv_w": d_conv_w[None], "ffn_conv_b": d_conv_b,
    }
    assert list(small) == small_names
    wide = ["ssm_b_re", "ssm_b_im", "ssm_c_re", "ssm_c_im"]
    narrow = [n for n in small_names if n not in wide]
    packed = _pack([lossv] + [small[n] for n in narrow])
    packed_bf = _pack([small[n] for n in wide]).astype(BF16)
    st_small = _split_start("small_start", _broadcast_copies, NDEV - 1, [packed, packed_bf],
                            [_sds((NDEV,) + p.shape, p.dtype) for p in (packed, packed_bf)], packed)

    res = {}
    after = st_small[4]
    for tag, started, group in (("down", x_down, ["w_down"]), ("up", x_up, ["w_up"]),
                                ("branch", x_br, ["w_branch_ssm", "w_branch_pool", "w_out"]),
                                ("mix", x_mix, ["ssm_glu_w", "pool_w"]), ("in", x_in, ["w_in"])):
        for n, parts in zip(group, x_finish(tag, started, after)):
            shape = args[n].shape
            if n == "w_up":
                flat, back = (lambda t: t[0].T), (lambda t: t.T[None])
            else:
                flat, back = (lambda t: t.reshape(-1, shape[-1])), (lambda t: t.reshape(shape))
            w2 = flat(args[n])
            g, dl, nm, nv = _adamw("adamw_" + n, w2, flat(args["m_" + n]), flat(args["v_" + n]),
                                   parts.reshape((NDEV,) + w2.shape))
            res[n] = tuple(back(t) for t in (g, dl, nm, nv))
            after = g

    srcs, lands = _split_wait("small_wait", _broadcast_copies, st_small, after)
    small_all, small_all_bf = (lax.dynamic_update_index_in_dim(l, s, dev, 0) for l, s in zip(lands, srcs))
    loss_rows = (D + SUBLANES * LANES - 1) // (SUBLANES * LANES) * SUBLANES
    total, total_wide, loss = _small_sum(small_all, small_all_bf, loss_rows, D)
    totals = dict(zip(narrow, _unpack(total, [lossv.shape] + [small[n].shape for n in narrow])[1:]))
    totals.update(zip(wide, _unpack(total_wide, [small[n].shape for n in wide])))
    totals["pool_b"] = lax.dynamic_slice_in_dim(totals["pool_b"], dev * (PG // NDEV), PG // NDEV, axis=2)
    totals["ffn_conv_w"] = lax.dynamic_slice_in_dim(totals["ffn_conv_w"], dev * FC, FC, axis=2)
    sm_g = _pack([totals[n] for n in small_names])
    _, sm_d, sm_nm, sm_nv = _adamw("adamw_small", sm_w, sm_m, sm_v, sm_g[None])
    shapes = [args[n].shape for n in small_names]
    for n, dl, nm, nv in zip(small_names, _unpack(sm_d, shapes), _unpack(sm_nm, shapes), _unpack(sm_nv, shapes)):
        res[n] = (totals[n], dl, nm, nv)

    outs = [loss.reshape(()), grad_x.reshape(x.shape)]
    for k in range(4):
        outs += [res[n][k] for n in names]
    return tuple(outs)
```

```python
import functools
import math

import jax
import jax.numpy as jnp
from jax import lax
from jax.experimental import pallas as pl
from jax.experimental.pallas import tpu as pltpu

F32 = jnp.float32
BF16 = jnp.bfloat16
BS = pl.BlockSpec

NDEV = 8
SSM_GROUP = 16
SSM_STATE = 64
GROUPS_PER_CHUNK = 16
SCAN_UNROLL = 8
POOL_WINDOWS = (2, 4, 8, 16)
EPS = 1e-6
MIN_NEG_REAL = -1e-4
ADAM_LR, ADAM_B1, ADAM_B2, ADAM_EPS, ADAM_WD, ADAM_STEP = 0.001, 0.9, 0.999, 1e-08, 0.01, 10
LANES = 128
SUBLANES = 8
VMEM_LIMIT = 56 * 1024 * 1024

_DIMS = {"nn": (((1,), (0,)), ((), ())), "nt": (((1,), (1,)), ((), ())), "tn": (((0,), (0,)), ((), ()))}


def _tile(dim, pref, mult=LANES):
    if dim <= pref:
        return dim
    t = (pref // mult) * mult
    while t >= mult:
        if dim % t == 0:
            return t
        t -= mult
    return dim


def _pc(name, body, grid, ins, in_specs, outs, out_specs, scratch=(), deps=()):
    multi = isinstance(outs, (list, tuple))
    if deps:
        n_in, n_dep, inner = len(ins), len(deps), body

        def body(*refs):
            return inner(*refs[:n_in], *refs[n_in + n_dep:])

        ins = list(ins) + list(deps)
        in_specs = list(in_specs) + [BS(memory_space=pl.ANY)] * n_dep
    return pl.pallas_call(
        body, name=name, grid=grid, in_specs=list(in_specs),
        out_specs=list(out_specs) if multi else out_specs,
        out_shape=list(outs) if multi else outs, scratch_shapes=list(scratch),
        compiler_params=pltpu.CompilerParams(dimension_semantics=("arbitrary",) * len(grid),
                                             vmem_limit_bytes=VMEM_LIMIT),
    )(*ins)


def _sds(shape, dtype):
    return jax.ShapeDtypeStruct(tuple(shape), dtype)


def _gelu(x):
    k = math.sqrt(2.0 / math.pi)
    return 0.5 * x * (1.0 + jnp.tanh(k * (x + 0.044715 * (x * x * x))))


def _gelu_grad(x):
    k = math.sqrt(2.0 / math.pi)
    t = jnp.tanh(k * (x + 0.044715 * (x * x * x)))
    return 0.5 * (1.0 + t) + 0.5 * x * (1.0 - t * t) * (k * (1.0 + 3.0 * 0.044715 * x * x))


def _sigmoid(x):
    return jax.nn.sigmoid(x)


def _fused_matmul(name, grid, pairs, acc_shapes, extras, outs, epilogue, deps=()):
    n_p, n_e, n_o = len(pairs), len(extras), len(outs)
    rank = len(grid)
    nk = grid[-1]
    if nk == 1:
        acc_shapes = []

    def body(*refs):
        ab = refs[:2 * n_p]
        ex = refs[2 * n_p:2 * n_p + n_e]
        o = refs[2 * n_p + n_e:2 * n_p + n_e + n_o]
        accs = refs[2 * n_p + n_e + n_o:]
        ids = [pl.program_id(d) for d in range(rank)]
        k = ids[-1]

        def products():
            sums = {}
            for p in range(n_p):
                a = ab[2 * p][...].astype(BF16)
                mode = pairs[p][4]
                if mode == "nt_cat":
                    b_ref = ab[2 * p + 1]
                    b = jnp.concatenate([b_ref[q].astype(BF16) for q in range(b_ref.shape[0])], axis=1)
                    mode = "nt"
                else:
                    b = ab[2 * p + 1][...].astype(BF16)
                d = lax.dot_general(a, b, _DIMS[mode], preferred_element_type=F32)
                sums[pairs[p][5]] = d if pairs[p][5] not in sums else sums[pairs[p][5]] + d
            return [sums[i] for i in range(len(sums))]

        if nk == 1:
            epilogue(ids, products(), ex, o)
            return

        @pl.when(k == 0)
        def _():
            for acc, s in zip(accs, products()):
                acc[...] = s

        if nk > 2:
            @pl.when(jnp.logical_and(k > 0, k < nk - 1))
            def _():
                for acc, s in zip(accs, products()):
                    acc[...] += s

        @pl.when(k == nk - 1)
        def _():
            epilogue(ids, [acc[...] + s for acc, s in zip(accs, products())], ex, o)

    ins, in_specs = [], []
    for a, a_spec, b, b_spec, _, _ in pairs:
        ins += [a, b]
        in_specs += [a_spec, b_spec]
    for e, e_spec in extras:
        ins.append(e)
        in_specs.append(e_spec)
    res = _pc(name, body, grid, ins, in_specs, [s for s, _ in outs], [sp for _, sp in outs],
              scratch=[pltpu.VMEM(tuple(s), F32) for s in acc_shapes], deps=deps)
    return res


def _store(vals):
    def epilogue(ids, accs, ex, o):
        for r, v in zip(o, vals(accs)):
            r[...] = v.astype(r.dtype)
    return epilogue


def _rowsum_into(ref, first, v):
    s = jnp.sum(v, axis=0, keepdims=True)

    @pl.when(first)
    def _():
        ref[...] = s

    @pl.when(jnp.logical_not(first))
    def _():
        ref[...] += s


def _mesh_pos():
    return lax.axis_index("x"), lax.axis_index("y"), lax.axis_index("c")


def _slot(p):
    return 4 * p[0] + 2 * p[1] + p[2]


_HBM = BS(memory_space=pltpu.HBM)
_SEM = BS(memory_space=pltpu.SEMAPHORE)
_ANY = BS(memory_space=pl.ANY)
_EFFECT = pltpu.SideEffectType.DATAFLOW_SIDE_EFFECTING


def _other_chips(x, y):
    return [(1 - x, y), (x, 1 - y), (1 - x, 1 - y)]


def _all_peers(x, y, c):
    peers = []
    for k in range(1, NDEV):
        kx, ky, kc = (k >> 2) & 1, (k >> 1) & 1, k & 1
        peers.append((1 - x if kx else x, 1 - y if ky else y, 1 - c if kc else c))
    return peers


def _gather_copies(src, land, send_sems, recv_sems, base):
    x, y, c = _mesh_pos()
    return [pltpu.make_async_remote_copy(
        src_ref=src, dst_ref=land.at[_slot((x, y, c))],
        send_sem=send_sems.at[base + k], recv_sem=recv_sems.at[base + k],
        device_id=(*chip, c), device_id_type=pl.DeviceIdType.MESH) for k, chip in enumerate(_other_chips(x, y))]


def _d2d_copies(src, land, send_sems, recv_sems, base):
    x, y, c = _mesh_pos()
    blocks = [(x, y, c)] + [(*chip, c) for chip in _other_chips(x, y)]
    return [pltpu.make_async_remote_copy(
        src_ref=src if k == 0 else land.at[_slot(b)], dst_ref=land.at[_slot(b)],
        send_sem=send_sems.at[base + k], recv_sem=recv_sems.at[base + k],
        device_id=(x, y, 1 - c), device_id_type=pl.DeviceIdType.MESH) for k, b in enumerate(blocks)]


def _broadcast_copies(src, land, send_sems, recv_sems, base):
    x, y, c = _mesh_pos()
    return [pltpu.make_async_remote_copy(
        src_ref=src, dst_ref=land.at[_slot((x, y, c))],
        send_sem=send_sems.at[base + k], recv_sem=recv_sems.at[base + k],
        device_id=peer, device_id_type=pl.DeviceIdType.MESH) for k, peer in enumerate(_all_peers(x, y, c))]


def _exchange_copies(src, land, send_sems, recv_sems, base):
    x, y, c = _mesh_pos()
    return [pltpu.make_async_remote_copy(
        src_ref=src.at[_slot(peer)], dst_ref=land.at[_slot((x, y, c))],
        send_sem=send_sems.at[base + k], recv_sem=recv_sems.at[base + k],
        device_id=peer, device_id_type=pl.DeviceIdType.MESH) for k, peer in enumerate(_all_peers(x, y, c))]


def _split_start(name, copies, ncopy, srcs, land_shapes, after):
    n = len(srcs)

    def body(*refs):
        src_refs, land_refs = refs[:n], refs[n:2 * n]
        send_sems, recv_sems = refs[2 * n + 1], refs[2 * n + 2]
        token = refs[-1]
        for r in range(n):
            for cp in copies(src_refs[r], land_refs[r], send_sems, recv_sems, r * ncopy):
                cp.start()
        token[...] = jnp.zeros_like(token)

    lands = [s if isinstance(s, jax.Array) else pltpu.with_memory_space_constraint(lax.empty(s.shape, s.dtype), pltpu.HBM)
             for s in land_shapes]
    ins = list(srcs) + lands
    out_shape = ([pltpu.SemaphoreType.DMA((n * ncopy,)), pltpu.SemaphoreType.DMA((n * ncopy,))]
                 + [pltpu.HBM(a.shape, a.dtype) for a in lands]
                 + [_sds((SUBLANES, LANES), F32)])
    res = pl.pallas_call(
        body, name=name, out_shape=out_shape,
        in_specs=[_HBM] * (2 * n) + [_ANY], out_specs=[_SEM, _SEM] + [_HBM] * n + [BS(memory_space=pltpu.VMEM)],
        input_output_aliases={n + i: 2 + i for i in range(n)},
        compiler_params=pltpu.CompilerParams(has_side_effects=_EFFECT),
    )(*ins, after)
    return res[0], res[1], list(srcs), list(res[2:2 + n]), res[-1]


def _split_wait(name, copies, started, after):
    send_sems, recv_sems, srcs, lands, _ = started
    n = len(srcs)
    ncopy = send_sems.shape[0] // n
    after = list(after) if isinstance(after, (list, tuple)) else [after]

    def body(*refs):
        src_refs, land_refs = refs[:n], refs[n:2 * n]
        send_sems, recv_sems = refs[2 * n], refs[2 * n + 1]
        for r in range(n):
            for cp in copies(src_refs[r], land_refs[r], send_sems, recv_sems, r * ncopy):
                cp.wait_send()
                cp.wait_recv()

    res = pl.pallas_call(
        body, name=name, out_shape=[pltpu.HBM(a.shape, a.dtype) for a in lands],
        in_specs=[_HBM] * (2 * n) + [_SEM, _SEM] + [_ANY] * len(after), out_specs=[_HBM] * n,
        input_output_aliases={n + i: i for i in range(n)},
        compiler_params=pltpu.CompilerParams(has_side_effects=_EFFECT),
    )(*srcs, *lands, send_sems, recv_sems, *after)
    return list(srcs), list(res)


def _adamw(name, w, m, v, parts):
    R, C = w.shape
    S = parts.shape[0]
    tr = _tile(R, max(SUBLANES, (256 * 1024) // C), SUBLANES)

    def body(w_ref, m_ref, v_ref, p_ref, g_ref, d_ref, nm_ref, nv_ref):
        g = p_ref[0].astype(F32)
        for s in range(1, S):
            g = g + p_ref[s].astype(F32)
        m2 = ADAM_B1 * m_ref[...] + (1.0 - ADAM_B1) * g
        v2 = ADAM_B2 * v_ref[...] + (1.0 - ADAM_B2) * (g * g)
        m_hat = m2 / (1.0 - ADAM_B1 ** ADAM_STEP)
        v_hat = v2 / (1.0 - ADAM_B2 ** ADAM_STEP)
        g_ref[...] = g
        d_ref[...] = -ADAM_LR * (m_hat / (jnp.sqrt(v_hat) + ADAM_EPS) + ADAM_WD * w_ref[...])
        nm_ref[...] = m2
        nv_ref[...] = v2

    blk = BS((tr, C), lambda i: (i, 0))
    return _pc(name, body, (R // tr,), [w, m, v, parts],
               [blk, blk, blk, BS((S, tr, C), lambda i: (0, i, 0))],
               [_sds((R, C), F32)] * 4, [blk] * 4)


def _ssm_disc(lam_re, lam_im, log_step, br_t, bi_t):
    lr = jnp.minimum(lam_re, MIN_NEG_REAL)
    li = lam_im
    dt = jnp.exp(log_step)
    mag = jnp.exp(lr * dt)
    ang = li * dt
    ab_re = mag * jnp.cos(ang)
    ab_im = mag * jnp.sin(ang)
    nr = ab_re - 1.0
    ni = ab_im
    den = lr * lr + li * li
    f_re = (nr * lr + ni * li) / den
    f_im = (ni * lr - nr * li) / den
    bb_re = f_re[None] * br_t - f_im[None] * bi_t
    bb_im = f_re[None] * bi_t + f_im[None] * br_t
    return ab_re, ab_im, bb_re, bb_im


def _ssm_param_fwd(lam_re, lam_im, log_step, br_t, bi_t):
    G, P = lam_re.shape

    def body(lr_ref, li_ref, ls_ref, br_ref, bi_ref, pw_re_ref, pw_im_ref, pwf_re_ref, pwf_im_ref, bbr_ref, bbi_ref):
        ab_re, ab_im, bb_re, bb_im = _ssm_disc(lr_ref[...], li_ref[...], ls_ref[...], br_ref[...], bi_ref[...])
        bbr_ref[...] = bb_re
        bbi_ref[...] = bb_im
        pr, pi = ab_re, ab_im
        for r in range(SUBLANES):
            pw_re_ref[r] = pr
            pw_im_ref[r] = pi
            pwf_re_ref[SUBLANES - 1 - r] = pr
            pwf_im_ref[SUBLANES - 1 - r] = pi
            pr, pi = pr * ab_re - pi * ab_im, pr * ab_im + pi * ab_re

    full = lambda a: BS(a.shape, lambda i: (0,) * a.ndim)
    ins = [lam_re, lam_im, log_step, br_t, bi_t]
    outs = [_sds((SUBLANES, G, P), F32)] * 4 + [_sds(br_t.shape, F32)] * 2
    return _pc("ssm_param_fwd", body, (1,), ins, [full(a) for a in ins], outs, [full(o) for o in outs])


def _ssm_param_bwd(lam_re, lam_im, log_step, br_t, bi_t, d_ab_re, d_ab_im, d_bbr, d_bbi):
    def body(lr_ref, li_ref, ls_ref, br_ref, bi_ref, dar_ref, dai_ref, dbr_ref, dbi_ref,
             o_lr, o_li, o_ls, o_br, o_bi):
        prim = (lr_ref[...], li_ref[...], ls_ref[...], br_ref[...], bi_ref[...])
        _, vjp = jax.vjp(_ssm_disc, *prim)
        dar = dar_ref[0]
        dai = dai_ref[0]
        for k in range(1, dar_ref.shape[0]):
            dar = dar + dar_ref[k]
            dai = dai + dai_ref[k]
        g = vjp((dar, dai, dbr_ref[...], dbi_ref[...]))
        for r, v in zip((o_lr, o_li, o_ls, o_br, o_bi), g):
            r[...] = v

    full = lambda a: BS(a.shape, lambda i: (0,) * a.ndim)
    ins = [lam_re, lam_im, log_step, br_t, bi_t, d_ab_re, d_ab_im, d_bbr, d_bbi]
    outs = [_sds(a.shape, F32) for a in (lam_re, lam_im, log_step, br_t, bi_t)]
    return _pc("ssm_param_bwd", body, (1,), ins, [full(a) for a in ins], outs, [full(o) for o in outs])


def _bcast_row(ref, r, w):
    return jnp.broadcast_to(ref[pl.ds(r, 1), :], (SUBLANES, w))


def _pick_row(x, row, r):
    return jnp.broadcast_to(jnp.sum(jnp.where(row == r, x, 0.0), axis=0, keepdims=True), x.shape)


def _scan_fwd(bu_re, bu_im, pw_re, pw_im, nseq, L):
    N, SL = bu_re.shape
    W = _tile(SL, 256)
    unroll = math.gcd(L // SUBLANES, SCAN_UNROLL)

    def body(bre_ref, bim_ref, pre_ref, pim_ref, sre_ref, sim_ref):
        pre, pim = pre_ref[...], pim_ref[...]
        steps = [(k, _bcast_row(pre_ref, k - 1, W), _bcast_row(pim_ref, k - 1, W)) for k in (1, 2, 4)]
        row = lax.broadcasted_iota(jnp.int32, (SUBLANES, W), 0)

        def step(i, carry):
            cr, ci = carry
            r0 = pl.multiple_of(i * SUBLANES, SUBLANES)
            xr = bre_ref[pl.ds(r0, SUBLANES), :]
            xi = bim_ref[pl.ds(r0, SUBLANES), :]
            for k, ar, ai in steps:
                sr = pltpu.roll(xr, k, axis=0)
                si = pltpu.roll(xi, k, axis=0)
                keep = row >= k
                xr, xi = (xr + jnp.where(keep, ar * sr - ai * si, 0.0),
                          xi + jnp.where(keep, ar * si + ai * sr, 0.0))
            xr, xi = xr + (pre * cr - pim * ci), xi + (pre * ci + pim * cr)
            sre_ref[pl.ds(r0, SUBLANES), :] = xr
            sim_ref[pl.ds(r0, SUBLANES), :] = xi
            return _pick_row(xr, row, SUBLANES - 1), _pick_row(xi, row, SUBLANES - 1)

        def group(g, carry):
            for u in range(unroll):
                carry = step(g * unroll + u, carry)
            return carry

        zero = jnp.zeros((SUBLANES, W), F32)
        lax.fori_loop(0, L // SUBLANES // unroll, group, (zero, zero))

    blk = BS((L, W), lambda s, j: (s, j))
    pw = BS((SUBLANES, W), lambda s, j: (0, j))
    return _pc("ssm_scan_fwd", body, (nseq, SL // W), [bu_re, bu_im, pw_re, pw_im], [blk, blk, pw, pw],
               [_sds((N, SL), F32)] * 2, [blk, blk])


def _scan_bwd(ds_re, ds_im, s_re, s_im, pw_re, pw_im, pwf_re, pwf_im, nseq, L):
    N, SL = ds_re.shape
    W = _tile(SL, 256)
    nt = L // SUBLANES
    unroll = math.gcd(nt, SCAN_UNROLL)

    def body(dsr_ref, dsi_ref, sre_ref, sim_ref, pre_ref, pim_ref, fre_ref, fim_ref,
             lre_ref, lim_ref, dar_ref, dai_ref):
        fre, fim = fre_ref[...], -fim_ref[...]
        steps = [(k, _bcast_row(pre_ref, k - 1, W), -_bcast_row(pim_ref, k - 1, W)) for k in (1, 2, 4)]
        row = lax.broadcasted_iota(jnp.int32, (SUBLANES, W), 0)

        def step(ii, carry):
            cr, ci, acr, aci = carry
            i = nt - 1 - ii
            r0 = pl.multiple_of(i * SUBLANES, SUBLANES)
            xr = dsr_ref[pl.ds(r0, SUBLANES), :]
            xi = dsi_ref[pl.ds(r0, SUBLANES), :]
            for k, ar, ai in steps:
                sr = pltpu.roll(xr, SUBLANES - k, axis=0)
                si = pltpu.roll(xi, SUBLANES - k, axis=0)
                keep = row < SUBLANES - k
                xr, xi = (xr + jnp.where(keep, ar * sr - ai * si, 0.0),
                          xi + jnp.where(keep, ar * si + ai * sr, 0.0))
            xr, xi = xr + (fre * cr - fim * ci), xi + (fre * ci + fim * cr)
            lre_ref[pl.ds(r0, SUBLANES), :] = xr
            lim_ref[pl.ds(r0, SUBLANES), :] = xi
            p0 = pl.multiple_of(jnp.maximum(i - 1, 0) * SUBLANES, SUBLANES)
            has_prev = i > 0
            spr = jnp.where(row == 0,
                            jnp.where(has_prev, pltpu.roll(sre_ref[pl.ds(p0, SUBLANES), :], 1, axis=0), 0.0),
                            pltpu.roll(sre_ref[pl.ds(r0, SUBLANES), :], 1, axis=0))
            spi = jnp.where(row == 0,
                            jnp.where(has_prev, pltpu.roll(sim_ref[pl.ds(p0, SUBLANES), :], 1, axis=0), 0.0),
                            pltpu.roll(sim_ref[pl.ds(r0, SUBLANES), :], 1, axis=0))
            acr = acr + (xr * spr + xi * spi)
            aci = aci + (xi * spr - xr * spi)
            return _pick_row(xr, row, 0), _pick_row(xi, row, 0), acr, aci

        def group(g, carry):
            for u in range(unroll):
                carry = step(g * unroll + u, carry)
            return carry

        zero = jnp.zeros((SUBLANES, W), F32)
        _, _, acr, aci = lax.fori_loop(0, nt // unroll, group, (zero, zero, zero, zero))
        dar_ref[...] = acr
        dai_ref[...] = aci

    blk = BS((L, W), lambda s, j: (s, j))
    pw = BS((SUBLANES, W), lambda s, j: (0, j))
    da = BS((None, SUBLANES, W), lambda s, j: (s, 0, j))
    return _pc("ssm_scan_bwd", body, (nseq, SL // W),
               [ds_re, ds_im, s_re, s_im, pw_re, pw_im, pwf_re, pwf_im], [blk] * 4 + [pw] * 4,
               [_sds((N, SL), F32)] * 2 + [_sds((nseq, SUBLANES, SL), F32)] * 2, [blk, blk, da, da])


def _pool_select(g, vals):
    return jnp.where(g == 0, vals[0], jnp.where(g == 1, vals[1], jnp.where(g == 2, vals[2], vals[3])))


def _pool_fwd(proj, col0, DP, nseq, L):
    N = proj.shape[0]
    PG = DP // len(POOL_WINDOWS)
    W = _tile(PG, 256)

    def body(v_ref, z_ref):
        g = pl.program_id(1) // (PG // W)
        v = v_ref[...]
        row = lax.broadcasted_iota(jnp.int32, (L, W), 0)
        sums, s, k = [], v, 1
        for _ in POOL_WINDOWS:
            s = s + jnp.where(row >= k, pltpu.roll(s, k, axis=0), 0.0)
            sums.append(s)
            k *= 2
        win = _pool_select(g, [float(w) for w in POOL_WINDOWS])
        cnt = jnp.minimum((row + 1).astype(F32), win)
        z_ref[...] = (_pool_select(g, sums) / cnt - v).astype(z_ref.dtype)

    return _pc("pool_fwd", body, (nseq, DP // W), [proj], [BS((L, W), lambda s, j: (s, col0 // W + j))],
               _sds((N, DP), BF16), BS((L, W), lambda s, j: (s, j)))


def _pool_bwd(dz, nseq, L):
    N, DP = dz.shape
    PG = DP // len(POOL_WINDOWS)
    W = _tile(PG, 256)

    def body(dz_ref, dv_ref):
        g = pl.program_id(1) // (PG // W)
        d = dz_ref[...]
        row = lax.broadcasted_iota(jnp.int32, (L, W), 0)
        win = _pool_select(g, [float(w) for w in POOL_WINDOWS])
        s = d / jnp.minimum((row + 1).astype(F32), win)
        sums, k = [], 1
        for _ in POOL_WINDOWS:
            s = s + jnp.where(row < L - k, pltpu.roll(s, L - k, axis=0), 0.0)
            sums.append(s)
            k *= 2
        dv_ref[...] = (_pool_select(g, sums) - d).astype(dv_ref.dtype)

    blk = BS((L, W), lambda s, j: (s, j))
    return _pc("pool_bwd", body, (nseq, DP // W), [dz], [blk], _sds((N, DP), BF16), blk)


def _rstd(x):
    return lax.rsqrt(jnp.mean(x * x, axis=-1, keepdims=True) + EPS)


def _norm_bwd(dy, xhat, rstd, gain):
    t = dy * gain
    return rstd * (t - xhat * jnp.mean(t * xhat, axis=-1, keepdims=True))


def _pre_norm(x, g1):
    N, D = x.shape
    tr = _tile(N, 128, SUBLANES)

    def body(x_ref, g_ref, a_ref):
        xv = x_ref[...]
        a_ref[...] = (xv * _rstd(xv) * g_ref[...]).astype(a_ref.dtype)

    row = BS((tr, D), lambda i: (i, 0))
    vec = BS((1, D), lambda i: (0, 0))
    return _pc("pre_norm", body, (N // tr,), [x, g1], [row, vec], _sds((N, D), BF16), row)


def _mid_norm(x, o, g2, g3):
    N, D = x.shape
    tr = _tile(N, 128, SUBLANES)

    def body(x_ref, o_ref, g2_ref, g3_ref, h1_ref, c_ref):
        ov = o_ref[...]
        h1 = x_ref[...] + ov * _rstd(ov) * g2_ref[...]
        h1_ref[...] = h1
        c_ref[...] = (h1 * _rstd(h1) * g3_ref[...]).astype(c_ref.dtype)

    row = BS((tr, D), lambda i: (i, 0))
    vec = BS((1, D), lambda i: (0, 0))
    return _pc("mid_norm", body, (N // tr,), [x, o, g2, g3], [row, row, vec, vec],
               [_sds((N, D), F32), _sds((N, D), BF16)], [row, row])


def _post_ffn(h1, dn, tgt, g4):
    N, D = h1.shape
    tr = _tile(N, 128, SUBLANES)

    def body(h1_ref, dn_ref, t_ref, g_ref, dh2_ref, ddn_ref, lossv_ref, dg_ref):
        first = pl.program_id(0) == 0
        dnv = dn_ref[...]
        rstd = _rstd(dnv)
        xhat = dnv * rstd
        gain = g_ref[...]
        err = (h1_ref[...] + xhat * gain) - t_ref[...]
        dh2 = err / float(D)
        dh2_ref[...] = dh2
        ddn_ref[...] = _norm_bwd(dh2, xhat, rstd, gain).astype(ddn_ref.dtype)
        _rowsum_into(lossv_ref, first, err * err)
        _rowsum_into(dg_ref, first, dh2 * xhat)

    row = BS((tr, D), lambda i: (i, 0))
    vec = BS((1, D), lambda i: (0, 0))
    return _pc("post_ffn", body, (N // tr,), [h1, dn, tgt, g4], [row, row, row, vec],
               [_sds((N, D), F32), _sds((N, D), BF16), _sds((1, D), F32), _sds((1, D), F32)], [row, row, vec, vec])


def _mid_bwd(dh2, dc, h1, o, g2, g3, deps=()):
    N, D = h1.shape
    tr = _tile(N, 128, SUBLANES)

    def body(dh2_ref, dc_ref, h1_ref, o_ref, g2_ref, g3_ref, dh1_ref, do_ref, dg2_ref, dg3_ref):
        first = pl.program_id(0) == 0
        h1 = h1_ref[...]
        r3 = _rstd(h1)
        hc = h1 * r3
        dcv = dc_ref[...]
        dh1 = dh2_ref[...] + _norm_bwd(dcv, hc, r3, g3_ref[...])
        dh1_ref[...] = dh1
        ov = o_ref[...]
        r2 = _rstd(ov)
        ho = ov * r2
        do_ref[...] = _norm_bwd(dh1, ho, r2, g2_ref[...]).astype(do_ref.dtype)
        _rowsum_into(dg3_ref, first, dcv * hc)
        _rowsum_into(dg2_ref, first, dh1 * ho)

    row = BS((tr, D), lambda i: (i, 0))
    vec = BS((1, D), lambda i: (0, 0))
    return _pc("mid_bwd", body, (N // tr,), [dh2, dc, h1, o, g2, g3], [row] * 4 + [vec, vec],
               [_sds((N, D), F32), _sds((N, D), BF16), _sds((1, D), F32), _sds((1, D), F32)], [row, row, vec, vec],
               deps=deps)


def _pre_bwd(x, da, dh1, g1):
    N, D = x.shape
    tr = _tile(N, 128, SUBLANES)

    def body(x_ref, da_ref, dh1_ref, g_ref, dx_ref, dg_ref):
        first = pl.program_id(0) == 0
        xv = x_ref[...]
        r1 = _rstd(xv)
        xh = xv * r1
        dav = da_ref[...]
        dx_ref[...] = dh1_ref[...] + _norm_bwd(dav, xh, r1, g_ref[...])
        _rowsum_into(dg_ref, first, dav * xh)

    row = BS((tr, D), lambda i: (i, 0))
    vec = BS((1, D), lambda i: (0, 0))
    return _pc("pre_bwd", body, (N // tr,), [x, da, dh1, g1], [row, row, row, vec],
               [_sds((N, D), F32), _sds((1, D), F32)], [row, vec])


def _conv_rows(x_ref, halo_ref, first):
    x = x_ref[...]
    xx = jnp.concatenate([jnp.where(first, 0.0, halo_ref[...]), x], axis=0)
    x1 = pltpu.roll(xx, 1, axis=0)[SUBLANES:]
    x2 = pltpu.roll(xx, 2, axis=0)[SUBLANES:]
    return x, x1, x2


def _conv_apply(rows, w_ref, b_ref):
    x, x1, x2 = rows
    return ((b_ref[...] + x2 * w_ref[pl.ds(0, 1), :]) + x1 * w_ref[pl.ds(1, 1), :]) + x * w_ref[pl.ds(2, 1), :]


def _gate_specs(N, FC, TR, half):
    tile = BS((None, TR, FC), lambda jj, i: (jj + half, i, 0))
    halo = BS((None, SUBLANES, FC), lambda jj, i: (jj + half, jnp.maximum(i * (TR // SUBLANES) - 1, 0), 0))
    cw = BS((None, 3, FC), lambda jj, i: (jj + half, 0, 0))
    cb = BS((None, 1, FC), lambda jj, i: (jj + half, 0, 0))
    return tile, halo, cw, cb


def _gate_fwd(up_pre, cw, cb, L, deps=()):
    nb, N, FC = up_pre.shape
    half = nb // 2
    TR = _tile(L, 128, SUBLANES)

    def body(xa_ref, ha_ref, wa_ref, ba_ref, xb_ref, hb_ref, wb_ref, bb_ref, f_ref):
        first = (pl.program_id(1) % (L // TR)) == 0
        ua = _conv_apply(_conv_rows(xa_ref, ha_ref, first), wa_ref, ba_ref)
        ub = _conv_apply(_conv_rows(xb_ref, hb_ref, first), wb_ref, bb_ref)
        f_ref[...] = (_gelu(ua) * ub).astype(f_ref.dtype)

    sa, sb = _gate_specs(N, FC, TR, 0), _gate_specs(N, FC, TR, half)
    return _pc("gate_fwd", body, (half, N // TR), [up_pre, up_pre, cw, cb] * 2, list(sa) + list(sb),
               _sds((half, N, FC), BF16), BS((None, TR, FC), lambda jj, i: (jj, i, 0)), deps=deps)


def _gate_bwd(up_pre, cw, cb, df, L, deps=()):
    nb, N, FC = up_pre.shape
    half = nb // 2
    TR = _tile(L, 128, SUBLANES)

    def body(xa_ref, ha_ref, wa_ref, ba_ref, xb_ref, hb_ref, wb_ref, bb_ref, df_ref, dup_ref, dw_ref, dbias_ref):
        i = pl.program_id(1)
        first_row = i == 0
        first = (i % (L // TR)) == 0
        ra = _conv_rows(xa_ref, ha_ref, first)
        rb = _conv_rows(xb_ref, hb_ref, first)
        ua = _conv_apply(ra, wa_ref, ba_ref)
        ub = _conv_apply(rb, wb_ref, bb_ref)
        dfv = df_ref[...].astype(F32)
        dua = dfv * ub * _gelu_grad(ua)
        dub = dfv * _gelu(ua)
        dup_ref[0] = dua.astype(dup_ref.dtype)
        dup_ref[1] = dub.astype(dup_ref.dtype)
        for h, (rows, du) in enumerate(((ra, dua), (rb, dub))):
            x, x1, x2 = rows
            _rowsum_into(dbias_ref.at[h], first_row, du)
            for k, xs in enumerate((x2, x1, x)):
                _rowsum_into(dw_ref.at[h, pl.ds(k, 1), :], first_row, du * xs)

    sa, sb = _gate_specs(N, FC, TR, 0), _gate_specs(N, FC, TR, half)
    tile = BS((None, TR, FC), lambda jj, i: (jj, i, 0))
    both = BS((2, None, TR, FC), lambda jj, i: (0, jj, i, 0))
    dw = BS((2, None, 3, FC), lambda jj, i: (0, jj, 0, 0))
    dbias = BS((2, None, 1, FC), lambda jj, i: (0, jj, 0, 0))
    return _pc("gate_bwd", body, (half, N // TR), [up_pre, up_pre, cw, cb] * 2 + [df], list(sa) + list(sb) + [tile],
               [_sds((2, half, N, FC), BF16), _sds((2, half, 3, FC), F32), _sds((2, half, 1, FC), F32)],
               [both, dw, dbias], deps=deps)


def _conv_bwd(dup, cw, L):
    nb, N, FC = dup.shape
    TR = _tile(L, 128, 2 * SUBLANES)
    HR = 2 * SUBLANES
    nrb = N // HR

    def body(x_ref, h_ref, w_ref, o_ref):
        last = ((pl.program_id(1) + 1) % (L // TR)) == 0
        x = x_ref[...].astype(F32)
        xx = jnp.concatenate([x, jnp.where(last, 0.0, h_ref[...].astype(F32))], axis=0)
        x1 = pltpu.roll(xx, TR + HR - 1, axis=0)[:TR]
        x2 = pltpu.roll(xx, TR + HR - 2, axis=0)[:TR]
        o_ref[...] = (x * w_ref[pl.ds(2, 1), :] + x1 * w_ref[pl.ds(1, 1), :] + x2 * w_ref[pl.ds(0, 1), :]
                      ).astype(o_ref.dtype)

    tile = BS((None, TR, FC), lambda jj, i: (jj, i, 0))
    halo = BS((None, HR, FC), lambda jj, i: (jj, jnp.minimum((i + 1) * (TR // HR), nrb - 1), 0))
    w = BS((None, 3, FC), lambda jj, i: (jj, 0, 0))
    return _pc("conv_bwd", body, (nb, N // TR), [dup, dup, cw], [tile, halo, w], _sds((nb, N, FC), BF16), tile)


def _pack(arrs):
    parts = []
    for a in arrs:
        flat = a.reshape(-1).astype(F32)
        pad = (-flat.shape[0]) % (SUBLANES * LANES)
        parts.append(jnp.pad(flat, (0, pad)))
    return jnp.concatenate(parts).reshape(-1, LANES)


def _unpack(packed, shapes):
    flat = packed.reshape(-1)
    out, off = [], 0
    for s in shapes:
        n = math.prod(s)
        out.append(flat[off:off + n].reshape(s))
        off += n + ((-n) % (SUBLANES * LANES))
    return out


def _small_sum(gathered, gathered_bf, loss_rows, d_model):
    S, R, C = gathered.shape
    R2 = gathered_bf.shape[1]

    def body(p_ref, q_ref, tot_ref, tot2_ref, loss_ref):
        t = p_ref[0]
        u = q_ref[0].astype(F32)
        for s in range(1, S):
            t = t + p_ref[s]
            u = u + q_ref[s].astype(F32)
        tot_ref[...] = t
        tot2_ref[...] = u
        loss_ref[...] = jnp.full((1, 1), 0.5 / d_model, F32) * jnp.sum(t[:loss_rows])

    return _pc("small_sum", body, (1,), [gathered, gathered_bf],
               [BS((S, R, C), lambda i: (0, 0, 0)), BS((S, R2, C), lambda i: (0, 0, 0))],
               [_sds((R, C), F32), _sds((R2, C), F32), _sds((1, 1), F32)],
               [BS((R, C), lambda i: (0, 0)), BS((R2, C), lambda i: (0, 0)), BS((1, 1), lambda i: (0, 0))])


def _block_diag_in(bb_t, nch):
    J, G, P = bb_t.shape
    gl = G // nch
    b = bb_t.reshape(J, nch, gl, P).transpose(1, 0, 2, 3)
    eye = jnp.eye(gl, dtype=F32)
    w = eye[None, :, None, :, None] * b[:, None, :, :, :]
    return w.reshape(nch, gl * J, gl * P)


def _block_diag_in_grad(dw, J, G, P):
    nch = dw.shape[0]
    gl = G // nch
    d = dw.reshape(nch, gl, J, gl, P)
    d = jnp.einsum("cgjgp->jcgp", d)
    return d.reshape(J, G, P)


def _block_diag_out(c, nch):
    G, J, P = c.shape
    gl = G // nch
    cc = c.reshape(nch, gl, J, P).transpose(0, 1, 3, 2)
    eye = jnp.eye(gl, dtype=F32)
    w = cc[:, :, :, None, :] * eye[None, :, None, :, None]
    return w.reshape(nch, gl * P, gl * J)


def _block_diag_out_grad(dw, G, J, P):
    nch = dw.shape[0]
    gl = G // nch
    d = dw.reshape(nch, gl, P, gl, J)
    d = jnp.einsum("cgpgj->cgjp", d)
    return d.reshape(G, J, P)


def kernel(x, norm_pre_mix, w_in, ssm_lambda_re, ssm_lambda_im, ssm_log_step, ssm_b_re, ssm_b_im, ssm_c_re, ssm_c_im, ssm_d, ssm_glu_w, ssm_glu_b, pool_w, pool_b, pool_scale, w_branch_ssm, w_branch_pool, w_out, norm_post_mix, norm_pre_ffn, w_up, ffn_conv_w, ffn_conv_b, w_down, norm_post_ffn, loss_target, m_norm_pre_mix, m_w_in, m_ssm_lambda_re, m_ssm_lambda_im, m_ssm_log_step, m_ssm_b_re, m_ssm_b_im, m_ssm_c_re, m_ssm_c_im, m_ssm_d, m_ssm_glu_w, m_ssm_glu_b, m_pool_w, m_pool_b, m_pool_scale, m_w_branch_ssm, m_w_branch_pool, m_w_out, m_norm_post_mix, m_norm_pre_ffn, m_w_up, m_ffn_conv_w, m_ffn_conv_b, m_w_down, m_norm_post_ffn, v_norm_pre_mix, v_w_in, v_ssm_lambda_re, v_ssm_lambda_im, v_ssm_log_step, v_ssm_b_re, v_ssm_b_im, v_ssm_c_re, v_ssm_c_im, v_ssm_d, v_ssm_glu_w, v_ssm_glu_b, v_pool_w, v_pool_b, v_pool_scale, v_w_branch_ssm, v_w_branch_pool, v_w_out, v_norm_post_mix, v_norm_pre_ffn, v_w_up, v_ffn_conv_w, v_ffn_conv_b, v_w_down, v_norm_post_ffn):
    args = dict(locals())
    names = ["norm_pre_mix", "w_in", "ssm_lambda_re", "ssm_lambda_im", "ssm_log_step", "ssm_b_re", "ssm_b_im",
             "ssm_c_re", "ssm_c_im", "ssm_d", "ssm_glu_w", "ssm_glu_b", "pool_w", "pool_b", "pool_scale",
             "w_branch_ssm", "w_branch_pool", "w_out", "norm_post_mix", "norm_pre_ffn", "w_up", "ffn_conv_w",
             "ffn_conv_b", "w_down", "norm_post_ffn"]

    nseq, L, D = x.shape
    N = nseq * L
    U = D // NDEV
    DS = ssm_d.shape[1]
    DP = pool_scale.shape[1]
    G, P, J = ssm_b_re.shape[1:]
    SL = G * P
    CH = GROUPS_PER_CHUNK * J
    CS = GROUPS_PER_CHUNK * P
    NCH = DS // CH
    NPG = len(POOL_WINDOWS)
    PG = DP // NPG
    FC = w_up.shape[2]
    NB = NDEV
    HB = NB // 2
    F2 = NB * FC
    dev = _slot(_mesh_pos())
    tm = _tile(N, 1024)
    tm2 = _tile(N, 512)

    x2 = x.reshape(N, D)
    tgt = loss_target.reshape(N, D)

    def bf(t):
        return t.astype(BF16)

    def g_start(tag, group, after):
        return _split_start("gather_start_" + tag, _gather_copies, 3, group,
                            [_sds((NDEV,) + s.shape, s.dtype) for s in group], after)

    def g_land(tag, started, after):
        srcs, lands = _split_wait("gather_wait_" + tag, _gather_copies, started, after)
        return _split_start("d2d_start_" + tag, _d2d_copies, 4, srcs, lands, srcs[0])

    def g_finish(tag, d2d, after):
        srcs, lands = _split_wait("d2d_wait_" + tag, _d2d_copies, d2d, after)
        return [lax.dynamic_update_index_in_dim(l, s, dev, 0) for l, s in zip(lands, srcs)]

    def x_start(tag, group):
        return _split_start("exchange_start_" + tag, _exchange_copies, NDEV - 1, group,
                            [_sds(g.shape, g.dtype) for g in group], group[0])

    def x_finish(tag, started, after):
        srcs, lands = _split_wait("exchange_wait_" + tag, _exchange_copies, started, after)
        own = [lax.dynamic_index_in_dim(s, dev, 0, keepdims=False) for s in srcs]
        return [lax.dynamic_update_index_in_dim(l, o, dev, 0) for l, o in zip(lands, own)]

    st_in = g_start("in", [bf(w_in[0])], x2)
    st_mix = g_start("mix", [bf(ssm_glu_w[0]), bf(pool_w[0]), pool_b[0], ffn_conv_w[0]], st_in[4])
    conv_b_blk = ffn_conv_b.reshape(NB, 1, FC)

    (_, x2e, w_branch_ssm, w_branch_pool, w_out, w_up, w_down, ssm_lambda_re, ssm_lambda_im, ssm_log_step,
     ssm_b_re, ssm_b_im, ssm_c_re, ssm_c_im) = lax.optimization_barrier(
        (st_mix[4], x2, w_branch_ssm, w_branch_pool, w_out, w_up, w_down, ssm_lambda_re, ssm_lambda_im, ssm_log_step,
         ssm_b_re, ssm_b_im, ssm_c_re, ssm_c_im))
    lam_re, lam_im = ssm_lambda_re[0], ssm_lambda_im[0]
    log_step = ssm_log_step.reshape(G, 1)
    br_t = ssm_b_re[0].transpose(2, 0, 1)
    bi_t = ssm_b_im[0].transpose(2, 0, 1)
    pw_re3, pw_im3, pwf_re3, pwf_im3, bb_re, bb_im = _ssm_param_fwd(lam_re, lam_im, log_step, br_t, bi_t)
    pw_re, pw_im = pw_re3.reshape(SUBLANES, SL), pw_im3.reshape(SUBLANES, SL)
    pwf_re, pwf_im = pwf_re3.reshape(SUBLANES, SL), pwf_im3.reshape(SUBLANES, SL)
    WB = jnp.concatenate([_block_diag_in(bb_re, NCH), _block_diag_in(bb_im, NCH)], axis=2).astype(BF16)
    WCre = _block_diag_out(ssm_c_re[0], NCH).astype(BF16)
    WCim = _block_diag_out(-ssm_c_im[0], NCH).astype(BF16)
    a = _pre_norm(x2e, norm_pre_mix)
    small_names = ["norm_pre_mix", "norm_post_mix", "norm_pre_ffn", "norm_post_ffn", "ssm_lambda_re", "ssm_lambda_im",
                   "ssm_log_step", "ssm_b_re", "ssm_b_im", "ssm_c_re", "ssm_c_im", "ssm_d", "ssm_glu_b", "pool_scale",
                   "pool_b", "ffn_conv_w", "ffn_conv_b"]
    _, small_in = lax.optimization_barrier(
        (st_mix[4], [[args[p + n] for n in small_names] for p in ("", "m_", "v_")]))
    sm_w, sm_m, sm_v = (_pack(group) for group in small_in)
    g_br = [bf(w_branch_ssm[0]), bf(w_branch_pool[0]), bf(w_out[0])]
    g_up, g_down = [bf(w_up[0].T)], [bf(w_down[0])]
    early = [WB, WCre, WCim, pwf_re, pwf_im, a, sm_w, sm_m, sm_v] + g_br + g_up + g_down

    d_in = g_land("in", st_in, [st_mix[4]] + early)
    st_br = g_start("branch", g_br, d_in[4])
    st_up = g_start("up", g_up, st_br[4])
    st_down = g_start("down", g_down, st_up[4])
    (Win,) = g_finish("in", d_in, st_down[4])

    nq = 3 * NDEV
    (proj,) = _fused_matmul(
        "in_proj", (N // tm, nq, 1),
        [(a, BS((tm, D), lambda i, q, k: (i, 0)), Win, BS((None, D, U), lambda i, q, k: (q // 3, 0, q % 3)), "nn", 0)],
        [(tm, U)], [], [(_sds((N, 3 * D), F32), BS((tm, U), lambda i, q, k: (i, q)))],
        _store(lambda accs: accs))
    d_mix = g_land("mix", st_mix, proj)
    d_br = g_land("branch", st_br, d_mix[4])

    bu_re, bu_im = _fused_matmul(
        "ssm_in", (N // tm2, NCH, 1),
        [(proj, BS((tm2, CH), lambda i, c, k: (i, c)), WB, BS((None, CH, 2 * CS), lambda i, c, k: (c, 0, 0)), "nn", 0)],
        [(tm2, 2 * CS)], [],
        [(_sds((N, SL), F32), BS((tm2, CS), lambda i, c, k: (i, c)))] * 2,
        _store(lambda accs: (accs[0][:, :CS], accs[0][:, CS:])), deps=[d_br[4]])
    s_re, s_im = _scan_fwd(bu_re, bu_im, pw_re, pw_im, nseq, L)

    def ssm_out_epi(ids, accs, ex, o):
        u_ref, d_ref = ex
        y0 = accs[0] + d_ref[...] * u_ref[...]
        o[0][...] = y0
        o[1][...] = _gelu(y0).astype(BF16)

    y0, y1 = _fused_matmul(
        "ssm_out", (N // tm2, NCH, 1),
        [(s_re, BS((tm2, CS), lambda i, c, k: (i, c)), WCre, BS((None, CS, CH), lambda i, c, k: (c, 0, 0)), "nn", 0),
         (s_im, BS((tm2, CS), lambda i, c, k: (i, c)), WCim, BS((None, CS, CH), lambda i, c, k: (c, 0, 0)), "nn", 0)],
        [(tm2, CH)],
        [(proj, BS((tm2, CH), lambda i, c, k: (i, c))), (ssm_d, BS((1, CH), lambda i, c, k: (0, c)))],
        [(_sds((N, DS), F32), BS((tm2, CH), lambda i, c, k: (i, c))),
         (_sds((N, DS), BF16), BS((tm2, CH), lambda i, c, k: (i, c)))],
        ssm_out_epi)

    Wglu, Wpool, pool_b_all, conv_w_all = g_finish("mix", d_mix, y1)
    Wglu = Wglu.reshape(DS, DS)
    Wpool = Wpool.transpose(1, 0, 2, 3).reshape(NPG, PG, PG)
    pool_b_full = pool_b_all.transpose(1, 0, 2).reshape(1, DP)
    tn_s = _tile(DS, 512)

    def glu_epi(ids, accs, ex, o):
        y0_ref, b_ref = ex
        zg = accs[0] + b_ref[...]
        o[0][...] = zg
        o[1][...] = (_gelu(y0_ref[...]) * _sigmoid(zg)).astype(BF16)

    zg, ys = _fused_matmul(
        "ssm_glu", (N // tm, DS // tn_s, 1),
        [(y1, BS((tm, DS), lambda i, j, k: (i, 0)), Wglu, BS((DS, tn_s), lambda i, j, k: (0, j)), "nn", 0)],
        [(tm, tn_s)],
        [(y0, BS((tm, tn_s), lambda i, j, k: (i, j))), (ssm_glu_b, BS((1, tn_s), lambda i, j, k: (0, j)))],
        [(_sds((N, DS), F32), BS((tm, tn_s), lambda i, j, k: (i, j))),
         (_sds((N, DS), BF16), BS((tm, tn_s), lambda i, j, k: (i, j)))],
        glu_epi)

    z = _pool_fwd(proj, DS, DP, nseq, L)

    def pool_mm_epi(ids, accs, ex, o):
        b_ref, sc_ref = ex
        q = accs[0] + b_ref[...]
        o[0][...] = q
        o[1][...] = (q * sc_ref[...]).astype(BF16)

    qp, yp = _fused_matmul(
        "pool_mm", (N // tm, NPG, 1),
        [(z, BS((tm, PG), lambda i, g, k: (i, g)), Wpool, BS((None, PG, PG), lambda i, g, k: (g, 0, 0)), "nn", 0)],
        [(tm, PG)],
        [(pool_b_full, BS((1, PG), lambda i, g, k: (0, g))), (pool_scale, BS((1, PG), lambda i, g, k: (0, g)))],
        [(_sds((N, DP), F32), BS((tm, PG), lambda i, g, k: (i, g))),
         (_sds((N, DP), BF16), BS((tm, PG), lambda i, g, k: (i, g)))],
        pool_mm_epi)

    Wbs, Wbp, Wout = g_finish("branch", d_br, yp)
    Wout = Wout.reshape(D, D)
    gs_blk = BS((tm2, U), lambda i, q, k: (i, (DS + DP) // U + q))
    gp_blk = BS((tm2, U), lambda i, q, k: (i, (DS + DP + D) // U + q))
    out_blk = BS((tm2, U), lambda i, q, k: (i, q))

    def branch_epi(ids, accs, ex, o):
        gs_ref, gp_ref = ex
        o[0][...] = accs[0].astype(BF16)
        o[1][...] = accs[1].astype(BF16)
        o[2][...] = (_sigmoid(gs_ref[...]) * accs[0] + _sigmoid(gp_ref[...]) * accs[1]).astype(BF16)

    Ys, Yp, merged = _fused_matmul(
        "branch", (N // tm2, NDEV, 1),
        [(ys, BS((tm2, DS), lambda i, q, k: (i, 0)), Wbs, BS((None, DS, U), lambda i, q, k: (q, 0, 0)), "nn", 0),
         (yp, BS((tm2, DP), lambda i, q, k: (i, 0)), Wbp, BS((None, DP, U), lambda i, q, k: (q, 0, 0)), "nn", 1)],
        [(tm2, U), (tm2, U)],
        [(proj, gs_blk), (proj, gp_blk)],
        [(_sds((N, D), BF16), out_blk), (_sds((N, D), BF16), out_blk), (_sds((N, D), BF16), out_blk)],
        branch_epi)
    d_up = g_land("up", st_up, merged)

    tn_d = _tile(D, 512)
    (o_mix,) = _fused_matmul(
        "out_proj", (N // tm, D // tn_d, 1),
        [(merged, BS((tm, D), lambda i, j, k: (i, 0)), Wout, BS((D, tn_d), lambda i, j, k: (0, j)), "nn", 0)],
        [(tm, tn_d)], [], [(_sds((N, D), F32), BS((tm, tn_d), lambda i, j, k: (i, j)))],
        _store(lambda accs: accs), deps=[d_up[4]])
    h1, c = _mid_norm(x2, o_mix, norm_post_mix, norm_pre_ffn)

    (Wup,) = g_finish("up", d_up, c)
    tk_up = _tile(D, 2048)
    (up_pre,) = _fused_matmul(
        "ffn_up", (N // tm2, NB, D // tk_up),
        [(c, BS((tm2, tk_up), lambda i, j, k: (i, k)), Wup, BS((None, FC, tk_up), lambda i, j, k: (j, 0, k)), "nt", 0)],
        [(tm2, FC)], [], [(_sds((NB, N, FC), F32), BS((None, tm2, FC), lambda i, j, k: (j, i, 0)))],
        _store(lambda accs: accs))
    d_down = g_land("down", st_down, up_pre)
    f = _gate_fwd(up_pre, conv_w_all, conv_b_blk, L, deps=[d_down[4]])
    (Wdown,) = g_finish("down", d_down, f)
    Wdown = Wdown.reshape(HB, FC, D)
    tn_d2 = _tile(D, 1024)
    (dn,) = _fused_matmul(
        "ffn_down", (N // tm2, D // tn_d, 1),
        [(f, BS((None, tm2, FC), functools.partial(lambda b, i, j, k: (b, i, 0), b)), Wdown,
          BS((None, FC, tn_d), functools.partial(lambda b, i, j, k: (b, 0, j), b)), "nn", 0) for b in range(HB)],
        [(tm2, tn_d)], [], [(_sds((N, D), F32), BS((tm2, tn_d), lambda i, j, k: (i, j)))],
        _store(lambda accs: accs))
    dh2, d_dn, lossv, dg4 = _post_ffn(h1, dn, tgt, norm_post_ffn)

    tk_d = _tile(D, 2048)
    (df,) = _fused_matmul(
        "ffn_down_dx", (N // tm2, HB, D // tk_d),
        [(d_dn, BS((tm2, tk_d), lambda i, j, k: (i, k)), Wdown, BS((None, FC, tk_d), lambda i, j, k: (j, 0, k)), "nt", 0)],
        [(tm2, FC)], [], [(_sds((HB, N, FC), BF16), BS((None, tm2, FC), lambda i, j, k: (j, i, 0)))],
        _store(lambda accs: accs))
    tk_n = _tile(N, 2048)
    (gW_down,) = _fused_matmul(
        "ffn_down_dw", (HB, D // tn_d, N // tk_n),
        [(f, BS((None, tk_n, FC), lambda j, n, k: (j, k, 0)), d_dn, BS((tk_n, tn_d), lambda j, n, k: (k, n)), "tn", 0)],
        [(FC, tn_d)], [], [(_sds((HB, FC, D), BF16), BS((None, FC, tn_d), lambda j, n, k: (j, 0, n)))],
        _store(lambda accs: accs))
    x_down = x_start("down", [gW_down.reshape(NDEV, FC // 2, D)])
    dup, dcw, dcb = _gate_bwd(up_pre, conv_w_all, conv_b_blk, df, L, deps=[x_down[4]])
    dpre = _conv_bwd(dup.reshape(NB, N, FC), conv_w_all, L)
    (dc,) = _fused_matmul(
        "ffn_up_dx", (N // tm2, D // tn_d2, NB // 2),
        [(dpre, BS((None, tm2, FC), lambda i, j, k: (2 * k, i, 0)), Wup, BS((None, FC, tn_d2), lambda i, j, k: (2 * k, 0, j)), "nn", 0),
         (dpre, BS((None, tm2, FC), lambda i, j, k: (2 * k + 1, i, 0)), Wup, BS((None, FC, tn_d2), lambda i, j, k: (2 * k + 1, 0, j)), "nn", 0)],
        [(tm2, tn_d2)], [], [(_sds((N, D), F32), BS((tm2, tn_d2), lambda i, j, k: (i, j)))],
        _store(lambda accs: accs))
    tm_d = _tile(D, 512)
    (gW_up,) = _fused_matmul(
        "ffn_up_dw", (NB, D // tm_d, N // tk_n),
        [(dpre, BS((None, tk_n, FC), lambda j, n, k: (j, k, 0)), c, BS((tk_n, tm_d), lambda j, n, k: (k, n)), "tn", 0)],
        [(FC, tm_d)], [], [(_sds((NB, FC, D), BF16), BS((None, FC, tm_d), lambda j, n, k: (j, 0, n)))],
        _store(lambda accs: accs))
    x_up = x_start("up", [gW_up])

    dh1, d_o, dg2, dg3 = _mid_bwd(dh2, dc, h1, o_mix, norm_post_mix, norm_pre_ffn, deps=[x_up[4]])

    def dmerged_epi(ids, accs, ex, o):
        gs_ref, gp_ref, ys_ref, yp_ref = ex
        dm = accs[0]
        sg_s, sg_p = _sigmoid(gs_ref[...]), _sigmoid(gp_ref[...])
        o[0][...] = (dm * sg_s).astype(BF16)
        o[1][...] = (dm * sg_p).astype(BF16)
        o[2][...] = (dm * ys_ref[...].astype(F32) * sg_s * (1.0 - sg_s)).astype(BF16)
        o[3][...] = (dm * yp_ref[...].astype(F32) * sg_p * (1.0 - sg_p)).astype(BF16)

    dYs, dYp, dgs, dgp = _fused_matmul(
        "out_proj_dx", (N // tm2, NDEV, 1),
        [(d_o, BS((tm2, D), lambda i, q, k: (i, 0)), Wout, BS((U, D), lambda i, q, k: (q, 0)), "nt", 0)],
        [(tm2, U)],
        [(proj, gs_blk), (proj, gp_blk), (Ys, out_blk), (Yp, out_blk)],
        [(_sds((N, D), BF16), out_blk)] * 4,
        dmerged_epi)
    (gW_out,) = _fused_matmul(
        "out_proj_dw", (D // tm_d, D // tn_d, 1),
        [(merged, BS((N, tm_d), lambda i, j, k: (0, i)), d_o, BS((N, tn_d), lambda i, j, k: (0, j)), "tn", 0)],
        [(tm_d, tn_d)], [], [(_sds((D, D), BF16), BS((tm_d, tn_d), lambda i, j, k: (i, j)))],
        _store(lambda accs: accs))
    tm_s = _tile(DS, 512)
    gW_bs, gW_bp = _fused_matmul(
        "branch_dw", (DS // tm_s, NDEV, 1),
        [(ys, BS((N, tm_s), lambda i, q, k: (0, i)), dYs, BS((N, U), lambda i, q, k: (0, q)), "tn", 0),
         (yp, BS((N, tm_s), lambda i, q, k: (0, i)), dYp, BS((N, U), lambda i, q, k: (0, q)), "tn", 1)],
        [(tm_s, U), (tm_s, U)], [],
        [(_sds((NDEV, DS, U), BF16), BS((None, tm_s, U), lambda i, q, k: (q, i, 0)))] * 2,
        _store(lambda accs: accs))
    x_br = x_start("branch", [gW_bs, gW_bp, gW_out.reshape(NDEV, U, D)])

    tn_p = _tile(PG, 512)

    def dyp_epi(ids, accs, ex, o):
        q_ref, sc_ref = ex
        first = ids[1] == 0
        dyp = accs[0]
        dq = dyp * sc_ref[...]
        o[0][...] = dq.astype(BF16)
        _rowsum_into(o[1], first, dyp * q_ref[...])
        _rowsum_into(o[2], first, dq)

    dq, d_pscale, d_pb = _fused_matmul(
        "branch_pool_dx", (DP // tn_p, N // tm, 1),
        [(dYp, BS((tm, D), lambda j, i, k: (i, 0)), Wbp, BS((NDEV, tn_p, U), lambda j, i, k: (0, j, 0)), "nt_cat", 0)],
        [(tm, tn_p)],
        [(qp, BS((tm, tn_p), lambda j, i, k: (i, j))), (pool_scale, BS((1, tn_p), lambda j, i, k: (0, j)))],
        [(_sds((N, DP), BF16), BS((tm, tn_p), lambda j, i, k: (i, j))),
         (_sds((1, DP), F32), BS((1, tn_p), lambda j, i, k: (0, j))),
         (_sds((1, DP), F32), BS((1, tn_p), lambda j, i, k: (0, j)))],
        dyp_epi, deps=[x_br[4]])
    (dz,) = _fused_matmul(
        "pool_mm_dx", (N // tm, NPG, 1),
        [(dq, BS((tm, PG), lambda i, g, k: (i, g)), Wpool, BS((None, PG, PG), lambda i, g, k: (g, 0, 0)), "nt", 0)],
        [(tm, PG)], [], [(_sds((N, DP), F32), BS((tm, PG), lambda i, g, k: (i, g)))],
        _store(lambda accs: accs))
    (gW_pool,) = _fused_matmul(
        "pool_mm_dw", (NPG, 1),
        [(z, BS((N, PG), lambda g, k: (0, g)), dq, BS((N, PG), lambda g, k: (0, g)), "tn", 0)],
        [(PG, PG)], [], [(_sds((NPG, PG, PG), BF16), BS((None, PG, PG), lambda g, k: (g, 0, 0)))],
        _store(lambda accs: accs))
    du_pool = _pool_bwd(dz, nseq, L)

    def dys_epi(ids, accs, ex, o):
        zg_ref, y0_ref = ex
        first = ids[1] == 0
        dys = accs[0]
        sg = _sigmoid(zg_ref[...])
        dzg = dys * _gelu(y0_ref[...]) * sg * (1.0 - sg)
        o[0][...] = dzg.astype(BF16)
        o[1][...] = dys * sg
        _rowsum_into(o[2], first, dzg)

    dzg, dy1_direct, d_glu_b = _fused_matmul(
        "branch_ssm_dx", (DS // tn_s, N // tm, 1),
        [(dYs, BS((tm, D), lambda j, i, k: (i, 0)), Wbs, BS((NDEV, tn_s, U), lambda j, i, k: (0, j, 0)), "nt_cat", 0)],
        [(tm, tn_s)],
        [(zg, BS((tm, tn_s), lambda j, i, k: (i, j))), (y0, BS((tm, tn_s), lambda j, i, k: (i, j)))],
        [(_sds((N, DS), BF16), BS((tm, tn_s), lambda j, i, k: (i, j))),
         (_sds((N, DS), F32), BS((tm, tn_s), lambda j, i, k: (i, j))),
         (_sds((1, DS), F32), BS((1, tn_s), lambda j, i, k: (0, j)))],
        dys_epi)
    (gW_glu,) = _fused_matmul(
        "ssm_glu_dw", (DS // tm_s, DS // tn_s, 1),
        [(y1, BS((N, tm_s), lambda i, j, k: (0, i)), dzg, BS((N, tn_s), lambda i, j, k: (0, j)), "tn", 0)],
        [(tm_s, tn_s)], [], [(_sds((DS, DS), BF16), BS((tm_s, tn_s), lambda i, j, k: (i, j)))],
        _store(lambda accs: accs))
    x_mix = x_start("mix", [gW_glu.reshape(NDEV, DS // NDEV, DS),
                            gW_pool.reshape(NPG, NDEV, PG // NDEV, PG).transpose(1, 0, 2, 3)])

    tn_c = _tile(DS, CH)

    def dy0_epi(ids, accs, ex, o):
        d1_ref, y0_ref, u_ref = ex
        first = ids[1] == 0
        dy0 = (accs[0] + d1_ref[...]) * _gelu_grad(y0_ref[...])
        o[0][...] = dy0
        _rowsum_into(o[1], first, dy0 * u_ref[...])

    dy0, d_ssm_d = _fused_matmul(
        "ssm_glu_dx", (DS // tn_c, N // tm, 1),
        [(dzg, BS((tm, DS), lambda j, i, k: (i, 0)), Wglu, BS((tn_c, DS), lambda j, i, k: (j, 0)), "nt", 0)],
        [(tm, tn_c)],
        [(dy1_direct, BS((tm, tn_c), lambda j, i, k: (i, j))), (y0, BS((tm, tn_c), lambda j, i, k: (i, j))),
         (proj, BS((tm, tn_c), lambda j, i, k: (i, j)))],
        [(_sds((N, DS), F32), BS((tm, tn_c), lambda j, i, k: (i, j))),
         (_sds((1, DS), F32), BS((1, tn_c), lambda j, i, k: (0, j)))],
        dy0_epi, deps=[x_mix[4]])

    ds_re, ds_im = _fused_matmul(
        "ssm_out_dx", (N // tm2, NCH, 1),
        [(dy0, BS((tm2, CH), lambda i, c, k: (i, c)), WCre, BS((None, CS, CH), lambda i, c, k: (c, 0, 0)), "nt", 0),
         (dy0, BS((tm2, CH), lambda i, c, k: (i, c)), WCim, BS((None, CS, CH), lambda i, c, k: (c, 0, 0)), "nt", 1)],
        [(tm2, CS), (tm2, CS)], [],
        [(_sds((N, SL), F32), BS((tm2, CS), lambda i, c, k: (i, c)))] * 2,
        _store(lambda accs: accs))
    lam_r, lam_i, d_ab_re, d_ab_im = _scan_bwd(ds_re, ds_im, s_re, s_im, pw_re, pw_im, pwf_re, pwf_im, nseq, L)

    def du_epi(ids, accs, ex, o):
        dy0_ref, d_ref = ex
        o[0][...] = (accs[0] + dy0_ref[...] * d_ref[...]).astype(BF16)

    (du_ssm,) = _fused_matmul(
        "ssm_in_dx", (N // tm2, NCH, 1),
        [(lam_r, BS((tm2, CS), lambda i, c, k: (i, c)), WB, BS((None, CH, CS), lambda i, c, k: (c, 0, 0)), "nt", 0),
         (lam_i, BS((tm2, CS), lambda i, c, k: (i, c)), WB, BS((None, CH, CS), lambda i, c, k: (c, 0, 1)), "nt", 0)],
        [(tm2, CH)],
        [(dy0, BS((tm2, CH), lambda i, c, k: (i, c))), (ssm_d, BS((1, CH), lambda i, c, k: (0, c)))],
        [(_sds((N, DS), BF16), BS((tm2, CH), lambda i, c, k: (i, c)))],
        du_epi)
    dproj = jnp.concatenate([du_ssm, du_pool, dgs, dgp], axis=1)
    (gW_in,) = _fused_matmul(
        "in_proj_dw", (D // tm_d, nq, 1),
        [(a, BS((N, tm_d), lambda i, q, k: (0, i)), dproj, BS((N, U), lambda i, q, k: (0, q)), "tn", 0)],
        [(tm_d, U)], [], [(_sds((NDEV, D, 3 * U), BF16), BS((None, tm_d, U), lambda i, q, k: (q // 3, i, q % 3)))],
        _store(lambda accs: accs))
    x_in = x_start("in", [gW_in])
    (da,) = _fused_matmul(
        "in_proj_dx", (N // tm, D // tn_d2, NDEV // 2),
        [(dproj, BS((tm, 6 * U), lambda i, j, k: (i, k)), Win, BS((2, tn_d2, 3 * U), lambda i, j, k: (k, j, 0)), "nt_cat", 0)],
        [(tm, tn_d2)], [], [(_sds((N, D), F32), BS((tm, tn_d2), lambda i, j, k: (i, j)))],
        _store(lambda accs: accs), deps=[x_in[4]])
    grad_x, dg1 = _pre_bwd(x2, da, dh1, norm_pre_mix)

    tk_n = _tile(N, 1024)
    dWCre, dWCim = _fused_matmul(
        "ssm_out_dw", (NCH, N // tk_n),
        [(s_re, BS((tk_n, CS), lambda c, k: (k, c)), dy0, BS((tk_n, CH), lambda c, k: (k, c)), "tn", 0),
         (s_im, BS((tk_n, CS), lambda c, k: (k, c)), dy0, BS((tk_n, CH), lambda c, k: (k, c)), "tn", 1)],
        [(CS, CH), (CS, CH)], [],
        [(_sds((NCH, CS, CH), F32), BS((None, CS, CH), lambda c, k: (c, 0, 0)))] * 2,
        _store(lambda accs: accs), deps=[x_in[4]])
    dWBre, dWBim = _fused_matmul(
        "ssm_in_dw", (NCH, N // tk_n),
        [(proj, BS((tk_n, CH), lambda c, k: (k, c)), lam_r, BS((tk_n, CS), lambda c, k: (k, c)), "tn", 0),
         (proj, BS((tk_n, CH), lambda c, k: (k, c)), lam_i, BS((tk_n, CS), lambda c, k: (k, c)), "tn", 1)],
        [(CH, CS), (CH, CS)], [],
        [(_sds((NCH, CH, CS), F32), BS((None, CH, CS), lambda c, k: (c, 0, 0)))] * 2,
        _store(lambda accs: accs), deps=[x_in[4]])
    d_bbr = _block_diag_in_grad(dWBre, J, G, P)
    d_bbi = _block_diag_in_grad(dWBim, J, G, P)
    d_lam_re, d_lam_im, d_log_step, d_br_t, d_bi_t = _ssm_param_bwd(
        lam_re, lam_im, log_step, br_t, bi_t,
        d_ab_re.reshape(nseq * SUBLANES, G, P), d_ab_im.reshape(nseq * SUBLANES, G, P), d_bbr, d_bbi)
    d_c_re = _block_diag_out_grad(dWCre, G, J, P)
    d_c_im = -_block_diag_out_grad(dWCim, G, J, P)

    d_conv_w = dcw.reshape(NB, 3, FC).transpose(1, 0, 2).reshape(3, F2)
    d_conv_b = dcb.reshape(1, F2)
    small = {
        "norm_pre_mix": dg1, "norm_post_mix": dg2, "norm_pre_ffn": dg3, "norm_post_ffn": dg4,
        "ssm_lambda_re": d_lam_re[None], "ssm_lambda_im": d_lam_im[None], "ssm_log_step": d_log_step.reshape(1, G),
        "ssm_b_re": d_br_t.transpose(1, 2, 0)[None], "ssm_b_im": d_bi_t.transpose(1, 2, 0)[None],
        "ssm_c_re": d_c_re[None], "ssm_c_im": d_c_im[None],
        "ssm_d": d_ssm_d, "ssm_glu_b": d_glu_b, "pool_scale": d_pscale,
        "pool_b": d_pb.reshape(1, NPG, PG), "ffn_conv_w": d_conv_w[None], "ffn_conv_b": d_conv_b,
    }
    assert list(small) == small_names
    wide = ["ssm_b_re", "ssm_b_im", "ssm_c_re", "ssm_c_im"]
    narrow = [n for n in small_names if n not in wide]
    packed = _pack([lossv] + [small[n] for n in narrow])
    packed_bf = _pack([small[n] for n in wide]).astype(BF16)
    st_small = _split_start("small_start", _broadcast_copies, NDEV - 1, [packed, packed_bf],
                            [_sds((NDEV,) + p.shape, p.dtype) for p in (packed, packed_bf)], packed)

    res = {}
    after = st_small[4]
    for tag, started, group in (("down", x_down, ["w_down"]), ("up", x_up, ["w_up"]),
                                ("branch", x_br, ["w_branch_ssm", "w_branch_pool", "w_out"]),
                                ("mix", x_mix, ["ssm_glu_w", "pool_w"]), ("in", x_in, ["w_in"])):
        for n, parts in zip(group, x_finish(tag, started, after)):
            shape = args[n].shape
            if n == "w_up":
                flat, back = (lambda t: t[0].T), (lambda t: t.T[None])
            else:
                flat, back = (lambda t: t.reshape(-1, shape[-1])), (lambda t: t.reshape(shape))
            w2 = flat(args[n])
            g, dl, nm, nv = _adamw("adamw_" + n, w2, flat(args["m_" + n]), flat(args["v_" + n]),
                                   parts.reshape((NDEV,) + w2.shape))
            res[n] = tuple(back(t) for t in (g, dl, nm, nv))
            after = g

    srcs, lands = _split_wait("small_wait", _broadcast_copies, st_small, after)
    small_all, small_all_bf = (lax.dynamic_update_index_in_dim(l, s, dev, 0) for l, s in zip(lands, srcs))
    loss_rows = (D + SUBLANES * LANES - 1) // (SUBLANES * LANES) * SUBLANES
    total, total_wide, loss = _small_sum(small_all, small_all_bf, loss_rows, D)
    totals = dict(zip(narrow, _unpack(total, [lossv.shape] + [small[n].shape for n in narrow])[1:]))
    totals.update(zip(wide, _unpack(total_wide, [small[n].shape for n in wide])))
    totals["pool_b"] = lax.dynamic_slice_in_dim(totals["pool_b"], dev * (PG // NDEV), PG // NDEV, axis=2)
    totals["ffn_conv_w"] = lax.dynamic_slice_in_dim(totals["ffn_conv_w"], dev * FC, FC, axis=2)
    sm_g = _pack([totals[n] for n in small_names])
    _, sm_d, sm_nm, sm_nv = _adamw("adamw_small", sm_w, sm_m, sm_v, sm_g[None])
    shapes = [args[n].shape for n in small_names]
    for n, dl, nm, nv in zip(small_names, _unpack(sm_d, shapes), _unpack(sm_nm, shapes), _unpack(sm_nv, shapes)):
        res[n] = (totals[n], dl, nm, nv)

    outs = [loss.reshape(()), grad_x.reshape(x.shape)]
    for k in range(4):
        outs += [res[n][k] for n in names]
    return tuple(outs)
```

```python
import functools
import math

import jax
import jax.numpy as jnp
from jax import lax
from jax.experimental import pallas as pl
from jax.experimental.pallas import tpu as pltpu

F32 = jnp.float32
BF16 = jnp.bfloat16
BS = pl.BlockSpec

NDEV = 8
SSM_GROUP = 16
SSM_STATE = 64
GROUPS_PER_CHUNK = 16
SCAN_UNROLL = 8
POOL_WINDOWS = (2, 4, 8, 16)
EPS = 1e-6
MIN_NEG_REAL = -1e-4
ADAM_LR, ADAM_B1, ADAM_B2, ADAM_EPS, ADAM_WD, ADAM_STEP = 0.001, 0.9, 0.999, 1e-08, 0.01, 10
LANES = 128
SUBLANES = 8
VMEM_LIMIT = 56 * 1024 * 1024

_DIMS = {"nn": (((1,), (0,)), ((), ())), "nt": (((1,), (1,)), ((), ())), "tn": (((0,), (0,)), ((), ()))}


def _tile(dim, pref, mult=LANES):
    if dim <= pref:
        return dim
    t = (pref // mult) * mult
    while t >= mult:
        if dim % t == 0:
            return t
        t -= mult
    return dim


def _pc(name, body, grid, ins, in_specs, outs, out_specs, scratch=(), deps=()):
    multi = isinstance(outs, (list, tuple))
    if deps:
        n_in, n_dep, inner = len(ins), len(deps), body

        def body(*refs):
            return inner(*refs[:n_in], *refs[n_in + n_dep:])

        ins = list(ins) + list(deps)
        in_specs = list(in_specs) + [BS(memory_space=pl.ANY)] * n_dep
    return pl.pallas_call(
        body, name=name, grid=grid, in_specs=list(in_specs),
        out_specs=list(out_specs) if multi else out_specs,
        out_shape=list(outs) if multi else outs, scratch_shapes=list(scratch),
        compiler_params=pltpu.CompilerParams(dimension_semantics=("arbitrary",) * len(grid),
                                             vmem_limit_bytes=VMEM_LIMIT),
    )(*ins)


def _sds(shape, dtype):
    return jax.ShapeDtypeStruct(tuple(shape), dtype)


def _gelu(x):
    k = math.sqrt(2.0 / math.pi)
    return 0.5 * x * (1.0 + jnp.tanh(k * (x + 0.044715 * (x * x * x))))


def _gelu_grad(x):
    k = math.sqrt(2.0 / math.pi)
    t = jnp.tanh(k * (x + 0.044715 * (x * x * x)))
    return 0.5 * (1.0 + t) + 0.5 * x * (1.0 - t * t) * (k * (1.0 + 3.0 * 0.044715 * x * x))


def _sigmoid(x):
    return jax.nn.sigmoid(x)


def _fused_matmul(name, grid, pairs, acc_shapes, extras, outs, epilogue, deps=(), chunks=1):
    n_p, n_e, n_o = len(pairs), len(extras), len(outs)
    rank = len(grid)
    nk = grid[-1]
    if nk == 1:
        acc_shapes = []

    def body(*refs):
        ab = refs[:2 * n_p]
        ex = refs[2 * n_p:2 * n_p + n_e]
        o = refs[2 * n_p + n_e:2 * n_p + n_e + n_o]
        accs = refs[2 * n_p + n_e + n_o:]
        ids = [pl.program_id(d) for d in range(rank)]
        k = ids[-1]

        if chunks > 1:
            width = o[0].shape[-1] // chunks
            for ch in range(chunks):
                cols = pl.ds(ch * width, width)
                sums = {}
                for p in range(n_p):
                    a = ab[2 * p][...].astype(BF16)
                    b_ref, mode = ab[2 * p + 1], pairs[p][4]
                    b = (b_ref[:, cols] if mode == "nn" else b_ref[cols, :]).astype(BF16)
                    d = lax.dot_general(a, b, _DIMS[mode], preferred_element_type=F32)
                    sums[pairs[p][5]] = d if pairs[p][5] not in sums else sums[pairs[p][5]] + d
                epilogue(ids, [sums[i] for i in range(len(sums))],
                         [r.at[:, cols] for r in ex], [r.at[:, cols] for r in o])
            return

        def products():
            sums = {}
            for p in range(n_p):
                a = ab[2 * p][...].astype(BF16)
                mode = pairs[p][4]
                if mode == "nt_cat":
                    b_ref = ab[2 * p + 1]
                    b = jnp.concatenate([b_ref[q].astype(BF16) for q in range(b_ref.shape[0])], axis=1)
                    mode = "nt"
                else:
                    b = ab[2 * p + 1][...].astype(BF16)
                d = lax.dot_general(a, b, _DIMS[mode], preferred_element_type=F32)
                sums[pairs[p][5]] = d if pairs[p][5] not in sums else sums[pairs[p][5]] + d
            return [sums[i] for i in range(len(sums))]

        if nk == 1:
            epilogue(ids, products(), ex, o)
            return

        @pl.when(k == 0)
        def _():
            for acc, s in zip(accs, products()):
                acc[...] = s

        if nk > 2:
            @pl.when(jnp.logical_and(k > 0, k < nk - 1))
            def _():
                for acc, s in zip(accs, products()):
                    acc[...] += s

        @pl.when(k == nk - 1)
        def _():
            epilogue(ids, [acc[...] + s for acc, s in zip(accs, products())], ex, o)

    ins, in_specs = [], []
    for a, a_spec, b, b_spec, _, _ in pairs:
        ins += [a, b]
        in_specs += [a_spec, b_spec]
    for e, e_spec in extras:
        ins.append(e)
        in_specs.append(e_spec)
    res = _pc(name, body, grid, ins, in_specs, [s for s, _ in outs], [sp for _, sp in outs],
              scratch=[pltpu.VMEM(tuple(s), F32) for s in acc_shapes], deps=deps)
    return res


def _store(vals):
    def epilogue(ids, accs, ex, o):
        for r, v in zip(o, vals(accs)):
            r[...] = v.astype(r.dtype)
    return epilogue


def _rowsum_into(ref, first, v):
    s = jnp.sum(v, axis=0, keepdims=True)

    @pl.when(first)
    def _():
        ref[...] = s

    @pl.when(jnp.logical_not(first))
    def _():
        ref[...] += s


def _mesh_pos():
    return lax.axis_index("x"), lax.axis_index("y"), lax.axis_index("c")


def _slot(p):
    return 4 * p[0] + 2 * p[1] + p[2]


_HBM = BS(memory_space=pltpu.HBM)
_SEM = BS(memory_space=pltpu.SEMAPHORE)
_ANY = BS(memory_space=pl.ANY)
_EFFECT = pltpu.SideEffectType.DATAFLOW_SIDE_EFFECTING


def _other_chips(x, y):
    return [(1 - x, y), (x, 1 - y), (1 - x, 1 - y)]


def _all_peers(x, y, c):
    peers = []
    for k in range(1, NDEV):
        kx, ky, kc = (k >> 2) & 1, (k >> 1) & 1, k & 1
        peers.append((1 - x if kx else x, 1 - y if ky else y, 1 - c if kc else c))
    return peers


def _gather_copies(src, land, send_sems, recv_sems, base):
    x, y, c = _mesh_pos()
    return [pltpu.make_async_remote_copy(
        src_ref=src, dst_ref=land.at[_slot((x, y, c))],
        send_sem=send_sems.at[base + k], recv_sem=recv_sems.at[base + k],
        device_id=(*chip, c), device_id_type=pl.DeviceIdType.MESH) for k, chip in enumerate(_other_chips(x, y))]


def _d2d_copies(src, land, send_sems, recv_sems, base):
    x, y, c = _mesh_pos()
    blocks = [(x, y, c)] + [(*chip, c) for chip in _other_chips(x, y)]
    return [pltpu.make_async_remote_copy(
        src_ref=src if k == 0 else land.at[_slot(b)], dst_ref=land.at[_slot(b)],
        send_sem=send_sems.at[base + k], recv_sem=recv_sems.at[base + k],
        device_id=(x, y, 1 - c), device_id_type=pl.DeviceIdType.MESH) for k, b in enumerate(blocks)]


def _broadcast_copies(src, land, send_sems, recv_sems, base):
    x, y, c = _mesh_pos()
    return [pltpu.make_async_remote_copy(
        src_ref=src, dst_ref=land.at[_slot((x, y, c))],
        send_sem=send_sems.at[base + k], recv_sem=recv_sems.at[base + k],
        device_id=peer, device_id_type=pl.DeviceIdType.MESH) for k, peer in enumerate(_all_peers(x, y, c))]


def _exchange_copies(src, land, send_sems, recv_sems, base):
    x, y, c = _mesh_pos()
    return [pltpu.make_async_remote_copy(
        src_ref=src.at[_slot(peer)], dst_ref=land.at[_slot((x, y, c))],
        send_sem=send_sems.at[base + k], recv_sem=recv_sems.at[base + k],
        device_id=peer, device_id_type=pl.DeviceIdType.MESH) for k, peer in enumerate(_all_peers(x, y, c))]


def _split_start(name, copies, ncopy, srcs, land_shapes, after):
    n = len(srcs)

    def body(*refs):
        src_refs, land_refs = refs[:n], refs[n:2 * n]
        send_sems, recv_sems = refs[2 * n + 1], refs[2 * n + 2]
        token = refs[-1]
        for r in range(n):
            for cp in copies(src_refs[r], land_refs[r], send_sems, recv_sems, r * ncopy):
                cp.start()
        token[...] = jnp.zeros_like(token)

    lands = [s if isinstance(s, jax.Array) else pltpu.with_memory_space_constraint(lax.empty(s.shape, s.dtype), pltpu.HBM)
             for s in land_shapes]
    ins = list(srcs) + lands
    out_shape = ([pltpu.SemaphoreType.DMA((n * ncopy,)), pltpu.SemaphoreType.DMA((n * ncopy,))]
                 + [pltpu.HBM(a.shape, a.dtype) for a in lands]
                 + [_sds((SUBLANES, LANES), F32)])
    res = pl.pallas_call(
        body, name=name, out_shape=out_shape,
        in_specs=[_HBM] * (2 * n) + [_ANY], out_specs=[_SEM, _SEM] + [_HBM] * n + [BS(memory_space=pltpu.VMEM)],
        input_output_aliases={n + i: 2 + i for i in range(n)},
        compiler_params=pltpu.CompilerParams(has_side_effects=_EFFECT),
    )(*ins, after)
    return res[0], res[1], list(srcs), list(res[2:2 + n]), res[-1]


def _split_wait(name, copies, started, after):
    send_sems, recv_sems, srcs, lands, _ = started
    n = len(srcs)
    ncopy = send_sems.shape[0] // n
    after = list(after) if isinstance(after, (list, tuple)) else [after]

    def body(*refs):
        src_refs, land_refs = refs[:n], refs[n:2 * n]
        send_sems, recv_sems = refs[2 * n], refs[2 * n + 1]
        for r in range(n):
            for cp in copies(src_refs[r], land_refs[r], send_sems, recv_sems, r * ncopy):
                cp.wait_send()
                cp.wait_recv()

    res = pl.pallas_call(
        body, name=name, out_shape=[pltpu.HBM(a.shape, a.dtype) for a in lands],
        in_specs=[_HBM] * (2 * n) + [_SEM, _SEM] + [_ANY] * len(after), out_specs=[_HBM] * n,
        input_output_aliases={n + i: i for i in range(n)},
        compiler_params=pltpu.CompilerParams(has_side_effects=_EFFECT),
    )(*srcs, *lands, send_sems, recv_sems, *after)
    return list(srcs), list(res)


def _adamw(name, w, m, v, parts):
    R, C = w.shape
    S = parts.shape[0]
    tr = _tile(R, max(SUBLANES, (256 * 1024) // C), SUBLANES)

    def body(w_ref, m_ref, v_ref, p_ref, g_ref, d_ref, nm_ref, nv_ref):
        g = p_ref[0].astype(F32)
        for s in range(1, S):
            g = g + p_ref[s].astype(F32)
        m2 = ADAM_B1 * m_ref[...] + (1.0 - ADAM_B1) * g
        v2 = ADAM_B2 * v_ref[...] + (1.0 - ADAM_B2) * (g * g)
        m_hat = m2 / (1.0 - ADAM_B1 ** ADAM_STEP)
        v_hat = v2 / (1.0 - ADAM_B2 ** ADAM_STEP)
        g_ref[...] = g
        d_ref[...] = -ADAM_LR * (m_hat / (jnp.sqrt(v_hat) + ADAM_EPS) + ADAM_WD * w_ref[...])
        nm_ref[...] = m2
        nv_ref[...] = v2

    blk = BS((tr, C), lambda i: (i, 0))
    return _pc(name, body, (R // tr,), [w, m, v, parts],
               [blk, blk, blk, BS((S, tr, C), lambda i: (0, i, 0))],
               [_sds((R, C), F32)] * 4, [blk] * 4)


def _ssm_disc(lam_re, lam_im, log_step, br_t, bi_t):
    lr = jnp.minimum(lam_re, MIN_NEG_REAL)
    li = lam_im
    dt = jnp.exp(log_step)
    mag = jnp.exp(lr * dt)
    ang = li * dt
    ab_re = mag * jnp.cos(ang)
    ab_im = mag * jnp.sin(ang)
    nr = ab_re - 1.0
    ni = ab_im
    den = lr * lr + li * li
    f_re = (nr * lr + ni * li) / den
    f_im = (ni * lr - nr * li) / den
    bb_re = f_re[None] * br_t - f_im[None] * bi_t
    bb_im = f_re[None] * bi_t + f_im[None] * br_t
    return ab_re, ab_im, bb_re, bb_im


def _ssm_param_fwd(lam_re, lam_im, log_step, br_t, bi_t):
    G, P = lam_re.shape

    def body(lr_ref, li_ref, ls_ref, br_ref, bi_ref, pw_re_ref, pw_im_ref, pwf_re_ref, pwf_im_ref, bbr_ref, bbi_ref):
        ab_re, ab_im, bb_re, bb_im = _ssm_disc(lr_ref[...], li_ref[...], ls_ref[...], br_ref[...], bi_ref[...])
        bbr_ref[...] = bb_re
        bbi_ref[...] = bb_im
        pr, pi = ab_re, ab_im
        for r in range(SUBLANES):
            pw_re_ref[r] = pr
            pw_im_ref[r] = pi
            pwf_re_ref[SUBLANES - 1 - r] = pr
            pwf_im_ref[SUBLANES - 1 - r] = pi
            pr, pi = pr * ab_re - pi * ab_im, pr * ab_im + pi * ab_re

    full = lambda a: BS(a.shape, lambda i: (0,) * a.ndim)
    ins = [lam_re, lam_im, log_step, br_t, bi_t]
    outs = [_sds((SUBLANES, G, P), F32)] * 4 + [_sds(br_t.shape, F32)] * 2
    return _pc("ssm_param_fwd", body, (1,), ins, [full(a) for a in ins], outs, [full(o) for o in outs])


def _ssm_param_bwd(lam_re, lam_im, log_step, br_t, bi_t, d_ab_re, d_ab_im, d_bbr, d_bbi):
    def body(lr_ref, li_ref, ls_ref, br_ref, bi_ref, dar_ref, dai_ref, dbr_ref, dbi_ref,
             o_lr, o_li, o_ls, o_br, o_bi):
        prim = (lr_ref[...], li_ref[...], ls_ref[...], br_ref[...], bi_ref[...])
        _, vjp = jax.vjp(_ssm_disc, *prim)
        dar = dar_ref[0]
        dai = dai_ref[0]
        for k in range(1, dar_ref.shape[0]):
            dar = dar + dar_ref[k]
            dai = dai + dai_ref[k]
        g = vjp((dar, dai, dbr_ref[...], dbi_ref[...]))
        for r, v in zip((o_lr, o_li, o_ls, o_br, o_bi), g):
            r[...] = v

    full = lambda a: BS(a.shape, lambda i: (0,) * a.ndim)
    ins = [lam_re, lam_im, log_step, br_t, bi_t, d_ab_re, d_ab_im, d_bbr, d_bbi]
    outs = [_sds(a.shape, F32) for a in (lam_re, lam_im, log_step, br_t, bi_t)]
    return _pc("ssm_param_bwd", body, (1,), ins, [full(a) for a in ins], outs, [full(o) for o in outs])


def _bcast_row(ref, r, w):
    return jnp.broadcast_to(ref[pl.ds(r, 1), :], (SUBLANES, w))


def _pick_row(x, row, r):
    return jnp.broadcast_to(jnp.sum(jnp.where(row == r, x, 0.0), axis=0, keepdims=True), x.shape)


def _scan_fwd(bu_re, bu_im, pw_re, pw_im, nseq, L):
    N, SL = bu_re.shape
    W = _tile(SL, 256)
    unroll = math.gcd(L // SUBLANES, SCAN_UNROLL)

    def body(bre_ref, bim_ref, pre_ref, pim_ref, sre_ref, sim_ref):
        pre, pim = pre_ref[...], pim_ref[...]
        steps = [(k, _bcast_row(pre_ref, k - 1, W), _bcast_row(pim_ref, k - 1, W)) for k in (1, 2, 4)]
        row = lax.broadcasted_iota(jnp.int32, (SUBLANES, W), 0)

        def step(i, carry):
            cr, ci = carry
            r0 = pl.multiple_of(i * SUBLANES, SUBLANES)
            xr = bre_ref[pl.ds(r0, SUBLANES), :]
            xi = bim_ref[pl.ds(r0, SUBLANES), :]
            for k, ar, ai in steps:
                sr = pltpu.roll(xr, k, axis=0)
                si = pltpu.roll(xi, k, axis=0)
                keep = row >= k
                xr, xi = (xr + jnp.where(keep, ar * sr - ai * si, 0.0),
                          xi + jnp.where(keep, ar * si + ai * sr, 0.0))
            xr, xi = xr + (pre * cr - pim * ci), xi + (pre * ci + pim * cr)
            sre_ref[pl.ds(r0, SUBLANES), :] = xr
            sim_ref[pl.ds(r0, SUBLANES), :] = xi
            return _pick_row(xr, row, SUBLANES - 1), _pick_row(xi, row, SUBLANES - 1)

        def group(g, carry):
            for u in range(unroll):
                carry = step(g * unroll + u, carry)
            return carry

        zero = jnp.zeros((SUBLANES, W), F32)
        lax.fori_loop(0, L // SUBLANES // unroll, group, (zero, zero))

    blk = BS((L, W), lambda s, j: (s, j))
    pw = BS((SUBLANES, W), lambda s, j: (0, j))
    return _pc("ssm_scan_fwd", body, (nseq, SL // W), [bu_re, bu_im, pw_re, pw_im], [blk, blk, pw, pw],
               [_sds((N, SL), F32)] * 2, [blk, blk])


def _scan_bwd(ds_re, ds_im, s_re, s_im, pw_re, pw_im, pwf_re, pwf_im, nseq, L):
    N, SL = ds_re.shape
    W = _tile(SL, 256)
    nt = L // SUBLANES
    unroll = math.gcd(nt, SCAN_UNROLL)

    def body(dsr_ref, dsi_ref, sre_ref, sim_ref, pre_ref, pim_ref, fre_ref, fim_ref,
             lre_ref, lim_ref, dar_ref, dai_ref):
        fre, fim = fre_ref[...], -fim_ref[...]
        steps = [(k, _bcast_row(pre_ref, k - 1, W), -_bcast_row(pim_ref, k - 1, W)) for k in (1, 2, 4)]
        row = lax.broadcasted_iota(jnp.int32, (SUBLANES, W), 0)

        def step(ii, carry):
            cr, ci, acr, aci = carry
            i = nt - 1 - ii
            r0 = pl.multiple_of(i * SUBLANES, SUBLANES)
            xr = dsr_ref[pl.ds(r0, SUBLANES), :]
            xi = dsi_ref[pl.ds(r0, SUBLANES), :]
            for k, ar, ai in steps:
                sr = pltpu.roll(xr, SUBLANES - k, axis=0)
                si = pltpu.roll(xi, SUBLANES - k, axis=0)
                keep = row < SUBLANES - k
                xr, xi = (xr + jnp.where(keep, ar * sr - ai * si, 0.0),
                          xi + jnp.where(keep, ar * si + ai * sr, 0.0))
            xr, xi = xr + (fre * cr - fim * ci), xi + (fre * ci + fim * cr)
            lre_ref[pl.ds(r0, SUBLANES), :] = xr
            lim_ref[pl.ds(r0, SUBLANES), :] = xi
            p0 = pl.multiple_of(jnp.maximum(i - 1, 0) * SUBLANES, SUBLANES)
            has_prev = i > 0
            spr = jnp.where(row == 0,
                            jnp.where(has_prev, pltpu.roll(sre_ref[pl.ds(p0, SUBLANES), :], 1, axis=0), 0.0),
                            pltpu.roll(sre_ref[pl.ds(r0, SUBLANES), :], 1, axis=0))
            spi = jnp.where(row == 0,
                            jnp.where(has_prev, pltpu.roll(sim_ref[pl.ds(p0, SUBLANES), :], 1, axis=0), 0.0),
                            pltpu.roll(sim_ref[pl.ds(r0, SUBLANES), :], 1, axis=0))
            acr = acr + (xr * spr + xi * spi)
            aci = aci + (xi * spr - xr * spi)
            return _pick_row(xr, row, 0), _pick_row(xi, row, 0), acr, aci

        def group(g, carry):
            for u in range(unroll):
                carry = step(g * unroll + u, carry)
            return carry

        zero = jnp.zeros((SUBLANES, W), F32)
        _, _, acr, aci = lax.fori_loop(0, nt // unroll, group, (zero, zero, zero, zero))
        dar_ref[...] = acr
        dai_ref[...] = aci

    blk = BS((L, W), lambda s, j: (s, j))
    pw = BS((SUBLANES, W), lambda s, j: (0, j))
    da = BS((None, SUBLANES, W), lambda s, j: (s, 0, j))
    return _pc("ssm_scan_bwd", body, (nseq, SL // W),
               [ds_re, ds_im, s_re, s_im, pw_re, pw_im, pwf_re, pwf_im], [blk] * 4 + [pw] * 4,
               [_sds((N, SL), F32)] * 2 + [_sds((nseq, SUBLANES, SL), F32)] * 2, [blk, blk, da, da])


def _pool_select(g, vals):
    return jnp.where(g == 0, vals[0], jnp.where(g == 1, vals[1], jnp.where(g == 2, vals[2], vals[3])))


def _pool_fwd(proj, col0, DP, nseq, L):
    N = proj.shape[0]
    PG = DP // len(POOL_WINDOWS)
    W = _tile(PG, 256)

    def body(v_ref, z_ref):
        g = pl.program_id(1) // (PG // W)
        v = v_ref[...]
        row = lax.broadcasted_iota(jnp.int32, (L, W), 0)
        sums, s, k = [], v, 1
        for _ in POOL_WINDOWS:
            s = s + jnp.where(row >= k, pltpu.roll(s, k, axis=0), 0.0)
            sums.append(s)
            k *= 2
        win = _pool_select(g, [float(w) for w in POOL_WINDOWS])
        cnt = jnp.minimum((row + 1).astype(F32), win)
        z_ref[...] = (_pool_select(g, sums) / cnt - v).astype(z_ref.dtype)

    return _pc("pool_fwd", body, (nseq, DP // W), [proj], [BS((L, W), lambda s, j: (s, col0 // W + j))],
               _sds((N, DP), BF16), BS((L, W), lambda s, j: (s, j)))


def _pool_bwd(dz, nseq, L):
    N, DP = dz.shape
    PG = DP // len(POOL_WINDOWS)
    W = _tile(PG, 256)

    def body(dz_ref, dv_ref):
        g = pl.program_id(1) // (PG // W)
        d = dz_ref[...]
        row = lax.broadcasted_iota(jnp.int32, (L, W), 0)
        win = _pool_select(g, [float(w) for w in POOL_WINDOWS])
        s = d / jnp.minimum((row + 1).astype(F32), win)
        sums, k = [], 1
        for _ in POOL_WINDOWS:
            s = s + jnp.where(row < L - k, pltpu.roll(s, L - k, axis=0), 0.0)
            sums.append(s)
            k *= 2
        dv_ref[...] = (_pool_select(g, sums) - d).astype(dv_ref.dtype)

    blk = BS((L, W), lambda s, j: (s, j))
    return _pc("pool_bwd", body, (nseq, DP // W), [dz], [blk], _sds((N, DP), BF16), blk)


def _rstd(x):
    return lax.rsqrt(jnp.mean(x * x, axis=-1, keepdims=True) + EPS)


def _norm_bwd(dy, xhat, rstd, gain):
    t = dy * gain
    return rstd * (t - xhat * jnp.mean(t * xhat, axis=-1, keepdims=True))


def _pre_norm(x, g1):
    N, D = x.shape
    tr = _tile(N, 128, SUBLANES)

    def body(x_ref, g_ref, a_ref):
        xv = x_ref[...]
        a_ref[...] = (xv * _rstd(xv) * g_ref[...]).astype(a_ref.dtype)

    row = BS((tr, D), lambda i: (i, 0))
    vec = BS((1, D), lambda i: (0, 0))
    return _pc("pre_norm", body, (N // tr,), [x, g1], [row, vec], _sds((N, D), BF16), row)


def _mid_norm(x, o, g2, g3):
    N, D = x.shape
    tr = _tile(N, 128, SUBLANES)

    def body(x_ref, o_ref, g2_ref, g3_ref, h1_ref, c_ref):
        ov = o_ref[...]
        h1 = x_ref[...] + ov * _rstd(ov) * g2_ref[...]
        h1_ref[...] = h1
        c_ref[...] = (h1 * _rstd(h1) * g3_ref[...]).astype(c_ref.dtype)

    row = BS((tr, D), lambda i: (i, 0))
    vec = BS((1, D), lambda i: (0, 0))
    return _pc("mid_norm", body, (N // tr,), [x, o, g2, g3], [row, row, vec, vec],
               [_sds((N, D), F32), _sds((N, D), BF16)], [row, row])


def _post_ffn(h1, dn, tgt, g4):
    N, D = h1.shape
    tr = _tile(N, 128, SUBLANES)

    def body(h1_ref, dn_ref, t_ref, g_ref, dh2_ref, ddn_ref, lossv_ref, dg_ref):
        first = pl.program_id(0) == 0
        dnv = dn_ref[...]
        rstd = _rstd(dnv)
        xhat = dnv * rstd
        gain = g_ref[...]
        err = (h1_ref[...] + xhat * gain) - t_ref[...]
        dh2 = err / float(D)
        dh2_ref[...] = dh2
        ddn_ref[...] = _norm_bwd(dh2, xhat, rstd, gain).astype(ddn_ref.dtype)
        _rowsum_into(lossv_ref, first, err * err)
        _rowsum_into(dg_ref, first, dh2 * xhat)

    row = BS((tr, D), lambda i: (i, 0))
    vec = BS((1, D), lambda i: (0, 0))
    return _pc("post_ffn", body, (N // tr,), [h1, dn, tgt, g4], [row, row, row, vec],
               [_sds((N, D), F32), _sds((N, D), BF16), _sds((1, D), F32), _sds((1, D), F32)], [row, row, vec, vec])


def _mid_bwd(dh2, dc, h1, o, g2, g3, deps=()):
    N, D = h1.shape
    tr = _tile(N, 128, SUBLANES)

    def body(dh2_ref, dc_ref, h1_ref, o_ref, g2_ref, g3_ref, dh1_ref, do_ref, dg2_ref, dg3_ref):
        first = pl.program_id(0) == 0
        h1 = h1_ref[...]
        r3 = _rstd(h1)
        hc = h1 * r3
        dcv = dc_ref[...]
        dh1 = dh2_ref[...] + _norm_bwd(dcv, hc, r3, g3_ref[...])
        dh1_ref[...] = dh1
        ov = o_ref[...]
        r2 = _rstd(ov)
        ho = ov * r2
        do_ref[...] = _norm_bwd(dh1, ho, r2, g2_ref[...]).astype(do_ref.dtype)
        _rowsum_into(dg3_ref, first, dcv * hc)
        _rowsum_into(dg2_ref, first, dh1 * ho)

    row = BS((tr, D), lambda i: (i, 0))
    vec = BS((1, D), lambda i: (0, 0))
    return _pc("mid_bwd", body, (N // tr,), [dh2, dc, h1, o, g2, g3], [row] * 4 + [vec, vec],
               [_sds((N, D), F32), _sds((N, D), BF16), _sds((1, D), F32), _sds((1, D), F32)], [row, row, vec, vec],
               deps=deps)


def _pre_bwd(x, da, dh1, g1):
    N, D = x.shape
    tr = _tile(N, 128, SUBLANES)

    def body(x_ref, da_ref, dh1_ref, g_ref, dx_ref, dg_ref):
        first = pl.program_id(0) == 0
        xv = x_ref[...]
        r1 = _rstd(xv)
        xh = xv * r1
        dav = da_ref[...]
        dx_ref[...] = dh1_ref[...] + _norm_bwd(dav, xh, r1, g_ref[...])
        _rowsum_into(dg_ref, first, dav * xh)

    row = BS((tr, D), lambda i: (i, 0))
    vec = BS((1, D), lambda i: (0, 0))
    return _pc("pre_bwd", body, (N // tr,), [x, da, dh1, g1], [row, row, row, vec],
               [_sds((N, D), F32), _sds((1, D), F32)], [row, vec])


def _conv_rows(x_ref, halo_ref, first):
    x = x_ref[...]
    xx = jnp.concatenate([jnp.where(first, 0.0, halo_ref[...]), x], axis=0)
    x1 = pltpu.roll(xx, 1, axis=0)[SUBLANES:]
    x2 = pltpu.roll(xx, 2, axis=0)[SUBLANES:]
    return x, x1, x2


def _conv_apply(rows, w_ref, b_ref):
    x, x1, x2 = rows
    return ((b_ref[...] + x2 * w_ref[pl.ds(0, 1), :]) + x1 * w_ref[pl.ds(1, 1), :]) + x * w_ref[pl.ds(2, 1), :]


def _gate_specs(N, FC, TR, half):
    tile = BS((None, TR, FC), lambda jj, i: (jj + half, i, 0))
    halo = BS((None, SUBLANES, FC), lambda jj, i: (jj + half, jnp.maximum(i * (TR // SUBLANES) - 1, 0), 0))
    cw = BS((None, 3, FC), lambda jj, i: (jj + half, 0, 0))
    cb = BS((None, 1, FC), lambda jj, i: (jj + half, 0, 0))
    return tile, halo, cw, cb


def _gate_fwd(up_pre, cw, cb, L, deps=()):
    nb, N, FC = up_pre.shape
    half = nb // 2
    TR = _tile(L, 128, SUBLANES)

    def body(xa_ref, ha_ref, wa_ref, ba_ref, xb_ref, hb_ref, wb_ref, bb_ref, f_ref):
        first = (pl.program_id(1) % (L // TR)) == 0
        ua = _conv_apply(_conv_rows(xa_ref, ha_ref, first), wa_ref, ba_ref)
        ub = _conv_apply(_conv_rows(xb_ref, hb_ref, first), wb_ref, bb_ref)
        f_ref[...] = (_gelu(ua) * ub).astype(f_ref.dtype)

    sa, sb = _gate_specs(N, FC, TR, 0), _gate_specs(N, FC, TR, half)
    return _pc("gate_fwd", body, (half, N // TR), [up_pre, up_pre, cw, cb] * 2, list(sa) + list(sb),
               _sds((half, N, FC), BF16), BS((None, TR, FC), lambda jj, i: (jj, i, 0)), deps=deps)


def _gate_bwd(up_pre, cw, cb, df, L, deps=()):
    nb, N, FC = up_pre.shape
    half = nb // 2
    TR = _tile(L, 128, SUBLANES)

    def body(xa_ref, ha_ref, wa_ref, ba_ref, xb_ref, hb_ref, wb_ref, bb_ref, df_ref, dup_ref, dw_ref, dbias_ref):
        i = pl.program_id(1)
        first_row = i == 0
        first = (i % (L // TR)) == 0
        ra = _conv_rows(xa_ref, ha_ref, first)
        rb = _conv_rows(xb_ref, hb_ref, first)
        ua = _conv_apply(ra, wa_ref, ba_ref)
        ub = _conv_apply(rb, wb_ref, bb_ref)
        dfv = df_ref[...].astype(F32)
        dua = dfv * ub * _gelu_grad(ua)
        dub = dfv * _gelu(ua)
        dup_ref[0] = dua.astype(dup_ref.dtype)
        dup_ref[1] = dub.astype(dup_ref.dtype)
        for h, (rows, du) in enumerate(((ra, dua), (rb, dub))):
            x, x1, x2 = rows
            _rowsum_into(dbias_ref.at[h], first_row, du)
            for k, xs in enumerate((x2, x1, x)):
                _rowsum_into(dw_ref.at[h, pl.ds(k, 1), :], first_row, du * xs)

    sa, sb = _gate_specs(N, FC, TR, 0), _gate_specs(N, FC, TR, half)
    tile = BS((None, TR, FC), lambda jj, i: (jj, i, 0))
    both = BS((2, None, TR, FC), lambda jj, i: (0, jj, i, 0))
    dw = BS((2, None, 3, FC), lambda jj, i: (0, jj, 0, 0))
    dbias = BS((2, None, 1, FC), lambda jj, i: (0, jj, 0, 0))
    return _pc("gate_bwd", body, (half, N // TR), [up_pre, up_pre, cw, cb] * 2 + [df], list(sa) + list(sb) + [tile],
               [_sds((2, half, N, FC), BF16), _sds((2, half, 3, FC), F32), _sds((2, half, 1, FC), F32)],
               [both, dw, dbias], deps=deps)


def _conv_bwd(dup, cw, L):
    nb, N, FC = dup.shape
    TR = _tile(L, 128, 2 * SUBLANES)
    HR = 2 * SUBLANES
    nrb = N // HR

    def body(x_ref, h_ref, w_ref, o_ref):
        last = ((pl.program_id(1) + 1) % (L // TR)) == 0
        x = x_ref[...].astype(F32)
        xx = jnp.concatenate([x, jnp.where(last, 0.0, h_ref[...].astype(F32))], axis=0)
        x1 = pltpu.roll(xx, TR + HR - 1, axis=0)[:TR]
        x2 = pltpu.roll(xx, TR + HR - 2, axis=0)[:TR]
        o_ref[...] = (x * w_ref[pl.ds(2, 1), :] + x1 * w_ref[pl.ds(1, 1), :] + x2 * w_ref[pl.ds(0, 1), :]
                      ).astype(o_ref.dtype)

    tile = BS((None, TR, FC), lambda jj, i: (jj, i, 0))
    halo = BS((None, HR, FC), lambda jj, i: (jj, jnp.minimum((i + 1) * (TR // HR), nrb - 1), 0))
    w = BS((None, 3, FC), lambda jj, i: (jj, 0, 0))
    return _pc("conv_bwd", body, (nb, N // TR), [dup, dup, cw], [tile, halo, w], _sds((nb, N, FC), BF16), tile)


def _pack(arrs):
    parts = []
    for a in arrs:
        flat = a.reshape(-1).astype(F32)
        pad = (-flat.shape[0]) % (SUBLANES * LANES)
        parts.append(jnp.pad(flat, (0, pad)))
    return jnp.concatenate(parts).reshape(-1, LANES)


def _unpack(packed, shapes):
    flat = packed.reshape(-1)
    out, off = [], 0
    for s in shapes:
        n = math.prod(s)
        out.append(flat[off:off + n].reshape(s))
        off += n + ((-n) % (SUBLANES * LANES))
    return out


def _small_sum(gathered, gathered_bf, loss_rows, d_model):
    S, R, C = gathered.shape
    R2 = gathered_bf.shape[1]

    def body(p_ref, q_ref, tot_ref, tot2_ref, loss_ref):
        t = p_ref[0]
        u = q_ref[0].astype(F32)
        for s in range(1, S):
            t = t + p_ref[s]
            u = u + q_ref[s].astype(F32)
        tot_ref[...] = t
        tot2_ref[...] = u
        loss_ref[...] = jnp.full((1, 1), 0.5 / d_model, F32) * jnp.sum(t[:loss_rows])

    return _pc("small_sum", body, (1,), [gathered, gathered_bf],
               [BS((S, R, C), lambda i: (0, 0, 0)), BS((S, R2, C), lambda i: (0, 0, 0))],
               [_sds((R, C), F32), _sds((R2, C), F32), _sds((1, 1), F32)],
               [BS((R, C), lambda i: (0, 0)), BS((R2, C), lambda i: (0, 0)), BS((1, 1), lambda i: (0, 0))])


def _block_diag_in(bb_t, nch):
    J, G, P = bb_t.shape
    gl = G // nch
    b = bb_t.reshape(J, nch, gl, P).transpose(1, 0, 2, 3)
    eye = jnp.eye(gl, dtype=F32)
    w = eye[None, :, None, :, None] * b[:, None, :, :, :]
    return w.reshape(nch, gl * J, gl * P)


def _block_diag_in_grad(dw, J, G, P):
    nch = dw.shape[0]
    gl = G // nch
    d = dw.reshape(nch, gl, J, gl, P)
    d = jnp.einsum("cgjgp->jcgp", d)
    return d.reshape(J, G, P)


def _block_diag_out(c, nch):
    G, J, P = c.shape
    gl = G // nch
    cc = c.reshape(nch, gl, J, P).transpose(0, 1, 3, 2)
    eye = jnp.eye(gl, dtype=F32)
    w = cc[:, :, :, None, :] * eye[None, :, None, :, None]
    return w.reshape(nch, gl * P, gl * J)


def _block_diag_out_grad(dw, G, J, P):
    nch = dw.shape[0]
    gl = G // nch
    d = dw.reshape(nch, gl, P, gl, J)
    d = jnp.einsum("cgpgj->cgjp", d)
    return d.reshape(G, J, P)


def kernel(x, norm_pre_mix, w_in, ssm_lambda_re, ssm_lambda_im, ssm_log_step, ssm_b_re, ssm_b_im, ssm_c_re, ssm_c_im, ssm_d, ssm_glu_w, ssm_glu_b, pool_w, pool_b, pool_scale, w_branch_ssm, w_branch_pool, w_out, norm_post_mix, norm_pre_ffn, w_up, ffn_conv_w, ffn_conv_b, w_down, norm_post_ffn, loss_target, m_norm_pre_mix, m_w_in, m_ssm_lambda_re, m_ssm_lambda_im, m_ssm_log_step, m_ssm_b_re, m_ssm_b_im, m_ssm_c_re, m_ssm_c_im, m_ssm_d, m_ssm_glu_w, m_ssm_glu_b, m_pool_w, m_pool_b, m_pool_scale, m_w_branch_ssm, m_w_branch_pool, m_w_out, m_norm_post_mix, m_norm_pre_ffn, m_w_up, m_ffn_conv_w, m_ffn_conv_b, m_w_down, m_norm_post_ffn, v_norm_pre_mix, v_w_in, v_ssm_lambda_re, v_ssm_lambda_im, v_ssm_log_step, v_ssm_b_re, v_ssm_b_im, v_ssm_c_re, v_ssm_c_im, v_ssm_d, v_ssm_glu_w, v_ssm_glu_b, v_pool_w, v_pool_b, v_pool_scale, v_w_branch_ssm, v_w_branch_pool, v_w_out, v_norm_post_mix, v_norm_pre_ffn, v_w_up, v_ffn_conv_w, v_ffn_conv_b, v_w_down, v_norm_post_ffn):
    args = dict(locals())
    names = ["norm_pre_mix", "w_in", "ssm_lambda_re", "ssm_lambda_im", "ssm_log_step", "ssm_b_re", "ssm_b_im",
             "ssm_c_re", "ssm_c_im", "ssm_d", "ssm_glu_w", "ssm_glu_b", "pool_w", "pool_b", "pool_scale",
             "w_branch_ssm", "w_branch_pool", "w_out", "norm_post_mix", "norm_pre_ffn", "w_up", "ffn_conv_w",
             "ffn_conv_b", "w_down", "norm_post_ffn"]

    nseq, L, D = x.shape
    N = nseq * L
    U = D // NDEV
    DS = ssm_d.shape[1]
    DP = pool_scale.shape[1]
    G, P, J = ssm_b_re.shape[1:]
    SL = G * P
    CH = GROUPS_PER_CHUNK * J
    CS = GROUPS_PER_CHUNK * P
    NCH = DS // CH
    NPG = len(POOL_WINDOWS)
    PG = DP // NPG
    FC = w_up.shape[2]
    NB = NDEV
    HB = NB // 2
    F2 = NB * FC
    dev = _slot(_mesh_pos())
    tm = _tile(N, 1024)
    tm2 = _tile(N, 512)

    x2 = x.reshape(N, D)
    tgt = loss_target.reshape(N, D)

    def bf(t):
        return t.astype(BF16)

    def g_start(tag, group, after):
        return _split_start("gather_start_" + tag, _gather_copies, 3, group,
                            [_sds((NDEV,) + s.shape, s.dtype) for s in group], after)

    def g_land(tag, started, after):
        srcs, lands = _split_wait("gather_wait_" + tag, _gather_copies, started, after)
        return _split_start("d2d_start_" + tag, _d2d_copies, 4, srcs, lands, srcs[0])

    def g_finish(tag, d2d, after):
        srcs, lands = _split_wait("d2d_wait_" + tag, _d2d_copies, d2d, after)
        return [lax.dynamic_update_index_in_dim(l, s, dev, 0) for l, s in zip(lands, srcs)]

    def x_start(tag, group):
        return _split_start("exchange_start_" + tag, _exchange_copies, NDEV - 1, group,
                            [_sds(g.shape, g.dtype) for g in group], group[0])

    def x_finish(tag, started, after):
        srcs, lands = _split_wait("exchange_wait_" + tag, _exchange_copies, started, after)
        own = [lax.dynamic_index_in_dim(s, dev, 0, keepdims=False) for s in srcs]
        return [lax.dynamic_update_index_in_dim(l, o, dev, 0) for l, o in zip(lands, own)]

    st_in = g_start("in", [bf(w_in[0])], x2)
    st_mix = g_start("mix", [bf(ssm_glu_w[0]), bf(pool_w[0]), pool_b[0], ffn_conv_w[0]], st_in[4])
    conv_b_blk = ffn_conv_b.reshape(NB, 1, FC)

    (_, x2e, w_branch_ssm, w_branch_pool, w_out, w_up, w_down, ssm_lambda_re, ssm_lambda_im, ssm_log_step,
     ssm_b_re, ssm_b_im, ssm_c_re, ssm_c_im) = lax.optimization_barrier(
        (st_mix[4], x2, w_branch_ssm, w_branch_pool, w_out, w_up, w_down, ssm_lambda_re, ssm_lambda_im, ssm_log_step,
         ssm_b_re, ssm_b_im, ssm_c_re, ssm_c_im))
    lam_re, lam_im = ssm_lambda_re[0], ssm_lambda_im[0]
    log_step = ssm_log_step.reshape(G, 1)
    br_t = ssm_b_re[0].transpose(2, 0, 1)
    bi_t = ssm_b_im[0].transpose(2, 0, 1)
    pw_re3, pw_im3, pwf_re3, pwf_im3, bb_re, bb_im = _ssm_param_fwd(lam_re, lam_im, log_step, br_t, bi_t)
    pw_re, pw_im = pw_re3.reshape(SUBLANES, SL), pw_im3.reshape(SUBLANES, SL)
    pwf_re, pwf_im = pwf_re3.reshape(SUBLANES, SL), pwf_im3.reshape(SUBLANES, SL)
    WB = jnp.concatenate([_block_diag_in(bb_re, NCH), _block_diag_in(bb_im, NCH)], axis=2).astype(BF16)
    WCre = _block_diag_out(ssm_c_re[0], NCH).astype(BF16)
    WCim = _block_diag_out(-ssm_c_im[0], NCH).astype(BF16)
    a = _pre_norm(x2e, norm_pre_mix)
    small_names = ["norm_pre_mix", "norm_post_mix", "norm_pre_ffn", "norm_post_ffn", "ssm_lambda_re", "ssm_lambda_im",
                   "ssm_log_step", "ssm_b_re", "ssm_b_im", "ssm_c_re", "ssm_c_im", "ssm_d", "ssm_glu_b", "pool_scale",
                   "pool_b", "ffn_conv_w", "ffn_conv_b"]
    _, small_in = lax.optimization_barrier(
        (st_mix[4], [[args[p + n] for n in small_names] for p in ("", "m_", "v_")]))
    sm_w, sm_m, sm_v = (_pack(group) for group in small_in)
    g_br = [bf(w_branch_ssm[0]), bf(w_branch_pool[0]), bf(w_out[0])]
    g_up, g_down = [bf(w_up[0].T)], [bf(w_down[0])]
    early = [WB, WCre, WCim, pwf_re, pwf_im, a, sm_w, sm_m, sm_v] + g_br + g_up + g_down

    d_in = g_land("in", st_in, [st_mix[4]] + early)
    st_br = g_start("branch", g_br, d_in[4])
    st_up = g_start("up", g_up, st_br[4])
    st_down = g_start("down", g_down, st_up[4])
    (Win,) = g_finish("in", d_in, st_down[4])

    nq = 3 * NDEV
    (proj,) = _fused_matmul(
        "in_proj", (N // tm, nq, 1),
        [(a, BS((tm, D), lambda i, q, k: (i, 0)), Win, BS((None, D, U), lambda i, q, k: (q // 3, 0, q % 3)), "nn", 0)],
        [(tm, U)], [], [(_sds((N, 3 * D), F32), BS((tm, U), lambda i, q, k: (i, q)))],
        _store(lambda accs: accs))
    d_mix = g_land("mix", st_mix, proj)
    d_br = g_land("branch", st_br, d_mix[4])

    bu_re, bu_im = _fused_matmul(
        "ssm_in", (N // tm2, NCH, 1),
        [(proj, BS((tm2, CH), lambda i, c, k: (i, c)), WB, BS((None, CH, 2 * CS), lambda i, c, k: (c, 0, 0)), "nn", 0)],
        [(tm2, 2 * CS)], [],
        [(_sds((N, SL), F32), BS((tm2, CS), lambda i, c, k: (i, c)))] * 2,
        _store(lambda accs: (accs[0][:, :CS], accs[0][:, CS:])), deps=[d_br[4]])
    s_re, s_im = _scan_fwd(bu_re, bu_im, pw_re, pw_im, nseq, L)

    def ssm_out_epi(ids, accs, ex, o):
        u_ref, d_ref = ex
        y0 = accs[0] + d_ref[...] * u_ref[...]
        o[0][...] = y0
        o[1][...] = _gelu(y0).astype(BF16)

    y0, y1 = _fused_matmul(
        "ssm_out", (N // tm2, NCH, 1),
        [(s_re, BS((tm2, CS), lambda i, c, k: (i, c)), WCre, BS((None, CS, CH), lambda i, c, k: (c, 0, 0)), "nn", 0),
         (s_im, BS((tm2, CS), lambda i, c, k: (i, c)), WCim, BS((None, CS, CH), lambda i, c, k: (c, 0, 0)), "nn", 0)],
        [(tm2, CH)],
        [(proj, BS((tm2, CH), lambda i, c, k: (i, c))), (ssm_d, BS((1, CH), lambda i, c, k: (0, c)))],
        [(_sds((N, DS), F32), BS((tm2, CH), lambda i, c, k: (i, c))),
         (_sds((N, DS), BF16), BS((tm2, CH), lambda i, c, k: (i, c)))],
        ssm_out_epi)

    Wglu, Wpool, pool_b_all, conv_w_all = g_finish("mix", d_mix, y1)
    Wglu = Wglu.reshape(DS, DS)
    Wpool = Wpool.transpose(1, 0, 2, 3).reshape(NPG, PG, PG)
    pool_b_full = pool_b_all.transpose(1, 0, 2).reshape(1, DP)
    tn_s = _tile(DS, 512)

    def glu_epi(ids, accs, ex, o):
        y0_ref, b_ref = ex
        zg = accs[0] + b_ref[...]
        o[0][...] = zg
        o[1][...] = (_gelu(y0_ref[...]) * _sigmoid(zg)).astype(BF16)

    zg, ys = _fused_matmul(
        "ssm_glu", (N // tm, DS // tn_s, 1),
        [(y1, BS((tm, DS), lambda i, j, k: (i, 0)), Wglu, BS((DS, tn_s), lambda i, j, k: (0, j)), "nn", 0)],
        [(tm, tn_s)],
        [(y0, BS((tm, tn_s), lambda i, j, k: (i, j))), (ssm_glu_b, BS((1, tn_s), lambda i, j, k: (0, j)))],
        [(_sds((N, DS), F32), BS((tm, tn_s), lambda i, j, k: (i, j))),
         (_sds((N, DS), BF16), BS((tm, tn_s), lambda i, j, k: (i, j)))],
        glu_epi)

    z = _pool_fwd(proj, DS, DP, nseq, L)

    def pool_mm_epi(ids, accs, ex, o):
        b_ref, sc_ref = ex
        q = accs[0] + b_ref[...]
        o[0][...] = q
        o[1][...] = (q * sc_ref[...]).astype(BF16)

    qp, yp = _fused_matmul(
        "pool_mm", (N // tm, NPG, 1),
        [(z, BS((tm, PG), lambda i, g, k: (i, g)), Wpool, BS((None, PG, PG), lambda i, g, k: (g, 0, 0)), "nn", 0)],
        [(tm, PG)],
        [(pool_b_full, BS((1, PG), lambda i, g, k: (0, g))), (pool_scale, BS((1, PG), lambda i, g, k: (0, g)))],
        [(_sds((N, DP), F32), BS((tm, PG), lambda i, g, k: (i, g))),
         (_sds((N, DP), BF16), BS((tm, PG), lambda i, g, k: (i, g)))],
        pool_mm_epi)

    Wbs, Wbp, Wout = g_finish("branch", d_br, yp)
    Wout = Wout.reshape(D, D)
    gs_blk = BS((tm2, U), lambda i, q, k: (i, (DS + DP) // U + q))
    gp_blk = BS((tm2, U), lambda i, q, k: (i, (DS + DP + D) // U + q))
    out_blk = BS((tm2, U), lambda i, q, k: (i, q))

    def branch_epi(ids, accs, ex, o):
        gs_ref, gp_ref = ex
        o[0][...] = accs[0].astype(BF16)
        o[1][...] = accs[1].astype(BF16)
        o[2][...] = (_sigmoid(gs_ref[...]) * accs[0] + _sigmoid(gp_ref[...]) * accs[1]).astype(BF16)

    Ys, Yp, merged = _fused_matmul(
        "branch", (N // tm2, NDEV, 1),
        [(ys, BS((tm2, DS), lambda i, q, k: (i, 0)), Wbs, BS((None, DS, U), lambda i, q, k: (q, 0, 0)), "nn", 0),
         (yp, BS((tm2, DP), lambda i, q, k: (i, 0)), Wbp, BS((None, DP, U), lambda i, q, k: (q, 0, 0)), "nn", 1)],
        [(tm2, U), (tm2, U)],
        [(proj, gs_blk), (proj, gp_blk)],
        [(_sds((N, D), BF16), out_blk), (_sds((N, D), BF16), out_blk), (_sds((N, D), BF16), out_blk)],
        branch_epi, chunks=2)
    d_up = g_land("up", st_up, merged)

    tn_d = _tile(D, 512)
    (o_mix,) = _fused_matmul(
        "out_proj", (N // tm, D // tn_d, 1),
        [(merged, BS((tm, D), lambda i, j, k: (i, 0)), Wout, BS((D, tn_d), lambda i, j, k: (0, j)), "nn", 0)],
        [(tm, tn_d)], [], [(_sds((N, D), F32), BS((tm, tn_d), lambda i, j, k: (i, j)))],
        _store(lambda accs: accs), deps=[d_up[4]])
    h1, c = _mid_norm(x2, o_mix, norm_post_mix, norm_pre_ffn)

    (Wup,) = g_finish("up", d_up, c)
    tk_up = _tile(D, 2048)
    (up_pre,) = _fused_matmul(
        "ffn_up", (N // tm2, NB, D // tk_up),
        [(c, BS((tm2, tk_up), lambda i, j, k: (i, k)), Wup, BS((None, FC, tk_up), lambda i, j, k: (j, 0, k)), "nt", 0)],
        [(tm2, FC)], [], [(_sds((NB, N, FC), F32), BS((None, tm2, FC), lambda i, j, k: (j, i, 0)))],
        _store(lambda accs: accs))
    d_down = g_land("down", st_down, up_pre)
    f = _gate_fwd(up_pre, conv_w_all, conv_b_blk, L, deps=[d_down[4]])
    (Wdown,) = g_finish("down", d_down, f)
    Wdown = Wdown.reshape(HB, FC, D)
    tn_d2 = _tile(D, 1024)
    (dn,) = _fused_matmul(
        "ffn_down", (N // tm2, D // tn_d, 1),
        [(f, BS((None, tm2, FC), functools.partial(lambda b, i, j, k: (b, i, 0), b)), Wdown,
          BS((None, FC, tn_d), functools.partial(lambda b, i, j, k: (b, 0, j), b)), "nn", 0) for b in range(HB)],
        [(tm2, tn_d)], [], [(_sds((N, D), F32), BS((tm2, tn_d), lambda i, j, k: (i, j)))],
        _store(lambda accs: accs))
    dh2, d_dn, lossv, dg4 = _post_ffn(h1, dn, tgt, norm_post_ffn)

    tk_d = _tile(D, 2048)
    (df,) = _fused_matmul(
        "ffn_down_dx", (N // tm2, HB, D // tk_d),
        [(d_dn, BS((tm2, tk_d), lambda i, j, k: (i, k)), Wdown, BS((None, FC, tk_d), lambda i, j, k: (j, 0, k)), "nt", 0)],
        [(tm2, FC)], [], [(_sds((HB, N, FC), BF16), BS((None, tm2, FC), lambda i, j, k: (j, i, 0)))],
        _store(lambda accs: accs))
    tk_n = _tile(N, 2048)
    (gW_down,) = _fused_matmul(
        "ffn_down_dw", (HB, D // tn_d, N // tk_n),
        [(f, BS((None, tk_n, FC), lambda j, n, k: (j, k, 0)), d_dn, BS((tk_n, tn_d), lambda j, n, k: (k, n)), "tn", 0)],
        [(FC, tn_d)], [], [(_sds((HB, FC, D), BF16), BS((None, FC, tn_d), lambda j, n, k: (j, 0, n)))],
        _store(lambda accs: accs))
    x_down = x_start("down", [gW_down.reshape(NDEV, FC // 2, D)])
    dup, dcw, dcb = _gate_bwd(up_pre, conv_w_all, conv_b_blk, df, L, deps=[x_down[4]])
    dpre = _conv_bwd(dup.reshape(NB, N, FC), conv_w_all, L)
    (dc,) = _fused_matmul(
        "ffn_up_dx", (N // tm2, D // tn_d2, NB // 2),
        [(dpre, BS((None, tm2, FC), lambda i, j, k: (2 * k, i, 0)), Wup, BS((None, FC, tn_d2), lambda i, j, k: (2 * k, 0, j)), "nn", 0),
         (dpre, BS((None, tm2, FC), lambda i, j, k: (2 * k + 1, i, 0)), Wup, BS((None, FC, tn_d2), lambda i, j, k: (2 * k + 1, 0, j)), "nn", 0)],
        [(tm2, tn_d2)], [], [(_sds((N, D), F32), BS((tm2, tn_d2), lambda i, j, k: (i, j)))],
        _store(lambda accs: accs))
    tm_d = _tile(D, 512)
    (gW_up,) = _fused_matmul(
        "ffn_up_dw", (NB, D // tm_d, N // tk_n),
        [(dpre, BS((None, tk_n, FC), lambda j, n, k: (j, k, 0)), c, BS((tk_n, tm_d), lambda j, n, k: (k, n)), "tn", 0)],
        [(FC, tm_d)], [], [(_sds((NB, FC, D), BF16), BS((None, FC, tm_d), lambda j, n, k: (j, 0, n)))],
        _store(lambda accs: accs))
    x_up = x_start("up", [gW_up])

    dh1, d_o, dg2, dg3 = _mid_bwd(dh2, dc, h1, o_mix, norm_post_mix, norm_pre_ffn, deps=[x_up[4]])

    def dmerged_epi(ids, accs, ex, o):
        gs_ref, gp_ref, ys_ref, yp_ref = ex
        dm = accs[0]
        sg_s, sg_p = _sigmoid(gs_ref[...]), _sigmoid(gp_ref[...])
        o[0][...] = (dm * sg_s).astype(BF16)
        o[1][...] = (dm * sg_p).astype(BF16)
        o[2][...] = (dm * ys_ref[...].astype(F32) * sg_s * (1.0 - sg_s)).astype(BF16)
        o[3][...] = (dm * yp_ref[...].astype(F32) * sg_p * (1.0 - sg_p)).astype(BF16)

    dYs, dYp, dgs, dgp = _fused_matmul(
        "out_proj_dx", (N // tm2, NDEV, 1),
        [(d_o, BS((tm2, D), lambda i, q, k: (i, 0)), Wout, BS((U, D), lambda i, q, k: (q, 0)), "nt", 0)],
        [(tm2, U)],
        [(proj, gs_blk), (proj, gp_blk), (Ys, out_blk), (Yp, out_blk)],
        [(_sds((N, D), BF16), out_blk)] * 4,
        dmerged_epi, chunks=2)
    (gW_out,) = _fused_matmul(
        "out_proj_dw", (D // tm_d, D // tn_d, 1),
        [(merged, BS((N, tm_d), lambda i, j, k: (0, i)), d_o, BS((N, tn_d), lambda i, j, k: (0, j)), "tn", 0)],
        [(tm_d, tn_d)], [], [(_sds((D, D), BF16), BS((tm_d, tn_d), lambda i, j, k: (i, j)))],
        _store(lambda accs: accs))
    tm_s = _tile(DS, 512)
    gW_bs, gW_bp = _fused_matmul(
        "branch_dw", (DS // tm_s, NDEV, 1),
        [(ys, BS((N, tm_s), lambda i, q, k: (0, i)), dYs, BS((N, U), lambda i, q, k: (0, q)), "tn", 0),
         (yp, BS((N, tm_s), lambda i, q, k: (0, i)), dYp, BS((N, U), lambda i, q, k: (0, q)), "tn", 1)],
        [(tm_s, U), (tm_s, U)], [],
        [(_sds((NDEV, DS, U), BF16), BS((None, tm_s, U), lambda i, q, k: (q, i, 0)))] * 2,
        _store(lambda accs: accs))
    x_br = x_start("branch", [gW_bs, gW_bp, gW_out.reshape(NDEV, U, D)])

    tn_p = _tile(PG, 512)

    def dyp_epi(ids, accs, ex, o):
        q_ref, sc_ref = ex
        first = ids[1] == 0
        dyp = accs[0]
        dq = dyp * sc_ref[...]
        o[0][...] = dq.astype(BF16)
        _rowsum_into(o[1], first, dyp * q_ref[...])
        _rowsum_into(o[2], first, dq)

    dq, d_pscale, d_pb = _fused_matmul(
        "branch_pool_dx", (DP // tn_p, N // tm, 1),
        [(dYp, BS((tm, D), lambda j, i, k: (i, 0)), Wbp, BS((NDEV, tn_p, U), lambda j, i, k: (0, j, 0)), "nt_cat", 0)],
        [(tm, tn_p)],
        [(qp, BS((tm, tn_p), lambda j, i, k: (i, j))), (pool_scale, BS((1, tn_p), lambda j, i, k: (0, j)))],
        [(_sds((N, DP), BF16), BS((tm, tn_p), lambda j, i, k: (i, j))),
         (_sds((1, DP), F32), BS((1, tn_p), lambda j, i, k: (0, j))),
         (_sds((1, DP), F32), BS((1, tn_p), lambda j, i, k: (0, j)))],
        dyp_epi, deps=[x_br[4]])
    (dz,) = _fused_matmul(
        "pool_mm_dx", (N // tm, NPG, 1),
        [(dq, BS((tm, PG), lambda i, g, k: (i, g)), Wpool, BS((None, PG, PG), lambda i, g, k: (g, 0, 0)), "nt", 0)],
        [(tm, PG)], [], [(_sds((N, DP), F32), BS((tm, PG), lambda i, g, k: (i, g)))],
        _store(lambda accs: accs))
    (gW_pool,) = _fused_matmul(
        "pool_mm_dw", (NPG, 1),
        [(z, BS((N, PG), lambda g, k: (0, g)), dq, BS((N, PG), lambda g, k: (0, g)), "tn", 0)],
        [(PG, PG)], [], [(_sds((NPG, PG, PG), BF16), BS((None, PG, PG), lambda g, k: (g, 0, 0)))],
        _store(lambda accs: accs))
    du_pool = _pool_bwd(dz, nseq, L)

    def dys_epi(ids, accs, ex, o):
        zg_ref, y0_ref = ex
        first = ids[1] == 0
        dys = accs[0]
        sg = _sigmoid(zg_ref[...])
        dzg = dys * _gelu(y0_ref[...]) * sg * (1.0 - sg)
        o[0][...] = dzg.astype(BF16)
        o[1][...] = dys * sg
        _rowsum_into(o[2], first, dzg)

    dzg, dy1_direct, d_glu_b = _fused_matmul(
        "branch_ssm_dx", (DS // tn_s, N // tm, 1),
        [(dYs, BS((tm, D), lambda j, i, k: (i, 0)), Wbs, BS((NDEV, tn_s, U), lambda j, i, k: (0, j, 0)), "nt_cat", 0)],
        [(tm, tn_s)],
        [(zg, BS((tm, tn_s), lambda j, i, k: (i, j))), (y0, BS((tm, tn_s), lambda j, i, k: (i, j)))],
        [(_sds((N, DS), BF16), BS((tm, tn_s), lambda j, i, k: (i, j))),
         (_sds((N, DS), F32), BS((tm, tn_s), lambda j, i, k: (i, j))),
         (_sds((1, DS), F32), BS((1, tn_s), lambda j, i, k: (0, j)))],
        dys_epi)
    (gW_glu,) = _fused_matmul(
        "ssm_glu_dw", (DS // tm_s, DS // tn_s, 1),
        [(y1, BS((N, tm_s), lambda i, j, k: (0, i)), dzg, BS((N, tn_s), lambda i, j, k: (0, j)), "tn", 0)],
        [(tm_s, tn_s)], [], [(_sds((DS, DS), BF16), BS((tm_s, tn_s), lambda i, j, k: (i, j)))],
        _store(lambda accs: accs))
    x_mix = x_start("mix", [gW_glu.reshape(NDEV, DS // NDEV, DS),
                            gW_pool.reshape(NPG, NDEV, PG // NDEV, PG).transpose(1, 0, 2, 3)])

    tn_c = _tile(DS, CH)

    def dy0_epi(ids, accs, ex, o):
        d1_ref, y0_ref, u_ref = ex
        first = ids[1] == 0
        dy0 = (accs[0] + d1_ref[...]) * _gelu_grad(y0_ref[...])
        o[0][...] = dy0
        _rowsum_into(o[1], first, dy0 * u_ref[...])

    dy0, d_ssm_d = _fused_matmul(
        "ssm_glu_dx", (DS // tn_c, N // tm, 1),
        [(dzg, BS((tm, DS), lambda j, i, k: (i, 0)), Wglu, BS((tn_c, DS), lambda j, i, k: (j, 0)), "nt", 0)],
        [(tm, tn_c)],
        [(dy1_direct, BS((tm, tn_c), lambda j, i, k: (i, j))), (y0, BS((tm, tn_c), lambda j, i, k: (i, j))),
         (proj, BS((tm, tn_c), lambda j, i, k: (i, j)))],
        [(_sds((N, DS), F32), BS((tm, tn_c), lambda j, i, k: (i, j))),
         (_sds((1, DS), F32), BS((1, tn_c), lambda j, i, k: (0, j)))],
        dy0_epi, deps=[x_mix[4]])

    ds_re, ds_im = _fused_matmul(
        "ssm_out_dx", (N // tm2, NCH, 1),
        [(dy0, BS((tm2, CH), lambda i, c, k: (i, c)), WCre, BS((None, CS, CH), lambda i, c, k: (c, 0, 0)), "nt", 0),
         (dy0, BS((tm2, CH), lambda i, c, k: (i, c)), WCim, BS((None, CS, CH), lambda i, c, k: (c, 0, 0)), "nt", 1)],
        [(tm2, CS), (tm2, CS)], [],
        [(_sds((N, SL), F32), BS((tm2, CS), lambda i, c, k: (i, c)))] * 2,
        _store(lambda accs: accs))
    lam_r, lam_i, d_ab_re, d_ab_im = _scan_bwd(ds_re, ds_im, s_re, s_im, pw_re, pw_im, pwf_re, pwf_im, nseq, L)

    def du_epi(ids, accs, ex, o):
        dy0_ref, d_ref = ex
        o[0][...] = (accs[0] + dy0_ref[...] * d_ref[...]).astype(BF16)

    (du_ssm,) = _fused_matmul(
        "ssm_in_dx", (N // tm2, NCH, 1),
        [(lam_r, BS((tm2, CS), lambda i, c, k: (i, c)), WB, BS((None, CH, CS), lambda i, c, k: (c, 0, 0)), "nt", 0),
         (lam_i, BS((tm2, CS), lambda i, c, k: (i, c)), WB, BS((None, CH, CS), lambda i, c, k: (c, 0, 1)), "nt", 0)],
        [(tm2, CH)],
        [(dy0, BS((tm2, CH), lambda i, c, k: (i, c))), (ssm_d, BS((1, CH), lambda i, c, k: (0, c)))],
        [(_sds((N, DS), BF16), BS((tm2, CH), lambda i, c, k: (i, c)))],
        du_epi)
    dproj = jnp.concatenate([du_ssm, du_pool, dgs, dgp], axis=1)
    (gW_in,) = _fused_matmul(
        "in_proj_dw", (D // tm_d, nq, 1),
        [(a, BS((N, tm_d), lambda i, q, k: (0, i)), dproj, BS((N, U), lambda i, q, k: (0, q)), "tn", 0)],
        [(tm_d, U)], [], [(_sds((NDEV, D, 3 * U), BF16), BS((None, tm_d, U), lambda i, q, k: (q // 3, i, q % 3)))],
        _store(lambda accs: accs))
    x_in = x_start("in", [gW_in])
    (da,) = _fused_matmul(
        "in_proj_dx", (N // tm, D // tn_d2, NDEV // 2),
        [(dproj, BS((tm, 6 * U), lambda i, j, k: (i, k)), Win, BS((2, tn_d2, 3 * U), lambda i, j, k: (k, j, 0)), "nt_cat", 0)],
        [(tm, tn_d2)], [], [(_sds((N, D), F32), BS((tm, tn_d2), lambda i, j, k: (i, j)))],
        _store(lambda accs: accs), deps=[x_in[4]])
    grad_x, dg1 = _pre_bwd(x2, da, dh1, norm_pre_mix)

    tk_n = _tile(N, 1024)
    dWCre, dWCim = _fused_matmul(
        "ssm_out_dw", (NCH, N // tk_n),
        [(s_re, BS((tk_n, CS), lambda c, k: (k, c)), dy0, BS((tk_n, CH), lambda c, k: (k, c)), "tn", 0),
         (s_im, BS((tk_n, CS), lambda c, k: (k, c)), dy0, BS((tk_n, CH), lambda c, k: (k, c)), "tn", 1)],
        [(CS, CH), (CS, CH)], [],
        [(_sds((NCH, CS, CH), F32), BS((None, CS, CH), lambda c, k: (c, 0, 0)))] * 2,
        _store(lambda accs: accs), deps=[x_in[4]])
    dWBre, dWBim = _fused_matmul(
        "ssm_in_dw", (NCH, N // tk_n),
        [(proj, BS((tk_n, CH), lambda c, k: (k, c)), lam_r, BS((tk_n, CS), lambda c, k: (k, c)), "tn", 0),
         (proj, BS((tk_n, CH), lambda c, k: (k, c)), lam_i, BS((tk_n, CS), lambda c, k: (k, c)), "tn", 1)],
        [(CH, CS), (CH, CS)], [],
        [(_sds((NCH, CH, CS), F32), BS((None, CH, CS), lambda c, k: (c, 0, 0)))] * 2,
        _store(lambda accs: accs), deps=[x_in[4]])
    d_bbr = _block_diag_in_grad(dWBre, J, G, P)
    d_bbi = _block_diag_in_grad(dWBim, J, G, P)
    d_lam_re, d_lam_im, d_log_step, d_br_t, d_bi_t = _ssm_param_bwd(
        lam_re, lam_im, log_step, br_t, bi_t,
        d_ab_re.reshape(nseq * SUBLANES, G, P), d_ab_im.reshape(nseq * SUBLANES, G, P), d_bbr, d_bbi)
    d_c_re = _block_diag_out_grad(dWCre, G, J, P)
    d_c_im = -_block_diag_out_grad(dWCim, G, J, P)

    d_conv_w = dcw.reshape(NB, 3, FC).transpose(1, 0, 2).reshape(3, F2)
    d_conv_b = dcb.reshape(1, F2)
    small = {
        "norm_pre_mix": dg1, "norm_post_mix": dg2, "norm_pre_ffn": dg3, "norm_post_ffn": dg4,
        "ssm_lambda_re": d_lam_re[None], "ssm_lambda_im": d_lam_im[None], "ssm_log_step": d_log_step.reshape(1, G),
        "ssm_b_re": d_br_t.transpose(1, 2, 0)[None], "ssm_b_im": d_bi_t.transpose(1, 2, 0)[None],
        "ssm_c_re": d_c_re[None], "ssm_c_im": d_c_im[None],
        "ssm_d": d_ssm_d, "ssm_glu_b": d_glu_b, "pool_scale": d_pscale,
        "pool_b": d_pb.reshape(1, NPG, PG), "ffn_conv_w": d_conv_w[None], "ffn_conv_b": d_conv_b,
    }
    assert list(small) == small_names
    wide = ["ssm_b_re", "ssm_b_im", "ssm_c_re", "ssm_c_im"]
    narrow = [n for n in small_names if n not in wide]
    packed = _pack([lossv] + [small[n] for n in narrow])
    packed_bf = _pack([small[n] for n in wide]).astype(BF16)
    st_small = _split_start("small_start", _broadcast_copies, NDEV - 1, [packed, packed_bf],
                            [_sds((NDEV,) + p.shape, p.dtype) for p in (packed, packed_bf)], packed)

    res = {}
    after = st_small[4]
    for tag, started, group in (("down", x_down, ["w_down"]), ("up", x_up, ["w_up"]),
                                ("branch", x_br, ["w_branch_ssm", "w_branch_pool", "w_out"]),
                                ("mix", x_mix, ["ssm_glu_w", "pool_w"]), ("in", x_in, ["w_in"])):
        for n, parts in zip(group, x_finish(tag, started, after)):
            shape = args[n].shape
            if n == "w_up":
                flat, back = (lambda t: t[0].T), (lambda t: t.T[None])
            else:
                flat, back = (lambda t: t.reshape(-1, shape[-1])), (lambda t: t.reshape(shape))
            w2 = flat(args[n])
            g, dl, nm, nv = _adamw("adamw_" + n, w2, flat(args["m_" + n]), flat(args["v_" + n]),
                                   parts.reshape((NDEV,) + w2.shape))
            res[n] = tuple(back(t) for t in (g, dl, nm, nv))
            after = g

    srcs, lands = _split_wait("small_wait", _broadcast_copies, st_small, after)
    small_all, small_all_bf = (lax.dynamic_update_index_in_dim(l, s, dev, 0) for l, s in zip(lands, srcs))
    loss_rows = (D + SUBLANES * LANES - 1) // (SUBLANES * LANES) * SUBLANES
    total, total_wide, loss = _small_sum(small_all, small_all_bf, loss_rows, D)
    totals = dict(zip(narrow, _unpack(total, [lossv.shape] + [small[n].shape for n in narrow])[1:]))
    totals.update(zip(wide, _unpack(total_wide, [small[n].shape for n in wide])))
    totals["pool_b"] = lax.dynamic_slice_in_dim(totals["pool_b"], dev * (PG // NDEV), PG // NDEV, axis=2)
    totals["ffn_conv_w"] = lax.dynamic_slice_in_dim(totals["ffn_conv_w"], dev * FC, FC, axis=2)
    sm_g = _pack([totals[n] for n in small_names])
    _, sm_d, sm_nm, sm_nv = _adamw("adamw_small", sm_w, sm_m, sm_v, sm_g[None])
    shapes = [args[n].shape for n in small_names]
    for n, dl, nm, nv in zip(small_names, _unpack(sm_d, shapes), _unpack(sm_nm, shapes), _unpack(sm_nv, shapes)):
        res[n] = (totals[n], dl, nm, nv)

    outs = [loss.reshape(()), grad_x.reshape(x.shape)]
    for k in range(4):
        outs += [res[n][k] for n in names]
    return tuple(outs)
```
